```python
import math
import jax
import jax.numpy as jnp
from jax import lax
import numpy as np

D_MODEL = 2048
BATCH = 8
SEQ = 2048
DEPTH = 2

N_A_LAYERS = DEPTH // 2
N_B_LAYERS = DEPTH - N_A_LAYERS

NH_A = 8
DK_A = D_MODEL // 2 // NH_A
DV_A = D_MODEL // NH_A
QK_A = NH_A * DK_A
V_A = NH_A * DV_A
CONV_A = 4
CHUNK = 64
A_IN_COLS = 2 * QK_A + 3 * V_A + 2 * NH_A

NH_B = 16
DH_B = 128
W_B = NH_B * DH_B
QBLOCK = 128

ALPHA = (2.0 * DEPTH) ** 0.25
BETA = (8.0 * DEPTH) ** -0.25
LN_EPS = 1e-5

kernel_name = "yoco_mlstm_stickbreaking_hybrid"


def _layernorm(x, g, b):
    xf = x.astype(jnp.float32)
    mu = jnp.mean(xf, axis=-1, keepdims=True)
    xc = xf - mu
    var = jnp.mean(xc * xc, axis=-1, keepdims=True)
    return (xc * lax.rsqrt(var + LN_EPS) * g.astype(jnp.float32) + b.astype(jnp.float32)).astype(x.dtype)


def _causal_depthwise_conv(u, w, b):
    c = u.shape[-1]
    out = lax.conv_general_dilated(
        u, w[:, None, :], window_strides=(1,), padding=[(CONV_A - 1, 0)],
        dimension_numbers=("NWC", "WIO", "NWC"), feature_group_count=c)
    return out + b


def _mlstm_chunkwise(q, k, v, ig, lf):
    bsz, nh, s, dk = q.shape
    dv = v.shape[-1]
    nc = s // CHUNK

    def to_chunks(a):
        return jnp.moveaxis(a.reshape(bsz, nh, nc, CHUNK, *a.shape[3:]), 2, 0)

    causal = jnp.tril(jnp.ones((CHUNK, CHUNK), dtype=bool))

    def step(carry, xs):
        c_st, n_st, m_st = carry
        qc, kc, vc, ic, fc = xs
        bcum = jnp.cumsum(fc, axis=-1)
        gtot = bcum[..., -1]
        dmat = jnp.where(causal, bcum[..., :, None] - bcum[..., None, :] + ic[..., None, :], -jnp.inf)
        inter = bcum + m_st[..., None]
        m_q = jnp.maximum(inter, jnp.max(dmat, axis=-1))
        scores = jnp.einsum("bhjd,bhsd->bhjs", qc, kc) * jnp.exp(dmat - m_q[..., None])
        w_inter = jnp.exp(inter - m_q)
        num = jnp.einsum("bhjs,bhsv->bhjv", scores, vc) + w_inter[..., None] * jnp.einsum("bhjd,bhdv->bhjv", qc, c_st)
        den = jnp.sum(scores, axis=-1) + w_inter * jnp.einsum("bhjd,bhd->bhj", qc, n_st)
        h = num / jnp.maximum(jnp.abs(den), jnp.exp(-m_q))[..., None]
        wlog = gtot[..., None] - bcum + ic
        m_new = jnp.maximum(gtot + m_st, jnp.max(wlog, axis=-1))
        wk = jnp.exp(wlog - m_new[..., None])
        decay = jnp.exp(gtot + m_st - m_new)
        c_new = decay[..., None, None] * c_st + jnp.einsum("bhsd,bhsv->bhdv", kc * wk[..., None], vc)
        n_new = decay[..., None] * n_st + jnp.einsum("bhs,bhsd->bhd", wk, kc)
        return (c_new, n_new, m_new), h

    init = (jnp.zeros((bsz, nh, dk, dv), jnp.float32),
            jnp.zeros((bsz, nh, dk), jnp.float32),
            jnp.zeros((bsz, nh), jnp.float32))
    _, hs = lax.scan(step, init, (to_chunks(q), to_chunks(k), to_chunks(v), to_chunks(ig), to_chunks(lf)))
    return jnp.moveaxis(hs, 0, 2).reshape(bsz, nh, s, dv)


def _mlstm_layer(x, w_in, gate_b, conv_w, conv_b, head_g, w_out, ln_g, ln_b):
    bsz, s, _ = x.shape
    u = jnp.einsum("bsd,de->bse", x, w_in)
    qk, v, o, z, gates = jnp.split(u, [2 * QK_A, 2 * QK_A + V_A, 2 * QK_A + 2 * V_A, 2 * QK_A + 3 * V_A], axis=-1)
    qk = jax.nn.silu(_causal_depthwise_conv(qk, conv_w, conv_b))
    q, k = jnp.split(qk, 2, axis=-1)
    gates = (gates + gate_b).astype(jnp.float32)
    ig, fg = jnp.split(gates, 2, axis=-1)

    def heads(a, d):
        return a.reshape(bsz, s, NH_A, d).transpose(0, 2, 1, 3).astype(jnp.float32)

    qh = heads(q, DK_A)
    kh = heads(k, DK_A) * (DK_A ** -0.5)
    vh = heads(v, DV_A)
    ig_h = ig.transpose(0, 2, 1)
    lf_h = jax.nn.log_sigmoid(fg).transpose(0, 2, 1)
    h = _mlstm_chunkwise(qh, kh, vh, ig_h, lf_h)
    mu = jnp.mean(h, axis=-1, keepdims=True)
    hc = h - mu
    h = hc * lax.rsqrt(jnp.mean(hc * hc, axis=-1, keepdims=True) + LN_EPS)
    h = h.transpose(0, 2, 1, 3).reshape(bsz, s, V_A).astype(x.dtype) * head_g
    h = jax.nn.sigmoid(o) * h * jax.nn.silu(z)
    y = jnp.einsum("bse,ed->bsd", h, w_out)
    return _layernorm(ALPHA * x + y, ln_g, ln_b)


def _stick_breaking(q, k, v):
    s = q.shape[2]
    scale = DH_B ** -0.5
    outs = []
    for blk in range(s // QBLOCK):
        t0 = blk * QBLOCK
        t1 = t0 + QBLOCK
        qb = q[:, :, t0:t1].astype(jnp.float32)
        kp = k[:, :, :t1].astype(jnp.float32)
        vp = v[:, :, :t1].astype(jnp.float32)
        z = jnp.einsum("bhtd,bhsd->bhts", qb, kp) * scale
        mask = jnp.arange(t1)[None, :] < (t0 + jnp.arange(QBLOCK))[:, None]
        log_1mb = jnp.where(mask, jax.nn.log_sigmoid(-z), 0.0)
        between = lax.cumsum(log_1mb, axis=3, reverse=True) - log_1mb
        a = jnp.where(mask, jnp.exp(jax.nn.log_sigmoid(z) + between), 0.0)
        outs.append(jnp.einsum("bhts,bhsd->bhtd", a, vp))
    return jnp.concatenate(outs, axis=2)


def _stick_breaking_layer(x, k_sh, v_sh, w_in, w_out, ln_g, ln_b):
    bsz, s, _ = x.shape
    u = jnp.einsum("bsd,de->bse", x, w_in)
    q, z = jnp.split(u, [W_B], axis=-1)
    qh = q.reshape(bsz, s, NH_B, DH_B).transpose(0, 2, 1, 3)
    att = _stick_breaking(qh, k_sh, v_sh)
    att = att.transpose(0, 2, 1, 3).reshape(bsz, s, W_B).astype(x.dtype)
    y = jnp.einsum("bse,ed->bsd", att * jax.nn.silu(z), w_out)
    return _layernorm(ALPHA * x + y, ln_g, ln_b)


def _fwd_setup_inputs(seed: int = 0) -> dict:
    key = jax.random.key(seed)
    ks = jax.random.split(key, 16)
    f32 = jnp.float32
    nrm = lambda k, shape: jax.random.normal(k, shape, f32)
    x = nrm(ks[0], (BATCH, SEQ, D_MODEL))
    a_w_in = nrm(ks[1], (N_A_LAYERS, D_MODEL, A_IN_COLS)) * D_MODEL ** -0.5
    i_bias = 0.1 * nrm(ks[2], (N_A_LAYERS, NH_A))
    f_bias = jnp.linspace(3.0, 6.0, NH_A, dtype=f32)[None, :] + 0.1 * nrm(ks[3], (N_A_LAYERS, NH_A))
    a_gate_b = jnp.concatenate([i_bias, f_bias], axis=-1)
    a_conv_w = nrm(ks[4], (N_A_LAYERS, CONV_A, 2 * QK_A)) * CONV_A ** -0.5
    a_conv_b = 0.01 * nrm(ks[5], (N_A_LAYERS, 2 * QK_A))
    a_head_g = 1.0 + 0.02 * nrm(ks[6], (N_A_LAYERS, V_A))
    a_w_out = nrm(ks[7], (N_A_LAYERS, V_A, D_MODEL)) * (V_A ** -0.5) * BETA
    a_ln_g = 1.0 + 0.02 * nrm(ks[8], (N_A_LAYERS, D_MODEL))
    a_ln_b = 0.02 * nrm(ks[9], (N_A_LAYERS, D_MODEL))
    kv_w = nrm(ks[10], (D_MODEL, 2 * W_B)) * D_MODEL ** -0.5
    b_w_in = nrm(ks[11], (N_B_LAYERS, D_MODEL, 2 * W_B)) * D_MODEL ** -0.5
    b_w_out = nrm(ks[12], (N_B_LAYERS, W_B, D_MODEL)) * (W_B ** -0.5) * BETA
    b_ln_g = 1.0 + 0.02 * nrm(ks[13], (N_B_LAYERS, D_MODEL))
    b_ln_b = 0.02 * nrm(ks[14], (N_B_LAYERS, D_MODEL))
    return {"x": x, "a_w_in": a_w_in, "a_gate_b": a_gate_b, "a_conv_w": a_conv_w, "a_conv_b": a_conv_b,
            "a_head_g": a_head_g, "a_w_out": a_w_out, "a_ln_g": a_ln_g, "a_ln_b": a_ln_b,
            "kv_w": kv_w, "b_w_in": b_w_in, "b_w_out": b_w_out, "b_ln_g": b_ln_g, "b_ln_b": b_ln_b}


def _fwd_reference(x, a_w_in, a_gate_b, a_conv_w, a_conv_b, a_head_g, a_w_out, a_ln_g, a_ln_b,
              kv_w, b_w_in, b_w_out, b_ln_g, b_ln_b):
    bsz, s, _ = x.shape
    k_sh = None
    v_sh = None
    for layer in range(DEPTH):
        if layer < N_A_LAYERS:
            x = _mlstm_layer(x, a_w_in[layer], a_gate_b[layer], a_conv_w[layer], a_conv_b[layer],
                             a_head_g[layer], a_w_out[layer], a_ln_g[layer], a_ln_b[layer])
        else:
            if layer == N_A_LAYERS:
                kv = jnp.einsum("bsd,de->bse", x, kv_w)
                k_sh, v_sh = jnp.split(kv, 2, axis=-1)
                k_sh = k_sh.reshape(bsz, s, NH_B, DH_B).transpose(0, 2, 1, 3)
                v_sh = v_sh.reshape(bsz, s, NH_B, DH_B).transpose(0, 2, 1, 3)
            j = layer - N_A_LAYERS
            x = _stick_breaking_layer(x, k_sh, v_sh, b_w_in[j], b_w_out[j], b_ln_g[j], b_ln_b[j])
    return x


import jax as _jax
import jax.numpy as _jnp

TWIN_FORMAT = 'train_step'
FWD_PARAMS = ['x', 'a_w_in', 'a_gate_b', 'a_conv_w', 'a_conv_b', 'a_head_g', 'a_w_out', 'a_ln_g', 'a_ln_b', 'kv_w', 'b_w_in', 'b_w_out', 'b_ln_g', 'b_ln_b']
TWIN_WEIGHTS = ['a_w_in', 'a_gate_b', 'a_conv_w', 'a_conv_b', 'a_head_g', 'a_w_out', 'a_ln_g', 'a_ln_b', 'kv_w', 'b_w_in', 'b_w_out', 'b_ln_g', 'b_ln_b']
TWIN_DIFF_INPUT = 'x'
TWIN_INPUTS = ['x', 'a_w_in', 'a_gate_b', 'a_conv_w', 'a_conv_b', 'a_head_g', 'a_w_out', 'a_ln_g', 'a_ln_b', 'kv_w', 'b_w_in', 'b_w_out', 'b_ln_g', 'b_ln_b', 'loss_target', 'm_a_w_in', 'm_a_gate_b', 'm_a_conv_w', 'm_a_conv_b', 'm_a_head_g', 'm_a_w_out', 'm_a_ln_g', 'm_a_ln_b', 'm_kv_w', 'm_b_w_in', 'm_b_w_out', 'm_b_ln_g', 'm_b_ln_b', 'v_a_w_in', 'v_a_gate_b', 'v_a_conv_w', 'v_a_conv_b', 'v_a_head_g', 'v_a_w_out', 'v_a_ln_g', 'v_a_ln_b', 'v_kv_w', 'v_b_w_in', 'v_b_w_out', 'v_b_ln_g', 'v_b_ln_b']
TWIN_OUTPUTS = ['loss', 'grad_x', 'grad_a_w_in', 'grad_a_gate_b', 'grad_a_conv_w', 'grad_a_conv_b', 'grad_a_head_g', 'grad_a_w_out', 'grad_a_ln_g', 'grad_a_ln_b', 'grad_kv_w', 'grad_b_w_in', 'grad_b_w_out', 'grad_b_ln_g', 'grad_b_ln_b', 'delta_a_w_in', 'delta_a_gate_b', 'delta_a_conv_w', 'delta_a_conv_b', 'delta_a_head_g', 'delta_a_w_out', 'delta_a_ln_g', 'delta_a_ln_b', 'delta_kv_w', 'delta_b_w_in', 'delta_b_w_out', 'delta_b_ln_g', 'delta_b_ln_b', 'new_m_a_w_in', 'new_m_a_gate_b', 'new_m_a_conv_w', 'new_m_a_conv_b', 'new_m_a_head_g', 'new_m_a_w_out', 'new_m_a_ln_g', 'new_m_a_ln_b', 'new_m_kv_w', 'new_m_b_w_in', 'new_m_b_w_out', 'new_m_b_ln_g', 'new_m_b_ln_b', 'new_v_a_w_in', 'new_v_a_gate_b', 'new_v_a_conv_w', 'new_v_a_conv_b', 'new_v_a_head_g', 'new_v_a_w_out', 'new_v_a_ln_g', 'new_v_a_ln_b', 'new_v_kv_w', 'new_v_b_w_in', 'new_v_b_w_out', 'new_v_b_ln_g', 'new_v_b_ln_b']
TWIN_LEAF_KINDS = {'loss': 'loss', 'grad_x': 'grad_x', 'grad_a_w_in': 'grad_w', 'grad_a_gate_b': 'grad_w', 'grad_a_conv_w': 'grad_w', 'grad_a_conv_b': 'grad_w', 'grad_a_head_g': 'grad_w', 'grad_a_w_out': 'grad_w', 'grad_a_ln_g': 'grad_w', 'grad_a_ln_b': 'grad_w', 'grad_kv_w': 'grad_w', 'grad_b_w_in': 'grad_w', 'grad_b_w_out': 'grad_w', 'grad_b_ln_g': 'grad_w', 'grad_b_ln_b': 'grad_w', 'delta_a_w_in': 'delta_w', 'delta_a_gate_b': 'delta_w', 'delta_a_conv_w': 'delta_w', 'delta_a_conv_b': 'delta_w', 'delta_a_head_g': 'delta_w', 'delta_a_w_out': 'delta_w', 'delta_a_ln_g': 'delta_w', 'delta_a_ln_b': 'delta_w', 'delta_kv_w': 'delta_w', 'delta_b_w_in': 'delta_w', 'delta_b_w_out': 'delta_w', 'delta_b_ln_g': 'delta_w', 'delta_b_ln_b': 'delta_w', 'new_m_a_w_in': 'new_m', 'new_m_a_gate_b': 'new_m', 'new_m_a_conv_w': 'new_m', 'new_m_a_conv_b': 'new_m', 'new_m_a_head_g': 'new_m', 'new_m_a_w_out': 'new_m', 'new_m_a_ln_g': 'new_m', 'new_m_a_ln_b': 'new_m', 'new_m_kv_w': 'new_m', 'new_m_b_w_in': 'new_m', 'new_m_b_w_out': 'new_m', 'new_m_b_ln_g': 'new_m', 'new_m_b_ln_b': 'new_m', 'new_v_a_w_in': 'new_v', 'new_v_a_gate_b': 'new_v', 'new_v_a_conv_w': 'new_v', 'new_v_a_conv_b': 'new_v', 'new_v_a_head_g': 'new_v', 'new_v_a_w_out': 'new_v', 'new_v_a_ln_g': 'new_v', 'new_v_a_ln_b': 'new_v', 'new_v_kv_w': 'new_v', 'new_v_b_w_in': 'new_v', 'new_v_b_w_out': 'new_v', 'new_v_b_ln_g': 'new_v', 'new_v_b_ln_b': 'new_v'}


def _forward(args):
    return _fwd_reference(*[args[k] for k in FWD_PARAMS])


def _output_shape():
    out = _jax.eval_shape(lambda: _forward(_fwd_setup_inputs(0)))
    return out.shape, out.dtype

N_MICROBATCH = 1
ADAM_LR = 0.001
ADAM_B1 = 0.9
ADAM_B2 = 0.999
ADAM_EPS = 1e-08
ADAM_WD = 0.01
ADAM_STEP = 10
PER_EXAMPLE_BATCH_AXIS = {'x': 0, 'loss_target': 0}
SHARED_INPUTS = []
_WEIGHT_DTYPES = {'a_w_in': _jnp.float32, 'a_gate_b': _jnp.float32, 'a_conv_w': _jnp.float32, 'a_conv_b': _jnp.float32, 'a_head_g': _jnp.float32, 'a_w_out': _jnp.float32, 'a_ln_g': _jnp.float32, 'a_ln_b': _jnp.float32, 'kv_w': _jnp.float32, 'b_w_in': _jnp.float32, 'b_w_out': _jnp.float32, 'b_ln_g': _jnp.float32, 'b_ln_b': _jnp.float32}
MOMENT_SCALE = {'a_w_in': 7.324967e-03, 'a_gate_b': 4.497718e-02, 'a_conv_w': 7.586508e-03, 'a_conv_b': 8.265903e-03, 'a_head_g': 7.475184e-03, 'a_w_out': 1.448177e-02, 'a_ln_g': 2.879577e-01, 'a_ln_b': 1.437859e-01, 'kv_w': 6.609756e-03, 'b_w_in': 6.760625e-03, 'b_w_out': 1.690869e-02, 'b_ln_g': 8.005152e+00, 'b_ln_b': 2.006591e-01}


def _to_microbatches(a, axis):
    t = _jnp.moveaxis(a, axis, 0)
    t = t.reshape((N_MICROBATCH, t.shape[0] // N_MICROBATCH) + t.shape[1:])
    return _jnp.moveaxis(t, 1, axis + 1)


def setup_inputs(seed: int = 0) -> dict:
    inp = _fwd_setup_inputs(seed)
    key = _jax.random.fold_in(_jax.random.key(seed), 7919)
    shape, _ = _output_shape()
    out = dict(inp)
    out["loss_target"] = _jax.random.normal(_jax.random.fold_in(key, 0), shape, _jnp.float32)
    for i, name in enumerate(TWIN_WEIGHTS):
        w = inp[name].astype(_jnp.float32)
        if MOMENT_SCALE is None:
            s = _jnp.sqrt(_jnp.mean(_jnp.square(w)) + 1e-30)
        else:
            s = MOMENT_SCALE[name]
        km, kv = _jax.random.split(_jax.random.fold_in(key, i + 1))
        out[name] = w
        out["m_" + name] = s * _jax.random.normal(km, w.shape, _jnp.float32)
        out["v_" + name] = (s * s) * _jax.random.uniform(kv, w.shape, _jnp.float32, 0.5, 1.5)
    if N_MICROBATCH > 1:
        for name, axis in PER_EXAMPLE_BATCH_AXIS.items():
            out[name] = _to_microbatches(out[name], axis)
    return {'x': out['x'], 'a_w_in': out['a_w_in'], 'a_gate_b': out['a_gate_b'], 'a_conv_w': out['a_conv_w'], 'a_conv_b': out['a_conv_b'], 'a_head_g': out['a_head_g'], 'a_w_out': out['a_w_out'], 'a_ln_g': out['a_ln_g'], 'a_ln_b': out['a_ln_b'], 'kv_w': out['kv_w'], 'b_w_in': out['b_w_in'], 'b_w_out': out['b_w_out'], 'b_ln_g': out['b_ln_g'], 'b_ln_b': out['b_ln_b'], 'loss_target': out['loss_target'], 'm_a_w_in': out['m_a_w_in'], 'm_a_gate_b': out['m_a_gate_b'], 'm_a_conv_w': out['m_a_conv_w'], 'm_a_conv_b': out['m_a_conv_b'], 'm_a_head_g': out['m_a_head_g'], 'm_a_w_out': out['m_a_w_out'], 'm_a_ln_g': out['m_a_ln_g'], 'm_a_ln_b': out['m_a_ln_b'], 'm_kv_w': out['m_kv_w'], 'm_b_w_in': out['m_b_w_in'], 'm_b_w_out': out['m_b_w_out'], 'm_b_ln_g': out['m_b_ln_g'], 'm_b_ln_b': out['m_b_ln_b'], 'v_a_w_in': out['v_a_w_in'], 'v_a_gate_b': out['v_a_gate_b'], 'v_a_conv_w': out['v_a_conv_w'], 'v_a_conv_b': out['v_a_conv_b'], 'v_a_head_g': out['v_a_head_g'], 'v_a_w_out': out['v_a_w_out'], 'v_a_ln_g': out['v_a_ln_g'], 'v_a_ln_b': out['v_a_ln_b'], 'v_kv_w': out['v_kv_w'], 'v_b_w_in': out['v_b_w_in'], 'v_b_w_out': out['v_b_w_out'], 'v_b_ln_g': out['v_b_ln_g'], 'v_b_ln_b': out['v_b_ln_b']}


def _loss(weights, diff, rest, loss_target):
    with _jax.named_scope("forward"):
        args = {**rest, TWIN_DIFF_INPUT: diff, **{k: w.astype(_WEIGHT_DTYPES[k]) for k, w in weights.items()}}
        y = _forward(args)
    with _jax.named_scope("loss_head"):
        err = _jnp.square(y.astype(_jnp.float32) - loss_target)
        return 0.5 * _jnp.sum(_jnp.mean(err, axis=-1)) if err.ndim else 0.5 * err


def _adamw(w, g, m, v):
    m = ADAM_B1 * m + (1.0 - ADAM_B1) * g
    v = ADAM_B2 * v + (1.0 - ADAM_B2) * _jnp.square(g)
    m_hat = m / (1.0 - ADAM_B1 ** ADAM_STEP)
    v_hat = v / (1.0 - ADAM_B2 ** ADAM_STEP)
    delta = -ADAM_LR * (m_hat / (_jnp.sqrt(v_hat) + ADAM_EPS) + ADAM_WD * w)
    return delta, m, v


def reference(x, a_w_in, a_gate_b, a_conv_w, a_conv_b, a_head_g, a_w_out, a_ln_g, a_ln_b, kv_w, b_w_in, b_w_out, b_ln_g, b_ln_b, loss_target, m_a_w_in, m_a_gate_b, m_a_conv_w, m_a_conv_b, m_a_head_g, m_a_w_out, m_a_ln_g, m_a_ln_b, m_kv_w, m_b_w_in, m_b_w_out, m_b_ln_g, m_b_ln_b, v_a_w_in, v_a_gate_b, v_a_conv_w, v_a_conv_b, v_a_head_g, v_a_w_out, v_a_ln_g, v_a_ln_b, v_kv_w, v_b_w_in, v_b_w_out, v_b_ln_g, v_b_ln_b):
    given = dict(x=x, a_w_in=a_w_in, a_gate_b=a_gate_b, a_conv_w=a_conv_w, a_conv_b=a_conv_b, a_head_g=a_head_g, a_w_out=a_w_out, a_ln_g=a_ln_g, a_ln_b=a_ln_b, kv_w=kv_w, b_w_in=b_w_in, b_w_out=b_w_out, b_ln_g=b_ln_g, b_ln_b=b_ln_b, loss_target=loss_target, m_a_w_in=m_a_w_in, m_a_gate_b=m_a_gate_b, m_a_conv_w=m_a_conv_w, m_a_conv_b=m_a_conv_b, m_a_head_g=m_a_head_g, m_a_w_out=m_a_w_out, m_a_ln_g=m_a_ln_g, m_a_ln_b=m_a_ln_b, m_kv_w=m_kv_w, m_b_w_in=m_b_w_in, m_b_w_out=m_b_w_out, m_b_ln_g=m_b_ln_g, m_b_ln_b=m_b_ln_b, v_a_w_in=v_a_w_in, v_a_gate_b=v_a_gate_b, v_a_conv_w=v_a_conv_w, v_a_conv_b=v_a_conv_b, v_a_head_g=v_a_head_g, v_a_w_out=v_a_w_out, v_a_ln_g=v_a_ln_g, v_a_ln_b=v_a_ln_b, v_kv_w=v_kv_w, v_b_w_in=v_b_w_in, v_b_w_out=v_b_w_out, v_b_ln_g=v_b_ln_g, v_b_ln_b=v_b_ln_b)
    weights = {n: given[n] for n in TWIN_WEIGHTS}
    shared = {n: given[n] for n in SHARED_INPUTS}
    per_example = {n: given[n] for n in ['x']}
    grad_fn = _jax.value_and_grad(_loss, argnums=(0, 1))

    def one_microbatch(ex, loss_target):
        ex = dict(ex)
        diff = ex.pop(TWIN_DIFF_INPUT)
        return grad_fn(weights, diff, {**shared, **ex}, loss_target)

    if N_MICROBATCH == 1:
        loss, (grad_w, grad_x) = one_microbatch(per_example, given["loss_target"])
    else:
        def body(carry, xs):
            loss_sum, grad_sum = carry
            l_k, (gw_k, gx_k) = one_microbatch(xs[0], xs[1])
            with _jax.named_scope("update"):
                return (loss_sum + l_k, _jax.tree.map(_jnp.add, grad_sum, gw_k)), gx_k

        init = (_jnp.zeros((), _jnp.float32), _jax.tree.map(_jnp.zeros_like, weights))
        (loss, grad_w), grad_x = _jax.lax.scan(body, init, (per_example, given["loss_target"]))
    with _jax.named_scope("update"):
        delta_w, new_m, new_v = {}, {}, {}
        for n in TWIN_WEIGHTS:
            delta_w[n], new_m[n], new_v[n] = _adamw(weights[n], grad_w[n], given["m_" + n], given["v_" + n])
    return (loss, grad_x, *[grad_w[n] for n in TWIN_WEIGHTS], *[delta_w[n] for n in TWIN_WEIGHTS],
            *[new_m[n] for n in TWIN_WEIGHTS], *[new_v[n] for n in TWIN_WEIGHTS])
```

```python
import functools

import jax
import jax.numpy as jnp
from jax import lax
from jax.experimental import pallas as pl
from jax.experimental.pallas import tpu as pltpu

F32 = jnp.float32
BF16 = jnp.bfloat16

DEPTH = 2
ALPHA = (2.0 * DEPTH) ** 0.25
LN_EPS = 1e-5
DK_A = 128
DV_A = 256
DH_B = 128
CONV_A = 4
ADAM_LR = 0.001
ADAM_B1 = 0.9
ADAM_B2 = 0.999
ADAM_EPS = 1e-08
ADAM_WD = 0.01
ADAM_STEP = 10
N_CHIPS = 4
LANES = 128
V7X_VMEM_BYTES = 64 * 1024 * 1024
VMEM_LIMIT = (V7X_VMEM_BYTES * 3) // 4
NEG_BIG = -1e30
MESH = pl.DeviceIdType.MESH
ANY = pl.BlockSpec(memory_space=pl.ANY)


def _params(*sem):
    return pltpu.CompilerParams(dimension_semantics=sem, vmem_limit_bytes=VMEM_LIMIT)


def _dot(a, b, dims):
    return lax.dot_general(a, b, (dims, ((), ())), preferred_element_type=F32)


def _dot_nn(a, b):
    return _dot(a, b, ((1,), (0,)))


def _dot_nt(a, b):
    return _dot(a, b, ((1,), (1,)))


def _dot_tn(a, b):
    return _dot(a, b, ((0,), (0,)))


def _split2(x):
    hi = x.astype(BF16)
    lo = (x - hi.astype(F32)).astype(BF16)
    return hi, lo


def _split3(x):
    hi = x.astype(BF16)
    r = x - hi.astype(F32)
    mid = r.astype(BF16)
    lo = (r - mid.astype(F32)).astype(BF16)
    return hi, mid, lo


def _mask_dot2(x, t01):
    hi, lo = _split2(x)
    return _dot_nn(hi, t01) + _dot_nn(lo, t01)


def _mask_dot3(x, t01):
    hi, mid, lo = _split3(x)
    return _dot_nn(hi, t01) + _dot_nn(mid, t01) + _dot_nn(lo, t01)


def _mask_dot3_left(t01, x):
    hi, mid, lo = _split3(x)
    return _dot_nn(t01, hi) + _dot_nn(t01, mid) + _dot_nn(t01, lo)


def _log_sigmoid(z):
    return jnp.minimum(z, 0.0) - jnp.log(1.0 + jnp.exp(-jnp.abs(z)))


def _sigmoid(z):
    return 1.0 / (1.0 + jnp.exp(-z))


def _mm(a, b, *, mode, M, N, K, out_dtype, name, tm=1024, tn=1024, tk=1024,
        a_off=(0, 0), b_off=(0, 0), acc_in=None, acc_scale=1.0, into=None, out_off=(0, 0)):
    tm, tn, tk = min(tm, M), min(tn, N), min(tk, K)
    assert M % tm == 0 and N % tn == 0 and K % tk == 0
    nk = K // tk
    ar, ac = a_off
    br, bc = b_off
    orow, ocol = out_off
    if mode in ("nn", "nt"):
        assert ar % tm == 0 and ac % tk == 0
        a_spec = pl.BlockSpec((tm, tk), lambda i, j, k: (i + ar // tm, k + ac // tk))
    else:
        assert ar % tk == 0 and ac % tm == 0
        a_spec = pl.BlockSpec((tk, tm), lambda i, j, k: (k + ar // tk, i + ac // tm))
    if mode == "nt":
        assert br % tn == 0 and bc % tk == 0
        b_spec = pl.BlockSpec((tn, tk), lambda i, j, k: (j + br // tn, k + bc // tk))
    else:
        assert br % tk == 0 and bc % tn == 0
        b_spec = pl.BlockSpec((tk, tn), lambda i, j, k: (k + br // tk, j + bc // tn))
    assert orow % tm == 0 and ocol % tn == 0
    o_spec = pl.BlockSpec((tm, tn), lambda i, j, k: (i + orow // tm, j + ocol // tn))
    dims = {"nn": ((1,), (0,)), "nt": ((1,), (1,)), "tn": ((0,), (0,))}[mode]
    inputs, in_specs = [a, b], [a_spec, b_spec]
    has_acc = acc_in is not None
    if has_acc:
        inputs.append(acc_in)
        in_specs.append(pl.BlockSpec((tm, tn), lambda i, j, k: (i, j)))
    aliases = {}
    if into is not None:
        inputs.append(into)
        in_specs.append(ANY)
        aliases = {len(inputs) - 1: 0}
        out_shape = jax.ShapeDtypeStruct(into.shape, into.dtype)
        assert into.dtype == out_dtype
    else:
        out_shape = jax.ShapeDtypeStruct((M, N), out_dtype)

    def body(*refs):
        a_ref, b_ref = refs[0], refs[1]
        acc_in_ref = refs[2] if has_acc else None
        n_in = len(inputs)
        o_ref = refs[n_in]

        def first():
            if has_acc:
                return acc_scale * acc_in_ref[...]
            return None

        if nk == 1:
            r = _dot(a_ref[...], b_ref[...], dims)
            f = first()
            if f is not None:
                r = r + f
            o_ref[...] = r.astype(o_ref.dtype)
        else:
            acc_ref = refs[n_in + 1]
            kk = pl.program_id(2)

            @pl.when(kk == 0)
            def _():
                f = first()
                acc_ref[...] = jnp.zeros_like(acc_ref) if f is None else f

            acc_ref[...] += _dot(a_ref[...], b_ref[...], dims)

            @pl.when(kk == nk - 1)
            def _():
                o_ref[...] = acc_ref[...].astype(o_ref.dtype)

    scratch = [] if nk == 1 else [pltpu.VMEM((tm, tn), F32)]
    return pl.pallas_call(
        body, name=name, out_shape=out_shape, grid=(M // tm, N // tn, nk),
        in_specs=in_specs, out_specs=o_spec, scratch_shapes=scratch,
        input_output_aliases=aliases,
        compiler_params=_params("parallel", "parallel", "arbitrary"),
    )(*inputs)


def _shift_down(x, d, row):
    if d == 0:
        return x
    return jnp.where(row >= d, pltpu.roll(x, d, 0), 0.0)


def _shift_up(x, d, row, n):
    if d == 0:
        return x
    return jnp.where(row < n - d, pltpu.roll(x, n - d, 0), 0.0)


def _conv_pre(x, w_ref, b_ref, row):
    c = b_ref[...] + w_ref[CONV_A - 1:CONV_A, :] * x
    for k in range(CONV_A - 1):
        c = c + w_ref[k:k + 1, :] * _shift_down(x, CONV_A - 1 - k, row)
    return c


def _conv_fwd(u, conv_w, conv_b, *, S, D, name):
    tc = 256
    nq_blocks = (D // 2) // tc

    def body(u_ref, w_ref, b_ref, o_ref):
        x = u_ref[...]
        row = lax.broadcasted_iota(jnp.int32, x.shape, 0)
        c = _conv_pre(x, w_ref, b_ref, row)
        scale = jnp.where(pl.program_id(0) >= nq_blocks, DK_A ** -0.5, 1.0).astype(F32)
        o_ref[...] = (c * _sigmoid(c) * scale).astype(BF16)

    return pl.pallas_call(
        body, name=name, out_shape=jax.ShapeDtypeStruct((S, D), BF16), grid=(D // tc,),
        in_specs=[pl.BlockSpec((S, tc), lambda j: (0, j)),
                  pl.BlockSpec((CONV_A, tc), lambda j: (0, j)),
                  pl.BlockSpec((1, tc), lambda j: (0, j))],
        out_specs=pl.BlockSpec((S, tc), lambda j: (0, j)),
        compiler_params=_params("parallel"),
    )(u, conv_w, conv_b)


def _conv_bwd(u, dq, dk, conv_w, conv_b, du, *, S, D, name):
    tc = 256
    nq_blocks = (D // 2) // tc

    def body(u_ref, dq_ref, dk_ref, w_ref, b_ref, du_in, du_ref, dw_ref, db_ref):
        del du_in
        x = u_ref[...]
        n = x.shape[0]
        row = lax.broadcasted_iota(jnp.int32, x.shape, 0)
        c = _conv_pre(x, w_ref, b_ref, row)
        is_k = pl.program_id(0) >= nq_blocks
        dy = jnp.where(is_k, dk_ref[...] * (DK_A ** -0.5), dq_ref[...])
        sg = _sigmoid(c)
        dc = dy * (sg * (1.0 + c * (1.0 - sg)))
        db_ref[...] = jnp.sum(dc, axis=0, keepdims=True)
        dx = w_ref[CONV_A - 1:CONV_A, :] * dc
        dw_ref[CONV_A - 1:CONV_A, :] = jnp.sum(dc * x, axis=0, keepdims=True)
        for k in range(CONV_A - 1):
            d = CONV_A - 1 - k
            dw_ref[k:k + 1, :] = jnp.sum(dc * _shift_down(x, d, row), axis=0, keepdims=True)
            dx = dx + w_ref[k:k + 1, :] * _shift_up(dc, d, row, n)
        du_ref[...] = dx.astype(BF16)

    half = lambda j: (0, j % nq_blocks)
    return pl.pallas_call(
        body, name=name,
        out_shape=(jax.ShapeDtypeStruct(du.shape, du.dtype),
                   jax.ShapeDtypeStruct((CONV_A, D), F32), jax.ShapeDtypeStruct((1, D), F32)),
        grid=(D // tc,),
        in_specs=[pl.BlockSpec((S, tc), lambda j: (0, j)),
                  pl.BlockSpec((S, tc), half), pl.BlockSpec((S, tc), half),
                  pl.BlockSpec((CONV_A, tc), lambda j: (0, j)),
                  pl.BlockSpec((1, tc), lambda j: (0, j)), ANY],
        out_specs=(pl.BlockSpec((S, tc), lambda j: (0, j)),
                   pl.BlockSpec((CONV_A, tc), lambda j: (0, j)),
                   pl.BlockSpec((1, tc), lambda j: (0, j))),
        input_output_aliases={5: 0},
        compiler_params=_params("parallel"),
    )(u, dq, dk, conv_w, conv_b, du)


def _tri(n, cmp):
    r = lax.broadcasted_iota(jnp.int32, (n, n), 0)
    c = lax.broadcasted_iota(jnp.int32, (n, n), 1)
    return r, c, cmp(r, c)


def _gates_fwd(gt, bias, *, nha, nb, name):
    half = nha * nb
    nb_shift = nb.bit_length() - 1
    assert nb == 1 << nb_shift

    def body(g_ref, b_ref, f_ref, brow_ref):
        ig = g_ref[0:half, :] + b_ref[0:half, :]
        fg = g_ref[half:2 * half, :] + b_ref[half:2 * half, :]
        lf = _log_sigmoid(fg)
        _, _, upper = _tri(LANES, lambda r, c: r <= c)
        cs = _mask_dot3(lf, upper.astype(BF16))
        tot = jnp.broadcast_to(cs[:, LANES - 1:LANES], cs.shape)
        r, c, _ = _tri(half, lambda r, c: r <= c)
        before = jnp.logical_and(r >> nb_shift == c >> nb_shift, c < r).astype(BF16)
        f = cs + _mask_dot3_left(before, tot)
        f_ref[...] = f
        brow_ref[...] = ig - f

    return pl.pallas_call(
        body, name=name,
        out_shape=(jax.ShapeDtypeStruct((half, LANES), F32), jax.ShapeDtypeStruct((half, LANES), F32)),
    )(gt, bias)


def _gates_bwd(rowsum, colsum, gt, bias, *, nha, nb, name):
    half = nha * nb
    nb_shift = nb.bit_length() - 1
    assert nb == 1 << nb_shift

    def body(rs_ref, cs_ref, g_ref, b_ref, dg_ref, tot_ref):
        col = cs_ref[...]
        df = rs_ref[...] - col
        _, _, lower = _tri(LANES, lambda r, c: r >= c)
        rc = _mask_dot3(df, lower.astype(BF16))
        tot = jnp.broadcast_to(rc[:, 0:1], rc.shape)
        r, c, _ = _tri(half, lambda r, c: r <= c)
        same = r >> nb_shift == c >> nb_shift
        after = jnp.logical_and(same, c > r).astype(BF16)
        dlf = rc + _mask_dot3_left(after, tot)
        fg = g_ref[half:2 * half, :] + b_ref[half:2 * half, :]
        dfg = dlf * _sigmoid(-fg)
        dg_ref[0:half, :] = col
        dg_ref[half:2 * half, :] = dfg
        grp = same.astype(BF16)
        ones = jnp.ones((LANES, LANES), BF16)
        tot_ref[0:half, :] = _mask_dot3_left(grp, _mask_dot3(col, ones))
        tot_ref[half:2 * half, :] = _mask_dot3_left(grp, _mask_dot3(dfg, ones))

    return pl.pallas_call(
        body, name=name,
        out_shape=(jax.ShapeDtypeStruct((2 * half, LANES), F32), jax.ShapeDtypeStruct((2 * half, LANES), F32)),
    )(rowsum, colsum, gt, bias)


def _mlstm_tile(q, k_ref, fcol, brow_ref, j, i, tq, m=None):
    off = pl.multiple_of(j * tq, tq)
    kj = k_ref[pl.ds(off, tq), :]
    s = _dot_nt(q, kj)
    row = lax.broadcasted_iota(jnp.int32, (tq, tq), 0)
    col = lax.broadcasted_iota(jnp.int32, (tq, tq), 1)
    valid = jnp.logical_or(col <= row, j < i)
    logd = jnp.where(valid, fcol + brow_ref[0, :, pl.ds(off, tq)], NEG_BIG)
    return off, kj, s, logd


def _mlstm_fwd(qk, u, fcol, brow, *, S, D, nha, name):
    tq = min(256, S)
    nq = S // tq
    kb, vb = (D // 2) // DK_A, D // DV_A

    def body(q_ref, k_ref, v_ref, fcol_ref, brow_ref, h_ref, m_ref, den_ref):
        i = pl.program_id(1)
        q = q_ref[...]
        fc = fcol_ref[0]

        def step(j, carry):
            acc, den, m = carry
            off, _, s, logd = _mlstm_tile(q, k_ref, fc, brow_ref, j, i, tq)
            m_new = jnp.maximum(m, jnp.max(logd, axis=1, keepdims=True))
            a = s * jnp.exp(logd - m_new)
            alpha = jnp.exp(m - m_new)
            vj = v_ref[pl.ds(off, tq), :].astype(BF16)
            acc = alpha * acc + _dot_nn(a.astype(BF16), vj)
            den = alpha * den + jnp.sum(a, axis=1, keepdims=True)
            return acc, den, m_new

        acc, den, m = lax.fori_loop(
            0, i + 1, step,
            (jnp.zeros((tq, DV_A), F32), jnp.zeros((tq, 1), F32), jnp.full((tq, 1), NEG_BIG, F32)))
        h_ref[...] = acc / jnp.maximum(jnp.abs(den), jnp.exp(-m))
        m_ref[0] = m
        den_ref[0] = den

    stat = pl.BlockSpec((1, tq, 1), lambda h, i: (h, i, 0))
    return pl.pallas_call(
        body, name=name,
        out_shape=(jax.ShapeDtypeStruct((S, D), F32), jax.ShapeDtypeStruct((nha, S, 1), F32),
                   jax.ShapeDtypeStruct((nha, S, 1), F32)),
        grid=(nha, nq),
        in_specs=[pl.BlockSpec((tq, DK_A), lambda h, i: (i, h)),
                  pl.BlockSpec((S, DK_A), lambda h, i: (0, kb + h)),
                  pl.BlockSpec((S, DV_A), lambda h, i: (0, vb + h)),
                  stat, pl.BlockSpec((1, 1, S), lambda h, i: (h, 0, 0))],
        out_specs=(pl.BlockSpec((tq, DV_A), lambda h, i: (i, h)), stat, stat),
        compiler_params=_params("parallel", "arbitrary"),
    )(qk, qk, u, fcol, brow)


def _mlstm_bwd(qk, u, fcol, brow, m, den, dh, h, du, *, S, D, nha, name):
    tq = min(256, S)
    nq = S // tq
    kb, vb = (D // 2) // DK_A, D // DV_A

    def body(q_ref, k_ref, v_ref, fcol_ref, brow_ref, m_ref, den_ref, dh_ref, h_ref, du_in,
             du_ref, dq_ref, dk_ref, rs_ref, cs_ref, dv_acc):
        del du_in
        i = pl.program_id(1)

        @pl.when(i == 0)
        def _():
            dk_ref[...] = jnp.zeros_like(dk_ref)
            cs_ref[...] = jnp.zeros_like(cs_ref)
            dv_acc[...] = jnp.zeros_like(dv_acc)

        q = q_ref[...]
        fc = fcol_ref[0]
        mm = m_ref[0]
        dn = den_ref[0]
        floor = jnp.exp(-mm)
        nrm = jnp.maximum(jnp.abs(dn), floor)
        dhv = dh_ref[...]
        dnum = dhv / nrm
        dnrm = -jnp.sum(dhv * h_ref[...], axis=1, keepdims=True) / nrm
        dden = jnp.where(jnp.abs(dn) > floor, jnp.where(dn > 0.0, dnrm, -dnrm), 0.0)
        dnum_b = dnum.astype(BF16)

        def step(j, carry):
            dq, rs = carry
            off, kj, s, logd = _mlstm_tile(q, k_ref, fc, brow_ref, j, i, tq)
            p = jnp.exp(logd - mm)
            a = s * p
            vj = v_ref[pl.ds(off, tq), :].astype(BF16)
            da = _dot_nt(dnum_b, vj) + dden
            dv_acc[pl.ds(off, tq), :] += _dot_tn(a.astype(BF16), dnum_b)
            dqk = (da * p).astype(BF16)
            dq = dq + _dot_nn(dqk, kj)
            dk_ref[pl.ds(off, tq), :] += _dot_tn(dqk, q)
            pm = da * a
            cs_ref[0, :, pl.ds(off, tq)] += jnp.sum(pm, axis=0, keepdims=True)
            rs = rs + jnp.sum(pm, axis=1, keepdims=True)
            return dq, rs

        dq, rs = lax.fori_loop(0, i + 1, step, (jnp.zeros((tq, DK_A), F32), jnp.zeros((tq, 1), F32)))
        dq_ref[...] = dq
        rs_ref[0] = rs

        @pl.when(i == nq - 1)
        def _():
            du_ref[...] = dv_acc[...].astype(BF16)

    stat = pl.BlockSpec((1, tq, 1), lambda h, i: (h, i, 0))
    rowv = pl.BlockSpec((1, 1, S), lambda h, i: (h, 0, 0))
    hblk = pl.BlockSpec((tq, DV_A), lambda h, i: (i, h))
    return pl.pallas_call(
        body, name=name,
        out_shape=(jax.ShapeDtypeStruct(du.shape, du.dtype),
                   jax.ShapeDtypeStruct((S, D // 2), F32), jax.ShapeDtypeStruct((S, D // 2), F32),
                   jax.ShapeDtypeStruct((nha, S, 1), F32), jax.ShapeDtypeStruct((nha, 1, S), F32)),
        grid=(nha, nq),
        in_specs=[pl.BlockSpec((tq, DK_A), lambda h, i: (i, h)),
                  pl.BlockSpec((S, DK_A), lambda h, i: (0, kb + h)),
                  pl.BlockSpec((S, DV_A), lambda h, i: (0, vb + h)),
                  stat, rowv, stat, stat, hblk, hblk, ANY],
        out_specs=(pl.BlockSpec((S, DV_A), lambda h, i: (0, vb + h)),
                   pl.BlockSpec((tq, DK_A), lambda h, i: (i, h)),
                   pl.BlockSpec((S, DK_A), lambda h, i: (0, h)),
                   stat, rowv),
        scratch_shapes=[pltpu.VMEM((S, DV_A), F32)],
        input_output_aliases={9: 0},
        compiler_params=_params("parallel", "arbitrary"),
    )(qk, qk, u, fcol, brow, m, den, dh, h, du)


def _head_norm(hh):
    mu = jnp.mean(hh, axis=1, keepdims=True)
    hc = hh - mu
    rstd = lax.rsqrt(jnp.mean(hc * hc, axis=1, keepdims=True) + LN_EPS)
    return hc * rstd, rstd


def _hgate_fwd(h, u, head_g, *, S, D, name):
    tm = min(256, S)
    nh = D // DV_A

    def body(h_ref, o_ref, z_ref, g_ref, out_ref):
        for hd in range(nh):
            sl = slice(hd * DV_A, (hd + 1) * DV_A)
            hn, _ = _head_norm(h_ref[:, sl])
            z = z_ref[:, sl]
            out_ref[:, sl] = (_sigmoid(o_ref[:, sl]) * (hn * g_ref[:, sl]) * (z * _sigmoid(z))).astype(BF16)

    return pl.pallas_call(
        body, name=name, out_shape=jax.ShapeDtypeStruct((S, D), BF16), grid=(S // tm,),
        in_specs=[pl.BlockSpec((tm, D), lambda i: (i, 0)), pl.BlockSpec((tm, D), lambda i: (i, 2)),
                  pl.BlockSpec((tm, D), lambda i: (i, 3)), pl.BlockSpec((1, D), lambda i: (0, 0))],
        out_specs=pl.BlockSpec((tm, D), lambda i: (i, 0)),
        compiler_params=_params("parallel"),
    )(h, u, u, head_g)


def _hgate_bwd(dhg, h, u, head_g, du, *, S, D, name):
    tm = min(256, S)
    nh = D // DV_A

    def body(dhg_ref, h_ref, o_ref, z_ref, g_ref, du_in, du_ref, dh_ref, dg_ref):
        del du_in

        @pl.when(pl.program_id(0) == 0)
        def _():
            dg_ref[...] = jnp.zeros_like(dg_ref)

        for hd in range(nh):
            sl = slice(hd * DV_A, (hd + 1) * DV_A)
            hn, rstd = _head_norm(h_ref[:, sl])
            o, z, g, d = o_ref[:, sl], z_ref[:, sl], g_ref[:, sl], dhg_ref[:, sl]
            so, sz = _sigmoid(o), _sigmoid(z)
            silu_z = z * sz
            hng = hn * g
            du_ref[:, sl] = (d * hng * silu_z * so * (1.0 - so)).astype(BF16)
            du_ref[:, D + hd * DV_A:D + (hd + 1) * DV_A] = (
                d * so * hng * (sz * (1.0 + z * (1.0 - sz)))).astype(BF16)
            t = d * so * silu_z
            dg_ref[:, sl] += jnp.sum(t * hn, axis=0, keepdims=True)
            dhn = t * g
            dh_ref[:, sl] = rstd * (dhn - jnp.mean(dhn, axis=1, keepdims=True)
                                    - hn * jnp.mean(dhn * hn, axis=1, keepdims=True))

    row = lambda i: (i, 0)
    return pl.pallas_call(
        body, name=name,
        out_shape=(jax.ShapeDtypeStruct(du.shape, du.dtype), jax.ShapeDtypeStruct((S, D), F32),
                   jax.ShapeDtypeStruct((1, D), F32)),
        grid=(S // tm,),
        in_specs=[pl.BlockSpec((tm, D), row), pl.BlockSpec((tm, D), row),
                  pl.BlockSpec((tm, D), lambda i: (i, 2)), pl.BlockSpec((tm, D), lambda i: (i, 3)),
                  pl.BlockSpec((1, D), lambda i: (0, 0)), ANY],
        out_specs=(pl.BlockSpec((tm, 2 * D), lambda i: (i, 1)), pl.BlockSpec((tm, D), row),
                   pl.BlockSpec((1, D), lambda i: (0, 0))),
        input_output_aliases={5: 0},
        compiler_params=_params("arbitrary"),
    )(dhg, h, u, u, head_g, du)


def _ln_stats(r):
    mu = jnp.mean(r, axis=1, keepdims=True)
    xc = r - mu
    rstd = lax.rsqrt(jnp.mean(xc * xc, axis=1, keepdims=True) + LN_EPS)
    return xc * rstd, rstd


def _ln_back(dxhat, xhat, rstd):
    return rstd * (dxhat - jnp.mean(dxhat, axis=1, keepdims=True)
                   - xhat * jnp.mean(dxhat * xhat, axis=1, keepdims=True))


def _ln_fwd(x, y, g, b, *, S, D, name):
    tm = min(256, S)

    def body(x_ref, y_ref, g_ref, b_ref, o_ref, ob_ref):
        xhat, _ = _ln_stats(ALPHA * x_ref[...] + y_ref[...])
        o = xhat * g_ref[...] + b_ref[...]
        o_ref[...] = o
        ob_ref[...] = o.astype(BF16)

    row = lambda i: (i, 0)
    vec = pl.BlockSpec((1, D), lambda i: (0, 0))
    return pl.pallas_call(
        body, name=name,
        out_shape=(jax.ShapeDtypeStruct((S, D), F32), jax.ShapeDtypeStruct((S, D), BF16)),
        grid=(S // tm,),
        in_specs=[pl.BlockSpec((tm, D), row), pl.BlockSpec((tm, D), row), vec, vec],
        out_specs=(pl.BlockSpec((tm, D), row), pl.BlockSpec((tm, D), row)),
        compiler_params=_params("parallel"),
    )(x, y, g, b)


def _ln_loss_bwd(x1, y2, target, g, b, *, S, D, name):
    tm = min(256, S)

    def body(x_ref, y_ref, t_ref, g_ref, b_ref, dr_ref, drb_ref, dg_ref, db_ref, loss_ref):
        @pl.when(pl.program_id(0) == 0)
        def _():
            dg_ref[...] = jnp.zeros_like(dg_ref)
            db_ref[...] = jnp.zeros_like(db_ref)
            loss_ref[...] = jnp.zeros_like(loss_ref)

        xhat, rstd = _ln_stats(ALPHA * x_ref[...] + y_ref[...])
        diff = xhat * g_ref[...] + b_ref[...] - t_ref[...]
        loss_ref[...] += (0.5 / D) * jnp.sum(diff * diff)
        dx2 = diff * (1.0 / D)
        dg_ref[...] += jnp.sum(dx2 * xhat, axis=0, keepdims=True)
        db_ref[...] += jnp.sum(dx2, axis=0, keepdims=True)
        dr = _ln_back(dx2 * g_ref[...], xhat, rstd)
        dr_ref[...] = dr
        drb_ref[...] = dr.astype(BF16)

    row = lambda i: (i, 0)
    vec = pl.BlockSpec((1, D), lambda i: (0, 0))
    return pl.pallas_call(
        body, name=name,
        out_shape=(jax.ShapeDtypeStruct((S, D), F32), jax.ShapeDtypeStruct((S, D), BF16),
                   jax.ShapeDtypeStruct((1, D), F32), jax.ShapeDtypeStruct((1, D), F32),
                   jax.ShapeDtypeStruct((1, LANES), F32)),
        grid=(S // tm,),
        in_specs=[pl.BlockSpec((tm, D), row)] * 3 + [vec, vec],
        out_specs=(pl.BlockSpec((tm, D), row), pl.BlockSpec((tm, D), row), vec, vec,
                   pl.BlockSpec((1, LANES), lambda i: (0, 0))),
        compiler_params=_params("arbitrary"),
    )(x1, y2, target, g, b)


def _ln_bwd(x, y, g, dout, *, S, D, name):
    tm = min(256, S)

    def body(x_ref, y_ref, g_ref, d_ref, dr_ref, drb_ref, dg_ref, db_ref):
        @pl.when(pl.program_id(0) == 0)
        def _():
            dg_ref[...] = jnp.zeros_like(dg_ref)
            db_ref[...] = jnp.zeros_like(db_ref)

        xhat, rstd = _ln_stats(ALPHA * x_ref[...] + y_ref[...])
        d = d_ref[...]
        dg_ref[...] += jnp.sum(d * xhat, axis=0, keepdims=True)
        db_ref[...] += jnp.sum(d, axis=0, keepdims=True)
        dr = _ln_back(d * g_ref[...], xhat, rstd)
        dr_ref[...] = dr
        drb_ref[...] = dr.astype(BF16)

    row = lambda i: (i, 0)
    vec = pl.BlockSpec((1, D), lambda i: (0, 0))
    return pl.pallas_call(
        body, name=name,
        out_shape=(jax.ShapeDtypeStruct((S, D), F32), jax.ShapeDtypeStruct((S, D), BF16),
                   jax.ShapeDtypeStruct((1, D), F32), jax.ShapeDtypeStruct((1, D), F32)),
        grid=(S // tm,),
        in_specs=[pl.BlockSpec((tm, D), row), pl.BlockSpec((tm, D), row), vec, pl.BlockSpec((tm, D), row)],
        out_specs=(pl.BlockSpec((tm, D), row), pl.BlockSpec((tm, D), row), vec, vec),
        compiler_params=_params("arbitrary"),
    )(x, y, g, dout)


def _sb_scores(q, kj, j, i, tq):
    z = _dot_nt(q, kj) * (DH_B ** -0.5)
    row = lax.broadcasted_iota(jnp.int32, (tq, tq), 0)
    col = lax.broadcasted_iota(jnp.int32, (tq, tq), 1)
    valid = jnp.logical_or(col < row, j < i)
    ls = _log_sigmoid(z)
    lneg = jnp.where(valid, ls - z, 0.0)
    return valid, ls, lneg


def _sb_fwd(q2, kv, *, S, D, name):
    tq = min(128, S)
    nq = S // tq
    nh = D // DH_B

    def body(q_ref, k_ref, v_ref, o_ref, tot_ref):
        i = pl.program_id(1)
        q = q_ref[...]
        _, _, after = _tri(tq, lambda r, c: r > c)
        tri_after = after.astype(BF16)

        def step(jj, carry):
            acc, cr = carry
            j = i - jj
            off = pl.multiple_of(j * tq, tq)
            valid, ls, lneg = _sb_scores(q, k_ref[pl.ds(off, tq), :], j, i, tq)
            between = cr + _mask_dot2(lneg, tri_after)
            a = jnp.where(valid, jnp.exp(ls + between), 0.0)
            acc = acc + _dot_nn(a.astype(BF16), v_ref[pl.ds(off, tq), :])
            return acc, cr + jnp.sum(lneg, axis=1, keepdims=True)

        acc, cr = lax.fori_loop(0, i + 1, step, (jnp.zeros((tq, DH_B), F32), jnp.zeros((tq, 1), F32)))
        o_ref[...] = acc
        tot_ref[0] = cr

    return pl.pallas_call(
        body, name=name,
        out_shape=(jax.ShapeDtypeStruct((S, D), F32), jax.ShapeDtypeStruct((nh, S, 1), F32)), grid=(nh, nq),
        in_specs=[pl.BlockSpec((tq, DH_B), lambda h, i: (i, h)),
                  pl.BlockSpec((S, DH_B), lambda h, i: (0, h)),
                  pl.BlockSpec((S, DH_B), lambda h, i: (0, nh + h))],
        out_specs=(pl.BlockSpec((tq, DH_B), lambda h, i: (i, h)), pl.BlockSpec((1, tq, 1), lambda h, i: (h, i, 0))),
        compiler_params=_params("parallel", "arbitrary"),
    )(q2, kv, kv)


def _sb_bwd(q2, kv, datt, tot, *, S, D, name):
    tq = min(128, S)
    nq = S // tq
    nh = D // DH_B

    def body(q_ref, k_ref, v_ref, do_ref, tot_ref, dq_ref, dk_ref, dv_ref, dk_acc, dv_acc):
        i = pl.program_id(1)

        @pl.when(i == 0)
        def _():
            dk_acc[...] = jnp.zeros_like(dk_acc)
            dv_acc[...] = jnp.zeros_like(dv_acc)

        q = q_ref[...]
        do_b = do_ref[...].astype(BF16)
        total = tot_ref[0]
        _, _, upto = _tri(tq, lambda r, c: r <= c)
        tri_upto = upto.astype(BF16)
        _, _, before = _tri(tq, lambda r, c: r < c)
        tri_before = before.astype(BF16)

        def step(j, carry):
            dq, cl, cg = carry
            off = pl.multiple_of(j * tq, tq)
            kj = k_ref[pl.ds(off, tq), :]
            valid, ls, lneg = _sb_scores(q, kj, j, i, tq)
            between = total - (cl + _mask_dot3(lneg, tri_upto))
            a = jnp.where(valid, jnp.exp(ls + between), 0.0)
            g = _dot_nt(do_b, v_ref[pl.ds(off, tq), :]) * a
            dv_acc[pl.ds(off, tq), :] += _dot_tn(a.astype(BF16), do_b)
            e = cg + _mask_dot2(g, tri_before)
            dz = jnp.where(valid, g * jnp.exp(lneg) - e * jnp.exp(ls), 0.0) * (DH_B ** -0.5)
            dz_b = dz.astype(BF16)
            dq = dq + _dot_nn(dz_b, kj)
            dk_acc[pl.ds(off, tq), :] += _dot_tn(dz_b, q)
            return (dq, cl + jnp.sum(lneg, axis=1, keepdims=True), cg + jnp.sum(g, axis=1, keepdims=True))

        zero = jnp.zeros((tq, 1), F32)
        dq, _, _ = lax.fori_loop(0, i + 1, step, (jnp.zeros((tq, DH_B), F32), zero, zero))
        dq_ref[...] = dq.astype(BF16)

        @pl.when(i == nq - 1)
        def _():
            dk_ref[...] = dk_acc[...].astype(BF16)
            dv_ref[...] = dv_acc[...].astype(BF16)

    blk = pl.BlockSpec((tq, DH_B), lambda h, i: (i, h))
    return pl.pallas_call(
        body, name=name,
        out_shape=(jax.ShapeDtypeStruct((S, D), BF16), jax.ShapeDtypeStruct((S, D), BF16),
                   jax.ShapeDtypeStruct((S, D), BF16)),
        grid=(nh, nq),
        in_specs=[blk, pl.BlockSpec((S, DH_B), lambda h, i: (0, h)),
                  pl.BlockSpec((S, DH_B), lambda h, i: (0, nh + h)), blk,
                  pl.BlockSpec((1, tq, 1), lambda h, i: (h, i, 0))],
        out_specs=(blk, pl.BlockSpec((S, DH_B), lambda h, i: (0, h)),
                   pl.BlockSpec((S, DH_B), lambda h, i: (0, h))),
        scratch_shapes=[pltpu.VMEM((S, DH_B), F32), pltpu.VMEM((S, DH_B), F32)],
        compiler_params=_params("parallel", "arbitrary"),
    )(q2, kv, kv, datt, tot)


def _bgate_fwd(att, z2, *, S, D, name):
    tm = min(256, S)

    def body(a_ref, z_ref, o_ref):
        z = z_ref[...]
        o_ref[...] = (a_ref[...] * (z * _sigmoid(z))).astype(BF16)

    row = lambda i: (i, 0)
    return pl.pallas_call(
        body, name=name, out_shape=jax.ShapeDtypeStruct((S, D), BF16), grid=(S // tm,),
        in_specs=[pl.BlockSpec((tm, D), row)] * 2, out_specs=pl.BlockSpec((tm, D), row),
        compiler_params=_params("parallel"),
    )(att, z2)


def _bgate_bwd(dhb, att, z2, *, S, D, name):
    tm = min(256, S)

    def body(d_ref, a_ref, z_ref, da_ref, dz_ref):
        z, d = z_ref[...], d_ref[...]
        sz = _sigmoid(z)
        da_ref[...] = d * (z * sz)
        dz_ref[...] = (d * a_ref[...] * (sz * (1.0 + z * (1.0 - sz)))).astype(BF16)

    row = lambda i: (i, 0)
    return pl.pallas_call(
        body, name=name,
        out_shape=(jax.ShapeDtypeStruct((S, D), F32), jax.ShapeDtypeStruct((S, D), BF16)),
        grid=(S // tm,),
        in_specs=[pl.BlockSpec((tm, D), row)] * 3,
        out_specs=(pl.BlockSpec((tm, D), row), pl.BlockSpec((tm, D), row)),
        compiler_params=_params("parallel"),
    )(dhb, att, z2)


def _adamw(w, g, m, v, *, name, tr=128):
    rows, cols = w.shape
    tr = min(tr, rows)
    assert rows % tr == 0

    def body(w_ref, g_ref, m_ref, v_ref, d_ref, mo_ref, vo_ref):
        gg = g_ref[...]
        mn = ADAM_B1 * m_ref[...] + (1.0 - ADAM_B1) * gg
        vn = ADAM_B2 * v_ref[...] + (1.0 - ADAM_B2) * (gg * gg)
        m_hat = mn / (1.0 - ADAM_B1 ** ADAM_STEP)
        v_hat = vn / (1.0 - ADAM_B2 ** ADAM_STEP)
        d_ref[...] = -ADAM_LR * (m_hat / (jnp.sqrt(v_hat) + ADAM_EPS) + ADAM_WD * w_ref[...])
        mo_ref[...] = mn
        vo_ref[...] = vn

    blk = pl.BlockSpec((tr, cols), lambda i: (i, 0))
    sd = jax.ShapeDtypeStruct((rows, cols), F32)
    return pl.pallas_call(
        body, name=name, out_shape=(sd, sd, sd), grid=(rows // tr,),
        in_specs=[blk] * 4, out_specs=(blk, blk, blk),
        compiler_params=_params("parallel"),
    )(w, g, m, v)


def _local_step(x, target, wa, wg, wao, wbi, wkv, wbo, gate_b, conv_w, conv_b, head_g,
                a_ln_g, a_ln_b, b_ln_g, b_ln_b, *, S, D, nha):
    nb = S // LANES
    kw = dict(S=S, D=D)
    xb = x.astype(BF16)
    u = _mm(xb, wa, mode="nn", M=S, N=4 * D, K=D, out_dtype=F32, name="a_in")
    ug = _mm(xb, wg, mode="nn", M=S, N=LANES, K=D, out_dtype=F32, name="a_in_gates")
    qk = _conv_fwd(u, conv_w, conv_b, name="conv_fwd", **kw)
    gt = ug[:, :2 * nha].T.reshape(2 * nha * nb, LANES)
    gbias = jnp.repeat(gate_b.reshape(2 * nha), nb).reshape(2 * nha * nb, 1)
    fcs, brow = _gates_fwd(gt, gbias, nha=nha, nb=nb, name="gates_fwd")
    fcol = fcs.reshape(nha, S, 1)
    brow = brow.reshape(nha, 1, S)
    h, m, den = _mlstm_fwd(qk, u, fcol, brow, nha=nha, name="mlstm_fwd", **kw)
    hg = _hgate_fwd(h, u, head_g, name="hgate_fwd", **kw)
    y = _mm(hg, wao, mode="nn", M=S, N=D, K=D, out_dtype=F32, name="a_out")
    x1, x1b = _ln_fwd(x, y, a_ln_g, a_ln_b, name="ln_a_fwd", **kw)
    q2 = _mm(x1b, wbi, mode="nn", M=S, N=D, K=D, out_dtype=BF16, name="b_in_q")
    z2 = _mm(x1b, wbi, mode="nn", M=S, N=D, K=D, out_dtype=F32, name="b_in_z", b_off=(0, D))
    kv = _mm(x1b, wkv, mode="nn", M=S, N=2 * D, K=D, out_dtype=BF16, name="b_kv")
    att, sb_tot = _sb_fwd(q2, kv, name="sb_fwd", **kw)
    hb = _bgate_fwd(att, z2, name="bgate_fwd", **kw)
    y2 = _mm(hb, wbo, mode="nn", M=S, N=D, K=D, out_dtype=F32, name="b_out")
    dr2, dr2b, d_bln_g, d_bln_b, loss = _ln_loss_bwd(x1, y2, target, b_ln_g, b_ln_b, name="ln_b_loss", **kw)
    g_wbo = _mm(hb, dr2b, mode="tn", M=D, N=D, K=S, out_dtype=BF16, name="g_b_out")
    dhb = _mm(dr2b, wbo, mode="nt", M=S, N=D, K=D, out_dtype=F32, name="d_b_out")
    datt, dz2 = _bgate_bwd(dhb, att, z2, name="bgate_bwd", **kw)
    dq2, dk2, dv2 = _sb_bwd(q2, kv, datt, sb_tot, name="sb_bwd", **kw)
    g_wbi = _mm(x1b, dq2, mode="tn", M=D, N=D, K=S, out_dtype=BF16, name="g_b_in_q",
                into=lax.empty((D, 2 * D), BF16))
    g_wbi = _mm(x1b, dz2, mode="tn", M=D, N=D, K=S, out_dtype=BF16, name="g_b_in_z", into=g_wbi, out_off=(0, D))
    g_wkv = _mm(x1b, dk2, mode="tn", M=D, N=D, K=S, out_dtype=BF16, name="g_kv_k",
                into=lax.empty((D, 2 * D), BF16))
    g_wkv = _mm(x1b, dv2, mode="tn", M=D, N=D, K=S, out_dtype=BF16, name="g_kv_v", into=g_wkv, out_off=(0, D))
    dx1 = _mm(dq2, wbi, mode="nt", M=S, N=D, K=D, out_dtype=F32, name="d_b_in_q", acc_in=dr2, acc_scale=ALPHA)
    dx1 = _mm(dz2, wbi, mode="nt", M=S, N=D, K=D, out_dtype=F32, name="d_b_in_z", b_off=(0, D), acc_in=dx1)
    dx1 = _mm(dk2, wkv, mode="nt", M=S, N=D, K=D, out_dtype=F32, name="d_kv_k", acc_in=dx1)
    dx1 = _mm(dv2, wkv, mode="nt", M=S, N=D, K=D, out_dtype=F32, name="d_kv_v", b_off=(0, D), acc_in=dx1)
    dr, drb, d_aln_g, d_aln_b = _ln_bwd(x, y, a_ln_g, dx1, name="ln_a_bwd", **kw)
    g_wao = _mm(hg, drb, mode="tn", M=D, N=D, K=S, out_dtype=BF16, name="g_a_out")
    dhg = _mm(drb, wao, mode="nt", M=S, N=D, K=D, out_dtype=F32, name="d_a_out")
    du = lax.empty((S, 4 * D), BF16)
    du, dh, d_head_g = _hgate_bwd(dhg, h, u, head_g, du, name="hgate_bwd", **kw)
    du, dq, dk, rowsum, colsum = _mlstm_bwd(qk, u, fcol, brow, m, den, dh, h, du, nha=nha, name="mlstm_bwd", **kw)
    dgt, dgtot = _gates_bwd(rowsum.reshape(nha * nb, LANES), colsum.reshape(nha * nb, LANES), gt, gbias,
                            nha=nha, nb=nb, name="gates_bwd")
    d_gate_b = dgtot[::nb, 0].reshape(1, 2 * nha)
    dgp = jnp.pad(dgt.reshape(2 * nha, S).T, ((0, 0), (0, LANES - 2 * nha))).astype(BF16)
    du, d_conv_w, d_conv_b = _conv_bwd(u, dq, dk, conv_w, conv_b, du, name="conv_bwd", **kw)
    g_wa = _mm(xb, du, mode="tn", M=D, N=4 * D, K=S, out_dtype=BF16, name="g_a_in")
    g_wg = _mm(xb, dgp, mode="tn", M=D, N=LANES, K=S, out_dtype=BF16, name="g_a_in_gates")
    dx = _mm(du, wa, mode="nt", M=S, N=D, K=4 * D, out_dtype=F32, name="d_a_in", acc_in=dr, acc_scale=ALPHA)
    dx = _mm(dgp, wg, mode="nt", M=S, N=D, K=LANES, out_dtype=F32, name="d_a_in_gates", acc_in=dx)
    big = dict(wa=g_wa, wg=g_wg, wao=g_wao, wbi=g_wbi, wkv=g_wkv, wbo=g_wbo)
    small = dict(gate_b=d_gate_b, conv_w=d_conv_w, conv_b=d_conv_b, head_g=d_head_g,
                 a_ln_g=d_aln_g, a_ln_b=d_aln_b, b_ln_g=d_bln_g, b_ln_b=d_bln_b)
    return loss, dx, big, small


def _coords():
    return lax.axis_index("x"), lax.axis_index("y"), lax.axis_index("c")


def _other_chips(x, y):
    return [(1 - x, y), (x, 1 - y), (1 - x, 1 - y)]


def _rows(ref, start, size):
    return ref.at[pl.ds(pl.multiple_of(start, size), size), :]


def _window(kind, ref, shard_shape, j, hf=None):
    r, cw = shard_shape
    row0, nr = (0, r) if hf is None else (hf * (r // 2), r // 2)
    if kind == "stack":
        return ref.at[j, pl.ds(pl.multiple_of(row0, nr), nr), :]
    if kind == "rows":
        return ref.at[pl.ds(pl.multiple_of(j * r + row0, nr), nr), :]
    assert kind == "cols"
    return ref.at[pl.ds(pl.multiple_of(row0, nr), nr), pl.ds(pl.multiple_of(j * cw, cw), cw)]


def _gathered_shape(kind, shard_shape):
    r, cw = shard_shape
    return {"stack": (N_CHIPS, r, cw), "rows": (N_CHIPS * r, cw), "cols": (r, N_CHIPS * cw)}[kind]


def _remote(src, dst, send_sems, recv_sems, k, to):
    return pltpu.make_async_remote_copy(src_ref=src, dst_ref=dst, send_sem=send_sems.at[k],
                                        recv_sem=recv_sems.at[k], device_id=to, device_id_type=MESH)


def _allgather_weights(shards, kinds, small):
    n = len(shards)
    shapes = [s.shape for s in shards]
    w = small.shape[1]

    def body(*refs):
        s_refs, sm_ref = refs[:n], refs[n]
        g_refs, smg_ref = refs[n + 1:2 * n + 1], refs[2 * n + 1]
        send_sems, recv_sems, local_sems = refs[2 * n + 2:]
        x, y, c = _coords()
        me, sibling, chips = 2 * x + y, (x, y, 1 - c), _other_chips(x, y)
        ids = [2 * cx + cy for cx, cy in chips]

        def small_win(j):
            return smg_ref.at[:, pl.ds(pl.multiple_of(j * w, w), w)]

        own = [pltpu.make_async_copy(s_refs[a], _window(kinds[a], g_refs[a], shapes[a], me), local_sems.at[a])
               for a in range(n)]
        own.append(pltpu.make_async_copy(sm_ref, small_win(me), local_sems.at[n]))
        for cp in own:
            cp.start()
        first = []
        for a in range(n):
            src = _rows(s_refs[a], c * (shapes[a][0] // 2), shapes[a][0] // 2)
            dst = _window(kinds[a], g_refs[a], shapes[a], me, c)
            first += [_remote(src, dst, send_sems, recv_sems, 3 * a + k, (*chip, c)) for k, chip in enumerate(chips)]
        first += [_remote(sm_ref, small_win(me), send_sems, recv_sems, 6 * n + k, (*chip, c))
                  for k, chip in enumerate(chips)]
        for cp in first:
            cp.start()
        passed = []
        for a in range(n):
            for k in range(3):
                win = _window(kinds[a], g_refs[a], shapes[a], ids[k], c)
                _remote(win, win, send_sems, recv_sems, 3 * a + k, sibling).wait_recv()
                fwd = _remote(win, win, send_sems, recv_sems, 3 * n + 3 * a + k, sibling)
                fwd.start()
                passed.append(fwd)
        for k in range(3):
            win = small_win(ids[k])
            _remote(win, win, send_sems, recv_sems, 6 * n + k, sibling).wait_recv()
        for a in range(n):
            for k in range(3):
                win = _window(kinds[a], g_refs[a], shapes[a], ids[k], 1 - c)
                _remote(win, win, send_sems, recv_sems, 3 * n + 3 * a + k, sibling).wait_recv()
        for cp in first + passed:
            cp.wait_send()
        for cp in own:
            cp.wait()

    out_shape = [jax.ShapeDtypeStruct(_gathered_shape(kinds[a], shapes[a]), shards[a].dtype) for a in range(n)]
    out_shape.append(jax.ShapeDtypeStruct((small.shape[0], N_CHIPS * w), small.dtype))
    nsem = 6 * n + 3
    return pl.pallas_call(
        body, name="allgather_weights", out_shape=tuple(out_shape),
        in_specs=[ANY] * (n + 1), out_specs=tuple([ANY] * (n + 1)),
        scratch_shapes=[pltpu.SemaphoreType.DMA((nsem,)), pltpu.SemaphoreType.DMA((nsem,)),
                        pltpu.SemaphoreType.DMA((n + 1,))],
    )(*shards, small)


def _rs_pair_exchange(views):
    n = len(views)

    def body(*refs):
        g_refs, p_refs = refs[:n], refs[n:2 * n]
        send_sems, recv_sems = refs[2 * n:]
        x, y, c = _coords()
        cps = []
        for a in range(n):
            r2 = views[a].shape[1] // 2
            src = g_refs[a].at[:, pl.ds(pl.multiple_of((1 - c) * r2, r2), r2), :]
            cps.append(_remote(src, p_refs[a], send_sems, recv_sems, a, (x, y, 1 - c)))
        for cp in cps:
            cp.start()
        for cp in cps:
            cp.wait()

    out_shape = tuple(jax.ShapeDtypeStruct((v.shape[0], v.shape[1] // 2, v.shape[2]), v.dtype) for v in views)
    return pl.pallas_call(
        body, name="rs_pair_exchange", out_shape=out_shape,
        in_specs=[ANY] * n, out_specs=tuple([ANY] * n),
        scratch_shapes=[pltpu.SemaphoreType.DMA((n,)), pltpu.SemaphoreType.DMA((n,))],
    )(*views)


def _add_half(view, part, core, *, name):
    nch, r, cols = view.shape
    r2 = r // 2
    tr = min(128, r2)
    nt = r2 // tr

    def body(c_ref, g_ref, p_ref, o_ref):
        del c_ref
        o_ref[...] = (g_ref[...].astype(F32) + p_ref[...].astype(F32)).astype(o_ref.dtype)

    return pl.pallas_call(
        body, name=name, out_shape=jax.ShapeDtypeStruct((nch, r2, cols), view.dtype),
        grid_spec=pltpu.PrefetchScalarGridSpec(
            num_scalar_prefetch=1, grid=(nch, nt),
            in_specs=[pl.BlockSpec((1, tr, cols), lambda ch, i, c_ref: (ch, c_ref[0] * nt + i, 0)),
                      pl.BlockSpec((1, tr, cols), lambda ch, i, c_ref: (ch, i, 0))],
            out_specs=pl.BlockSpec((1, tr, cols), lambda ch, i, c_ref: (ch, i, 0))),
        compiler_params=_params("parallel", "parallel"),
    )(core, view, part)


def _chunk(kind, ref, j, cw):
    if kind == "cols":
        return ref.at[0, :, pl.ds(pl.multiple_of(j * cw, cw), cw)]
    return ref.at[j]


def _rs_chips(pairs, kinds):
    n = len(pairs)
    half_shapes = []
    for a in range(n):
        nch, r2, cols = pairs[a].shape
        half_shapes.append((r2, cols // N_CHIPS) if kinds[a] == "cols" else (r2, cols))

    def body(*refs):
        q_refs, t_refs = refs[:n], refs[n:2 * n]
        send_sems, recv_sems, local_sems = refs[2 * n:]
        x, y, c = _coords()
        me, chips = 2 * x + y, _other_chips(x, y)
        ids = [2 * cx + cy for cx, cy in chips]
        own = [pltpu.make_async_copy(_chunk(kinds[a], q_refs[a], me, half_shapes[a][1]), t_refs[a].at[me],
                                     local_sems.at[a]) for a in range(n)]
        for cp in own:
            cp.start()
        sends = []
        for a in range(n):
            for k, chip in enumerate(chips):
                src = _chunk(kinds[a], q_refs[a], ids[k], half_shapes[a][1])
                sends.append(_remote(src, t_refs[a].at[me], send_sems, recv_sems, 3 * a + k, (*chip, c)))
        for cp in sends:
            cp.start()
        for a in range(n):
            for k in range(3):
                slot = t_refs[a].at[ids[k]]
                _remote(slot, slot, send_sems, recv_sems, 3 * a + k, (x, y, c)).wait_recv()
        for cp in sends:
            cp.wait_send()
        for cp in own:
            cp.wait()

    out_shape = tuple(jax.ShapeDtypeStruct((N_CHIPS, *half_shapes[a]), pairs[a].dtype) for a in range(n))
    return pl.pallas_call(
        body, name="rs_chips", out_shape=out_shape,
        in_specs=[ANY] * n, out_specs=tuple([ANY] * n),
        scratch_shapes=[pltpu.SemaphoreType.DMA((3 * n,)), pltpu.SemaphoreType.DMA((3 * n,)),
                        pltpu.SemaphoreType.DMA((n,))],
    )(*pairs)


def _sum_chips(parts, *, name):
    _, r, cols = parts.shape
    tr = min(128, r)

    def body(p_ref, o_ref):
        acc = p_ref[0].astype(F32)
        for s in range(1, N_CHIPS):
            acc = acc + p_ref[s].astype(F32)
        o_ref[...] = acc

    return pl.pallas_call(
        body, name=name, out_shape=jax.ShapeDtypeStruct((r, cols), F32), grid=(r // tr,),
        in_specs=[pl.BlockSpec((N_CHIPS, tr, cols), lambda i: (0, i, 0))],
        out_specs=pl.BlockSpec((tr, cols), lambda i: (i, 0)),
        compiler_params=_params("parallel"),
    )(parts)


def _rs_share(halves):
    n = len(halves)

    def body(*refs):
        h_refs, g_refs = refs[:n], refs[n:2 * n]
        send_sems, recv_sems, local_sems = refs[2 * n:]
        x, y, c = _coords()
        own, cps = [], []
        for a in range(n):
            hr = halves[a].shape[0]
            mine = _rows(g_refs[a], c * hr, hr)
            own.append(pltpu.make_async_copy(h_refs[a], mine, local_sems.at[a]))
            cps.append(_remote(h_refs[a], mine, send_sems, recv_sems, a, (x, y, 1 - c)))
        for cp in own + cps:
            cp.start()
        for a in range(n):
            hr = halves[a].shape[0]
            theirs = _rows(g_refs[a], (1 - c) * hr, hr)
            _remote(theirs, theirs, send_sems, recv_sems, a, (x, y, 1 - c)).wait_recv()
        for cp in cps:
            cp.wait_send()
        for cp in own:
            cp.wait()

    out_shape = tuple(jax.ShapeDtypeStruct((2 * h.shape[0], h.shape[1]), h.dtype) for h in halves)
    return pl.pallas_call(
        body, name="rs_share", out_shape=out_shape,
        in_specs=[ANY] * n, out_specs=tuple([ANY] * n),
        scratch_shapes=[pltpu.SemaphoreType.DMA((n,)), pltpu.SemaphoreType.DMA((n,)),
                        pltpu.SemaphoreType.DMA((n,))],
    )(*halves)


def _allreduce_small(vec):
    m_per, ncol = vec.shape
    n_dev = 2 * N_CHIPS

    def body(x_ref, out_ref, sum_ref, send_sems, recv_sems, local_sem):
        x, y, c = _coords()
        me, sibling = (x, y, c), (x, y, 1 - c)
        chips = _other_chips(x, y)

        def rows(px, py, pc):
            return out_ref.at[pl.ds(pl.multiple_of((4 * px + 2 * py + pc) * m_per, m_per), m_per), :]

        def copy(k, block, to, src=None):
            return _remote(rows(*block) if src is None else src, rows(*block), send_sems, recv_sems, k, to)

        mine = pltpu.make_async_copy(x_ref, rows(*me), local_sem)
        mine.start()
        first = [copy(0, me, sibling, src=x_ref)]
        first += [copy(1 + j, me, (*chip, c), src=x_ref) for j, chip in enumerate(chips)]
        for cp in first:
            cp.start()
        passed = [copy(4 + j, (*chip, c), sibling) for j, chip in enumerate(chips)]
        for j, chip in enumerate(chips):
            copy(1 + j, (*chip, c), me).wait_recv()
            passed[j].start()
        copy(0, sibling, me).wait_recv()
        for j, chip in enumerate(chips):
            copy(4 + j, (*chip, 1 - c), me).wait_recv()
        for cp in first + passed:
            cp.wait_send()
        mine.wait()
        acc = out_ref[0:m_per, :]
        for d in range(1, n_dev):
            acc = acc + out_ref[d * m_per:(d + 1) * m_per, :]
        sum_ref[...] = acc

    vm = pl.BlockSpec(memory_space=pltpu.VMEM)
    return pl.pallas_call(
        body, name="allreduce_small",
        out_shape=(jax.ShapeDtypeStruct((n_dev * m_per, ncol), vec.dtype), jax.ShapeDtypeStruct((m_per, ncol), vec.dtype)),
        in_specs=[vm], out_specs=(vm, vm),
        scratch_shapes=[pltpu.SemaphoreType.DMA((7,)), pltpu.SemaphoreType.DMA((7,)), pltpu.SemaphoreType.DMA],
    )(vec)[1]


def _pack_rows(pieces, total_rows):
    rows = []
    for p in pieces:
        flat = p.reshape(-1)
        flat = jnp.pad(flat, (0, (-flat.shape[0]) % LANES))
        rows.append(flat.reshape(-1, LANES))
    out = jnp.concatenate(rows, axis=0)
    return jnp.pad(out, ((0, total_rows - out.shape[0]), (0, 0)))


def _unpack_rows(packed, shapes):
    out, r = [], 0
    for shp in shapes:
        size = 1
        for d in shp:
            size *= d
        nr = -(-size // LANES)
        out.append(packed[r:r + nr].reshape(-1)[:size].reshape(shp))
        r += nr
    return out


def _round_up(v, m):
    return -(-v // m) * m


def kernel(x, a_w_in, a_gate_b, a_conv_w, a_conv_b, a_head_g, a_w_out, a_ln_g, a_ln_b, kv_w, b_w_in, b_w_out, b_ln_g, b_ln_b, loss_target, m_a_w_in, m_a_gate_b, m_a_conv_w, m_a_conv_b, m_a_head_g, m_a_w_out, m_a_ln_g, m_a_ln_b, m_kv_w, m_b_w_in, m_b_w_out, m_b_ln_g, m_b_ln_b, v_a_w_in, v_a_gate_b, v_a_conv_w, v_a_conv_b, v_a_head_g, v_a_w_out, v_a_ln_g, v_a_ln_b, v_kv_w, v_b_w_in, v_b_w_out, v_b_ln_g, v_b_ln_b):
    _, S, D = x.shape
    nha = a_gate_b.shape[1] // 2
    chip = 2 * lax.axis_index("x") + lax.axis_index("y")
    core = lax.axis_index("c").astype(jnp.int32).reshape(1)
    dq = D // N_CHIPS

    shards = [a_w_in[0].astype(BF16), a_w_out[0].astype(BF16), kv_w.astype(BF16), b_w_in[0].astype(BF16),
              b_w_out[0].astype(BF16)]
    kinds = ["stack", "rows", "cols", "cols", "rows"]
    small_shard = jnp.concatenate([a_conv_w[0], a_conv_b, a_head_g, a_ln_g, a_ln_b], axis=0)
    wa_g, wao, wkv, wbi, wbo, small_full = _allgather_weights(shards, kinds, small_shard)
    wa_full = jnp.concatenate([wa_g[j] for j in range(N_CHIPS)], axis=1)
    wa = wa_full[:, :4 * D]
    wg = jnp.pad(wa_full[:, 4 * D:], ((0, 0), (0, LANES - 2 * nha)))
    conv_w, conv_b, head_g, ln_g_a, ln_b_a = (small_full[0:CONV_A], small_full[4:5], small_full[5:6],
                                              small_full[6:7], small_full[7:8])

    loss_row, grad_x, big, small = _local_step(
        x[0], loss_target[0], wa, wg, wao, wbi, wkv, wbo, a_gate_b, conv_w, conv_b, head_g,
        ln_g_a, ln_b_a, b_ln_g, b_ln_b, S=S, D=D, nha=nha)

    g_full = jnp.concatenate([big["wa"], big["wg"][:, :2 * nha]], axis=1)
    ca = g_full.shape[1] // N_CHIPS
    g_wa4 = jnp.stack([g_full[:, j * ca:(j + 1) * ca] for j in range(N_CHIPS)], axis=0)
    views = [g_wa4, big["wao"].reshape(N_CHIPS, dq, D), big["wkv"][None], big["wbi"][None],
             big["wbo"].reshape(N_CHIPS, dq, D)]
    rs_kinds = ["stack", "stack", "cols", "cols", "stack"]
    names = ["a_w_in", "a_w_out", "kv_w", "b_w_in", "b_w_out"]
    parts = _rs_pair_exchange(views)
    pairs = [_add_half(v, p, core, name="pair_sum_" + nm) for v, p, nm in zip(views, parts, names)]
    slots = _rs_chips(pairs, rs_kinds)
    halves = [_sum_chips(t, name="chip_sum_" + nm) for t, nm in zip(slots, names)]
    g_big = _rs_share(halves)

    w_big = [a_w_in[0], a_w_out[0], kv_w, b_w_in[0], b_w_out[0]]
    m_big = [m_a_w_in[0], m_a_w_out[0], m_kv_w, m_b_w_in[0], m_b_w_out[0]]
    v_big = [v_a_w_in[0], v_a_w_out[0], v_kv_w, v_b_w_in[0], v_b_w_out[0]]
    upd_big = [_adamw(w, g, m, v, name="adamw_" + nm) for w, g, m, v, nm in zip(w_big, g_big, m_big, v_big, names)]

    order = ["conv_w", "conv_b", "head_g", "a_ln_g", "a_ln_b", "b_ln_g", "b_ln_b", "gate_b"]
    full_shapes = [(CONV_A, D), (1, D), (1, D), (1, D), (1, D), (1, D), (1, D), (1, 2 * nha)]
    n_rows = sum(-(-(s[0] * s[1]) // LANES) for s in full_shapes) + 1
    packed = _pack_rows([small[k] for k in order] + [loss_row], _round_up(n_rows, 8))
    total = _allreduce_small(packed)
    sums = dict(zip(order, _unpack_rows(total, full_shapes)))
    loss = total[n_rows - 1, 0]

    def mine(v):
        return lax.dynamic_slice_in_dim(v, chip * dq, dq, axis=1)

    g_small = [sums["gate_b"], mine(sums["conv_w"]), mine(sums["conv_b"]), mine(sums["head_g"]),
               mine(sums["a_ln_g"]), mine(sums["a_ln_b"]), sums["b_ln_g"], sums["b_ln_b"]]
    w_small = [a_gate_b, a_conv_w[0], a_conv_b, a_head_g, a_ln_g, a_ln_b, b_ln_g, b_ln_b]
    m_small = [m_a_gate_b, m_a_conv_w[0], m_a_conv_b, m_a_head_g, m_a_ln_g, m_a_ln_b, m_b_ln_g, m_b_ln_b]
    v_small = [v_a_gate_b, v_a_conv_w[0], v_a_conv_b, v_a_head_g, v_a_ln_g, v_a_ln_b, v_b_ln_g, v_b_ln_b]
    shp_small = [w.shape for w in w_small]
    rows_small = _round_up(sum(-(-(s[0] * s[1]) // LANES) for s in shp_small), 8)
    upd_small = _adamw(_pack_rows(w_small, rows_small), _pack_rows(g_small, rows_small),
                       _pack_rows(m_small, rows_small), _pack_rows(v_small, rows_small), name="adamw_small")
    d_small, mn_small, vn_small = (_unpack_rows(u, shp_small) for u in upd_small)

    def assemble(big5, small8):
        awi, awo, kvw, bwi, bwo = big5
        gb, cw, cb, hg, alg, alb, blg, blb = small8
        return [awi[None], gb, cw[None], cb, hg, awo[None], alg, alb, kvw, bwi[None], bwo[None], blg, blb]

    grads = assemble(g_big, g_small)
    deltas = assemble([u[0] for u in upd_big], d_small)
    new_m = assemble([u[1] for u in upd_big], mn_small)
    new_v = assemble([u[2] for u in upd_big], vn_small)
    return (loss, grad_x[None], *grads, *deltas, *new_m, *new_v)
```

```python
import functools

import jax
import jax.numpy as jnp
from jax import lax
from jax.experimental import pallas as pl
from jax.experimental.pallas import tpu as pltpu

F32 = jnp.float32
BF16 = jnp.bfloat16

DEPTH = 2
ALPHA = (2.0 * DEPTH) ** 0.25
LN_EPS = 1e-5
DK_A = 128
DV_A = 256
DH_B = 128
CONV_A = 4
ADAM_LR = 0.001
ADAM_B1 = 0.9
ADAM_B2 = 0.999
ADAM_EPS = 1e-08
ADAM_WD = 0.01
ADAM_STEP = 10
N_CHIPS = 4
LANES = 128
V7X_VMEM_BYTES = 64 * 1024 * 1024
VMEM_LIMIT = (V7X_VMEM_BYTES * 3) // 4
NEG_BIG = -1e30
MESH = pl.DeviceIdType.MESH
ANY = pl.BlockSpec(memory_space=pl.ANY)


def _params(*sem):
    return pltpu.CompilerParams(dimension_semantics=sem, vmem_limit_bytes=VMEM_LIMIT)


def _dot(a, b, dims):
    return lax.dot_general(a, b, (dims, ((), ())), preferred_element_type=F32)


def _dot_nn(a, b):
    return _dot(a, b, ((1,), (0,)))


def _dot_nt(a, b):
    return _dot(a, b, ((1,), (1,)))


def _dot_tn(a, b):
    return _dot(a, b, ((0,), (0,)))


def _split2(x):
    hi = x.astype(BF16)
    lo = (x - hi.astype(F32)).astype(BF16)
    return hi, lo


def _split3(x):
    hi = x.astype(BF16)
    r = x - hi.astype(F32)
    mid = r.astype(BF16)
    lo = (r - mid.astype(F32)).astype(BF16)
    return hi, mid, lo


def _mask_dot2(x, t01):
    hi, lo = _split2(x)
    return _dot_nn(hi, t01) + _dot_nn(lo, t01)


def _mask_dot3(x, t01):
    hi, mid, lo = _split3(x)
    return _dot_nn(hi, t01) + _dot_nn(mid, t01) + _dot_nn(lo, t01)


def _mask_dot3_left(t01, x):
    hi, mid, lo = _split3(x)
    return _dot_nn(t01, hi) + _dot_nn(t01, mid) + _dot_nn(t01, lo)


def _log_sigmoid(z):
    return jnp.minimum(z, 0.0) - jnp.log(1.0 + jnp.exp(-jnp.abs(z)))


def _sigmoid(z):
    return 1.0 / (1.0 + jnp.exp(-z))


def _mm(a, b, *, mode, M, N, K, out_dtype, name, tm=1024, tn=1024, tk=1024,
        a_off=(0, 0), b_off=(0, 0), acc_in=None, acc_scale=1.0, into=None, out_off=(0, 0)):
    tm, tn, tk = min(tm, M), min(tn, N), min(tk, K)
    assert M % tm == 0 and N % tn == 0 and K % tk == 0
    nk = K // tk
    ar, ac = a_off
    br, bc = b_off
    orow, ocol = out_off
    if mode in ("nn", "nt"):
        assert ar % tm == 0 and ac % tk == 0
        a_spec = pl.BlockSpec((tm, tk), lambda i, j, k: (i + ar // tm, k + ac // tk))
    else:
        assert ar % tk == 0 and ac % tm == 0
        a_spec = pl.BlockSpec((tk, tm), lambda i, j, k: (k + ar // tk, i + ac // tm))
    if mode == "nt":
        assert br % tn == 0 and bc % tk == 0
        b_spec = pl.BlockSpec((tn, tk), lambda i, j, k: (j + br // tn, k + bc // tk))
    else:
        assert br % tk == 0 and bc % tn == 0
        b_spec = pl.BlockSpec((tk, tn), lambda i, j, k: (k + br // tk, j + bc // tn))
    assert orow % tm == 0 and ocol % tn == 0
    o_spec = pl.BlockSpec((tm, tn), lambda i, j, k: (i + orow // tm, j + ocol // tn))
    dims = {"nn": ((1,), (0,)), "nt": ((1,), (1,)), "tn": ((0,), (0,))}[mode]
    inputs, in_specs = [a, b], [a_spec, b_spec]
    has_acc = acc_in is not None
    if has_acc:
        inputs.append(acc_in)
        in_specs.append(pl.BlockSpec((tm, tn), lambda i, j, k: (i, j)))
    aliases = {}
    if into is not None:
        inputs.append(into)
        in_specs.append(ANY)
        aliases = {len(inputs) - 1: 0}
        out_shape = jax.ShapeDtypeStruct(into.shape, into.dtype)
        assert into.dtype == out_dtype
    else:
        out_shape = jax.ShapeDtypeStruct((M, N), out_dtype)

    def body(*refs):
        a_ref, b_ref = refs[0], refs[1]
        acc_in_ref = refs[2] if has_acc else None
        n_in = len(inputs)
        o_ref = refs[n_in]

        def first():
            if has_acc:
                return acc_scale * acc_in_ref[...]
            return None

        if nk == 1:
            r = _dot(a_ref[...], b_ref[...], dims)
            f = first()
            if f is not None:
                r = r + f
            o_ref[...] = r.astype(o_ref.dtype)
        else:
            acc_ref = refs[n_in + 1]
            kk = pl.program_id(2)

            @pl.when(kk == 0)
            def _():
                f = first()
                acc_ref[...] = jnp.zeros_like(acc_ref) if f is None else f

            acc_ref[...] += _dot(a_ref[...], b_ref[...], dims)

            @pl.when(kk == nk - 1)
            def _():
                o_ref[...] = acc_ref[...].astype(o_ref.dtype)

    scratch = [] if nk == 1 else [pltpu.VMEM((tm, tn), F32)]
    return pl.pallas_call(
        body, name=name, out_shape=out_shape, grid=(M // tm, N // tn, nk),
        in_specs=in_specs, out_specs=o_spec, scratch_shapes=scratch,
        input_output_aliases=aliases,
        compiler_params=_params("parallel", "parallel", "arbitrary"),
    )(*inputs)


def _shift_down(x, d, row):
    if d == 0:
        return x
    return jnp.where(row >= d, pltpu.roll(x, d, 0), 0.0)


def _shift_up(x, d, row, n):
    if d == 0:
        return x
    return jnp.where(row < n - d, pltpu.roll(x, n - d, 0), 0.0)


def _conv_pre(x, w_ref, b_ref, row):
    c = b_ref[...] + w_ref[CONV_A - 1:CONV_A, :] * x
    for k in range(CONV_A - 1):
        c = c + w_ref[k:k + 1, :] * _shift_down(x, CONV_A - 1 - k, row)
    return c


def _conv_fwd(u, conv_w, conv_b, *, S, D, name):
    tc = 256
    nq_blocks = (D // 2) // tc

    def body(u_ref, w_ref, b_ref, o_ref):
        x = u_ref[...]
        row = lax.broadcasted_iota(jnp.int32, x.shape, 0)
        c = _conv_pre(x, w_ref, b_ref, row)
        scale = jnp.where(pl.program_id(0) >= nq_blocks, DK_A ** -0.5, 1.0).astype(F32)
        o_ref[...] = (c * _sigmoid(c) * scale).astype(BF16)

    return pl.pallas_call(
        body, name=name, out_shape=jax.ShapeDtypeStruct((S, D), BF16), grid=(D // tc,),
        in_specs=[pl.BlockSpec((S, tc), lambda j: (0, j)),
                  pl.BlockSpec((CONV_A, tc), lambda j: (0, j)),
                  pl.BlockSpec((1, tc), lambda j: (0, j))],
        out_specs=pl.BlockSpec((S, tc), lambda j: (0, j)),
        compiler_params=_params("parallel"),
    )(u, conv_w, conv_b)


def _conv_bwd(u, dq, dk, conv_w, conv_b, du, *, S, D, name):
    tc = 256
    nq_blocks = (D // 2) // tc

    def body(u_ref, dq_ref, dk_ref, w_ref, b_ref, du_in, du_ref, dw_ref, db_ref):
        del du_in
        x = u_ref[...]
        n = x.shape[0]
        row = lax.broadcasted_iota(jnp.int32, x.shape, 0)
        c = _conv_pre(x, w_ref, b_ref, row)
        is_k = pl.program_id(0) >= nq_blocks
        dy = jnp.where(is_k, dk_ref[...] * (DK_A ** -0.5), dq_ref[...])
        sg = _sigmoid(c)
        dc = dy * (sg * (1.0 + c * (1.0 - sg)))
        db_ref[...] = jnp.sum(dc, axis=0, keepdims=True)
        dx = w_ref[CONV_A - 1:CONV_A, :] * dc
        dw_ref[CONV_A - 1:CONV_A, :] = jnp.sum(dc * x, axis=0, keepdims=True)
        for k in range(CONV_A - 1):
            d = CONV_A - 1 - k
            dw_ref[k:k + 1, :] = jnp.sum(dc * _shift_down(x, d, row), axis=0, keepdims=True)
            dx = dx + w_ref[k:k + 1, :] * _shift_up(dc, d, row, n)
        du_ref[...] = dx.astype(BF16)

    half = lambda j: (0, j % nq_blocks)
    return pl.pallas_call(
        body, name=name,
        out_shape=(jax.ShapeDtypeStruct(du.shape, du.dtype),
                   jax.ShapeDtypeStruct((CONV_A, D), F32), jax.ShapeDtypeStruct((1, D), F32)),
        grid=(D // tc,),
        in_specs=[pl.BlockSpec((S, tc), lambda j: (0, j)),
                  pl.BlockSpec((S, tc), half), pl.BlockSpec((S, tc), half),
                  pl.BlockSpec((CONV_A, tc), lambda j: (0, j)),
                  pl.BlockSpec((1, tc), lambda j: (0, j)), ANY],
        out_specs=(pl.BlockSpec((S, tc), lambda j: (0, j)),
                   pl.BlockSpec((CONV_A, tc), lambda j: (0, j)),
                   pl.BlockSpec((1, tc), lambda j: (0, j))),
        input_output_aliases={5: 0},
        compiler_params=_params("parallel"),
    )(u, dq, dk, conv_w, conv_b, du)


def _tri(n, cmp):
    r = lax.broadcasted_iota(jnp.int32, (n, n), 0)
    c = lax.broadcasted_iota(jnp.int32, (n, n), 1)
    return r, c, cmp(r, c)


def _gates_fwd(gt, bias, *, nha, nb, name):
    half = nha * nb
    nb_shift = nb.bit_length() - 1
    assert nb == 1 << nb_shift

    def body(g_ref, b_ref, f_ref, brow_ref):
        ig = g_ref[0:half, :] + b_ref[0:half, :]
        fg = g_ref[half:2 * half, :] + b_ref[half:2 * half, :]
        lf = _log_sigmoid(fg)
        _, _, upper = _tri(LANES, lambda r, c: r <= c)
        cs = _mask_dot3(lf, upper.astype(BF16))
        tot = jnp.broadcast_to(cs[:, LANES - 1:LANES], cs.shape)
        r, c, _ = _tri(half, lambda r, c: r <= c)
        before = jnp.logical_and(r >> nb_shift == c >> nb_shift, c < r).astype(BF16)
        f = cs + _mask_dot3_left(before, tot)
        f_ref[...] = f
        brow_ref[...] = ig - f

    return pl.pallas_call(
        body, name=name,
        out_shape=(jax.ShapeDtypeStruct((half, LANES), F32), jax.ShapeDtypeStruct((half, LANES), F32)),
    )(gt, bias)


def _gates_bwd(rowsum, colsum, gt, bias, *, nha, nb, name):
    half = nha * nb
    nb_shift = nb.bit_length() - 1
    assert nb == 1 << nb_shift

    def body(rs_ref, cs_ref, g_ref, b_ref, dg_ref, tot_ref):
        col = cs_ref[...]
        df = rs_ref[...] - col
        _, _, lower = _tri(LANES, lambda r, c: r >= c)
        rc = _mask_dot3(df, lower.astype(BF16))
        tot = jnp.broadcast_to(rc[:, 0:1], rc.shape)
        r, c, _ = _tri(half, lambda r, c: r <= c)
        same = r >> nb_shift == c >> nb_shift
        after = jnp.logical_and(same, c > r).astype(BF16)
        dlf = rc + _mask_dot3_left(after, tot)
        fg = g_ref[half:2 * half, :] + b_ref[half:2 * half, :]
        dfg = dlf * _sigmoid(-fg)
        dg_ref[0:half, :] = col
        dg_ref[half:2 * half, :] = dfg
        grp = same.astype(BF16)
        ones = jnp.ones((LANES, LANES), BF16)
        tot_ref[0:half, :] = _mask_dot3_left(grp, _mask_dot3(col, ones))
        tot_ref[half:2 * half, :] = _mask_dot3_left(grp, _mask_dot3(dfg, ones))

    return pl.pallas_call(
        body, name=name,
        out_shape=(jax.ShapeDtypeStruct((2 * half, LANES), F32), jax.ShapeDtypeStruct((2 * half, LANES), F32)),
    )(rowsum, colsum, gt, bias)


def _mlstm_tile(q, k_ref, fcol, brow_ref, j, i, tq, m=None):
    off = pl.multiple_of(j * tq, tq)
    kj = k_ref[pl.ds(off, tq), :]
    s = _dot_nt(q, kj)
    row = lax.broadcasted_iota(jnp.int32, (tq, tq), 0)
    col = lax.broadcasted_iota(jnp.int32, (tq, tq), 1)
    valid = jnp.logical_or(col <= row, j < i)
    logd = jnp.where(valid, fcol + brow_ref[0, :, pl.ds(off, tq)], NEG_BIG)
    return off, kj, s, logd


def _mlstm_fwd(qk, u, fcol, brow, *, S, D, nha, name):
    tq = min(256, S)
    nq = S // tq
    kb, vb = (D // 2) // DK_A, D // DV_A

    def body(q_ref, k_ref, v_ref, fcol_ref, brow_ref, h_ref, m_ref, den_ref):
        i = pl.program_id(1)
        q = q_ref[...]
        fc = fcol_ref[0]

        def step(j, carry):
            acc, den, m = carry
            off, _, s, logd = _mlstm_tile(q, k_ref, fc, brow_ref, j, i, tq)
            m_new = jnp.maximum(m, jnp.max(logd, axis=1, keepdims=True))
            a = s * jnp.exp(logd - m_new)
            alpha = jnp.exp(m - m_new)
            vj = v_ref[pl.ds(off, tq), :].astype(BF16)
            acc = alpha * acc + _dot_nn(a.astype(BF16), vj)
            den = alpha * den + jnp.sum(a, axis=1, keepdims=True)
            return acc, den, m_new

        acc, den, m = lax.fori_loop(
            0, i + 1, step,
            (jnp.zeros((tq, DV_A), F32), jnp.zeros((tq, 1), F32), jnp.full((tq, 1), NEG_BIG, F32)))
        h_ref[...] = acc / jnp.maximum(jnp.abs(den), jnp.exp(-m))
        m_ref[0] = m
        den_ref[0] = den

    stat = pl.BlockSpec((1, tq, 1), lambda h, i: (h, i, 0))
    return pl.pallas_call(
        body, name=name,
        out_shape=(jax.ShapeDtypeStruct((S, D), F32), jax.ShapeDtypeStruct((nha, S, 1), F32),
                   jax.ShapeDtypeStruct((nha, S, 1), F32)),
        grid=(nha, nq),
        in_specs=[pl.BlockSpec((tq, DK_A), lambda h, i: (i, h)),
                  pl.BlockSpec((S, DK_A), lambda h, i: (0, kb + h)),
                  pl.BlockSpec((S, DV_A), lambda h, i: (0, vb + h)),
                  stat, pl.BlockSpec((1, 1, S), lambda h, i: (h, 0, 0))],
        out_specs=(pl.BlockSpec((tq, DV_A), lambda h, i: (i, h)), stat, stat),
        compiler_params=_params("parallel", "arbitrary"),
    )(qk, qk, u, fcol, brow)


def _mlstm_bwd(qk, u, fcol, brow, m, den, dh, h, du, *, S, D, nha, name):
    tq = min(256, S)
    nq = S // tq
    kb, vb = (D // 2) // DK_A, D // DV_A

    def body(q_ref, k_ref, v_ref, fcol_ref, brow_ref, m_ref, den_ref, dh_ref, h_ref, du_in,
             du_ref, dq_ref, dk_ref, rs_ref, cs_ref, dv_acc):
        del du_in
        i = pl.program_id(1)

        @pl.when(i == 0)
        def _():
            dk_ref[...] = jnp.zeros_like(dk_ref)
            cs_ref[...] = jnp.zeros_like(cs_ref)
            dv_acc[...] = jnp.zeros_like(dv_acc)

        q = q_ref[...]
        fc = fcol_ref[0]
        mm = m_ref[0]
        dn = den_ref[0]
        floor = jnp.exp(-mm)
        nrm = jnp.maximum(jnp.abs(dn), floor)
        dhv = dh_ref[...]
        dnum = dhv / nrm
        dnrm = -jnp.sum(dhv * h_ref[...], axis=1, keepdims=True) / nrm
        dden = jnp.where(jnp.abs(dn) > floor, jnp.where(dn > 0.0, dnrm, -dnrm), 0.0)
        dnum_b = dnum.astype(BF16)

        def step(j, carry):
            dq, rs = carry
            off, kj, s, logd = _mlstm_tile(q, k_ref, fc, brow_ref, j, i, tq)
            p = jnp.exp(logd - mm)
            a = s * p
            vj = v_ref[pl.ds(off, tq), :].astype(BF16)
            da = _dot_nt(dnum_b, vj) + dden
            dv_acc[pl.ds(off, tq), :] += _dot_tn(a.astype(BF16), dnum_b)
            dqk = (da * p).astype(BF16)
            dq = dq + _dot_nn(dqk, kj)
            dk_ref[pl.ds(off, tq), :] += _dot_tn(dqk, q)
            pm = da * a
            cs_ref[0, :, pl.ds(off, tq)] += jnp.sum(pm, axis=0, keepdims=True)
            rs = rs + jnp.sum(pm, axis=1, keepdims=True)
            return dq, rs

        dq, rs = lax.fori_loop(0, i + 1, step, (jnp.zeros((tq, DK_A), F32), jnp.zeros((tq, 1), F32)))
        dq_ref[...] = dq
        rs_ref[0] = rs

        @pl.when(i == nq - 1)
        def _():
            du_ref[...] = dv_acc[...].astype(BF16)

    stat = pl.BlockSpec((1, tq, 1), lambda h, i: (h, i, 0))
    rowv = pl.BlockSpec((1, 1, S), lambda h, i: (h, 0, 0))
    hblk = pl.BlockSpec((tq, DV_A), lambda h, i: (i, h))
    return pl.pallas_call(
        body, name=name,
        out_shape=(jax.ShapeDtypeStruct(du.shape, du.dtype),
                   jax.ShapeDtypeStruct((S, D // 2), F32), jax.ShapeDtypeStruct((S, D // 2), F32),
                   jax.ShapeDtypeStruct((nha, S, 1), F32), jax.ShapeDtypeStruct((nha, 1, S), F32)),
        grid=(nha, nq),
        in_specs=[pl.BlockSpec((tq, DK_A), lambda h, i: (i, h)),
                  pl.BlockSpec((S, DK_A), lambda h, i: (0, kb + h)),
                  pl.BlockSpec((S, DV_A), lambda h, i: (0, vb + h)),
                  stat, rowv, stat, stat, hblk, hblk, ANY],
        out_specs=(pl.BlockSpec((S, DV_A), lambda h, i: (0, vb + h)),
                   pl.BlockSpec((tq, DK_A), lambda h, i: (i, h)),
                   pl.BlockSpec((S, DK_A), lambda h, i: (0, h)),
                   stat, rowv),
        scratch_shapes=[pltpu.VMEM((S, DV_A), F32)],
        input_output_aliases={9: 0},
        compiler_params=_params("parallel", "arbitrary"),
    )(qk, qk, u, fcol, brow, m, den, dh, h, du)


def _head_norm(hh):
    mu = jnp.mean(hh, axis=1, keepdims=True)
    hc = hh - mu
    rstd = lax.rsqrt(jnp.mean(hc * hc, axis=1, keepdims=True) + LN_EPS)
    return hc * rstd, rstd


def _hgate_fwd(h, u, head_g, *, S, D, name):
    tm = min(256, S)
    nh = D // DV_A

    def body(h_ref, o_ref, z_ref, g_ref, out_ref):
        for hd in range(nh):
            sl = slice(hd * DV_A, (hd + 1) * DV_A)
            hn, _ = _head_norm(h_ref[:, sl])
            z = z_ref[:, sl]
            out_ref[:, sl] = (_sigmoid(o_ref[:, sl]) * (hn * g_ref[:, sl]) * (z * _sigmoid(z))).astype(BF16)

    return pl.pallas_call(
        body, name=name, out_shape=jax.ShapeDtypeStruct((S, D), BF16), grid=(S // tm,),
        in_specs=[pl.BlockSpec((tm, D), lambda i: (i, 0)), pl.BlockSpec((tm, D), lambda i: (i, 2)),
                  pl.BlockSpec((tm, D), lambda i: (i, 3)), pl.BlockSpec((1, D), lambda i: (0, 0))],
        out_specs=pl.BlockSpec((tm, D), lambda i: (i, 0)),
        compiler_params=_params("parallel"),
    )(h, u, u, head_g)


def _hgate_bwd(dhg, h, u, head_g, du, *, S, D, name):
    tm = min(256, S)
    nh = D // DV_A

    def body(dhg_ref, h_ref, o_ref, z_ref, g_ref, du_in, du_ref, dh_ref, dg_ref):
        del du_in

        @pl.when(pl.program_id(0) == 0)
        def _():
            dg_ref[...] = jnp.zeros_like(dg_ref)

        for hd in range(nh):
            sl = slice(hd * DV_A, (hd + 1) * DV_A)
            hn, rstd = _head_norm(h_ref[:, sl])
            o, z, g, d = o_ref[:, sl], z_ref[:, sl], g_ref[:, sl], dhg_ref[:, sl]
            so, sz = _sigmoid(o), _sigmoid(z)
            silu_z = z * sz
            hng = hn * g
            du_ref[:, sl] = (d * hng * silu_z * so * (1.0 - so)).astype(BF16)
            du_ref[:, D + hd * DV_A:D + (hd + 1) * DV_A] = (
                d * so * hng * (sz * (1.0 + z * (1.0 - sz)))).astype(BF16)
            t = d * so * silu_z
            dg_ref[:, sl] += jnp.sum(t * hn, axis=0, keepdims=True)
            dhn = t * g
            dh_ref[:, sl] = rstd * (dhn - jnp.mean(dhn, axis=1, keepdims=True)
                                    - hn * jnp.mean(dhn * hn, axis=1, keepdims=True))

    row = lambda i: (i, 0)
    return pl.pallas_call(
        body, name=name,
        out_shape=(jax.ShapeDtypeStruct(du.shape, du.dtype), jax.ShapeDtypeStruct((S, D), F32),
                   jax.ShapeDtypeStruct((1, D), F32)),
        grid=(S // tm,),
        in_specs=[pl.BlockSpec((tm, D), row), pl.BlockSpec((tm, D), row),
                  pl.BlockSpec((tm, D), lambda i: (i, 2)), pl.BlockSpec((tm, D), lambda i: (i, 3)),
                  pl.BlockSpec((1, D), lambda i: (0, 0)), ANY],
        out_specs=(pl.BlockSpec((tm, 2 * D), lambda i: (i, 1)), pl.BlockSpec((tm, D), row),
                   pl.BlockSpec((1, D), lambda i: (0, 0))),
        input_output_aliases={5: 0},
        compiler_params=_params("arbitrary"),
    )(dhg, h, u, u, head_g, du)


def _ln_stats(r):
    mu = jnp.mean(r, axis=1, keepdims=True)
    xc = r - mu
    rstd = lax.rsqrt(jnp.mean(xc * xc, axis=1, keepdims=True) + LN_EPS)
    return xc * rstd, rstd


def _ln_back(dxhat, xhat, rstd):
    return rstd * (dxhat - jnp.mean(dxhat, axis=1, keepdims=True)
                   - xhat * jnp.mean(dxhat * xhat, axis=1, keepdims=True))


def _ln_fwd(x, y, g, b, *, S, D, name):
    tm = min(256, S)

    def body(x_ref, y_ref, g_ref, b_ref, o_ref, ob_ref):
        xhat, _ = _ln_stats(ALPHA * x_ref[...] + y_ref[...])
        o = xhat * g_ref[...] + b_ref[...]
        o_ref[...] = o
        ob_ref[...] = o.astype(BF16)

    row = lambda i: (i, 0)
    vec = pl.BlockSpec((1, D), lambda i: (0, 0))
    return pl.pallas_call(
        body, name=name,
        out_shape=(jax.ShapeDtypeStruct((S, D), F32), jax.ShapeDtypeStruct((S, D), BF16)),
        grid=(S // tm,),
        in_specs=[pl.BlockSpec((tm, D), row), pl.BlockSpec((tm, D), row), vec, vec],
        out_specs=(pl.BlockSpec((tm, D), row), pl.BlockSpec((tm, D), row)),
        compiler_params=_params("parallel"),
    )(x, y, g, b)


def _ln_loss_bwd(x1, y2, target, g, b, *, S, D, name):
    tm = min(256, S)

    def body(x_ref, y_ref, t_ref, g_ref, b_ref, dr_ref, drb_ref, dg_ref, db_ref, loss_ref):
        @pl.when(pl.program_id(0) == 0)
        def _():
            dg_ref[...] = jnp.zeros_like(dg_ref)
            db_ref[...] = jnp.zeros_like(db_ref)
            loss_ref[...] = jnp.zeros_like(loss_ref)

        xhat, rstd = _ln_stats(ALPHA * x_ref[...] + y_ref[...])
        diff = xhat * g_ref[...] + b_ref[...] - t_ref[...]
        loss_ref[...] += (0.5 / D) * jnp.sum(diff * diff)
        dx2 = diff * (1.0 / D)
        dg_ref[...] += jnp.sum(dx2 * xhat, axis=0, keepdims=True)
        db_ref[...] += jnp.sum(dx2, axis=0, keepdims=True)
        dr = _ln_back(dx2 * g_ref[...], xhat, rstd)
        dr_ref[...] = dr
        drb_ref[...] = dr.astype(BF16)

    row = lambda i: (i, 0)
    vec = pl.BlockSpec((1, D), lambda i: (0, 0))
    return pl.pallas_call(
        body, name=name,
        out_shape=(jax.ShapeDtypeStruct((S, D), F32), jax.ShapeDtypeStruct((S, D), BF16),
                   jax.ShapeDtypeStruct((1, D), F32), jax.ShapeDtypeStruct((1, D), F32),
                   jax.ShapeDtypeStruct((1, LANES), F32)),
        grid=(S // tm,),
        in_specs=[pl.BlockSpec((tm, D), row)] * 3 + [vec, vec],
        out_specs=(pl.BlockSpec((tm, D), row), pl.BlockSpec((tm, D), row), vec, vec,
                   pl.BlockSpec((1, LANES), lambda i: (0, 0))),
        compiler_params=_params("arbitrary"),
    )(x1, y2, target, g, b)


def _ln_bwd(x, y, g, dout, *, S, D, name):
    tm = min(256, S)

    def body(x_ref, y_ref, g_ref, d_ref, dr_ref, drb_ref, dg_ref, db_ref):
        @pl.when(pl.program_id(0) == 0)
        def _():
            dg_ref[...] = jnp.zeros_like(dg_ref)
            db_ref[...] = jnp.zeros_like(db_ref)

        xhat, rstd = _ln_stats(ALPHA * x_ref[...] + y_ref[...])
        d = d_ref[...]
        dg_ref[...] += jnp.sum(d * xhat, axis=0, keepdims=True)
        db_ref[...] += jnp.sum(d, axis=0, keepdims=True)
        dr = _ln_back(d * g_ref[...], xhat, rstd)
        dr_ref[...] = dr
        drb_ref[...] = dr.astype(BF16)

    row = lambda i: (i, 0)
    vec = pl.BlockSpec((1, D), lambda i: (0, 0))
    return pl.pallas_call(
        body, name=name,
        out_shape=(jax.ShapeDtypeStruct((S, D), F32), jax.ShapeDtypeStruct((S, D), BF16),
                   jax.ShapeDtypeStruct((1, D), F32), jax.ShapeDtypeStruct((1, D), F32)),
        grid=(S // tm,),
        in_specs=[pl.BlockSpec((tm, D), row), pl.BlockSpec((tm, D), row), vec, pl.BlockSpec((tm, D), row)],
        out_specs=(pl.BlockSpec((tm, D), row), pl.BlockSpec((tm, D), row), vec, vec),
        compiler_params=_params("arbitrary"),
    )(x, y, g, dout)


def _sb_scores(q, kj, j, i, tq):
    z = _dot_nt(q, kj) * (DH_B ** -0.5)
    row = lax.broadcasted_iota(jnp.int32, (tq, tq), 0)
    col = lax.broadcasted_iota(jnp.int32, (tq, tq), 1)
    valid = jnp.logical_or(col < row, j < i)
    ls = _log_sigmoid(z)
    lneg = jnp.where(valid, ls - z, 0.0)
    return valid, ls, lneg


def _sb_fwd(q2, kv, *, S, D, name):
    tq = min(128, S)
    nq = S // tq
    nh = D // DH_B

    def body(q_ref, k_ref, v_ref, o_ref, tot_ref):
        i = pl.program_id(1)
        q = q_ref[...]
        _, _, after = _tri(tq, lambda r, c: r > c)
        tri_after = after.astype(BF16)

        def step(jj, carry):
            acc, cr = carry
            j = i - jj
            off = pl.multiple_of(j * tq, tq)
            valid, ls, lneg = _sb_scores(q, k_ref[pl.ds(off, tq), :], j, i, tq)
            between = cr + _mask_dot2(lneg, tri_after)
            a = jnp.where(valid, jnp.exp(ls + between), 0.0)
            acc = acc + _dot_nn(a.astype(BF16), v_ref[pl.ds(off, tq), :])
            return acc, cr + jnp.sum(lneg, axis=1, keepdims=True)

        acc, cr = lax.fori_loop(0, i + 1, step, (jnp.zeros((tq, DH_B), F32), jnp.zeros((tq, 1), F32)))
        o_ref[...] = acc
        tot_ref[0] = cr

    return pl.pallas_call(
        body, name=name,
        out_shape=(jax.ShapeDtypeStruct((S, D), F32), jax.ShapeDtypeStruct((nh, S, 1), F32)), grid=(nh, nq),
        in_specs=[pl.BlockSpec((tq, DH_B), lambda h, i: (i, h)),
                  pl.BlockSpec((S, DH_B), lambda h, i: (0, h)),
                  pl.BlockSpec((S, DH_B), lambda h, i: (0, nh + h))],
        out_specs=(pl.BlockSpec((tq, DH_B), lambda h, i: (i, h)), pl.BlockSpec((1, tq, 1), lambda h, i: (h, i, 0))),
        compiler_params=_params("parallel", "arbitrary"),
    )(q2, kv, kv)


def _sb_bwd(q2, kv, datt, tot, *, S, D, name):
    tq = min(128, S)
    nq = S // tq
    nh = D // DH_B

    def body(q_ref, k_ref, v_ref, do_ref, tot_ref, dq_ref, dk_ref, dv_ref, dk_acc, dv_acc):
        i = pl.program_id(1)

        @pl.when(i == 0)
        def _():
            dk_acc[...] = jnp.zeros_like(dk_acc)
            dv_acc[...] = jnp.zeros_like(dv_acc)

        q = q_ref[...]
        do_b = do_ref[...].astype(BF16)
        total = tot_ref[0]
        _, _, upto = _tri(tq, lambda r, c: r <= c)
        tri_upto = upto.astype(BF16)
        _, _, before = _tri(tq, lambda r, c: r < c)
        tri_before = before.astype(BF16)

        def step(j, carry):
            dq, cl, cg = carry
            off = pl.multiple_of(j * tq, tq)
            kj = k_ref[pl.ds(off, tq), :]
            valid, ls, lneg = _sb_scores(q, kj, j, i, tq)
            between = total - (cl + _mask_dot3(lneg, tri_upto))
            a = jnp.where(valid, jnp.exp(ls + between), 0.0)
            g = _dot_nt(do_b, v_ref[pl.ds(off, tq), :]) * a
            dv_acc[pl.ds(off, tq), :] += _dot_tn(a.astype(BF16), do_b)
            e = cg + _mask_dot2(g, tri_before)
            dz = jnp.where(valid, g * jnp.exp(lneg) - e * jnp.exp(ls), 0.0) * (DH_B ** -0.5)
            dz_b = dz.astype(BF16)
            dq = dq + _dot_nn(dz_b, kj)
            dk_acc[pl.ds(off, tq), :] += _dot_tn(dz_b, q)
            return (dq, cl + jnp.sum(lneg, axis=1, keepdims=True), cg + jnp.sum(g, axis=1, keepdims=True))

        zero = jnp.zeros((tq, 1), F32)
        dq, _, _ = lax.fori_loop(0, i + 1, step, (jnp.zeros((tq, DH_B), F32), zero, zero))
        dq_ref[...] = dq.astype(BF16)

        @pl.when(i == nq - 1)
        def _():
            dk_ref[...] = dk_acc[...].astype(BF16)
            dv_ref[...] = dv_acc[...].astype(BF16)

    blk = pl.BlockSpec((tq, DH_B), lambda h, i: (i, h))
    return pl.pallas_call(
        body, name=name,
        out_shape=(jax.ShapeDtypeStruct((S, D), BF16), jax.ShapeDtypeStruct((S, D), BF16),
                   jax.ShapeDtypeStruct((S, D), BF16)),
        grid=(nh, nq),
        in_specs=[blk, pl.BlockSpec((S, DH_B), lambda h, i: (0, h)),
                  pl.BlockSpec((S, DH_B), lambda h, i: (0, nh + h)), blk,
                  pl.BlockSpec((1, tq, 1), lambda h, i: (h, i, 0))],
        out_specs=(blk, pl.BlockSpec((S, DH_B), lambda h, i: (0, h)),
                   pl.BlockSpec((S, DH_B), lambda h, i: (0, h))),
        scratch_shapes=[pltpu.VMEM((S, DH_B), F32), pltpu.VMEM((S, DH_B), F32)],
        compiler_params=_params("parallel", "arbitrary"),
    )(q2, kv, kv, datt, tot)


def _bgate_fwd(att, z2, *, S, D, name):
    tm = min(256, S)

    def body(a_ref, z_ref, o_ref):
        z = z_ref[...]
        o_ref[...] = (a_ref[...] * (z * _sigmoid(z))).astype(BF16)

    row = lambda i: (i, 0)
    return pl.pallas_call(
        body, name=name, out_shape=jax.ShapeDtypeStruct((S, D), BF16), grid=(S // tm,),
        in_specs=[pl.BlockSpec((tm, D), row)] * 2, out_specs=pl.BlockSpec((tm, D), row),
        compiler_params=_params("parallel"),
    )(att, z2)


def _bgate_bwd(dhb, att, z2, *, S, D, name):
    tm = min(256, S)

    def body(d_ref, a_ref, z_ref, da_ref, dz_ref):
        z, d = z_ref[...], d_ref[...]
        sz = _sigmoid(z)
        da_ref[...] = d * (z * sz)
        dz_ref[...] = (d * a_ref[...] * (sz * (1.0 + z * (1.0 - sz)))).astype(BF16)

    row = lambda i: (i, 0)
    return pl.pallas_call(
        body, name=name,
        out_shape=(jax.ShapeDtypeStruct((S, D), F32), jax.ShapeDtypeStruct((S, D), BF16)),
        grid=(S // tm,),
        in_specs=[pl.BlockSpec((tm, D), row)] * 3,
        out_specs=(pl.BlockSpec((tm, D), row), pl.BlockSpec((tm, D), row)),
        compiler_params=_params("parallel"),
    )(dhb, att, z2)


def _adamw_math(w, g, m, v):
    mn = ADAM_B1 * m + (1.0 - ADAM_B1) * g
    vn = ADAM_B2 * v + (1.0 - ADAM_B2) * (g * g)
    m_hat = mn / (1.0 - ADAM_B1 ** ADAM_STEP)
    v_hat = vn / (1.0 - ADAM_B2 ** ADAM_STEP)
    return -ADAM_LR * (m_hat / (jnp.sqrt(v_hat) + ADAM_EPS) + ADAM_WD * w), mn, vn


def _adamw(w, g, m, v, *, name, tr=128):
    rows, cols = w.shape
    tr = min(tr, rows)
    assert rows % tr == 0

    def body(w_ref, g_ref, m_ref, v_ref, d_ref, mo_ref, vo_ref):
        d, mn, vn = _adamw_math(w_ref[...], g_ref[...], m_ref[...], v_ref[...])
        d_ref[...] = d
        mo_ref[...] = mn
        vo_ref[...] = vn

    blk = pl.BlockSpec((tr, cols), lambda i: (i, 0))
    sd = jax.ShapeDtypeStruct((rows, cols), F32)
    return pl.pallas_call(
        body, name=name, out_shape=(sd, sd, sd), grid=(rows // tr,),
        in_specs=[blk] * 4, out_specs=(blk, blk, blk),
        compiler_params=_params("parallel"),
    )(w, g, m, v)


def _local_step(x, target, wa, wg, wao, wbi, wkv, wbo, gate_b, conv_w, conv_b, head_g,
                a_ln_g, a_ln_b, b_ln_g, b_ln_b, *, S, D, nha):
    nb = S // LANES
    kw = dict(S=S, D=D)
    xb = x.astype(BF16)
    u = _mm(xb, wa, mode="nn", M=S, N=4 * D, K=D, out_dtype=F32, name="a_in")
    ug = _mm(xb, wg, mode="nn", M=S, N=LANES, K=D, out_dtype=F32, name="a_in_gates")
    qk = _conv_fwd(u, conv_w, conv_b, name="conv_fwd", **kw)
    gt = ug[:, :2 * nha].T.reshape(2 * nha * nb, LANES)
    gbias = jnp.repeat(gate_b.reshape(2 * nha), nb).reshape(2 * nha * nb, 1)
    fcs, brow = _gates_fwd(gt, gbias, nha=nha, nb=nb, name="gates_fwd")
    fcol = fcs.reshape(nha, S, 1)
    brow = brow.reshape(nha, 1, S)
    h, m, den = _mlstm_fwd(qk, u, fcol, brow, nha=nha, name="mlstm_fwd", **kw)
    hg = _hgate_fwd(h, u, head_g, name="hgate_fwd", **kw)
    y = _mm(hg, wao, mode="nn", M=S, N=D, K=D, out_dtype=F32, name="a_out")
    x1, x1b = _ln_fwd(x, y, a_ln_g, a_ln_b, name="ln_a_fwd", **kw)
    q2 = _mm(x1b, wbi, mode="nn", M=S, N=D, K=D, out_dtype=BF16, name="b_in_q")
    z2 = _mm(x1b, wbi, mode="nn", M=S, N=D, K=D, out_dtype=F32, name="b_in_z", b_off=(0, D))
    kv = _mm(x1b, wkv, mode="nn", M=S, N=2 * D, K=D, out_dtype=BF16, name="b_kv")
    att, sb_tot = _sb_fwd(q2, kv, name="sb_fwd", **kw)
    hb = _bgate_fwd(att, z2, name="bgate_fwd", **kw)
    y2 = _mm(hb, wbo, mode="nn", M=S, N=D, K=D, out_dtype=F32, name="b_out")
    dr2, dr2b, d_bln_g, d_bln_b, loss = _ln_loss_bwd(x1, y2, target, b_ln_g, b_ln_b, name="ln_b_loss", **kw)
    g_wbo = _mm(hb, dr2b, mode="tn", M=D, N=D, K=S, out_dtype=BF16, name="g_b_out")
    dhb = _mm(dr2b, wbo, mode="nt", M=S, N=D, K=D, out_dtype=F32, name="d_b_out")
    datt, dz2 = _bgate_bwd(dhb, att, z2, name="bgate_bwd", **kw)
    dq2, dk2, dv2 = _sb_bwd(q2, kv, datt, sb_tot, name="sb_bwd", **kw)
    g_wbi = _mm(x1b, dq2, mode="tn", M=D, N=D, K=S, out_dtype=BF16, name="g_b_in_q",
                into=lax.empty((D, 2 * D), BF16))
    g_wbi = _mm(x1b, dz2, mode="tn", M=D, N=D, K=S, out_dtype=BF16, name="g_b_in_z", into=g_wbi, out_off=(0, D))
    g_wkv = _mm(x1b, dk2, mode="tn", M=D, N=D, K=S, out_dtype=BF16, name="g_kv_k",
                into=lax.empty((D, 2 * D), BF16))
    g_wkv = _mm(x1b, dv2, mode="tn", M=D, N=D, K=S, out_dtype=BF16, name="g_kv_v", into=g_wkv, out_off=(0, D))
    dx1 = _mm(dq2, wbi, mode="nt", M=S, N=D, K=D, out_dtype=F32, name="d_b_in_q", acc_in=dr2, acc_scale=ALPHA)
    dx1 = _mm(dz2, wbi, mode="nt", M=S, N=D, K=D, out_dtype=F32, name="d_b_in_z", b_off=(0, D), acc_in=dx1)
    dx1 = _mm(dk2, wkv, mode="nt", M=S, N=D, K=D, out_dtype=F32, name="d_kv_k", acc_in=dx1)
    dx1 = _mm(dv2, wkv, mode="nt", M=S, N=D, K=D, out_dtype=F32, name="d_kv_v", b_off=(0, D), acc_in=dx1)
    dr, drb, d_aln_g, d_aln_b = _ln_bwd(x, y, a_ln_g, dx1, name="ln_a_bwd", **kw)
    g_wao = _mm(hg, drb, mode="tn", M=D, N=D, K=S, out_dtype=BF16, name="g_a_out")
    dhg = _mm(drb, wao, mode="nt", M=S, N=D, K=D, out_dtype=F32, name="d_a_out")
    du = lax.empty((S, 4 * D), BF16)
    du, dh, d_head_g = _hgate_bwd(dhg, h, u, head_g, du, name="hgate_bwd", **kw)
    du, dq, dk, rowsum, colsum = _mlstm_bwd(qk, u, fcol, brow, m, den, dh, h, du, nha=nha, name="mlstm_bwd", **kw)
    dgt, dgtot = _gates_bwd(rowsum.reshape(nha * nb, LANES), colsum.reshape(nha * nb, LANES), gt, gbias,
                            nha=nha, nb=nb, name="gates_bwd")
    d_gate_b = dgtot[::nb, 0].reshape(1, 2 * nha)
    dgp = jnp.pad(dgt.reshape(2 * nha, S).T, ((0, 0), (0, LANES - 2 * nha))).astype(BF16)
    du, d_conv_w, d_conv_b = _conv_bwd(u, dq, dk, conv_w, conv_b, du, name="conv_bwd", **kw)
    g_wa = _mm(xb, du, mode="tn", M=D, N=4 * D, K=S, out_dtype=BF16, name="g_a_in")
    g_wg = _mm(xb, dgp, mode="tn", M=D, N=LANES, K=S, out_dtype=BF16, name="g_a_in_gates")
    dx = _mm(du, wa, mode="nt", M=S, N=D, K=4 * D, out_dtype=F32, name="d_a_in", acc_in=dr, acc_scale=ALPHA)
    dx = _mm(dgp, wg, mode="nt", M=S, N=D, K=LANES, out_dtype=F32, name="d_a_in_gates", acc_in=dx)
    big = dict(wa=g_wa, wg=g_wg, wao=g_wao, wbi=g_wbi, wkv=g_wkv, wbo=g_wbo)
    small = dict(gate_b=d_gate_b, conv_w=d_conv_w, conv_b=d_conv_b, head_g=d_head_g,
                 a_ln_g=d_aln_g, a_ln_b=d_aln_b, b_ln_g=d_bln_g, b_ln_b=d_bln_b)
    return loss, dx, big, small


def _coords():
    return lax.axis_index("x"), lax.axis_index("y"), lax.axis_index("c")


def _other_chips(x, y):
    return [(1 - x, y), (x, 1 - y), (1 - x, 1 - y)]


def _rows(ref, start, size):
    return ref.at[pl.ds(pl.multiple_of(start, size), size), :]


def _window(kind, ref, shard_shape, j, hf=None):
    r, cw = shard_shape
    row0, nr = (0, r) if hf is None else (hf * (r // 2), r // 2)
    if kind == "stack":
        return ref.at[j, pl.ds(pl.multiple_of(row0, nr), nr), :]
    if kind == "rows":
        return ref.at[pl.ds(pl.multiple_of(j * r + row0, nr), nr), :]
    assert kind == "cols"
    return ref.at[pl.ds(pl.multiple_of(row0, nr), nr), pl.ds(pl.multiple_of(j * cw, cw), cw)]


def _gathered_shape(kind, shard_shape):
    r, cw = shard_shape
    return {"stack": (N_CHIPS, r, cw), "rows": (N_CHIPS * r, cw), "cols": (r, N_CHIPS * cw)}[kind]


def _remote(src, dst, send_sems, recv_sems, k, to):
    return pltpu.make_async_remote_copy(src_ref=src, dst_ref=dst, send_sem=send_sems.at[k],
                                        recv_sem=recv_sems.at[k], device_id=to, device_id_type=MESH)


def _allgather_weights(shards, kinds, small):
    n = len(shards)
    shapes = [s.shape for s in shards]
    w = small.shape[1]

    def body(*refs):
        s_refs, sm_ref = refs[:n], refs[n]
        g_refs, smg_ref = refs[n + 1:2 * n + 1], refs[2 * n + 1]
        send_sems, recv_sems = refs[2 * n + 2:]
        x, y, c = _coords()
        me, sibling, chips = 2 * x + y, (x, y, 1 - c), _other_chips(x, y)
        ids = [2 * cx + cy for cx, cy in chips]
        own_base = 6 * n + 3

        def small_win(j):
            return smg_ref.at[:, pl.ds(pl.multiple_of(j * w, w), w)]

        first = [_remote(s_refs[a], _window(kinds[a], g_refs[a], shapes[a], me), send_sems, recv_sems,
                         own_base + a, sibling) for a in range(n)]
        first.append(_remote(sm_ref, small_win(me), send_sems, recv_sems, own_base + n, sibling))
        for a in range(n):
            src = _rows(s_refs[a], c * (shapes[a][0] // 2), shapes[a][0] // 2)
            dst = _window(kinds[a], g_refs[a], shapes[a], me, c)
            first += [_remote(src, dst, send_sems, recv_sems, 3 * a + k, (*chip, c)) for k, chip in enumerate(chips)]
        first += [_remote(sm_ref, small_win(me), send_sems, recv_sems, 6 * n + k, (*chip, c))
                  for k, chip in enumerate(chips)]
        for cp in first:
            cp.start()
        passed = []
        for a in range(n):
            for k in range(3):
                win = _window(kinds[a], g_refs[a], shapes[a], ids[k], c)
                _remote(win, win, send_sems, recv_sems, 3 * a + k, sibling).wait_recv()
                fwd = _remote(win, win, send_sems, recv_sems, 3 * n + 3 * a + k, sibling)
                fwd.start()
                passed.append(fwd)
        for k in range(3):
            win = small_win(ids[k])
            _remote(win, win, send_sems, recv_sems, 6 * n + k, sibling).wait_recv()
        for a in range(n):
            for k in range(3):
                win = _window(kinds[a], g_refs[a], shapes[a], ids[k], 1 - c)
                _remote(win, win, send_sems, recv_sems, 3 * n + 3 * a + k, sibling).wait_recv()
        for a in range(n):
            win = _window(kinds[a], g_refs[a], shapes[a], me)
            _remote(win, win, send_sems, recv_sems, own_base + a, sibling).wait_recv()
        _remote(small_win(me), small_win(me), send_sems, recv_sems, own_base + n, sibling).wait_recv()
        for cp in first + passed:
            cp.wait_send()

    out_shape = [jax.ShapeDtypeStruct(_gathered_shape(kinds[a], shapes[a]), shards[a].dtype) for a in range(n)]
    out_shape.append(jax.ShapeDtypeStruct((small.shape[0], N_CHIPS * w), small.dtype))
    nsem = 7 * n + 4
    return pl.pallas_call(
        body, name="allgather_weights", out_shape=tuple(out_shape),
        in_specs=[ANY] * (n + 1), out_specs=tuple([ANY] * (n + 1)),
        scratch_shapes=[pltpu.SemaphoreType.DMA((nsem,)), pltpu.SemaphoreType.DMA((nsem,))],
    )(*shards, small)


def _rs_pair_exchange(views):
    n = len(views)

    def body(*refs):
        g_refs, p_refs = refs[:n], refs[n:2 * n]
        send_sems, recv_sems = refs[2 * n:]
        x, y, c = _coords()
        cps = []
        for a in range(n):
            r2 = views[a].shape[1] // 2
            src = g_refs[a].at[:, pl.ds(pl.multiple_of((1 - c) * r2, r2), r2), :]
            cps.append(_remote(src, p_refs[a], send_sems, recv_sems, a, (x, y, 1 - c)))
        for cp in cps:
            cp.start()
        for cp in cps:
            cp.wait()

    out_shape = tuple(jax.ShapeDtypeStruct((v.shape[0], v.shape[1] // 2, v.shape[2]), v.dtype) for v in views)
    return pl.pallas_call(
        body, name="rs_pair_exchange", out_shape=out_shape,
        in_specs=[ANY] * n, out_specs=tuple([ANY] * n),
        scratch_shapes=[pltpu.SemaphoreType.DMA((n,)), pltpu.SemaphoreType.DMA((n,))],
    )(*views)


def _add_half(view, part, core, *, name):
    nch, r, cols = view.shape
    r2 = r // 2
    tr = min(128, r2)
    nt = r2 // tr

    def body(c_ref, g_ref, p_ref, o_ref):
        del c_ref
        o_ref[...] = (g_ref[...].astype(F32) + p_ref[...].astype(F32)).astype(o_ref.dtype)

    return pl.pallas_call(
        body, name=name, out_shape=jax.ShapeDtypeStruct((nch, r2, cols), view.dtype),
        grid_spec=pltpu.PrefetchScalarGridSpec(
            num_scalar_prefetch=1, grid=(nch, nt),
            in_specs=[pl.BlockSpec((1, tr, cols), lambda ch, i, c_ref: (ch, c_ref[0] * nt + i, 0)),
                      pl.BlockSpec((1, tr, cols), lambda ch, i, c_ref: (ch, i, 0))],
            out_specs=pl.BlockSpec((1, tr, cols), lambda ch, i, c_ref: (ch, i, 0))),
        compiler_params=_params("parallel", "parallel"),
    )(core, view, part)


def _chunk(kind, ref, j, cw):
    if kind == "cols":
        return ref.at[0, :, pl.ds(pl.multiple_of(j * cw, cw), cw)]
    return ref.at[j]


def _rs_chips(pairs, kinds):
    n = len(pairs)
    half_shapes = []
    for a in range(n):
        nch, r2, cols = pairs[a].shape
        half_shapes.append((r2, cols // N_CHIPS) if kinds[a] == "cols" else (r2, cols))

    def body(*refs):
        q_refs, t_refs = refs[:n], refs[n:2 * n]
        send_sems, recv_sems = refs[2 * n:]
        x, y, c = _coords()
        chips = _other_chips(x, y)
        sends = []
        for a in range(n):
            for k, (cx, cy) in enumerate(chips):
                src = _chunk(kinds[a], q_refs[a], 2 * cx + cy, half_shapes[a][1])
                sends.append(_remote(src, t_refs[a].at[k], send_sems, recv_sems, 3 * a + k, (cx, cy, c)))
        for cp in sends:
            cp.start()
        for a in range(n):
            for k in range(3):
                slot = t_refs[a].at[k]
                _remote(slot, slot, send_sems, recv_sems, 3 * a + k, (x, y, c)).wait_recv()
        for cp in sends:
            cp.wait_send()

    out_shape = tuple(jax.ShapeDtypeStruct((3, *half_shapes[a]), pairs[a].dtype) for a in range(n))
    return pl.pallas_call(
        body, name="rs_chips", out_shape=out_shape,
        in_specs=[ANY] * n, out_specs=tuple([ANY] * n),
        scratch_shapes=[pltpu.SemaphoreType.DMA((3 * n,)), pltpu.SemaphoreType.DMA((3 * n,))],
    )(*pairs)


def _sum_chips(pair, parts, chip, kind, *, name):
    _, r, cols = parts.shape
    tr = min(128, r)
    if kind == "cols":
        own_spec = pl.BlockSpec((1, tr, cols), lambda i, chip_ref: (0, i, chip_ref[0]))
    else:
        own_spec = pl.BlockSpec((1, tr, cols), lambda i, chip_ref: (chip_ref[0], i, 0))

    def body(chip_ref, q_ref, p_ref, o_ref):
        del chip_ref
        acc = q_ref[0].astype(F32)
        for s in range(3):
            acc = acc + p_ref[s].astype(F32)
        o_ref[...] = acc

    return pl.pallas_call(
        body, name=name, out_shape=jax.ShapeDtypeStruct((r, cols), F32),
        grid_spec=pltpu.PrefetchScalarGridSpec(
            num_scalar_prefetch=1, grid=(r // tr,),
            in_specs=[own_spec, pl.BlockSpec((3, tr, cols), lambda i, chip_ref: (0, i, 0))],
            out_specs=pl.BlockSpec((tr, cols), lambda i, chip_ref: (i, 0))),
        compiler_params=_params("parallel"),
    )(chip, pair, parts)


def _rs_share(halves):
    n = len(halves)

    def body(*refs):
        h_refs, g_refs = refs[:n], refs[n:2 * n]
        send_sems, recv_sems = refs[2 * n:]
        x, y, c = _coords()
        cps = [_remote(h_refs[a], g_refs[a], send_sems, recv_sems, a, (x, y, 1 - c)) for a in range(n)]
        for cp in cps:
            cp.start()
        for cp in cps:
            cp.wait()

    out_shape = tuple(jax.ShapeDtypeStruct(h.shape, h.dtype) for h in halves)
    return pl.pallas_call(
        body, name="rs_share", out_shape=out_shape,
        in_specs=[ANY] * n, out_specs=tuple([ANY] * n),
        scratch_shapes=[pltpu.SemaphoreType.DMA((n,)), pltpu.SemaphoreType.DMA((n,))],
    )(*halves)


def _adamw_halves(w, mine, theirs, m, v, core, *, name, tr=128):
    rows, cols = w.shape
    hr = rows // 2
    tr = min(tr, hr)
    nt = hr // tr

    def body(c_ref, w_ref, a_ref, b_ref, m_ref, v_ref, g_ref, d_ref, mo_ref, vo_ref):
        gg = jnp.where(pl.program_id(0) == c_ref[0], a_ref[...], b_ref[...])
        g_ref[...] = gg
        d, mn, vn = _adamw_math(w_ref[...], gg, m_ref[...], v_ref[...])
        d_ref[...] = d
        mo_ref[...] = mn
        vo_ref[...] = vn

    full = pl.BlockSpec((tr, cols), lambda hf, i, c_ref: (hf * nt + i, 0))
    half = pl.BlockSpec((tr, cols), lambda hf, i, c_ref: (i, 0))
    sd = jax.ShapeDtypeStruct((rows, cols), F32)
    return pl.pallas_call(
        body, name=name, out_shape=(sd, sd, sd, sd),
        grid_spec=pltpu.PrefetchScalarGridSpec(
            num_scalar_prefetch=1, grid=(2, nt),
            in_specs=[full, half, half, full, full], out_specs=(full, full, full, full)),
        compiler_params=_params("parallel", "parallel"),
    )(core, w, mine, theirs, m, v)


def _allreduce_small(vec):
    m_per, ncol = vec.shape
    n_dev = 2 * N_CHIPS

    def body(x_ref, out_ref, sum_ref, send_sems, recv_sems, local_sem):
        x, y, c = _coords()
        me, sibling = (x, y, c), (x, y, 1 - c)
        chips = _other_chips(x, y)

        def rows(px, py, pc):
            return out_ref.at[pl.ds(pl.multiple_of((4 * px + 2 * py + pc) * m_per, m_per), m_per), :]

        def copy(k, block, to, src=None):
            return _remote(rows(*block) if src is None else src, rows(*block), send_sems, recv_sems, k, to)

        mine = pltpu.make_async_copy(x_ref, rows(*me), local_sem)
        mine.start()
        first = [copy(0, me, sibling, src=x_ref)]
        first += [copy(1 + j, me, (*chip, c), src=x_ref) for j, chip in enumerate(chips)]
        for cp in first:
            cp.start()
        passed = [copy(4 + j, (*chip, c), sibling) for j, chip in enumerate(chips)]
        for j, chip in enumerate(chips):
            copy(1 + j, (*chip, c), me).wait_recv()
            passed[j].start()
        copy(0, sibling, me).wait_recv()
        for j, chip in enumerate(chips):
            copy(4 + j, (*chip, 1 - c), me).wait_recv()
        for cp in first + passed:
            cp.wait_send()
        mine.wait()
        acc = out_ref[0:m_per, :]
        for d in range(1, n_dev):
            acc = acc + out_ref[d * m_per:(d + 1) * m_per, :]
        sum_ref[...] = acc

    vm = pl.BlockSpec(memory_space=pltpu.VMEM)
    return pl.pallas_call(
        body, name="allreduce_small",
        out_shape=(jax.ShapeDtypeStruct((n_dev * m_per, ncol), vec.dtype), jax.ShapeDtypeStruct((m_per, ncol), vec.dtype)),
        in_specs=[vm], out_specs=(vm, vm),
        scratch_shapes=[pltpu.SemaphoreType.DMA((7,)), pltpu.SemaphoreType.DMA((7,)), pltpu.SemaphoreType.DMA],
    )(vec)[1]


def _pack_rows(pieces, total_rows):
    rows = []
    for p in pieces:
        flat = p.reshape(-1)
        flat = jnp.pad(flat, (0, (-flat.shape[0]) % LANES))
        rows.append(flat.reshape(-1, LANES))
    out = jnp.concatenate(rows, axis=0)
    return jnp.pad(out, ((0, total_rows - out.shape[0]), (0, 0)))


def _unpack_rows(packed, shapes):
    out, r = [], 0
    for shp in shapes:
        size = 1
        for d in shp:
            size *= d
        nr = -(-size // LANES)
        out.append(packed[r:r + nr].reshape(-1)[:size].reshape(shp))
        r += nr
    return out


def _round_up(v, m):
    return -(-v // m) * m


def kernel(x, a_w_in, a_gate_b, a_conv_w, a_conv_b, a_head_g, a_w_out, a_ln_g, a_ln_b, kv_w, b_w_in, b_w_out, b_ln_g, b_ln_b, loss_target, m_a_w_in, m_a_gate_b, m_a_conv_w, m_a_conv_b, m_a_head_g, m_a_w_out, m_a_ln_g, m_a_ln_b, m_kv_w, m_b_w_in, m_b_w_out, m_b_ln_g, m_b_ln_b, v_a_w_in, v_a_gate_b, v_a_conv_w, v_a_conv_b, v_a_head_g, v_a_w_out, v_a_ln_g, v_a_ln_b, v_kv_w, v_b_w_in, v_b_w_out, v_b_ln_g, v_b_ln_b):
    _, S, D = x.shape
    nha = a_gate_b.shape[1] // 2
    chip = 2 * lax.axis_index("x") + lax.axis_index("y")
    core = lax.axis_index("c").astype(jnp.int32).reshape(1)
    dq = D // N_CHIPS

    shards = [a_w_in[0].astype(BF16), a_w_out[0].astype(BF16), kv_w.astype(BF16), b_w_in[0].astype(BF16),
              b_w_out[0].astype(BF16)]
    kinds = ["stack", "rows", "cols", "cols", "rows"]
    small_shard = jnp.concatenate([a_conv_w[0], a_conv_b, a_head_g, a_ln_g, a_ln_b], axis=0)
    wa_g, wao, wkv, wbi, wbo, small_full = _allgather_weights(shards, kinds, small_shard)
    wa_full = jnp.concatenate([wa_g[j] for j in range(N_CHIPS)], axis=1)
    wa = wa_full[:, :4 * D]
    wg = jnp.pad(wa_full[:, 4 * D:], ((0, 0), (0, LANES - 2 * nha)))
    conv_w, conv_b, head_g, ln_g_a, ln_b_a = (small_full[0:CONV_A], small_full[4:5], small_full[5:6],
                                              small_full[6:7], small_full[7:8])

    loss_row, grad_x, big, small = _local_step(
        x[0], loss_target[0], wa, wg, wao, wbi, wkv, wbo, a_gate_b, conv_w, conv_b, head_g,
        ln_g_a, ln_b_a, b_ln_g, b_ln_b, S=S, D=D, nha=nha)

    g_full = jnp.concatenate([big["wa"], big["wg"][:, :2 * nha]], axis=1)
    ca = g_full.shape[1] // N_CHIPS
    g_wa4 = jnp.stack([g_full[:, j * ca:(j + 1) * ca] for j in range(N_CHIPS)], axis=0)
    views = [g_wa4, big["wao"].reshape(N_CHIPS, dq, D), big["wkv"][None], big["wbi"][None],
             big["wbo"].reshape(N_CHIPS, dq, D)]
    rs_kinds = ["stack", "stack", "cols", "cols", "stack"]
    names = ["a_w_in", "a_w_out", "kv_w", "b_w_in", "b_w_out"]
    parts = _rs_pair_exchange(views)
    pairs = [_add_half(v, p, core, name="pair_sum_" + nm) for v, p, nm in zip(views, parts, names)]
    slots = _rs_chips(pairs, rs_kinds)
    chip_ix = chip.astype(jnp.int32).reshape(1)
    halves = [_sum_chips(q, t, chip_ix, kd, name="chip_sum_" + nm)
              for q, t, kd, nm in zip(pairs, slots, rs_kinds, names)]
    others = _rs_share(halves)

    w_big = [a_w_in[0], a_w_out[0], kv_w, b_w_in[0], b_w_out[0]]
    m_big = [m_a_w_in[0], m_a_w_out[0], m_kv_w, m_b_w_in[0], m_b_w_out[0]]
    v_big = [v_a_w_in[0], v_a_w_out[0], v_kv_w, v_b_w_in[0], v_b_w_out[0]]
    upd_big = [_adamw_halves(w, a, b, m, v, core, name="adamw_" + nm)
               for w, a, b, m, v, nm in zip(w_big, halves, others, m_big, v_big, names)]
    g_big = [u[0] for u in upd_big]

    order = ["conv_w", "conv_b", "head_g", "a_ln_g", "a_ln_b", "b_ln_g", "b_ln_b", "gate_b"]
    full_shapes = [(CONV_A, D), (1, D), (1, D), (1, D), (1, D), (1, D), (1, D), (1, 2 * nha)]
    n_rows = sum(-(-(s[0] * s[1]) // LANES) for s in full_shapes) + 1
    packed = _pack_rows([small[k] for k in order] + [loss_row], _round_up(n_rows, 8))
    total = _allreduce_small(packed)
    sums = dict(zip(order, _unpack_rows(total, full_shapes)))
    loss = total[n_rows - 1, 0]

    def mine(v):
        return lax.dynamic_slice_in_dim(v, chip * dq, dq, axis=1)

    g_small = [sums["gate_b"], mine(sums["conv_w"]), mine(sums["conv_b"]), mine(sums["head_g"]),
               mine(sums["a_ln_g"]), mine(sums["a_ln_b"]), sums["b_ln_g"], sums["b_ln_b"]]
    w_small = [a_gate_b, a_conv_w[0], a_conv_b, a_head_g, a_ln_g, a_ln_b, b_ln_g, b_ln_b]
    m_small = [m_a_gate_b, m_a_conv_w[0], m_a_conv_b, m_a_head_g, m_a_ln_g, m_a_ln_b, m_b_ln_g, m_b_ln_b]
    v_small = [v_a_gate_b, v_a_conv_w[0], v_a_conv_b, v_a_head_g, v_a_ln_g, v_a_ln_b, v_b_ln_g, v_b_ln_b]
    shp_small = [w.shape for w in w_small]
    rows_small = _round_up(sum(-(-(s[0] * s[1]) // LANES) for s in shp_small), 8)
    upd_small = _adamw(_pack_rows(w_small, rows_small), _pack_rows(g_small, rows_small),
                       _pack_rows(m_small, rows_small), _pack_rows(v_small, rows_small), name="adamw_small")
    d_small, mn_small, vn_small = (_unpack_rows(u, shp_small) for u in upd_small)

    def assemble(big5, small8):
        awi, awo, kvw, bwi, bwo = big5
        gb, cw, cb, hg, alg, alb, blg, blb = small8
        return [awi[None], gb, cw[None], cb, hg, awo[None], alg, alb, kvw, bwi[None], bwo[None], blg, blb]

    grads = assemble(g_big, g_small)
    deltas = assemble([u[1] for u in upd_big], d_small)
    new_m = assemble([u[2] for u in upd_big], mn_small)
    new_v = assemble([u[3] for u in upd_big], vn_small)
    return (loss, grad_x[None], *grads, *deltas, *new_m, *new_v)
```

```python
import functools

import jax
import jax.numpy as jnp
from jax import lax
from jax.experimental import pallas as pl
from jax.experimental.pallas import tpu as pltpu

F32 = jnp.float32
BF16 = jnp.bfloat16

DEPTH = 2
ALPHA = (2.0 * DEPTH) ** 0.25
LN_EPS = 1e-5
DK_A = 128
DV_A = 256
DH_B = 128
SB_TQ = 256
SB_TK = 256
CONV_A = 4
ADAM_LR = 0.001
ADAM_B1 = 0.9
ADAM_B2 = 0.999
ADAM_EPS = 1e-08
ADAM_WD = 0.01
ADAM_STEP = 10
N_CHIPS = 4
LANES = 128
V7X_VMEM_BYTES = 64 * 1024 * 1024
VMEM_LIMIT = (V7X_VMEM_BYTES * 3) // 4
NEG_BIG = -1e30
MESH = pl.DeviceIdType.MESH
ANY = pl.BlockSpec(memory_space=pl.ANY)


def _params(*sem):
    return pltpu.CompilerParams(dimension_semantics=sem, vmem_limit_bytes=VMEM_LIMIT)


def _dot(a, b, dims):
    return lax.dot_general(a, b, (dims, ((), ())), preferred_element_type=F32)


def _dot_nn(a, b):
    return _dot(a, b, ((1,), (0,)))


def _dot_nt(a, b):
    return _dot(a, b, ((1,), (1,)))


def _dot_tn(a, b):
    return _dot(a, b, ((0,), (0,)))


def _split2(x):
    hi = x.astype(BF16)
    lo = (x - hi.astype(F32)).astype(BF16)
    return hi, lo


def _split3(x):
    hi = x.astype(BF16)
    r = x - hi.astype(F32)
    mid = r.astype(BF16)
    lo = (r - mid.astype(F32)).astype(BF16)
    return hi, mid, lo


def _mask_dot2(x, t01):
    hi, lo = _split2(x)
    return _dot_nn(hi, t01) + _dot_nn(lo, t01)


def _mask_dot3(x, t01):
    hi, mid, lo = _split3(x)
    return _dot_nn(hi, t01) + _dot_nn(mid, t01) + _dot_nn(lo, t01)


def _mask_dot3_left(t01, x):
    hi, mid, lo = _split3(x)
    return _dot_nn(t01, hi) + _dot_nn(t01, mid) + _dot_nn(t01, lo)


def _log_sigmoid(z):
    return jnp.minimum(z, 0.0) - jnp.log(1.0 + jnp.exp(-jnp.abs(z)))


def _sigmoid(z):
    return 1.0 / (1.0 + jnp.exp(-z))


def _mm(a, b, *, mode, M, N, K, out_dtype, name, tm=1024, tn=1024, tk=1024,
        a_off=(0, 0), b_off=(0, 0), acc_in=None, acc_scale=1.0, into=None, out_off=(0, 0)):
    tm, tn, tk = min(tm, M), min(tn, N), min(tk, K)
    assert M % tm == 0 and N % tn == 0 and K % tk == 0
    nk = K // tk
    ar, ac = a_off
    br, bc = b_off
    orow, ocol = out_off
    if mode in ("nn", "nt"):
        assert ar % tm == 0 and ac % tk == 0
        a_spec = pl.BlockSpec((tm, tk), lambda i, j, k: (i + ar // tm, k + ac // tk))
    else:
        assert ar % tk == 0 and ac % tm == 0
        a_spec = pl.BlockSpec((tk, tm), lambda i, j, k: (k + ar // tk, i + ac // tm))
    if mode == "nt":
        assert br % tn == 0 and bc % tk == 0
        b_spec = pl.BlockSpec((tn, tk), lambda i, j, k: (j + br // tn, k + bc // tk))
    else:
        assert br % tk == 0 and bc % tn == 0
        b_spec = pl.BlockSpec((tk, tn), lambda i, j, k: (k + br // tk, j + bc // tn))
    assert orow % tm == 0 and ocol % tn == 0
    o_spec = pl.BlockSpec((tm, tn), lambda i, j, k: (i + orow // tm, j + ocol // tn))
    dims = {"nn": ((1,), (0,)), "nt": ((1,), (1,)), "tn": ((0,), (0,))}[mode]
    inputs, in_specs = [a, b], [a_spec, b_spec]
    has_acc = acc_in is not None
    if has_acc:
        inputs.append(acc_in)
        in_specs.append(pl.BlockSpec((tm, tn), lambda i, j, k: (i, j)))
    aliases = {}
    if into is not None:
        inputs.append(into)
        in_specs.append(ANY)
        aliases = {len(inputs) - 1: 0}
        out_shape = jax.ShapeDtypeStruct(into.shape, into.dtype)
        assert into.dtype == out_dtype
    else:
        out_shape = jax.ShapeDtypeStruct((M, N), out_dtype)

    def body(*refs):
        a_ref, b_ref = refs[0], refs[1]
        acc_in_ref = refs[2] if has_acc else None
        n_in = len(inputs)
        o_ref = refs[n_in]

        def first():
            if has_acc:
                return acc_scale * acc_in_ref[...]
            return None

        if nk == 1:
            r = _dot(a_ref[...], b_ref[...], dims)
            f = first()
            if f is not None:
                r = r + f
            o_ref[...] = r.astype(o_ref.dtype)
        else:
            acc_ref = refs[n_in + 1]
            kk = pl.program_id(2)

            @pl.when(kk == 0)
            def _():
                f = first()
                acc_ref[...] = jnp.zeros_like(acc_ref) if f is None else f

            acc_ref[...] += _dot(a_ref[...], b_ref[...], dims)

            @pl.when(kk == nk - 1)
            def _():
                o_ref[...] = acc_ref[...].astype(o_ref.dtype)

    scratch = [] if nk == 1 else [pltpu.VMEM((tm, tn), F32)]
    return pl.pallas_call(
        body, name=name, out_shape=out_shape, grid=(M // tm, N // tn, nk),
        in_specs=in_specs, out_specs=o_spec, scratch_shapes=scratch,
        input_output_aliases=aliases,
        compiler_params=_params("parallel", "parallel", "arbitrary"),
    )(*inputs)


def _shift_down(x, d, row):
    if d == 0:
        return x
    return jnp.where(row >= d, pltpu.roll(x, d, 0), 0.0)


def _shift_up(x, d, row, n):
    if d == 0:
        return x
    return jnp.where(row < n - d, pltpu.roll(x, n - d, 0), 0.0)


def _conv_pre(x, w_ref, b_ref, row):
    c = b_ref[...] + w_ref[CONV_A - 1:CONV_A, :] * x
    for k in range(CONV_A - 1):
        c = c + w_ref[k:k + 1, :] * _shift_down(x, CONV_A - 1 - k, row)
    return c


def _conv_fwd(u, conv_w, conv_b, *, S, D, name):
    tc = 256
    nq_blocks = (D // 2) // tc

    def body(u_ref, w_ref, b_ref, o_ref):
        x = u_ref[...]
        row = lax.broadcasted_iota(jnp.int32, x.shape, 0)
        c = _conv_pre(x, w_ref, b_ref, row)
        scale = jnp.where(pl.program_id(0) >= nq_blocks, DK_A ** -0.5, 1.0).astype(F32)
        o_ref[...] = (c * _sigmoid(c) * scale).astype(BF16)

    return pl.pallas_call(
        body, name=name, out_shape=jax.ShapeDtypeStruct((S, D), BF16), grid=(D // tc,),
        in_specs=[pl.BlockSpec((S, tc), lambda j: (0, j)),
                  pl.BlockSpec((CONV_A, tc), lambda j: (0, j)),
                  pl.BlockSpec((1, tc), lambda j: (0, j))],
        out_specs=pl.BlockSpec((S, tc), lambda j: (0, j)),
        compiler_params=_params("parallel"),
    )(u, conv_w, conv_b)


def _conv_bwd(u, dq, dk, conv_w, conv_b, du, *, S, D, name):
    tc = 256
    nq_blocks = (D // 2) // tc

    def body(u_ref, dq_ref, dk_ref, w_ref, b_ref, du_in, du_ref, dw_ref, db_ref):
        del du_in
        x = u_ref[...]
        n = x.shape[0]
        row = lax.broadcasted_iota(jnp.int32, x.shape, 0)
        c = _conv_pre(x, w_ref, b_ref, row)
        is_k = pl.program_id(0) >= nq_blocks
        dy = jnp.where(is_k, dk_ref[...] * (DK_A ** -0.5), dq_ref[...])
        sg = _sigmoid(c)
        dc = dy * (sg * (1.0 + c * (1.0 - sg)))
        db_ref[...] = jnp.sum(dc, axis=0, keepdims=True)
        dx = w_ref[CONV_A - 1:CONV_A, :] * dc
        dw_ref[CONV_A - 1:CONV_A, :] = jnp.sum(dc * x, axis=0, keepdims=True)
        for k in range(CONV_A - 1):
            d = CONV_A - 1 - k
            dw_ref[k:k + 1, :] = jnp.sum(dc * _shift_down(x, d, row), axis=0, keepdims=True)
            dx = dx + w_ref[k:k + 1, :] * _shift_up(dc, d, row, n)
        du_ref[...] = dx.astype(BF16)

    half = lambda j: (0, j % nq_blocks)
    return pl.pallas_call(
        body, name=name,
        out_shape=(jax.ShapeDtypeStruct(du.shape, du.dtype),
                   jax.ShapeDtypeStruct((CONV_A, D), F32), jax.ShapeDtypeStruct((1, D), F32)),
        grid=(D // tc,),
        in_specs=[pl.BlockSpec((S, tc), lambda j: (0, j)),
                  pl.BlockSpec((S, tc), half), pl.BlockSpec((S, tc), half),
                  pl.BlockSpec((CONV_A, tc), lambda j: (0, j)),
                  pl.BlockSpec((1, tc), lambda j: (0, j)), ANY],
        out_specs=(pl.BlockSpec((S, tc), lambda j: (0, j)),
                   pl.BlockSpec((CONV_A, tc), lambda j: (0, j)),
                   pl.BlockSpec((1, tc), lambda j: (0, j))),
        input_output_aliases={5: 0},
        compiler_params=_params("parallel"),
    )(u, dq, dk, conv_w, conv_b, du)


def _tri(n, cmp):
    r = lax.broadcasted_iota(jnp.int32, (n, n), 0)
    c = lax.broadcasted_iota(jnp.int32, (n, n), 1)
    return r, c, cmp(r, c)


def _gates_fwd(gt, bias, *, nha, nb, name):
    half = nha * nb
    nb_shift = nb.bit_length() - 1
    assert nb == 1 << nb_shift

    def body(g_ref, b_ref, f_ref, brow_ref):
        ig = g_ref[0:half, :] + b_ref[0:half, :]
        fg = g_ref[half:2 * half, :] + b_ref[half:2 * half, :]
        lf = _log_sigmoid(fg)
        _, _, upper = _tri(LANES, lambda r, c: r <= c)
        cs = _mask_dot3(lf, upper.astype(BF16))
        tot = jnp.broadcast_to(cs[:, LANES - 1:LANES], cs.shape)
        r, c, _ = _tri(half, lambda r, c: r <= c)
        before = jnp.logical_and(r >> nb_shift == c >> nb_shift, c < r).astype(BF16)
        f = cs + _mask_dot3_left(before, tot)
        f_ref[...] = f
        brow_ref[...] = ig - f

    return pl.pallas_call(
        body, name=name,
        out_shape=(jax.ShapeDtypeStruct((half, LANES), F32), jax.ShapeDtypeStruct((half, LANES), F32)),
    )(gt, bias)


def _gates_bwd(rowsum, colsum, gt, bias, *, nha, nb, name):
    half = nha * nb
    nb_shift = nb.bit_length() - 1
    assert nb == 1 << nb_shift

    def body(rs_ref, cs_ref, g_ref, b_ref, dg_ref, tot_ref):
        col = cs_ref[...]
        df = rs_ref[...] - col
        _, _, lower = _tri(LANES, lambda r, c: r >= c)
        rc = _mask_dot3(df, lower.astype(BF16))
        tot = jnp.broadcast_to(rc[:, 0:1], rc.shape)
        r, c, _ = _tri(half, lambda r, c: r <= c)
        same = r >> nb_shift == c >> nb_shift
        after = jnp.logical_and(same, c > r).astype(BF16)
        dlf = rc + _mask_dot3_left(after, tot)
        fg = g_ref[half:2 * half, :] + b_ref[half:2 * half, :]
        dfg = dlf * _sigmoid(-fg)
        dg_ref[0:half, :] = col
        dg_ref[half:2 * half, :] = dfg
        grp = same.astype(BF16)
        ones = jnp.ones((LANES, LANES), BF16)
        tot_ref[0:half, :] = _mask_dot3_left(grp, _mask_dot3(col, ones))
        tot_ref[half:2 * half, :] = _mask_dot3_left(grp, _mask_dot3(dfg, ones))

    return pl.pallas_call(
        body, name=name,
        out_shape=(jax.ShapeDtypeStruct((2 * half, LANES), F32), jax.ShapeDtypeStruct((2 * half, LANES), F32)),
    )(rowsum, colsum, gt, bias)


def _mlstm_tile(q, k_ref, fcol, brow_ref, j, i, tq, m=None):
    off = pl.multiple_of(j * tq, tq)
    kj = k_ref[pl.ds(off, tq), :]
    s = _dot_nt(q, kj)
    row = lax.broadcasted_iota(jnp.int32, (tq, tq), 0)
    col = lax.broadcasted_iota(jnp.int32, (tq, tq), 1)
    valid = jnp.logical_or(col <= row, j < i)
    logd = jnp.where(valid, fcol + brow_ref[0, :, pl.ds(off, tq)], NEG_BIG)
    return off, kj, s, logd


def _mlstm_fwd(qk, u, fcol, brow, *, S, D, nha, name):
    tq = min(256, S)
    nq = S // tq
    kb, vb = (D // 2) // DK_A, D // DV_A

    def body(q_ref, k_ref, v_ref, fcol_ref, brow_ref, h_ref, m_ref, den_ref):
        i = pl.program_id(1)
        q = q_ref[...]
        fc = fcol_ref[0]

        def step(j, carry):
            acc, den, m = carry
            off, _, s, logd = _mlstm_tile(q, k_ref, fc, brow_ref, j, i, tq)
            m_new = jnp.maximum(m, jnp.max(logd, axis=1, keepdims=True))
            a = s * jnp.exp(logd - m_new)
            alpha = jnp.exp(m - m_new)
            vj = v_ref[pl.ds(off, tq), :].astype(BF16)
            acc = alpha * acc + _dot_nn(a.astype(BF16), vj)
            den = alpha * den + jnp.sum(a, axis=1, keepdims=True)
            return acc, den, m_new

        acc, den, m = lax.fori_loop(
            0, i + 1, step,
            (jnp.zeros((tq, DV_A), F32), jnp.zeros((tq, 1), F32), jnp.full((tq, 1), NEG_BIG, F32)))
        h_ref[...] = acc / jnp.maximum(jnp.abs(den), jnp.exp(-m))
        m_ref[0] = m
        den_ref[0] = den

    stat = pl.BlockSpec((1, tq, 1), lambda h, i: (h, i, 0))
    return pl.pallas_call(
        body, name=name,
        out_shape=(jax.ShapeDtypeStruct((S, D), F32), jax.ShapeDtypeStruct((nha, S, 1), F32),
                   jax.ShapeDtypeStruct((nha, S, 1), F32)),
        grid=(nha, nq),
        in_specs=[pl.BlockSpec((tq, DK_A), lambda h, i: (i, h)),
                  pl.BlockSpec((S, DK_A), lambda h, i: (0, kb + h)),
                  pl.BlockSpec((S, DV_A), lambda h, i: (0, vb + h)),
                  stat, pl.BlockSpec((1, 1, S), lambda h, i: (h, 0, 0))],
        out_specs=(pl.BlockSpec((tq, DV_A), lambda h, i: (i, h)), stat, stat),
        compiler_params=_params("parallel", "arbitrary"),
    )(qk, qk, u, fcol, brow)


def _mlstm_bwd(qk, u, fcol, brow, m, den, dh, h, du, *, S, D, nha, name):
    tq = min(256, S)
    nq = S // tq
    kb, vb = (D // 2) // DK_A, D // DV_A

    def body(q_ref, k_ref, v_ref, fcol_ref, brow_ref, m_ref, den_ref, dh_ref, h_ref, du_in,
             du_ref, dq_ref, dk_ref, rs_ref, cs_ref, dv_acc):
        del du_in
        i = pl.program_id(1)

        @pl.when(i == 0)
        def _():
            dk_ref[...] = jnp.zeros_like(dk_ref)
            cs_ref[...] = jnp.zeros_like(cs_ref)
            dv_acc[...] = jnp.zeros_like(dv_acc)

        q = q_ref[...]
        fc = fcol_ref[0]
        mm = m_ref[0]
        dn = den_ref[0]
        floor = jnp.exp(-mm)
        nrm = jnp.maximum(jnp.abs(dn), floor)
        dhv = dh_ref[...]
        dnum = dhv / nrm
        dnrm = -jnp.sum(dhv * h_ref[...], axis=1, keepdims=True) / nrm
        dden = jnp.where(jnp.abs(dn) > floor, jnp.where(dn > 0.0, dnrm, -dnrm), 0.0)
        dnum_b = dnum.astype(BF16)

        def step(j, carry):
            dq, rs = carry
            off, kj, s, logd = _mlstm_tile(q, k_ref, fc, brow_ref, j, i, tq)
            p = jnp.exp(logd - mm)
            a = s * p
            vj = v_ref[pl.ds(off, tq), :].astype(BF16)
            da = _dot_nt(dnum_b, vj) + dden
            dv_acc[pl.ds(off, tq), :] += _dot_tn(a.astype(BF16), dnum_b)
            dqk = (da * p).astype(BF16)
            dq = dq + _dot_nn(dqk, kj)
            dk_ref[pl.ds(off, tq), :] += _dot_tn(dqk, q)
            pm = da * a
            cs_ref[0, :, pl.ds(off, tq)] += jnp.sum(pm, axis=0, keepdims=True)
            rs = rs + jnp.sum(pm, axis=1, keepdims=True)
            return dq, rs

        dq, rs = lax.fori_loop(0, i + 1, step, (jnp.zeros((tq, DK_A), F32), jnp.zeros((tq, 1), F32)))
        dq_ref[...] = dq
        rs_ref[0] = rs

        @pl.when(i == nq - 1)
        def _():
            du_ref[...] = dv_acc[...].astype(BF16)

    stat = pl.BlockSpec((1, tq, 1), lambda h, i: (h, i, 0))
    rowv = pl.BlockSpec((1, 1, S), lambda h, i: (h, 0, 0))
    hblk = pl.BlockSpec((tq, DV_A), lambda h, i: (i, h))
    return pl.pallas_call(
        body, name=name,
        out_shape=(jax.ShapeDtypeStruct(du.shape, du.dtype),
                   jax.ShapeDtypeStruct((S, D // 2), F32), jax.ShapeDtypeStruct((S, D // 2), F32),
                   jax.ShapeDtypeStruct((nha, S, 1), F32), jax.ShapeDtypeStruct((nha, 1, S), F32)),
        grid=(nha, nq),
        in_specs=[pl.BlockSpec((tq, DK_A), lambda h, i: (i, h)),
                  pl.BlockSpec((S, DK_A), lambda h, i: (0, kb + h)),
                  pl.BlockSpec((S, DV_A), lambda h, i: (0, vb + h)),
                  stat, rowv, stat, stat, hblk, hblk, ANY],
        out_specs=(pl.BlockSpec((S, DV_A), lambda h, i: (0, vb + h)),
                   pl.BlockSpec((tq, DK_A), lambda h, i: (i, h)),
                   pl.BlockSpec((S, DK_A), lambda h, i: (0, h)),
                   stat, rowv),
        scratch_shapes=[pltpu.VMEM((S, DV_A), F32)],
        input_output_aliases={9: 0},
        compiler_params=_params("parallel", "arbitrary"),
    )(qk, qk, u, fcol, brow, m, den, dh, h, du)


def _head_norm(hh):
    mu = jnp.mean(hh, axis=1, keepdims=True)
    hc = hh - mu
    rstd = lax.rsqrt(jnp.mean(hc * hc, axis=1, keepdims=True) + LN_EPS)
    return hc * rstd, rstd


def _hgate_fwd(h, u, head_g, *, S, D, name):
    tm = min(256, S)
    nh = D // DV_A

    def body(h_ref, o_ref, z_ref, g_ref, out_ref):
        for hd in range(nh):
            sl = slice(hd * DV_A, (hd + 1) * DV_A)
            hn, _ = _head_norm(h_ref[:, sl])
            z = z_ref[:, sl]
            out_ref[:, sl] = (_sigmoid(o_ref[:, sl]) * (hn * g_ref[:, sl]) * (z * _sigmoid(z))).astype(BF16)

    return pl.pallas_call(
        body, name=name, out_shape=jax.ShapeDtypeStruct((S, D), BF16), grid=(S // tm,),
        in_specs=[pl.BlockSpec((tm, D), lambda i: (i, 0)), pl.BlockSpec((tm, D), lambda i: (i, 2)),
                  pl.BlockSpec((tm, D), lambda i: (i, 3)), pl.BlockSpec((1, D), lambda i: (0, 0))],
        out_specs=pl.BlockSpec((tm, D), lambda i: (i, 0)),
        compiler_params=_params("parallel"),
    )(h, u, u, head_g)


def _hgate_bwd(dhg, h, u, head_g, du, *, S, D, name):
    tm = min(256, S)
    nh = D // DV_A

    def body(dhg_ref, h_ref, o_ref, z_ref, g_ref, du_in, du_ref, dh_ref, dg_ref):
        del du_in

        @pl.when(pl.program_id(0) == 0)
        def _():
            dg_ref[...] = jnp.zeros_like(dg_ref)

        for hd in range(nh):
            sl = slice(hd * DV_A, (hd + 1) * DV_A)
            hn, rstd = _head_norm(h_ref[:, sl])
            o, z, g, d = o_ref[:, sl], z_ref[:, sl], g_ref[:, sl], dhg_ref[:, sl]
            so, sz = _sigmoid(o), _sigmoid(z)
            silu_z = z * sz
            hng = hn * g
            du_ref[:, sl] = (d * hng * silu_z * so * (1.0 - so)).astype(BF16)
            du_ref[:, D + hd * DV_A:D + (hd + 1) * DV_A] = (
                d * so * hng * (sz * (1.0 + z * (1.0 - sz)))).astype(BF16)
            t = d * so * silu_z
            dg_ref[:, sl] += jnp.sum(t * hn, axis=0, keepdims=True)
            dhn = t * g
            dh_ref[:, sl] = rstd * (dhn - jnp.mean(dhn, axis=1, keepdims=True)
                                    - hn * jnp.mean(dhn * hn, axis=1, keepdims=True))

    row = lambda i: (i, 0)
    return pl.pallas_call(
        body, name=name,
        out_shape=(jax.ShapeDtypeStruct(du.shape, du.dtype), jax.ShapeDtypeStruct((S, D), F32),
                   jax.ShapeDtypeStruct((1, D), F32)),
        grid=(S // tm,),
        in_specs=[pl.BlockSpec((tm, D), row), pl.BlockSpec((tm, D), row),
                  pl.BlockSpec((tm, D), lambda i: (i, 2)), pl.BlockSpec((tm, D), lambda i: (i, 3)),
                  pl.BlockSpec((1, D), lambda i: (0, 0)), ANY],
        out_specs=(pl.BlockSpec((tm, 2 * D), lambda i: (i, 1)), pl.BlockSpec((tm, D), row),
                   pl.BlockSpec((1, D), lambda i: (0, 0))),
        input_output_aliases={5: 0},
        compiler_params=_params("arbitrary"),
    )(dhg, h, u, u, head_g, du)


def _ln_stats(r):
    mu = jnp.mean(r, axis=1, keepdims=True)
    xc = r - mu
    rstd = lax.rsqrt(jnp.mean(xc * xc, axis=1, keepdims=True) + LN_EPS)
    return xc * rstd, rstd


def _ln_back(dxhat, xhat, rstd):
    return rstd * (dxhat - jnp.mean(dxhat, axis=1, keepdims=True)
                   - xhat * jnp.mean(dxhat * xhat, axis=1, keepdims=True))


def _ln_fwd(x, y, g, b, *, S, D, name):
    tm = min(256, S)

    def body(x_ref, y_ref, g_ref, b_ref, o_ref, ob_ref):
        xhat, _ = _ln_stats(ALPHA * x_ref[...] + y_ref[...])
        o = xhat * g_ref[...] + b_ref[...]
        o_ref[...] = o
        ob_ref[...] = o.astype(BF16)

    row = lambda i: (i, 0)
    vec = pl.BlockSpec((1, D), lambda i: (0, 0))
    return pl.pallas_call(
        body, name=name,
        out_shape=(jax.ShapeDtypeStruct((S, D), F32), jax.ShapeDtypeStruct((S, D), BF16)),
        grid=(S // tm,),
        in_specs=[pl.BlockSpec((tm, D), row), pl.BlockSpec((tm, D), row), vec, vec],
        out_specs=(pl.BlockSpec((tm, D), row), pl.BlockSpec((tm, D), row)),
        compiler_params=_params("parallel"),
    )(x, y, g, b)


def _ln_loss_bwd(x1, y2, target, g, b, *, S, D, name):
    tm = min(256, S)

    def body(x_ref, y_ref, t_ref, g_ref, b_ref, dr_ref, drb_ref, dg_ref, db_ref, loss_ref):
        @pl.when(pl.program_id(0) == 0)
        def _():
            dg_ref[...] = jnp.zeros_like(dg_ref)
            db_ref[...] = jnp.zeros_like(db_ref)
            loss_ref[...] = jnp.zeros_like(loss_ref)

        xhat, rstd = _ln_stats(ALPHA * x_ref[...] + y_ref[...])
        diff = xhat * g_ref[...] + b_ref[...] - t_ref[...]
        loss_ref[...] += (0.5 / D) * jnp.sum(diff * diff)
        dx2 = diff * (1.0 / D)
        dg_ref[...] += jnp.sum(dx2 * xhat, axis=0, keepdims=True)
        db_ref[...] += jnp.sum(dx2, axis=0, keepdims=True)
        dr = _ln_back(dx2 * g_ref[...], xhat, rstd)
        dr_ref[...] = dr
        drb_ref[...] = dr.astype(BF16)

    row = lambda i: (i, 0)
    vec = pl.BlockSpec((1, D), lambda i: (0, 0))
    return pl.pallas_call(
        body, name=name,
        out_shape=(jax.ShapeDtypeStruct((S, D), F32), jax.ShapeDtypeStruct((S, D), BF16),
                   jax.ShapeDtypeStruct((1, D), F32), jax.ShapeDtypeStruct((1, D), F32),
                   jax.ShapeDtypeStruct((1, LANES), F32)),
        grid=(S // tm,),
        in_specs=[pl.BlockSpec((tm, D), row)] * 3 + [vec, vec],
        out_specs=(pl.BlockSpec((tm, D), row), pl.BlockSpec((tm, D), row), vec, vec,
                   pl.BlockSpec((1, LANES), lambda i: (0, 0))),
        compiler_params=_params("arbitrary"),
    )(x1, y2, target, g, b)


def _ln_bwd(x, y, g, dout, *, S, D, name):
    tm = min(256, S)

    def body(x_ref, y_ref, g_ref, d_ref, dr_ref, drb_ref, dg_ref, db_ref):
        @pl.when(pl.program_id(0) == 0)
        def _():
            dg_ref[...] = jnp.zeros_like(dg_ref)
            db_ref[...] = jnp.zeros_like(db_ref)

        xhat, rstd = _ln_stats(ALPHA * x_ref[...] + y_ref[...])
        d = d_ref[...]
        dg_ref[...] += jnp.sum(d * xhat, axis=0, keepdims=True)
        db_ref[...] += jnp.sum(d, axis=0, keepdims=True)
        dr = _ln_back(d * g_ref[...], xhat, rstd)
        dr_ref[...] = dr
        drb_ref[...] = dr.astype(BF16)

    row = lambda i: (i, 0)
    vec = pl.BlockSpec((1, D), lambda i: (0, 0))
    return pl.pallas_call(
        body, name=name,
        out_shape=(jax.ShapeDtypeStruct((S, D), F32), jax.ShapeDtypeStruct((S, D), BF16),
                   jax.ShapeDtypeStruct((1, D), F32), jax.ShapeDtypeStruct((1, D), F32)),
        grid=(S // tm,),
        in_specs=[pl.BlockSpec((tm, D), row), pl.BlockSpec((tm, D), row), vec, pl.BlockSpec((tm, D), row)],
        out_specs=(pl.BlockSpec((tm, D), row), pl.BlockSpec((tm, D), row), vec, vec),
        compiler_params=_params("arbitrary"),
    )(x, y, g, dout)


def _sb_scores(q, kj, j, i, tq, tk):
    z = _dot_nt(q, kj) * (DH_B ** -0.5)
    row = lax.broadcasted_iota(jnp.int32, (tq, tk), 0) + i * tq
    col = lax.broadcasted_iota(jnp.int32, (tq, tk), 1) + j * tk
    valid = col < row
    ls = _log_sigmoid(z)
    lneg = jnp.where(valid, ls - z, 0.0)
    return valid, ls, lneg


def _sb_blocks(i, tq, tk):
    return ((i + 1) * tq + tk - 1) // tk


def _sb_fwd(q2, kv, *, S, D, name, tq=SB_TQ, tk=SB_TK):
    tq, tk = min(tq, S), min(tk, S)
    nq = S // tq
    nh = D // DH_B

    def body(q_ref, k_ref, v_ref, o_ref, tot_ref):
        i = pl.program_id(1)
        q = q_ref[...]
        _, _, after = _tri(tk, lambda r, c: r > c)
        tri_after = after.astype(BF16)
        nkv = _sb_blocks(i, tq, tk)

        def step(jj, carry):
            acc, cr = carry
            j = nkv - 1 - jj
            off = pl.multiple_of(j * tk, tk)
            valid, ls, lneg = _sb_scores(q, k_ref[pl.ds(off, tk), :], j, i, tq, tk)
            between = cr + _mask_dot2(lneg, tri_after)
            a = jnp.where(valid, jnp.exp(ls + between), 0.0)
            acc = acc + _dot_nn(a.astype(BF16), v_ref[pl.ds(off, tk), :])
            return acc, cr + jnp.sum(lneg, axis=1, keepdims=True)

        acc, cr = lax.fori_loop(0, nkv, step, (jnp.zeros((tq, DH_B), F32), jnp.zeros((tq, 1), F32)))
        o_ref[...] = acc
        tot_ref[0] = cr

    return pl.pallas_call(
        body, name=name,
        out_shape=(jax.ShapeDtypeStruct((S, D), F32), jax.ShapeDtypeStruct((nh, S, 1), F32)), grid=(nh, nq),
        in_specs=[pl.BlockSpec((tq, DH_B), lambda h, i: (i, h)),
                  pl.BlockSpec((S, DH_B), lambda h, i: (0, h)),
                  pl.BlockSpec((S, DH_B), lambda h, i: (0, nh + h))],
        out_specs=(pl.BlockSpec((tq, DH_B), lambda h, i: (i, h)), pl.BlockSpec((1, tq, 1), lambda h, i: (h, i, 0))),
        compiler_params=_params("parallel", "arbitrary"),
    )(q2, kv, kv)


def _sb_bwd(q2, kv, datt, tot, *, S, D, name, tq=SB_TQ, tk=SB_TK):
    tq, tk = min(tq, S), min(tk, S)
    nq = S // tq
    nh = D // DH_B

    def body(q_ref, k_ref, v_ref, do_ref, tot_ref, dq_ref, dk_ref, dv_ref, dk_acc, dv_acc):
        i = pl.program_id(1)

        @pl.when(i == 0)
        def _():
            dk_acc[...] = jnp.zeros_like(dk_acc)
            dv_acc[...] = jnp.zeros_like(dv_acc)

        q = q_ref[...]
        do_b = do_ref[...].astype(BF16)
        _, _, after = _tri(tk, lambda r, c: r > c)
        tri_after = after.astype(BF16)
        _, _, before = _tri(tk, lambda r, c: r < c)
        tri_before = before.astype(BF16)

        def step(j, carry):
            dq, rest, cg = carry
            off = pl.multiple_of(j * tk, tk)
            kj = k_ref[pl.ds(off, tk), :]
            valid, ls, lneg = _sb_scores(q, kj, j, i, tq, tk)
            rest = rest - jnp.sum(lneg, axis=1, keepdims=True)
            between = rest + _mask_dot2(lneg, tri_after)
            a = jnp.where(valid, jnp.exp(ls + between), 0.0)
            g = _dot_nt(do_b, v_ref[pl.ds(off, tk), :]) * a
            dv_acc[pl.ds(off, tk), :] += _dot_tn(a.astype(BF16), do_b)
            e = cg + _mask_dot2(g, tri_before)
            dz = jnp.where(valid, g * jnp.exp(lneg) - e * jnp.exp(ls), 0.0) * (DH_B ** -0.5)
            dz_b = dz.astype(BF16)
            dq = dq + _dot_nn(dz_b, kj)
            dk_acc[pl.ds(off, tk), :] += _dot_tn(dz_b, q)
            return dq, rest, cg + jnp.sum(g, axis=1, keepdims=True)

        dq, _, _ = lax.fori_loop(0, _sb_blocks(i, tq, tk), step,
                                 (jnp.zeros((tq, DH_B), F32), tot_ref[0], jnp.zeros((tq, 1), F32)))
        dq_ref[...] = dq.astype(BF16)

        @pl.when(i == nq - 1)
        def _():
            dk_ref[...] = dk_acc[...].astype(BF16)
            dv_ref[...] = dv_acc[...].astype(BF16)

    blk = pl.BlockSpec((tq, DH_B), lambda h, i: (i, h))
    return pl.pallas_call(
        body, name=name,
        out_shape=(jax.ShapeDtypeStruct((S, D), BF16), jax.ShapeDtypeStruct((S, D), BF16),
                   jax.ShapeDtypeStruct((S, D), BF16)),
        grid=(nh, nq),
        in_specs=[blk, pl.BlockSpec((S, DH_B), lambda h, i: (0, h)),
                  pl.BlockSpec((S, DH_B), lambda h, i: (0, nh + h)), blk,
                  pl.BlockSpec((1, tq, 1), lambda h, i: (h, i, 0))],
        out_specs=(blk, pl.BlockSpec((S, DH_B), lambda h, i: (0, h)),
                   pl.BlockSpec((S, DH_B), lambda h, i: (0, h))),
        scratch_shapes=[pltpu.VMEM((S, DH_B), F32), pltpu.VMEM((S, DH_B), F32)],
        compiler_params=_params("parallel", "arbitrary"),
    )(q2, kv, kv, datt, tot)


def _bgate_fwd(att, z2, *, S, D, name):
    tm = min(256, S)

    def body(a_ref, z_ref, o_ref):
        z = z_ref[...]
        o_ref[...] = (a_ref[...] * (z * _sigmoid(z))).astype(BF16)

    row = lambda i: (i, 0)
    return pl.pallas_call(
        body, name=name, out_shape=jax.ShapeDtypeStruct((S, D), BF16), grid=(S // tm,),
        in_specs=[pl.BlockSpec((tm, D), row)] * 2, out_specs=pl.BlockSpec((tm, D), row),
        compiler_params=_params("parallel"),
    )(att, z2)


def _bgate_bwd(dhb, att, z2, *, S, D, name):
    tm = min(256, S)

    def body(d_ref, a_ref, z_ref, da_ref, dz_ref):
        z, d = z_ref[...], d_ref[...]
        sz = _sigmoid(z)
        da_ref[...] = d * (z * sz)
        dz_ref[...] = (d * a_ref[...] * (sz * (1.0 + z * (1.0 - sz)))).astype(BF16)

    row = lambda i: (i, 0)
    return pl.pallas_call(
        body, name=name,
        out_shape=(jax.ShapeDtypeStruct((S, D), F32), jax.ShapeDtypeStruct((S, D), BF16)),
        grid=(S // tm,),
        in_specs=[pl.BlockSpec((tm, D), row)] * 3,
        out_specs=(pl.BlockSpec((tm, D), row), pl.BlockSpec((tm, D), row)),
        compiler_params=_params("parallel"),
    )(dhb, att, z2)


def _adamw_math(w, g, m, v):
    mn = ADAM_B1 * m + (1.0 - ADAM_B1) * g
    vn = ADAM_B2 * v + (1.0 - ADAM_B2) * (g * g)
    m_hat = mn / (1.0 - ADAM_B1 ** ADAM_STEP)
    v_hat = vn / (1.0 - ADAM_B2 ** ADAM_STEP)
    return -ADAM_LR * (m_hat / (jnp.sqrt(v_hat) + ADAM_EPS) + ADAM_WD * w), mn, vn


def _adamw(w, g, m, v, *, name, tr=128):
    rows, cols = w.shape
    tr = min(tr, rows)
    assert rows % tr == 0

    def body(w_ref, g_ref, m_ref, v_ref, d_ref, mo_ref, vo_ref):
        d, mn, vn = _adamw_math(w_ref[...], g_ref[...], m_ref[...], v_ref[...])
        d_ref[...] = d
        mo_ref[...] = mn
        vo_ref[...] = vn

    blk = pl.BlockSpec((tr, cols), lambda i: (i, 0))
    sd = jax.ShapeDtypeStruct((rows, cols), F32)
    return pl.pallas_call(
        body, name=name, out_shape=(sd, sd, sd), grid=(rows // tr,),
        in_specs=[blk] * 4, out_specs=(blk, blk, blk),
        compiler_params=_params("parallel"),
    )(w, g, m, v)


def _local_step(x, target, wa, wg, wao, wbi, wkv, wbo, gate_b, conv_w, conv_b, head_g,
                a_ln_g, a_ln_b, b_ln_g, b_ln_b, *, S, D, nha):
    nb = S // LANES
    kw = dict(S=S, D=D)
    xb = x.astype(BF16)
    u = _mm(xb, wa, mode="nn", M=S, N=4 * D, K=D, out_dtype=F32, name="a_in")
    ug = _mm(xb, wg, mode="nn", M=S, N=LANES, K=D, out_dtype=F32, name="a_in_gates")
    qk = _conv_fwd(u, conv_w, conv_b, name="conv_fwd", **kw)
    gt = ug[:, :2 * nha].T.reshape(2 * nha * nb, LANES)
    gbias = jnp.repeat(gate_b.reshape(2 * nha), nb).reshape(2 * nha * nb, 1)
    fcs, brow = _gates_fwd(gt, gbias, nha=nha, nb=nb, name="gates_fwd")
    fcol = fcs.reshape(nha, S, 1)
    brow = brow.reshape(nha, 1, S)
    h, m, den = _mlstm_fwd(qk, u, fcol, brow, nha=nha, name="mlstm_fwd", **kw)
    hg = _hgate_fwd(h, u, head_g, name="hgate_fwd", **kw)
    y = _mm(hg, wao, mode="nn", M=S, N=D, K=D, out_dtype=F32, name="a_out")
    x1, x1b = _ln_fwd(x, y, a_ln_g, a_ln_b, name="ln_a_fwd", **kw)
    q2 = _mm(x1b, wbi, mode="nn", M=S, N=D, K=D, out_dtype=BF16, name="b_in_q")
    z2 = _mm(x1b, wbi, mode="nn", M=S, N=D, K=D, out_dtype=F32, name="b_in_z", b_off=(0, D))
    kv = _mm(x1b, wkv, mode="nn", M=S, N=2 * D, K=D, out_dtype=BF16, name="b_kv")
    att, sb_tot = _sb_fwd(q2, kv, name="sb_fwd", **kw)
    hb = _bgate_fwd(att, z2, name="bgate_fwd", **kw)
    y2 = _mm(hb, wbo, mode="nn", M=S, N=D, K=D, out_dtype=F32, name="b_out")
    dr2, dr2b, d_bln_g, d_bln_b, loss = _ln_loss_bwd(x1, y2, target, b_ln_g, b_ln_b, name="ln_b_loss", **kw)
    g_wbo = _mm(hb, dr2b, mode="tn", M=D, N=D, K=S, out_dtype=BF16, name="g_b_out")
    dhb = _mm(dr2b, wbo, mode="nt", M=S, N=D, K=D, out_dtype=F32, name="d_b_out")
    datt, dz2 = _bgate_bwd(dhb, att, z2, name="bgate_bwd", **kw)
    dq2, dk2, dv2 = _sb_bwd(q2, kv, datt, sb_tot, name="sb_bwd", **kw)
    g_wbi = _mm(x1b, dq2, mode="tn", M=D, N=D, K=S, out_dtype=BF16, name="g_b_in_q",
                into=lax.empty((D, 2 * D), BF16))
    g_wbi = _mm(x1b, dz2, mode="tn", M=D, N=D, K=S, out_dtype=BF16, name="g_b_in_z", into=g_wbi, out_off=(0, D))
    g_wkv = _mm(x1b, dk2, mode="tn", M=D, N=D, K=S, out_dtype=BF16, name="g_kv_k",
                into=lax.empty((D, 2 * D), BF16))
    g_wkv = _mm(x1b, dv2, mode="tn", M=D, N=D, K=S, out_dtype=BF16, name="g_kv_v", into=g_wkv, out_off=(0, D))
    dx1 = _mm(dq2, wbi, mode="nt", M=S, N=D, K=D, out_dtype=F32, name="d_b_in_q", acc_in=dr2, acc_scale=ALPHA)
    dx1 = _mm(dz2, wbi, mode="nt", M=S, N=D, K=D, out_dtype=F32, name="d_b_in_z", b_off=(0, D), acc_in=dx1)
    dx1 = _mm(dk2, wkv, mode="nt", M=S, N=D, K=D, out_dtype=F32, name="d_kv_k", acc_in=dx1)
    dx1 = _mm(dv2, wkv, mode="nt", M=S, N=D, K=D, out_dtype=F32, name="d_kv_v", b_off=(0, D), acc_in=dx1)
    dr, drb, d_aln_g, d_aln_b = _ln_bwd(x, y, a_ln_g, dx1, name="ln_a_bwd", **kw)
    g_wao = _mm(hg, drb, mode="tn", M=D, N=D, K=S, out_dtype=BF16, name="g_a_out")
    dhg = _mm(drb, wao, mode="nt", M=S, N=D, K=D, out_dtype=F32, name="d_a_out")
    du = lax.empty((S, 4 * D), BF16)
    du, dh, d_head_g = _hgate_bwd(dhg, h, u, head_g, du, name="hgate_bwd", **kw)
    du, dq, dk, rowsum, colsum = _mlstm_bwd(qk, u, fcol, brow, m, den, dh, h, du, nha=nha, name="mlstm_bwd", **kw)
    dgt, dgtot = _gates_bwd(rowsum.reshape(nha * nb, LANES), colsum.reshape(nha * nb, LANES), gt, gbias,
                            nha=nha, nb=nb, name="gates_bwd")
    d_gate_b = dgtot[::nb, 0].reshape(1, 2 * nha)
    dgp = jnp.pad(dgt.reshape(2 * nha, S).T, ((0, 0), (0, LANES - 2 * nha))).astype(BF16)
    du, d_conv_w, d_conv_b = _conv_bwd(u, dq, dk, conv_w, conv_b, du, name="conv_bwd", **kw)
    g_wa = _mm(xb, du, mode="tn", M=D, N=4 * D, K=S, out_dtype=BF16, name="g_a_in")
    g_wg = _mm(xb, dgp, mode="tn", M=D, N=LANES, K=S, out_dtype=BF16, name="g_a_in_gates")
    dx = _mm(du, wa, mode="nt", M=S, N=D, K=4 * D, out_dtype=F32, name="d_a_in", acc_in=dr, acc_scale=ALPHA)
    dx = _mm(dgp, wg, mode="nt", M=S, N=D, K=LANES, out_dtype=F32, name="d_a_in_gates", acc_in=dx)
    big = dict(wa=g_wa, wg=g_wg, wao=g_wao, wbi=g_wbi, wkv=g_wkv, wbo=g_wbo)
    small = dict(gate_b=d_gate_b, conv_w=d_conv_w, conv_b=d_conv_b, head_g=d_head_g,
                 a_ln_g=d_aln_g, a_ln_b=d_aln_b, b_ln_g=d_bln_g, b_ln_b=d_bln_b)
    return loss, dx, big, small


def _coords():
    return lax.axis_index("x"), lax.axis_index("y"), lax.axis_index("c")


def _other_chips(x, y):
    return [(1 - x, y), (x, 1 - y), (1 - x, 1 - y)]


def _rows(ref, start, size):
    return ref.at[pl.ds(pl.multiple_of(start, size), size), :]


def _window(kind, ref, shard_shape, j, hf=None):
    r, cw = shard_shape
    row0, nr = (0, r) if hf is None else (hf * (r // 2), r // 2)
    if kind == "stack":
        return ref.at[j, pl.ds(pl.multiple_of(row0, nr), nr), :]
    if kind == "rows":
        return ref.at[pl.ds(pl.multiple_of(j * r + row0, nr), nr), :]
    assert kind == "cols"
    return ref.at[pl.ds(pl.multiple_of(row0, nr), nr), pl.ds(pl.multiple_of(j * cw, cw), cw)]


def _gathered_shape(kind, shard_shape):
    r, cw = shard_shape
    return {"stack": (N_CHIPS, r, cw), "rows": (N_CHIPS * r, cw), "cols": (r, N_CHIPS * cw)}[kind]


def _remote(src, dst, send_sems, recv_sems, k, to):
    return pltpu.make_async_remote_copy(src_ref=src, dst_ref=dst, send_sem=send_sems.at[k],
                                        recv_sem=recv_sems.at[k], device_id=to, device_id_type=MESH)


def _allgather_weights(shards, kinds, small):
    n = len(shards)
    shapes = [s.shape for s in shards]
    w = small.shape[1]

    def body(*refs):
        s_refs, sm_ref = refs[:n], refs[n]
        g_refs, smg_ref = refs[n + 1:2 * n + 1], refs[2 * n + 1]
        send_sems, recv_sems = refs[2 * n + 2:]
        x, y, c = _coords()
        me, sibling, chips = 2 * x + y, (x, y, 1 - c), _other_chips(x, y)
        ids = [2 * cx + cy for cx, cy in chips]
        own_base = 6 * n + 3

        def small_win(j):
            return smg_ref.at[:, pl.ds(pl.multiple_of(j * w, w), w)]

        first = [_remote(s_refs[a], _window(kinds[a], g_refs[a], shapes[a], me), send_sems, recv_sems,
                         own_base + a, sibling) for a in range(n)]
        first.append(_remote(sm_ref, small_win(me), send_sems, recv_sems, own_base + n, sibling))
        for a in range(n):
            src = _rows(s_refs[a], c * (shapes[a][0] // 2), shapes[a][0] // 2)
            dst = _window(kinds[a], g_refs[a], shapes[a], me, c)
            first += [_remote(src, dst, send_sems, recv_sems, 3 * a + k, (*chip, c)) for k, chip in enumerate(chips)]
        first += [_remote(sm_ref, small_win(me), send_sems, recv_sems, 6 * n + k, (*chip, c))
                  for k, chip in enumerate(chips)]
        for cp in first:
            cp.start()
        passed = []
        for a in range(n):
            for k in range(3):
                win = _window(kinds[a], g_refs[a], shapes[a], ids[k], c)
                _remote(win, win, send_sems, recv_sems, 3 * a + k, sibling).wait_recv()
                fwd = _remote(win, win, send_sems, recv_sems, 3 * n + 3 * a + k, sibling)
                fwd.start()
                passed.append(fwd)
        for k in range(3):
            win = small_win(ids[k])
            _remote(win, win, send_sems, recv_sems, 6 * n + k, sibling).wait_recv()
        for a in range(n):
            for k in range(3):
                win = _window(kinds[a], g_refs[a], shapes[a], ids[k], 1 - c)
                _remote(win, win, send_sems, recv_sems, 3 * n + 3 * a + k, sibling).wait_recv()
        for a in range(n):
            win = _window(kinds[a], g_refs[a], shapes[a], me)
            _remote(win, win, send_sems, recv_sems, own_base + a, sibling).wait_recv()
        _remote(small_win(me), small_win(me), send_sems, recv_sems, own_base + n, sibling).wait_recv()
        for cp in first + passed:
            cp.wait_send()

    out_shape = [jax.ShapeDtypeStruct(_gathered_shape(kinds[a], shapes[a]), shards[a].dtype) for a in range(n)]
    out_shape.append(jax.ShapeDtypeStruct((small.shape[0], N_CHIPS * w), small.dtype))
    nsem = 7 * n + 4
    return pl.pallas_call(
        body, name="allgather_weights", out_shape=tuple(out_shape),
        in_specs=[ANY] * (n + 1), out_specs=tuple([ANY] * (n + 1)),
        scratch_shapes=[pltpu.SemaphoreType.DMA((nsem,)), pltpu.SemaphoreType.DMA((nsem,))],
    )(*shards, small)


def _rs_pair_exchange(views):
    n = len(views)

    def body(*refs):
        g_refs, p_refs = refs[:n], refs[n:2 * n]
        send_sems, recv_sems = refs[2 * n:]
        x, y, c = _coords()
        cps = []
        for a in range(n):
            r2 = views[a].shape[1] // 2
            src = g_refs[a].at[:, pl.ds(pl.multiple_of((1 - c) * r2, r2), r2), :]
            cps.append(_remote(src, p_refs[a], send_sems, recv_sems, a, (x, y, 1 - c)))
        for cp in cps:
            cp.start()
        for cp in cps:
            cp.wait()

    out_shape = tuple(jax.ShapeDtypeStruct((v.shape[0], v.shape[1] // 2, v.shape[2]), v.dtype) for v in views)
    return pl.pallas_call(
        body, name="rs_pair_exchange", out_shape=out_shape,
        in_specs=[ANY] * n, out_specs=tuple([ANY] * n),
        scratch_shapes=[pltpu.SemaphoreType.DMA((n,)), pltpu.SemaphoreType.DMA((n,))],
    )(*views)


def _add_half(view, part, core, *, name):
    nch, r, cols = view.shape
    r2 = r // 2
    tr = min(128, r2)
    nt = r2 // tr

    def body(c_ref, g_ref, p_ref, o_ref):
        del c_ref
        o_ref[...] = (g_ref[...].astype(F32) + p_ref[...].astype(F32)).astype(o_ref.dtype)

    return pl.pallas_call(
        body, name=name, out_shape=jax.ShapeDtypeStruct((nch, r2, cols), view.dtype),
        grid_spec=pltpu.PrefetchScalarGridSpec(
            num_scalar_prefetch=1, grid=(nch, nt),
            in_specs=[pl.BlockSpec((1, tr, cols), lambda ch, i, c_ref: (ch, c_ref[0] * nt + i, 0)),
                      pl.BlockSpec((1, tr, cols), lambda ch, i, c_ref: (ch, i, 0))],
            out_specs=pl.BlockSpec((1, tr, cols), lambda ch, i, c_ref: (ch, i, 0))),
        compiler_params=_params("parallel", "parallel"),
    )(core, view, part)


def _chunk(kind, ref, j, cw):
    if kind == "cols":
        return ref.at[0, :, pl.ds(pl.multiple_of(j * cw, cw), cw)]
    return ref.at[j]


def _rs_chips(pairs, kinds):
    n = len(pairs)
    half_shapes = []
    for a in range(n):
        nch, r2, cols = pairs[a].shape
        half_shapes.append((r2, cols // N_CHIPS) if kinds[a] == "cols" else (r2, cols))

    def body(*refs):
        q_refs, t_refs = refs[:n], refs[n:2 * n]
        send_sems, recv_sems = refs[2 * n:]
        x, y, c = _coords()
        chips = _other_chips(x, y)
        sends = []
        for a in range(n):
            for k, (cx, cy) in enumerate(chips):
                src = _chunk(kinds[a], q_refs[a], 2 * cx + cy, half_shapes[a][1])
                sends.append(_remote(src, t_refs[a].at[k], send_sems, recv_sems, 3 * a + k, (cx, cy, c)))
        for cp in sends:
            cp.start()
        for a in range(n):
            for k in range(3):
                slot = t_refs[a].at[k]
                _remote(slot, slot, send_sems, recv_sems, 3 * a + k, (x, y, c)).wait_recv()
        for cp in sends:
            cp.wait_send()

    out_shape = tuple(jax.ShapeDtypeStruct((3, *half_shapes[a]), pairs[a].dtype) for a in range(n))
    return pl.pallas_call(
        body, name="rs_chips", out_shape=out_shape,
        in_specs=[ANY] * n, out_specs=tuple([ANY] * n),
        scratch_shapes=[pltpu.SemaphoreType.DMA((3 * n,)), pltpu.SemaphoreType.DMA((3 * n,))],
    )(*pairs)


def _sum_chips(pair, parts, chip, kind, *, name):
    _, r, cols = parts.shape
    tr = min(128, r)
    if kind == "cols":
        own_spec = pl.BlockSpec((1, tr, cols), lambda i, chip_ref: (0, i, chip_ref[0]))
    else:
        own_spec = pl.BlockSpec((1, tr, cols), lambda i, chip_ref: (chip_ref[0], i, 0))

    def body(chip_ref, q_ref, p_ref, o_ref):
        del chip_ref
        acc = q_ref[0].astype(F32)
        for s in range(3):
            acc = acc + p_ref[s].astype(F32)
        o_ref[...] = acc

    return pl.pallas_call(
        body, name=name, out_shape=jax.ShapeDtypeStruct((r, cols), F32),
        grid_spec=pltpu.PrefetchScalarGridSpec(
            num_scalar_prefetch=1, grid=(r // tr,),
            in_specs=[own_spec, pl.BlockSpec((3, tr, cols), lambda i, chip_ref: (0, i, 0))],
            out_specs=pl.BlockSpec((tr, cols), lambda i, chip_ref: (i, 0))),
        compiler_params=_params("parallel"),
    )(chip, pair, parts)


def _rs_share(halves):
    n = len(halves)

    def body(*refs):
        h_refs, g_refs = refs[:n], refs[n:2 * n]
        send_sems, recv_sems = refs[2 * n:]
        x, y, c = _coords()
        cps = [_remote(h_refs[a], g_refs[a], send_sems, recv_sems, a, (x, y, 1 - c)) for a in range(n)]
        for cp in cps:
            cp.start()
        for cp in cps:
            cp.wait()

    out_shape = tuple(jax.ShapeDtypeStruct(h.shape, h.dtype) for h in halves)
    return pl.pallas_call(
        body, name="rs_share", out_shape=out_shape,
        in_specs=[ANY] * n, out_specs=tuple([ANY] * n),
        scratch_shapes=[pltpu.SemaphoreType.DMA((n,)), pltpu.SemaphoreType.DMA((n,))],
    )(*halves)


def _adamw_halves(w, mine, theirs, m, v, core, *, name, tr=128):
    rows, cols = w.shape
    hr = rows // 2
    tr = min(tr, hr)
    nt = hr // tr

    def body(c_ref, w_ref, a_ref, b_ref, m_ref, v_ref, g_ref, d_ref, mo_ref, vo_ref):
        gg = jnp.where(pl.program_id(0) == c_ref[0], a_ref[...], b_ref[...])
        g_ref[...] = gg
        d, mn, vn = _adamw_math(w_ref[...], gg, m_ref[...], v_ref[...])
        d_ref[...] = d
        mo_ref[...] = mn
        vo_ref[...] = vn

    full = pl.BlockSpec((tr, cols), lambda hf, i, c_ref: (hf * nt + i, 0))
    half = pl.BlockSpec((tr, cols), lambda hf, i, c_ref: (i, 0))
    sd = jax.ShapeDtypeStruct((rows, cols), F32)
    return pl.pallas_call(
        body, name=name, out_shape=(sd, sd, sd, sd),
        grid_spec=pltpu.PrefetchScalarGridSpec(
            num_scalar_prefetch=1, grid=(2, nt),
            in_specs=[full, half, half, full, full], out_specs=(full, full, full, full)),
        compiler_params=_params("parallel", "parallel"),
    )(core, w, mine, theirs, m, v)


def _allreduce_small(vec):
    m_per, ncol = vec.shape
    n_dev = 2 * N_CHIPS

    def body(x_ref, out_ref, sum_ref, send_sems, recv_sems, local_sem):
        x, y, c = _coords()
        me, sibling = (x, y, c), (x, y, 1 - c)
        chips = _other_chips(x, y)

        def rows(px, py, pc):
            return out_ref.at[pl.ds(pl.multiple_of((4 * px + 2 * py + pc) * m_per, m_per), m_per), :]

        def copy(k, block, to, src=None):
            return _remote(rows(*block) if src is None else src, rows(*block), send_sems, recv_sems, k, to)

        mine = pltpu.make_async_copy(x_ref, rows(*me), local_sem)
        mine.start()
        first = [copy(0, me, sibling, src=x_ref)]
        first += [copy(1 + j, me, (*chip, c), src=x_ref) for j, chip in enumerate(chips)]
        for cp in first:
            cp.start()
        passed = [copy(4 + j, (*chip, c), sibling) for j, chip in enumerate(chips)]
        for j, chip in enumerate(chips):
            copy(1 + j, (*chip, c), me).wait_recv()
            passed[j].start()
        copy(0, sibling, me).wait_recv()
        for j, chip in enumerate(chips):
            copy(4 + j, (*chip, 1 - c), me).wait_recv()
        for cp in first + passed:
            cp.wait_send()
        mine.wait()
        acc = out_ref[0:m_per, :]
        for d in range(1, n_dev):
            acc = acc + out_ref[d * m_per:(d + 1) * m_per, :]
        sum_ref[...] = acc

    vm = pl.BlockSpec(memory_space=pltpu.VMEM)
    return pl.pallas_call(
        body, name="allreduce_small",
        out_shape=(jax.ShapeDtypeStruct((n_dev * m_per, ncol), vec.dtype), jax.ShapeDtypeStruct((m_per, ncol), vec.dtype)),
        in_specs=[vm], out_specs=(vm, vm),
        scratch_shapes=[pltpu.SemaphoreType.DMA((7,)), pltpu.SemaphoreType.DMA((7,)), pltpu.SemaphoreType.DMA],
    )(vec)[1]


def _pack_rows(pieces, total_rows):
    rows = []
    for p in pieces:
        flat = p.reshape(-1)
        flat = jnp.pad(flat, (0, (-flat.shape[0]) % LANES))
        rows.append(flat.reshape(-1, LANES))
    out = jnp.concatenate(rows, axis=0)
    return jnp.pad(out, ((0, total_rows - out.shape[0]), (0, 0)))


def _unpack_rows(packed, shapes):
    out, r = [], 0
    for shp in shapes:
        size = 1
        for d in shp:
            size *= d
        nr = -(-size // LANES)
        out.append(packed[r:r + nr].reshape(-1)[:size].reshape(shp))
        r += nr
    return out


def _round_up(v, m):
    return -(-v // m) * m


def kernel(x, a_w_in, a_gate_b, a_conv_w, a_conv_b, a_head_g, a_w_out, a_ln_g, a_ln_b, kv_w, b_w_in, b_w_out, b_ln_g, b_ln_b, loss_target, m_a_w_in, m_a_gate_b, m_a_conv_w, m_a_conv_b, m_a_head_g, m_a_w_out, m_a_ln_g, m_a_ln_b, m_kv_w, m_b_w_in, m_b_w_out, m_b_ln_g, m_b_ln_b, v_a_w_in, v_a_gate_b, v_a_conv_w, v_a_conv_b, v_a_head_g, v_a_w_out, v_a_ln_g, v_a_ln_b, v_kv_w, v_b_w_in, v_b_w_out, v_b_ln_g, v_b_ln_b):
    _, S, D = x.shape
    nha = a_gate_b.shape[1] // 2
    chip = 2 * lax.axis_index("x") + lax.axis_index("y")
    core = lax.axis_index("c").astype(jnp.int32).reshape(1)
    dq = D // N_CHIPS

    shards = [a_w_in[0].astype(BF16), a_w_out[0].astype(BF16), kv_w.astype(BF16), b_w_in[0].astype(BF16),
              b_w_out[0].astype(BF16)]
    kinds = ["stack", "rows", "cols", "cols", "rows"]
    small_shard = jnp.concatenate([a_conv_w[0], a_conv_b, a_head_g, a_ln_g, a_ln_b], axis=0)
    wa_g, wao, wkv, wbi, wbo, small_full = _allgather_weights(shards, kinds, small_shard)
    wa_full = jnp.concatenate([wa_g[j] for j in range(N_CHIPS)], axis=1)
    wa = wa_full[:, :4 * D]
    wg = jnp.pad(wa_full[:, 4 * D:], ((0, 0), (0, LANES - 2 * nha)))
    conv_w, conv_b, head_g, ln_g_a, ln_b_a = (small_full[0:CONV_A], small_full[4:5], small_full[5:6],
                                              small_full[6:7], small_full[7:8])

    loss_row, grad_x, big, small = _local_step(
        x[0], loss_target[0], wa, wg, wao, wbi, wkv, wbo, a_gate_b, conv_w, conv_b, head_g,
        ln_g_a, ln_b_a, b_ln_g, b_ln_b, S=S, D=D, nha=nha)

    g_full = jnp.concatenate([big["wa"], big["wg"][:, :2 * nha]], axis=1)
    ca = g_full.shape[1] // N_CHIPS
    g_wa4 = jnp.stack([g_full[:, j * ca:(j + 1) * ca] for j in range(N_CHIPS)], axis=0)
    views = [g_wa4, big["wao"].reshape(N_CHIPS, dq, D), big["wkv"][None], big["wbi"][None],
             big["wbo"].reshape(N_CHIPS, dq, D)]
    rs_kinds = ["stack", "stack", "cols", "cols", "stack"]
    names = ["a_w_in", "a_w_out", "kv_w", "b_w_in", "b_w_out"]
    parts = _rs_pair_exchange(views)
    pairs = [_add_half(v, p, core, name="pair_sum_" + nm) for v, p, nm in zip(views, parts, names)]
    slots = _rs_chips(pairs, rs_kinds)
    chip_ix = chip.astype(jnp.int32).reshape(1)
    halves = [_sum_chips(q, t, chip_ix, kd, name="chip_sum_" + nm)
              for q, t, kd, nm in zip(pairs, slots, rs_kinds, names)]
    others = _rs_share(halves)

    w_big = [a_w_in[0], a_w_out[0], kv_w, b_w_in[0], b_w_out[0]]
    m_big = [m_a_w_in[0], m_a_w_out[0], m_kv_w, m_b_w_in[0], m_b_w_out[0]]
    v_big = [v_a_w_in[0], v_a_w_out[0], v_kv_w, v_b_w_in[0], v_b_w_out[0]]
    upd_big = [_adamw_halves(w, a, b, m, v, core, name="adamw_" + nm)
               for w, a, b, m, v, nm in zip(w_big, halves, others, m_big, v_big, names)]
    g_big = [u[0] for u in upd_big]

    order = ["conv_w", "conv_b", "head_g", "a_ln_g", "a_ln_b", "b_ln_g", "b_ln_b", "gate_b"]
    full_shapes = [(CONV_A, D), (1, D), (1, D), (1, D), (1, D), (1, D), (1, D), (1, 2 * nha)]
    n_rows = sum(-(-(s[0] * s[1]) // LANES) for s in full_shapes) + 1
    packed = _pack_rows([small[k] for k in order] + [loss_row], _round_up(n_rows, 8))
    total = _allreduce_small(packed)
    sums = dict(zip(order, _unpack_rows(total, full_shapes)))
    loss = total[n_rows - 1, 0]

    def mine(v):
        return lax.dynamic_slice_in_dim(v, chip * dq, dq, axis=1)

    g_small = [sums["gate_b"], mine(sums["conv_w"]), mine(sums["conv_b"]), mine(sums["head_g"]),
               mine(sums["a_ln_g"]), mine(sums["a_ln_b"]), sums["b_ln_g"], sums["b_ln_b"]]
    w_small = [a_gate_b, a_conv_w[0], a_conv_b, a_head_g, a_ln_g, a_ln_b, b_ln_g, b_ln_b]
    m_small = [m_a_gate_b, m_a_conv_w[0], m_a_conv_b, m_a_head_g, m_a_ln_g, m_a_ln_b, m_b_ln_g, m_b_ln_b]
    v_small = [v_a_gate_b, v_a_conv_w[0], v_a_conv_b, v_a_head_g, v_a_ln_g, v_a_ln_b, v_b_ln_g, v_b_ln_b]
    shp_small = [w.shape for w in w_small]
    rows_small = _round_up(sum(-(-(s[0] * s[1]) // LANES) for s in shp_small), 8)
    upd_small = _adamw(_pack_rows(w_small, rows_small), _pack_rows(g_small, rows_small),
                       _pack_rows(m_small, rows_small), _pack_rows(v_small, rows_small), name="adamw_small")
    d_small, mn_small, vn_small = (_unpack_rows(u, shp_small) for u in upd_small)

    def assemble(big5, small8):
        awi, awo, kvw, bwi, bwo = big5
        gb, cw, cb, hg, alg, alb, blg, blb = small8
        return [awi[None], gb, cw[None], cb, hg, awo[None], alg, alb, kvw, bwi[None], bwo[None], blg, blb]

    grads = assemble(g_big, g_small)
    deltas = assemble([u[1] for u in upd_big], d_small)
    new_m = assemble([u[2] for u in upd_big], mn_small)
    new_v = assemble([u[3] for u in upd_big], vn_small)
    return (loss, grad_x[None], *grads, *deltas, *new_m, *new_v)
```

```python
import functools

import jax
import jax.numpy as jnp
from jax import lax
from jax.experimental import pallas as pl
from jax.experimental.pallas import tpu as pltpu
from jax.experimental.pallas import tpu_sc as plsc

F32 = jnp.float32
BF16 = jnp.bfloat16

DEPTH = 2
ALPHA = (2.0 * DEPTH) ** 0.25
LN_EPS = 1e-5
DK_A = 128
DV_A = 256
DH_B = 128
SB_TQ = 256
SB_TK = 256
CONV_A = 4
ADAM_LR = 0.001
ADAM_B1 = 0.9
ADAM_B2 = 0.999
ADAM_EPS = 1e-08
ADAM_WD = 0.01
ADAM_STEP = 10
N_CHIPS = 4
LANES = 128
V7X_VMEM_BYTES = 64 * 1024 * 1024
VMEM_LIMIT = (V7X_VMEM_BYTES * 3) // 4
NEG_BIG = -1e30
MESH = pl.DeviceIdType.MESH
ANY = pl.BlockSpec(memory_space=pl.ANY)


def _params(*sem):
    return pltpu.CompilerParams(dimension_semantics=sem, vmem_limit_bytes=VMEM_LIMIT)


def _dot(a, b, dims):
    return lax.dot_general(a, b, (dims, ((), ())), preferred_element_type=F32)


def _dot_nn(a, b):
    return _dot(a, b, ((1,), (0,)))


def _dot_nt(a, b):
    return _dot(a, b, ((1,), (1,)))


def _dot_tn(a, b):
    return _dot(a, b, ((0,), (0,)))


def _split2(x):
    hi = x.astype(BF16)
    lo = (x - hi.astype(F32)).astype(BF16)
    return hi, lo


def _split3(x):
    hi = x.astype(BF16)
    r = x - hi.astype(F32)
    mid = r.astype(BF16)
    lo = (r - mid.astype(F32)).astype(BF16)
    return hi, mid, lo


def _mask_dot2(x, t01):
    hi, lo = _split2(x)
    return _dot_nn(hi, t01) + _dot_nn(lo, t01)


def _mask_dot3(x, t01):
    hi, mid, lo = _split3(x)
    return _dot_nn(hi, t01) + _dot_nn(mid, t01) + _dot_nn(lo, t01)


def _mask_dot3_left(t01, x):
    hi, mid, lo = _split3(x)
    return _dot_nn(t01, hi) + _dot_nn(t01, mid) + _dot_nn(t01, lo)


def _log_sigmoid(z):
    return jnp.minimum(z, 0.0) - jnp.log(1.0 + jnp.exp(-jnp.abs(z)))


def _sigmoid(z):
    return 1.0 / (1.0 + jnp.exp(-z))


def _mm(a, b, *, mode, M, N, K, out_dtype, name, tm=1024, tn=1024, tk=1024,
        a_off=(0, 0), b_off=(0, 0), acc_in=None, acc_scale=1.0, into=None, out_off=(0, 0)):
    tm, tn, tk = min(tm, M), min(tn, N), min(tk, K)
    assert M % tm == 0 and N % tn == 0 and K % tk == 0
    nk = K // tk
    ar, ac = a_off
    br, bc = b_off
    orow, ocol = out_off
    if mode in ("nn", "nt"):
        assert ar % tm == 0 and ac % tk == 0
        a_spec = pl.BlockSpec((tm, tk), lambda i, j, k: (i + ar // tm, k + ac // tk))
    else:
        assert ar % tk == 0 and ac % tm == 0
        a_spec = pl.BlockSpec((tk, tm), lambda i, j, k: (k + ar // tk, i + ac // tm))
    if mode == "nt":
        assert br % tn == 0 and bc % tk == 0
        b_spec = pl.BlockSpec((tn, tk), lambda i, j, k: (j + br // tn, k + bc // tk))
    else:
        assert br % tk == 0 and bc % tn == 0
        b_spec = pl.BlockSpec((tk, tn), lambda i, j, k: (k + br // tk, j + bc // tn))
    assert orow % tm == 0 and ocol % tn == 0
    o_spec = pl.BlockSpec((tm, tn), lambda i, j, k: (i + orow // tm, j + ocol // tn))
    dims = {"nn": ((1,), (0,)), "nt": ((1,), (1,)), "tn": ((0,), (0,))}[mode]
    inputs, in_specs = [a, b], [a_spec, b_spec]
    has_acc = acc_in is not None
    if has_acc:
        inputs.append(acc_in)
        in_specs.append(pl.BlockSpec((tm, tn), lambda i, j, k: (i, j)))
    aliases = {}
    if into is not None:
        inputs.append(into)
        in_specs.append(ANY)
        aliases = {len(inputs) - 1: 0}
        out_shape = jax.ShapeDtypeStruct(into.shape, into.dtype)
        assert into.dtype == out_dtype
    else:
        out_shape = jax.ShapeDtypeStruct((M, N), out_dtype)

    def body(*refs):
        a_ref, b_ref = refs[0], refs[1]
        acc_in_ref = refs[2] if has_acc else None
        n_in = len(inputs)
        o_ref = refs[n_in]

        def first():
            if has_acc:
                return acc_scale * acc_in_ref[...]
            return None

        if nk == 1:
            r = _dot(a_ref[...], b_ref[...], dims)
            f = first()
            if f is not None:
                r = r + f
            o_ref[...] = r.astype(o_ref.dtype)
        else:
            acc_ref = refs[n_in + 1]
            kk = pl.program_id(2)

            @pl.when(kk == 0)
            def _():
                f = first()
                acc_ref[...] = jnp.zeros_like(acc_ref) if f is None else f

            acc_ref[...] += _dot(a_ref[...], b_ref[...], dims)

            @pl.when(kk == nk - 1)
            def _():
                o_ref[...] = acc_ref[...].astype(o_ref.dtype)

    scratch = [] if nk == 1 else [pltpu.VMEM((tm, tn), F32)]
    return pl.pallas_call(
        body, name=name, out_shape=out_shape, grid=(M // tm, N // tn, nk),
        in_specs=in_specs, out_specs=o_spec, scratch_shapes=scratch,
        input_output_aliases=aliases,
        compiler_params=_params("parallel", "parallel", "arbitrary"),
    )(*inputs)


def _shift_down(x, d, row):
    if d == 0:
        return x
    return jnp.where(row >= d, pltpu.roll(x, d, 0), 0.0)


def _shift_up(x, d, row, n):
    if d == 0:
        return x
    return jnp.where(row < n - d, pltpu.roll(x, n - d, 0), 0.0)


def _conv_pre(x, w_ref, b_ref, row):
    c = b_ref[...] + w_ref[CONV_A - 1:CONV_A, :] * x
    for k in range(CONV_A - 1):
        c = c + w_ref[k:k + 1, :] * _shift_down(x, CONV_A - 1 - k, row)
    return c


def _conv_fwd(u, conv_w, conv_b, *, S, D, name):
    tc = 256
    nq_blocks = (D // 2) // tc

    def body(u_ref, w_ref, b_ref, o_ref):
        x = u_ref[...]
        row = lax.broadcasted_iota(jnp.int32, x.shape, 0)
        c = _conv_pre(x, w_ref, b_ref, row)
        scale = jnp.where(pl.program_id(0) >= nq_blocks, DK_A ** -0.5, 1.0).astype(F32)
        o_ref[...] = (c * _sigmoid(c) * scale).astype(BF16)

    return pl.pallas_call(
        body, name=name, out_shape=jax.ShapeDtypeStruct((S, D), BF16), grid=(D // tc,),
        in_specs=[pl.BlockSpec((S, tc), lambda j: (0, j)),
                  pl.BlockSpec((CONV_A, tc), lambda j: (0, j)),
                  pl.BlockSpec((1, tc), lambda j: (0, j))],
        out_specs=pl.BlockSpec((S, tc), lambda j: (0, j)),
        compiler_params=_params("parallel"),
    )(u, conv_w, conv_b)


def _conv_bwd(u, dq, dk, conv_w, conv_b, du, *, S, D, name):
    tc = 256
    nq_blocks = (D // 2) // tc

    def body(u_ref, dq_ref, dk_ref, w_ref, b_ref, du_in, du_ref, dw_ref, db_ref):
        del du_in
        x = u_ref[...]
        n = x.shape[0]
        row = lax.broadcasted_iota(jnp.int32, x.shape, 0)
        c = _conv_pre(x, w_ref, b_ref, row)
        is_k = pl.program_id(0) >= nq_blocks
        dy = jnp.where(is_k, dk_ref[...] * (DK_A ** -0.5), dq_ref[...])
        sg = _sigmoid(c)
        dc = dy * (sg * (1.0 + c * (1.0 - sg)))
        db_ref[...] = jnp.sum(dc, axis=0, keepdims=True)
        dx = w_ref[CONV_A - 1:CONV_A, :] * dc
        dw_ref[CONV_A - 1:CONV_A, :] = jnp.sum(dc * x, axis=0, keepdims=True)
        for k in range(CONV_A - 1):
            d = CONV_A - 1 - k
            dw_ref[k:k + 1, :] = jnp.sum(dc * _shift_down(x, d, row), axis=0, keepdims=True)
            dx = dx + w_ref[k:k + 1, :] * _shift_up(dc, d, row, n)
        du_ref[...] = dx.astype(BF16)

    half = lambda j: (0, j % nq_blocks)
    return pl.pallas_call(
        body, name=name,
        out_shape=(jax.ShapeDtypeStruct(du.shape, du.dtype),
                   jax.ShapeDtypeStruct((CONV_A, D), F32), jax.ShapeDtypeStruct((1, D), F32)),
        grid=(D // tc,),
        in_specs=[pl.BlockSpec((S, tc), lambda j: (0, j)),
                  pl.BlockSpec((S, tc), half), pl.BlockSpec((S, tc), half),
                  pl.BlockSpec((CONV_A, tc), lambda j: (0, j)),
                  pl.BlockSpec((1, tc), lambda j: (0, j)), ANY],
        out_specs=(pl.BlockSpec((S, tc), lambda j: (0, j)),
                   pl.BlockSpec((CONV_A, tc), lambda j: (0, j)),
                   pl.BlockSpec((1, tc), lambda j: (0, j))),
        input_output_aliases={5: 0},
        compiler_params=_params("parallel"),
    )(u, dq, dk, conv_w, conv_b, du)


def _tri(n, cmp):
    r = lax.broadcasted_iota(jnp.int32, (n, n), 0)
    c = lax.broadcasted_iota(jnp.int32, (n, n), 1)
    return r, c, cmp(r, c)


def _gates_fwd(gt, bias, *, nha, nb, name):
    half = nha * nb
    nb_shift = nb.bit_length() - 1
    assert nb == 1 << nb_shift

    def body(g_ref, b_ref, f_ref, brow_ref):
        ig = g_ref[0:half, :] + b_ref[0:half, :]
        fg = g_ref[half:2 * half, :] + b_ref[half:2 * half, :]
        lf = _log_sigmoid(fg)
        _, _, upper = _tri(LANES, lambda r, c: r <= c)
        cs = _mask_dot3(lf, upper.astype(BF16))
        tot = jnp.broadcast_to(cs[:, LANES - 1:LANES], cs.shape)
        r, c, _ = _tri(half, lambda r, c: r <= c)
        before = jnp.logical_and(r >> nb_shift == c >> nb_shift, c < r).astype(BF16)
        f = cs + _mask_dot3_left(before, tot)
        f_ref[...] = f
        brow_ref[...] = ig - f

    return pl.pallas_call(
        body, name=name,
        out_shape=(jax.ShapeDtypeStruct((half, LANES), F32), jax.ShapeDtypeStruct((half, LANES), F32)),
    )(gt, bias)


def _gates_bwd(rowsum, colsum, gt, bias, *, nha, nb, name):
    half = nha * nb
    nb_shift = nb.bit_length() - 1
    assert nb == 1 << nb_shift

    def body(rs_ref, cs_ref, g_ref, b_ref, dg_ref, tot_ref):
        col = cs_ref[...]
        df = rs_ref[...] - col
        _, _, lower = _tri(LANES, lambda r, c: r >= c)
        rc = _mask_dot3(df, lower.astype(BF16))
        tot = jnp.broadcast_to(rc[:, 0:1], rc.shape)
        r, c, _ = _tri(half, lambda r, c: r <= c)
        same = r >> nb_shift == c >> nb_shift
        after = jnp.logical_and(same, c > r).astype(BF16)
        dlf = rc + _mask_dot3_left(after, tot)
        fg = g_ref[half:2 * half, :] + b_ref[half:2 * half, :]
        dfg = dlf * _sigmoid(-fg)
        dg_ref[0:half, :] = col
        dg_ref[half:2 * half, :] = dfg
        grp = same.astype(BF16)
        ones = jnp.ones((LANES, LANES), BF16)
        tot_ref[0:half, :] = _mask_dot3_left(grp, _mask_dot3(col, ones))
        tot_ref[half:2 * half, :] = _mask_dot3_left(grp, _mask_dot3(dfg, ones))

    return pl.pallas_call(
        body, name=name,
        out_shape=(jax.ShapeDtypeStruct((2 * half, LANES), F32), jax.ShapeDtypeStruct((2 * half, LANES), F32)),
    )(rowsum, colsum, gt, bias)


def _mlstm_tile(q, k_ref, fcol, brow_ref, j, i, tq, m=None):
    off = pl.multiple_of(j * tq, tq)
    kj = k_ref[pl.ds(off, tq), :]
    s = _dot_nt(q, kj)
    row = lax.broadcasted_iota(jnp.int32, (tq, tq), 0)
    col = lax.broadcasted_iota(jnp.int32, (tq, tq), 1)
    valid = jnp.logical_or(col <= row, j < i)
    logd = jnp.where(valid, fcol + brow_ref[0, :, pl.ds(off, tq)], NEG_BIG)
    return off, kj, s, logd


def _mlstm_fwd(qk, u, fcol, brow, *, S, D, nha, name):
    tq = min(256, S)
    nq = S // tq
    kb, vb = (D // 2) // DK_A, D // DV_A

    def body(q_ref, k_ref, v_ref, fcol_ref, brow_ref, h_ref, m_ref, den_ref):
        i = pl.program_id(1)
        q = q_ref[...]
        fc = fcol_ref[0]

        def step(j, carry):
            acc, den, m = carry
            off, _, s, logd = _mlstm_tile(q, k_ref, fc, brow_ref, j, i, tq)
            m_new = jnp.maximum(m, jnp.max(logd, axis=1, keepdims=True))
            a = s * jnp.exp(logd - m_new)
            alpha = jnp.exp(m - m_new)
            vj = v_ref[pl.ds(off, tq), :].astype(BF16)
            acc = alpha * acc + _dot_nn(a.astype(BF16), vj)
            den = alpha * den + jnp.sum(a, axis=1, keepdims=True)
            return acc, den, m_new

        acc, den, m = lax.fori_loop(
            0, i + 1, step,
            (jnp.zeros((tq, DV_A), F32), jnp.zeros((tq, 1), F32), jnp.full((tq, 1), NEG_BIG, F32)))
        h_ref[...] = acc / jnp.maximum(jnp.abs(den), jnp.exp(-m))
        m_ref[0] = m
        den_ref[0] = den

    stat = pl.BlockSpec((1, tq, 1), lambda h, i: (h, i, 0))
    return pl.pallas_call(
        body, name=name,
        out_shape=(jax.ShapeDtypeStruct((S, D), F32), jax.ShapeDtypeStruct((nha, S, 1), F32),
                   jax.ShapeDtypeStruct((nha, S, 1), F32)),
        grid=(nha, nq),
        in_specs=[pl.BlockSpec((tq, DK_A), lambda h, i: (i, h)),
                  pl.BlockSpec((S, DK_A), lambda h, i: (0, kb + h)),
                  pl.BlockSpec((S, DV_A), lambda h, i: (0, vb + h)),
                  stat, pl.BlockSpec((1, 1, S), lambda h, i: (h, 0, 0))],
        out_specs=(pl.BlockSpec((tq, DV_A), lambda h, i: (i, h)), stat, stat),
        compiler_params=_params("parallel", "arbitrary"),
    )(qk, qk, u, fcol, brow)


def _mlstm_bwd(qk, u, fcol, brow, m, den, dh, h, du, *, S, D, nha, name):
    tq = min(256, S)
    nq = S // tq
    kb, vb = (D // 2) // DK_A, D // DV_A

    def body(q_ref, k_ref, v_ref, fcol_ref, brow_ref, m_ref, den_ref, dh_ref, h_ref, du_in,
             du_ref, dq_ref, dk_ref, rs_ref, cs_ref, dv_acc):
        del du_in
        i = pl.program_id(1)

        @pl.when(i == 0)
        def _():
            dk_ref[...] = jnp.zeros_like(dk_ref)
            cs_ref[...] = jnp.zeros_like(cs_ref)
            dv_acc[...] = jnp.zeros_like(dv_acc)

        q = q_ref[...]
        fc = fcol_ref[0]
        mm = m_ref[0]
        dn = den_ref[0]
        floor = jnp.exp(-mm)
        nrm = jnp.maximum(jnp.abs(dn), floor)
        dhv = dh_ref[...]
        dnum = dhv / nrm
        dnrm = -jnp.sum(dhv * h_ref[...], axis=1, keepdims=True) / nrm
        dden = jnp.where(jnp.abs(dn) > floor, jnp.where(dn > 0.0, dnrm, -dnrm), 0.0)
        dnum_b = dnum.astype(BF16)

        def step(j, carry):
            dq, rs = carry
            off, kj, s, logd = _mlstm_tile(q, k_ref, fc, brow_ref, j, i, tq)
            p = jnp.exp(logd - mm)
            a = s * p
            vj = v_ref[pl.ds(off, tq), :].astype(BF16)
            da = _dot_nt(dnum_b, vj) + dden
            dv_acc[pl.ds(off, tq), :] += _dot_tn(a.astype(BF16), dnum_b)
            dqk = (da * p).astype(BF16)
            dq = dq + _dot_nn(dqk, kj)
            dk_ref[pl.ds(off, tq), :] += _dot_tn(dqk, q)
            pm = da * a
            cs_ref[0, :, pl.ds(off, tq)] += jnp.sum(pm, axis=0, keepdims=True)
            rs = rs + jnp.sum(pm, axis=1, keepdims=True)
            return dq, rs

        dq, rs = lax.fori_loop(0, i + 1, step, (jnp.zeros((tq, DK_A), F32), jnp.zeros((tq, 1), F32)))
        dq_ref[...] = dq
        rs_ref[0] = rs

        @pl.when(i == nq - 1)
        def _():
            du_ref[...] = dv_acc[...].astype(BF16)

    stat = pl.BlockSpec((1, tq, 1), lambda h, i: (h, i, 0))
    rowv = pl.BlockSpec((1, 1, S), lambda h, i: (h, 0, 0))
    hblk = pl.BlockSpec((tq, DV_A), lambda h, i: (i, h))
    return pl.pallas_call(
        body, name=name,
        out_shape=(jax.ShapeDtypeStruct(du.shape, du.dtype),
                   jax.ShapeDtypeStruct((S, D // 2), F32), jax.ShapeDtypeStruct((S, D // 2), F32),
                   jax.ShapeDtypeStruct((nha, S, 1), F32), jax.ShapeDtypeStruct((nha, 1, S), F32)),
        grid=(nha, nq),
        in_specs=[pl.BlockSpec((tq, DK_A), lambda h, i: (i, h)),
                  pl.BlockSpec((S, DK_A), lambda h, i: (0, kb + h)),
                  pl.BlockSpec((S, DV_A), lambda h, i: (0, vb + h)),
                  stat, rowv, stat, stat, hblk, hblk, ANY],
        out_specs=(pl.BlockSpec((S, DV_A), lambda h, i: (0, vb + h)),
                   pl.BlockSpec((tq, DK_A), lambda h, i: (i, h)),
                   pl.BlockSpec((S, DK_A), lambda h, i: (0, h)),
                   stat, rowv),
        scratch_shapes=[pltpu.VMEM((S, DV_A), F32)],
        input_output_aliases={9: 0},
        compiler_params=_params("parallel", "arbitrary"),
    )(qk, qk, u, fcol, brow, m, den, dh, h, du)


def _head_norm(hh):
    mu = jnp.mean(hh, axis=1, keepdims=True)
    hc = hh - mu
    rstd = lax.rsqrt(jnp.mean(hc * hc, axis=1, keepdims=True) + LN_EPS)
    return hc * rstd, rstd


def _hgate_fwd(h, u, head_g, *, S, D, name):
    tm = min(256, S)
    nh = D // DV_A

    def body(h_ref, o_ref, z_ref, g_ref, out_ref):
        for hd in range(nh):
            sl = slice(hd * DV_A, (hd + 1) * DV_A)
            hn, _ = _head_norm(h_ref[:, sl])
            z = z_ref[:, sl]
            out_ref[:, sl] = (_sigmoid(o_ref[:, sl]) * (hn * g_ref[:, sl]) * (z * _sigmoid(z))).astype(BF16)

    return pl.pallas_call(
        body, name=name, out_shape=jax.ShapeDtypeStruct((S, D), BF16), grid=(S // tm,),
        in_specs=[pl.BlockSpec((tm, D), lambda i: (i, 0)), pl.BlockSpec((tm, D), lambda i: (i, 2)),
                  pl.BlockSpec((tm, D), lambda i: (i, 3)), pl.BlockSpec((1, D), lambda i: (0, 0))],
        out_specs=pl.BlockSpec((tm, D), lambda i: (i, 0)),
        compiler_params=_params("parallel"),
    )(h, u, u, head_g)


def _hgate_bwd(dhg, h, u, head_g, du, *, S, D, name):
    tm = min(256, S)
    nh = D // DV_A

    def body(dhg_ref, h_ref, o_ref, z_ref, g_ref, du_in, du_ref, dh_ref, dg_ref):
        del du_in

        @pl.when(pl.program_id(0) == 0)
        def _():
            dg_ref[...] = jnp.zeros_like(dg_ref)

        for hd in range(nh):
            sl = slice(hd * DV_A, (hd + 1) * DV_A)
            hn, rstd = _head_norm(h_ref[:, sl])
            o, z, g, d = o_ref[:, sl], z_ref[:, sl], g_ref[:, sl], dhg_ref[:, sl]
            so, sz = _sigmoid(o), _sigmoid(z)
            silu_z = z * sz
            hng = hn * g
            du_ref[:, sl] = (d * hng * silu_z * so * (1.0 - so)).astype(BF16)
            du_ref[:, D + hd * DV_A:D + (hd + 1) * DV_A] = (
                d * so * hng * (sz * (1.0 + z * (1.0 - sz)))).astype(BF16)
            t = d * so * silu_z
            dg_ref[:, sl] += jnp.sum(t * hn, axis=0, keepdims=True)
            dhn = t * g
            dh_ref[:, sl] = rstd * (dhn - jnp.mean(dhn, axis=1, keepdims=True)
                                    - hn * jnp.mean(dhn * hn, axis=1, keepdims=True))

    row = lambda i: (i, 0)
    return pl.pallas_call(
        body, name=name,
        out_shape=(jax.ShapeDtypeStruct(du.shape, du.dtype), jax.ShapeDtypeStruct((S, D), F32),
                   jax.ShapeDtypeStruct((1, D), F32)),
        grid=(S // tm,),
        in_specs=[pl.BlockSpec((tm, D), row), pl.BlockSpec((tm, D), row),
                  pl.BlockSpec((tm, D), lambda i: (i, 2)), pl.BlockSpec((tm, D), lambda i: (i, 3)),
                  pl.BlockSpec((1, D), lambda i: (0, 0)), ANY],
        out_specs=(pl.BlockSpec((tm, 2 * D), lambda i: (i, 1)), pl.BlockSpec((tm, D), row),
                   pl.BlockSpec((1, D), lambda i: (0, 0))),
        input_output_aliases={5: 0},
        compiler_params=_params("arbitrary"),
    )(dhg, h, u, u, head_g, du)


def _ln_stats(r):
    mu = jnp.mean(r, axis=1, keepdims=True)
    xc = r - mu
    rstd = lax.rsqrt(jnp.mean(xc * xc, axis=1, keepdims=True) + LN_EPS)
    return xc * rstd, rstd


def _ln_back(dxhat, xhat, rstd):
    return rstd * (dxhat - jnp.mean(dxhat, axis=1, keepdims=True)
                   - xhat * jnp.mean(dxhat * xhat, axis=1, keepdims=True))


def _ln_fwd(x, y, g, b, *, S, D, name):
    tm = min(256, S)

    def body(x_ref, y_ref, g_ref, b_ref, o_ref, ob_ref):
        xhat, _ = _ln_stats(ALPHA * x_ref[...] + y_ref[...])
        o = xhat * g_ref[...] + b_ref[...]
        o_ref[...] = o
        ob_ref[...] = o.astype(BF16)

    row = lambda i: (i, 0)
    vec = pl.BlockSpec((1, D), lambda i: (0, 0))
    return pl.pallas_call(
        body, name=name,
        out_shape=(jax.ShapeDtypeStruct((S, D), F32), jax.ShapeDtypeStruct((S, D), BF16)),
        grid=(S // tm,),
        in_specs=[pl.BlockSpec((tm, D), row), pl.BlockSpec((tm, D), row), vec, vec],
        out_specs=(pl.BlockSpec((tm, D), row), pl.BlockSpec((tm, D), row)),
        compiler_params=_params("parallel"),
    )(x, y, g, b)


def _ln_loss_bwd(x1, y2, target, g, b, *, S, D, name):
    tm = min(256, S)

    def body(x_ref, y_ref, t_ref, g_ref, b_ref, dr_ref, drb_ref, dg_ref, db_ref, loss_ref):
        @pl.when(pl.program_id(0) == 0)
        def _():
            dg_ref[...] = jnp.zeros_like(dg_ref)
            db_ref[...] = jnp.zeros_like(db_ref)
            loss_ref[...] = jnp.zeros_like(loss_ref)

        xhat, rstd = _ln_stats(ALPHA * x_ref[...] + y_ref[...])
        diff = xhat * g_ref[...] + b_ref[...] - t_ref[...]
        loss_ref[...] += (0.5 / D) * jnp.sum(diff * diff)
        dx2 = diff * (1.0 / D)
        dg_ref[...] += jnp.sum(dx2 * xhat, axis=0, keepdims=True)
        db_ref[...] += jnp.sum(dx2, axis=0, keepdims=True)
        dr = _ln_back(dx2 * g_ref[...], xhat, rstd)
        dr_ref[...] = dr
        drb_ref[...] = dr.astype(BF16)

    row = lambda i: (i, 0)
    vec = pl.BlockSpec((1, D), lambda i: (0, 0))
    return pl.pallas_call(
        body, name=name,
        out_shape=(jax.ShapeDtypeStruct((S, D), F32), jax.ShapeDtypeStruct((S, D), BF16),
                   jax.ShapeDtypeStruct((1, D), F32), jax.ShapeDtypeStruct((1, D), F32),
                   jax.ShapeDtypeStruct((1, LANES), F32)),
        grid=(S // tm,),
        in_specs=[pl.BlockSpec((tm, D), row)] * 3 + [vec, vec],
        out_specs=(pl.BlockSpec((tm, D), row), pl.BlockSpec((tm, D), row), vec, vec,
                   pl.BlockSpec((1, LANES), lambda i: (0, 0))),
        compiler_params=_params("arbitrary"),
    )(x1, y2, target, g, b)


def _ln_bwd(x, y, g, dout, *, S, D, name):
    tm = min(256, S)

    def body(x_ref, y_ref, g_ref, d_ref, dr_ref, drb_ref, dg_ref, db_ref):
        @pl.when(pl.program_id(0) == 0)
        def _():
            dg_ref[...] = jnp.zeros_like(dg_ref)
            db_ref[...] = jnp.zeros_like(db_ref)

        xhat, rstd = _ln_stats(ALPHA * x_ref[...] + y_ref[...])
        d = d_ref[...]
        dg_ref[...] += jnp.sum(d * xhat, axis=0, keepdims=True)
        db_ref[...] += jnp.sum(d, axis=0, keepdims=True)
        dr = _ln_back(d * g_ref[...], xhat, rstd)
        dr_ref[...] = dr
        drb_ref[...] = dr.astype(BF16)

    row = lambda i: (i, 0)
    vec = pl.BlockSpec((1, D), lambda i: (0, 0))
    return pl.pallas_call(
        body, name=name,
        out_shape=(jax.ShapeDtypeStruct((S, D), F32), jax.ShapeDtypeStruct((S, D), BF16),
                   jax.ShapeDtypeStruct((1, D), F32), jax.ShapeDtypeStruct((1, D), F32)),
        grid=(S // tm,),
        in_specs=[pl.BlockSpec((tm, D), row), pl.BlockSpec((tm, D), row), vec, pl.BlockSpec((tm, D), row)],
        out_specs=(pl.BlockSpec((tm, D), row), pl.BlockSpec((tm, D), row), vec, vec),
        compiler_params=_params("arbitrary"),
    )(x, y, g, dout)


def _sb_scores(q, kj, j, i, tq, tk):
    z = _dot_nt(q, kj) * (DH_B ** -0.5)
    row = lax.broadcasted_iota(jnp.int32, (tq, tk), 0) + i * tq
    col = lax.broadcasted_iota(jnp.int32, (tq, tk), 1) + j * tk
    valid = col < row
    ls = _log_sigmoid(z)
    lneg = jnp.where(valid, ls - z, 0.0)
    return valid, ls, lneg


def _sb_blocks(i, tq, tk):
    return ((i + 1) * tq + tk - 1) // tk


def _sb_fwd(q2, kv, *, S, D, name, tq=SB_TQ, tk=SB_TK):
    tq, tk = min(tq, S), min(tk, S)
    nq = S // tq
    nh = D // DH_B

    def body(q_ref, k_ref, v_ref, o_ref, tot_ref):
        i = pl.program_id(1)
        q = q_ref[...]
        _, _, after = _tri(tk, lambda r, c: r > c)
        tri_after = after.astype(BF16)
        nkv = _sb_blocks(i, tq, tk)

        def step(jj, carry):
            acc, cr = carry
            j = nkv - 1 - jj
            off = pl.multiple_of(j * tk, tk)
            valid, ls, lneg = _sb_scores(q, k_ref[pl.ds(off, tk), :], j, i, tq, tk)
            between = cr + _mask_dot2(lneg, tri_after)
            a = jnp.where(valid, jnp.exp(ls + between), 0.0)
            acc = acc + _dot_nn(a.astype(BF16), v_ref[pl.ds(off, tk), :])
            return acc, cr + jnp.sum(lneg, axis=1, keepdims=True)

        acc, cr = lax.fori_loop(0, nkv, step, (jnp.zeros((tq, DH_B), F32), jnp.zeros((tq, 1), F32)))
        o_ref[...] = acc
        tot_ref[0] = cr

    return pl.pallas_call(
        body, name=name,
        out_shape=(jax.ShapeDtypeStruct((S, D), F32), jax.ShapeDtypeStruct((nh, S, 1), F32)), grid=(nh, nq),
        in_specs=[pl.BlockSpec((tq, DH_B), lambda h, i: (i, h)),
                  pl.BlockSpec((S, DH_B), lambda h, i: (0, h)),
                  pl.BlockSpec((S, DH_B), lambda h, i: (0, nh + h))],
        out_specs=(pl.BlockSpec((tq, DH_B), lambda h, i: (i, h)), pl.BlockSpec((1, tq, 1), lambda h, i: (h, i, 0))),
        compiler_params=_params("parallel", "arbitrary"),
    )(q2, kv, kv)


def _sb_bwd(q2, kv, datt, tot, *, S, D, name, tq=SB_TQ, tk=SB_TK):
    tq, tk = min(tq, S), min(tk, S)
    nq = S // tq
    nh = D // DH_B

    def body(q_ref, k_ref, v_ref, do_ref, tot_ref, dq_ref, dk_ref, dv_ref, dk_acc, dv_acc):
        i = pl.program_id(1)

        @pl.when(i == 0)
        def _():
            dk_acc[...] = jnp.zeros_like(dk_acc)
            dv_acc[...] = jnp.zeros_like(dv_acc)

        q = q_ref[...]
        do_b = do_ref[...].astype(BF16)
        _, _, after = _tri(tk, lambda r, c: r > c)
        tri_after = after.astype(BF16)
        _, _, before = _tri(tk, lambda r, c: r < c)
        tri_before = before.astype(BF16)

        def step(j, carry):
            dq, rest, cg = carry
            off = pl.multiple_of(j * tk, tk)
            kj = k_ref[pl.ds(off, tk), :]
            valid, ls, lneg = _sb_scores(q, kj, j, i, tq, tk)
            rest = rest - jnp.sum(lneg, axis=1, keepdims=True)
            between = rest + _mask_dot2(lneg, tri_after)
            a = jnp.where(valid, jnp.exp(ls + between), 0.0)
            g = _dot_nt(do_b, v_ref[pl.ds(off, tk), :]) * a
            dv_acc[pl.ds(off, tk), :] += _dot_tn(a.astype(BF16), do_b)
            e = cg + _mask_dot2(g, tri_before)
            dz = jnp.where(valid, g * jnp.exp(lneg) - e * jnp.exp(ls), 0.0) * (DH_B ** -0.5)
            dz_b = dz.astype(BF16)
            dq = dq + _dot_nn(dz_b, kj)
            dk_acc[pl.ds(off, tk), :] += _dot_tn(dz_b, q)
            return dq, rest, cg + jnp.sum(g, axis=1, keepdims=True)

        dq, _, _ = lax.fori_loop(0, _sb_blocks(i, tq, tk), step,
                                 (jnp.zeros((tq, DH_B), F32), tot_ref[0], jnp.zeros((tq, 1), F32)))
        dq_ref[...] = dq.astype(BF16)

        @pl.when(i == nq - 1)
        def _():
            dk_ref[...] = dk_acc[...].astype(BF16)
            dv_ref[...] = dv_acc[...].astype(BF16)

    blk = pl.BlockSpec((tq, DH_B), lambda h, i: (i, h))
    return pl.pallas_call(
        body, name=name,
        out_shape=(jax.ShapeDtypeStruct((S, D), BF16), jax.ShapeDtypeStruct((S, D), BF16),
                   jax.ShapeDtypeStruct((S, D), BF16)),
        grid=(nh, nq),
        in_specs=[blk, pl.BlockSpec((S, DH_B), lambda h, i: (0, h)),
                  pl.BlockSpec((S, DH_B), lambda h, i: (0, nh + h)), blk,
                  pl.BlockSpec((1, tq, 1), lambda h, i: (h, i, 0))],
        out_specs=(blk, pl.BlockSpec((S, DH_B), lambda h, i: (0, h)),
                   pl.BlockSpec((S, DH_B), lambda h, i: (0, h))),
        scratch_shapes=[pltpu.VMEM((S, DH_B), F32), pltpu.VMEM((S, DH_B), F32)],
        compiler_params=_params("parallel", "arbitrary"),
    )(q2, kv, kv, datt, tot)


def _bgate_fwd(att, z2, *, S, D, name):
    tm = min(256, S)

    def body(a_ref, z_ref, o_ref):
        z = z_ref[...]
        o_ref[...] = (a_ref[...] * (z * _sigmoid(z))).astype(BF16)

    row = lambda i: (i, 0)
    return pl.pallas_call(
        body, name=name, out_shape=jax.ShapeDtypeStruct((S, D), BF16), grid=(S // tm,),
        in_specs=[pl.BlockSpec((tm, D), row)] * 2, out_specs=pl.BlockSpec((tm, D), row),
        compiler_params=_params("parallel"),
    )(att, z2)


def _bgate_bwd(dhb, att, z2, *, S, D, name):
    tm = min(256, S)

    def body(d_ref, a_ref, z_ref, da_ref, dz_ref):
        z, d = z_ref[...], d_ref[...]
        sz = _sigmoid(z)
        da_ref[...] = d * (z * sz)
        dz_ref[...] = (d * a_ref[...] * (sz * (1.0 + z * (1.0 - sz)))).astype(BF16)

    row = lambda i: (i, 0)
    return pl.pallas_call(
        body, name=name,
        out_shape=(jax.ShapeDtypeStruct((S, D), F32), jax.ShapeDtypeStruct((S, D), BF16)),
        grid=(S // tm,),
        in_specs=[pl.BlockSpec((tm, D), row)] * 3,
        out_specs=(pl.BlockSpec((tm, D), row), pl.BlockSpec((tm, D), row)),
        compiler_params=_params("parallel"),
    )(dhb, att, z2)


def _adamw_math(w, g, m, v):
    mn = ADAM_B1 * m + (1.0 - ADAM_B1) * g
    vn = ADAM_B2 * v + (1.0 - ADAM_B2) * (g * g)
    m_hat = mn / (1.0 - ADAM_B1 ** ADAM_STEP)
    v_hat = vn / (1.0 - ADAM_B2 ** ADAM_STEP)
    return -ADAM_LR * (m_hat / (jnp.sqrt(v_hat) + ADAM_EPS) + ADAM_WD * w), mn, vn


def _adamw(w, g, m, v, *, name, tr=128):
    rows, cols = w.shape
    tr = min(tr, rows)
    assert rows % tr == 0

    def body(w_ref, g_ref, m_ref, v_ref, d_ref, mo_ref, vo_ref):
        d, mn, vn = _adamw_math(w_ref[...], g_ref[...], m_ref[...], v_ref[...])
        d_ref[...] = d
        mo_ref[...] = mn
        vo_ref[...] = vn

    blk = pl.BlockSpec((tr, cols), lambda i: (i, 0))
    sd = jax.ShapeDtypeStruct((rows, cols), F32)
    return pl.pallas_call(
        body, name=name, out_shape=(sd, sd, sd), grid=(rows // tr,),
        in_specs=[blk] * 4, out_specs=(blk, blk, blk),
        compiler_params=_params("parallel"),
    )(w, g, m, v)


def _local_step(x, target, wa, wg, wao, wbi, wkv, wbo, gate_b, conv_w, conv_b, head_g,
                a_ln_g, a_ln_b, b_ln_g, b_ln_b, *, S, D, nha, rs):
    nb = S // LANES
    kw = dict(S=S, D=D)
    xb = x.astype(BF16)
    u = _mm(xb, wa, mode="nn", M=S, N=4 * D, K=D, out_dtype=F32, name="a_in")
    ug = _mm(xb, wg, mode="nn", M=S, N=LANES, K=D, out_dtype=F32, name="a_in_gates")
    qk = _conv_fwd(u, conv_w, conv_b, name="conv_fwd", **kw)
    gt = ug[:, :2 * nha].T.reshape(2 * nha * nb, LANES)
    gbias = jnp.repeat(gate_b.reshape(2 * nha), nb).reshape(2 * nha * nb, 1)
    fcs, brow = _gates_fwd(gt, gbias, nha=nha, nb=nb, name="gates_fwd")
    fcol = fcs.reshape(nha, S, 1)
    brow = brow.reshape(nha, 1, S)
    h, m, den = _mlstm_fwd(qk, u, fcol, brow, nha=nha, name="mlstm_fwd", **kw)
    hg = _hgate_fwd(h, u, head_g, name="hgate_fwd", **kw)
    y = _mm(hg, wao, mode="nn", M=S, N=D, K=D, out_dtype=F32, name="a_out")
    x1, x1b = _ln_fwd(x, y, a_ln_g, a_ln_b, name="ln_a_fwd", **kw)
    q2 = _mm(x1b, wbi, mode="nn", M=S, N=D, K=D, out_dtype=BF16, name="b_in_q")
    z2 = _mm(x1b, wbi, mode="nn", M=S, N=D, K=D, out_dtype=F32, name="b_in_z", b_off=(0, D))
    kv = _mm(x1b, wkv, mode="nn", M=S, N=2 * D, K=D, out_dtype=BF16, name="b_kv")
    att, sb_tot = _sb_fwd(q2, kv, name="sb_fwd", **kw)
    hb = _bgate_fwd(att, z2, name="bgate_fwd", **kw)
    y2 = _mm(hb, wbo, mode="nn", M=S, N=D, K=D, out_dtype=F32, name="b_out")
    dr2, dr2b, d_bln_g, d_bln_b, loss = _ln_loss_bwd(x1, y2, target, b_ln_g, b_ln_b, name="ln_b_loss", **kw)
    g_wbo = _mm(hb, dr2b, mode="tn", M=D, N=D, K=S, out_dtype=BF16, name="g_b_out")
    dhb = _mm(dr2b, wbo, mode="nt", M=S, N=D, K=D, out_dtype=F32, name="d_b_out")
    datt, dz2 = _bgate_bwd(dhb, att, z2, name="bgate_bwd", **kw)
    dq2, dk2, dv2 = _sb_bwd(q2, kv, datt, sb_tot, name="sb_bwd", **kw)
    g_wbi = _mm(x1b, dq2, mode="tn", M=D, N=D, K=S, out_dtype=BF16, name="g_b_in_q",
                into=lax.empty((D, 2 * D), BF16))
    g_wbi = _mm(x1b, dz2, mode="tn", M=D, N=D, K=S, out_dtype=BF16, name="g_b_in_z", into=g_wbi, out_off=(0, D))
    g_wkv = _mm(x1b, dk2, mode="tn", M=D, N=D, K=S, out_dtype=BF16, name="g_kv_k",
                into=lax.empty((D, 2 * D), BF16))
    g_wkv = _mm(x1b, dv2, mode="tn", M=D, N=D, K=S, out_dtype=BF16, name="g_kv_v", into=g_wkv, out_off=(0, D))
    dq2 = rs.begin("b", dict(wbo=g_wbo, wbi=g_wbi, wkv=g_wkv), dq2)
    dx1 = _mm(dq2, wbi, mode="nt", M=S, N=D, K=D, out_dtype=F32, name="d_b_in_q", acc_in=dr2, acc_scale=ALPHA)
    dx1 = _mm(dz2, wbi, mode="nt", M=S, N=D, K=D, out_dtype=F32, name="d_b_in_z", b_off=(0, D), acc_in=dx1)
    dx1 = _mm(dk2, wkv, mode="nt", M=S, N=D, K=D, out_dtype=F32, name="d_kv_k", acc_in=dx1)
    dx1 = _mm(dv2, wkv, mode="nt", M=S, N=D, K=D, out_dtype=F32, name="d_kv_v", b_off=(0, D), acc_in=dx1)
    dx1 = rs.pair("b", dx1)
    dr, drb, d_aln_g, d_aln_b = _ln_bwd(x, y, a_ln_g, dx1, name="ln_a_bwd", **kw)
    g_wao = _mm(hg, drb, mode="tn", M=D, N=D, K=S, out_dtype=BF16, name="g_a_out")
    drb = rs.begin("a", dict(wao=g_wao), drb)
    dhg = _mm(drb, wao, mode="nt", M=S, N=D, K=D, out_dtype=F32, name="d_a_out")
    du = lax.empty((S, 4 * D), BF16)
    du, dh, d_head_g = _hgate_bwd(dhg, h, u, head_g, du, name="hgate_bwd", **kw)
    dh = rs.pair("a", dh)
    du, dq, dk, rowsum, colsum = _mlstm_bwd(qk, u, fcol, brow, m, den, dh, h, du, nha=nha, name="mlstm_bwd", **kw)
    dq = rs.chip("a", rs.chip("b", dq))
    dgt, dgtot = _gates_bwd(rowsum.reshape(nha * nb, LANES), colsum.reshape(nha * nb, LANES), gt, gbias,
                            nha=nha, nb=nb, name="gates_bwd")
    d_gate_b = dgtot[::nb, 0].reshape(1, 2 * nha)
    dgp = jnp.pad(dgt.reshape(2 * nha, S).T, ((0, 0), (0, LANES - 2 * nha))).astype(BF16)
    du, d_conv_w, d_conv_b = _conv_bwd(u, dq, dk, conv_w, conv_b, du, name="conv_bwd", **kw)
    g_wa = _mm(xb, du, mode="tn", M=D, N=4 * D, K=S, out_dtype=BF16, name="g_a_in")
    g_wg = _mm(xb, dgp, mode="tn", M=D, N=LANES, K=S, out_dtype=BF16, name="g_a_in_gates")
    du = rs.begin("w", dict(wa=g_wa, wg=g_wg), du)
    du = rs.pair("w", rs.finish("a", rs.finish("b", du)))
    dx = _mm(du, wa, mode="nt", M=S, N=D, K=4 * D, out_dtype=F32, name="d_a_in", acc_in=dr, acc_scale=ALPHA)
    dx = _mm(dgp, wg, mode="nt", M=S, N=D, K=LANES, out_dtype=F32, name="d_a_in_gates", acc_in=dx)
    small = dict(gate_b=d_gate_b, conv_w=d_conv_w, conv_b=d_conv_b, head_g=d_head_g,
                 a_ln_g=d_aln_g, a_ln_b=d_aln_b, b_ln_g=d_bln_g, b_ln_b=d_bln_b)
    return loss, dx, small


def _coords():
    return lax.axis_index("x"), lax.axis_index("y"), lax.axis_index("c")


def _other_chips(x, y):
    return [(1 - x, y), (x, 1 - y), (1 - x, 1 - y)]


def _rows(ref, start, size):
    return ref.at[pl.ds(pl.multiple_of(start, size), size), :]


def _window(kind, ref, shard_shape, j, hf=None):
    r, cw = shard_shape
    row0, nr = (0, r) if hf is None else (hf * (r // 2), r // 2)
    if kind == "stack":
        return ref.at[j, pl.ds(pl.multiple_of(row0, nr), nr), :]
    if kind == "rows":
        return ref.at[pl.ds(pl.multiple_of(j * r + row0, nr), nr), :]
    assert kind == "cols"
    return ref.at[pl.ds(pl.multiple_of(row0, nr), nr), pl.ds(pl.multiple_of(j * cw, cw), cw)]


def _gathered_shape(kind, shard_shape):
    r, cw = shard_shape
    return {"stack": (N_CHIPS, r, cw), "rows": (N_CHIPS * r, cw), "cols": (r, N_CHIPS * cw)}[kind]


def _remote(src, dst, send_sems, recv_sems, k, to):
    return pltpu.make_async_remote_copy(src_ref=src, dst_ref=dst, send_sem=send_sems.at[k],
                                        recv_sem=recv_sems.at[k], device_id=to, device_id_type=MESH)


def _allgather_copies(s_refs, sm_ref, g_refs, smg_ref, send_sems, recv_sems, kinds, shapes):
    n = len(s_refs)
    x, y, c = _coords()
    me, sibling, chips = 2 * x + y, (x, y, 1 - c), _other_chips(x, y)
    ids = [2 * cx + cy for cx, cy in chips]
    own_base = 6 * n + 3

    def small_win(j):
        w = sm_ref.shape[1]
        return smg_ref.at[:, pl.ds(pl.multiple_of(j * w, w), w)]

    first = [_remote(s_refs[a], _window(kinds[a], g_refs[a], shapes[a], me), send_sems, recv_sems,
                     own_base + a, sibling) for a in range(n)]
    if sm_ref is not None:
        first.append(_remote(sm_ref, small_win(me), send_sems, recv_sems, own_base + n, sibling))
    for a in range(n):
        src = _rows(s_refs[a], c * (shapes[a][0] // 2), shapes[a][0] // 2)
        dst = _window(kinds[a], g_refs[a], shapes[a], me, c)
        first += [_remote(src, dst, send_sems, recv_sems, 3 * a + k, (*chip, c)) for k, chip in enumerate(chips)]
    if sm_ref is not None:
        first += [_remote(sm_ref, small_win(me), send_sems, recv_sems, 6 * n + k, (*chip, c))
                  for k, chip in enumerate(chips)]
    for cp in first:
        cp.start()
    passed = []
    for a in range(n):
        for k in range(3):
            win = _window(kinds[a], g_refs[a], shapes[a], ids[k], c)
            _remote(win, win, send_sems, recv_sems, 3 * a + k, sibling).wait_recv()
            fwd = _remote(win, win, send_sems, recv_sems, 3 * n + 3 * a + k, sibling)
            fwd.start()
            passed.append(fwd)
    if sm_ref is not None:
        for k in range(3):
            win = small_win(ids[k])
            _remote(win, win, send_sems, recv_sems, 6 * n + k, sibling).wait_recv()
    for a in range(n):
        for k in range(3):
            win = _window(kinds[a], g_refs[a], shapes[a], ids[k], 1 - c)
            _remote(win, win, send_sems, recv_sems, 3 * n + 3 * a + k, sibling).wait_recv()
    for a in range(n):
        win = _window(kinds[a], g_refs[a], shapes[a], me)
        _remote(win, win, send_sems, recv_sems, own_base + a, sibling).wait_recv()
    if sm_ref is not None:
        _remote(small_win(me), small_win(me), send_sems, recv_sems, own_base + n, sibling).wait_recv()
    for cp in first + passed:
        cp.wait_send()


def _allgather_weights(shards, kinds, small):
    n = len(shards)
    shapes = [s.shape for s in shards]

    def body(*refs):
        _allgather_copies(refs[:n], refs[n], refs[n + 1:2 * n + 1], refs[2 * n + 1], refs[2 * n + 2],
                          refs[2 * n + 3], kinds, shapes)

    out_shape = [jax.ShapeDtypeStruct(_gathered_shape(kinds[a], shapes[a]), shards[a].dtype) for a in range(n)]
    out_shape.append(jax.ShapeDtypeStruct((small.shape[0], N_CHIPS * small.shape[1]), small.dtype))
    nsem = 7 * n + 4
    return pl.pallas_call(
        body, name="allgather_weights", out_shape=tuple(out_shape),
        in_specs=[ANY] * (n + 1), out_specs=tuple([ANY] * (n + 1)),
        scratch_shapes=[pltpu.SemaphoreType.DMA((nsem,)), pltpu.SemaphoreType.DMA((nsem,))],
    )(*shards, small)


def _peer_handshake(peers):
    barrier = pltpu.get_barrier_semaphore()
    for peer in peers:
        pl.semaphore_signal(barrier, inc=1, device_id=peer, device_id_type=MESH)
    pl.semaphore_wait(barrier, len(peers))


def _behind(copies, inputs, out_structs, nsem, peers, *, name, collective_id):
    hbm = pltpu.MemorySpace.HBM
    in_refs = [jax.new_ref(a, memory_space=hbm) for a in inputs]
    out_refs = [jax.empty_ref(st, memory_space=hbm) for st in out_structs]

    @pl.kernel(mesh=plsc.ScalarSubcoreMesh(axis_name="sequencer", num_cores=1), name=name,
               scratch_types=(pltpu.SemaphoreType.DMA((nsem,)), pltpu.SemaphoreType.DMA((nsem,))),
               compiler_params=pltpu.CompilerParams(collective_id=collective_id))
    def launch(send_sems, recv_sems):
        _peer_handshake(peers(*_coords()))
        copies(in_refs, out_refs, send_sems, recv_sems)

    launch()
    return [r[...] for r in out_refs]


def _sibling(x, y, c):
    return [(x, y, 1 - c)]


def _same_core_of_other_chips(x, y, c):
    return [(cx, cy, c) for cx, cy in _other_chips(x, y)]


def _sibling_and_other_chips(x, y, c):
    return _sibling(x, y, c) + _same_core_of_other_chips(x, y, c)


ID_ALL_PEERS, ID_SIBLING, ID_CHIPS = 0, 1, 2


def _allgather_behind(shards, kinds, *, name):
    n = len(shards)
    shapes = [sh.shape for sh in shards]

    def copies(s_refs, g_refs, send_sems, recv_sems):
        _allgather_copies(s_refs, None, g_refs, None, send_sems, recv_sems, kinds, shapes)

    outs = [jax.ShapeDtypeStruct(_gathered_shape(kinds[a], shapes[a]), shards[a].dtype) for a in range(n)]
    return _behind(copies, shards, outs, 7 * n + 4, _sibling_and_other_chips, name=name, collective_id=ID_ALL_PEERS)


def _rs_pair_exchange(views, *, name):
    n = len(views)

    def copies(g_refs, p_refs, send_sems, recv_sems):
        x, y, c = _coords()
        cps = []
        for a in range(n):
            r2 = views[a].shape[1] // 2
            src = g_refs[a].at[:, pl.ds(pl.multiple_of((1 - c) * r2, r2), r2), :]
            cps.append(_remote(src, p_refs[a], send_sems, recv_sems, a, (x, y, 1 - c)))
        for cp in cps:
            cp.start()
        for cp in cps:
            cp.wait()

    outs = [jax.ShapeDtypeStruct((v.shape[0], v.shape[1] // 2, v.shape[2]), v.dtype) for v in views]
    return _behind(copies, views, outs, n, _sibling, name=name, collective_id=ID_SIBLING)


def _add_half(view, part, core, *, name):
    nch, r, cols = view.shape
    r2 = r // 2
    tr = min(128, r2)
    nt = r2 // tr

    def body(c_ref, g_ref, p_ref, o_ref):
        del c_ref
        o_ref[...] = (g_ref[...].astype(F32) + p_ref[...].astype(F32)).astype(o_ref.dtype)

    return pl.pallas_call(
        body, name=name, out_shape=jax.ShapeDtypeStruct((nch, r2, cols), view.dtype),
        grid_spec=pltpu.PrefetchScalarGridSpec(
            num_scalar_prefetch=1, grid=(nch, nt),
            in_specs=[pl.BlockSpec((1, tr, cols), lambda ch, i, c_ref: (ch, c_ref[0] * nt + i, 0)),
                      pl.BlockSpec((1, tr, cols), lambda ch, i, c_ref: (ch, i, 0))],
            out_specs=pl.BlockSpec((1, tr, cols), lambda ch, i, c_ref: (ch, i, 0))),
        compiler_params=_params("parallel", "parallel"),
    )(core, view, part)


def _chunk(kind, ref, j, cw):
    if kind == "cols":
        return ref.at[0, :, pl.ds(pl.multiple_of(j * cw, cw), cw)]
    return ref.at[j]


def _rs_chips(pairs, kinds, *, name):
    n = len(pairs)
    half_shapes = []
    for a in range(n):
        nch, r2, cols = pairs[a].shape
        half_shapes.append((r2, cols // N_CHIPS) if kinds[a] == "cols" else (r2, cols))

    def copies(q_refs, t_refs, send_sems, recv_sems):
        x, y, c = _coords()
        chips = _other_chips(x, y)
        sends = []
        for a in range(n):
            for k, (cx, cy) in enumerate(chips):
                src = _chunk(kinds[a], q_refs[a], 2 * cx + cy, half_shapes[a][1])
                sends.append(_remote(src, t_refs[a].at[k], send_sems, recv_sems, 3 * a + k, (cx, cy, c)))
        for cp in sends:
            cp.start()
        for a in range(n):
            for k in range(3):
                slot = t_refs[a].at[k]
                _remote(slot, slot, send_sems, recv_sems, 3 * a + k, (x, y, c)).wait_recv()
        for cp in sends:
            cp.wait_send()

    outs = [jax.ShapeDtypeStruct((3, *half_shapes[a]), pairs[a].dtype) for a in range(n)]
    return _behind(copies, pairs, outs, 3 * n, _same_core_of_other_chips, name=name, collective_id=ID_CHIPS)


def _sum_chips(pair, parts, chip, kind, *, name):
    _, r, cols = parts.shape
    tr = min(128, r)
    if kind == "cols":
        own_spec = pl.BlockSpec((1, tr, cols), lambda i, chip_ref: (0, i, chip_ref[0]))
    else:
        own_spec = pl.BlockSpec((1, tr, cols), lambda i, chip_ref: (chip_ref[0], i, 0))

    def body(chip_ref, q_ref, p_ref, o_ref):
        del chip_ref
        acc = q_ref[0].astype(F32)
        for s in range(3):
            acc = acc + p_ref[s].astype(F32)
        o_ref[...] = acc

    return pl.pallas_call(
        body, name=name, out_shape=jax.ShapeDtypeStruct((r, cols), F32),
        grid_spec=pltpu.PrefetchScalarGridSpec(
            num_scalar_prefetch=1, grid=(r // tr,),
            in_specs=[own_spec, pl.BlockSpec((3, tr, cols), lambda i, chip_ref: (0, i, 0))],
            out_specs=pl.BlockSpec((tr, cols), lambda i, chip_ref: (i, 0))),
        compiler_params=_params("parallel"),
    )(chip, pair, parts)


def _rs_share(halves, *, name):
    n = len(halves)

    def copies(h_refs, g_refs, send_sems, recv_sems):
        x, y, c = _coords()
        cps = [_remote(h_refs[a], g_refs[a], send_sems, recv_sems, a, (x, y, 1 - c)) for a in range(n)]
        for cp in cps:
            cp.start()
        for cp in cps:
            cp.wait()

    outs = [jax.ShapeDtypeStruct(h.shape, h.dtype) for h in halves]
    return _behind(copies, halves, outs, n, _sibling, name=name, collective_id=ID_SIBLING)


class _GradReducer:
    def __init__(self, w, m, v, long_name, core, chip_ix, as_views):
        self.w, self.m, self.v, self.long_name = w, m, v, long_name
        self.core, self.chip_ix, self.as_views = core, chip_ix, as_views
        self.groups, self.results = {}, {}

    def begin(self, tag, grads, tie):
        keys, views, kinds = self.as_views(grads)
        views, tie = lax.optimization_barrier((views, tie))
        parts = _rs_pair_exchange(views, name="rs_pair_exchange_" + tag)
        self.groups[tag] = dict(keys=keys, views=views, kinds=kinds, parts=parts)
        return tie

    def pair(self, tag, tie):
        g = self.groups[tag]
        g["parts"], tie = lax.optimization_barrier((g["parts"], tie))
        pairs = [_add_half(v, p, self.core, name="pair_sum_" + self.long_name[k])
                 for v, p, k in zip(g["views"], g["parts"], g["keys"])]
        pairs, tie = lax.optimization_barrier((pairs, tie))
        g["pairs"] = pairs
        g["slots"] = _rs_chips(pairs, g["kinds"], name="rs_chips_" + tag)
        return tie

    def chip(self, tag, tie):
        g = self.groups[tag]
        g["slots"], tie = lax.optimization_barrier((g["slots"], tie))
        halves = [_sum_chips(q, t, self.chip_ix, kd, name="chip_sum_" + self.long_name[k])
                  for q, t, kd, k in zip(g["pairs"], g["slots"], g["kinds"], g["keys"])]
        halves, tie = lax.optimization_barrier((halves, tie))
        g["halves"] = halves
        g["others"] = _rs_share(halves, name="rs_share_" + tag)
        return tie

    def finish(self, tag, tie):
        g = self.groups[tag]
        g["others"], tie = lax.optimization_barrier((g["others"], tie))
        out = [_adamw_halves(self.w[k], a, b, self.m[k], self.v[k], self.core, name="adamw_" + self.long_name[k])
               for k, a, b in zip(g["keys"], g["halves"], g["others"])]
        out, tie = lax.optimization_barrier((out, tie))
        self.results.update(zip(g["keys"], out))
        return tie


def _adamw_halves(w, mine, theirs, m, v, core, *, name, tr=128):
    rows, cols = w.shape
    hr = rows // 2
    tr = min(tr, hr)
    nt = hr // tr

    def body(c_ref, w_ref, a_ref, b_ref, m_ref, v_ref, g_ref, d_ref, mo_ref, vo_ref):
        gg = jnp.where(pl.program_id(0) == c_ref[0], a_ref[...], b_ref[...])
        g_ref[...] = gg
        d, mn, vn = _adamw_math(w_ref[...], gg, m_ref[...], v_ref[...])
        d_ref[...] = d
        mo_ref[...] = mn
        vo_ref[...] = vn

    full = pl.BlockSpec((tr, cols), lambda hf, i, c_ref: (hf * nt + i, 0))
    half = pl.BlockSpec((tr, cols), lambda hf, i, c_ref: (i, 0))
    sd = jax.ShapeDtypeStruct((rows, cols), F32)
    return pl.pallas_call(
        body, name=name, out_shape=(sd, sd, sd, sd),
        grid_spec=pltpu.PrefetchScalarGridSpec(
            num_scalar_prefetch=1, grid=(2, nt),
            in_specs=[full, half, half, full, full], out_specs=(full, full, full, full)),
        compiler_params=_params("parallel", "parallel"),
    )(core, w, mine, theirs, m, v)


def _allreduce_small(vec):
    m_per, ncol = vec.shape
    n_dev = 2 * N_CHIPS

    def body(x_ref, out_ref, sum_ref, send_sems, recv_sems, local_sem):
        x, y, c = _coords()
        me, sibling = (x, y, c), (x, y, 1 - c)
        chips = _other_chips(x, y)

        def rows(px, py, pc):
            return out_ref.at[pl.ds(pl.multiple_of((4 * px + 2 * py + pc) * m_per, m_per), m_per), :]

        def copy(k, block, to, src=None):
            return _remote(rows(*block) if src is None else src, rows(*block), send_sems, recv_sems, k, to)

        mine = pltpu.make_async_copy(x_ref, rows(*me), local_sem)
        mine.start()
        first = [copy(0, me, sibling, src=x_ref)]
        first += [copy(1 + j, me, (*chip, c), src=x_ref) for j, chip in enumerate(chips)]
        for cp in first:
            cp.start()
        passed = [copy(4 + j, (*chip, c), sibling) for j, chip in enumerate(chips)]
        for j, chip in enumerate(chips):
            copy(1 + j, (*chip, c), me).wait_recv()
            passed[j].start()
        copy(0, sibling, me).wait_recv()
        for j, chip in enumerate(chips):
            copy(4 + j, (*chip, 1 - c), me).wait_recv()
        for cp in first + passed:
            cp.wait_send()
        mine.wait()
        acc = out_ref[0:m_per, :]
        for d in range(1, n_dev):
            acc = acc + out_ref[d * m_per:(d + 1) * m_per, :]
        sum_ref[...] = acc

    vm = pl.BlockSpec(memory_space=pltpu.VMEM)
    return pl.pallas_call(
        body, name="allreduce_small",
        out_shape=(jax.ShapeDtypeStruct((n_dev * m_per, ncol), vec.dtype), jax.ShapeDtypeStruct((m_per, ncol), vec.dtype)),
        in_specs=[vm], out_specs=(vm, vm),
        scratch_shapes=[pltpu.SemaphoreType.DMA((7,)), pltpu.SemaphoreType.DMA((7,)), pltpu.SemaphoreType.DMA],
    )(vec)[1]


def _pack_rows(pieces, total_rows):
    rows = []
    for p in pieces:
        flat = p.reshape(-1)
        flat = jnp.pad(flat, (0, (-flat.shape[0]) % LANES))
        rows.append(flat.reshape(-1, LANES))
    out = jnp.concatenate(rows, axis=0)
    return jnp.pad(out, ((0, total_rows - out.shape[0]), (0, 0)))


def _unpack_rows(packed, shapes):
    out, r = [], 0
    for shp in shapes:
        size = 1
        for d in shp:
            size *= d
        nr = -(-size // LANES)
        out.append(packed[r:r + nr].reshape(-1)[:size].reshape(shp))
        r += nr
    return out


def _round_up(v, m):
    return -(-v // m) * m


def kernel(x, a_w_in, a_gate_b, a_conv_w, a_conv_b, a_head_g, a_w_out, a_ln_g, a_ln_b, kv_w, b_w_in, b_w_out, b_ln_g, b_ln_b, loss_target, m_a_w_in, m_a_gate_b, m_a_conv_w, m_a_conv_b, m_a_head_g, m_a_w_out, m_a_ln_g, m_a_ln_b, m_kv_w, m_b_w_in, m_b_w_out, m_b_ln_g, m_b_ln_b, v_a_w_in, v_a_gate_b, v_a_conv_w, v_a_conv_b, v_a_head_g, v_a_w_out, v_a_ln_g, v_a_ln_b, v_kv_w, v_b_w_in, v_b_w_out, v_b_ln_g, v_b_ln_b):
    _, S, D = x.shape
    nha = a_gate_b.shape[1] // 2
    chip = 2 * lax.axis_index("x") + lax.axis_index("y")
    core = lax.axis_index("c").astype(jnp.int32).reshape(1)
    dq = D // N_CHIPS

    shards = [a_w_in[0].astype(BF16), a_w_out[0].astype(BF16), kv_w.astype(BF16), b_w_in[0].astype(BF16),
              b_w_out[0].astype(BF16)]
    kinds = ["stack", "rows", "cols", "cols", "rows"]
    small_shard = jnp.concatenate([a_conv_w[0], a_conv_b, a_head_g, a_ln_g, a_ln_b], axis=0)
    wa_g, small_full = _allgather_weights(shards[:1], kinds[:1], small_shard)
    later, _ = lax.optimization_barrier((shards[1:], small_full))
    wao, wkv, wbi, wbo = _allgather_behind(later, kinds[1:], name="allgather_behind")
    wa_full = jnp.concatenate([wa_g[j] for j in range(N_CHIPS)], axis=1)
    wa = wa_full[:, :4 * D]
    wg = jnp.pad(wa_full[:, 4 * D:], ((0, 0), (0, LANES - 2 * nha)))
    conv_w, conv_b, head_g, ln_g_a, ln_b_a = (small_full[0:CONV_A], small_full[4:5], small_full[5:6],
                                              small_full[6:7], small_full[7:8])

    chip_ix = chip.astype(jnp.int32).reshape(1)
    w_big = dict(wa=a_w_in[0], wao=a_w_out[0], wkv=kv_w, wbi=b_w_in[0], wbo=b_w_out[0])
    m_big = dict(wa=m_a_w_in[0], wao=m_a_w_out[0], wkv=m_kv_w, wbi=m_b_w_in[0], wbo=m_b_w_out[0])
    v_big = dict(wa=v_a_w_in[0], wao=v_a_w_out[0], wkv=v_kv_w, wbi=v_b_w_in[0], wbo=v_b_w_out[0])
    long_name = dict(wa="a_w_in", wao="a_w_out", wkv="kv_w", wbi="b_w_in", wbo="b_w_out")

    def as_views(g):
        if "wa" in g:
            g_full = jnp.concatenate([g["wa"], g["wg"][:, :2 * nha]], axis=1)
            ca = g_full.shape[1] // N_CHIPS
            return ["wa"], [jnp.stack([g_full[:, j * ca:(j + 1) * ca] for j in range(N_CHIPS)], axis=0)], ["stack"]
        keys = list(g)
        views = [g[k].reshape(N_CHIPS, dq, D) if k in ("wao", "wbo") else g[k][None] for k in keys]
        return keys, views, ["stack" if k in ("wao", "wbo") else "cols" for k in keys]

    rs = _GradReducer(w_big, m_big, v_big, long_name, core, chip_ix, as_views)
    loss_row, grad_x, small = _local_step(
        x[0], loss_target[0], wa, wg, wao, wbi, wkv, wbo, a_gate_b, conv_w, conv_b, head_g,
        ln_g_a, ln_b_a, b_ln_g, b_ln_b, S=S, D=D, nha=nha, rs=rs)
    order = ["conv_w", "conv_b", "head_g", "a_ln_g", "a_ln_b", "b_ln_g", "b_ln_b", "gate_b"]
    full_shapes = [(CONV_A, D), (1, D), (1, D), (1, D), (1, D), (1, D), (1, D), (1, 2 * nha)]
    n_rows = sum(-(-(s[0] * s[1]) // LANES) for s in full_shapes) + 1
    packed = _pack_rows([small[k] for k in order] + [loss_row], _round_up(n_rows, 8))
    packed, grad_x = lax.optimization_barrier((packed, grad_x))
    total = _allreduce_small(packed)
    sums = dict(zip(order, _unpack_rows(total, full_shapes)))
    loss = total[n_rows - 1, 0]

    def mine(v):
        return lax.dynamic_slice_in_dim(v, chip * dq, dq, axis=1)

    g_small = [sums["gate_b"], mine(sums["conv_w"]), mine(sums["conv_b"]), mine(sums["head_g"]),
               mine(sums["a_ln_g"]), mine(sums["a_ln_b"]), sums["b_ln_g"], sums["b_ln_b"]]
    w_small = [a_gate_b, a_conv_w[0], a_conv_b, a_head_g, a_ln_g, a_ln_b, b_ln_g, b_ln_b]
    m_small = [m_a_gate_b, m_a_conv_w[0], m_a_conv_b, m_a_head_g, m_a_ln_g, m_a_ln_b, m_b_ln_g, m_b_ln_b]
    v_small = [v_a_gate_b, v_a_conv_w[0], v_a_conv_b, v_a_head_g, v_a_ln_g, v_a_ln_b, v_b_ln_g, v_b_ln_b]
    shp_small = [w.shape for w in w_small]
    rows_small = _round_up(sum(-(-(s[0] * s[1]) // LANES) for s in shp_small), 8)
    upd_small = _adamw(_pack_rows(w_small, rows_small), _pack_rows(g_small, rows_small),
                       _pack_rows(m_small, rows_small), _pack_rows(v_small, rows_small), name="adamw_small")
    loss, upd_small = rs.finish("w", rs.chip("w", (loss, upd_small)))
    upd_big = [rs.results[k] for k in ("wa", "wao", "wkv", "wbi", "wbo")]
    g_big = [u[0] for u in upd_big]
    d_small, mn_small, vn_small = (_unpack_rows(u, shp_small) for u in upd_small)

    def assemble(big5, small8):
        awi, awo, kvw, bwi, bwo = big5
        gb, cw, cb, hg, alg, alb, blg, blb = small8
        return [awi[None], gb, cw[None], cb, hg, awo[None], alg, alb, kvw, bwi[None], bwo[None], blg, blb]

    grads = assemble(g_big, g_small)
    deltas = assemble([u[1] for u in upd_big], d_small)
    new_m = assemble([u[2] for u in upd_big], mn_small)
    new_v = assemble([u[3] for u in upd_big], vn_small)
    return (loss, grad_x[None], *grads, *deltas, *new_m, *new_v)
```

```python
import functools

import jax
import jax.numpy as jnp
from jax import lax
from jax.experimental import pallas as pl
from jax.experimental.pallas import tpu as pltpu
from jax.experimental.pallas import tpu_sc as plsc

F32 = jnp.float32
BF16 = jnp.bfloat16

DEPTH = 2
ALPHA = (2.0 * DEPTH) ** 0.25
LN_EPS = 1e-5
DK_A = 128
DV_A = 256
DH_B = 128
SB_TQ = 512
SB_TK = 512
ML_TQ = 512
CONV_A = 4
ADAM_LR = 0.001
ADAM_B1 = 0.9
ADAM_B2 = 0.999
ADAM_EPS = 1e-08
ADAM_WD = 0.01
ADAM_STEP = 10
N_CHIPS = 4
LANES = 128
V7X_VMEM_BYTES = 64 * 1024 * 1024
VMEM_LIMIT = (V7X_VMEM_BYTES * 3) // 4
NEG_BIG = -1e30
MESH = pl.DeviceIdType.MESH
ANY = pl.BlockSpec(memory_space=pl.ANY)


def _params(*sem):
    return pltpu.CompilerParams(dimension_semantics=sem, vmem_limit_bytes=VMEM_LIMIT)


def _dot(a, b, dims):
    return lax.dot_general(a, b, (dims, ((), ())), preferred_element_type=F32)


def _dot_nn(a, b):
    return _dot(a, b, ((1,), (0,)))


def _dot_nt(a, b):
    return _dot(a, b, ((1,), (1,)))


def _dot_tn(a, b):
    return _dot(a, b, ((0,), (0,)))


def _split2(x):
    hi = x.astype(BF16)
    lo = (x - hi.astype(F32)).astype(BF16)
    return hi, lo


def _split3(x):
    hi = x.astype(BF16)
    r = x - hi.astype(F32)
    mid = r.astype(BF16)
    lo = (r - mid.astype(F32)).astype(BF16)
    return hi, mid, lo


def _mask_dot2(x, t01):
    hi, lo = _split2(x)
    return _dot_nn(hi, t01) + _dot_nn(lo, t01)


def _mask_dot3(x, t01):
    hi, mid, lo = _split3(x)
    return _dot_nn(hi, t01) + _dot_nn(mid, t01) + _dot_nn(lo, t01)


def _mask_dot3_left(t01, x):
    hi, mid, lo = _split3(x)
    return _dot_nn(t01, hi) + _dot_nn(t01, mid) + _dot_nn(t01, lo)


def _log_sigmoid(z):
    return jnp.minimum(z, 0.0) - jnp.log(1.0 + jnp.exp(-jnp.abs(z)))


def _sigmoid(z):
    return 1.0 / (1.0 + jnp.exp(-z))


def _mm(a, b, *, mode, M, N, K, out_dtype, name, tm=1024, tn=1024, tk=1024,
        a_off=(0, 0), b_off=(0, 0), acc_in=None, acc_scale=1.0, into=None, out_off=(0, 0)):
    tm, tn, tk = min(tm, M), min(tn, N), min(tk, K)
    assert M % tm == 0 and N % tn == 0 and K % tk == 0
    nk = K // tk
    ar, ac = a_off
    br, bc = b_off
    orow, ocol = out_off
    if mode in ("nn", "nt"):
        assert ar % tm == 0 and ac % tk == 0
        a_spec = pl.BlockSpec((tm, tk), lambda i, j, k: (i + ar // tm, k + ac // tk))
    else:
        assert ar % tk == 0 and ac % tm == 0
        a_spec = pl.BlockSpec((tk, tm), lambda i, j, k: (k + ar // tk, i + ac // tm))
    if mode == "nt":
        assert br % tn == 0 and bc % tk == 0
        b_spec = pl.BlockSpec((tn, tk), lambda i, j, k: (j + br // tn, k + bc // tk))
    else:
        assert br % tk == 0 and bc % tn == 0
        b_spec = pl.BlockSpec((tk, tn), lambda i, j, k: (k + br // tk, j + bc // tn))
    assert orow % tm == 0 and ocol % tn == 0
    o_spec = pl.BlockSpec((tm, tn), lambda i, j, k: (i + orow // tm, j + ocol // tn))
    dims = {"nn": ((1,), (0,)), "nt": ((1,), (1,)), "tn": ((0,), (0,))}[mode]
    inputs, in_specs = [a, b], [a_spec, b_spec]
    has_acc = acc_in is not None
    if has_acc:
        inputs.append(acc_in)
        in_specs.append(pl.BlockSpec((tm, tn), lambda i, j, k: (i, j)))
    aliases = {}
    if into is not None:
        inputs.append(into)
        in_specs.append(ANY)
        aliases = {len(inputs) - 1: 0}
        out_shape = jax.ShapeDtypeStruct(into.shape, into.dtype)
        assert into.dtype == out_dtype
    else:
        out_shape = jax.ShapeDtypeStruct((M, N), out_dtype)

    def body(*refs):
        a_ref, b_ref = refs[0], refs[1]
        acc_in_ref = refs[2] if has_acc else None
        n_in = len(inputs)
        o_ref = refs[n_in]

        def first():
            if has_acc:
                return acc_scale * acc_in_ref[...]
            return None

        if nk == 1:
            r = _dot(a_ref[...], b_ref[...], dims)
            f = first()
            if f is not None:
                r = r + f
            o_ref[...] = r.astype(o_ref.dtype)
        else:
            acc_ref = refs[n_in + 1]
            kk = pl.program_id(2)

            @pl.when(kk == 0)
            def _():
                f = first()
                acc_ref[...] = jnp.zeros_like(acc_ref) if f is None else f

            acc_ref[...] += _dot(a_ref[...], b_ref[...], dims)

            @pl.when(kk == nk - 1)
            def _():
                o_ref[...] = acc_ref[...].astype(o_ref.dtype)

    scratch = [] if nk == 1 else [pltpu.VMEM((tm, tn), F32)]
    return pl.pallas_call(
        body, name=name, out_shape=out_shape, grid=(M // tm, N // tn, nk),
        in_specs=in_specs, out_specs=o_spec, scratch_shapes=scratch,
        input_output_aliases=aliases,
        compiler_params=_params("parallel", "parallel", "arbitrary"),
    )(*inputs)


def _shift_down(x, d, row):
    if d == 0:
        return x
    return jnp.where(row >= d, pltpu.roll(x, d, 0), 0.0)


def _shift_up(x, d, row, n):
    if d == 0:
        return x
    return jnp.where(row < n - d, pltpu.roll(x, n - d, 0), 0.0)


def _conv_pre(x, w_ref, b_ref, row):
    c = b_ref[...] + w_ref[CONV_A - 1:CONV_A, :] * x
    for k in range(CONV_A - 1):
        c = c + w_ref[k:k + 1, :] * _shift_down(x, CONV_A - 1 - k, row)
    return c


def _conv_fwd(u, conv_w, conv_b, *, S, D, name):
    tc = 256
    nq_blocks = (D // 2) // tc

    def body(u_ref, w_ref, b_ref, o_ref):
        x = u_ref[...]
        row = lax.broadcasted_iota(jnp.int32, x.shape, 0)
        c = _conv_pre(x, w_ref, b_ref, row)
        scale = jnp.where(pl.program_id(0) >= nq_blocks, DK_A ** -0.5, 1.0).astype(F32)
        o_ref[...] = (c * _sigmoid(c) * scale).astype(BF16)

    return pl.pallas_call(
        body, name=name, out_shape=jax.ShapeDtypeStruct((S, D), BF16), grid=(D // tc,),
        in_specs=[pl.BlockSpec((S, tc), lambda j: (0, j)),
                  pl.BlockSpec((CONV_A, tc), lambda j: (0, j)),
                  pl.BlockSpec((1, tc), lambda j: (0, j))],
        out_specs=pl.BlockSpec((S, tc), lambda j: (0, j)),
        compiler_params=_params("parallel"),
    )(u, conv_w, conv_b)


def _conv_bwd(u, dq, dk, conv_w, conv_b, du, *, S, D, name):
    tc = 256
    nq_blocks = (D // 2) // tc

    def body(u_ref, dq_ref, dk_ref, w_ref, b_ref, du_in, du_ref, dw_ref, db_ref):
        del du_in
        x = u_ref[...]
        n = x.shape[0]
        row = lax.broadcasted_iota(jnp.int32, x.shape, 0)
        c = _conv_pre(x, w_ref, b_ref, row)
        is_k = pl.program_id(0) >= nq_blocks
        dy = jnp.where(is_k, dk_ref[...] * (DK_A ** -0.5), dq_ref[...])
        sg = _sigmoid(c)
        dc = dy * (sg * (1.0 + c * (1.0 - sg)))
        db_ref[...] = jnp.sum(dc, axis=0, keepdims=True)
        dx = w_ref[CONV_A - 1:CONV_A, :] * dc
        dw_ref[CONV_A - 1:CONV_A, :] = jnp.sum(dc * x, axis=0, keepdims=True)
        for k in range(CONV_A - 1):
            d = CONV_A - 1 - k
            dw_ref[k:k + 1, :] = jnp.sum(dc * _shift_down(x, d, row), axis=0, keepdims=True)
            dx = dx + w_ref[k:k + 1, :] * _shift_up(dc, d, row, n)
        du_ref[...] = dx.astype(BF16)

    half = lambda j: (0, j % nq_blocks)
    return pl.pallas_call(
        body, name=name,
        out_shape=(jax.ShapeDtypeStruct(du.shape, du.dtype),
                   jax.ShapeDtypeStruct((CONV_A, D), F32), jax.ShapeDtypeStruct((1, D), F32)),
        grid=(D // tc,),
        in_specs=[pl.BlockSpec((S, tc), lambda j: (0, j)),
                  pl.BlockSpec((S, tc), half), pl.BlockSpec((S, tc), half),
                  pl.BlockSpec((CONV_A, tc), lambda j: (0, j)),
                  pl.BlockSpec((1, tc), lambda j: (0, j)), ANY],
        out_specs=(pl.BlockSpec((S, tc), lambda j: (0, j)),
                   pl.BlockSpec((CONV_A, tc), lambda j: (0, j)),
                   pl.BlockSpec((1, tc), lambda j: (0, j))),
        input_output_aliases={5: 0},
        compiler_params=_params("parallel"),
    )(u, dq, dk, conv_w, conv_b, du)


def _tri(n, cmp):
    r = lax.broadcasted_iota(jnp.int32, (n, n), 0)
    c = lax.broadcasted_iota(jnp.int32, (n, n), 1)
    return r, c, cmp(r, c)


def _gates_fwd(gt, bias, *, nha, nb, name):
    half = nha * nb
    nb_shift = nb.bit_length() - 1
    assert nb == 1 << nb_shift

    def body(g_ref, b_ref, f_ref, brow_ref):
        ig = g_ref[0:half, :] + b_ref[0:half, :]
        fg = g_ref[half:2 * half, :] + b_ref[half:2 * half, :]
        lf = _log_sigmoid(fg)
        _, _, upper = _tri(LANES, lambda r, c: r <= c)
        cs = _mask_dot3(lf, upper.astype(BF16))
        tot = jnp.broadcast_to(cs[:, LANES - 1:LANES], cs.shape)
        r, c, _ = _tri(half, lambda r, c: r <= c)
        before = jnp.logical_and(r >> nb_shift == c >> nb_shift, c < r).astype(BF16)
        f = cs + _mask_dot3_left(before, tot)
        f_ref[...] = f
        brow_ref[...] = ig - f

    return pl.pallas_call(
        body, name=name,
        out_shape=(jax.ShapeDtypeStruct((half, LANES), F32), jax.ShapeDtypeStruct((half, LANES), F32)),
    )(gt, bias)


def _gates_bwd(rowsum, colsum, gt, bias, *, nha, nb, name):
    half = nha * nb
    nb_shift = nb.bit_length() - 1
    assert nb == 1 << nb_shift

    def body(rs_ref, cs_ref, g_ref, b_ref, dg_ref, tot_ref):
        col = cs_ref[...]
        df = rs_ref[...] - col
        _, _, lower = _tri(LANES, lambda r, c: r >= c)
        rc = _mask_dot3(df, lower.astype(BF16))
        tot = jnp.broadcast_to(rc[:, 0:1], rc.shape)
        r, c, _ = _tri(half, lambda r, c: r <= c)
        same = r >> nb_shift == c >> nb_shift
        after = jnp.logical_and(same, c > r).astype(BF16)
        dlf = rc + _mask_dot3_left(after, tot)
        fg = g_ref[half:2 * half, :] + b_ref[half:2 * half, :]
        dfg = dlf * _sigmoid(-fg)
        dg_ref[0:half, :] = col
        dg_ref[half:2 * half, :] = dfg
        grp = same.astype(BF16)
        ones = jnp.ones((LANES, LANES), BF16)
        tot_ref[0:half, :] = _mask_dot3_left(grp, _mask_dot3(col, ones))
        tot_ref[half:2 * half, :] = _mask_dot3_left(grp, _mask_dot3(dfg, ones))

    return pl.pallas_call(
        body, name=name,
        out_shape=(jax.ShapeDtypeStruct((2 * half, LANES), F32), jax.ShapeDtypeStruct((2 * half, LANES), F32)),
    )(rowsum, colsum, gt, bias)


def _mlstm_tile(q, k_ref, fcol, brow_ref, j, i, tq, m=None):
    off = pl.multiple_of(j * tq, tq)
    kj = k_ref[pl.ds(off, tq), :]
    s = _dot_nt(q, kj)
    row = lax.broadcasted_iota(jnp.int32, (tq, tq), 0)
    col = lax.broadcasted_iota(jnp.int32, (tq, tq), 1)
    valid = jnp.logical_or(col <= row, j < i)
    logd = jnp.where(valid, fcol + brow_ref[0, :, pl.ds(off, tq)], NEG_BIG)
    return off, kj, s, logd


def _mlstm_fwd(qk, u, fcol, brow, *, S, D, nha, name):
    tq = min(ML_TQ, S)
    nq = S // tq
    kb, vb = (D // 2) // DK_A, D // DV_A

    def body(q_ref, k_ref, v_ref, fcol_ref, brow_ref, h_ref, m_ref, den_ref):
        i = pl.program_id(1)
        q = q_ref[...]
        fc = fcol_ref[0]

        def step(j, carry):
            acc, den, m = carry
            off, _, s, logd = _mlstm_tile(q, k_ref, fc, brow_ref, j, i, tq)
            m_new = jnp.maximum(m, jnp.max(logd, axis=1, keepdims=True))
            a = s * jnp.exp(logd - m_new)
            alpha = jnp.exp(m - m_new)
            vj = v_ref[pl.ds(off, tq), :].astype(BF16)
            acc = alpha * acc + _dot_nn(a.astype(BF16), vj)
            den = alpha * den + jnp.sum(a, axis=1, keepdims=True)
            return acc, den, m_new

        acc, den, m = lax.fori_loop(
            0, i + 1, step,
            (jnp.zeros((tq, DV_A), F32), jnp.zeros((tq, 1), F32), jnp.full((tq, 1), NEG_BIG, F32)))
        h_ref[...] = acc / jnp.maximum(jnp.abs(den), jnp.exp(-m))
        m_ref[0] = m
        den_ref[0] = den

    stat = pl.BlockSpec((1, tq, 1), lambda h, i: (h, i, 0))
    return pl.pallas_call(
        body, name=name,
        out_shape=(jax.ShapeDtypeStruct((S, D), F32), jax.ShapeDtypeStruct((nha, S, 1), F32),
                   jax.ShapeDtypeStruct((nha, S, 1), F32)),
        grid=(nha, nq),
        in_specs=[pl.BlockSpec((tq, DK_A), lambda h, i: (i, h)),
                  pl.BlockSpec((S, DK_A), lambda h, i: (0, kb + h)),
                  pl.BlockSpec((S, DV_A), lambda h, i: (0, vb + h)),
                  stat, pl.BlockSpec((1, 1, S), lambda h, i: (h, 0, 0))],
        out_specs=(pl.BlockSpec((tq, DV_A), lambda h, i: (i, h)), stat, stat),
        compiler_params=_params("parallel", "arbitrary"),
    )(qk, qk, u, fcol, brow)


def _mlstm_bwd(qk, u, fcol, brow, m, den, dh, h, du, *, S, D, nha, name):
    tq = min(ML_TQ, S)
    nq = S // tq
    kb, vb = (D // 2) // DK_A, D // DV_A

    def body(q_ref, k_ref, v_ref, fcol_ref, brow_ref, m_ref, den_ref, dh_ref, h_ref, du_in,
             du_ref, dq_ref, dk_ref, rs_ref, cs_ref, dv_acc):
        del du_in
        i = pl.program_id(1)

        @pl.when(i == 0)
        def _():
            dk_ref[...] = jnp.zeros_like(dk_ref)
            cs_ref[...] = jnp.zeros_like(cs_ref)
            dv_acc[...] = jnp.zeros_like(dv_acc)

        q = q_ref[...]
        fc = fcol_ref[0]
        mm = m_ref[0]
        dn = den_ref[0]
        floor = jnp.exp(-mm)
        nrm = jnp.maximum(jnp.abs(dn), floor)
        dhv = dh_ref[...]
        dnum = dhv / nrm
        dnrm = -jnp.sum(dhv * h_ref[...], axis=1, keepdims=True) / nrm
        dden = jnp.where(jnp.abs(dn) > floor, jnp.where(dn > 0.0, dnrm, -dnrm), 0.0)
        dnum_b = dnum.astype(BF16)

        def step(j, carry):
            dq, rs = carry
            off, kj, s, logd = _mlstm_tile(q, k_ref, fc, brow_ref, j, i, tq)
            p = jnp.exp(logd - mm)
            a = s * p
            vj = v_ref[pl.ds(off, tq), :].astype(BF16)
            da = _dot_nt(dnum_b, vj) + dden
            dv_acc[pl.ds(off, tq), :] += _dot_tn(a.astype(BF16), dnum_b)
            dqk = (da * p).astype(BF16)
            dq = dq + _dot_nn(dqk, kj)
            dk_ref[pl.ds(off, tq), :] += _dot_tn(dqk, q)
            pm = da * a
            cs_ref[0, :, pl.ds(off, tq)] += jnp.sum(pm, axis=0, keepdims=True)
            rs = rs + jnp.sum(pm, axis=1, keepdims=True)
            return dq, rs

        dq, rs = lax.fori_loop(0, i + 1, step, (jnp.zeros((tq, DK_A), F32), jnp.zeros((tq, 1), F32)))
        dq_ref[...] = dq
        rs_ref[0] = rs

        @pl.when(i == nq - 1)
        def _():
            du_ref[...] = dv_acc[...].astype(BF16)

    stat = pl.BlockSpec((1, tq, 1), lambda h, i: (h, i, 0))
    rowv = pl.BlockSpec((1, 1, S), lambda h, i: (h, 0, 0))
    hblk = pl.BlockSpec((tq, DV_A), lambda h, i: (i, h))
    return pl.pallas_call(
        body, name=name,
        out_shape=(jax.ShapeDtypeStruct(du.shape, du.dtype),
                   jax.ShapeDtypeStruct((S, D // 2), F32), jax.ShapeDtypeStruct((S, D // 2), F32),
                   jax.ShapeDtypeStruct((nha, S, 1), F32), jax.ShapeDtypeStruct((nha, 1, S), F32)),
        grid=(nha, nq),
        in_specs=[pl.BlockSpec((tq, DK_A), lambda h, i: (i, h)),
                  pl.BlockSpec((S, DK_A), lambda h, i: (0, kb + h)),
                  pl.BlockSpec((S, DV_A), lambda h, i: (0, vb + h)),
                  stat, rowv, stat, stat, hblk, hblk, ANY],
        out_specs=(pl.BlockSpec((S, DV_A), lambda h, i: (0, vb + h)),
                   pl.BlockSpec((tq, DK_A), lambda h, i: (i, h)),
                   pl.BlockSpec((S, DK_A), lambda h, i: (0, h)),
                   stat, rowv),
        scratch_shapes=[pltpu.VMEM((S, DV_A), F32)],
        input_output_aliases={9: 0},
        compiler_params=_params("parallel", "arbitrary"),
    )(qk, qk, u, fcol, brow, m, den, dh, h, du)


def _head_norm(hh):
    mu = jnp.mean(hh, axis=1, keepdims=True)
    hc = hh - mu
    rstd = lax.rsqrt(jnp.mean(hc * hc, axis=1, keepdims=True) + LN_EPS)
    return hc * rstd, rstd


def _hgate_fwd(h, u, head_g, *, S, D, name):
    tm = min(256, S)
    nh = D // DV_A

    def body(h_ref, o_ref, z_ref, g_ref, out_ref):
        for hd in range(nh):
            sl = slice(hd * DV_A, (hd + 1) * DV_A)
            hn, _ = _head_norm(h_ref[:, sl])
            z = z_ref[:, sl]
            out_ref[:, sl] = (_sigmoid(o_ref[:, sl]) * (hn * g_ref[:, sl]) * (z * _sigmoid(z))).astype(BF16)

    return pl.pallas_call(
        body, name=name, out_shape=jax.ShapeDtypeStruct((S, D), BF16), grid=(S // tm,),
        in_specs=[pl.BlockSpec((tm, D), lambda i: (i, 0)), pl.BlockSpec((tm, D), lambda i: (i, 2)),
                  pl.BlockSpec((tm, D), lambda i: (i, 3)), pl.BlockSpec((1, D), lambda i: (0, 0))],
        out_specs=pl.BlockSpec((tm, D), lambda i: (i, 0)),
        compiler_params=_params("parallel"),
    )(h, u, u, head_g)


def _hgate_bwd(dhg, h, u, head_g, du, *, S, D, name):
    tm = min(256, S)
    nh = D // DV_A

    def body(dhg_ref, h_ref, o_ref, z_ref, g_ref, du_in, du_ref, dh_ref, dg_ref):
        del du_in

        @pl.when(pl.program_id(0) == 0)
        def _():
            dg_ref[...] = jnp.zeros_like(dg_ref)

        for hd in range(nh):
            sl = slice(hd * DV_A, (hd + 1) * DV_A)
            hn, rstd = _head_norm(h_ref[:, sl])
            o, z, g, d = o_ref[:, sl], z_ref[:, sl], g_ref[:, sl], dhg_ref[:, sl]
            so, sz = _sigmoid(o), _sigmoid(z)
            silu_z = z * sz
            hng = hn * g
            du_ref[:, sl] = (d * hng * silu_z * so * (1.0 - so)).astype(BF16)
            du_ref[:, D + hd * DV_A:D + (hd + 1) * DV_A] = (
                d * so * hng * (sz * (1.0 + z * (1.0 - sz)))).astype(BF16)
            t = d * so * silu_z
            dg_ref[:, sl] += jnp.sum(t * hn, axis=0, keepdims=True)
            dhn = t * g
            dh_ref[:, sl] = rstd * (dhn - jnp.mean(dhn, axis=1, keepdims=True)
                                    - hn * jnp.mean(dhn * hn, axis=1, keepdims=True))

    row = lambda i: (i, 0)
    return pl.pallas_call(
        body, name=name,
        out_shape=(jax.ShapeDtypeStruct(du.shape, du.dtype), jax.ShapeDtypeStruct((S, D), F32),
                   jax.ShapeDtypeStruct((1, D), F32)),
        grid=(S // tm,),
        in_specs=[pl.BlockSpec((tm, D), row), pl.BlockSpec((tm, D), row),
                  pl.BlockSpec((tm, D), lambda i: (i, 2)), pl.BlockSpec((tm, D), lambda i: (i, 3)),
                  pl.BlockSpec((1, D), lambda i: (0, 0)), ANY],
        out_specs=(pl.BlockSpec((tm, 2 * D), lambda i: (i, 1)), pl.BlockSpec((tm, D), row),
                   pl.BlockSpec((1, D), lambda i: (0, 0))),
        input_output_aliases={5: 0},
        compiler_params=_params("arbitrary"),
    )(dhg, h, u, u, head_g, du)


def _ln_stats(r):
    mu = jnp.mean(r, axis=1, keepdims=True)
    xc = r - mu
    rstd = lax.rsqrt(jnp.mean(xc * xc, axis=1, keepdims=True) + LN_EPS)
    return xc * rstd, rstd


def _ln_back(dxhat, xhat, rstd):
    return rstd * (dxhat - jnp.mean(dxhat, axis=1, keepdims=True)
                   - xhat * jnp.mean(dxhat * xhat, axis=1, keepdims=True))


def _ln_fwd(x, y, g, b, *, S, D, name):
    tm = min(256, S)

    def body(x_ref, y_ref, g_ref, b_ref, o_ref, ob_ref):
        xhat, _ = _ln_stats(ALPHA * x_ref[...] + y_ref[...])
        o = xhat * g_ref[...] + b_ref[...]
        o_ref[...] = o
        ob_ref[...] = o.astype(BF16)

    row = lambda i: (i, 0)
    vec = pl.BlockSpec((1, D), lambda i: (0, 0))
    return pl.pallas_call(
        body, name=name,
        out_shape=(jax.ShapeDtypeStruct((S, D), F32), jax.ShapeDtypeStruct((S, D), BF16)),
        grid=(S // tm,),
        in_specs=[pl.BlockSpec((tm, D), row), pl.BlockSpec((tm, D), row), vec, vec],
        out_specs=(pl.BlockSpec((tm, D), row), pl.BlockSpec((tm, D), row)),
        compiler_params=_params("parallel"),
    )(x, y, g, b)


def _ln_loss_bwd(x1, y2, target, g, b, *, S, D, name):
    tm = min(256, S)

    def body(x_ref, y_ref, t_ref, g_ref, b_ref, dr_ref, drb_ref, dg_ref, db_ref, loss_ref):
        @pl.when(pl.program_id(0) == 0)
        def _():
            dg_ref[...] = jnp.zeros_like(dg_ref)
            db_ref[...] = jnp.zeros_like(db_ref)
            loss_ref[...] = jnp.zeros_like(loss_ref)

        xhat, rstd = _ln_stats(ALPHA * x_ref[...] + y_ref[...])
        diff = xhat * g_ref[...] + b_ref[...] - t_ref[...]
        loss_ref[...] += (0.5 / D) * jnp.sum(diff * diff)
        dx2 = diff * (1.0 / D)
        dg_ref[...] += jnp.sum(dx2 * xhat, axis=0, keepdims=True)
        db_ref[...] += jnp.sum(dx2, axis=0, keepdims=True)
        dr = _ln_back(dx2 * g_ref[...], xhat, rstd)
        dr_ref[...] = dr
        drb_ref[...] = dr.astype(BF16)

    row = lambda i: (i, 0)
    vec = pl.BlockSpec((1, D), lambda i: (0, 0))
    return pl.pallas_call(
        body, name=name,
        out_shape=(jax.ShapeDtypeStruct((S, D), F32), jax.ShapeDtypeStruct((S, D), BF16),
                   jax.ShapeDtypeStruct((1, D), F32), jax.ShapeDtypeStruct((1, D), F32),
                   jax.ShapeDtypeStruct((1, LANES), F32)),
        grid=(S // tm,),
        in_specs=[pl.BlockSpec((tm, D), row)] * 3 + [vec, vec],
        out_specs=(pl.BlockSpec((tm, D), row), pl.BlockSpec((tm, D), row), vec, vec,
                   pl.BlockSpec((1, LANES), lambda i: (0, 0))),
        compiler_params=_params("arbitrary"),
    )(x1, y2, target, g, b)


def _ln_bwd(x, y, g, dout, *, S, D, name):
    tm = min(256, S)

    def body(x_ref, y_ref, g_ref, d_ref, dr_ref, drb_ref, dg_ref, db_ref):
        @pl.when(pl.program_id(0) == 0)
        def _():
            dg_ref[...] = jnp.zeros_like(dg_ref)
            db_ref[...] = jnp.zeros_like(db_ref)

        xhat, rstd = _ln_stats(ALPHA * x_ref[...] + y_ref[...])
        d = d_ref[...]
        dg_ref[...] += jnp.sum(d * xhat, axis=0, keepdims=True)
        db_ref[...] += jnp.sum(d, axis=0, keepdims=True)
        dr = _ln_back(d * g_ref[...], xhat, rstd)
        dr_ref[...] = dr
        drb_ref[...] = dr.astype(BF16)

    row = lambda i: (i, 0)
    vec = pl.BlockSpec((1, D), lambda i: (0, 0))
    return pl.pallas_call(
        body, name=name,
        out_shape=(jax.ShapeDtypeStruct((S, D), F32), jax.ShapeDtypeStruct((S, D), BF16),
                   jax.ShapeDtypeStruct((1, D), F32), jax.ShapeDtypeStruct((1, D), F32)),
        grid=(S // tm,),
        in_specs=[pl.BlockSpec((tm, D), row), pl.BlockSpec((tm, D), row), vec, pl.BlockSpec((tm, D), row)],
        out_specs=(pl.BlockSpec((tm, D), row), pl.BlockSpec((tm, D), row), vec, vec),
        compiler_params=_params("arbitrary"),
    )(x, y, g, dout)


def _sb_scores(q, kj, j, i, tq, tk):
    z = _dot_nt(q, kj) * (DH_B ** -0.5)
    row = lax.broadcasted_iota(jnp.int32, (tq, tk), 0) + i * tq
    col = lax.broadcasted_iota(jnp.int32, (tq, tk), 1) + j * tk
    valid = col < row
    ls = _log_sigmoid(z)
    lneg = jnp.where(valid, ls - z, 0.0)
    return valid, ls, lneg


def _sb_blocks(i, tq, tk):
    return ((i + 1) * tq + tk - 1) // tk


def _sb_fwd(q2, kv, *, S, D, name, tq=SB_TQ, tk=SB_TK):
    tq, tk = min(tq, S), min(tk, S)
    nq = S // tq
    nh = D // DH_B

    def body(q_ref, k_ref, v_ref, o_ref, tot_ref):
        i = pl.program_id(1)
        q = q_ref[...]
        _, _, after = _tri(tk, lambda r, c: r > c)
        tri_after = after.astype(BF16)
        nkv = _sb_blocks(i, tq, tk)

        def step(jj, carry):
            acc, cr = carry
            j = nkv - 1 - jj
            off = pl.multiple_of(j * tk, tk)
            valid, ls, lneg = _sb_scores(q, k_ref[pl.ds(off, tk), :], j, i, tq, tk)
            between = cr + _mask_dot2(lneg, tri_after)
            a = jnp.where(valid, jnp.exp(ls + between), 0.0)
            acc = acc + _dot_nn(a.astype(BF16), v_ref[pl.ds(off, tk), :])
            return acc, cr + jnp.sum(lneg, axis=1, keepdims=True)

        acc, cr = lax.fori_loop(0, nkv, step, (jnp.zeros((tq, DH_B), F32), jnp.zeros((tq, 1), F32)))
        o_ref[...] = acc
        tot_ref[0] = cr

    return pl.pallas_call(
        body, name=name,
        out_shape=(jax.ShapeDtypeStruct((S, D), F32), jax.ShapeDtypeStruct((nh, S, 1), F32)), grid=(nh, nq),
        in_specs=[pl.BlockSpec((tq, DH_B), lambda h, i: (i, h)),
                  pl.BlockSpec((S, DH_B), lambda h, i: (0, h)),
                  pl.BlockSpec((S, DH_B), lambda h, i: (0, nh + h))],
        out_specs=(pl.BlockSpec((tq, DH_B), lambda h, i: (i, h)), pl.BlockSpec((1, tq, 1), lambda h, i: (h, i, 0))),
        compiler_params=_params("parallel", "arbitrary"),
    )(q2, kv, kv)


def _sb_bwd(q2, kv, datt, tot, *, S, D, name, tq=SB_TQ, tk=SB_TK):
    tq, tk = min(tq, S), min(tk, S)
    nq = S // tq
    nh = D // DH_B

    def body(q_ref, k_ref, v_ref, do_ref, tot_ref, dq_ref, dk_ref, dv_ref, dk_acc, dv_acc):
        i = pl.program_id(1)

        @pl.when(i == 0)
        def _():
            dk_acc[...] = jnp.zeros_like(dk_acc)
            dv_acc[...] = jnp.zeros_like(dv_acc)

        q = q_ref[...]
        do_b = do_ref[...].astype(BF16)
        _, _, after = _tri(tk, lambda r, c: r > c)
        tri_after = after.astype(BF16)
        _, _, before = _tri(tk, lambda r, c: r < c)
        tri_before = before.astype(BF16)

        def step(j, carry):
            dq, rest, cg = carry
            off = pl.multiple_of(j * tk, tk)
            kj = k_ref[pl.ds(off, tk), :]
            valid, ls, lneg = _sb_scores(q, kj, j, i, tq, tk)
            rest = rest - jnp.sum(lneg, axis=1, keepdims=True)
            between = rest + _mask_dot2(lneg, tri_after)
            a = jnp.where(valid, jnp.exp(ls + between), 0.0)
            g = _dot_nt(do_b, v_ref[pl.ds(off, tk), :]) * a
            dv_acc[pl.ds(off, tk), :] += _dot_tn(a.astype(BF16), do_b)
            e = cg + _mask_dot2(g, tri_before)
            dz = jnp.where(valid, g * jnp.exp(lneg) - e * jnp.exp(ls), 0.0) * (DH_B ** -0.5)
            dz_b = dz.astype(BF16)
            dq = dq + _dot_nn(dz_b, kj)
            dk_acc[pl.ds(off, tk), :] += _dot_tn(dz_b, q)
            return dq, rest, cg + jnp.sum(g, axis=1, keepdims=True)

        dq, _, _ = lax.fori_loop(0, _sb_blocks(i, tq, tk), step,
                                 (jnp.zeros((tq, DH_B), F32), tot_ref[0], jnp.zeros((tq, 1), F32)))
        dq_ref[...] = dq.astype(BF16)

        @pl.when(i == nq - 1)
        def _():
            dk_ref[...] = dk_acc[...].astype(BF16)
            dv_ref[...] = dv_acc[...].astype(BF16)

    blk = pl.BlockSpec((tq, DH_B), lambda h, i: (i, h))
    return pl.pallas_call(
        body, name=name,
        out_shape=(jax.ShapeDtypeStruct((S, D), BF16), jax.ShapeDtypeStruct((S, D), BF16),
                   jax.ShapeDtypeStruct((S, D), BF16)),
        grid=(nh, nq),
        in_specs=[blk, pl.BlockSpec((S, DH_B), lambda h, i: (0, h)),
                  pl.BlockSpec((S, DH_B), lambda h, i: (0, nh + h)), blk,
                  pl.BlockSpec((1, tq, 1), lambda h, i: (h, i, 0))],
        out_specs=(blk, pl.BlockSpec((S, DH_B), lambda h, i: (0, h)),
                   pl.BlockSpec((S, DH_B), lambda h, i: (0, h))),
        scratch_shapes=[pltpu.VMEM((S, DH_B), F32), pltpu.VMEM((S, DH_B), F32)],
        compiler_params=_params("parallel", "arbitrary"),
    )(q2, kv, kv, datt, tot)


def _bgate_fwd(att, z2, *, S, D, name):
    tm = min(256, S)

    def body(a_ref, z_ref, o_ref):
        z = z_ref[...]
        o_ref[...] = (a_ref[...] * (z * _sigmoid(z))).astype(BF16)

    row = lambda i: (i, 0)
    return pl.pallas_call(
        body, name=name, out_shape=jax.ShapeDtypeStruct((S, D), BF16), grid=(S // tm,),
        in_specs=[pl.BlockSpec((tm, D), row)] * 2, out_specs=pl.BlockSpec((tm, D), row),
        compiler_params=_params("parallel"),
    )(att, z2)


def _bgate_bwd(dhb, att, z2, *, S, D, name):
    tm = min(256, S)

    def body(d_ref, a_ref, z_ref, da_ref, dz_ref):
        z, d = z_ref[...], d_ref[...]
        sz = _sigmoid(z)
        da_ref[...] = d * (z * sz)
        dz_ref[...] = (d * a_ref[...] * (sz * (1.0 + z * (1.0 - sz)))).astype(BF16)

    row = lambda i: (i, 0)
    return pl.pallas_call(
        body, name=name,
        out_shape=(jax.ShapeDtypeStruct((S, D), F32), jax.ShapeDtypeStruct((S, D), BF16)),
        grid=(S // tm,),
        in_specs=[pl.BlockSpec((tm, D), row)] * 3,
        out_specs=(pl.BlockSpec((tm, D), row), pl.BlockSpec((tm, D), row)),
        compiler_params=_params("parallel"),
    )(dhb, att, z2)


def _adamw_math(w, g, m, v):
    mn = ADAM_B1 * m + (1.0 - ADAM_B1) * g
    vn = ADAM_B2 * v + (1.0 - ADAM_B2) * (g * g)
    m_hat = mn / (1.0 - ADAM_B1 ** ADAM_STEP)
    v_hat = vn / (1.0 - ADAM_B2 ** ADAM_STEP)
    return -ADAM_LR * (m_hat / (jnp.sqrt(v_hat) + ADAM_EPS) + ADAM_WD * w), mn, vn


def _adamw(w, g, m, v, *, name, tr=128):
    rows, cols = w.shape
    tr = min(tr, rows)
    assert rows % tr == 0

    def body(w_ref, g_ref, m_ref, v_ref, d_ref, mo_ref, vo_ref):
        d, mn, vn = _adamw_math(w_ref[...], g_ref[...], m_ref[...], v_ref[...])
        d_ref[...] = d
        mo_ref[...] = mn
        vo_ref[...] = vn

    blk = pl.BlockSpec((tr, cols), lambda i: (i, 0))
    sd = jax.ShapeDtypeStruct((rows, cols), F32)
    return pl.pallas_call(
        body, name=name, out_shape=(sd, sd, sd), grid=(rows // tr,),
        in_specs=[blk] * 4, out_specs=(blk, blk, blk),
        compiler_params=_params("parallel"),
    )(w, g, m, v)


def _local_step(x, target, wa, wg, wao, wbi, wkv, wbo, gate_b, conv_w, conv_b, head_g,
                a_ln_g, a_ln_b, b_ln_g, b_ln_b, *, S, D, nha, rs):
    nb = S // LANES
    kw = dict(S=S, D=D)
    xb = x.astype(BF16)
    u = _mm(xb, wa, mode="nn", M=S, N=4 * D, K=D, out_dtype=F32, name="a_in")
    ug = _mm(xb, wg, mode="nn", M=S, N=LANES, K=D, out_dtype=F32, name="a_in_gates")
    qk = _conv_fwd(u, conv_w, conv_b, name="conv_fwd", **kw)
    gt = ug[:, :2 * nha].T.reshape(2 * nha * nb, LANES)
    gbias = jnp.repeat(gate_b.reshape(2 * nha), nb).reshape(2 * nha * nb, 1)
    fcs, brow = _gates_fwd(gt, gbias, nha=nha, nb=nb, name="gates_fwd")
    fcol = fcs.reshape(nha, S, 1)
    brow = brow.reshape(nha, 1, S)
    h, m, den = _mlstm_fwd(qk, u, fcol, brow, nha=nha, name="mlstm_fwd", **kw)
    hg = _hgate_fwd(h, u, head_g, name="hgate_fwd", **kw)
    y = _mm(hg, wao, mode="nn", M=S, N=D, K=D, out_dtype=F32, name="a_out")
    x1, x1b = _ln_fwd(x, y, a_ln_g, a_ln_b, name="ln_a_fwd", **kw)
    q2 = _mm(x1b, wbi, mode="nn", M=S, N=D, K=D, out_dtype=BF16, name="b_in_q")
    z2 = _mm(x1b, wbi, mode="nn", M=S, N=D, K=D, out_dtype=F32, name="b_in_z", b_off=(0, D))
    kv = _mm(x1b, wkv, mode="nn", M=S, N=2 * D, K=D, out_dtype=BF16, name="b_kv")
    att, sb_tot = _sb_fwd(q2, kv, name="sb_fwd", **kw)
    hb = _bgate_fwd(att, z2, name="bgate_fwd", **kw)
    y2 = _mm(hb, wbo, mode="nn", M=S, N=D, K=D, out_dtype=F32, name="b_out")
    dr2, dr2b, d_bln_g, d_bln_b, loss = _ln_loss_bwd(x1, y2, target, b_ln_g, b_ln_b, name="ln_b_loss", **kw)
    g_wbo = _mm(hb, dr2b, mode="tn", M=D, N=D, K=S, out_dtype=BF16, name="g_b_out")
    dhb = _mm(dr2b, wbo, mode="nt", M=S, N=D, K=D, out_dtype=F32, name="d_b_out")
    datt, dz2 = _bgate_bwd(dhb, att, z2, name="bgate_bwd", **kw)
    dq2, dk2, dv2 = _sb_bwd(q2, kv, datt, sb_tot, name="sb_bwd", **kw)
    g_wbi = _mm(x1b, dq2, mode="tn", M=D, N=D, K=S, out_dtype=BF16, name="g_b_in_q",
                into=lax.empty((D, 2 * D), BF16))
    g_wbi = _mm(x1b, dz2, mode="tn", M=D, N=D, K=S, out_dtype=BF16, name="g_b_in_z", into=g_wbi, out_off=(0, D))
    g_wkv = _mm(x1b, dk2, mode="tn", M=D, N=D, K=S, out_dtype=BF16, name="g_kv_k",
                into=lax.empty((D, 2 * D), BF16))
    g_wkv = _mm(x1b, dv2, mode="tn", M=D, N=D, K=S, out_dtype=BF16, name="g_kv_v", into=g_wkv, out_off=(0, D))
    dq2 = rs.begin("b", dict(wbo=g_wbo, wbi=g_wbi, wkv=g_wkv), dq2)
    dx1 = _mm(dq2, wbi, mode="nt", M=S, N=D, K=D, out_dtype=F32, name="d_b_in_q", acc_in=dr2, acc_scale=ALPHA)
    dx1 = _mm(dz2, wbi, mode="nt", M=S, N=D, K=D, out_dtype=F32, name="d_b_in_z", b_off=(0, D), acc_in=dx1)
    dx1 = _mm(dk2, wkv, mode="nt", M=S, N=D, K=D, out_dtype=F32, name="d_kv_k", acc_in=dx1)
    dx1 = _mm(dv2, wkv, mode="nt", M=S, N=D, K=D, out_dtype=F32, name="d_kv_v", b_off=(0, D), acc_in=dx1)
    dx1 = rs.pair("b", dx1)
    dr, drb, d_aln_g, d_aln_b = _ln_bwd(x, y, a_ln_g, dx1, name="ln_a_bwd", **kw)
    g_wao = _mm(hg, drb, mode="tn", M=D, N=D, K=S, out_dtype=BF16, name="g_a_out")
    drb = rs.begin("a", dict(wao=g_wao), drb)
    dhg = _mm(drb, wao, mode="nt", M=S, N=D, K=D, out_dtype=F32, name="d_a_out")
    du = lax.empty((S, 4 * D), BF16)
    du, dh, d_head_g = _hgate_bwd(dhg, h, u, head_g, du, name="hgate_bwd", **kw)
    dh = rs.pair("a", dh)
    du, dq, dk, rowsum, colsum = _mlstm_bwd(qk, u, fcol, brow, m, den, dh, h, du, nha=nha, name="mlstm_bwd", **kw)
    dq = rs.chip("a", rs.chip("b", dq))
    dgt, dgtot = _gates_bwd(rowsum.reshape(nha * nb, LANES), colsum.reshape(nha * nb, LANES), gt, gbias,
                            nha=nha, nb=nb, name="gates_bwd")
    d_gate_b = dgtot[::nb, 0].reshape(1, 2 * nha)
    dgp = jnp.pad(dgt.reshape(2 * nha, S).T, ((0, 0), (0, LANES - 2 * nha))).astype(BF16)
    du, d_conv_w, d_conv_b = _conv_bwd(u, dq, dk, conv_w, conv_b, du, name="conv_bwd", **kw)
    small = dict(gate_b=d_gate_b, conv_w=d_conv_w, conv_b=d_conv_b, head_g=d_head_g,
                 a_ln_g=d_aln_g, a_ln_b=d_aln_b, b_ln_g=d_bln_g, b_ln_b=d_bln_b)
    du = rs.small(small, loss, du)
    g_wa = _mm(xb, du, mode="tn", M=D, N=4 * D, K=S, out_dtype=BF16, name="g_a_in")
    g_wg = _mm(xb, dgp, mode="tn", M=D, N=LANES, K=S, out_dtype=BF16, name="g_a_in_gates")
    du = rs.begin("w", dict(wa=g_wa, wg=g_wg), du)
    du = rs.pair("w", rs.finish("a", rs.finish("b", du)))
    dx = _mm(du, wa, mode="nt", M=S, N=D, K=4 * D, out_dtype=F32, name="d_a_in", acc_in=dr, acc_scale=ALPHA)
    dx = _mm(dgp, wg, mode="nt", M=S, N=D, K=LANES, out_dtype=F32, name="d_a_in_gates", acc_in=dx)
    return dx


def _coords():
    return lax.axis_index("x"), lax.axis_index("y"), lax.axis_index("c")


def _other_chips(x, y):
    return [(1 - x, y), (x, 1 - y), (1 - x, 1 - y)]


def _rows(ref, start, size):
    return ref.at[pl.ds(pl.multiple_of(start, size), size), :]


def _window(kind, ref, shard_shape, j, hf=None):
    r, cw = shard_shape
    row0, nr = (0, r) if hf is None else (hf * (r // 2), r // 2)
    if kind == "stack":
        return ref.at[j, pl.ds(pl.multiple_of(row0, nr), nr), :]
    if kind == "rows":
        return ref.at[pl.ds(pl.multiple_of(j * r + row0, nr), nr), :]
    assert kind == "cols"
    return ref.at[pl.ds(pl.multiple_of(row0, nr), nr), pl.ds(pl.multiple_of(j * cw, cw), cw)]


def _gathered_shape(kind, shard_shape):
    r, cw = shard_shape
    return {"stack": (N_CHIPS, r, cw), "rows": (N_CHIPS * r, cw), "cols": (r, N_CHIPS * cw)}[kind]


def _remote(src, dst, send_sems, recv_sems, k, to):
    return pltpu.make_async_remote_copy(src_ref=src, dst_ref=dst, send_sem=send_sems.at[k],
                                        recv_sem=recv_sems.at[k], device_id=to, device_id_type=MESH)


def _allgather_copies(s_refs, sm_ref, g_refs, smg_ref, send_sems, recv_sems, kinds, shapes):
    n = len(s_refs)
    x, y, c = _coords()
    me, sibling, chips = 2 * x + y, (x, y, 1 - c), _other_chips(x, y)
    ids = [2 * cx + cy for cx, cy in chips]
    own_base = 6 * n + 3

    def small_win(j):
        w = sm_ref.shape[1]
        return smg_ref.at[:, pl.ds(pl.multiple_of(j * w, w), w)]

    first = [_remote(s_refs[a], _window(kinds[a], g_refs[a], shapes[a], me), send_sems, recv_sems,
                     own_base + a, sibling) for a in range(n)]
    if sm_ref is not None:
        first.append(_remote(sm_ref, small_win(me), send_sems, recv_sems, own_base + n, sibling))
    for a in range(n):
        src = _rows(s_refs[a], c * (shapes[a][0] // 2), shapes[a][0] // 2)
        dst = _window(kinds[a], g_refs[a], shapes[a], me, c)
        first += [_remote(src, dst, send_sems, recv_sems, 3 * a + k, (*chip, c)) for k, chip in enumerate(chips)]
    if sm_ref is not None:
        first += [_remote(sm_ref, small_win(me), send_sems, recv_sems, 6 * n + k, (*chip, c))
                  for k, chip in enumerate(chips)]
    for cp in first:
        cp.start()
    passed = []
    for a in range(n):
        for k in range(3):
            win = _window(kinds[a], g_refs[a], shapes[a], ids[k], c)
            _remote(win, win, send_sems, recv_sems, 3 * a + k, sibling).wait_recv()
            fwd = _remote(win, win, send_sems, recv_sems, 3 * n + 3 * a + k, sibling)
            fwd.start()
            passed.append(fwd)
    if sm_ref is not None:
        for k in range(3):
            win = small_win(ids[k])
            _remote(win, win, send_sems, recv_sems, 6 * n + k, sibling).wait_recv()
    for a in range(n):
        for k in range(3):
            win = _window(kinds[a], g_refs[a], shapes[a], ids[k], 1 - c)
            _remote(win, win, send_sems, recv_sems, 3 * n + 3 * a + k, sibling).wait_recv()
    for a in range(n):
        win = _window(kinds[a], g_refs[a], shapes[a], me)
        _remote(win, win, send_sems, recv_sems, own_base + a, sibling).wait_recv()
    if sm_ref is not None:
        _remote(small_win(me), small_win(me), send_sems, recv_sems, own_base + n, sibling).wait_recv()
    for cp in first + passed:
        cp.wait_send()


def _allgather_weights(shards, kinds, small):
    n = len(shards)
    shapes = [s.shape for s in shards]

    def body(*refs):
        _allgather_copies(refs[:n], refs[n], refs[n + 1:2 * n + 1], refs[2 * n + 1], refs[2 * n + 2],
                          refs[2 * n + 3], kinds, shapes)

    out_shape = [jax.ShapeDtypeStruct(_gathered_shape(kinds[a], shapes[a]), shards[a].dtype) for a in range(n)]
    out_shape.append(jax.ShapeDtypeStruct((small.shape[0], N_CHIPS * small.shape[1]), small.dtype))
    nsem = 7 * n + 4
    return pl.pallas_call(
        body, name="allgather_weights", out_shape=tuple(out_shape),
        in_specs=[ANY] * (n + 1), out_specs=tuple([ANY] * (n + 1)),
        scratch_shapes=[pltpu.SemaphoreType.DMA((nsem,)), pltpu.SemaphoreType.DMA((nsem,))],
    )(*shards, small)


def _peer_handshake(peers):
    barrier = pltpu.get_barrier_semaphore()
    for peer in peers:
        pl.semaphore_signal(barrier, inc=1, device_id=peer, device_id_type=MESH)
    pl.semaphore_wait(barrier, len(peers))


def _behind(copies, inputs, out_structs, nsem, peers, *, name, collective_id):
    hbm = pltpu.MemorySpace.HBM
    in_refs = [jax.new_ref(a, memory_space=hbm) for a in inputs]
    out_refs = [jax.empty_ref(st, memory_space=hbm) for st in out_structs]

    @pl.kernel(mesh=plsc.ScalarSubcoreMesh(axis_name="sequencer", num_cores=1), name=name,
               scratch_types=(pltpu.SemaphoreType.DMA((nsem,)), pltpu.SemaphoreType.DMA((nsem,))),
               compiler_params=pltpu.CompilerParams(collective_id=collective_id))
    def launch(send_sems, recv_sems):
        _peer_handshake(peers(*_coords()))
        copies(in_refs, out_refs, send_sems, recv_sems)

    launch()
    return [r[...] for r in out_refs]


def _sibling(x, y, c):
    return [(x, y, 1 - c)]


def _same_core_of_other_chips(x, y, c):
    return [(cx, cy, c) for cx, cy in _other_chips(x, y)]


def _sibling_and_other_chips(x, y, c):
    return _sibling(x, y, c) + _same_core_of_other_chips(x, y, c)


ID_ALL_PEERS, ID_SIBLING, ID_CHIPS = 0, 1, 2


def _allgather_behind(shards, kinds, *, name):
    n = len(shards)
    shapes = [sh.shape for sh in shards]

    def copies(s_refs, g_refs, send_sems, recv_sems):
        _allgather_copies(s_refs, None, g_refs, None, send_sems, recv_sems, kinds, shapes)

    outs = [jax.ShapeDtypeStruct(_gathered_shape(kinds[a], shapes[a]), shards[a].dtype) for a in range(n)]
    return _behind(copies, shards, outs, 7 * n + 4, _sibling_and_other_chips, name=name, collective_id=ID_ALL_PEERS)


def _rs_pair_exchange(views, *, name):
    n = len(views)

    def copies(g_refs, p_refs, send_sems, recv_sems):
        x, y, c = _coords()
        cps = []
        for a in range(n):
            r2 = views[a].shape[1] // 2
            src = g_refs[a].at[:, pl.ds(pl.multiple_of((1 - c) * r2, r2), r2), :]
            cps.append(_remote(src, p_refs[a], send_sems, recv_sems, a, (x, y, 1 - c)))
        for cp in cps:
            cp.start()
        for cp in cps:
            cp.wait()

    outs = [jax.ShapeDtypeStruct((v.shape[0], v.shape[1] // 2, v.shape[2]), v.dtype) for v in views]
    return _behind(copies, views, outs, n, _sibling, name=name, collective_id=ID_SIBLING)


def _add_half(view, part, core, *, name):
    nch, r, cols = view.shape
    r2 = r // 2
    tr = min(128, r2)
    nt = r2 // tr

    def body(c_ref, g_ref, p_ref, o_ref):
        del c_ref
        o_ref[...] = (g_ref[...].astype(F32) + p_ref[...].astype(F32)).astype(o_ref.dtype)

    return pl.pallas_call(
        body, name=name, out_shape=jax.ShapeDtypeStruct((nch, r2, cols), view.dtype),
        grid_spec=pltpu.PrefetchScalarGridSpec(
            num_scalar_prefetch=1, grid=(nch, nt),
            in_specs=[pl.BlockSpec((1, tr, cols), lambda ch, i, c_ref: (ch, c_ref[0] * nt + i, 0)),
                      pl.BlockSpec((1, tr, cols), lambda ch, i, c_ref: (ch, i, 0))],
            out_specs=pl.BlockSpec((1, tr, cols), lambda ch, i, c_ref: (ch, i, 0))),
        compiler_params=_params("parallel", "parallel"),
    )(core, view, part)


def _chunk(kind, ref, j, cw):
    if kind == "cols":
        return ref.at[0, :, pl.ds(pl.multiple_of(j * cw, cw), cw)]
    return ref.at[j]


def _rs_chips(pairs, kinds, *, name):
    n = len(pairs)
    half_shapes = []
    for a in range(n):
        nch, r2, cols = pairs[a].shape
        half_shapes.append((r2, cols // N_CHIPS) if kinds[a] == "cols" else (r2, cols))

    def copies(q_refs, t_refs, send_sems, recv_sems):
        x, y, c = _coords()
        chips = _other_chips(x, y)
        sends = []
        for a in range(n):
            for k, (cx, cy) in enumerate(chips):
                src = _chunk(kinds[a], q_refs[a], 2 * cx + cy, half_shapes[a][1])
                sends.append(_remote(src, t_refs[a].at[k], send_sems, recv_sems, 3 * a + k, (cx, cy, c)))
        for cp in sends:
            cp.start()
        for a in range(n):
            for k in range(3):
                slot = t_refs[a].at[k]
                _remote(slot, slot, send_sems, recv_sems, 3 * a + k, (x, y, c)).wait_recv()
        for cp in sends:
            cp.wait_send()

    outs = [jax.ShapeDtypeStruct((3, *half_shapes[a]), pairs[a].dtype) for a in range(n)]
    return _behind(copies, pairs, outs, 3 * n, _same_core_of_other_chips, name=name, collective_id=ID_CHIPS)


def _sum_chips(pair, parts, chip, kind, *, name):
    _, r, cols = parts.shape
    tr = min(128, r)
    if kind == "cols":
        own_spec = pl.BlockSpec((1, tr, cols), lambda i, chip_ref: (0, i, chip_ref[0]))
    else:
        own_spec = pl.BlockSpec((1, tr, cols), lambda i, chip_ref: (chip_ref[0], i, 0))

    def body(chip_ref, q_ref, p_ref, o_ref):
        del chip_ref
        acc = q_ref[0].astype(F32)
        for s in range(3):
            acc = acc + p_ref[s].astype(F32)
        o_ref[...] = acc

    return pl.pallas_call(
        body, name=name, out_shape=jax.ShapeDtypeStruct((r, cols), F32),
        grid_spec=pltpu.PrefetchScalarGridSpec(
            num_scalar_prefetch=1, grid=(r // tr,),
            in_specs=[own_spec, pl.BlockSpec((3, tr, cols), lambda i, chip_ref: (0, i, 0))],
            out_specs=pl.BlockSpec((tr, cols), lambda i, chip_ref: (i, 0))),
        compiler_params=_params("parallel"),
    )(chip, pair, parts)


def _rs_share(halves, *, name):
    n = len(halves)

    def copies(h_refs, g_refs, send_sems, recv_sems):
        x, y, c = _coords()
        cps = [_remote(h_refs[a], g_refs[a], send_sems, recv_sems, a, (x, y, 1 - c)) for a in range(n)]
        for cp in cps:
            cp.start()
        for cp in cps:
            cp.wait()

    outs = [jax.ShapeDtypeStruct(h.shape, h.dtype) for h in halves]
    return _behind(copies, halves, outs, n, _sibling, name=name, collective_id=ID_SIBLING)


class _GradReducer:
    def __init__(self, w, m, v, long_name, core, chip_ix, as_views, small_step):
        self.w, self.m, self.v, self.long_name = w, m, v, long_name
        self.core, self.chip_ix, self.as_views, self.small_step = core, chip_ix, as_views, small_step
        self.groups, self.results, self.small_results = {}, {}, None

    def small(self, small, loss_row, tie):
        (small, loss_row), tie = lax.optimization_barrier(((small, loss_row), tie))
        self.small_results, tie = lax.optimization_barrier((self.small_step(small, loss_row), tie))
        return tie

    def begin(self, tag, grads, tie):
        keys, views, kinds = self.as_views(grads)
        views, tie = lax.optimization_barrier((views, tie))
        parts = _rs_pair_exchange(views, name="rs_pair_exchange_" + tag)
        self.groups[tag] = dict(keys=keys, views=views, kinds=kinds, parts=parts)
        return tie

    def pair(self, tag, tie):
        g = self.groups[tag]
        g["parts"], tie = lax.optimization_barrier((g["parts"], tie))
        pairs = [_add_half(v, p, self.core, name="pair_sum_" + self.long_name[k])
                 for v, p, k in zip(g["views"], g["parts"], g["keys"])]
        pairs, tie = lax.optimization_barrier((pairs, tie))
        g["pairs"] = pairs
        g["slots"] = _rs_chips(pairs, g["kinds"], name="rs_chips_" + tag)
        return tie

    def chip(self, tag, tie):
        g = self.groups[tag]
        g["slots"], tie = lax.optimization_barrier((g["slots"], tie))
        halves = [_sum_chips(q, t, self.chip_ix, kd, name="chip_sum_" + self.long_name[k])
                  for q, t, kd, k in zip(g["pairs"], g["slots"], g["kinds"], g["keys"])]
        halves, tie = lax.optimization_barrier((halves, tie))
        g["halves"] = halves
        g["others"] = _rs_share(halves, name="rs_share_" + tag)
        return tie

    def finish(self, tag, tie):
        g = self.groups[tag]
        g["others"], tie = lax.optimization_barrier((g["others"], tie))
        out = [_adamw_halves(self.w[k], a, b, self.m[k], self.v[k], self.core, name="adamw_" + self.long_name[k])
               for k, a, b in zip(g["keys"], g["halves"], g["others"])]
        out, tie = lax.optimization_barrier((out, tie))
        self.results.update(zip(g["keys"], out))
        return tie


def _adamw_halves(w, mine, theirs, m, v, core, *, name, tr=128):
    rows, cols = w.shape
    hr = rows // 2
    tr = min(tr, hr)
    nt = hr // tr

    def body(c_ref, w_ref, a_ref, b_ref, m_ref, v_ref, g_ref, d_ref, mo_ref, vo_ref):
        gg = jnp.where(pl.program_id(0) == c_ref[0], a_ref[...], b_ref[...])
        g_ref[...] = gg
        d, mn, vn = _adamw_math(w_ref[...], gg, m_ref[...], v_ref[...])
        d_ref[...] = d
        mo_ref[...] = mn
        vo_ref[...] = vn

    full = pl.BlockSpec((tr, cols), lambda hf, i, c_ref: (hf * nt + i, 0))
    half = pl.BlockSpec((tr, cols), lambda hf, i, c_ref: (i, 0))
    sd = jax.ShapeDtypeStruct((rows, cols), F32)
    return pl.pallas_call(
        body, name=name, out_shape=(sd, sd, sd, sd),
        grid_spec=pltpu.PrefetchScalarGridSpec(
            num_scalar_prefetch=1, grid=(2, nt),
            in_specs=[full, half, half, full, full], out_specs=(full, full, full, full)),
        compiler_params=_params("parallel", "parallel"),
    )(core, w, mine, theirs, m, v)


def _allreduce_small(vec):
    m_per, ncol = vec.shape
    n_dev = 2 * N_CHIPS

    def body(x_ref, out_ref, sum_ref, send_sems, recv_sems, local_sem):
        x, y, c = _coords()
        me, sibling = (x, y, c), (x, y, 1 - c)
        chips = _other_chips(x, y)

        def rows(px, py, pc):
            return out_ref.at[pl.ds(pl.multiple_of((4 * px + 2 * py + pc) * m_per, m_per), m_per), :]

        def copy(k, block, to, src=None):
            return _remote(rows(*block) if src is None else src, rows(*block), send_sems, recv_sems, k, to)

        mine = pltpu.make_async_copy(x_ref, rows(*me), local_sem)
        mine.start()
        first = [copy(0, me, sibling, src=x_ref)]
        first += [copy(1 + j, me, (*chip, c), src=x_ref) for j, chip in enumerate(chips)]
        for cp in first:
            cp.start()
        passed = [copy(4 + j, (*chip, c), sibling) for j, chip in enumerate(chips)]
        for j, chip in enumerate(chips):
            copy(1 + j, (*chip, c), me).wait_recv()
            passed[j].start()
        copy(0, sibling, me).wait_recv()
        for j, chip in enumerate(chips):
            copy(4 + j, (*chip, 1 - c), me).wait_recv()
        for cp in first + passed:
            cp.wait_send()
        mine.wait()
        acc = out_ref[0:m_per, :]
        for d in range(1, n_dev):
            acc = acc + out_ref[d * m_per:(d + 1) * m_per, :]
        sum_ref[...] = acc

    vm = pl.BlockSpec(memory_space=pltpu.VMEM)
    return pl.pallas_call(
        body, name="allreduce_small",
        out_shape=(jax.ShapeDtypeStruct((n_dev * m_per, ncol), vec.dtype), jax.ShapeDtypeStruct((m_per, ncol), vec.dtype)),
        in_specs=[vm], out_specs=(vm, vm),
        scratch_shapes=[pltpu.SemaphoreType.DMA((7,)), pltpu.SemaphoreType.DMA((7,)), pltpu.SemaphoreType.DMA],
    )(vec)[1]


def _pack_rows(pieces, total_rows):
    rows = []
    for p in pieces:
        flat = p.reshape(-1)
        flat = jnp.pad(flat, (0, (-flat.shape[0]) % LANES))
        rows.append(flat.reshape(-1, LANES))
    out = jnp.concatenate(rows, axis=0)
    return jnp.pad(out, ((0, total_rows - out.shape[0]), (0, 0)))


def _unpack_rows(packed, shapes):
    out, r = [], 0
    for shp in shapes:
        size = 1
        for d in shp:
            size *= d
        nr = -(-size // LANES)
        out.append(packed[r:r + nr].reshape(-1)[:size].reshape(shp))
        r += nr
    return out


def _round_up(v, m):
    return -(-v // m) * m


def kernel(x, a_w_in, a_gate_b, a_conv_w, a_conv_b, a_head_g, a_w_out, a_ln_g, a_ln_b, kv_w, b_w_in, b_w_out, b_ln_g, b_ln_b, loss_target, m_a_w_in, m_a_gate_b, m_a_conv_w, m_a_conv_b, m_a_head_g, m_a_w_out, m_a_ln_g, m_a_ln_b, m_kv_w, m_b_w_in, m_b_w_out, m_b_ln_g, m_b_ln_b, v_a_w_in, v_a_gate_b, v_a_conv_w, v_a_conv_b, v_a_head_g, v_a_w_out, v_a_ln_g, v_a_ln_b, v_kv_w, v_b_w_in, v_b_w_out, v_b_ln_g, v_b_ln_b):
    _, S, D = x.shape
    nha = a_gate_b.shape[1] // 2
    chip = 2 * lax.axis_index("x") + lax.axis_index("y")
    core = lax.axis_index("c").astype(jnp.int32).reshape(1)
    dq = D // N_CHIPS

    shards = [a_w_in[0].astype(BF16), a_w_out[0].astype(BF16), kv_w.astype(BF16), b_w_in[0].astype(BF16),
              b_w_out[0].astype(BF16)]
    kinds = ["stack", "rows", "cols", "cols", "rows"]
    small_shard = jnp.concatenate([a_conv_w[0], a_conv_b, a_head_g, a_ln_g, a_ln_b], axis=0)
    wa_g, small_full = _allgather_weights(shards[:1], kinds[:1], small_shard)
    later, _ = lax.optimization_barrier((shards[1:], small_full))
    wao, wkv, wbi, wbo = _allgather_behind(later, kinds[1:], name="allgather_behind")
    wa_full = jnp.concatenate([wa_g[j] for j in range(N_CHIPS)], axis=1)
    wa = wa_full[:, :4 * D]
    wg = jnp.pad(wa_full[:, 4 * D:], ((0, 0), (0, LANES - 2 * nha)))
    conv_w, conv_b, head_g, ln_g_a, ln_b_a = (small_full[0:CONV_A], small_full[4:5], small_full[5:6],
                                              small_full[6:7], small_full[7:8])

    chip_ix = chip.astype(jnp.int32).reshape(1)
    w_big = dict(wa=a_w_in[0], wao=a_w_out[0], wkv=kv_w, wbi=b_w_in[0], wbo=b_w_out[0])
    m_big = dict(wa=m_a_w_in[0], wao=m_a_w_out[0], wkv=m_kv_w, wbi=m_b_w_in[0], wbo=m_b_w_out[0])
    v_big = dict(wa=v_a_w_in[0], wao=v_a_w_out[0], wkv=v_kv_w, wbi=v_b_w_in[0], wbo=v_b_w_out[0])
    long_name = dict(wa="a_w_in", wao="a_w_out", wkv="kv_w", wbi="b_w_in", wbo="b_w_out")

    def as_views(g):
        if "wa" in g:
            g_full = jnp.concatenate([g["wa"], g["wg"][:, :2 * nha]], axis=1)
            ca = g_full.shape[1] // N_CHIPS
            return ["wa"], [jnp.stack([g_full[:, j * ca:(j + 1) * ca] for j in range(N_CHIPS)], axis=0)], ["stack"]
        keys = list(g)
        views = [g[k].reshape(N_CHIPS, dq, D) if k in ("wao", "wbo") else g[k][None] for k in keys]
        return keys, views, ["stack" if k in ("wao", "wbo") else "cols" for k in keys]

    w_small = [a_gate_b, a_conv_w[0], a_conv_b, a_head_g, a_ln_g, a_ln_b, b_ln_g, b_ln_b]
    m_small = [m_a_gate_b, m_a_conv_w[0], m_a_conv_b, m_a_head_g, m_a_ln_g, m_a_ln_b, m_b_ln_g, m_b_ln_b]
    v_small = [v_a_gate_b, v_a_conv_w[0], v_a_conv_b, v_a_head_g, v_a_ln_g, v_a_ln_b, v_b_ln_g, v_b_ln_b]

    def small_step(small, loss_row):
        order = ["conv_w", "conv_b", "head_g", "a_ln_g", "a_ln_b", "b_ln_g", "b_ln_b", "gate_b"]
        full_shapes = [(CONV_A, D), (1, D), (1, D), (1, D), (1, D), (1, D), (1, D), (1, 2 * nha)]
        n_rows = sum(-(-(s[0] * s[1]) // LANES) for s in full_shapes) + 1
        total = _allreduce_small(_pack_rows([small[k] for k in order] + [loss_row], _round_up(n_rows, 8)))
        sums = dict(zip(order, _unpack_rows(total, full_shapes)))

        def mine(v):
            return lax.dynamic_slice_in_dim(v, chip * dq, dq, axis=1)

        g_small = [sums["gate_b"], mine(sums["conv_w"]), mine(sums["conv_b"]), mine(sums["head_g"]),
                   mine(sums["a_ln_g"]), mine(sums["a_ln_b"]), sums["b_ln_g"], sums["b_ln_b"]]
        rows_small = _round_up(sum(-(-(w.shape[0] * w.shape[1]) // LANES) for w in w_small), 8)
        upd_small = _adamw(_pack_rows(w_small, rows_small), _pack_rows(g_small, rows_small),
                           _pack_rows(m_small, rows_small), _pack_rows(v_small, rows_small), name="adamw_small")
        return total[n_rows - 1, 0], g_small, upd_small

    rs = _GradReducer(w_big, m_big, v_big, long_name, core, chip_ix, as_views, small_step)
    grad_x = _local_step(
        x[0], loss_target[0], wa, wg, wao, wbi, wkv, wbo, a_gate_b, conv_w, conv_b, head_g,
        ln_g_a, ln_b_a, b_ln_g, b_ln_b, S=S, D=D, nha=nha, rs=rs)
    grad_x = rs.finish("w", rs.chip("w", grad_x))
    upd_big = [rs.results[k] for k in ("wa", "wao", "wkv", "wbi", "wbo")]
    g_big = [u[0] for u in upd_big]
    loss, g_small, upd_small = rs.small_results
    d_small, mn_small, vn_small = (_unpack_rows(u, [w.shape for w in w_small]) for u in upd_small)

    def assemble(big5, small8):
        awi, awo, kvw, bwi, bwo = big5
        gb, cw, cb, hg, alg, alb, blg, blb = small8
        return [awi[None], gb, cw[None], cb, hg, awo[None], alg, alb, kvw, bwi[None], bwo[None], blg, blb]

    grads = assemble(g_big, g_small)
    deltas = assemble([u[1] for u in upd_big], d_small)
    new_m = assemble([u[2] for u in upd_big], mn_small)
    new_v = assemble([u[3] for u in upd_big], vn_small)
    return (loss, grad_x[None], *grads, *deltas, *new_m, *new_v)
```

```python
import functools

import jax
import jax.numpy as jnp
from jax import lax
from jax.experimental import pallas as pl
from jax.experimental.pallas import tpu as pltpu
from jax.experimental.pallas import tpu_sc as plsc

F32 = jnp.float32
BF16 = jnp.bfloat16

DEPTH = 2
ALPHA = (2.0 * DEPTH) ** 0.25
LN_EPS = 1e-5
DK_A = 128
DV_A = 256
DH_B = 128
SB_TQ = 512
SB_TK = 512
ML_TQ = 512
CONV_A = 4
ADAM_LR = 0.001
ADAM_B1 = 0.9
ADAM_B2 = 0.999
ADAM_EPS = 1e-08
ADAM_WD = 0.01
ADAM_STEP = 10
N_CHIPS = 4
LANES = 128
V7X_VMEM_BYTES = 64 * 1024 * 1024
VMEM_LIMIT = (V7X_VMEM_BYTES * 3) // 4
NEG_BIG = -1e30
MESH = pl.DeviceIdType.MESH
ANY = pl.BlockSpec(memory_space=pl.ANY)


def _params(*sem):
    return pltpu.CompilerParams(dimension_semantics=sem, vmem_limit_bytes=VMEM_LIMIT)


def _dot(a, b, dims):
    return lax.dot_general(a, b, (dims, ((), ())), preferred_element_type=F32)


def _dot_nn(a, b):
    return _dot(a, b, ((1,), (0,)))


def _dot_nt(a, b):
    return _dot(a, b, ((1,), (1,)))


def _dot_tn(a, b):
    return _dot(a, b, ((0,), (0,)))


def _split2(x):
    hi = x.astype(BF16)
    lo = (x - hi.astype(F32)).astype(BF16)
    return hi, lo


def _split3(x):
    hi = x.astype(BF16)
    r = x - hi.astype(F32)
    mid = r.astype(BF16)
    lo = (r - mid.astype(F32)).astype(BF16)
    return hi, mid, lo


def _mask_dot2(x, t01):
    hi, lo = _split2(x)
    return _dot_nn(hi, t01) + _dot_nn(lo, t01)


def _mask_dot3(x, t01):
    hi, mid, lo = _split3(x)
    return _dot_nn(hi, t01) + _dot_nn(mid, t01) + _dot_nn(lo, t01)


def _mask_dot3_left(t01, x):
    hi, mid, lo = _split3(x)
    return _dot_nn(t01, hi) + _dot_nn(t01, mid) + _dot_nn(t01, lo)


def _log_sigmoid(z):
    return jnp.minimum(z, 0.0) - jnp.log(1.0 + jnp.exp(-jnp.abs(z)))


def _sigmoid(z):
    return 1.0 / (1.0 + jnp.exp(-z))


def _mm(a, b, *, mode, M, N, K, out_dtype, name, tm=1024, tn=1024, tk=2048,
        a_off=(0, 0), b_off=(0, 0), acc_in=None, acc_scale=1.0, into=None, out_off=(0, 0)):
    tm, tn, tk = min(tm, M), min(tn, N), min(tk, K)
    assert M % tm == 0 and N % tn == 0 and K % tk == 0
    nk = K // tk
    ar, ac = a_off
    br, bc = b_off
    orow, ocol = out_off
    if mode in ("nn", "nt"):
        assert ar % tm == 0 and ac % tk == 0
        a_spec = pl.BlockSpec((tm, tk), lambda i, j, k: (i + ar // tm, k + ac // tk))
    else:
        assert ar % tk == 0 and ac % tm == 0
        a_spec = pl.BlockSpec((tk, tm), lambda i, j, k: (k + ar // tk, i + ac // tm))
    if mode == "nt":
        assert br % tn == 0 and bc % tk == 0
        b_spec = pl.BlockSpec((tn, tk), lambda i, j, k: (j + br // tn, k + bc // tk))
    else:
        assert br % tk == 0 and bc % tn == 0
        b_spec = pl.BlockSpec((tk, tn), lambda i, j, k: (k + br // tk, j + bc // tn))
    assert orow % tm == 0 and ocol % tn == 0
    o_spec = pl.BlockSpec((tm, tn), lambda i, j, k: (i + orow // tm, j + ocol // tn))
    dims = {"nn": ((1,), (0,)), "nt": ((1,), (1,)), "tn": ((0,), (0,))}[mode]
    inputs, in_specs = [a, b], [a_spec, b_spec]
    has_acc = acc_in is not None
    if has_acc:
        inputs.append(acc_in)
        in_specs.append(pl.BlockSpec((tm, tn), lambda i, j, k: (i, j)))
    aliases = {}
    if into is not None:
        inputs.append(into)
        in_specs.append(ANY)
        aliases = {len(inputs) - 1: 0}
        out_shape = jax.ShapeDtypeStruct(into.shape, into.dtype)
        assert into.dtype == out_dtype
    else:
        out_shape = jax.ShapeDtypeStruct((M, N), out_dtype)

    def body(*refs):
        a_ref, b_ref = refs[0], refs[1]
        acc_in_ref = refs[2] if has_acc else None
        n_in = len(inputs)
        o_ref = refs[n_in]

        def first():
            if has_acc:
                return acc_scale * acc_in_ref[...]
            return None

        if nk == 1:
            r = _dot(a_ref[...], b_ref[...], dims)
            f = first()
            if f is not None:
                r = r + f
            o_ref[...] = r.astype(o_ref.dtype)
        else:
            acc_ref = refs[n_in + 1]
            kk = pl.program_id(2)

            @pl.when(kk == 0)
            def _():
                f = first()
                acc_ref[...] = jnp.zeros_like(acc_ref) if f is None else f

            acc_ref[...] += _dot(a_ref[...], b_ref[...], dims)

            @pl.when(kk == nk - 1)
            def _():
                o_ref[...] = acc_ref[...].astype(o_ref.dtype)

    scratch = [] if nk == 1 else [pltpu.VMEM((tm, tn), F32)]
    return pl.pallas_call(
        body, name=name, out_shape=out_shape, grid=(M // tm, N // tn, nk),
        in_specs=in_specs, out_specs=o_spec, scratch_shapes=scratch,
        input_output_aliases=aliases,
        compiler_params=_params("parallel", "parallel", "arbitrary"),
    )(*inputs)


def _shift_down(x, d, row):
    if d == 0:
        return x
    return jnp.where(row >= d, pltpu.roll(x, d, 0), 0.0)


def _shift_up(x, d, row, n):
    if d == 0:
        return x
    return jnp.where(row < n - d, pltpu.roll(x, n - d, 0), 0.0)


def _conv_pre(x, w_ref, b_ref, row):
    c = b_ref[...] + w_ref[CONV_A - 1:CONV_A, :] * x
    for k in range(CONV_A - 1):
        c = c + w_ref[k:k + 1, :] * _shift_down(x, CONV_A - 1 - k, row)
    return c


def _conv_fwd(u, conv_w, conv_b, *, S, D, name):
    tc = 256
    nq_blocks = (D // 2) // tc

    def body(u_ref, w_ref, b_ref, o_ref):
        x = u_ref[...]
        row = lax.broadcasted_iota(jnp.int32, x.shape, 0)
        c = _conv_pre(x, w_ref, b_ref, row)
        scale = jnp.where(pl.program_id(0) >= nq_blocks, DK_A ** -0.5, 1.0).astype(F32)
        o_ref[...] = (c * _sigmoid(c) * scale).astype(BF16)

    return pl.pallas_call(
        body, name=name, out_shape=jax.ShapeDtypeStruct((S, D), BF16), grid=(D // tc,),
        in_specs=[pl.BlockSpec((S, tc), lambda j: (0, j)),
                  pl.BlockSpec((CONV_A, tc), lambda j: (0, j)),
                  pl.BlockSpec((1, tc), lambda j: (0, j))],
        out_specs=pl.BlockSpec((S, tc), lambda j: (0, j)),
        compiler_params=_params("parallel"),
    )(u, conv_w, conv_b)


def _conv_bwd(u, dq, dk, conv_w, conv_b, du, *, S, D, name):
    tc = 256
    nq_blocks = (D // 2) // tc

    def body(u_ref, dq_ref, dk_ref, w_ref, b_ref, du_in, du_ref, dw_ref, db_ref):
        del du_in
        x = u_ref[...]
        n = x.shape[0]
        row = lax.broadcasted_iota(jnp.int32, x.shape, 0)
        c = _conv_pre(x, w_ref, b_ref, row)
        is_k = pl.program_id(0) >= nq_blocks
        dy = jnp.where(is_k, dk_ref[...] * (DK_A ** -0.5), dq_ref[...])
        sg = _sigmoid(c)
        dc = dy * (sg * (1.0 + c * (1.0 - sg)))
        db_ref[...] = jnp.sum(dc, axis=0, keepdims=True)
        dx = w_ref[CONV_A - 1:CONV_A, :] * dc
        dw_ref[CONV_A - 1:CONV_A, :] = jnp.sum(dc * x, axis=0, keepdims=True)
        for k in range(CONV_A - 1):
            d = CONV_A - 1 - k
            dw_ref[k:k + 1, :] = jnp.sum(dc * _shift_down(x, d, row), axis=0, keepdims=True)
            dx = dx + w_ref[k:k + 1, :] * _shift_up(dc, d, row, n)
        du_ref[...] = dx.astype(BF16)

    half = lambda j: (0, j % nq_blocks)
    return pl.pallas_call(
        body, name=name,
        out_shape=(jax.ShapeDtypeStruct(du.shape, du.dtype),
                   jax.ShapeDtypeStruct((CONV_A, D), F32), jax.ShapeDtypeStruct((1, D), F32)),
        grid=(D // tc,),
        in_specs=[pl.BlockSpec((S, tc), lambda j: (0, j)),
                  pl.BlockSpec((S, tc), half), pl.BlockSpec((S, tc), half),
                  pl.BlockSpec((CONV_A, tc), lambda j: (0, j)),
                  pl.BlockSpec((1, tc), lambda j: (0, j)), ANY],
        out_specs=(pl.BlockSpec((S, tc), lambda j: (0, j)),
                   pl.BlockSpec((CONV_A, tc), lambda j: (0, j)),
                   pl.BlockSpec((1, tc), lambda j: (0, j))),
        input_output_aliases={5: 0},
        compiler_params=_params("parallel"),
    )(u, dq, dk, conv_w, conv_b, du)


def _tri(n, cmp):
    r = lax.broadcasted_iota(jnp.int32, (n, n), 0)
    c = lax.broadcasted_iota(jnp.int32, (n, n), 1)
    return r, c, cmp(r, c)


def _gates_fwd(gt, bias, *, nha, nb, name):
    half = nha * nb
    nb_shift = nb.bit_length() - 1
    assert nb == 1 << nb_shift

    def body(g_ref, b_ref, f_ref, brow_ref):
        ig = g_ref[0:half, :] + b_ref[0:half, :]
        fg = g_ref[half:2 * half, :] + b_ref[half:2 * half, :]
        lf = _log_sigmoid(fg)
        _, _, upper = _tri(LANES, lambda r, c: r <= c)
        cs = _mask_dot3(lf, upper.astype(BF16))
        tot = jnp.broadcast_to(cs[:, LANES - 1:LANES], cs.shape)
        r, c, _ = _tri(half, lambda r, c: r <= c)
        before = jnp.logical_and(r >> nb_shift == c >> nb_shift, c < r).astype(BF16)
        f = cs + _mask_dot3_left(before, tot)
        f_ref[...] = f
        brow_ref[...] = ig - f

    return pl.pallas_call(
        body, name=name,
        out_shape=(jax.ShapeDtypeStruct((half, LANES), F32), jax.ShapeDtypeStruct((half, LANES), F32)),
    )(gt, bias)


def _gates_bwd(rowsum, colsum, gt, bias, *, nha, nb, name):
    half = nha * nb
    nb_shift = nb.bit_length() - 1
    assert nb == 1 << nb_shift

    def body(rs_ref, cs_ref, g_ref, b_ref, dg_ref, tot_ref):
        col = cs_ref[...]
        df = rs_ref[...] - col
        _, _, lower = _tri(LANES, lambda r, c: r >= c)
        rc = _mask_dot3(df, lower.astype(BF16))
        tot = jnp.broadcast_to(rc[:, 0:1], rc.shape)
        r, c, _ = _tri(half, lambda r, c: r <= c)
        same = r >> nb_shift == c >> nb_shift
        after = jnp.logical_and(same, c > r).astype(BF16)
        dlf = rc + _mask_dot3_left(after, tot)
        fg = g_ref[half:2 * half, :] + b_ref[half:2 * half, :]
        dfg = dlf * _sigmoid(-fg)
        dg_ref[0:half, :] = col
        dg_ref[half:2 * half, :] = dfg
        grp = same.astype(BF16)
        ones = jnp.ones((LANES, LANES), BF16)
        tot_ref[0:half, :] = _mask_dot3_left(grp, _mask_dot3(col, ones))
        tot_ref[half:2 * half, :] = _mask_dot3_left(grp, _mask_dot3(dfg, ones))

    return pl.pallas_call(
        body, name=name,
        out_shape=(jax.ShapeDtypeStruct((2 * half, LANES), F32), jax.ShapeDtypeStruct((2 * half, LANES), F32)),
    )(rowsum, colsum, gt, bias)


def _mlstm_tile(q, k_ref, fcol, brow_ref, j, i, tq, m=None):
    off = pl.multiple_of(j * tq, tq)
    kj = k_ref[pl.ds(off, tq), :]
    s = _dot_nt(q, kj)
    row = lax.broadcasted_iota(jnp.int32, (tq, tq), 0)
    col = lax.broadcasted_iota(jnp.int32, (tq, tq), 1)
    valid = jnp.logical_or(col <= row, j < i)
    logd = jnp.where(valid, fcol + brow_ref[0, :, pl.ds(off, tq)], NEG_BIG)
    return off, kj, s, logd


def _mlstm_fwd(qk, u, fcol, brow, *, S, D, nha, name):
    tq = min(ML_TQ, S)
    nq = S // tq
    kb, vb = (D // 2) // DK_A, D // DV_A

    def body(q_ref, k_ref, v_ref, fcol_ref, brow_ref, h_ref, m_ref, den_ref):
        i = pl.program_id(1)
        q = q_ref[...]
        fc = fcol_ref[0]

        def step(j, carry):
            acc, den, m = carry
            off, _, s, logd = _mlstm_tile(q, k_ref, fc, brow_ref, j, i, tq)
            m_new = jnp.maximum(m, jnp.max(logd, axis=1, keepdims=True))
            a = s * jnp.exp(logd - m_new)
            alpha = jnp.exp(m - m_new)
            vj = v_ref[pl.ds(off, tq), :].astype(BF16)
            acc = alpha * acc + _dot_nn(a.astype(BF16), vj)
            den = alpha * den + jnp.sum(a, axis=1, keepdims=True)
            return acc, den, m_new

        acc, den, m = lax.fori_loop(
            0, i + 1, step,
            (jnp.zeros((tq, DV_A), F32), jnp.zeros((tq, 1), F32), jnp.full((tq, 1), NEG_BIG, F32)))
        h_ref[...] = acc / jnp.maximum(jnp.abs(den), jnp.exp(-m))
        m_ref[0] = m
        den_ref[0] = den

    stat = pl.BlockSpec((1, tq, 1), lambda h, i: (h, i, 0))
    return pl.pallas_call(
        body, name=name,
        out_shape=(jax.ShapeDtypeStruct((S, D), F32), jax.ShapeDtypeStruct((nha, S, 1), F32),
                   jax.ShapeDtypeStruct((nha, S, 1), F32)),
        grid=(nha, nq),
        in_specs=[pl.BlockSpec((tq, DK_A), lambda h, i: (i, h)),
                  pl.BlockSpec((S, DK_A), lambda h, i: (0, kb + h)),
                  pl.BlockSpec((S, DV_A), lambda h, i: (0, vb + h)),
                  stat, pl.BlockSpec((1, 1, S), lambda h, i: (h, 0, 0))],
        out_specs=(pl.BlockSpec((tq, DV_A), lambda h, i: (i, h)), stat, stat),
        compiler_params=_params("parallel", "arbitrary"),
    )(qk, qk, u, fcol, brow)


def _mlstm_bwd(qk, u, fcol, brow, m, den, dh, h, du, *, S, D, nha, name):
    tq = min(ML_TQ, S)
    nq = S // tq
    kb, vb = (D // 2) // DK_A, D // DV_A

    def body(q_ref, k_ref, v_ref, fcol_ref, brow_ref, m_ref, den_ref, dh_ref, h_ref, du_in,
             du_ref, dq_ref, dk_ref, rs_ref, cs_ref, dv_acc):
        del du_in
        i = pl.program_id(1)

        @pl.when(i == 0)
        def _():
            dk_ref[...] = jnp.zeros_like(dk_ref)
            cs_ref[...] = jnp.zeros_like(cs_ref)
            dv_acc[...] = jnp.zeros_like(dv_acc)

        q = q_ref[...]
        fc = fcol_ref[0]
        mm = m_ref[0]
        dn = den_ref[0]
        floor = jnp.exp(-mm)
        nrm = jnp.maximum(jnp.abs(dn), floor)
        dhv = dh_ref[...]
        dnum = dhv / nrm
        dnrm = -jnp.sum(dhv * h_ref[...], axis=1, keepdims=True) / nrm
        dden = jnp.where(jnp.abs(dn) > floor, jnp.where(dn > 0.0, dnrm, -dnrm), 0.0)
        dnum_b = dnum.astype(BF16)

        def step(j, carry):
            dq, rs = carry
            off, kj, s, logd = _mlstm_tile(q, k_ref, fc, brow_ref, j, i, tq)
            p = jnp.exp(logd - mm)
            a = s * p
            vj = v_ref[pl.ds(off, tq), :].astype(BF16)
            da = _dot_nt(dnum_b, vj) + dden
            dv_acc[pl.ds(off, tq), :] += _dot_tn(a.astype(BF16), dnum_b)
            dqk = (da * p).astype(BF16)
            dq = dq + _dot_nn(dqk, kj)
            dk_ref[pl.ds(off, tq), :] += _dot_tn(dqk, q)
            pm = da * a
            cs_ref[0, :, pl.ds(off, tq)] += jnp.sum(pm, axis=0, keepdims=True)
            rs = rs + jnp.sum(pm, axis=1, keepdims=True)
            return dq, rs

        dq, rs = lax.fori_loop(0, i + 1, step, (jnp.zeros((tq, DK_A), F32), jnp.zeros((tq, 1), F32)))
        dq_ref[...] = dq
        rs_ref[0] = rs

        @pl.when(i == nq - 1)
        def _():
            du_ref[...] = dv_acc[...].astype(BF16)

    stat = pl.BlockSpec((1, tq, 1), lambda h, i: (h, i, 0))
    rowv = pl.BlockSpec((1, 1, S), lambda h, i: (h, 0, 0))
    hblk = pl.BlockSpec((tq, DV_A), lambda h, i: (i, h))
    return pl.pallas_call(
        body, name=name,
        out_shape=(jax.ShapeDtypeStruct(du.shape, du.dtype),
                   jax.ShapeDtypeStruct((S, D // 2), F32), jax.ShapeDtypeStruct((S, D // 2), F32),
                   jax.ShapeDtypeStruct((nha, S, 1), F32), jax.ShapeDtypeStruct((nha, 1, S), F32)),
        grid=(nha, nq),
        in_specs=[pl.BlockSpec((tq, DK_A), lambda h, i: (i, h)),
                  pl.BlockSpec((S, DK_A), lambda h, i: (0, kb + h)),
                  pl.BlockSpec((S, DV_A), lambda h, i: (0, vb + h)),
                  stat, rowv, stat, stat, hblk, hblk, ANY],
        out_specs=(pl.BlockSpec((S, DV_A), lambda h, i: (0, vb + h)),
                   pl.BlockSpec((tq, DK_A), lambda h, i: (i, h)),
                   pl.BlockSpec((S, DK_A), lambda h, i: (0, h)),
                   stat, rowv),
        scratch_shapes=[pltpu.VMEM((S, DV_A), F32)],
        input_output_aliases={9: 0},
        compiler_params=_params("parallel", "arbitrary"),
    )(qk, qk, u, fcol, brow, m, den, dh, h, du)


def _head_norm(hh):
    mu = jnp.mean(hh, axis=1, keepdims=True)
    hc = hh - mu
    rstd = lax.rsqrt(jnp.mean(hc * hc, axis=1, keepdims=True) + LN_EPS)
    return hc * rstd, rstd


def _hgate_fwd(h, u, head_g, *, S, D, name):
    tm = min(256, S)
    nh = D // DV_A

    def body(h_ref, o_ref, z_ref, g_ref, out_ref):
        for hd in range(nh):
            sl = slice(hd * DV_A, (hd + 1) * DV_A)
            hn, _ = _head_norm(h_ref[:, sl])
            z = z_ref[:, sl]
            out_ref[:, sl] = (_sigmoid(o_ref[:, sl]) * (hn * g_ref[:, sl]) * (z * _sigmoid(z))).astype(BF16)

    return pl.pallas_call(
        body, name=name, out_shape=jax.ShapeDtypeStruct((S, D), BF16), grid=(S // tm,),
        in_specs=[pl.BlockSpec((tm, D), lambda i: (i, 0)), pl.BlockSpec((tm, D), lambda i: (i, 2)),
                  pl.BlockSpec((tm, D), lambda i: (i, 3)), pl.BlockSpec((1, D), lambda i: (0, 0))],
        out_specs=pl.BlockSpec((tm, D), lambda i: (i, 0)),
        compiler_params=_params("parallel"),
    )(h, u, u, head_g)


def _hgate_bwd(dhg, h, u, head_g, du, *, S, D, name):
    tm = min(256, S)
    nh = D // DV_A

    def body(dhg_ref, h_ref, o_ref, z_ref, g_ref, du_in, du_ref, dh_ref, dg_ref):
        del du_in

        @pl.when(pl.program_id(0) == 0)
        def _():
            dg_ref[...] = jnp.zeros_like(dg_ref)

        for hd in range(nh):
            sl = slice(hd * DV_A, (hd + 1) * DV_A)
            hn, rstd = _head_norm(h_ref[:, sl])
            o, z, g, d = o_ref[:, sl], z_ref[:, sl], g_ref[:, sl], dhg_ref[:, sl]
            so, sz = _sigmoid(o), _sigmoid(z)
            silu_z = z * sz
            hng = hn * g
            du_ref[:, sl] = (d * hng * silu_z * so * (1.0 - so)).astype(BF16)
            du_ref[:, D + hd * DV_A:D + (hd + 1) * DV_A] = (
                d * so * hng * (sz * (1.0 + z * (1.0 - sz)))).astype(BF16)
            t = d * so * silu_z
            dg_ref[:, sl] += jnp.sum(t * hn, axis=0, keepdims=True)
            dhn = t * g
            dh_ref[:, sl] = rstd * (dhn - jnp.mean(dhn, axis=1, keepdims=True)
                                    - hn * jnp.mean(dhn * hn, axis=1, keepdims=True))

    row = lambda i: (i, 0)
    return pl.pallas_call(
        body, name=name,
        out_shape=(jax.ShapeDtypeStruct(du.shape, du.dtype), jax.ShapeDtypeStruct((S, D), F32),
                   jax.ShapeDtypeStruct((1, D), F32)),
        grid=(S // tm,),
        in_specs=[pl.BlockSpec((tm, D), row), pl.BlockSpec((tm, D), row),
                  pl.BlockSpec((tm, D), lambda i: (i, 2)), pl.BlockSpec((tm, D), lambda i: (i, 3)),
                  pl.BlockSpec((1, D), lambda i: (0, 0)), ANY],
        out_specs=(pl.BlockSpec((tm, 2 * D), lambda i: (i, 1)), pl.BlockSpec((tm, D), row),
                   pl.BlockSpec((1, D), lambda i: (0, 0))),
        input_output_aliases={5: 0},
        compiler_params=_params("arbitrary"),
    )(dhg, h, u, u, head_g, du)


def _ln_stats(r):
    mu = jnp.mean(r, axis=1, keepdims=True)
    xc = r - mu
    rstd = lax.rsqrt(jnp.mean(xc * xc, axis=1, keepdims=True) + LN_EPS)
    return xc * rstd, rstd


def _ln_back(dxhat, xhat, rstd):
    return rstd * (dxhat - jnp.mean(dxhat, axis=1, keepdims=True)
                   - xhat * jnp.mean(dxhat * xhat, axis=1, keepdims=True))


def _ln_fwd(x, y, g, b, *, S, D, name):
    tm = min(256, S)

    def body(x_ref, y_ref, g_ref, b_ref, o_ref, ob_ref):
        xhat, _ = _ln_stats(ALPHA * x_ref[...] + y_ref[...])
        o = xhat * g_ref[...] + b_ref[...]
        o_ref[...] = o
        ob_ref[...] = o.astype(BF16)

    row = lambda i: (i, 0)
    vec = pl.BlockSpec((1, D), lambda i: (0, 0))
    return pl.pallas_call(
        body, name=name,
        out_shape=(jax.ShapeDtypeStruct((S, D), F32), jax.ShapeDtypeStruct((S, D), BF16)),
        grid=(S // tm,),
        in_specs=[pl.BlockSpec((tm, D), row), pl.BlockSpec((tm, D), row), vec, vec],
        out_specs=(pl.BlockSpec((tm, D), row), pl.BlockSpec((tm, D), row)),
        compiler_params=_params("parallel"),
    )(x, y, g, b)


def _ln_loss_bwd(x1, y2, target, g, b, *, S, D, name):
    tm = min(256, S)

    def body(x_ref, y_ref, t_ref, g_ref, b_ref, dr_ref, drb_ref, dg_ref, db_ref, loss_ref):
        @pl.when(pl.program_id(0) == 0)
        def _():
            dg_ref[...] = jnp.zeros_like(dg_ref)
            db_ref[...] = jnp.zeros_like(db_ref)
            loss_ref[...] = jnp.zeros_like(loss_ref)

        xhat, rstd = _ln_stats(ALPHA * x_ref[...] + y_ref[...])
        diff = xhat * g_ref[...] + b_ref[...] - t_ref[...]
        loss_ref[...] += (0.5 / D) * jnp.sum(diff * diff)
        dx2 = diff * (1.0 / D)
        dg_ref[...] += jnp.sum(dx2 * xhat, axis=0, keepdims=True)
        db_ref[...] += jnp.sum(dx2, axis=0, keepdims=True)
        dr = _ln_back(dx2 * g_ref[...], xhat, rstd)
        dr_ref[...] = dr
        drb_ref[...] = dr.astype(BF16)

    row = lambda i: (i, 0)
    vec = pl.BlockSpec((1, D), lambda i: (0, 0))
    return pl.pallas_call(
        body, name=name,
        out_shape=(jax.ShapeDtypeStruct((S, D), F32), jax.ShapeDtypeStruct((S, D), BF16),
                   jax.ShapeDtypeStruct((1, D), F32), jax.ShapeDtypeStruct((1, D), F32),
                   jax.ShapeDtypeStruct((1, LANES), F32)),
        grid=(S // tm,),
        in_specs=[pl.BlockSpec((tm, D), row)] * 3 + [vec, vec],
        out_specs=(pl.BlockSpec((tm, D), row), pl.BlockSpec((tm, D), row), vec, vec,
                   pl.BlockSpec((1, LANES), lambda i: (0, 0))),
        compiler_params=_params("arbitrary"),
    )(x1, y2, target, g, b)


def _ln_bwd(x, y, g, dout, *, S, D, name):
    tm = min(256, S)

    def body(x_ref, y_ref, g_ref, d_ref, dr_ref, drb_ref, dg_ref, db_ref):
        @pl.when(pl.program_id(0) == 0)
        def _():
            dg_ref[...] = jnp.zeros_like(dg_ref)
            db_ref[...] = jnp.zeros_like(db_ref)

        xhat, rstd = _ln_stats(ALPHA * x_ref[...] + y_ref[...])
        d = d_ref[...]
        dg_ref[...] += jnp.sum(d * xhat, axis=0, keepdims=True)
        db_ref[...] += jnp.sum(d, axis=0, keepdims=True)
        dr = _ln_back(d * g_ref[...], xhat, rstd)
        dr_ref[...] = dr
        drb_ref[...] = dr.astype(BF16)

    row = lambda i: (i, 0)
    vec = pl.BlockSpec((1, D), lambda i: (0, 0))
    return pl.pallas_call(
        body, name=name,
        out_shape=(jax.ShapeDtypeStruct((S, D), F32), jax.ShapeDtypeStruct((S, D), BF16),
                   jax.ShapeDtypeStruct((1, D), F32), jax.ShapeDtypeStruct((1, D), F32)),
        grid=(S // tm,),
        in_specs=[pl.BlockSpec((tm, D), row), pl.BlockSpec((tm, D), row), vec, pl.BlockSpec((tm, D), row)],
        out_specs=(pl.BlockSpec((tm, D), row), pl.BlockSpec((tm, D), row), vec, vec),
        compiler_params=_params("arbitrary"),
    )(x, y, g, dout)


def _sb_scores(q, kj, j, i, tq, tk):
    z = _dot_nt(q, kj) * (DH_B ** -0.5)
    row = lax.broadcasted_iota(jnp.int32, (tq, tk), 0) + i * tq
    col = lax.broadcasted_iota(jnp.int32, (tq, tk), 1) + j * tk
    valid = col < row
    ls = _log_sigmoid(z)
    lneg = jnp.where(valid, ls - z, 0.0)
    return valid, ls, lneg


def _sb_blocks(i, tq, tk):
    return ((i + 1) * tq + tk - 1) // tk


def _sb_fwd(q2, kv, *, S, D, name, tq=SB_TQ, tk=SB_TK):
    tq, tk = min(tq, S), min(tk, S)
    nq = S // tq
    nh = D // DH_B

    def body(q_ref, k_ref, v_ref, o_ref, tot_ref):
        i = pl.program_id(1)
        q = q_ref[...]
        _, _, after = _tri(tk, lambda r, c: r > c)
        tri_after = after.astype(BF16)
        nkv = _sb_blocks(i, tq, tk)

        def step(jj, carry):
            acc, cr = carry
            j = nkv - 1 - jj
            off = pl.multiple_of(j * tk, tk)
            valid, ls, lneg = _sb_scores(q, k_ref[pl.ds(off, tk), :], j, i, tq, tk)
            between = cr + _mask_dot2(lneg, tri_after)
            a = jnp.where(valid, jnp.exp(ls + between), 0.0)
            acc = acc + _dot_nn(a.astype(BF16), v_ref[pl.ds(off, tk), :])
            return acc, cr + jnp.sum(lneg, axis=1, keepdims=True)

        acc, cr = lax.fori_loop(0, nkv, step, (jnp.zeros((tq, DH_B), F32), jnp.zeros((tq, 1), F32)))
        o_ref[...] = acc
        tot_ref[0] = cr

    return pl.pallas_call(
        body, name=name,
        out_shape=(jax.ShapeDtypeStruct((S, D), F32), jax.ShapeDtypeStruct((nh, S, 1), F32)), grid=(nh, nq),
        in_specs=[pl.BlockSpec((tq, DH_B), lambda h, i: (i, h)),
                  pl.BlockSpec((S, DH_B), lambda h, i: (0, h)),
                  pl.BlockSpec((S, DH_B), lambda h, i: (0, nh + h))],
        out_specs=(pl.BlockSpec((tq, DH_B), lambda h, i: (i, h)), pl.BlockSpec((1, tq, 1), lambda h, i: (h, i, 0))),
        compiler_params=_params("parallel", "arbitrary"),
    )(q2, kv, kv)


def _sb_bwd(q2, kv, datt, tot, *, S, D, name, tq=SB_TQ, tk=SB_TK):
    tq, tk = min(tq, S), min(tk, S)
    nq = S // tq
    nh = D // DH_B

    def body(q_ref, k_ref, v_ref, do_ref, tot_ref, dq_ref, dk_ref, dv_ref, dk_acc, dv_acc):
        i = pl.program_id(1)

        @pl.when(i == 0)
        def _():
            dk_acc[...] = jnp.zeros_like(dk_acc)
            dv_acc[...] = jnp.zeros_like(dv_acc)

        q = q_ref[...]
        do_b = do_ref[...].astype(BF16)
        _, _, after = _tri(tk, lambda r, c: r > c)
        tri_after = after.astype(BF16)
        _, _, before = _tri(tk, lambda r, c: r < c)
        tri_before = before.astype(BF16)

        def step(j, carry):
            dq, rest, cg = carry
            off = pl.multiple_of(j * tk, tk)
            kj = k_ref[pl.ds(off, tk), :]
            valid, ls, lneg = _sb_scores(q, kj, j, i, tq, tk)
            rest = rest - jnp.sum(lneg, axis=1, keepdims=True)
            between = rest + _mask_dot2(lneg, tri_after)
            a = jnp.where(valid, jnp.exp(ls + between), 0.0)
            g = _dot_nt(do_b, v_ref[pl.ds(off, tk), :]) * a
            dv_acc[pl.ds(off, tk), :] += _dot_tn(a.astype(BF16), do_b)
            e = cg + _mask_dot2(g, tri_before)
            dz = jnp.where(valid, g * jnp.exp(lneg) - e * jnp.exp(ls), 0.0) * (DH_B ** -0.5)
            dz_b = dz.astype(BF16)
            dq = dq + _dot_nn(dz_b, kj)
            dk_acc[pl.ds(off, tk), :] += _dot_tn(dz_b, q)
            return dq, rest, cg + jnp.sum(g, axis=1, keepdims=True)

        dq, _, _ = lax.fori_loop(0, _sb_blocks(i, tq, tk), step,
                                 (jnp.zeros((tq, DH_B), F32), tot_ref[0], jnp.zeros((tq, 1), F32)))
        dq_ref[...] = dq.astype(BF16)

        @pl.when(i == nq - 1)
        def _():
            dk_ref[...] = dk_acc[...].astype(BF16)
            dv_ref[...] = dv_acc[...].astype(BF16)

    blk = pl.BlockSpec((tq, DH_B), lambda h, i: (i, h))
    return pl.pallas_call(
        body, name=name,
        out_shape=(jax.ShapeDtypeStruct((S, D), BF16), jax.ShapeDtypeStruct((S, D), BF16),
                   jax.ShapeDtypeStruct((S, D), BF16)),
        grid=(nh, nq),
        in_specs=[blk, pl.BlockSpec((S, DH_B), lambda h, i: (0, h)),
                  pl.BlockSpec((S, DH_B), lambda h, i: (0, nh + h)), blk,
                  pl.BlockSpec((1, tq, 1), lambda h, i: (h, i, 0))],
        out_specs=(blk, pl.BlockSpec((S, DH_B), lambda h, i: (0, h)),
                   pl.BlockSpec((S, DH_B), lambda h, i: (0, h))),
        scratch_shapes=[pltpu.VMEM((S, DH_B), F32), pltpu.VMEM((S, DH_B), F32)],
        compiler_params=_params("parallel", "arbitrary"),
    )(q2, kv, kv, datt, tot)


def _bgate_fwd(att, z2, *, S, D, name):
    tm = min(256, S)

    def body(a_ref, z_ref, o_ref):
        z = z_ref[...]
        o_ref[...] = (a_ref[...] * (z * _sigmoid(z))).astype(BF16)

    row = lambda i: (i, 0)
    return pl.pallas_call(
        body, name=name, out_shape=jax.ShapeDtypeStruct((S, D), BF16), grid=(S // tm,),
        in_specs=[pl.BlockSpec((tm, D), row)] * 2, out_specs=pl.BlockSpec((tm, D), row),
        compiler_params=_params("parallel"),
    )(att, z2)


def _bgate_bwd(dhb, att, z2, *, S, D, name):
    tm = min(256, S)

    def body(d_ref, a_ref, z_ref, da_ref, dz_ref):
        z, d = z_ref[...], d_ref[...]
        sz = _sigmoid(z)
        da_ref[...] = d * (z * sz)
        dz_ref[...] = (d * a_ref[...] * (sz * (1.0 + z * (1.0 - sz)))).astype(BF16)

    row = lambda i: (i, 0)
    return pl.pallas_call(
        body, name=name,
        out_shape=(jax.ShapeDtypeStruct((S, D), F32), jax.ShapeDtypeStruct((S, D), BF16)),
        grid=(S // tm,),
        in_specs=[pl.BlockSpec((tm, D), row)] * 3,
        out_specs=(pl.BlockSpec((tm, D), row), pl.BlockSpec((tm, D), row)),
        compiler_params=_params("parallel"),
    )(dhb, att, z2)


def _adamw_math(w, g, m, v):
    mn = ADAM_B1 * m + (1.0 - ADAM_B1) * g
    vn = ADAM_B2 * v + (1.0 - ADAM_B2) * (g * g)
    m_hat = mn / (1.0 - ADAM_B1 ** ADAM_STEP)
    v_hat = vn / (1.0 - ADAM_B2 ** ADAM_STEP)
    return -ADAM_LR * (m_hat / (jnp.sqrt(v_hat) + ADAM_EPS) + ADAM_WD * w), mn, vn


def _adamw(w, g, m, v, *, name, tr=128):
    rows, cols = w.shape
    tr = min(tr, rows)
    assert rows % tr == 0

    def body(w_ref, g_ref, m_ref, v_ref, d_ref, mo_ref, vo_ref):
        d, mn, vn = _adamw_math(w_ref[...], g_ref[...], m_ref[...], v_ref[...])
        d_ref[...] = d
        mo_ref[...] = mn
        vo_ref[...] = vn

    blk = pl.BlockSpec((tr, cols), lambda i: (i, 0))
    sd = jax.ShapeDtypeStruct((rows, cols), F32)
    return pl.pallas_call(
        body, name=name, out_shape=(sd, sd, sd), grid=(rows // tr,),
        in_specs=[blk] * 4, out_specs=(blk, blk, blk),
        compiler_params=_params("parallel"),
    )(w, g, m, v)


def _local_step(x, target, wa, wg, wao, wbi, wkv, wbo, gate_b, conv_w, conv_b, head_g,
                a_ln_g, a_ln_b, b_ln_g, b_ln_b, *, S, D, nha, rs):
    nb = S // LANES
    kw = dict(S=S, D=D)
    xb = x.astype(BF16)
    u = _mm(xb, wa, mode="nn", M=S, N=4 * D, K=D, out_dtype=F32, name="a_in")
    ug = _mm(xb, wg, mode="nn", M=S, N=LANES, K=D, out_dtype=F32, name="a_in_gates")
    qk = _conv_fwd(u, conv_w, conv_b, name="conv_fwd", **kw)
    gt = ug[:, :2 * nha].T.reshape(2 * nha * nb, LANES)
    gbias = jnp.repeat(gate_b.reshape(2 * nha), nb).reshape(2 * nha * nb, 1)
    fcs, brow = _gates_fwd(gt, gbias, nha=nha, nb=nb, name="gates_fwd")
    fcol = fcs.reshape(nha, S, 1)
    brow = brow.reshape(nha, 1, S)
    h, m, den = _mlstm_fwd(qk, u, fcol, brow, nha=nha, name="mlstm_fwd", **kw)
    hg = _hgate_fwd(h, u, head_g, name="hgate_fwd", **kw)
    y = _mm(hg, wao, mode="nn", M=S, N=D, K=D, out_dtype=F32, name="a_out")
    x1, x1b = _ln_fwd(x, y, a_ln_g, a_ln_b, name="ln_a_fwd", **kw)
    q2 = _mm(x1b, wbi, mode="nn", M=S, N=D, K=D, out_dtype=BF16, name="b_in_q")
    z2 = _mm(x1b, wbi, mode="nn", M=S, N=D, K=D, out_dtype=F32, name="b_in_z", b_off=(0, D))
    kv = _mm(x1b, wkv, mode="nn", M=S, N=2 * D, K=D, out_dtype=BF16, name="b_kv")
    att, sb_tot = _sb_fwd(q2, kv, name="sb_fwd", **kw)
    hb = _bgate_fwd(att, z2, name="bgate_fwd", **kw)
    y2 = _mm(hb, wbo, mode="nn", M=S, N=D, K=D, out_dtype=F32, name="b_out")
    dr2, dr2b, d_bln_g, d_bln_b, loss = _ln_loss_bwd(x1, y2, target, b_ln_g, b_ln_b, name="ln_b_loss", **kw)
    g_wbo = _mm(hb, dr2b, mode="tn", M=D, N=D, K=S, out_dtype=BF16, name="g_b_out")
    dhb = _mm(dr2b, wbo, mode="nt", M=S, N=D, K=D, out_dtype=F32, name="d_b_out")
    datt, dz2 = _bgate_bwd(dhb, att, z2, name="bgate_bwd", **kw)
    dq2, dk2, dv2 = _sb_bwd(q2, kv, datt, sb_tot, name="sb_bwd", **kw)
    g_wbi = _mm(x1b, dq2, mode="tn", M=D, N=D, K=S, out_dtype=BF16, name="g_b_in_q",
                into=lax.empty((D, 2 * D), BF16))
    g_wbi = _mm(x1b, dz2, mode="tn", M=D, N=D, K=S, out_dtype=BF16, name="g_b_in_z", into=g_wbi, out_off=(0, D))
    g_wkv = _mm(x1b, dk2, mode="tn", M=D, N=D, K=S, out_dtype=BF16, name="g_kv_k",
                into=lax.empty((D, 2 * D), BF16))
    g_wkv = _mm(x1b, dv2, mode="tn", M=D, N=D, K=S, out_dtype=BF16, name="g_kv_v", into=g_wkv, out_off=(0, D))
    dq2 = rs.begin("b", dict(wbo=g_wbo, wbi=g_wbi, wkv=g_wkv), dq2)
    dx1 = _mm(dq2, wbi, mode="nt", M=S, N=D, K=D, out_dtype=F32, name="d_b_in_q", acc_in=dr2, acc_scale=ALPHA)
    dx1 = _mm(dz2, wbi, mode="nt", M=S, N=D, K=D, out_dtype=F32, name="d_b_in_z", b_off=(0, D), acc_in=dx1)
    dx1 = _mm(dk2, wkv, mode="nt", M=S, N=D, K=D, out_dtype=F32, name="d_kv_k", acc_in=dx1)
    dx1 = _mm(dv2, wkv, mode="nt", M=S, N=D, K=D, out_dtype=F32, name="d_kv_v", b_off=(0, D), acc_in=dx1)
    dx1 = rs.pair("b", dx1)
    dr, drb, d_aln_g, d_aln_b = _ln_bwd(x, y, a_ln_g, dx1, name="ln_a_bwd", **kw)
    g_wao = _mm(hg, drb, mode="tn", M=D, N=D, K=S, out_dtype=BF16, name="g_a_out")
    drb = rs.begin("a", dict(wao=g_wao), drb)
    dhg = _mm(drb, wao, mode="nt", M=S, N=D, K=D, out_dtype=F32, name="d_a_out")
    du = lax.empty((S, 4 * D), BF16)
    du, dh, d_head_g = _hgate_bwd(dhg, h, u, head_g, du, name="hgate_bwd", **kw)
    dh = rs.pair("a", dh)
    du, dq, dk, rowsum, colsum = _mlstm_bwd(qk, u, fcol, brow, m, den, dh, h, du, nha=nha, name="mlstm_bwd", **kw)
    dgt, dgtot = _gates_bwd(rowsum.reshape(nha * nb, LANES), colsum.reshape(nha * nb, LANES), gt, gbias,
                            nha=nha, nb=nb, name="gates_bwd")
    d_gate_b = dgtot[::nb, 0].reshape(1, 2 * nha)
    dgp = jnp.pad(dgt.reshape(2 * nha, S).T, ((0, 0), (0, LANES - 2 * nha))).astype(BF16)
    du, d_conv_w, d_conv_b = _conv_bwd(u, dq, dk, conv_w, conv_b, du, name="conv_bwd", **kw)
    small = dict(gate_b=d_gate_b, conv_w=d_conv_w, conv_b=d_conv_b, head_g=d_head_g,
                 a_ln_g=d_aln_g, a_ln_b=d_aln_b, b_ln_g=d_bln_g, b_ln_b=d_bln_b)
    du = rs.chip("a", rs.chip("b", du))
    g_wa = _mm(xb, du, mode="tn", M=D, N=4 * D, K=S, out_dtype=BF16, name="g_a_in")
    g_wg = _mm(xb, dgp, mode="tn", M=D, N=LANES, K=S, out_dtype=BF16, name="g_a_in_gates")
    du = rs.begin("w", dict(wa=g_wa, wg=g_wg), du)
    du = rs.pair("w", rs.small(small, loss, du))
    du = rs.finish("a", rs.finish("b", du))
    dx = _mm(du, wa, mode="nt", M=S, N=D, K=4 * D, out_dtype=F32, name="d_a_in", acc_in=dr, acc_scale=ALPHA)
    dx = _mm(dgp, wg, mode="nt", M=S, N=D, K=LANES, out_dtype=F32, name="d_a_in_gates", acc_in=dx)
    return dx


def _coords():
    return lax.axis_index("x"), lax.axis_index("y"), lax.axis_index("c")


def _other_chips(x, y):
    return [(1 - x, y), (x, 1 - y), (1 - x, 1 - y)]


def _rows(ref, start, size):
    return ref.at[pl.ds(pl.multiple_of(start, size), size), :]


def _window(kind, ref, shard_shape, j, hf=None):
    r, cw = shard_shape
    row0, nr = (0, r) if hf is None else (hf * (r // 2), r // 2)
    if kind == "stack":
        return ref.at[j, pl.ds(pl.multiple_of(row0, nr), nr), :]
    if kind == "rows":
        return ref.at[pl.ds(pl.multiple_of(j * r + row0, nr), nr), :]
    assert kind == "cols"
    return ref.at[pl.ds(pl.multiple_of(row0, nr), nr), pl.ds(pl.multiple_of(j * cw, cw), cw)]


def _gathered_shape(kind, shard_shape):
    r, cw = shard_shape
    return {"stack": (N_CHIPS, r, cw), "rows": (N_CHIPS * r, cw), "cols": (r, N_CHIPS * cw)}[kind]


def _remote(src, dst, send_sems, recv_sems, k, to):
    return pltpu.make_async_remote_copy(src_ref=src, dst_ref=dst, send_sem=send_sems.at[k],
                                        recv_sem=recv_sems.at[k], device_id=to, device_id_type=MESH)


def _allgather_copies(s_refs, sm_ref, g_refs, smg_ref, send_sems, recv_sems, kinds, shapes):
    n = len(s_refs)
    x, y, c = _coords()
    me, sibling, chips = 2 * x + y, (x, y, 1 - c), _other_chips(x, y)
    ids = [2 * cx + cy for cx, cy in chips]
    own_base = 6 * n + 3

    def small_win(j):
        w = sm_ref.shape[1]
        return smg_ref.at[:, pl.ds(pl.multiple_of(j * w, w), w)]

    first = [_remote(s_refs[a], _window(kinds[a], g_refs[a], shapes[a], me), send_sems, recv_sems,
                     own_base + a, sibling) for a in range(n)]
    if sm_ref is not None:
        first.append(_remote(sm_ref, small_win(me), send_sems, recv_sems, own_base + n, sibling))
    for a in range(n):
        src = _rows(s_refs[a], c * (shapes[a][0] // 2), shapes[a][0] // 2)
        dst = _window(kinds[a], g_refs[a], shapes[a], me, c)
        first += [_remote(src, dst, send_sems, recv_sems, 3 * a + k, (*chip, c)) for k, chip in enumerate(chips)]
    if sm_ref is not None:
        first += [_remote(sm_ref, small_win(me), send_sems, recv_sems, 6 * n + k, (*chip, c))
                  for k, chip in enumerate(chips)]
    for cp in first:
        cp.start()
    passed = []
    for a in range(n):
        for k in range(3):
            win = _window(kinds[a], g_refs[a], shapes[a], ids[k], c)
            _remote(win, win, send_sems, recv_sems, 3 * a + k, sibling).wait_recv()
            fwd = _remote(win, win, send_sems, recv_sems, 3 * n + 3 * a + k, sibling)
            fwd.start()
            passed.append(fwd)
    if sm_ref is not None:
        for k in range(3):
            win = small_win(ids[k])
            _remote(win, win, send_sems, recv_sems, 6 * n + k, sibling).wait_recv()
    for a in range(n):
        for k in range(3):
            win = _window(kinds[a], g_refs[a], shapes[a], ids[k], 1 - c)
            _remote(win, win, send_sems, recv_sems, 3 * n + 3 * a + k, sibling).wait_recv()
    for a in range(n):
        win = _window(kinds[a], g_refs[a], shapes[a], me)
        _remote(win, win, send_sems, recv_sems, own_base + a, sibling).wait_recv()
    if sm_ref is not None:
        _remote(small_win(me), small_win(me), send_sems, recv_sems, own_base + n, sibling).wait_recv()
    for cp in first + passed:
        cp.wait_send()


def _allgather_weights(shards, kinds, small):
    n = len(shards)
    shapes = [s.shape for s in shards]

    def body(*refs):
        _allgather_copies(refs[:n], refs[n], refs[n + 1:2 * n + 1], refs[2 * n + 1], refs[2 * n + 2],
                          refs[2 * n + 3], kinds, shapes)

    out_shape = [jax.ShapeDtypeStruct(_gathered_shape(kinds[a], shapes[a]), shards[a].dtype) for a in range(n)]
    out_shape.append(jax.ShapeDtypeStruct((small.shape[0], N_CHIPS * small.shape[1]), small.dtype))
    nsem = 7 * n + 4
    return pl.pallas_call(
        body, name="allgather_weights", out_shape=tuple(out_shape),
        in_specs=[ANY] * (n + 1), out_specs=tuple([ANY] * (n + 1)),
        scratch_shapes=[pltpu.SemaphoreType.DMA((nsem,)), pltpu.SemaphoreType.DMA((nsem,))],
    )(*shards, small)


def _peer_handshake(peers):
    barrier = pltpu.get_barrier_semaphore()
    for peer in peers:
        pl.semaphore_signal(barrier, inc=1, device_id=peer, device_id_type=MESH)
    pl.semaphore_wait(barrier, len(peers))


def _behind(copies, inputs, out_structs, nsem, peers, *, name, collective_id):
    hbm = pltpu.MemorySpace.HBM
    in_refs = [jax.new_ref(a, memory_space=hbm) for a in inputs]
    out_refs = [jax.empty_ref(st, memory_space=hbm) for st in out_structs]

    @pl.kernel(mesh=plsc.ScalarSubcoreMesh(axis_name="sequencer", num_cores=1), name=name,
               scratch_types=(pltpu.SemaphoreType.DMA((nsem,)), pltpu.SemaphoreType.DMA((nsem,))),
               compiler_params=pltpu.CompilerParams(collective_id=collective_id))
    def launch(send_sems, recv_sems):
        _peer_handshake(peers(*_coords()))
        copies(in_refs, out_refs, send_sems, recv_sems)

    launch()
    return [r[...] for r in out_refs]


def _sibling(x, y, c):
    return [(x, y, 1 - c)]


def _same_core_of_other_chips(x, y, c):
    return [(cx, cy, c) for cx, cy in _other_chips(x, y)]


def _sibling_and_other_chips(x, y, c):
    return _sibling(x, y, c) + _same_core_of_other_chips(x, y, c)


ID_ALL_PEERS, ID_SIBLING, ID_CHIPS = 0, 1, 2


def _allgather_behind(shards, kinds, *, name):
    n = len(shards)
    shapes = [sh.shape for sh in shards]

    def copies(s_refs, g_refs, send_sems, recv_sems):
        _allgather_copies(s_refs, None, g_refs, None, send_sems, recv_sems, kinds, shapes)

    outs = [jax.ShapeDtypeStruct(_gathered_shape(kinds[a], shapes[a]), shards[a].dtype) for a in range(n)]
    return _behind(copies, shards, outs, 7 * n + 4, _sibling_and_other_chips, name=name, collective_id=ID_ALL_PEERS)


def _rs_pair_exchange(views, *, name):
    n = len(views)

    def copies(g_refs, p_refs, send_sems, recv_sems):
        x, y, c = _coords()
        cps = []
        for a in range(n):
            r2 = views[a].shape[1] // 2
            src = g_refs[a].at[:, pl.ds(pl.multiple_of((1 - c) * r2, r2), r2), :]
            cps.append(_remote(src, p_refs[a], send_sems, recv_sems, a, (x, y, 1 - c)))
        for cp in cps:
            cp.start()
        for cp in cps:
            cp.wait()

    outs = [jax.ShapeDtypeStruct((v.shape[0], v.shape[1] // 2, v.shape[2]), v.dtype) for v in views]
    return _behind(copies, views, outs, n, _sibling, name=name, collective_id=ID_SIBLING)


def _add_half(view, part, core, *, name):
    nch, r, cols = view.shape
    r2 = r // 2
    tr = min(128, r2)
    nt = r2 // tr

    def body(c_ref, g_ref, p_ref, o_ref):
        del c_ref
        o_ref[...] = (g_ref[...].astype(F32) + p_ref[...].astype(F32)).astype(o_ref.dtype)

    return pl.pallas_call(
        body, name=name, out_shape=jax.ShapeDtypeStruct((nch, r2, cols), view.dtype),
        grid_spec=pltpu.PrefetchScalarGridSpec(
            num_scalar_prefetch=1, grid=(nch, nt),
            in_specs=[pl.BlockSpec((1, tr, cols), lambda ch, i, c_ref: (ch, c_ref[0] * nt + i, 0)),
                      pl.BlockSpec((1, tr, cols), lambda ch, i, c_ref: (ch, i, 0))],
            out_specs=pl.BlockSpec((1, tr, cols), lambda ch, i, c_ref: (ch, i, 0))),
        compiler_params=_params("parallel", "parallel"),
    )(core, view, part)


def _chunk(kind, ref, j, cw):
    if kind == "cols":
        return ref.at[0, :, pl.ds(pl.multiple_of(j * cw, cw), cw)]
    return ref.at[j]


def _rs_chips(pairs, kinds, *, name):
    n = len(pairs)
    half_shapes = []
    for a in range(n):
        nch, r2, cols = pairs[a].shape
        half_shapes.append((r2, cols // N_CHIPS) if kinds[a] == "cols" else (r2, cols))

    def copies(q_refs, t_refs, send_sems, recv_sems):
        x, y, c = _coords()
        chips = _other_chips(x, y)
        sends = []
        for a in range(n):
            for k, (cx, cy) in enumerate(chips):
                src = _chunk(kinds[a], q_refs[a], 2 * cx + cy, half_shapes[a][1])
                sends.append(_remote(src, t_refs[a].at[k], send_sems, recv_sems, 3 * a + k, (cx, cy, c)))
        for cp in sends:
            cp.start()
        for a in range(n):
            for k in range(3):
                slot = t_refs[a].at[k]
                _remote(slot, slot, send_sems, recv_sems, 3 * a + k, (x, y, c)).wait_recv()
        for cp in sends:
            cp.wait_send()

    outs = [jax.ShapeDtypeStruct((3, *half_shapes[a]), pairs[a].dtype) for a in range(n)]
    return _behind(copies, pairs, outs, 3 * n, _same_core_of_other_chips, name=name, collective_id=ID_CHIPS)


def _sum_chips(pair, parts, chip, kind, *, name):
    _, r, cols = parts.shape
    tr = min(128, r)
    if kind == "cols":
        own_spec = pl.BlockSpec((1, tr, cols), lambda i, chip_ref: (0, i, chip_ref[0]))
    else:
        own_spec = pl.BlockSpec((1, tr, cols), lambda i, chip_ref: (chip_ref[0], i, 0))

    def body(chip_ref, q_ref, p_ref, o_ref):
        del chip_ref
        acc = q_ref[0].astype(F32)
        for s in range(3):
            acc = acc + p_ref[s].astype(F32)
        o_ref[...] = acc

    return pl.pallas_call(
        body, name=name, out_shape=jax.ShapeDtypeStruct((r, cols), F32),
        grid_spec=pltpu.PrefetchScalarGridSpec(
            num_scalar_prefetch=1, grid=(r // tr,),
            in_specs=[own_spec, pl.BlockSpec((3, tr, cols), lambda i, chip_ref: (0, i, 0))],
            out_specs=pl.BlockSpec((tr, cols), lambda i, chip_ref: (i, 0))),
        compiler_params=_params("parallel"),
    )(chip, pair, parts)


def _rs_share(halves, *, name):
    n = len(halves)

    def copies(h_refs, g_refs, send_sems, recv_sems):
        x, y, c = _coords()
        cps = [_remote(h_refs[a], g_refs[a], send_sems, recv_sems, a, (x, y, 1 - c)) for a in range(n)]
        for cp in cps:
            cp.start()
        for cp in cps:
            cp.wait()

    outs = [jax.ShapeDtypeStruct(h.shape, h.dtype) for h in halves]
    return _behind(copies, halves, outs, n, _sibling, name=name, collective_id=ID_SIBLING)


class _GradReducer:
    def __init__(self, w, m, v, long_name, core, chip_ix, as_views, small_step):
        self.w, self.m, self.v, self.long_name = w, m, v, long_name
        self.core, self.chip_ix, self.as_views, self.small_step = core, chip_ix, as_views, small_step
        self.groups, self.results, self.small_results = {}, {}, None

    def small(self, small, loss_row, tie):
        (small, loss_row), tie = lax.optimization_barrier(((small, loss_row), tie))
        self.small_results, tie = lax.optimization_barrier((self.small_step(small, loss_row), tie))
        return tie

    def begin(self, tag, grads, tie):
        keys, views, kinds = self.as_views(grads)
        views, tie = lax.optimization_barrier((views, tie))
        parts = _rs_pair_exchange(views, name="rs_pair_exchange_" + tag)
        self.groups[tag] = dict(keys=keys, views=views, kinds=kinds, parts=parts)
        return tie

    def pair(self, tag, tie):
        g = self.groups[tag]
        g["parts"], tie = lax.optimization_barrier((g["parts"], tie))
        pairs = [_add_half(v, p, self.core, name="pair_sum_" + self.long_name[k])
                 for v, p, k in zip(g["views"], g["parts"], g["keys"])]
        pairs, tie = lax.optimization_barrier((pairs, tie))
        g["pairs"] = pairs
        g["slots"] = _rs_chips(pairs, g["kinds"], name="rs_chips_" + tag)
        return tie

    def chip(self, tag, tie):
        g = self.groups[tag]
        g["slots"], tie = lax.optimization_barrier((g["slots"], tie))
        halves = [_sum_chips(q, t, self.chip_ix, kd, name="chip_sum_" + self.long_name[k])
                  for q, t, kd, k in zip(g["pairs"], g["slots"], g["kinds"], g["keys"])]
        halves, tie = lax.optimization_barrier((halves, tie))
        g["halves"] = halves
        g["others"] = _rs_share(halves, name="rs_share_" + tag)
        return tie

    def finish(self, tag, tie):
        g = self.groups[tag]
        g["others"], tie = lax.optimization_barrier((g["others"], tie))
        out = [_adamw_halves(self.w[k], a, b, self.m[k], self.v[k], self.core, name="adamw_" + self.long_name[k])
               for k, a, b in zip(g["keys"], g["halves"], g["others"])]
        out, tie = lax.optimization_barrier((out, tie))
        self.results.update(zip(g["keys"], out))
        return tie


def _adamw_halves(w, mine, theirs, m, v, core, *, name, tr=128):
    rows, cols = w.shape
    hr = rows // 2
    tr = min(tr, hr)
    nt = hr // tr

    def body(c_ref, w_ref, a_ref, b_ref, m_ref, v_ref, g_ref, d_ref, mo_ref, vo_ref):
        gg = jnp.where(pl.program_id(0) == c_ref[0], a_ref[...], b_ref[...])
        g_ref[...] = gg
        d, mn, vn = _adamw_math(w_ref[...], gg, m_ref[...], v_ref[...])
        d_ref[...] = d
        mo_ref[...] = mn
        vo_ref[...] = vn

    full = pl.BlockSpec((tr, cols), lambda hf, i, c_ref: (hf * nt + i, 0))
    half = pl.BlockSpec((tr, cols), lambda hf, i, c_ref: (i, 0))
    sd = jax.ShapeDtypeStruct((rows, cols), F32)
    return pl.pallas_call(
        body, name=name, out_shape=(sd, sd, sd, sd),
        grid_spec=pltpu.PrefetchScalarGridSpec(
            num_scalar_prefetch=1, grid=(2, nt),
            in_specs=[full, half, half, full, full], out_specs=(full, full, full, full)),
        compiler_params=_params("parallel", "parallel"),
    )(core, w, mine, theirs, m, v)


def _allreduce_small(vec):
    m_per, ncol = vec.shape
    n_dev = 2 * N_CHIPS

    def body(x_ref, out_ref, sum_ref, send_sems, recv_sems, local_sem):
        x, y, c = _coords()
        me, sibling = (x, y, c), (x, y, 1 - c)
        chips = _other_chips(x, y)

        def rows(px, py, pc):
            return out_ref.at[pl.ds(pl.multiple_of((4 * px + 2 * py + pc) * m_per, m_per), m_per), :]

        def copy(k, block, to, src=None):
            return _remote(rows(*block) if src is None else src, rows(*block), send_sems, recv_sems, k, to)

        mine = pltpu.make_async_copy(x_ref, rows(*me), local_sem)
        mine.start()
        first = [copy(0, me, sibling, src=x_ref)]
        first += [copy(1 + j, me, (*chip, c), src=x_ref) for j, chip in enumerate(chips)]
        for cp in first:
            cp.start()
        passed = [copy(4 + j, (*chip, c), sibling) for j, chip in enumerate(chips)]
        for j, chip in enumerate(chips):
            copy(1 + j, (*chip, c), me).wait_recv()
            passed[j].start()
        copy(0, sibling, me).wait_recv()
        for j, chip in enumerate(chips):
            copy(4 + j, (*chip, 1 - c), me).wait_recv()
        for cp in first + passed:
            cp.wait_send()
        mine.wait()
        acc = out_ref[0:m_per, :]
        for d in range(1, n_dev):
            acc = acc + out_ref[d * m_per:(d + 1) * m_per, :]
        sum_ref[...] = acc

    vm = pl.BlockSpec(memory_space=pltpu.VMEM)
    return pl.pallas_call(
        body, name="allreduce_small",
        out_shape=(jax.ShapeDtypeStruct((n_dev * m_per, ncol), vec.dtype), jax.ShapeDtypeStruct((m_per, ncol), vec.dtype)),
        in_specs=[vm], out_specs=(vm, vm),
        scratch_shapes=[pltpu.SemaphoreType.DMA((7,)), pltpu.SemaphoreType.DMA((7,)), pltpu.SemaphoreType.DMA],
    )(vec)[1]


def _pack_rows(pieces, total_rows):
    rows = []
    for p in pieces:
        flat = p.reshape(-1)
        flat = jnp.pad(flat, (0, (-flat.shape[0]) % LANES))
        rows.append(flat.reshape(-1, LANES))
    out = jnp.concatenate(rows, axis=0)
    return jnp.pad(out, ((0, total_rows - out.shape[0]), (0, 0)))


def _unpack_rows(packed, shapes):
    out, r = [], 0
    for shp in shapes:
        size = 1
        for d in shp:
            size *= d
        nr = -(-size // LANES)
        out.append(packed[r:r + nr].reshape(-1)[:size].reshape(shp))
        r += nr
    return out


def _round_up(v, m):
    return -(-v // m) * m


def kernel(x, a_w_in, a_gate_b, a_conv_w, a_conv_b, a_head_g, a_w_out, a_ln_g, a_ln_b, kv_w, b_w_in, b_w_out, b_ln_g, b_ln_b, loss_target, m_a_w_in, m_a_gate_b, m_a_conv_w, m_a_conv_b, m_a_head_g, m_a_w_out, m_a_ln_g, m_a_ln_b, m_kv_w, m_b_w_in, m_b_w_out, m_b_ln_g, m_b_ln_b, v_a_w_in, v_a_gate_b, v_a_conv_w, v_a_conv_b, v_a_head_g, v_a_w_out, v_a_ln_g, v_a_ln_b, v_kv_w, v_b_w_in, v_b_w_out, v_b_ln_g, v_b_ln_b):
    _, S, D = x.shape
    nha = a_gate_b.shape[1] // 2
    chip = 2 * lax.axis_index("x") + lax.axis_index("y")
    core = lax.axis_index("c").astype(jnp.int32).reshape(1)
    dq = D // N_CHIPS

    shards = [a_w_in[0].astype(BF16), a_w_out[0].astype(BF16), kv_w.astype(BF16), b_w_in[0].astype(BF16),
              b_w_out[0].astype(BF16)]
    kinds = ["stack", "rows", "cols", "cols", "rows"]
    small_shard = jnp.concatenate([a_conv_w[0], a_conv_b, a_head_g, a_ln_g, a_ln_b], axis=0)
    wa_g, small_full = _allgather_weights(shards[:1], kinds[:1], small_shard)
    later, _ = lax.optimization_barrier((shards[1:], small_full))
    wao, wkv, wbi, wbo = _allgather_behind(later, kinds[1:], name="allgather_behind")
    wa_full = jnp.concatenate([wa_g[j] for j in range(N_CHIPS)], axis=1)
    wa = wa_full
    wg = jnp.pad(wa_full[:, 4 * D:], ((0, 0), (0, LANES - 2 * nha)))
    conv_w, conv_b, head_g, ln_g_a, ln_b_a = (small_full[0:CONV_A], small_full[4:5], small_full[5:6],
                                              small_full[6:7], small_full[7:8])

    chip_ix = chip.astype(jnp.int32).reshape(1)
    w_big = dict(wa=a_w_in[0], wao=a_w_out[0], wkv=kv_w, wbi=b_w_in[0], wbo=b_w_out[0])
    m_big = dict(wa=m_a_w_in[0], wao=m_a_w_out[0], wkv=m_kv_w, wbi=m_b_w_in[0], wbo=m_b_w_out[0])
    v_big = dict(wa=v_a_w_in[0], wao=v_a_w_out[0], wkv=v_kv_w, wbi=v_b_w_in[0], wbo=v_b_w_out[0])
    long_name = dict(wa="a_w_in", wao="a_w_out", wkv="kv_w", wbi="b_w_in", wbo="b_w_out")

    def as_views(g):
        if "wa" in g:
            g_full = jnp.concatenate([g["wa"], g["wg"][:, :2 * nha]], axis=1)
            ca = g_full.shape[1] // N_CHIPS
            return ["wa"], [jnp.stack([g_full[:, j * ca:(j + 1) * ca] for j in range(N_CHIPS)], axis=0)], ["stack"]
        keys = list(g)
        views = [g[k].reshape(N_CHIPS, dq, D) if k in ("wao", "wbo") else g[k][None] for k in keys]
        return keys, views, ["stack" if k in ("wao", "wbo") else "cols" for k in keys]

    w_small = [a_gate_b, a_conv_w[0], a_conv_b, a_head_g, a_ln_g, a_ln_b, b_ln_g, b_ln_b]
    m_small = [m_a_gate_b, m_a_conv_w[0], m_a_conv_b, m_a_head_g, m_a_ln_g, m_a_ln_b, m_b_ln_g, m_b_ln_b]
    v_small = [v_a_gate_b, v_a_conv_w[0], v_a_conv_b, v_a_head_g, v_a_ln_g, v_a_ln_b, v_b_ln_g, v_b_ln_b]

    def small_step(small, loss_row):
        order = ["conv_w", "conv_b", "head_g", "a_ln_g", "a_ln_b", "b_ln_g", "b_ln_b", "gate_b"]
        full_shapes = [(CONV_A, D), (1, D), (1, D), (1, D), (1, D), (1, D), (1, D), (1, 2 * nha)]
        n_rows = sum(-(-(s[0] * s[1]) // LANES) for s in full_shapes) + 1
        total = _allreduce_small(_pack_rows([small[k] for k in order] + [loss_row], _round_up(n_rows, 8)))
        sums = dict(zip(order, _unpack_rows(total, full_shapes)))

        def mine(v):
            return lax.dynamic_slice_in_dim(v, chip * dq, dq, axis=1)

        g_small = [sums["gate_b"], mine(sums["conv_w"]), mine(sums["conv_b"]), mine(sums["head_g"]),
                   mine(sums["a_ln_g"]), mine(sums["a_ln_b"]), sums["b_ln_g"], sums["b_ln_b"]]
        rows_small = _round_up(sum(-(-(w.shape[0] * w.shape[1]) // LANES) for w in w_small), 8)
        upd_small = _adamw(_pack_rows(w_small, rows_small), _pack_rows(g_small, rows_small),
                           _pack_rows(m_small, rows_small), _pack_rows(v_small, rows_small), name="adamw_small")
        return total[n_rows - 1, 0], g_small, upd_small

    rs = _GradReducer(w_big, m_big, v_big, long_name, core, chip_ix, as_views, small_step)
    grad_x = _local_step(
        x[0], loss_target[0], wa, wg, wao, wbi, wkv, wbo, a_gate_b, conv_w, conv_b, head_g,
        ln_g_a, ln_b_a, b_ln_g, b_ln_b, S=S, D=D, nha=nha, rs=rs)
    grad_x = rs.finish("w", rs.chip("w", grad_x))
    upd_big = [rs.results[k] for k in ("wa", "wao", "wkv", "wbi", "wbo")]
    g_big = [u[0] for u in upd_big]
    loss, g_small, upd_small = rs.small_results
    d_small, mn_small, vn_small = (_unpack_rows(u, [w.shape for w in w_small]) for u in upd_small)

    def assemble(big5, small8):
        awi, awo, kvw, bwi, bwo = big5
        gb, cw, cb, hg, alg, alb, blg, blb = small8
        return [awi[None], gb, cw[None], cb, hg, awo[None], alg, alb, kvw, bwi[None], bwo[None], blg, blb]

    grads = assemble(g_big, g_small)
    deltas = assemble([u[1] for u in upd_big], d_small)
    new_m = assemble([u[2] for u in upd_big], mn_small)
    new_v = assemble([u[3] for u in upd_big], vn_small)
    return (loss, grad_x[None], *grads, *deltas, *new_m, *new_v)
```

```python
import functools

import jax
import jax.numpy as jnp
from jax import lax
from jax.experimental import pallas as pl
from jax.experimental.pallas import tpu as pltpu
from jax.experimental.pallas import tpu_sc as plsc

F32 = jnp.float32
BF16 = jnp.bfloat16

DEPTH = 2
ALPHA = (2.0 * DEPTH) ** 0.25
LN_EPS = 1e-5
DK_A = 128
DV_A = 256
DH_B = 128
SB_TILE = 512
ML_TQ = 512
CONV_A = 4
ADAM_LR = 0.001
ADAM_B1 = 0.9
ADAM_B2 = 0.999
ADAM_EPS = 1e-08
ADAM_WD = 0.01
ADAM_STEP = 10
N_CHIPS = 4
LANES = 128
MXU_DEPTH = 256
V7X_VMEM_BYTES = 64 * 1024 * 1024
VMEM_LIMIT = (V7X_VMEM_BYTES * 3) // 4
NEG_BIG = -1e30
MESH = pl.DeviceIdType.MESH
ANY = pl.BlockSpec(memory_space=pl.ANY)


def _params(*sem):
    return pltpu.CompilerParams(dimension_semantics=sem, vmem_limit_bytes=VMEM_LIMIT)


def _dot(a, b, dims):
    return lax.dot_general(a, b, (dims, ((), ())), preferred_element_type=F32)


def _dot_nn(a, b):
    return _dot(a, b, ((1,), (0,)))


def _dot_nt(a, b):
    return _dot(a, b, ((1,), (1,)))


def _dot_tn(a, b):
    return _dot(a, b, ((0,), (0,)))


def _split2(x):
    hi = x.astype(BF16)
    lo = (x - hi.astype(F32)).astype(BF16)
    return hi, lo


def _split3(x):
    hi = x.astype(BF16)
    r = x - hi.astype(F32)
    mid = r.astype(BF16)
    lo = (r - mid.astype(F32)).astype(BF16)
    return hi, mid, lo


def _mask_dot2(x, t01):
    hi, lo = _split2(x)
    return _dot_nn(hi, t01) + _dot_nn(lo, t01)


def _block_sums(x, tri, later):
    sub = tri.shape[0]
    n = x.shape[1] // sub
    if n == 1:
        return _mask_dot2(x, tri)
    parts = [x[:, b * sub:(b + 1) * sub] for b in range(n)]
    sums = [jnp.sum(p, axis=1, keepdims=True) for p in parts]
    out = []
    for b in range(n):
        acc = _mask_dot2(parts[b], tri)
        for o in (range(b + 1, n) if later else range(b)):
            acc = acc + sums[o]
        out.append(acc)
    return jnp.concatenate(out, axis=1)


def _mask_dot3(x, t01):
    hi, mid, lo = _split3(x)
    return _dot_nn(hi, t01) + _dot_nn(mid, t01) + _dot_nn(lo, t01)


def _mask_dot3_left(t01, x):
    hi, mid, lo = _split3(x)
    return _dot_nn(t01, hi) + _dot_nn(t01, mid) + _dot_nn(t01, lo)


def _log_sigmoid(z):
    return jnp.minimum(z, 0.0) - jnp.log(1.0 + jnp.exp(-jnp.abs(z)))


def _sigmoid(z):
    return 1.0 / (1.0 + jnp.exp(-z))


def _mm(a, b, *, mode, M, N, K, out_dtype, name, tm=1024, tn=1024, tk=2048,
        a_off=(0, 0), b_off=(0, 0), acc_in=None, acc_scale=1.0, into=None, out_off=(0, 0)):
    tm, tn, tk = min(tm, M), min(tn, N), min(tk, K)
    assert M % tm == 0 and N % tn == 0 and K % tk == 0
    nk = K // tk
    ar, ac = a_off
    br, bc = b_off
    orow, ocol = out_off
    if mode in ("nn", "nt"):
        assert ar % tm == 0 and ac % tk == 0
        a_spec = pl.BlockSpec((tm, tk), lambda i, j, k: (i + ar // tm, k + ac // tk))
    else:
        assert ar % tk == 0 and ac % tm == 0
        a_spec = pl.BlockSpec((tk, tm), lambda i, j, k: (k + ar // tk, i + ac // tm))
    if mode == "nt":
        assert br % tn == 0 and bc % tk == 0
        b_spec = pl.BlockSpec((tn, tk), lambda i, j, k: (j + br // tn, k + bc // tk))
    else:
        assert br % tk == 0 and bc % tn == 0
        b_spec = pl.BlockSpec((tk, tn), lambda i, j, k: (k + br // tk, j + bc // tn))
    assert orow % tm == 0 and ocol % tn == 0
    o_spec = pl.BlockSpec((tm, tn), lambda i, j, k: (i + orow // tm, j + ocol // tn))
    dims = {"nn": ((1,), (0,)), "nt": ((1,), (1,)), "tn": ((0,), (0,))}[mode]
    inputs, in_specs = [a, b], [a_spec, b_spec]
    has_acc = acc_in is not None
    if has_acc:
        inputs.append(acc_in)
        in_specs.append(pl.BlockSpec((tm, tn), lambda i, j, k: (i, j)))
    aliases = {}
    if into is not None:
        inputs.append(into)
        in_specs.append(ANY)
        aliases = {len(inputs) - 1: 0}
        out_shape = jax.ShapeDtypeStruct(into.shape, into.dtype)
        assert into.dtype == out_dtype
    else:
        out_shape = jax.ShapeDtypeStruct((M, N), out_dtype)

    def body(*refs):
        a_ref, b_ref = refs[0], refs[1]
        acc_in_ref = refs[2] if has_acc else None
        n_in = len(inputs)
        o_ref = refs[n_in]

        def first():
            if has_acc:
                return acc_scale * acc_in_ref[...]
            return None

        if nk == 1:
            r = _dot(a_ref[...], b_ref[...], dims)
            f = first()
            if f is not None:
                r = r + f
            o_ref[...] = r.astype(o_ref.dtype)
        else:
            acc_ref = refs[n_in + 1]
            kk = pl.program_id(2)

            @pl.when(kk == 0)
            def _():
                f = first()
                acc_ref[...] = jnp.zeros_like(acc_ref) if f is None else f

            acc_ref[...] += _dot(a_ref[...], b_ref[...], dims)

            @pl.when(kk == nk - 1)
            def _():
                o_ref[...] = acc_ref[...].astype(o_ref.dtype)

    scratch = [] if nk == 1 else [pltpu.VMEM((tm, tn), F32)]
    return pl.pallas_call(
        body, name=name, out_shape=out_shape, grid=(M // tm, N // tn, nk),
        in_specs=in_specs, out_specs=o_spec, scratch_shapes=scratch,
        input_output_aliases=aliases,
        compiler_params=_params("parallel", "parallel", "arbitrary"),
    )(*inputs)


def _shift_down(x, d, row):
    if d == 0:
        return x
    return jnp.where(row >= d, pltpu.roll(x, d, 0), 0.0)


def _shift_up(x, d, row, n):
    if d == 0:
        return x
    return jnp.where(row < n - d, pltpu.roll(x, n - d, 0), 0.0)


def _conv_pre(x, w_ref, b_ref, row):
    c = b_ref[...] + w_ref[CONV_A - 1:CONV_A, :] * x
    for k in range(CONV_A - 1):
        c = c + w_ref[k:k + 1, :] * _shift_down(x, CONV_A - 1 - k, row)
    return c


def _conv_fwd(u, conv_w, conv_b, *, S, D, name):
    tc = 256
    nq_blocks = (D // 2) // tc

    def body(u_ref, w_ref, b_ref, o_ref):
        x = u_ref[...]
        row = lax.broadcasted_iota(jnp.int32, x.shape, 0)
        c = _conv_pre(x, w_ref, b_ref, row)
        scale = jnp.where(pl.program_id(0) >= nq_blocks, DK_A ** -0.5, 1.0).astype(F32)
        o_ref[...] = (c * _sigmoid(c) * scale).astype(BF16)

    return pl.pallas_call(
        body, name=name, out_shape=jax.ShapeDtypeStruct((S, D), BF16), grid=(D // tc,),
        in_specs=[pl.BlockSpec((S, tc), lambda j: (0, j)),
                  pl.BlockSpec((CONV_A, tc), lambda j: (0, j)),
                  pl.BlockSpec((1, tc), lambda j: (0, j))],
        out_specs=pl.BlockSpec((S, tc), lambda j: (0, j)),
        compiler_params=_params("parallel"),
    )(u, conv_w, conv_b)


def _conv_bwd(u, dq, dk, conv_w, conv_b, du, *, S, D, name):
    tc = 256
    nq_blocks = (D // 2) // tc

    def body(u_ref, dq_ref, dk_ref, w_ref, b_ref, du_in, du_ref, dw_ref, db_ref):
        del du_in
        x = u_ref[...]
        n = x.shape[0]
        row = lax.broadcasted_iota(jnp.int32, x.shape, 0)
        c = _conv_pre(x, w_ref, b_ref, row)
        is_k = pl.program_id(0) >= nq_blocks
        dy = jnp.where(is_k, dk_ref[...] * (DK_A ** -0.5), dq_ref[...])
        sg = _sigmoid(c)
        dc = dy * (sg * (1.0 + c * (1.0 - sg)))
        db_ref[...] = jnp.sum(dc, axis=0, keepdims=True)
        dx = w_ref[CONV_A - 1:CONV_A, :] * dc
        dw_ref[CONV_A - 1:CONV_A, :] = jnp.sum(dc * x, axis=0, keepdims=True)
        for k in range(CONV_A - 1):
            d = CONV_A - 1 - k
            dw_ref[k:k + 1, :] = jnp.sum(dc * _shift_down(x, d, row), axis=0, keepdims=True)
            dx = dx + w_ref[k:k + 1, :] * _shift_up(dc, d, row, n)
        du_ref[...] = dx.astype(BF16)

    half = lambda j: (0, j % nq_blocks)
    return pl.pallas_call(
        body, name=name,
        out_shape=(jax.ShapeDtypeStruct(du.shape, du.dtype),
                   jax.ShapeDtypeStruct((CONV_A, D), F32), jax.ShapeDtypeStruct((1, D), F32)),
        grid=(D // tc,),
        in_specs=[pl.BlockSpec((S, tc), lambda j: (0, j)),
                  pl.BlockSpec((S, tc), half), pl.BlockSpec((S, tc), half),
                  pl.BlockSpec((CONV_A, tc), lambda j: (0, j)),
                  pl.BlockSpec((1, tc), lambda j: (0, j)), ANY],
        out_specs=(pl.BlockSpec((S, tc), lambda j: (0, j)),
                   pl.BlockSpec((CONV_A, tc), lambda j: (0, j)),
                   pl.BlockSpec((1, tc), lambda j: (0, j))),
        input_output_aliases={5: 0},
        compiler_params=_params("parallel"),
    )(u, dq, dk, conv_w, conv_b, du)


def _tri(n, cmp):
    r = lax.broadcasted_iota(jnp.int32, (n, n), 0)
    c = lax.broadcasted_iota(jnp.int32, (n, n), 1)
    return r, c, cmp(r, c)


def _gates_fwd(gt, bias, *, nha, nb, name):
    half = nha * nb
    nb_shift = nb.bit_length() - 1
    assert nb == 1 << nb_shift

    def body(g_ref, b_ref, f_ref, brow_ref):
        ig = g_ref[0:half, :] + b_ref[0:half, :]
        fg = g_ref[half:2 * half, :] + b_ref[half:2 * half, :]
        lf = _log_sigmoid(fg)
        _, _, upper = _tri(LANES, lambda r, c: r <= c)
        cs = _mask_dot3(lf, upper.astype(BF16))
        tot = jnp.broadcast_to(cs[:, LANES - 1:LANES], cs.shape)
        r, c, _ = _tri(half, lambda r, c: r <= c)
        before = jnp.logical_and(r >> nb_shift == c >> nb_shift, c < r).astype(BF16)
        f = cs + _mask_dot3_left(before, tot)
        f_ref[...] = f
        brow_ref[...] = ig - f

    return pl.pallas_call(
        body, name=name,
        out_shape=(jax.ShapeDtypeStruct((half, LANES), F32), jax.ShapeDtypeStruct((half, LANES), F32)),
    )(gt, bias)


def _gates_bwd(rowsum, colsum, gt, bias, *, nha, nb, name):
    half = nha * nb
    nb_shift = nb.bit_length() - 1
    assert nb == 1 << nb_shift

    def body(rs_ref, cs_ref, g_ref, b_ref, dg_ref, tot_ref):
        col = cs_ref[...]
        df = rs_ref[...] - col
        _, _, lower = _tri(LANES, lambda r, c: r >= c)
        rc = _mask_dot3(df, lower.astype(BF16))
        tot = jnp.broadcast_to(rc[:, 0:1], rc.shape)
        r, c, _ = _tri(half, lambda r, c: r <= c)
        same = r >> nb_shift == c >> nb_shift
        after = jnp.logical_and(same, c > r).astype(BF16)
        dlf = rc + _mask_dot3_left(after, tot)
        fg = g_ref[half:2 * half, :] + b_ref[half:2 * half, :]
        dfg = dlf * _sigmoid(-fg)
        dg_ref[0:half, :] = col
        dg_ref[half:2 * half, :] = dfg
        grp = same.astype(BF16)
        ones = jnp.ones((LANES, LANES), BF16)
        tot_ref[0:half, :] = _mask_dot3_left(grp, _mask_dot3(col, ones))
        tot_ref[half:2 * half, :] = _mask_dot3_left(grp, _mask_dot3(dfg, ones))

    return pl.pallas_call(
        body, name=name,
        out_shape=(jax.ShapeDtypeStruct((2 * half, LANES), F32), jax.ShapeDtypeStruct((2 * half, LANES), F32)),
    )(rowsum, colsum, gt, bias)


def _mlstm_tile(q, k_ref, fcol, brow_ref, j, i, tq, m=None):
    off = pl.multiple_of(j * tq, tq)
    kj = k_ref[pl.ds(off, tq), :]
    s = _dot_nt(q, kj)
    row = lax.broadcasted_iota(jnp.int32, (tq, tq), 0)
    col = lax.broadcasted_iota(jnp.int32, (tq, tq), 1)
    valid = jnp.logical_or(col <= row, j < i)
    logd = jnp.where(valid, fcol + brow_ref[0, :, pl.ds(off, tq)], NEG_BIG)
    return off, kj, s, logd


def _mlstm_fwd(qk, u, fcol, brow, *, S, D, nha, name):
    tq = min(ML_TQ, S)
    nq = S // tq
    kb, vb = (D // 2) // DK_A, D // DV_A

    def body(q_ref, k_ref, v_ref, fcol_ref, brow_ref, h_ref, m_ref, den_ref):
        i = pl.program_id(1)
        q = q_ref[...]
        fc = fcol_ref[0]

        def step(j, carry):
            acc, den, m = carry
            off, _, s, logd = _mlstm_tile(q, k_ref, fc, brow_ref, j, i, tq)
            m_new = jnp.maximum(m, jnp.max(logd, axis=1, keepdims=True))
            a = s * jnp.exp(logd - m_new)
            alpha = jnp.exp(m - m_new)
            vj = v_ref[pl.ds(off, tq), :].astype(BF16)
            acc = alpha * acc + _dot_nn(a.astype(BF16), vj)
            den = alpha * den + jnp.sum(a, axis=1, keepdims=True)
            return acc, den, m_new

        acc, den, m = lax.fori_loop(
            0, i + 1, step,
            (jnp.zeros((tq, DV_A), F32), jnp.zeros((tq, 1), F32), jnp.full((tq, 1), NEG_BIG, F32)))
        h_ref[...] = acc / jnp.maximum(jnp.abs(den), jnp.exp(-m))
        m_ref[0] = m
        den_ref[0] = den

    stat = pl.BlockSpec((1, tq, 1), lambda h, i: (h, i, 0))
    return pl.pallas_call(
        body, name=name,
        out_shape=(jax.ShapeDtypeStruct((S, D), F32), jax.ShapeDtypeStruct((nha, S, 1), F32),
                   jax.ShapeDtypeStruct((nha, S, 1), F32)),
        grid=(nha, nq),
        in_specs=[pl.BlockSpec((tq, DK_A), lambda h, i: (i, h)),
                  pl.BlockSpec((S, DK_A), lambda h, i: (0, kb + h)),
                  pl.BlockSpec((S, DV_A), lambda h, i: (0, vb + h)),
                  stat, pl.BlockSpec((1, 1, S), lambda h, i: (h, 0, 0))],
        out_specs=(pl.BlockSpec((tq, DV_A), lambda h, i: (i, h)), stat, stat),
        compiler_params=_params("parallel", "arbitrary"),
    )(qk, qk, u, fcol, brow)


def _mlstm_bwd(qk, u, fcol, brow, m, den, dh, h, du, *, S, D, nha, name):
    tq = min(ML_TQ, S)
    nq = S // tq
    kb, vb = (D // 2) // DK_A, D // DV_A

    def body(q_ref, k_ref, v_ref, fcol_ref, brow_ref, m_ref, den_ref, dh_ref, h_ref, du_in,
             du_ref, dq_ref, dk_ref, rs_ref, cs_ref, dv_acc):
        del du_in
        i = pl.program_id(1)

        @pl.when(i == 0)
        def _():
            dk_ref[...] = jnp.zeros_like(dk_ref)
            cs_ref[...] = jnp.zeros_like(cs_ref)
            dv_acc[...] = jnp.zeros_like(dv_acc)

        q = q_ref[...]
        fc = fcol_ref[0]
        mm = m_ref[0]
        dn = den_ref[0]
        floor = jnp.exp(-mm)
        nrm = jnp.maximum(jnp.abs(dn), floor)
        dhv = dh_ref[...]
        dnum = dhv / nrm
        dnrm = -jnp.sum(dhv * h_ref[...], axis=1, keepdims=True) / nrm
        dden = jnp.where(jnp.abs(dn) > floor, jnp.where(dn > 0.0, dnrm, -dnrm), 0.0)
        dnum_b = dnum.astype(BF16)

        def step(j, carry):
            dq, rs = carry
            off, kj, s, logd = _mlstm_tile(q, k_ref, fc, brow_ref, j, i, tq)
            p = jnp.exp(logd - mm)
            a = s * p
            vj = v_ref[pl.ds(off, tq), :].astype(BF16)
            da = _dot_nt(dnum_b, vj) + dden
            dv_acc[pl.ds(off, tq), :] += _dot_tn(a.astype(BF16), dnum_b)
            dqk = (da * p).astype(BF16)
            dq = dq + _dot_nn(dqk, kj)
            dk_ref[pl.ds(off, tq), :] += _dot_tn(dqk, q)
            pm = da * a
            cs_ref[0, :, pl.ds(off, tq)] += jnp.sum(pm, axis=0, keepdims=True)
            rs = rs + jnp.sum(pm, axis=1, keepdims=True)
            return dq, rs

        dq, rs = lax.fori_loop(0, i + 1, step, (jnp.zeros((tq, DK_A), F32), jnp.zeros((tq, 1), F32)))
        dq_ref[...] = dq
        rs_ref[0] = rs

        @pl.when(i == nq - 1)
        def _():
            du_ref[...] = dv_acc[...].astype(BF16)

    stat = pl.BlockSpec((1, tq, 1), lambda h, i: (h, i, 0))
    rowv = pl.BlockSpec((1, 1, S), lambda h, i: (h, 0, 0))
    hblk = pl.BlockSpec((tq, DV_A), lambda h, i: (i, h))
    return pl.pallas_call(
        body, name=name,
        out_shape=(jax.ShapeDtypeStruct(du.shape, du.dtype),
                   jax.ShapeDtypeStruct((S, D // 2), F32), jax.ShapeDtypeStruct((S, D // 2), F32),
                   jax.ShapeDtypeStruct((nha, S, 1), F32), jax.ShapeDtypeStruct((nha, 1, S), F32)),
        grid=(nha, nq),
        in_specs=[pl.BlockSpec((tq, DK_A), lambda h, i: (i, h)),
                  pl.BlockSpec((S, DK_A), lambda h, i: (0, kb + h)),
                  pl.BlockSpec((S, DV_A), lambda h, i: (0, vb + h)),
                  stat, rowv, stat, stat, hblk, hblk, ANY],
        out_specs=(pl.BlockSpec((S, DV_A), lambda h, i: (0, vb + h)),
                   pl.BlockSpec((tq, DK_A), lambda h, i: (i, h)),
                   pl.BlockSpec((S, DK_A), lambda h, i: (0, h)),
                   stat, rowv),
        scratch_shapes=[pltpu.VMEM((S, DV_A), F32)],
        input_output_aliases={9: 0},
        compiler_params=_params("parallel", "arbitrary"),
    )(qk, qk, u, fcol, brow, m, den, dh, h, du)


def _head_norm(hh):
    mu = jnp.mean(hh, axis=1, keepdims=True)
    hc = hh - mu
    rstd = lax.rsqrt(jnp.mean(hc * hc, axis=1, keepdims=True) + LN_EPS)
    return hc * rstd, rstd


def _hgate_fwd(h, u, head_g, *, S, D, name):
    tm = min(256, S)
    nh = D // DV_A

    def body(h_ref, o_ref, z_ref, g_ref, out_ref):
        for hd in range(nh):
            sl = slice(hd * DV_A, (hd + 1) * DV_A)
            hn, _ = _head_norm(h_ref[:, sl])
            z = z_ref[:, sl]
            out_ref[:, sl] = (_sigmoid(o_ref[:, sl]) * (hn * g_ref[:, sl]) * (z * _sigmoid(z))).astype(BF16)

    return pl.pallas_call(
        body, name=name, out_shape=jax.ShapeDtypeStruct((S, D), BF16), grid=(S // tm,),
        in_specs=[pl.BlockSpec((tm, D), lambda i: (i, 0)), pl.BlockSpec((tm, D), lambda i: (i, 2)),
                  pl.BlockSpec((tm, D), lambda i: (i, 3)), pl.BlockSpec((1, D), lambda i: (0, 0))],
        out_specs=pl.BlockSpec((tm, D), lambda i: (i, 0)),
        compiler_params=_params("parallel"),
    )(h, u, u, head_g)


def _hgate_bwd(dhg, h, u, head_g, du, *, S, D, name):
    tm = min(256, S)
    nh = D // DV_A

    def body(dhg_ref, h_ref, o_ref, z_ref, g_ref, du_in, du_ref, dh_ref, dg_ref):
        del du_in

        @pl.when(pl.program_id(0) == 0)
        def _():
            dg_ref[...] = jnp.zeros_like(dg_ref)

        for hd in range(nh):
            sl = slice(hd * DV_A, (hd + 1) * DV_A)
            hn, rstd = _head_norm(h_ref[:, sl])
            o, z, g, d = o_ref[:, sl], z_ref[:, sl], g_ref[:, sl], dhg_ref[:, sl]
            so, sz = _sigmoid(o), _sigmoid(z)
            silu_z = z * sz
            hng = hn * g
            du_ref[:, sl] = (d * hng * silu_z * so * (1.0 - so)).astype(BF16)
            du_ref[:, D + hd * DV_A:D + (hd + 1) * DV_A] = (
                d * so * hng * (sz * (1.0 + z * (1.0 - sz)))).astype(BF16)
            t = d * so * silu_z
            dg_ref[:, sl] += jnp.sum(t * hn, axis=0, keepdims=True)
            dhn = t * g
            dh_ref[:, sl] = rstd * (dhn - jnp.mean(dhn, axis=1, keepdims=True)
                                    - hn * jnp.mean(dhn * hn, axis=1, keepdims=True))

    row = lambda i: (i, 0)
    return pl.pallas_call(
        body, name=name,
        out_shape=(jax.ShapeDtypeStruct(du.shape, du.dtype), jax.ShapeDtypeStruct((S, D), F32),
                   jax.ShapeDtypeStruct((1, D), F32)),
        grid=(S // tm,),
        in_specs=[pl.BlockSpec((tm, D), row), pl.BlockSpec((tm, D), row),
                  pl.BlockSpec((tm, D), lambda i: (i, 2)), pl.BlockSpec((tm, D), lambda i: (i, 3)),
                  pl.BlockSpec((1, D), lambda i: (0, 0)), ANY],
        out_specs=(pl.BlockSpec((tm, 2 * D), lambda i: (i, 1)), pl.BlockSpec((tm, D), row),
                   pl.BlockSpec((1, D), lambda i: (0, 0))),
        input_output_aliases={5: 0},
        compiler_params=_params("arbitrary"),
    )(dhg, h, u, u, head_g, du)


def _ln_stats(r):
    mu = jnp.mean(r, axis=1, keepdims=True)
    xc = r - mu
    rstd = lax.rsqrt(jnp.mean(xc * xc, axis=1, keepdims=True) + LN_EPS)
    return xc * rstd, rstd


def _ln_back(dxhat, xhat, rstd):
    return rstd * (dxhat - jnp.mean(dxhat, axis=1, keepdims=True)
                   - xhat * jnp.mean(dxhat * xhat, axis=1, keepdims=True))


def _ln_fwd(x, y, g, b, *, S, D, name):
    tm = min(256, S)

    def body(x_ref, y_ref, g_ref, b_ref, o_ref, ob_ref):
        xhat, _ = _ln_stats(ALPHA * x_ref[...] + y_ref[...])
        o = xhat * g_ref[...] + b_ref[...]
        o_ref[...] = o
        ob_ref[...] = o.astype(BF16)

    row = lambda i: (i, 0)
    vec = pl.BlockSpec((1, D), lambda i: (0, 0))
    return pl.pallas_call(
        body, name=name,
        out_shape=(jax.ShapeDtypeStruct((S, D), F32), jax.ShapeDtypeStruct((S, D), BF16)),
        grid=(S // tm,),
        in_specs=[pl.BlockSpec((tm, D), row), pl.BlockSpec((tm, D), row), vec, vec],
        out_specs=(pl.BlockSpec((tm, D), row), pl.BlockSpec((tm, D), row)),
        compiler_params=_params("parallel"),
    )(x, y, g, b)


def _ln_loss_bwd(x1, y2, target, g, b, *, S, D, name):
    tm = min(256, S)

    def body(x_ref, y_ref, t_ref, g_ref, b_ref, dr_ref, drb_ref, dg_ref, db_ref, loss_ref):
        @pl.when(pl.program_id(0) == 0)
        def _():
            dg_ref[...] = jnp.zeros_like(dg_ref)
            db_ref[...] = jnp.zeros_like(db_ref)
            loss_ref[...] = jnp.zeros_like(loss_ref)

        xhat, rstd = _ln_stats(ALPHA * x_ref[...] + y_ref[...])
        diff = xhat * g_ref[...] + b_ref[...] - t_ref[...]
        loss_ref[...] += (0.5 / D) * jnp.sum(diff * diff)
        dx2 = diff * (1.0 / D)
        dg_ref[...] += jnp.sum(dx2 * xhat, axis=0, keepdims=True)
        db_ref[...] += jnp.sum(dx2, axis=0, keepdims=True)
        dr = _ln_back(dx2 * g_ref[...], xhat, rstd)
        dr_ref[...] = dr
        drb_ref[...] = dr.astype(BF16)

    row = lambda i: (i, 0)
    vec = pl.BlockSpec((1, D), lambda i: (0, 0))
    return pl.pallas_call(
        body, name=name,
        out_shape=(jax.ShapeDtypeStruct((S, D), F32), jax.ShapeDtypeStruct((S, D), BF16),
                   jax.ShapeDtypeStruct((1, D), F32), jax.ShapeDtypeStruct((1, D), F32),
                   jax.ShapeDtypeStruct((1, LANES), F32)),
        grid=(S // tm,),
        in_specs=[pl.BlockSpec((tm, D), row)] * 3 + [vec, vec],
        out_specs=(pl.BlockSpec((tm, D), row), pl.BlockSpec((tm, D), row), vec, vec,
                   pl.BlockSpec((1, LANES), lambda i: (0, 0))),
        compiler_params=_params("arbitrary"),
    )(x1, y2, target, g, b)


def _ln_bwd(x, y, g, dout, *, S, D, name):
    tm = min(256, S)

    def body(x_ref, y_ref, g_ref, d_ref, dr_ref, drb_ref, dg_ref, db_ref):
        @pl.when(pl.program_id(0) == 0)
        def _():
            dg_ref[...] = jnp.zeros_like(dg_ref)
            db_ref[...] = jnp.zeros_like(db_ref)

        xhat, rstd = _ln_stats(ALPHA * x_ref[...] + y_ref[...])
        d = d_ref[...]
        dg_ref[...] += jnp.sum(d * xhat, axis=0, keepdims=True)
        db_ref[...] += jnp.sum(d, axis=0, keepdims=True)
        dr = _ln_back(d * g_ref[...], xhat, rstd)
        dr_ref[...] = dr
        drb_ref[...] = dr.astype(BF16)

    row = lambda i: (i, 0)
    vec = pl.BlockSpec((1, D), lambda i: (0, 0))
    return pl.pallas_call(
        body, name=name,
        out_shape=(jax.ShapeDtypeStruct((S, D), F32), jax.ShapeDtypeStruct((S, D), BF16),
                   jax.ShapeDtypeStruct((1, D), F32), jax.ShapeDtypeStruct((1, D), F32)),
        grid=(S // tm,),
        in_specs=[pl.BlockSpec((tm, D), row), pl.BlockSpec((tm, D), row), vec, pl.BlockSpec((tm, D), row)],
        out_specs=(pl.BlockSpec((tm, D), row), pl.BlockSpec((tm, D), row), vec, vec),
        compiler_params=_params("arbitrary"),
    )(x, y, g, dout)


def _sb_scores(q, kj, diagonal, t):
    z = _dot_nt(q, kj) * (DH_B ** -0.5)
    ls = _log_sigmoid(z)
    if not diagonal:
        return None, ls, ls - z
    valid = lax.broadcasted_iota(jnp.int32, (t, t), 1) < lax.broadcasted_iota(jnp.int32, (t, t), 0)
    return valid, ls, jnp.where(valid, ls - z, 0.0)


def _keep(valid, x):
    return x if valid is None else jnp.where(valid, x, 0.0)


def _sb_fwd(q2, kv, *, S, D, name, t=SB_TILE):
    t = min(t, S)
    nq = S // t
    nh = D // DH_B

    def body(q_ref, k_ref, v_ref, o_ref, tot_ref):
        i = pl.program_id(1)
        q = q_ref[...]
        _, _, after = _tri(min(t, MXU_DEPTH), lambda r, c: r > c)
        tri_after = after.astype(BF16)

        def step(j, carry, diagonal):
            acc, cr = carry
            off = pl.multiple_of(j * t, t)
            valid, ls, lneg = _sb_scores(q, k_ref[pl.ds(off, t), :], diagonal, t)
            between = cr + _block_sums(lneg, tri_after, True)
            a = _keep(valid, jnp.exp(ls + between))
            acc = acc + _dot_nn(a.astype(BF16), v_ref[pl.ds(off, t), :])
            return acc, cr + jnp.sum(lneg, axis=1, keepdims=True)

        carry = step(i, (jnp.zeros((t, DH_B), F32), jnp.zeros((t, 1), F32)), True)
        acc, cr = lax.fori_loop(0, i, lambda jj, c: step(i - 1 - jj, c, False), carry)
        o_ref[...] = acc
        tot_ref[0] = cr

    return pl.pallas_call(
        body, name=name,
        out_shape=(jax.ShapeDtypeStruct((S, D), F32), jax.ShapeDtypeStruct((nh, S, 1), F32)), grid=(nh, nq),
        in_specs=[pl.BlockSpec((t, DH_B), lambda h, i: (i, h)),
                  pl.BlockSpec((S, DH_B), lambda h, i: (0, h)),
                  pl.BlockSpec((S, DH_B), lambda h, i: (0, nh + h))],
        out_specs=(pl.BlockSpec((t, DH_B), lambda h, i: (i, h)), pl.BlockSpec((1, t, 1), lambda h, i: (h, i, 0))),
        compiler_params=_params("parallel", "arbitrary"),
    )(q2, kv, kv)


def _sb_bwd(q2, kv, datt, tot, *, S, D, name, t=SB_TILE):
    t = min(t, S)
    nq = S // t
    nh = D // DH_B

    def body(q_ref, k_ref, v_ref, do_ref, tot_ref, dq_ref, dk_ref, dv_ref, dk_acc, dv_acc):
        i = pl.program_id(1)

        @pl.when(i == 0)
        def _():
            dk_acc[...] = jnp.zeros_like(dk_acc)
            dv_acc[...] = jnp.zeros_like(dv_acc)

        q = q_ref[...]
        do_b = do_ref[...].astype(BF16)
        _, _, after = _tri(min(t, MXU_DEPTH), lambda r, c: r > c)
        tri_after = after.astype(BF16)
        _, _, before = _tri(min(t, MXU_DEPTH), lambda r, c: r < c)
        tri_before = before.astype(BF16)

        def step(j, carry, diagonal):
            dq, rest, cg = carry
            off = pl.multiple_of(j * t, t)
            kj = k_ref[pl.ds(off, t), :]
            valid, ls, lneg = _sb_scores(q, kj, diagonal, t)
            rest = rest - jnp.sum(lneg, axis=1, keepdims=True)
            between = rest + _block_sums(lneg, tri_after, True)
            a = _keep(valid, jnp.exp(ls + between))
            g = _dot_nt(do_b, v_ref[pl.ds(off, t), :]) * a
            dv_acc[pl.ds(off, t), :] += _dot_tn(a.astype(BF16), do_b)
            e = cg + _block_sums(g, tri_before, False)
            dz = _keep(valid, g * jnp.exp(lneg) - e * jnp.exp(ls)) * (DH_B ** -0.5)
            dz_b = dz.astype(BF16)
            dq = dq + _dot_nn(dz_b, kj)
            dk_acc[pl.ds(off, t), :] += _dot_tn(dz_b, q)
            return dq, rest, cg + jnp.sum(g, axis=1, keepdims=True)

        carry = lax.fori_loop(0, i, lambda j, c: step(j, c, False),
                              (jnp.zeros((t, DH_B), F32), tot_ref[0], jnp.zeros((t, 1), F32)))
        dq, _, _ = step(i, carry, True)
        dq_ref[...] = dq.astype(BF16)

        @pl.when(i == nq - 1)
        def _():
            dk_ref[...] = dk_acc[...].astype(BF16)
            dv_ref[...] = dv_acc[...].astype(BF16)

    blk = pl.BlockSpec((t, DH_B), lambda h, i: (i, h))
    return pl.pallas_call(
        body, name=name,
        out_shape=(jax.ShapeDtypeStruct((S, D), BF16), jax.ShapeDtypeStruct((S, D), BF16),
                   jax.ShapeDtypeStruct((S, D), BF16)),
        grid=(nh, nq),
        in_specs=[blk, pl.BlockSpec((S, DH_B), lambda h, i: (0, h)),
                  pl.BlockSpec((S, DH_B), lambda h, i: (0, nh + h)), blk,
                  pl.BlockSpec((1, t, 1), lambda h, i: (h, i, 0))],
        out_specs=(blk, pl.BlockSpec((S, DH_B), lambda h, i: (0, h)),
                   pl.BlockSpec((S, DH_B), lambda h, i: (0, h))),
        scratch_shapes=[pltpu.VMEM((S, DH_B), F32), pltpu.VMEM((S, DH_B), F32)],
        compiler_params=_params("parallel", "arbitrary"),
    )(q2, kv, kv, datt, tot)


def _bgate_fwd(att, z2, *, S, D, name):
    tm = min(256, S)

    def body(a_ref, z_ref, o_ref):
        z = z_ref[...]
        o_ref[...] = (a_ref[...] * (z * _sigmoid(z))).astype(BF16)

    row = lambda i: (i, 0)
    return pl.pallas_call(
        body, name=name, out_shape=jax.ShapeDtypeStruct((S, D), BF16), grid=(S // tm,),
        in_specs=[pl.BlockSpec((tm, D), row)] * 2, out_specs=pl.BlockSpec((tm, D), row),
        compiler_params=_params("parallel"),
    )(att, z2)


def _bgate_bwd(dhb, att, z2, *, S, D, name):
    tm = min(256, S)

    def body(d_ref, a_ref, z_ref, da_ref, dz_ref):
        z, d = z_ref[...], d_ref[...]
        sz = _sigmoid(z)
        da_ref[...] = d * (z * sz)
        dz_ref[...] = (d * a_ref[...] * (sz * (1.0 + z * (1.0 - sz)))).astype(BF16)

    row = lambda i: (i, 0)
    return pl.pallas_call(
        body, name=name,
        out_shape=(jax.ShapeDtypeStruct((S, D), F32), jax.ShapeDtypeStruct((S, D), BF16)),
        grid=(S // tm,),
        in_specs=[pl.BlockSpec((tm, D), row)] * 3,
        out_specs=(pl.BlockSpec((tm, D), row), pl.BlockSpec((tm, D), row)),
        compiler_params=_params("parallel"),
    )(dhb, att, z2)


def _adamw_math(w, g, m, v):
    mn = ADAM_B1 * m + (1.0 - ADAM_B1) * g
    vn = ADAM_B2 * v + (1.0 - ADAM_B2) * (g * g)
    m_hat = mn / (1.0 - ADAM_B1 ** ADAM_STEP)
    v_hat = vn / (1.0 - ADAM_B2 ** ADAM_STEP)
    return -ADAM_LR * (m_hat / (jnp.sqrt(v_hat) + ADAM_EPS) + ADAM_WD * w), mn, vn


def _adamw(w, g, m, v, *, name, tr=128):
    rows, cols = w.shape
    tr = min(tr, rows)
    assert rows % tr == 0

    def body(w_ref, g_ref, m_ref, v_ref, d_ref, mo_ref, vo_ref):
        d, mn, vn = _adamw_math(w_ref[...], g_ref[...], m_ref[...], v_ref[...])
        d_ref[...] = d
        mo_ref[...] = mn
        vo_ref[...] = vn

    blk = pl.BlockSpec((tr, cols), lambda i: (i, 0))
    sd = jax.ShapeDtypeStruct((rows, cols), F32)
    return pl.pallas_call(
        body, name=name, out_shape=(sd, sd, sd), grid=(rows // tr,),
        in_specs=[blk] * 4, out_specs=(blk, blk, blk),
        compiler_params=_params("parallel"),
    )(w, g, m, v)


def _local_step(x, target, wa, wg, wao, wbi, wkv, wbo, gate_b, conv_w, conv_b, head_g,
                a_ln_g, a_ln_b, b_ln_g, b_ln_b, *, S, D, nha, rs):
    nb = S // LANES
    kw = dict(S=S, D=D)
    xb = x.astype(BF16)
    u = _mm(xb, wa, mode="nn", M=S, N=4 * D, K=D, out_dtype=F32, name="a_in")
    ug = _mm(xb, wg, mode="nn", M=S, N=LANES, K=D, out_dtype=F32, name="a_in_gates")
    qk = _conv_fwd(u, conv_w, conv_b, name="conv_fwd", **kw)
    gt = ug[:, :2 * nha].T.reshape(2 * nha * nb, LANES)
    gbias = jnp.repeat(gate_b.reshape(2 * nha), nb).reshape(2 * nha * nb, 1)
    fcs, brow = _gates_fwd(gt, gbias, nha=nha, nb=nb, name="gates_fwd")
    fcol = fcs.reshape(nha, S, 1)
    brow = brow.reshape(nha, 1, S)
    h, m, den = _mlstm_fwd(qk, u, fcol, brow, nha=nha, name="mlstm_fwd", **kw)
    hg = _hgate_fwd(h, u, head_g, name="hgate_fwd", **kw)
    y = _mm(hg, wao, mode="nn", M=S, N=D, K=D, out_dtype=F32, name="a_out")
    x1, x1b = _ln_fwd(x, y, a_ln_g, a_ln_b, name="ln_a_fwd", **kw)
    q2 = _mm(x1b, wbi, mode="nn", M=S, N=D, K=D, out_dtype=BF16, name="b_in_q")
    z2 = _mm(x1b, wbi, mode="nn", M=S, N=D, K=D, out_dtype=F32, name="b_in_z", b_off=(0, D))
    kv = _mm(x1b, wkv, mode="nn", M=S, N=2 * D, K=D, out_dtype=BF16, name="b_kv")
    att, sb_tot = _sb_fwd(q2, kv, name="sb_fwd", **kw)
    hb = _bgate_fwd(att, z2, name="bgate_fwd", **kw)
    y2 = _mm(hb, wbo, mode="nn", M=S, N=D, K=D, out_dtype=F32, name="b_out")
    dr2, dr2b, d_bln_g, d_bln_b, loss = _ln_loss_bwd(x1, y2, target, b_ln_g, b_ln_b, name="ln_b_loss", **kw)
    g_wbo = _mm(hb, dr2b, mode="tn", M=D, N=D, K=S, out_dtype=BF16, name="g_b_out")
    dhb = _mm(dr2b, wbo, mode="nt", M=S, N=D, K=D, out_dtype=F32, name="d_b_out")
    datt, dz2 = _bgate_bwd(dhb, att, z2, name="bgate_bwd", **kw)
    dq2, dk2, dv2 = _sb_bwd(q2, kv, datt, sb_tot, name="sb_bwd", **kw)
    g_wbi = _mm(x1b, dq2, mode="tn", M=D, N=D, K=S, out_dtype=BF16, name="g_b_in_q",
                into=lax.empty((D, 2 * D), BF16))
    g_wbi = _mm(x1b, dz2, mode="tn", M=D, N=D, K=S, out_dtype=BF16, name="g_b_in_z", into=g_wbi, out_off=(0, D))
    g_wkv = _mm(x1b, dk2, mode="tn", M=D, N=D, K=S, out_dtype=BF16, name="g_kv_k",
                into=lax.empty((D, 2 * D), BF16))
    g_wkv = _mm(x1b, dv2, mode="tn", M=D, N=D, K=S, out_dtype=BF16, name="g_kv_v", into=g_wkv, out_off=(0, D))
    dq2 = rs.begin("b", dict(wbo=g_wbo, wbi=g_wbi, wkv=g_wkv), dq2)
    dx1 = _mm(dq2, wbi, mode="nt", M=S, N=D, K=D, out_dtype=F32, name="d_b_in_q", acc_in=dr2, acc_scale=ALPHA)
    dx1 = _mm(dz2, wbi, mode="nt", M=S, N=D, K=D, out_dtype=F32, name="d_b_in_z", b_off=(0, D), acc_in=dx1)
    dx1 = _mm(dk2, wkv, mode="nt", M=S, N=D, K=D, out_dtype=F32, name="d_kv_k", acc_in=dx1)
    dx1 = _mm(dv2, wkv, mode="nt", M=S, N=D, K=D, out_dtype=F32, name="d_kv_v", b_off=(0, D), acc_in=dx1)
    dx1 = rs.pair("b", dx1)
    dr, drb, d_aln_g, d_aln_b = _ln_bwd(x, y, a_ln_g, dx1, name="ln_a_bwd", **kw)
    g_wao = _mm(hg, drb, mode="tn", M=D, N=D, K=S, out_dtype=BF16, name="g_a_out")
    drb = rs.begin("a", dict(wao=g_wao), drb)
    dhg = _mm(drb, wao, mode="nt", M=S, N=D, K=D, out_dtype=F32, name="d_a_out")
    du = lax.empty((S, 4 * D), BF16)
    du, dh, d_head_g = _hgate_bwd(dhg, h, u, head_g, du, name="hgate_bwd", **kw)
    dh = rs.pair("a", dh)
    du, dq, dk, rowsum, colsum = _mlstm_bwd(qk, u, fcol, brow, m, den, dh, h, du, nha=nha, name="mlstm_bwd", **kw)
    dgt, dgtot = _gates_bwd(rowsum.reshape(nha * nb, LANES), colsum.reshape(nha * nb, LANES), gt, gbias,
                            nha=nha, nb=nb, name="gates_bwd")
    d_gate_b = dgtot[::nb, 0].reshape(1, 2 * nha)
    dgp = jnp.pad(dgt.reshape(2 * nha, S).T, ((0, 0), (0, LANES - 2 * nha))).astype(BF16)
    du, d_conv_w, d_conv_b = _conv_bwd(u, dq, dk, conv_w, conv_b, du, name="conv_bwd", **kw)
    small = dict(gate_b=d_gate_b, conv_w=d_conv_w, conv_b=d_conv_b, head_g=d_head_g,
                 a_ln_g=d_aln_g, a_ln_b=d_aln_b, b_ln_g=d_bln_g, b_ln_b=d_bln_b)
    du = rs.chip("a", rs.chip("b", du))
    g_wa = _mm(xb, du, mode="tn", M=D, N=4 * D, K=S, out_dtype=BF16, name="g_a_in")
    g_wg = _mm(xb, dgp, mode="tn", M=D, N=LANES, K=S, out_dtype=BF16, name="g_a_in_gates")
    du = rs.begin("w", dict(wa=g_wa, wg=g_wg), du)
    du = rs.pair("w", rs.small(small, loss, du))
    du = rs.finish("a", rs.finish("b", du))
    dx = _mm(du, wa, mode="nt", M=S, N=D, K=4 * D, out_dtype=F32, name="d_a_in", acc_in=dr, acc_scale=ALPHA)
    dx = _mm(dgp, wg, mode="nt", M=S, N=D, K=LANES, out_dtype=F32, name="d_a_in_gates", acc_in=dx)
    return dx


def _coords():
    return lax.axis_index("x"), lax.axis_index("y"), lax.axis_index("c")


def _other_chips(x, y):
    return [(1 - x, y), (x, 1 - y), (1 - x, 1 - y)]


def _rows(ref, start, size):
    return ref.at[pl.ds(pl.multiple_of(start, size), size), :]


def _window(kind, ref, shard_shape, j, hf=None):
    r, cw = shard_shape
    row0, nr = (0, r) if hf is None else (hf * (r // 2), r // 2)
    if kind == "stack":
        return ref.at[j, pl.ds(pl.multiple_of(row0, nr), nr), :]
    if kind == "rows":
        return ref.at[pl.ds(pl.multiple_of(j * r + row0, nr), nr), :]
    assert kind == "cols"
    return ref.at[pl.ds(pl.multiple_of(row0, nr), nr), pl.ds(pl.multiple_of(j * cw, cw), cw)]


def _gathered_shape(kind, shard_shape):
    r, cw = shard_shape
    return {"stack": (N_CHIPS, r, cw), "rows": (N_CHIPS * r, cw), "cols": (r, N_CHIPS * cw)}[kind]


def _remote(src, dst, send_sems, recv_sems, k, to):
    return pltpu.make_async_remote_copy(src_ref=src, dst_ref=dst, send_sem=send_sems.at[k],
                                        recv_sem=recv_sems.at[k], device_id=to, device_id_type=MESH)


def _allgather_copies(s_refs, sm_ref, g_refs, smg_ref, send_sems, recv_sems, kinds, shapes):
    n = len(s_refs)
    x, y, c = _coords()
    me, sibling, chips = 2 * x + y, (x, y, 1 - c), _other_chips(x, y)
    ids = [2 * cx + cy for cx, cy in chips]
    own_base = 6 * n + 3

    def small_win(j):
        w = sm_ref.shape[1]
        return smg_ref.at[:, pl.ds(pl.multiple_of(j * w, w), w)]

    first = [_remote(s_refs[a], _window(kinds[a], g_refs[a], shapes[a], me), send_sems, recv_sems,
                     own_base + a, sibling) for a in range(n)]
    if sm_ref is not None:
        first.append(_remote(sm_ref, small_win(me), send_sems, recv_sems, own_base + n, sibling))
    for a in range(n):
        src = _rows(s_refs[a], c * (shapes[a][0] // 2), shapes[a][0] // 2)
        dst = _window(kinds[a], g_refs[a], shapes[a], me, c)
        first += [_remote(src, dst, send_sems, recv_sems, 3 * a + k, (*chip, c)) for k, chip in enumerate(chips)]
    if sm_ref is not None:
        first += [_remote(sm_ref, small_win(me), send_sems, recv_sems, 6 * n + k, (*chip, c))
                  for k, chip in enumerate(chips)]
    for cp in first:
        cp.start()
    passed = []
    for a in range(n):
        for k in range(3):
            win = _window(kinds[a], g_refs[a], shapes[a], ids[k], c)
            _remote(win, win, send_sems, recv_sems, 3 * a + k, sibling).wait_recv()
            fwd = _remote(win, win, send_sems, recv_sems, 3 * n + 3 * a + k, sibling)
            fwd.start()
            passed.append(fwd)
    if sm_ref is not None:
        for k in range(3):
            win = small_win(ids[k])
            _remote(win, win, send_sems, recv_sems, 6 * n + k, sibling).wait_recv()
    for a in range(n):
        for k in range(3):
            win = _window(kinds[a], g_refs[a], shapes[a], ids[k], 1 - c)
            _remote(win, win, send_sems, recv_sems, 3 * n + 3 * a + k, sibling).wait_recv()
    for a in range(n):
        win = _window(kinds[a], g_refs[a], shapes[a], me)
        _remote(win, win, send_sems, recv_sems, own_base + a, sibling).wait_recv()
    if sm_ref is not None:
        _remote(small_win(me), small_win(me), send_sems, recv_sems, own_base + n, sibling).wait_recv()
    for cp in first + passed:
        cp.wait_send()


def _peer_handshake(peers):
    barrier = pltpu.get_barrier_semaphore()
    for peer in peers:
        pl.semaphore_signal(barrier, inc=1, device_id=peer, device_id_type=MESH)
    pl.semaphore_wait(barrier, len(peers))


def _behind(copies, inputs, out_structs, nsem, peers, *, name, collective_id):
    hbm = pltpu.MemorySpace.HBM
    in_refs = [jax.new_ref(a, memory_space=hbm) for a in inputs]
    out_refs = [jax.empty_ref(st, memory_space=hbm) for st in out_structs]

    @pl.kernel(mesh=plsc.ScalarSubcoreMesh(axis_name="sequencer", num_cores=1), name=name,
               scratch_types=(pltpu.SemaphoreType.DMA((nsem,)), pltpu.SemaphoreType.DMA((nsem,))),
               compiler_params=pltpu.CompilerParams(collective_id=collective_id))
    def launch(send_sems, recv_sems):
        _peer_handshake(peers(*_coords()))
        copies(in_refs, out_refs, send_sems, recv_sems)

    launch()
    return [r[...] for r in out_refs]


def _sibling(x, y, c):
    return [(x, y, 1 - c)]


def _same_core_of_other_chips(x, y, c):
    return [(cx, cy, c) for cx, cy in _other_chips(x, y)]


def _sibling_and_other_chips(x, y, c):
    return _sibling(x, y, c) + _same_core_of_other_chips(x, y, c)


ID_ALL_PEERS, ID_SIBLING, ID_CHIPS = 0, 1, 2


def _allgather_behind(shards, kinds, small=None, *, name):
    n = len(shards)
    shapes = [sh.shape for sh in shards]

    def copies(in_refs, out_refs, send_sems, recv_sems):
        if small is None:
            _allgather_copies(in_refs, None, out_refs, None, send_sems, recv_sems, kinds, shapes)
        else:
            _allgather_copies(in_refs[:n], in_refs[n], out_refs[:n], out_refs[n], send_sems, recv_sems, kinds, shapes)

    outs = [jax.ShapeDtypeStruct(_gathered_shape(kinds[a], shapes[a]), shards[a].dtype) for a in range(n)]
    inputs = list(shards)
    if small is not None:
        outs.append(jax.ShapeDtypeStruct((small.shape[0], N_CHIPS * small.shape[1]), small.dtype))
        inputs.append(small)
    return _behind(copies, inputs, outs, 7 * n + 4, _sibling_and_other_chips, name=name, collective_id=ID_ALL_PEERS)


def _rs_pair_exchange(views, *, name):
    n = len(views)

    def copies(g_refs, p_refs, send_sems, recv_sems):
        x, y, c = _coords()
        cps = []
        for a in range(n):
            r2 = views[a].shape[1] // 2
            src = g_refs[a].at[:, pl.ds(pl.multiple_of((1 - c) * r2, r2), r2), :]
            cps.append(_remote(src, p_refs[a], send_sems, recv_sems, a, (x, y, 1 - c)))
        for cp in cps:
            cp.start()
        for cp in cps:
            cp.wait()

    outs = [jax.ShapeDtypeStruct((v.shape[0], v.shape[1] // 2, v.shape[2]), v.dtype) for v in views]
    return _behind(copies, views, outs, n, _sibling, name=name, collective_id=ID_SIBLING)


def _add_half(view, part, core, *, name):
    nch, r, cols = view.shape
    r2 = r // 2
    tr = min(128, r2)
    nt = r2 // tr

    def body(c_ref, g_ref, p_ref, o_ref):
        del c_ref
        o_ref[...] = (g_ref[...].astype(F32) + p_ref[...].astype(F32)).astype(o_ref.dtype)

    return pl.pallas_call(
        body, name=name, out_shape=jax.ShapeDtypeStruct((nch, r2, cols), view.dtype),
        grid_spec=pltpu.PrefetchScalarGridSpec(
            num_scalar_prefetch=1, grid=(nch, nt),
            in_specs=[pl.BlockSpec((1, tr, cols), lambda ch, i, c_ref: (ch, c_ref[0] * nt + i, 0)),
                      pl.BlockSpec((1, tr, cols), lambda ch, i, c_ref: (ch, i, 0))],
            out_specs=pl.BlockSpec((1, tr, cols), lambda ch, i, c_ref: (ch, i, 0))),
        compiler_params=_params("parallel", "parallel"),
    )(core, view, part)


def _chunk(kind, ref, j, cw):
    if kind == "cols":
        return ref.at[0, :, pl.ds(pl.multiple_of(j * cw, cw), cw)]
    return ref.at[j]


def _rs_chips(pairs, kinds, *, name):
    n = len(pairs)
    half_shapes = []
    for a in range(n):
        nch, r2, cols = pairs[a].shape
        half_shapes.append((r2, cols // N_CHIPS) if kinds[a] == "cols" else (r2, cols))

    def copies(q_refs, t_refs, send_sems, recv_sems):
        x, y, c = _coords()
        chips = _other_chips(x, y)
        sends = []
        for a in range(n):
            for k, (cx, cy) in enumerate(chips):
                src = _chunk(kinds[a], q_refs[a], 2 * cx + cy, half_shapes[a][1])
                sends.append(_remote(src, t_refs[a].at[k], send_sems, recv_sems, 3 * a + k, (cx, cy, c)))
        for cp in sends:
            cp.start()
        for a in range(n):
            for k in range(3):
                slot = t_refs[a].at[k]
                _remote(slot, slot, send_sems, recv_sems, 3 * a + k, (x, y, c)).wait_recv()
        for cp in sends:
            cp.wait_send()

    outs = [jax.ShapeDtypeStruct((3, *half_shapes[a]), pairs[a].dtype) for a in range(n)]
    return _behind(copies, pairs, outs, 3 * n, _same_core_of_other_chips, name=name, collective_id=ID_CHIPS)


def _sum_chips(pair, parts, chip, kind, *, name):
    _, r, cols = parts.shape
    tr = min(128, r)
    if kind == "cols":
        own_spec = pl.BlockSpec((1, tr, cols), lambda i, chip_ref: (0, i, chip_ref[0]))
    else:
        own_spec = pl.BlockSpec((1, tr, cols), lambda i, chip_ref: (chip_ref[0], i, 0))

    def body(chip_ref, q_ref, p_ref, o_ref):
        del chip_ref
        acc = q_ref[0].astype(F32)
        for s in range(3):
            acc = acc + p_ref[s].astype(F32)
        o_ref[...] = acc

    return pl.pallas_call(
        body, name=name, out_shape=jax.ShapeDtypeStruct((r, cols), F32),
        grid_spec=pltpu.PrefetchScalarGridSpec(
            num_scalar_prefetch=1, grid=(r // tr,),
            in_specs=[own_spec, pl.BlockSpec((3, tr, cols), lambda i, chip_ref: (0, i, 0))],
            out_specs=pl.BlockSpec((tr, cols), lambda i, chip_ref: (i, 0))),
        compiler_params=_params("parallel"),
    )(chip, pair, parts)


def _rs_share(halves, *, name):
    n = len(halves)

    def copies(h_refs, g_refs, send_sems, recv_sems):
        x, y, c = _coords()
        cps = [_remote(h_refs[a], g_refs[a], send_sems, recv_sems, a, (x, y, 1 - c)) for a in range(n)]
        for cp in cps:
            cp.start()
        for cp in cps:
            cp.wait()

    outs = [jax.ShapeDtypeStruct(h.shape, h.dtype) for h in halves]
    return _behind(copies, halves, outs, n, _sibling, name=name, collective_id=ID_SIBLING)


class _GradReducer:
    def __init__(self, w, m, v, long_name, core, chip_ix, as_views, small_step):
        self.w, self.m, self.v, self.long_name = w, m, v, long_name
        self.core, self.chip_ix, self.as_views, self.small_step = core, chip_ix, as_views, small_step
        self.groups, self.results, self.small_results = {}, {}, None

    def small(self, small, loss_row, tie):
        (small, loss_row), tie = lax.optimization_barrier(((small, loss_row), tie))
        self.small_results, tie = lax.optimization_barrier((self.small_step(small, loss_row), tie))
        return tie

    def begin(self, tag, grads, tie):
        keys, views, kinds = self.as_views(grads)
        views, tie = lax.optimization_barrier((views, tie))
        parts = _rs_pair_exchange(views, name="rs_pair_exchange_" + tag)
        self.groups[tag] = dict(keys=keys, views=views, kinds=kinds, parts=parts)
        return tie

    def pair(self, tag, tie):
        g = self.groups[tag]
        g["parts"], tie = lax.optimization_barrier((g["parts"], tie))
        pairs = [_add_half(v, p, self.core, name="pair_sum_" + self.long_name[k])
                 for v, p, k in zip(g["views"], g["parts"], g["keys"])]
        pairs, tie = lax.optimization_barrier((pairs, tie))
        g["pairs"] = pairs
        g["slots"] = _rs_chips(pairs, g["kinds"], name="rs_chips_" + tag)
        return tie

    def chip(self, tag, tie):
        g = self.groups[tag]
        g["slots"], tie = lax.optimization_barrier((g["slots"], tie))
        halves = [_sum_chips(q, t, self.chip_ix, kd, name="chip_sum_" + self.long_name[k])
                  for q, t, kd, k in zip(g["pairs"], g["slots"], g["kinds"], g["keys"])]
        halves, tie = lax.optimization_barrier((halves, tie))
        g["halves"] = halves
        g["others"] = _rs_share(halves, name="rs_share_" + tag)
        return tie

    def finish(self, tag, tie):
        g = self.groups[tag]
        g["others"], tie = lax.optimization_barrier((g["others"], tie))
        out = [_adamw_halves(self.w[k], a, b, self.m[k], self.v[k], self.core, name="adamw_" + self.long_name[k])
               for k, a, b in zip(g["keys"], g["halves"], g["others"])]
        out, tie = lax.optimization_barrier((out, tie))
        self.results.update(zip(g["keys"], out))
        return tie


def _adamw_halves(w, mine, theirs, m, v, core, *, name, tr=128):
    rows, cols = w.shape
    hr = rows // 2
    tr = min(tr, hr)
    nt = hr // tr

    def body(c_ref, w_ref, a_ref, b_ref, m_ref, v_ref, g_ref, d_ref, mo_ref, vo_ref):
        gg = jnp.where(pl.program_id(0) == c_ref[0], a_ref[...], b_ref[...])
        g_ref[...] = gg
        d, mn, vn = _adamw_math(w_ref[...], gg, m_ref[...], v_ref[...])
        d_ref[...] = d
        mo_ref[...] = mn
        vo_ref[...] = vn

    full = pl.BlockSpec((tr, cols), lambda hf, i, c_ref: (hf * nt + i, 0))
    half = pl.BlockSpec((tr, cols), lambda hf, i, c_ref: (i, 0))
    sd = jax.ShapeDtypeStruct((rows, cols), F32)
    return pl.pallas_call(
        body, name=name, out_shape=(sd, sd, sd, sd),
        grid_spec=pltpu.PrefetchScalarGridSpec(
            num_scalar_prefetch=1, grid=(2, nt),
            in_specs=[full, half, half, full, full], out_specs=(full, full, full, full)),
        compiler_params=_params("parallel", "parallel"),
    )(core, w, mine, theirs, m, v)


def _allreduce_small(vec):
    m_per, ncol = vec.shape
    n_dev = 2 * N_CHIPS

    def body(x_ref, out_ref, sum_ref, send_sems, recv_sems, local_sem):
        x, y, c = _coords()
        me, sibling = (x, y, c), (x, y, 1 - c)
        chips = _other_chips(x, y)

        def rows(px, py, pc):
            return out_ref.at[pl.ds(pl.multiple_of((4 * px + 2 * py + pc) * m_per, m_per), m_per), :]

        def copy(k, block, to, src=None):
            return _remote(rows(*block) if src is None else src, rows(*block), send_sems, recv_sems, k, to)

        mine = pltpu.make_async_copy(x_ref, rows(*me), local_sem)
        mine.start()
        first = [copy(0, me, sibling, src=x_ref)]
        first += [copy(1 + j, me, (*chip, c), src=x_ref) for j, chip in enumerate(chips)]
        for cp in first:
            cp.start()
        passed = [copy(4 + j, (*chip, c), sibling) for j, chip in enumerate(chips)]
        for j, chip in enumerate(chips):
            copy(1 + j, (*chip, c), me).wait_recv()
            passed[j].start()
        copy(0, sibling, me).wait_recv()
        for j, chip in enumerate(chips):
            copy(4 + j, (*chip, 1 - c), me).wait_recv()
        for cp in first + passed:
            cp.wait_send()
        mine.wait()
        acc = out_ref[0:m_per, :]
        for d in range(1, n_dev):
            acc = acc + out_ref[d * m_per:(d + 1) * m_per, :]
        sum_ref[...] = acc

    vm = pl.BlockSpec(memory_space=pltpu.VMEM)
    return pl.pallas_call(
        body, name="allreduce_small",
        out_shape=(jax.ShapeDtypeStruct((n_dev * m_per, ncol), vec.dtype), jax.ShapeDtypeStruct((m_per, ncol), vec.dtype)),
        in_specs=[vm], out_specs=(vm, vm),
        scratch_shapes=[pltpu.SemaphoreType.DMA((7,)), pltpu.SemaphoreType.DMA((7,)), pltpu.SemaphoreType.DMA],
    )(vec)[1]


def _pack_rows(pieces, total_rows):
    rows = []
    for p in pieces:
        flat = p.reshape(-1)
        flat = jnp.pad(flat, (0, (-flat.shape[0]) % LANES))
        rows.append(flat.reshape(-1, LANES))
    out = jnp.concatenate(rows, axis=0)
    return jnp.pad(out, ((0, total_rows - out.shape[0]), (0, 0)))


def _unpack_rows(packed, shapes):
    out, r = [], 0
    for shp in shapes:
        size = 1
        for d in shp:
            size *= d
        nr = -(-size // LANES)
        out.append(packed[r:r + nr].reshape(-1)[:size].reshape(shp))
        r += nr
    return out


def _round_up(v, m):
    return -(-v // m) * m


def kernel(x, a_w_in, a_gate_b, a_conv_w, a_conv_b, a_head_g, a_w_out, a_ln_g, a_ln_b, kv_w, b_w_in, b_w_out, b_ln_g, b_ln_b, loss_target, m_a_w_in, m_a_gate_b, m_a_conv_w, m_a_conv_b, m_a_head_g, m_a_w_out, m_a_ln_g, m_a_ln_b, m_kv_w, m_b_w_in, m_b_w_out, m_b_ln_g, m_b_ln_b, v_a_w_in, v_a_gate_b, v_a_conv_w, v_a_conv_b, v_a_head_g, v_a_w_out, v_a_ln_g, v_a_ln_b, v_kv_w, v_b_w_in, v_b_w_out, v_b_ln_g, v_b_ln_b):
    _, S, D = x.shape
    nha = a_gate_b.shape[1] // 2
    chip = 2 * lax.axis_index("x") + lax.axis_index("y")
    core = lax.axis_index("c").astype(jnp.int32).reshape(1)
    dq = D // N_CHIPS

    shards = [a_w_in[0].astype(BF16), a_w_out[0].astype(BF16), kv_w.astype(BF16), b_w_in[0].astype(BF16),
              b_w_out[0].astype(BF16)]
    kinds = ["stack", "rows", "cols", "cols", "rows"]
    small_shard = jnp.concatenate([a_conv_w[0], a_conv_b, a_head_g, a_ln_g, a_ln_b], axis=0)
    wa_g, small_full = _allgather_behind(shards[:1], kinds[:1], small_shard, name="allgather_first")
    wao, wkv, wbi, wbo = _allgather_behind(shards[1:], kinds[1:], name="allgather_behind")
    wa_full = jnp.concatenate([wa_g[j] for j in range(N_CHIPS)], axis=1)
    wa = wa_full
    wg = jnp.pad(wa_full[:, 4 * D:], ((0, 0), (0, LANES - 2 * nha)))
    conv_w, conv_b, head_g, ln_g_a, ln_b_a = (small_full[0:CONV_A], small_full[4:5], small_full[5:6],
                                              small_full[6:7], small_full[7:8])

    chip_ix = chip.astype(jnp.int32).reshape(1)
    (w_wa, m_wa, v_wa), x = lax.optimization_barrier(((a_w_in[0], m_a_w_in[0], v_a_w_in[0]), x))
    w_big = dict(wa=w_wa, wao=a_w_out[0], wkv=kv_w, wbi=b_w_in[0], wbo=b_w_out[0])
    m_big = dict(wa=m_wa, wao=m_a_w_out[0], wkv=m_kv_w, wbi=m_b_w_in[0], wbo=m_b_w_out[0])
    v_big = dict(wa=v_wa, wao=v_a_w_out[0], wkv=v_kv_w, wbi=v_b_w_in[0], wbo=v_b_w_out[0])
    long_name = dict(wa="a_w_in", wao="a_w_out", wkv="kv_w", wbi="b_w_in", wbo="b_w_out")

    def as_views(g):
        if "wa" in g:
            g_full = jnp.concatenate([g["wa"], g["wg"][:, :2 * nha]], axis=1)
            ca = g_full.shape[1] // N_CHIPS
            return ["wa"], [jnp.stack([g_full[:, j * ca:(j + 1) * ca] for j in range(N_CHIPS)], axis=0)], ["stack"]
        keys = list(g)
        views = [g[k].reshape(N_CHIPS, dq, D) if k in ("wao", "wbo") else g[k][None] for k in keys]
        return keys, views, ["stack" if k in ("wao", "wbo") else "cols" for k in keys]

    w_small = [a_gate_b, a_conv_w[0], a_conv_b, a_head_g, a_ln_g, a_ln_b, b_ln_g, b_ln_b]
    m_small = [m_a_gate_b, m_a_conv_w[0], m_a_conv_b, m_a_head_g, m_a_ln_g, m_a_ln_b, m_b_ln_g, m_b_ln_b]
    v_small = [v_a_gate_b, v_a_conv_w[0], v_a_conv_b, v_a_head_g, v_a_ln_g, v_a_ln_b, v_b_ln_g, v_b_ln_b]

    def small_step(small, loss_row):
        order = ["conv_w", "conv_b", "head_g", "a_ln_g", "a_ln_b", "b_ln_g", "b_ln_b", "gate_b"]
        full_shapes = [(CONV_A, D), (1, D), (1, D), (1, D), (1, D), (1, D), (1, D), (1, 2 * nha)]
        n_rows = sum(-(-(s[0] * s[1]) // LANES) for s in full_shapes) + 1
        total = _allreduce_small(_pack_rows([small[k] for k in order] + [loss_row], _round_up(n_rows, 8)))
        sums = dict(zip(order, _unpack_rows(total, full_shapes)))

        def mine(v):
            return lax.dynamic_slice_in_dim(v, chip * dq, dq, axis=1)

        g_small = [sums["gate_b"], mine(sums["conv_w"]), mine(sums["conv_b"]), mine(sums["head_g"]),
                   mine(sums["a_ln_g"]), mine(sums["a_ln_b"]), sums["b_ln_g"], sums["b_ln_b"]]
        rows_small = _round_up(sum(-(-(w.shape[0] * w.shape[1]) // LANES) for w in w_small), 8)
        upd_small = _adamw(_pack_rows(w_small, rows_small), _pack_rows(g_small, rows_small),
                           _pack_rows(m_small, rows_small), _pack_rows(v_small, rows_small), name="adamw_small")
        return total[n_rows - 1, 0], g_small, upd_small

    rs = _GradReducer(w_big, m_big, v_big, long_name, core, chip_ix, as_views, small_step)
    grad_x = _local_step(
        x[0], loss_target[0], wa, wg, wao, wbi, wkv, wbo, a_gate_b, conv_w, conv_b, head_g,
        ln_g_a, ln_b_a, b_ln_g, b_ln_b, S=S, D=D, nha=nha, rs=rs)
    grad_x = rs.finish("w", rs.chip("w", grad_x))
    upd_big = [rs.results[k] for k in ("wa", "wao", "wkv", "wbi", "wbo")]
    g_big = [u[0] for u in upd_big]
    loss, g_small, upd_small = rs.small_results
    d_small, mn_small, vn_small = (_unpack_rows(u, [w.shape for w in w_small]) for u in upd_small)

    def assemble(big5, small8):
        awi, awo, kvw, bwi, bwo = big5
        gb, cw, cb, hg, alg, alb, blg, blb = small8
        return [awi[None], gb, cw[None], cb, hg, awo[None], alg, alb, kvw, bwi[None], bwo[None], blg, blb]

    grads = assemble(g_big, g_small)
    deltas = assemble([u[1] for u in upd_big], d_small)
    new_m = assemble([u[2] for u in upd_big], mn_small)
    new_v = assemble([u[3] for u in upd_big], vn_small)
    return (loss, grad_x[None], *grads, *deltas, *new_m, *new_v)
```

```python
import functools

import jax
import jax.numpy as jnp
from jax import lax
from jax.experimental import pallas as pl
from jax.experimental.pallas import tpu as pltpu
from jax.experimental.pallas import tpu_sc as plsc

F32 = jnp.float32
BF16 = jnp.bfloat16

DEPTH = 2
ALPHA = (2.0 * DEPTH) ** 0.25
LN_EPS = 1e-5
DK_A = 128
DV_A = 256
DH_B = 128
SB_TILE = 512
ML_TQ = 512
CONV_A = 4
ADAM_LR = 0.001
ADAM_B1 = 0.9
ADAM_B2 = 0.999
ADAM_EPS = 1e-08
ADAM_WD = 0.01
ADAM_STEP = 10
N_CHIPS = 4
LANES = 128
MXU_DEPTH = 256
V7X_VMEM_BYTES = 64 * 1024 * 1024
VMEM_LIMIT = (V7X_VMEM_BYTES * 3) // 4
NEG_BIG = -1e30
MESH = pl.DeviceIdType.MESH
ANY = pl.BlockSpec(memory_space=pl.ANY)


def _params(*sem):
    return pltpu.CompilerParams(dimension_semantics=sem, vmem_limit_bytes=VMEM_LIMIT)


def _dot(a, b, dims):
    return lax.dot_general(a, b, (dims, ((), ())), preferred_element_type=F32)


def _dot_nn(a, b):
    return _dot(a, b, ((1,), (0,)))


def _dot_nt(a, b):
    return _dot(a, b, ((1,), (1,)))


def _dot_tn(a, b):
    return _dot(a, b, ((0,), (0,)))


def _split2(x):
    hi = x.astype(BF16)
    lo = (x - hi.astype(F32)).astype(BF16)
    return hi, lo


def _split3(x):
    hi = x.astype(BF16)
    r = x - hi.astype(F32)
    mid = r.astype(BF16)
    lo = (r - mid.astype(F32)).astype(BF16)
    return hi, mid, lo


def _mask_dot2(x, t01):
    hi, lo = _split2(x)
    return _dot_nn(hi, t01) + _dot_nn(lo, t01)


def _block_sums(x, tri, later):
    sub = tri.shape[0]
    n = x.shape[1] // sub
    if n == 1:
        return _mask_dot2(x, tri)
    parts = [x[:, b * sub:(b + 1) * sub] for b in range(n)]
    sums = [jnp.sum(p, axis=1, keepdims=True) for p in parts]
    out = []
    for b in range(n):
        acc = _mask_dot2(parts[b], tri)
        for o in (range(b + 1, n) if later else range(b)):
            acc = acc + sums[o]
        out.append(acc)
    return jnp.concatenate(out, axis=1)


def _mask_dot3(x, t01):
    hi, mid, lo = _split3(x)
    return _dot_nn(hi, t01) + _dot_nn(mid, t01) + _dot_nn(lo, t01)


def _mask_dot3_left(t01, x):
    hi, mid, lo = _split3(x)
    return _dot_nn(t01, hi) + _dot_nn(t01, mid) + _dot_nn(t01, lo)


def _log_sigmoid(z):
    return jnp.minimum(z, 0.0) - jnp.log(1.0 + jnp.exp(-jnp.abs(z)))


def _sigmoid(z):
    return 1.0 / (1.0 + jnp.exp(-z))


def _mm(a, b, *, mode, M, N, K, out_dtype, name, tm=1024, tn=1024, tk=2048,
        a_off=(0, 0), b_off=(0, 0), acc_in=None, acc_scale=1.0, into=None, out_off=(0, 0), b_chunk=None):
    tm, tn, tk = min(tm, M), min(tn, N), min(tk, K)
    if b_chunk is not None:
        tn, tk = (min(tn, b_chunk), tk) if mode == "nn" else (tn, min(tk, b_chunk))
        assert b.ndim == 3 and b_off == (0, 0) and mode in ("nn", "nt")
    assert M % tm == 0 and N % tn == 0 and K % tk == 0
    nk = K // tk
    ar, ac = a_off
    br, bc = b_off
    orow, ocol = out_off
    if mode in ("nn", "nt"):
        assert ar % tm == 0 and ac % tk == 0
        a_spec = pl.BlockSpec((tm, tk), lambda i, j, k: (i + ar // tm, k + ac // tk))
    else:
        assert ar % tk == 0 and ac % tm == 0
        a_spec = pl.BlockSpec((tk, tm), lambda i, j, k: (k + ar // tk, i + ac // tm))
    if b_chunk is not None and mode == "nn":
        per = b_chunk // tn
        b_spec = pl.BlockSpec((None, tk, tn), lambda i, j, k: (j // per, k, j % per))
    elif b_chunk is not None:
        per = b_chunk // tk
        b_spec = pl.BlockSpec((None, tn, tk), lambda i, j, k: (k // per, j, k % per))
    elif mode == "nt":
        assert br % tn == 0 and bc % tk == 0
        b_spec = pl.BlockSpec((tn, tk), lambda i, j, k: (j + br // tn, k + bc // tk))
    else:
        assert br % tk == 0 and bc % tn == 0
        b_spec = pl.BlockSpec((tk, tn), lambda i, j, k: (k + br // tk, j + bc // tn))
    assert orow % tm == 0 and ocol % tn == 0
    o_spec = pl.BlockSpec((tm, tn), lambda i, j, k: (i + orow // tm, j + ocol // tn))
    dims = {"nn": ((1,), (0,)), "nt": ((1,), (1,)), "tn": ((0,), (0,))}[mode]
    inputs, in_specs = [a, b], [a_spec, b_spec]
    has_acc = acc_in is not None
    if has_acc:
        inputs.append(acc_in)
        in_specs.append(pl.BlockSpec((tm, tn), lambda i, j, k: (i, j)))
    aliases = {}
    if into is not None:
        inputs.append(into)
        in_specs.append(ANY)
        aliases = {len(inputs) - 1: 0}
        out_shape = jax.ShapeDtypeStruct(into.shape, into.dtype)
        assert into.dtype == out_dtype
    else:
        out_shape = jax.ShapeDtypeStruct((M, N), out_dtype)

    def body(*refs):
        a_ref, b_ref = refs[0], refs[1]
        acc_in_ref = refs[2] if has_acc else None
        n_in = len(inputs)
        o_ref = refs[n_in]

        def first():
            if has_acc:
                return acc_scale * acc_in_ref[...]
            return None

        if nk == 1:
            r = _dot(a_ref[...], b_ref[...], dims)
            f = first()
            if f is not None:
                r = r + f
            o_ref[...] = r.astype(o_ref.dtype)
        else:
            acc_ref = refs[n_in + 1]
            kk = pl.program_id(2)

            @pl.when(kk == 0)
            def _():
                f = first()
                acc_ref[...] = jnp.zeros_like(acc_ref) if f is None else f

            acc_ref[...] += _dot(a_ref[...], b_ref[...], dims)

            @pl.when(kk == nk - 1)
            def _():
                o_ref[...] = acc_ref[...].astype(o_ref.dtype)

    scratch = [] if nk == 1 else [pltpu.VMEM((tm, tn), F32)]
    return pl.pallas_call(
        body, name=name, out_shape=out_shape, grid=(M // tm, N // tn, nk),
        in_specs=in_specs, out_specs=o_spec, scratch_shapes=scratch,
        input_output_aliases=aliases,
        compiler_params=_params("parallel", "parallel", "arbitrary"),
    )(*inputs)


def _join_chunk_edges(w, d):
    nch, rows, _ = w.shape

    def body(w_ref, o_ref, first, tail, sems):
        del w_ref
        for j in range(1, nch):
            mine = o_ref.at[j, :, pl.ds(0, LANES)]
            loads = [pltpu.make_async_copy(mine, first, sems.at[0]),
                     pltpu.make_async_copy(o_ref.at[j - 1, :, pl.ds(d, LANES)], tail, sems.at[1])]
            for cp in loads:
                cp.start()
            for cp in loads:
                cp.wait()
            first[...] = first[...] + tail[...]
            store = pltpu.make_async_copy(first, mine, sems.at[0])
            store.start()
            store.wait()

    return pl.pallas_call(
        body, name="join_chunk_edges", out_shape=jax.ShapeDtypeStruct(w.shape, w.dtype),
        in_specs=[ANY], out_specs=ANY, input_output_aliases={0: 0},
        scratch_shapes=[pltpu.VMEM((rows, LANES), w.dtype), pltpu.VMEM((rows, LANES), w.dtype),
                        pltpu.SemaphoreType.DMA((2,))],
    )(w)


def _shift_down(x, d, row):
    if d == 0:
        return x
    return jnp.where(row >= d, pltpu.roll(x, d, 0), 0.0)


def _shift_up(x, d, row, n):
    if d == 0:
        return x
    return jnp.where(row < n - d, pltpu.roll(x, n - d, 0), 0.0)


def _conv_pre(x, w_ref, b_ref, row):
    c = b_ref[...] + w_ref[CONV_A - 1:CONV_A, :] * x
    for k in range(CONV_A - 1):
        c = c + w_ref[k:k + 1, :] * _shift_down(x, CONV_A - 1 - k, row)
    return c


def _conv_fwd(u, conv_w, conv_b, *, S, D, name):
    tc = 256
    nq_blocks = (D // 2) // tc

    def body(u_ref, w_ref, b_ref, o_ref):
        x = u_ref[...]
        row = lax.broadcasted_iota(jnp.int32, x.shape, 0)
        c = _conv_pre(x, w_ref, b_ref, row)
        scale = jnp.where(pl.program_id(0) >= nq_blocks, DK_A ** -0.5, 1.0).astype(F32)
        o_ref[...] = (c * _sigmoid(c) * scale).astype(BF16)

    return pl.pallas_call(
        body, name=name, out_shape=jax.ShapeDtypeStruct((S, D), BF16), grid=(D // tc,),
        in_specs=[pl.BlockSpec((S, tc), lambda j: (0, j)),
                  pl.BlockSpec((CONV_A, tc), lambda j: (0, j)),
                  pl.BlockSpec((1, tc), lambda j: (0, j))],
        out_specs=pl.BlockSpec((S, tc), lambda j: (0, j)),
        compiler_params=_params("parallel"),
    )(u, conv_w, conv_b)


def _conv_bwd(u, dq, dk, conv_w, conv_b, du, *, S, D, name):
    tc = 256
    nq_blocks = (D // 2) // tc

    def body(u_ref, dq_ref, dk_ref, w_ref, b_ref, du_in, du_ref, dw_ref, db_ref):
        del du_in
        x = u_ref[...]
        n = x.shape[0]
        row = lax.broadcasted_iota(jnp.int32, x.shape, 0)
        c = _conv_pre(x, w_ref, b_ref, row)
        is_k = pl.program_id(0) >= nq_blocks
        dy = jnp.where(is_k, dk_ref[...] * (DK_A ** -0.5), dq_ref[...])
        sg = _sigmoid(c)
        dc = dy * (sg * (1.0 + c * (1.0 - sg)))
        db_ref[...] = jnp.sum(dc, axis=0, keepdims=True)
        dx = w_ref[CONV_A - 1:CONV_A, :] * dc
        dw_ref[CONV_A - 1:CONV_A, :] = jnp.sum(dc * x, axis=0, keepdims=True)
        for k in range(CONV_A - 1):
            d = CONV_A - 1 - k
            dw_ref[k:k + 1, :] = jnp.sum(dc * _shift_down(x, d, row), axis=0, keepdims=True)
            dx = dx + w_ref[k:k + 1, :] * _shift_up(dc, d, row, n)
        du_ref[...] = dx.astype(BF16)

    half = lambda j: (0, j % nq_blocks)
    return pl.pallas_call(
        body, name=name,
        out_shape=(jax.ShapeDtypeStruct(du.shape, du.dtype),
                   jax.ShapeDtypeStruct((CONV_A, D), F32), jax.ShapeDtypeStruct((1, D), F32)),
        grid=(D // tc,),
        in_specs=[pl.BlockSpec((S, tc), lambda j: (0, j)),
                  pl.BlockSpec((S, tc), half), pl.BlockSpec((S, tc), half),
                  pl.BlockSpec((CONV_A, tc), lambda j: (0, j)),
                  pl.BlockSpec((1, tc), lambda j: (0, j)), ANY],
        out_specs=(pl.BlockSpec((S, tc), lambda j: (0, j)),
                   pl.BlockSpec((CONV_A, tc), lambda j: (0, j)),
                   pl.BlockSpec((1, tc), lambda j: (0, j))),
        input_output_aliases={5: 0},
        compiler_params=_params("parallel"),
    )(u, dq, dk, conv_w, conv_b, du)


def _tri(n, cmp):
    r = lax.broadcasted_iota(jnp.int32, (n, n), 0)
    c = lax.broadcasted_iota(jnp.int32, (n, n), 1)
    return r, c, cmp(r, c)


def _gates_fwd(gt, bias, *, nha, nb, name):
    half = nha * nb
    nb_shift = nb.bit_length() - 1
    assert nb == 1 << nb_shift

    def body(g_ref, b_ref, f_ref, brow_ref):
        ig = g_ref[0:half, :] + b_ref[0:half, :]
        fg = g_ref[half:2 * half, :] + b_ref[half:2 * half, :]
        lf = _log_sigmoid(fg)
        _, _, upper = _tri(LANES, lambda r, c: r <= c)
        cs = _mask_dot3(lf, upper.astype(BF16))
        tot = jnp.broadcast_to(cs[:, LANES - 1:LANES], cs.shape)
        r, c, _ = _tri(half, lambda r, c: r <= c)
        before = jnp.logical_and(r >> nb_shift == c >> nb_shift, c < r).astype(BF16)
        f = cs + _mask_dot3_left(before, tot)
        f_ref[...] = f
        brow_ref[...] = ig - f

    return pl.pallas_call(
        body, name=name,
        out_shape=(jax.ShapeDtypeStruct((half, LANES), F32), jax.ShapeDtypeStruct((half, LANES), F32)),
    )(gt, bias)


def _gates_bwd(rowsum, colsum, gt, bias, *, nha, nb, name):
    half = nha * nb
    nb_shift = nb.bit_length() - 1
    assert nb == 1 << nb_shift

    def body(rs_ref, cs_ref, g_ref, b_ref, dg_ref, tot_ref):
        col = cs_ref[...]
        df = rs_ref[...] - col
        _, _, lower = _tri(LANES, lambda r, c: r >= c)
        rc = _mask_dot3(df, lower.astype(BF16))
        tot = jnp.broadcast_to(rc[:, 0:1], rc.shape)
        r, c, _ = _tri(half, lambda r, c: r <= c)
        same = r >> nb_shift == c >> nb_shift
        after = jnp.logical_and(same, c > r).astype(BF16)
        dlf = rc + _mask_dot3_left(after, tot)
        fg = g_ref[half:2 * half, :] + b_ref[half:2 * half, :]
        dfg = dlf * _sigmoid(-fg)
        dg_ref[0:half, :] = col
        dg_ref[half:2 * half, :] = dfg
        grp = same.astype(BF16)
        ones = jnp.ones((LANES, LANES), BF16)
        tot_ref[0:half, :] = _mask_dot3_left(grp, _mask_dot3(col, ones))
        tot_ref[half:2 * half, :] = _mask_dot3_left(grp, _mask_dot3(dfg, ones))

    return pl.pallas_call(
        body, name=name,
        out_shape=(jax.ShapeDtypeStruct((2 * half, LANES), F32), jax.ShapeDtypeStruct((2 * half, LANES), F32)),
    )(rowsum, colsum, gt, bias)


def _mlstm_tile(q, k_ref, fcol, brow_ref, j, tq, diagonal):
    off = pl.multiple_of(j * tq, tq)
    kj = k_ref[pl.ds(off, tq), :]
    s = _dot_nt(q, kj)
    logd = fcol + brow_ref[0, :, pl.ds(off, tq)]
    if diagonal:
        valid = lax.broadcasted_iota(jnp.int32, (tq, tq), 1) <= lax.broadcasted_iota(jnp.int32, (tq, tq), 0)
        logd = jnp.where(valid, logd, NEG_BIG)
    return off, kj, s, logd


def _mlstm_fwd(qk, u, fcol, brow, *, S, D, nha, name):
    tq = min(ML_TQ, S)
    nq = S // tq
    kb, vb = (D // 2) // DK_A, D // DV_A

    def body(q_ref, k_ref, v_ref, fcol_ref, brow_ref, h_ref, m_ref, den_ref):
        i = pl.program_id(1)
        q = q_ref[...]
        fc = fcol_ref[0]

        def step(j, carry, diagonal):
            acc, den, m = carry
            off, _, s, logd = _mlstm_tile(q, k_ref, fc, brow_ref, j, tq, diagonal)
            m_new = jnp.maximum(m, jnp.max(logd, axis=1, keepdims=True))
            a = s * jnp.exp(logd - m_new)
            alpha = jnp.exp(m - m_new)
            vj = v_ref[pl.ds(off, tq), :].astype(BF16)
            acc = alpha * acc + _dot_nn(a.astype(BF16), vj)
            den = alpha * den + jnp.sum(a, axis=1, keepdims=True)
            return acc, den, m_new

        first = step(i, (jnp.zeros((tq, DV_A), F32), jnp.zeros((tq, 1), F32), jnp.full((tq, 1), NEG_BIG, F32)), True)
        acc, den, m = lax.fori_loop(0, i, lambda j, c: step(j, c, False), first)
        h_ref[...] = acc / jnp.maximum(jnp.abs(den), jnp.exp(-m))
        m_ref[0] = m
        den_ref[0] = den

    stat = pl.BlockSpec((1, tq, 1), lambda h, i: (h, i, 0))
    return pl.pallas_call(
        body, name=name,
        out_shape=(jax.ShapeDtypeStruct((S, D), F32), jax.ShapeDtypeStruct((nha, S, 1), F32),
                   jax.ShapeDtypeStruct((nha, S, 1), F32)),
        grid=(nha, nq),
        in_specs=[pl.BlockSpec((tq, DK_A), lambda h, i: (i, h)),
                  pl.BlockSpec((S, DK_A), lambda h, i: (0, kb + h)),
                  pl.BlockSpec((S, DV_A), lambda h, i: (0, vb + h)),
                  stat, pl.BlockSpec((1, 1, S), lambda h, i: (h, 0, 0))],
        out_specs=(pl.BlockSpec((tq, DV_A), lambda h, i: (i, h)), stat, stat),
        compiler_params=_params("parallel", "arbitrary"),
    )(qk, qk, u, fcol, brow)


def _mlstm_bwd(qk, u, fcol, brow, m, den, dh, h, du, *, S, D, nha, name):
    tq = min(ML_TQ, S)
    nq = S // tq
    kb, vb = (D // 2) // DK_A, D // DV_A

    def body(q_ref, k_ref, v_ref, fcol_ref, brow_ref, m_ref, den_ref, dh_ref, h_ref, du_in,
             du_ref, dq_ref, dk_ref, rs_ref, cs_ref, dv_acc):
        del du_in
        i = pl.program_id(1)

        @pl.when(i == 0)
        def _():
            dk_ref[...] = jnp.zeros_like(dk_ref)
            cs_ref[...] = jnp.zeros_like(cs_ref)
            dv_acc[...] = jnp.zeros_like(dv_acc)

        q = q_ref[...]
        fc = fcol_ref[0]
        mm = m_ref[0]
        dn = den_ref[0]
        floor = jnp.exp(-mm)
        nrm = jnp.maximum(jnp.abs(dn), floor)
        dhv = dh_ref[...]
        dnum = dhv / nrm
        dnrm = -jnp.sum(dhv * h_ref[...], axis=1, keepdims=True) / nrm
        dden = jnp.where(jnp.abs(dn) > floor, jnp.where(dn > 0.0, dnrm, -dnrm), 0.0)
        dnum_b = dnum.astype(BF16)

        def step(j, carry, diagonal):
            dq, rs = carry
            off, kj, s, logd = _mlstm_tile(q, k_ref, fc, brow_ref, j, tq, diagonal)
            p = jnp.exp(logd - mm)
            a = s * p
            vj = v_ref[pl.ds(off, tq), :].astype(BF16)
            da = _dot_nt(dnum_b, vj) + dden
            dv_acc[pl.ds(off, tq), :] += _dot_tn(a.astype(BF16), dnum_b)
            dqk = (da * p).astype(BF16)
            dq = dq + _dot_nn(dqk, kj)
            dk_ref[pl.ds(off, tq), :] += _dot_tn(dqk, q)
            pm = da * a
            cs_ref[0, :, pl.ds(off, tq)] += jnp.sum(pm, axis=0, keepdims=True)
            rs = rs + jnp.sum(pm, axis=1, keepdims=True)
            return dq, rs

        first = step(i, (jnp.zeros((tq, DK_A), F32), jnp.zeros((tq, 1), F32)), True)
        dq, rs = lax.fori_loop(0, i, lambda j, c: step(j, c, False), first)
        dq_ref[...] = dq
        rs_ref[0] = rs

        @pl.when(i == nq - 1)
        def _():
            du_ref[...] = dv_acc[...].astype(BF16)

    stat = pl.BlockSpec((1, tq, 1), lambda h, i: (h, i, 0))
    rowv = pl.BlockSpec((1, 1, S), lambda h, i: (h, 0, 0))
    hblk = pl.BlockSpec((tq, DV_A), lambda h, i: (i, h))
    return pl.pallas_call(
        body, name=name,
        out_shape=(jax.ShapeDtypeStruct(du.shape, du.dtype),
                   jax.ShapeDtypeStruct((S, D // 2), F32), jax.ShapeDtypeStruct((S, D // 2), F32),
                   jax.ShapeDtypeStruct((nha, S, 1), F32), jax.ShapeDtypeStruct((nha, 1, S), F32)),
        grid=(nha, nq),
        in_specs=[pl.BlockSpec((tq, DK_A), lambda h, i: (i, h)),
                  pl.BlockSpec((S, DK_A), lambda h, i: (0, kb + h)),
                  pl.BlockSpec((S, DV_A), lambda h, i: (0, vb + h)),
                  stat, rowv, stat, stat, hblk, hblk, ANY],
        out_specs=(pl.BlockSpec((S, DV_A), lambda h, i: (0, vb + h)),
                   pl.BlockSpec((tq, DK_A), lambda h, i: (i, h)),
                   pl.BlockSpec((S, DK_A), lambda h, i: (0, h)),
                   stat, rowv),
        scratch_shapes=[pltpu.VMEM((S, DV_A), F32)],
        input_output_aliases={9: 0},
        compiler_params=_params("parallel", "arbitrary"),
    )(qk, qk, u, fcol, brow, m, den, dh, h, du)


def _head_norm(hh):
    mu = jnp.mean(hh, axis=1, keepdims=True)
    hc = hh - mu
    rstd = lax.rsqrt(jnp.mean(hc * hc, axis=1, keepdims=True) + LN_EPS)
    return hc * rstd, rstd


def _hgate_fwd(h, u, head_g, *, S, D, name):
    tm = min(256, S)
    nh = D // DV_A

    def body(h_ref, o_ref, z_ref, g_ref, out_ref):
        for hd in range(nh):
            sl = slice(hd * DV_A, (hd + 1) * DV_A)
            hn, _ = _head_norm(h_ref[:, sl])
            z = z_ref[:, sl]
            out_ref[:, sl] = (_sigmoid(o_ref[:, sl]) * (hn * g_ref[:, sl]) * (z * _sigmoid(z))).astype(BF16)

    return pl.pallas_call(
        body, name=name, out_shape=jax.ShapeDtypeStruct((S, D), BF16), grid=(S // tm,),
        in_specs=[pl.BlockSpec((tm, D), lambda i: (i, 0)), pl.BlockSpec((tm, D), lambda i: (i, 2)),
                  pl.BlockSpec((tm, D), lambda i: (i, 3)), pl.BlockSpec((1, D), lambda i: (0, 0))],
        out_specs=pl.BlockSpec((tm, D), lambda i: (i, 0)),
        compiler_params=_params("parallel"),
    )(h, u, u, head_g)


def _hgate_bwd(dhg, h, u, head_g, du, *, S, D, name):
    tm = min(256, S)
    nh = D // DV_A

    def body(dhg_ref, h_ref, o_ref, z_ref, g_ref, du_in, du_ref, dh_ref, dg_ref):
        del du_in

        @pl.when(pl.program_id(0) == 0)
        def _():
            dg_ref[...] = jnp.zeros_like(dg_ref)

        for hd in range(nh):
            sl = slice(hd * DV_A, (hd + 1) * DV_A)
            hn, rstd = _head_norm(h_ref[:, sl])
            o, z, g, d = o_ref[:, sl], z_ref[:, sl], g_ref[:, sl], dhg_ref[:, sl]
            so, sz = _sigmoid(o), _sigmoid(z)
            silu_z = z * sz
            hng = hn * g
            du_ref[:, sl] = (d * hng * silu_z * so * (1.0 - so)).astype(BF16)
            du_ref[:, D + hd * DV_A:D + (hd + 1) * DV_A] = (
                d * so * hng * (sz * (1.0 + z * (1.0 - sz)))).astype(BF16)
            t = d * so * silu_z
            dg_ref[:, sl] += jnp.sum(t * hn, axis=0, keepdims=True)
            dhn = t * g
            dh_ref[:, sl] = rstd * (dhn - jnp.mean(dhn, axis=1, keepdims=True)
                                    - hn * jnp.mean(dhn * hn, axis=1, keepdims=True))

    row = lambda i: (i, 0)
    return pl.pallas_call(
        body, name=name,
        out_shape=(jax.ShapeDtypeStruct(du.shape, du.dtype), jax.ShapeDtypeStruct((S, D), F32),
                   jax.ShapeDtypeStruct((1, D), F32)),
        grid=(S // tm,),
        in_specs=[pl.BlockSpec((tm, D), row), pl.BlockSpec((tm, D), row),
                  pl.BlockSpec((tm, D), lambda i: (i, 2)), pl.BlockSpec((tm, D), lambda i: (i, 3)),
                  pl.BlockSpec((1, D), lambda i: (0, 0)), ANY],
        out_specs=(pl.BlockSpec((tm, 2 * D), lambda i: (i, 1)), pl.BlockSpec((tm, D), row),
                   pl.BlockSpec((1, D), lambda i: (0, 0))),
        input_output_aliases={5: 0},
        compiler_params=_params("arbitrary"),
    )(dhg, h, u, u, head_g, du)


def _ln_stats(r):
    mu = jnp.mean(r, axis=1, keepdims=True)
    xc = r - mu
    rstd = lax.rsqrt(jnp.mean(xc * xc, axis=1, keepdims=True) + LN_EPS)
    return xc * rstd, rstd


def _ln_back(dxhat, xhat, rstd):
    return rstd * (dxhat - jnp.mean(dxhat, axis=1, keepdims=True)
                   - xhat * jnp.mean(dxhat * xhat, axis=1, keepdims=True))


def _ln_fwd(x, y, g, b, *, S, D, name):
    tm = min(256, S)

    def body(x_ref, y_ref, g_ref, b_ref, o_ref, ob_ref):
        xhat, _ = _ln_stats(ALPHA * x_ref[...] + y_ref[...])
        o = xhat * g_ref[...] + b_ref[...]
        o_ref[...] = o
        ob_ref[...] = o.astype(BF16)

    row = lambda i: (i, 0)
    vec = pl.BlockSpec((1, D), lambda i: (0, 0))
    return pl.pallas_call(
        body, name=name,
        out_shape=(jax.ShapeDtypeStruct((S, D), F32), jax.ShapeDtypeStruct((S, D), BF16)),
        grid=(S // tm,),
        in_specs=[pl.BlockSpec((tm, D), row), pl.BlockSpec((tm, D), row), vec, vec],
        out_specs=(pl.BlockSpec((tm, D), row), pl.BlockSpec((tm, D), row)),
        compiler_params=_params("parallel"),
    )(x, y, g, b)


def _ln_loss_bwd(x1, y2, target, g, b, *, S, D, name):
    tm = min(256, S)

    def body(x_ref, y_ref, t_ref, g_ref, b_ref, dr_ref, drb_ref, dg_ref, db_ref, loss_ref):
        @pl.when(pl.program_id(0) == 0)
        def _():
            dg_ref[...] = jnp.zeros_like(dg_ref)
            db_ref[...] = jnp.zeros_like(db_ref)
            loss_ref[...] = jnp.zeros_like(loss_ref)

        xhat, rstd = _ln_stats(ALPHA * x_ref[...] + y_ref[...])
        diff = xhat * g_ref[...] + b_ref[...] - t_ref[...]
        loss_ref[...] += (0.5 / D) * jnp.sum(diff * diff)
        dx2 = diff * (1.0 / D)
        dg_ref[...] += jnp.sum(dx2 * xhat, axis=0, keepdims=True)
        db_ref[...] += jnp.sum(dx2, axis=0, keepdims=True)
        dr = _ln_back(dx2 * g_ref[...], xhat, rstd)
        dr_ref[...] = dr
        drb_ref[...] = dr.astype(BF16)

    row = lambda i: (i, 0)
    vec = pl.BlockSpec((1, D), lambda i: (0, 0))
    return pl.pallas_call(
        body, name=name,
        out_shape=(jax.ShapeDtypeStruct((S, D), F32), jax.ShapeDtypeStruct((S, D), BF16),
                   jax.ShapeDtypeStruct((1, D), F32), jax.ShapeDtypeStruct((1, D), F32),
                   jax.ShapeDtypeStruct((1, LANES), F32)),
        grid=(S // tm,),
        in_specs=[pl.BlockSpec((tm, D), row)] * 3 + [vec, vec],
        out_specs=(pl.BlockSpec((tm, D), row), pl.BlockSpec((tm, D), row), vec, vec,
                   pl.BlockSpec((1, LANES), lambda i: (0, 0))),
        compiler_params=_params("arbitrary"),
    )(x1, y2, target, g, b)


def _ln_bwd(x, y, g, dout, *, S, D, name):
    tm = min(256, S)

    def body(x_ref, y_ref, g_ref, d_ref, dr_ref, drb_ref, dg_ref, db_ref):
        @pl.when(pl.program_id(0) == 0)
        def _():
            dg_ref[...] = jnp.zeros_like(dg_ref)
            db_ref[...] = jnp.zeros_like(db_ref)

        xhat, rstd = _ln_stats(ALPHA * x_ref[...] + y_ref[...])
        d = d_ref[...]
        dg_ref[...] += jnp.sum(d * xhat, axis=0, keepdims=True)
        db_ref[...] += jnp.sum(d, axis=0, keepdims=True)
        dr = _ln_back(d * g_ref[...], xhat, rstd)
        dr_ref[...] = dr
        drb_ref[...] = dr.astype(BF16)

    row = lambda i: (i, 0)
    vec = pl.BlockSpec((1, D), lambda i: (0, 0))
    return pl.pallas_call(
        body, name=name,
        out_shape=(jax.ShapeDtypeStruct((S, D), F32), jax.ShapeDtypeStruct((S, D), BF16),
                   jax.ShapeDtypeStruct((1, D), F32), jax.ShapeDtypeStruct((1, D), F32)),
        grid=(S // tm,),
        in_specs=[pl.BlockSpec((tm, D), row), pl.BlockSpec((tm, D), row), vec, pl.BlockSpec((tm, D), row)],
        out_specs=(pl.BlockSpec((tm, D), row), pl.BlockSpec((tm, D), row), vec, vec),
        compiler_params=_params("arbitrary"),
    )(x, y, g, dout)


def _sb_scores(q, kj, diagonal, t):
    z = _dot_nt(q, kj) * (DH_B ** -0.5)
    ls = _log_sigmoid(z)
    if not diagonal:
        return None, ls, ls - z
    valid = lax.broadcasted_iota(jnp.int32, (t, t), 1) < lax.broadcasted_iota(jnp.int32, (t, t), 0)
    return valid, ls, jnp.where(valid, ls - z, 0.0)


def _keep(valid, x):
    return x if valid is None else jnp.where(valid, x, 0.0)


def _sb_fwd(q2, kv, *, S, D, name, t=SB_TILE):
    t = min(t, S)
    nq = S // t
    nh = D // DH_B

    def body(q_ref, k_ref, v_ref, o_ref, tot_ref):
        i = pl.program_id(1)
        q = q_ref[...]
        _, _, after = _tri(min(t, MXU_DEPTH), lambda r, c: r > c)
        tri_after = after.astype(BF16)

        def step(j, carry, diagonal):
            acc, cr = carry
            off = pl.multiple_of(j * t, t)
            valid, ls, lneg = _sb_scores(q, k_ref[pl.ds(off, t), :], diagonal, t)
            between = cr + _block_sums(lneg, tri_after, True)
            a = _keep(valid, jnp.exp(ls + between))
            acc = acc + _dot_nn(a.astype(BF16), v_ref[pl.ds(off, t), :])
            return acc, cr + jnp.sum(lneg, axis=1, keepdims=True)

        carry = step(i, (jnp.zeros((t, DH_B), F32), jnp.zeros((t, 1), F32)), True)
        acc, cr = lax.fori_loop(0, i, lambda jj, c: step(i - 1 - jj, c, False), carry)
        o_ref[...] = acc
        tot_ref[0] = cr

    return pl.pallas_call(
        body, name=name,
        out_shape=(jax.ShapeDtypeStruct((S, D), F32), jax.ShapeDtypeStruct((nh, S, 1), F32)), grid=(nh, nq),
        in_specs=[pl.BlockSpec((t, DH_B), lambda h, i: (i, h)),
                  pl.BlockSpec((S, DH_B), lambda h, i: (0, h)),
                  pl.BlockSpec((S, DH_B), lambda h, i: (0, nh + h))],
        out_specs=(pl.BlockSpec((t, DH_B), lambda h, i: (i, h)), pl.BlockSpec((1, t, 1), lambda h, i: (h, i, 0))),
        compiler_params=_params("parallel", "arbitrary"),
    )(q2, kv, kv)


def _sb_bwd(q2, kv, datt, tot, *, S, D, name, t=SB_TILE):
    t = min(t, S)
    nq = S // t
    nh = D // DH_B

    def body(q_ref, k_ref, v_ref, do_ref, tot_ref, dq_ref, dk_ref, dv_ref, dk_acc, dv_acc):
        i = pl.program_id(1)

        @pl.when(i == 0)
        def _():
            dk_acc[...] = jnp.zeros_like(dk_acc)
            dv_acc[...] = jnp.zeros_like(dv_acc)

        q = q_ref[...]
        do_b = do_ref[...].astype(BF16)
        _, _, after = _tri(min(t, MXU_DEPTH), lambda r, c: r > c)
        tri_after = after.astype(BF16)
        _, _, before = _tri(min(t, MXU_DEPTH), lambda r, c: r < c)
        tri_before = before.astype(BF16)

        def step(j, carry, diagonal):
            dq, rest, cg = carry
            off = pl.multiple_of(j * t, t)
            kj = k_ref[pl.ds(off, t), :]
            valid, ls, lneg = _sb_scores(q, kj, diagonal, t)
            rest = rest - jnp.sum(lneg, axis=1, keepdims=True)
            between = rest + _block_sums(lneg, tri_after, True)
            a = _keep(valid, jnp.exp(ls + between))
            g = _dot_nt(do_b, v_ref[pl.ds(off, t), :]) * a
            dv_acc[pl.ds(off, t), :] += _dot_tn(a.astype(BF16), do_b)
            e = cg + _block_sums(g, tri_before, False)
            dz = _keep(valid, g * jnp.exp(lneg) - e * jnp.exp(ls)) * (DH_B ** -0.5)
            dz_b = dz.astype(BF16)
            dq = dq + _dot_nn(dz_b, kj)
            dk_acc[pl.ds(off, t), :] += _dot_tn(dz_b, q)
            return dq, rest, cg + jnp.sum(g, axis=1, keepdims=True)

        carry = lax.fori_loop(0, i, lambda j, c: step(j, c, False),
                              (jnp.zeros((t, DH_B), F32), tot_ref[0], jnp.zeros((t, 1), F32)))
        dq, _, _ = step(i, carry, True)
        dq_ref[...] = dq.astype(BF16)

        @pl.when(i == nq - 1)
        def _():
            dk_ref[...] = dk_acc[...].astype(BF16)
            dv_ref[...] = dv_acc[...].astype(BF16)

    blk = pl.BlockSpec((t, DH_B), lambda h, i: (i, h))
    return pl.pallas_call(
        body, name=name,
        out_shape=(jax.ShapeDtypeStruct((S, D), BF16), jax.ShapeDtypeStruct((S, D), BF16),
                   jax.ShapeDtypeStruct((S, D), BF16)),
        grid=(nh, nq),
        in_specs=[blk, pl.BlockSpec((S, DH_B), lambda h, i: (0, h)),
                  pl.BlockSpec((S, DH_B), lambda h, i: (0, nh + h)), blk,
                  pl.BlockSpec((1, t, 1), lambda h, i: (h, i, 0))],
        out_specs=(blk, pl.BlockSpec((S, DH_B), lambda h, i: (0, h)),
                   pl.BlockSpec((S, DH_B), lambda h, i: (0, h))),
        scratch_shapes=[pltpu.VMEM((S, DH_B), F32), pltpu.VMEM((S, DH_B), F32)],
        compiler_params=_params("parallel", "arbitrary"),
    )(q2, kv, kv, datt, tot)


def _bgate_fwd(att, z2, *, S, D, name):
    tm = min(256, S)

    def body(a_ref, z_ref, o_ref):
        z = z_ref[...]
        o_ref[...] = (a_ref[...] * (z * _sigmoid(z))).astype(BF16)

    row = lambda i: (i, 0)
    return pl.pallas_call(
        body, name=name, out_shape=jax.ShapeDtypeStruct((S, D), BF16), grid=(S // tm,),
        in_specs=[pl.BlockSpec((tm, D), row)] * 2, out_specs=pl.BlockSpec((tm, D), row),
        compiler_params=_params("parallel"),
    )(att, z2)


def _bgate_bwd(dhb, att, z2, *, S, D, name):
    tm = min(256, S)

    def body(d_ref, a_ref, z_ref, da_ref, dz_ref):
        z, d = z_ref[...], d_ref[...]
        sz = _sigmoid(z)
        da_ref[...] = d * (z * sz)
        dz_ref[...] = (d * a_ref[...] * (sz * (1.0 + z * (1.0 - sz)))).astype(BF16)

    row = lambda i: (i, 0)
    return pl.pallas_call(
        body, name=name,
        out_shape=(jax.ShapeDtypeStruct((S, D), F32), jax.ShapeDtypeStruct((S, D), BF16)),
        grid=(S // tm,),
        in_specs=[pl.BlockSpec((tm, D), row)] * 3,
        out_specs=(pl.BlockSpec((tm, D), row), pl.BlockSpec((tm, D), row)),
        compiler_params=_params("parallel"),
    )(dhb, att, z2)


def _adamw_math(w, g, m, v):
    mn = ADAM_B1 * m + (1.0 - ADAM_B1) * g
    vn = ADAM_B2 * v + (1.0 - ADAM_B2) * (g * g)
    m_hat = mn / (1.0 - ADAM_B1 ** ADAM_STEP)
    v_hat = vn / (1.0 - ADAM_B2 ** ADAM_STEP)
    return -ADAM_LR * (m_hat / (jnp.sqrt(v_hat) + ADAM_EPS) + ADAM_WD * w), mn, vn


def _adamw(w, g, m, v, *, name, tr=128):
    rows, cols = w.shape
    tr = min(tr, rows)
    assert rows % tr == 0

    def body(w_ref, g_ref, m_ref, v_ref, d_ref, mo_ref, vo_ref):
        d, mn, vn = _adamw_math(w_ref[...], g_ref[...], m_ref[...], v_ref[...])
        d_ref[...] = d
        mo_ref[...] = mn
        vo_ref[...] = vn

    blk = pl.BlockSpec((tr, cols), lambda i: (i, 0))
    sd = jax.ShapeDtypeStruct((rows, cols), F32)
    return pl.pallas_call(
        body, name=name, out_shape=(sd, sd, sd), grid=(rows // tr,),
        in_specs=[blk] * 4, out_specs=(blk, blk, blk),
        compiler_params=_params("parallel"),
    )(w, g, m, v)


def _local_step(x, target, wa, wg, wao, wbi, wkv, wbo, gate_b, conv_w, conv_b, head_g,
                a_ln_g, a_ln_b, b_ln_g, b_ln_b, *, S, D, nha, rs):
    nb = S // LANES
    kw = dict(S=S, D=D)
    xb = x.astype(BF16)
    wa_chunk = D if wa.ndim == 3 else None
    u = _mm(xb, wa, mode="nn", M=S, N=4 * D, K=D, out_dtype=F32, name="a_in", b_chunk=wa_chunk)
    ug = _mm(xb, wg, mode="nn", M=S, N=LANES, K=D, out_dtype=F32, name="a_in_gates")
    qk = _conv_fwd(u, conv_w, conv_b, name="conv_fwd", **kw)
    gt = ug[:, :2 * nha].T.reshape(2 * nha * nb, LANES)
    gbias = jnp.repeat(gate_b.reshape(2 * nha), nb).reshape(2 * nha * nb, 1)
    fcs, brow = _gates_fwd(gt, gbias, nha=nha, nb=nb, name="gates_fwd")
    fcol = fcs.reshape(nha, S, 1)
    brow = brow.reshape(nha, 1, S)
    h, m, den = _mlstm_fwd(qk, u, fcol, brow, nha=nha, name="mlstm_fwd", **kw)
    hg = _hgate_fwd(h, u, head_g, name="hgate_fwd", **kw)
    y = _mm(hg, wao, mode="nn", M=S, N=D, K=D, out_dtype=F32, name="a_out")
    x1, x1b = _ln_fwd(x, y, a_ln_g, a_ln_b, name="ln_a_fwd", **kw)
    q2 = _mm(x1b, wbi, mode="nn", M=S, N=D, K=D, out_dtype=BF16, name="b_in_q")
    z2 = _mm(x1b, wbi, mode="nn", M=S, N=D, K=D, out_dtype=F32, name="b_in_z", b_off=(0, D))
    kv = _mm(x1b, wkv, mode="nn", M=S, N=2 * D, K=D, out_dtype=BF16, name="b_kv")
    att, sb_tot = _sb_fwd(q2, kv, name="sb_fwd", **kw)
    hb = _bgate_fwd(att, z2, name="bgate_fwd", **kw)
    y2 = _mm(hb, wbo, mode="nn", M=S, N=D, K=D, out_dtype=F32, name="b_out")
    dr2, dr2b, d_bln_g, d_bln_b, loss = _ln_loss_bwd(x1, y2, target, b_ln_g, b_ln_b, name="ln_b_loss", **kw)
    g_wbo = _mm(hb, dr2b, mode="tn", M=D, N=D, K=S, out_dtype=BF16, name="g_b_out")
    dhb = _mm(dr2b, wbo, mode="nt", M=S, N=D, K=D, out_dtype=F32, name="d_b_out")
    datt, dz2 = _bgate_bwd(dhb, att, z2, name="bgate_bwd", **kw)
    dq2, dk2, dv2 = _sb_bwd(q2, kv, datt, sb_tot, name="sb_bwd", **kw)
    g_wbi = _mm(x1b, dq2, mode="tn", M=D, N=D, K=S, out_dtype=BF16, name="g_b_in_q",
                into=lax.empty((D, 2 * D), BF16))
    g_wbi = _mm(x1b, dz2, mode="tn", M=D, N=D, K=S, out_dtype=BF16, name="g_b_in_z", into=g_wbi, out_off=(0, D))
    g_wkv = _mm(x1b, dk2, mode="tn", M=D, N=D, K=S, out_dtype=BF16, name="g_kv_k",
                into=lax.empty((D, 2 * D), BF16))
    g_wkv = _mm(x1b, dv2, mode="tn", M=D, N=D, K=S, out_dtype=BF16, name="g_kv_v", into=g_wkv, out_off=(0, D))
    dq2 = rs.begin("b", dict(wbo=g_wbo, wbi=g_wbi, wkv=g_wkv), dq2)
    dx1 = _mm(dq2, wbi, mode="nt", M=S, N=D, K=D, out_dtype=F32, name="d_b_in_q", acc_in=dr2, acc_scale=ALPHA)
    dx1 = _mm(dz2, wbi, mode="nt", M=S, N=D, K=D, out_dtype=F32, name="d_b_in_z", b_off=(0, D), acc_in=dx1)
    dx1 = _mm(dk2, wkv, mode="nt", M=S, N=D, K=D, out_dtype=F32, name="d_kv_k", acc_in=dx1)
    dx1 = _mm(dv2, wkv, mode="nt", M=S, N=D, K=D, out_dtype=F32, name="d_kv_v", b_off=(0, D), acc_in=dx1)
    dx1 = rs.pair("b", dx1)
    dr, drb, d_aln_g, d_aln_b = _ln_bwd(x, y, a_ln_g, dx1, name="ln_a_bwd", **kw)
    g_wao = _mm(hg, drb, mode="tn", M=D, N=D, K=S, out_dtype=BF16, name="g_a_out")
    drb = rs.begin("a", dict(wao=g_wao), drb)
    dhg = _mm(drb, wao, mode="nt", M=S, N=D, K=D, out_dtype=F32, name="d_a_out")
    du = lax.empty((S, 4 * D), BF16)
    du, dh, d_head_g = _hgate_bwd(dhg, h, u, head_g, du, name="hgate_bwd", **kw)
    dh = rs.pair("a", dh)
    du, dq, dk, rowsum, colsum = _mlstm_bwd(qk, u, fcol, brow, m, den, dh, h, du, nha=nha, name="mlstm_bwd", **kw)
    dgt, dgtot = _gates_bwd(rowsum.reshape(nha * nb, LANES), colsum.reshape(nha * nb, LANES), gt, gbias,
                            nha=nha, nb=nb, name="gates_bwd")
    d_gate_b = dgtot[::nb, 0].reshape(1, 2 * nha)
    dgp = jnp.pad(dgt.reshape(2 * nha, S).T, ((0, 0), (0, LANES - 2 * nha))).astype(BF16)
    du, d_conv_w, d_conv_b = _conv_bwd(u, dq, dk, conv_w, conv_b, du, name="conv_bwd", **kw)
    small = dict(gate_b=d_gate_b, conv_w=d_conv_w, conv_b=d_conv_b, head_g=d_head_g,
                 a_ln_g=d_aln_g, a_ln_b=d_aln_b, b_ln_g=d_bln_g, b_ln_b=d_bln_b)
    du = rs.chip("a", rs.chip("b", du))
    g_wa = _mm(xb, du, mode="tn", M=D, N=4 * D, K=S, out_dtype=BF16, name="g_a_in")
    g_wg = _mm(xb, dgp, mode="tn", M=D, N=LANES, K=S, out_dtype=BF16, name="g_a_in_gates")
    du = rs.begin("w", dict(wa=g_wa, wg=g_wg), du)
    du = rs.pair("w", rs.small(small, loss, du))
    du = rs.finish("a", rs.finish("b", du))
    dx = _mm(du, wa, mode="nt", M=S, N=D, K=4 * D, out_dtype=F32, name="d_a_in", acc_in=dr, acc_scale=ALPHA,
             b_chunk=wa_chunk)
    dx = _mm(dgp, wg, mode="nt", M=S, N=D, K=LANES, out_dtype=F32, name="d_a_in_gates", acc_in=dx)
    return dx


def _coords():
    return lax.axis_index("x"), lax.axis_index("y"), lax.axis_index("c")


def _other_chips(x, y):
    return [(1 - x, y), (x, 1 - y), (1 - x, 1 - y)]


def _rows(ref, start, size):
    return ref.at[pl.ds(pl.multiple_of(start, size), size), :]


def _window(kind, ref, shard_shape, j, hf=None):
    r, cw = shard_shape
    row0, nr = (0, r) if hf is None else (hf * (r // 2), r // 2)
    if kind == "stack":
        return ref.at[j, pl.ds(pl.multiple_of(row0, nr), nr), :]
    if kind == "rows":
        return ref.at[pl.ds(pl.multiple_of(j * r + row0, nr), nr), :]
    assert kind == "cols"
    return ref.at[pl.ds(pl.multiple_of(row0, nr), nr), pl.ds(pl.multiple_of(j * cw, cw), cw)]


def _gathered_shape(kind, shard_shape):
    r, cw = shard_shape
    return {"stack": (N_CHIPS, r, cw), "rows": (N_CHIPS * r, cw), "cols": (r, N_CHIPS * cw)}[kind]


def _remote(src, dst, send_sems, recv_sems, k, to):
    return pltpu.make_async_remote_copy(src_ref=src, dst_ref=dst, send_sem=send_sems.at[k],
                                        recv_sem=recv_sems.at[k], device_id=to, device_id_type=MESH)


def _allgather_copies(s_refs, sm_ref, g_refs, smg_ref, send_sems, recv_sems, kinds, shapes):
    n = len(s_refs)
    x, y, c = _coords()
    me, sibling, chips = 2 * x + y, (x, y, 1 - c), _other_chips(x, y)
    ids = [2 * cx + cy for cx, cy in chips]
    own_base = 6 * n + 3

    def small_win(j):
        w = sm_ref.shape[1]
        return smg_ref.at[:, pl.ds(pl.multiple_of(j * w, w), w)]

    first = [_remote(s_refs[a], _window(kinds[a], g_refs[a], shapes[a], me), send_sems, recv_sems,
                     own_base + a, sibling) for a in range(n)]
    if sm_ref is not None:
        first.append(_remote(sm_ref, small_win(me), send_sems, recv_sems, own_base + n, sibling))
    for a in range(n):
        src = _rows(s_refs[a], c * (shapes[a][0] // 2), shapes[a][0] // 2)
        dst = _window(kinds[a], g_refs[a], shapes[a], me, c)
        first += [_remote(src, dst, send_sems, recv_sems, 3 * a + k, (*chip, c)) for k, chip in enumerate(chips)]
    if sm_ref is not None:
        first += [_remote(sm_ref, small_win(me), send_sems, recv_sems, 6 * n + k, (*chip, c))
                  for k, chip in enumerate(chips)]
    for cp in first:
        cp.start()
    passed = []
    for a in range(n):
        for k in range(3):
            win = _window(kinds[a], g_refs[a], shapes[a], ids[k], c)
            _remote(win, win, send_sems, recv_sems, 3 * a + k, sibling).wait_recv()
            fwd = _remote(win, win, send_sems, recv_sems, 3 * n + 3 * a + k, sibling)
            fwd.start()
            passed.append(fwd)
    if sm_ref is not None:
        for k in range(3):
            win = small_win(ids[k])
            _remote(win, win, send_sems, recv_sems, 6 * n + k, sibling).wait_recv()
    for a in range(n):
        for k in range(3):
            win = _window(kinds[a], g_refs[a], shapes[a], ids[k], 1 - c)
            _remote(win, win, send_sems, recv_sems, 3 * n + 3 * a + k, sibling).wait_recv()
    for a in range(n):
        win = _window(kinds[a], g_refs[a], shapes[a], me)
        _remote(win, win, send_sems, recv_sems, own_base + a, sibling).wait_recv()
    if sm_ref is not None:
        _remote(small_win(me), small_win(me), send_sems, recv_sems, own_base + n, sibling).wait_recv()
    for cp in first + passed:
        cp.wait_send()


def _peer_handshake(peers):
    barrier = pltpu.get_barrier_semaphore()
    for peer in peers:
        pl.semaphore_signal(barrier, inc=1, device_id=peer, device_id_type=MESH)
    pl.semaphore_wait(barrier, len(peers))


def _behind(copies, inputs, out_structs, nsem, peers, *, name, collective_id):
    hbm = pltpu.MemorySpace.HBM
    in_refs = [jax.new_ref(a, memory_space=hbm) for a in inputs]
    out_refs = [jax.empty_ref(st, memory_space=hbm) for st in out_structs]

    @pl.kernel(mesh=plsc.ScalarSubcoreMesh(axis_name="sequencer", num_cores=1), name=name,
               scratch_types=(pltpu.SemaphoreType.DMA((nsem,)), pltpu.SemaphoreType.DMA((nsem,))),
               compiler_params=pltpu.CompilerParams(collective_id=collective_id))
    def launch(send_sems, recv_sems):
        _peer_handshake(peers(*_coords()))
        copies(in_refs, out_refs, send_sems, recv_sems)

    launch()
    return [r[...] for r in out_refs]


def _sibling(x, y, c):
    return [(x, y, 1 - c)]


def _same_core_of_other_chips(x, y, c):
    return [(cx, cy, c) for cx, cy in _other_chips(x, y)]


def _sibling_and_other_chips(x, y, c):
    return _sibling(x, y, c) + _same_core_of_other_chips(x, y, c)


ID_ALL_PEERS, ID_SIBLING, ID_CHIPS = 0, 1, 2


def _allgather_behind(shards, kinds, small=None, *, name):
    n = len(shards)
    shapes = [sh.shape for sh in shards]

    def copies(in_refs, out_refs, send_sems, recv_sems):
        if small is None:
            _allgather_copies(in_refs, None, out_refs, None, send_sems, recv_sems, kinds, shapes)
        else:
            _allgather_copies(in_refs[:n], in_refs[n], out_refs[:n], out_refs[n], send_sems, recv_sems, kinds, shapes)

    outs = [jax.ShapeDtypeStruct(_gathered_shape(kinds[a], shapes[a]), shards[a].dtype) for a in range(n)]
    inputs = list(shards)
    if small is not None:
        outs.append(jax.ShapeDtypeStruct((small.shape[0], N_CHIPS * small.shape[1]), small.dtype))
        inputs.append(small)
    return _behind(copies, inputs, outs, 7 * n + 4, _sibling_and_other_chips, name=name, collective_id=ID_ALL_PEERS)


def _rs_pair_exchange(views, *, name):
    n = len(views)

    def copies(g_refs, p_refs, send_sems, recv_sems):
        x, y, c = _coords()
        cps = []
        for a in range(n):
            r2 = views[a].shape[1] // 2
            src = g_refs[a].at[:, pl.ds(pl.multiple_of((1 - c) * r2, r2), r2), :]
            cps.append(_remote(src, p_refs[a], send_sems, recv_sems, a, (x, y, 1 - c)))
        for cp in cps:
            cp.start()
        for cp in cps:
            cp.wait()

    outs = [jax.ShapeDtypeStruct((v.shape[0], v.shape[1] // 2, v.shape[2]), v.dtype) for v in views]
    return _behind(copies, views, outs, n, _sibling, name=name, collective_id=ID_SIBLING)


def _add_half(view, part, core, *, name):
    nch, r, cols = view.shape
    r2 = r // 2
    tr = min(128, r2)
    nt = r2 // tr

    def body(c_ref, g_ref, p_ref, o_ref):
        del c_ref
        o_ref[...] = (g_ref[...].astype(F32) + p_ref[...].astype(F32)).astype(o_ref.dtype)

    return pl.pallas_call(
        body, name=name, out_shape=jax.ShapeDtypeStruct((nch, r2, cols), view.dtype),
        grid_spec=pltpu.PrefetchScalarGridSpec(
            num_scalar_prefetch=1, grid=(nch, nt),
            in_specs=[pl.BlockSpec((1, tr, cols), lambda ch, i, c_ref: (ch, c_ref[0] * nt + i, 0)),
                      pl.BlockSpec((1, tr, cols), lambda ch, i, c_ref: (ch, i, 0))],
            out_specs=pl.BlockSpec((1, tr, cols), lambda ch, i, c_ref: (ch, i, 0))),
        compiler_params=_params("parallel", "parallel"),
    )(core, view, part)


def _chunk(kind, ref, j, cw):
    if kind == "cols":
        return ref.at[0, :, pl.ds(pl.multiple_of(j * cw, cw), cw)]
    return ref.at[j]


def _rs_chips(pairs, kinds, *, name):
    n = len(pairs)
    half_shapes = []
    for a in range(n):
        nch, r2, cols = pairs[a].shape
        half_shapes.append((r2, cols // N_CHIPS) if kinds[a] == "cols" else (r2, cols))

    def copies(q_refs, t_refs, send_sems, recv_sems):
        x, y, c = _coords()
        chips = _other_chips(x, y)
        sends = []
        for a in range(n):
            for k, (cx, cy) in enumerate(chips):
                src = _chunk(kinds[a], q_refs[a], 2 * cx + cy, half_shapes[a][1])
                sends.append(_remote(src, t_refs[a].at[k], send_sems, recv_sems, 3 * a + k, (cx, cy, c)))
        for cp in sends:
            cp.start()
        for a in range(n):
            for k in range(3):
                slot = t_refs[a].at[k]
                _remote(slot, slot, send_sems, recv_sems, 3 * a + k, (x, y, c)).wait_recv()
        for cp in sends:
            cp.wait_send()

    outs = [jax.ShapeDtypeStruct((3, *half_shapes[a]), pairs[a].dtype) for a in range(n)]
    return _behind(copies, pairs, outs, 3 * n, _same_core_of_other_chips, name=name, collective_id=ID_CHIPS)


def _sum_chips(pair, parts, chip, kind, *, name):
    _, r, cols = parts.shape
    tr = min(128, r)
    if kind == "cols":
        own_spec = pl.BlockSpec((1, tr, cols), lambda i, chip_ref: (0, i, chip_ref[0]))
    else:
        own_spec = pl.BlockSpec((1, tr, cols), lambda i, chip_ref: (chip_ref[0], i, 0))

    def body(chip_ref, q_ref, p_ref, o_ref):
        del chip_ref
        acc = q_ref[0].astype(F32)
        for s in range(3):
            acc = acc + p_ref[s].astype(F32)
        o_ref[...] = acc

    return pl.pallas_call(
        body, name=name, out_shape=jax.ShapeDtypeStruct((r, cols), F32),
        grid_spec=pltpu.PrefetchScalarGridSpec(
            num_scalar_prefetch=1, grid=(r // tr,),
            in_specs=[own_spec, pl.BlockSpec((3, tr, cols), lambda i, chip_ref: (0, i, 0))],
            out_specs=pl.BlockSpec((tr, cols), lambda i, chip_ref: (i, 0))),
        compiler_params=_params("parallel"),
    )(chip, pair, parts)


def _rs_share(halves, *, name):
    n = len(halves)

    def copies(h_refs, g_refs, send_sems, recv_sems):
        x, y, c = _coords()
        cps = [_remote(h_refs[a], g_refs[a], send_sems, recv_sems, a, (x, y, 1 - c)) for a in range(n)]
        for cp in cps:
            cp.start()
        for cp in cps:
            cp.wait()

    outs = [jax.ShapeDtypeStruct(h.shape, h.dtype) for h in halves]
    return _behind(copies, halves, outs, n, _sibling, name=name, collective_id=ID_SIBLING)


class _GradReducer:
    def __init__(self, w, m, v, long_name, core, chip_ix, as_views, small_step):
        self.w, self.m, self.v, self.long_name = w, m, v, long_name
        self.core, self.chip_ix, self.as_views, self.small_step = core, chip_ix, as_views, small_step
        self.groups, self.results, self.small_results = {}, {}, None

    def small(self, small, loss_row, tie):
        (small, loss_row), tie = lax.optimization_barrier(((small, loss_row), tie))
        self.small_results, tie = lax.optimization_barrier((self.small_step(small, loss_row), tie))
        return tie

    def begin(self, tag, grads, tie):
        keys, views, kinds = self.as_views(grads)
        views, tie = lax.optimization_barrier((views, tie))
        parts = _rs_pair_exchange(views, name="rs_pair_exchange_" + tag)
        self.groups[tag] = dict(keys=keys, views=views, kinds=kinds, parts=parts)
        return tie

    def pair(self, tag, tie):
        g = self.groups[tag]
        g["parts"], tie = lax.optimization_barrier((g["parts"], tie))
        pairs = [_add_half(v, p, self.core, name="pair_sum_" + self.long_name[k])
                 for v, p, k in zip(g["views"], g["parts"], g["keys"])]
        pairs, tie = lax.optimization_barrier((pairs, tie))
        g["pairs"] = pairs
        g["slots"] = _rs_chips(pairs, g["kinds"], name="rs_chips_" + tag)
        return tie

    def chip(self, tag, tie):
        g = self.groups[tag]
        g["slots"], tie = lax.optimization_barrier((g["slots"], tie))
        halves = [_sum_chips(q, t, self.chip_ix, kd, name="chip_sum_" + self.long_name[k])
                  for q, t, kd, k in zip(g["pairs"], g["slots"], g["kinds"], g["keys"])]
        halves, tie = lax.optimization_barrier((halves, tie))
        g["halves"] = halves
        g["others"] = _rs_share(halves, name="rs_share_" + tag)
        return tie

    def finish(self, tag, tie):
        g = self.groups[tag]
        g["others"], tie = lax.optimization_barrier((g["others"], tie))
        out = [_adamw_halves(self.w[k], a, b, self.m[k], self.v[k], self.core, name="adamw_" + self.long_name[k])
               for k, a, b in zip(g["keys"], g["halves"], g["others"])]
        out, tie = lax.optimization_barrier((out, tie))
        self.results.update(zip(g["keys"], out))
        return tie


def _adamw_halves(w, mine, theirs, m, v, core, *, name, tr=128):
    rows, cols = w.shape
    hr = rows // 2
    tr = min(tr, hr)
    nt = hr // tr

    def body(c_ref, w_ref, a_ref, b_ref, m_ref, v_ref, g_ref, d_ref, mo_ref, vo_ref):
        gg = jnp.where(pl.program_id(0) == c_ref[0], a_ref[...], b_ref[...])
        g_ref[...] = gg
        d, mn, vn = _adamw_math(w_ref[...], gg, m_ref[...], v_ref[...])
        d_ref[...] = d
        mo_ref[...] = mn
        vo_ref[...] = vn

    full = pl.BlockSpec((tr, cols), lambda hf, i, c_ref: (hf * nt + i, 0))
    half = pl.BlockSpec((tr, cols), lambda hf, i, c_ref: (i, 0))
    sd = jax.ShapeDtypeStruct((rows, cols), F32)
    return pl.pallas_call(
        body, name=name, out_shape=(sd, sd, sd, sd),
        grid_spec=pltpu.PrefetchScalarGridSpec(
            num_scalar_prefetch=1, grid=(2, nt),
            in_specs=[full, half, half, full, full], out_specs=(full, full, full, full)),
        compiler_params=_params("parallel", "parallel"),
    )(core, w, mine, theirs, m, v)


def _allreduce_small(vec):
    m_per, ncol = vec.shape
    n_dev = 2 * N_CHIPS

    def body(x_ref, out_ref, sum_ref, send_sems, recv_sems, local_sem):
        x, y, c = _coords()
        me, sibling = (x, y, c), (x, y, 1 - c)
        chips = _other_chips(x, y)

        def rows(px, py, pc):
            return out_ref.at[pl.ds(pl.multiple_of((4 * px + 2 * py + pc) * m_per, m_per), m_per), :]

        def copy(k, block, to, src=None):
            return _remote(rows(*block) if src is None else src, rows(*block), send_sems, recv_sems, k, to)

        mine = pltpu.make_async_copy(x_ref, rows(*me), local_sem)
        mine.start()
        first = [copy(0, me, sibling, src=x_ref)]
        first += [copy(1 + j, me, (*chip, c), src=x_ref) for j, chip in enumerate(chips)]
        for cp in first:
            cp.start()
        passed = [copy(4 + j, (*chip, c), sibling) for j, chip in enumerate(chips)]
        for j, chip in enumerate(chips):
            copy(1 + j, (*chip, c), me).wait_recv()
            passed[j].start()
        copy(0, sibling, me).wait_recv()
        for j, chip in enumerate(chips):
            copy(4 + j, (*chip, 1 - c), me).wait_recv()
        for cp in first + passed:
            cp.wait_send()
        mine.wait()
        acc = out_ref[0:m_per, :]
        for d in range(1, n_dev):
            acc = acc + out_ref[d * m_per:(d + 1) * m_per, :]
        sum_ref[...] = acc

    vm = pl.BlockSpec(memory_space=pltpu.VMEM)
    return pl.pallas_call(
        body, name="allreduce_small",
        out_shape=(jax.ShapeDtypeStruct((n_dev * m_per, ncol), vec.dtype), jax.ShapeDtypeStruct((m_per, ncol), vec.dtype)),
        in_specs=[vm], out_specs=(vm, vm),
        scratch_shapes=[pltpu.SemaphoreType.DMA((7,)), pltpu.SemaphoreType.DMA((7,)), pltpu.SemaphoreType.DMA],
    )(vec)[1]


def _pack_rows(pieces, total_rows):
    rows = []
    for p in pieces:
        flat = p.reshape(-1)
        flat = jnp.pad(flat, (0, (-flat.shape[0]) % LANES))
        rows.append(flat.reshape(-1, LANES))
    out = jnp.concatenate(rows, axis=0)
    return jnp.pad(out, ((0, total_rows - out.shape[0]), (0, 0)))


def _unpack_rows(packed, shapes):
    out, r = [], 0
    for shp in shapes:
        size = 1
        for d in shp:
            size *= d
        nr = -(-size // LANES)
        out.append(packed[r:r + nr].reshape(-1)[:size].reshape(shp))
        r += nr
    return out


def _round_up(v, m):
    return -(-v // m) * m


def kernel(x, a_w_in, a_gate_b, a_conv_w, a_conv_b, a_head_g, a_w_out, a_ln_g, a_ln_b, kv_w, b_w_in, b_w_out, b_ln_g, b_ln_b, loss_target, m_a_w_in, m_a_gate_b, m_a_conv_w, m_a_conv_b, m_a_head_g, m_a_w_out, m_a_ln_g, m_a_ln_b, m_kv_w, m_b_w_in, m_b_w_out, m_b_ln_g, m_b_ln_b, v_a_w_in, v_a_gate_b, v_a_conv_w, v_a_conv_b, v_a_head_g, v_a_w_out, v_a_ln_g, v_a_ln_b, v_kv_w, v_b_w_in, v_b_w_out, v_b_ln_g, v_b_ln_b):
    _, S, D = x.shape
    nha = a_gate_b.shape[1] // 2
    chip = 2 * lax.axis_index("x") + lax.axis_index("y")
    core = lax.axis_index("c").astype(jnp.int32).reshape(1)
    dq = D // N_CHIPS

    ca = a_w_in.shape[2]
    edge = (N_CHIPS - 1) * (ca - D)
    wide = _round_up(ca + edge, LANES)
    shifted = lax.dynamic_slice_in_dim(
        jnp.pad(a_w_in[0].astype(BF16), ((0, 0), (edge, wide - ca))), edge - chip * (ca - D), wide, axis=1)
    shards = [shifted, a_w_out[0].astype(BF16), kv_w.astype(BF16), b_w_in[0].astype(BF16), b_w_out[0].astype(BF16)]
    kinds = ["stack", "rows", "cols", "cols", "rows"]
    small_shard = jnp.concatenate([a_conv_w[0], a_conv_b, a_head_g, a_ln_g, a_ln_b], axis=0)
    wa_g, small_full = _allgather_behind(shards[:1], kinds[:1], small_shard, name="allgather_first")
    wao, wkv, wbi, wbo = _allgather_behind(shards[1:], kinds[1:], name="allgather_behind")
    wa = _join_chunk_edges(wa_g, D)
    wg = wa[N_CHIPS - 1, :, D:D + LANES]
    conv_w, conv_b, head_g, ln_g_a, ln_b_a = (small_full[0:CONV_A], small_full[4:5], small_full[5:6],
                                              small_full[6:7], small_full[7:8])

    chip_ix = chip.astype(jnp.int32).reshape(1)
    (w_wa, m_wa, v_wa), x = lax.optimization_barrier(((a_w_in[0], m_a_w_in[0], v_a_w_in[0]), x))
    w_big = dict(wa=w_wa, wao=a_w_out[0], wkv=kv_w, wbi=b_w_in[0], wbo=b_w_out[0])
    m_big = dict(wa=m_wa, wao=m_a_w_out[0], wkv=m_kv_w, wbi=m_b_w_in[0], wbo=m_b_w_out[0])
    v_big = dict(wa=v_wa, wao=v_a_w_out[0], wkv=v_kv_w, wbi=v_b_w_in[0], wbo=v_b_w_out[0])
    long_name = dict(wa="a_w_in", wao="a_w_out", wkv="kv_w", wbi="b_w_in", wbo="b_w_out")

    def as_views(g):
        if "wa" in g:
            g_full = jnp.concatenate([g["wa"], g["wg"][:, :2 * nha]], axis=1)
            ca = g_full.shape[1] // N_CHIPS
            return ["wa"], [jnp.stack([g_full[:, j * ca:(j + 1) * ca] for j in range(N_CHIPS)], axis=0)], ["stack"]
        keys = list(g)
        views = [g[k].reshape(N_CHIPS, dq, D) if k in ("wao", "wbo") else g[k][None] for k in keys]
        return keys, views, ["stack" if k in ("wao", "wbo") else "cols" for k in keys]

    w_small = [a_gate_b, a_conv_w[0], a_conv_b, a_head_g, a_ln_g, a_ln_b, b_ln_g, b_ln_b]
    m_small = [m_a_gate_b, m_a_conv_w[0], m_a_conv_b, m_a_head_g, m_a_ln_g, m_a_ln_b, m_b_ln_g, m_b_ln_b]
    v_small = [v_a_gate_b, v_a_conv_w[0], v_a_conv_b, v_a_head_g, v_a_ln_g, v_a_ln_b, v_b_ln_g, v_b_ln_b]

    def small_step(small, loss_row):
        order = ["conv_w", "conv_b", "head_g", "a_ln_g", "a_ln_b", "b_ln_g", "b_ln_b", "gate_b"]
        full_shapes = [(CONV_A, D), (1, D), (1, D), (1, D), (1, D), (1, D), (1, D), (1, 2 * nha)]
        n_rows = sum(-(-(s[0] * s[1]) // LANES) for s in full_shapes) + 1
        total = _allreduce_small(_pack_rows([small[k] for k in order] + [loss_row], _round_up(n_rows, 8)))
        sums = dict(zip(order, _unpack_rows(total, full_shapes)))

        def mine(v):
            return lax.dynamic_slice_in_dim(v, chip * dq, dq, axis=1)

        g_small = [sums["gate_b"], mine(sums["conv_w"]), mine(sums["conv_b"]), mine(sums["head_g"]),
                   mine(sums["a_ln_g"]), mine(sums["a_ln_b"]), sums["b_ln_g"], sums["b_ln_b"]]
        rows_small = _round_up(sum(-(-(w.shape[0] * w.shape[1]) // LANES) for w in w_small), 8)
        upd_small = _adamw(_pack_rows(w_small, rows_small), _pack_rows(g_small, rows_small),
                           _pack_rows(m_small, rows_small), _pack_rows(v_small, rows_small), name="adamw_small")
        return total[n_rows - 1, 0], g_small, upd_small

    rs = _GradReducer(w_big, m_big, v_big, long_name, core, chip_ix, as_views, small_step)
    grad_x = _local_step(
        x[0], loss_target[0], wa, wg, wao, wbi, wkv, wbo, a_gate_b, conv_w, conv_b, head_g,
        ln_g_a, ln_b_a, b_ln_g, b_ln_b, S=S, D=D, nha=nha, rs=rs)
    grad_x = rs.finish("w", rs.chip("w", grad_x))
    upd_big = [rs.results[k] for k in ("wa", "wao", "wkv", "wbi", "wbo")]
    g_big = [u[0] for u in upd_big]
    loss, g_small, upd_small = rs.small_results
    d_small, mn_small, vn_small = (_unpack_rows(u, [w.shape for w in w_small]) for u in upd_small)

    def assemble(big5, small8):
        awi, awo, kvw, bwi, bwo = big5
        gb, cw, cb, hg, alg, alb, blg, blb = small8
        return [awi[None], gb, cw[None], cb, hg, awo[None], alg, alb, kvw, bwi[None], bwo[None], blg, blb]

    grads = assemble(g_big, g_small)
    deltas = assemble([u[1] for u in upd_big], d_small)
    new_m = assemble([u[2] for u in upd_big], mn_small)
    new_v = assemble([u[3] for u in upd_big], vn_small)
    return (loss, grad_x[None], *grads, *deltas, *new_m, *new_v)
```

```python
import functools

import jax
import jax.numpy as jnp
from jax import lax
from jax.experimental import pallas as pl
from jax.experimental.pallas import tpu as pltpu
from jax.experimental.pallas import tpu_sc as plsc

F32 = jnp.float32
BF16 = jnp.bfloat16

DEPTH = 2
ALPHA = (2.0 * DEPTH) ** 0.25
LN_EPS = 1e-5
DK_A = 128
DV_A = 256
DH_B = 128
SB_TILE = 512
ML_TQ = 512
CONV_A = 4
ADAM_LR = 0.001
ADAM_B1 = 0.9
ADAM_B2 = 0.999
ADAM_EPS = 1e-08
ADAM_WD = 0.01
ADAM_STEP = 10
N_CHIPS = 4
LANES = 128
MXU_DEPTH = 256
V7X_VMEM_BYTES = 64 * 1024 * 1024
VMEM_LIMIT = (V7X_VMEM_BYTES * 3) // 4
NEG_BIG = -1e30
MESH = pl.DeviceIdType.MESH
ANY = pl.BlockSpec(memory_space=pl.ANY)


def _params(*sem):
    return pltpu.CompilerParams(dimension_semantics=sem, vmem_limit_bytes=VMEM_LIMIT)


def _dot(a, b, dims):
    return lax.dot_general(a, b, (dims, ((), ())), preferred_element_type=F32)


def _dot_nn(a, b):
    return _dot(a, b, ((1,), (0,)))


def _dot_nt(a, b):
    return _dot(a, b, ((1,), (1,)))


def _dot_tn(a, b):
    return _dot(a, b, ((0,), (0,)))


def _split2(x):
    hi = x.astype(BF16)
    lo = (x - hi.astype(F32)).astype(BF16)
    return hi, lo


def _split3(x):
    hi = x.astype(BF16)
    r = x - hi.astype(F32)
    mid = r.astype(BF16)
    lo = (r - mid.astype(F32)).astype(BF16)
    return hi, mid, lo


def _mask_dot2(x, t01):
    hi, lo = _split2(x)
    return _dot_nn(hi, t01) + _dot_nn(lo, t01)


def _block_sums(x, tri, later):
    sub = tri.shape[0]
    n = x.shape[1] // sub
    if n == 1:
        return _mask_dot2(x, tri)
    parts = [x[:, b * sub:(b + 1) * sub] for b in range(n)]
    sums = [jnp.sum(p, axis=1, keepdims=True) for p in parts]
    out = []
    for b in range(n):
        acc = _mask_dot2(parts[b], tri)
        for o in (range(b + 1, n) if later else range(b)):
            acc = acc + sums[o]
        out.append(acc)
    return jnp.concatenate(out, axis=1)


def _mask_dot3(x, t01):
    hi, mid, lo = _split3(x)
    return _dot_nn(hi, t01) + _dot_nn(mid, t01) + _dot_nn(lo, t01)


def _mask_dot3_left(t01, x):
    hi, mid, lo = _split3(x)
    return _dot_nn(t01, hi) + _dot_nn(t01, mid) + _dot_nn(t01, lo)


def _log_sigmoid(z):
    return jnp.minimum(z, 0.0) - jnp.log(1.0 + jnp.exp(-jnp.abs(z)))


def _sigmoid(z):
    return 1.0 / (1.0 + jnp.exp(-z))


def _mm(a, b, *, mode, M, N, K, out_dtype, name, tm=1024, tn=1024, tk=2048,
        a_off=(0, 0), b_off=(0, 0), acc_in=None, acc_scale=1.0, into=None, out_off=(0, 0), b_chunk=None):
    tm, tn, tk = min(tm, M), min(tn, N), min(tk, K)
    if b_chunk is not None:
        tn, tk = (min(tn, b_chunk), tk) if mode == "nn" else (tn, min(tk, b_chunk))
        assert b.ndim == 3 and b_off == (0, 0) and mode in ("nn", "nt")
    assert M % tm == 0 and N % tn == 0 and K % tk == 0
    nk = K // tk
    ar, ac = a_off
    br, bc = b_off
    orow, ocol = out_off
    if mode in ("nn", "nt"):
        assert ar % tm == 0 and ac % tk == 0
        a_spec = pl.BlockSpec((tm, tk), lambda i, j, k: (i + ar // tm, k + ac // tk))
    else:
        assert ar % tk == 0 and ac % tm == 0
        a_spec = pl.BlockSpec((tk, tm), lambda i, j, k: (k + ar // tk, i + ac // tm))
    if b_chunk is not None and mode == "nn":
        per = b_chunk // tn
        b_spec = pl.BlockSpec((None, tk, tn), lambda i, j, k: (j // per, k, j % per))
    elif b_chunk is not None:
        per = b_chunk // tk
        b_spec = pl.BlockSpec((None, tn, tk), lambda i, j, k: (k // per, j, k % per))
    elif mode == "nt":
        assert br % tn == 0 and bc % tk == 0
        b_spec = pl.BlockSpec((tn, tk), lambda i, j, k: (j + br // tn, k + bc // tk))
    else:
        assert br % tk == 0 and bc % tn == 0
        b_spec = pl.BlockSpec((tk, tn), lambda i, j, k: (k + br // tk, j + bc // tn))
    assert orow % tm == 0 and ocol % tn == 0
    o_spec = pl.BlockSpec((tm, tn), lambda i, j, k: (i + orow // tm, j + ocol // tn))
    dims = {"nn": ((1,), (0,)), "nt": ((1,), (1,)), "tn": ((0,), (0,))}[mode]
    inputs, in_specs = [a, b], [a_spec, b_spec]
    has_acc = acc_in is not None
    if has_acc:
        inputs.append(acc_in)
        in_specs.append(pl.BlockSpec((tm, tn), lambda i, j, k: (i, j)))
    aliases = {}
    if into is not None:
        inputs.append(into)
        in_specs.append(ANY)
        aliases = {len(inputs) - 1: 0}
        out_shape = jax.ShapeDtypeStruct(into.shape, into.dtype)
        assert into.dtype == out_dtype
    else:
        out_shape = jax.ShapeDtypeStruct((M, N), out_dtype)

    def body(*refs):
        a_ref, b_ref = refs[0], refs[1]
        acc_in_ref = refs[2] if has_acc else None
        n_in = len(inputs)
        o_ref = refs[n_in]

        def first():
            if has_acc:
                return acc_scale * acc_in_ref[...]
            return None

        if nk == 1:
            r = _dot(a_ref[...], b_ref[...], dims)
            f = first()
            if f is not None:
                r = r + f
            o_ref[...] = r.astype(o_ref.dtype)
        else:
            acc_ref = refs[n_in + 1]
            kk = pl.program_id(2)

            @pl.when(kk == 0)
            def _():
                f = first()
                acc_ref[...] = jnp.zeros_like(acc_ref) if f is None else f

            acc_ref[...] += _dot(a_ref[...], b_ref[...], dims)

            @pl.when(kk == nk - 1)
            def _():
                o_ref[...] = acc_ref[...].astype(o_ref.dtype)

    scratch = [] if nk == 1 else [pltpu.VMEM((tm, tn), F32)]
    return pl.pallas_call(
        body, name=name, out_shape=out_shape, grid=(M // tm, N // tn, nk),
        in_specs=in_specs, out_specs=o_spec, scratch_shapes=scratch,
        input_output_aliases=aliases,
        compiler_params=_params("parallel", "parallel", "arbitrary"),
    )(*inputs)


def _join_chunk_edges(w, d):
    nch, rows, _ = w.shape

    def body(w_ref, o_ref, first, tail, sems):
        del w_ref
        for j in range(1, nch):
            mine = o_ref.at[j, :, pl.ds(0, LANES)]
            loads = [pltpu.make_async_copy(mine, first, sems.at[0]),
                     pltpu.make_async_copy(o_ref.at[j - 1, :, pl.ds(d, LANES)], tail, sems.at[1])]
            for cp in loads:
                cp.start()
            for cp in loads:
                cp.wait()
            first[...] = first[...] + tail[...]
            store = pltpu.make_async_copy(first, mine, sems.at[0])
            store.start()
            store.wait()

    return pl.pallas_call(
        body, name="join_chunk_edges", out_shape=jax.ShapeDtypeStruct(w.shape, w.dtype),
        in_specs=[ANY], out_specs=ANY, input_output_aliases={0: 0},
        scratch_shapes=[pltpu.VMEM((rows, LANES), w.dtype), pltpu.VMEM((rows, LANES), w.dtype),
                        pltpu.SemaphoreType.DMA((2,))],
    )(w)


def _shift_down(x, d, row):
    if d == 0:
        return x
    return jnp.where(row >= d, pltpu.roll(x, d, 0), 0.0)


def _shift_up(x, d, row, n):
    if d == 0:
        return x
    return jnp.where(row < n - d, pltpu.roll(x, n - d, 0), 0.0)


def _conv_pre(x, w_ref, b_ref, row):
    c = b_ref[...] + w_ref[CONV_A - 1:CONV_A, :] * x
    for k in range(CONV_A - 1):
        c = c + w_ref[k:k + 1, :] * _shift_down(x, CONV_A - 1 - k, row)
    return c


def _conv_fwd(u, conv_w, conv_b, *, S, D, name):
    tc = 256
    nq_blocks = (D // 2) // tc

    def body(u_ref, w_ref, b_ref, o_ref):
        x = u_ref[...]
        row = lax.broadcasted_iota(jnp.int32, x.shape, 0)
        c = _conv_pre(x, w_ref, b_ref, row)
        scale = jnp.where(pl.program_id(0) >= nq_blocks, DK_A ** -0.5, 1.0).astype(F32)
        o_ref[...] = (c * _sigmoid(c) * scale).astype(BF16)

    return pl.pallas_call(
        body, name=name, out_shape=jax.ShapeDtypeStruct((S, D), BF16), grid=(D // tc,),
        in_specs=[pl.BlockSpec((S, tc), lambda j: (0, j)),
                  pl.BlockSpec((CONV_A, tc), lambda j: (0, j)),
                  pl.BlockSpec((1, tc), lambda j: (0, j))],
        out_specs=pl.BlockSpec((S, tc), lambda j: (0, j)),
        compiler_params=_params("parallel"),
    )(u, conv_w, conv_b)


def _conv_bwd(u, dq, dk, conv_w, conv_b, du, *, S, D, name):
    tc = 256
    nq_blocks = (D // 2) // tc

    def body(u_ref, dq_ref, dk_ref, w_ref, b_ref, du_in, du_ref, dw_ref, db_ref):
        del du_in
        x = u_ref[...]
        n = x.shape[0]
        row = lax.broadcasted_iota(jnp.int32, x.shape, 0)
        c = _conv_pre(x, w_ref, b_ref, row)
        is_k = pl.program_id(0) >= nq_blocks
        dy = jnp.where(is_k, dk_ref[...] * (DK_A ** -0.5), dq_ref[...])
        sg = _sigmoid(c)
        dc = dy * (sg * (1.0 + c * (1.0 - sg)))
        db_ref[...] = jnp.sum(dc, axis=0, keepdims=True)
        dx = w_ref[CONV_A - 1:CONV_A, :] * dc
        dw_ref[CONV_A - 1:CONV_A, :] = jnp.sum(dc * x, axis=0, keepdims=True)
        for k in range(CONV_A - 1):
            d = CONV_A - 1 - k
            dw_ref[k:k + 1, :] = jnp.sum(dc * _shift_down(x, d, row), axis=0, keepdims=True)
            dx = dx + w_ref[k:k + 1, :] * _shift_up(dc, d, row, n)
        du_ref[...] = dx.astype(BF16)

    half = lambda j: (0, j % nq_blocks)
    return pl.pallas_call(
        body, name=name,
        out_shape=(jax.ShapeDtypeStruct(du.shape, du.dtype),
                   jax.ShapeDtypeStruct((CONV_A, D), F32), jax.ShapeDtypeStruct((1, D), F32)),
        grid=(D // tc,),
        in_specs=[pl.BlockSpec((S, tc), lambda j: (0, j)),
                  pl.BlockSpec((S, tc), half), pl.BlockSpec((S, tc), half),
                  pl.BlockSpec((CONV_A, tc), lambda j: (0, j)),
                  pl.BlockSpec((1, tc), lambda j: (0, j)), ANY],
        out_specs=(pl.BlockSpec((S, tc), lambda j: (0, j)),
                   pl.BlockSpec((CONV_A, tc), lambda j: (0, j)),
                   pl.BlockSpec((1, tc), lambda j: (0, j))),
        input_output_aliases={5: 0},
        compiler_params=_params("parallel"),
    )(u, dq, dk, conv_w, conv_b, du)


def _tri(n, cmp):
    r = lax.broadcasted_iota(jnp.int32, (n, n), 0)
    c = lax.broadcasted_iota(jnp.int32, (n, n), 1)
    return r, c, cmp(r, c)


def _gates_fwd(gt, bias, *, nha, nb, name):
    half = nha * nb
    nb_shift = nb.bit_length() - 1
    assert nb == 1 << nb_shift

    def body(g_ref, b_ref, f_ref, brow_ref):
        ig = g_ref[0:half, :] + b_ref[0:half, :]
        fg = g_ref[half:2 * half, :] + b_ref[half:2 * half, :]
        lf = _log_sigmoid(fg)
        _, _, upper = _tri(LANES, lambda r, c: r <= c)
        cs = _mask_dot3(lf, upper.astype(BF16))
        tot = jnp.broadcast_to(cs[:, LANES - 1:LANES], cs.shape)
        r, c, _ = _tri(half, lambda r, c: r <= c)
        before = jnp.logical_and(r >> nb_shift == c >> nb_shift, c < r).astype(BF16)
        f = cs + _mask_dot3_left(before, tot)
        f_ref[...] = f
        brow_ref[...] = ig - f

    return pl.pallas_call(
        body, name=name,
        out_shape=(jax.ShapeDtypeStruct((half, LANES), F32), jax.ShapeDtypeStruct((half, LANES), F32)),
    )(gt, bias)


def _gates_bwd(rowsum, colsum, gt, bias, *, nha, nb, name):
    half = nha * nb
    nb_shift = nb.bit_length() - 1
    assert nb == 1 << nb_shift

    def body(rs_ref, cs_ref, g_ref, b_ref, dg_ref, tot_ref):
        col = cs_ref[...]
        df = rs_ref[...] - col
        _, _, lower = _tri(LANES, lambda r, c: r >= c)
        rc = _mask_dot3(df, lower.astype(BF16))
        tot = jnp.broadcast_to(rc[:, 0:1], rc.shape)
        r, c, _ = _tri(half, lambda r, c: r <= c)
        same = r >> nb_shift == c >> nb_shift
        after = jnp.logical_and(same, c > r).astype(BF16)
        dlf = rc + _mask_dot3_left(after, tot)
        fg = g_ref[half:2 * half, :] + b_ref[half:2 * half, :]
        dfg = dlf * _sigmoid(-fg)
        dg_ref[0:half, :] = col
        dg_ref[half:2 * half, :] = dfg
        grp = same.astype(BF16)
        ones = jnp.ones((LANES, LANES), BF16)
        tot_ref[0:half, :] = _mask_dot3_left(grp, _mask_dot3(col, ones))
        tot_ref[half:2 * half, :] = _mask_dot3_left(grp, _mask_dot3(dfg, ones))

    return pl.pallas_call(
        body, name=name,
        out_shape=(jax.ShapeDtypeStruct((2 * half, LANES), F32), jax.ShapeDtypeStruct((2 * half, LANES), F32)),
    )(rowsum, colsum, gt, bias)


def _mlstm_tile(q, k_ref, fcol, brow_ref, j, tq, diagonal):
    off = pl.multiple_of(j * tq, tq)
    kj = k_ref[pl.ds(off, tq), :]
    s = _dot_nt(q, kj)
    logd = fcol + brow_ref[0, :, pl.ds(off, tq)]
    if diagonal:
        valid = lax.broadcasted_iota(jnp.int32, (tq, tq), 1) <= lax.broadcasted_iota(jnp.int32, (tq, tq), 0)
        logd = jnp.where(valid, logd, NEG_BIG)
    return off, kj, s, logd


def _mlstm_fwd(qk, u, fcol, brow, *, S, D, nha, name):
    tq = min(ML_TQ, S)
    nq = S // tq
    kb, vb = (D // 2) // DK_A, D // DV_A

    def body(q_ref, k_ref, v_ref, fcol_ref, brow_ref, h_ref, m_ref, den_ref):
        i = pl.program_id(1)
        q = q_ref[...]
        fc = fcol_ref[0]

        def step(j, carry, diagonal):
            acc, den, m = carry
            off, _, s, logd = _mlstm_tile(q, k_ref, fc, brow_ref, j, tq, diagonal)
            m_new = jnp.maximum(m, jnp.max(logd, axis=1, keepdims=True))
            a = s * jnp.exp(logd - m_new)
            alpha = jnp.exp(m - m_new)
            vj = v_ref[pl.ds(off, tq), :].astype(BF16)
            acc = alpha * acc + _dot_nn(a.astype(BF16), vj)
            den = alpha * den + jnp.sum(a, axis=1, keepdims=True)
            return acc, den, m_new

        first = step(i, (jnp.zeros((tq, DV_A), F32), jnp.zeros((tq, 1), F32), jnp.full((tq, 1), NEG_BIG, F32)), True)
        acc, den, m = lax.fori_loop(0, i, lambda j, c: step(j, c, False), first)
        h_ref[...] = acc / jnp.maximum(jnp.abs(den), jnp.exp(-m))
        m_ref[0] = m
        den_ref[0] = den

    stat = pl.BlockSpec((1, tq, 1), lambda h, i: (h, i, 0))
    return pl.pallas_call(
        body, name=name,
        out_shape=(jax.ShapeDtypeStruct((S, D), F32), jax.ShapeDtypeStruct((nha, S, 1), F32),
                   jax.ShapeDtypeStruct((nha, S, 1), F32)),
        grid=(nha, nq),
        in_specs=[pl.BlockSpec((tq, DK_A), lambda h, i: (i, h)),
                  pl.BlockSpec((S, DK_A), lambda h, i: (0, kb + h)),
                  pl.BlockSpec((S, DV_A), lambda h, i: (0, vb + h)),
                  stat, pl.BlockSpec((1, 1, S), lambda h, i: (h, 0, 0))],
        out_specs=(pl.BlockSpec((tq, DV_A), lambda h, i: (i, h)), stat, stat),
        compiler_params=_params("parallel", "arbitrary"),
    )(qk, qk, u, fcol, brow)


def _mlstm_bwd(qk, u, fcol, brow, m, den, dh, h, du, *, S, D, nha, name):
    tq = min(ML_TQ, S)
    nq = S // tq
    kb, vb = (D // 2) // DK_A, D // DV_A

    def body(q_ref, k_ref, v_ref, fcol_ref, brow_ref, m_ref, den_ref, dh_ref, h_ref, du_in,
             du_ref, dq_ref, dk_ref, rs_ref, cs_ref, dv_acc):
        del du_in
        i = pl.program_id(1)

        @pl.when(i == 0)
        def _():
            dk_ref[...] = jnp.zeros_like(dk_ref)
            cs_ref[...] = jnp.zeros_like(cs_ref)
            dv_acc[...] = jnp.zeros_like(dv_acc)

        q = q_ref[...]
        fc = fcol_ref[0]
        mm = m_ref[0]
        dn = den_ref[0]
        floor = jnp.exp(-mm)
        nrm = jnp.maximum(jnp.abs(dn), floor)
        dhv = dh_ref[...]
        dnum = dhv / nrm
        dnrm = -jnp.sum(dhv * h_ref[...], axis=1, keepdims=True) / nrm
        dden = jnp.where(jnp.abs(dn) > floor, jnp.where(dn > 0.0, dnrm, -dnrm), 0.0)
        dnum_b = dnum.astype(BF16)

        def step(j, carry, diagonal):
            dq, rs = carry
            off, kj, s, logd = _mlstm_tile(q, k_ref, fc, brow_ref, j, tq, diagonal)
            p = jnp.exp(logd - mm)
            a = s * p
            vj = v_ref[pl.ds(off, tq), :].astype(BF16)
            da = _dot_nt(dnum_b, vj) + dden
            dv_acc[pl.ds(off, tq), :] += _dot_tn(a.astype(BF16), dnum_b)
            dqk = (da * p).astype(BF16)
            dq = dq + _dot_nn(dqk, kj)
            dk_ref[pl.ds(off, tq), :] += _dot_tn(dqk, q)
            pm = da * a
            cs_ref[0, :, pl.ds(off, tq)] += jnp.sum(pm, axis=0, keepdims=True)
            rs = rs + jnp.sum(pm, axis=1, keepdims=True)
            return dq, rs

        first = step(i, (jnp.zeros((tq, DK_A), F32), jnp.zeros((tq, 1), F32)), True)
        dq, rs = lax.fori_loop(0, i, lambda j, c: step(j, c, False), first)
        dq_ref[...] = dq
        rs_ref[0] = rs

        @pl.when(i == nq - 1)
        def _():
            du_ref[...] = dv_acc[...].astype(BF16)

    stat = pl.BlockSpec((1, tq, 1), lambda h, i: (h, i, 0))
    rowv = pl.BlockSpec((1, 1, S), lambda h, i: (h, 0, 0))
    hblk = pl.BlockSpec((tq, DV_A), lambda h, i: (i, h))
    return pl.pallas_call(
        body, name=name,
        out_shape=(jax.ShapeDtypeStruct(du.shape, du.dtype),
                   jax.ShapeDtypeStruct((S, D // 2), F32), jax.ShapeDtypeStruct((S, D // 2), F32),
                   jax.ShapeDtypeStruct((nha, S, 1), F32), jax.ShapeDtypeStruct((nha, 1, S), F32)),
        grid=(nha, nq),
        in_specs=[pl.BlockSpec((tq, DK_A), lambda h, i: (i, h)),
                  pl.BlockSpec((S, DK_A), lambda h, i: (0, kb + h)),
                  pl.BlockSpec((S, DV_A), lambda h, i: (0, vb + h)),
                  stat, rowv, stat, stat, hblk, hblk, ANY],
        out_specs=(pl.BlockSpec((S, DV_A), lambda h, i: (0, vb + h)),
                   pl.BlockSpec((tq, DK_A), lambda h, i: (i, h)),
                   pl.BlockSpec((S, DK_A), lambda h, i: (0, h)),
                   stat, rowv),
        scratch_shapes=[pltpu.VMEM((S, DV_A), F32)],
        input_output_aliases={9: 0},
        compiler_params=_params("parallel", "arbitrary"),
    )(qk, qk, u, fcol, brow, m, den, dh, h, du)


def _head_norm(hh):
    mu = jnp.mean(hh, axis=1, keepdims=True)
    hc = hh - mu
    rstd = lax.rsqrt(jnp.mean(hc * hc, axis=1, keepdims=True) + LN_EPS)
    return hc * rstd, rstd


def _hgate_fwd(h, u, head_g, *, S, D, name):
    tm = min(256, S)
    nh = D // DV_A

    def body(h_ref, o_ref, z_ref, g_ref, out_ref):
        for hd in range(nh):
            sl = slice(hd * DV_A, (hd + 1) * DV_A)
            hn, _ = _head_norm(h_ref[:, sl])
            z = z_ref[:, sl]
            out_ref[:, sl] = (_sigmoid(o_ref[:, sl]) * (hn * g_ref[:, sl]) * (z * _sigmoid(z))).astype(BF16)

    return pl.pallas_call(
        body, name=name, out_shape=jax.ShapeDtypeStruct((S, D), BF16), grid=(S // tm,),
        in_specs=[pl.BlockSpec((tm, D), lambda i: (i, 0)), pl.BlockSpec((tm, D), lambda i: (i, 2)),
                  pl.BlockSpec((tm, D), lambda i: (i, 3)), pl.BlockSpec((1, D), lambda i: (0, 0))],
        out_specs=pl.BlockSpec((tm, D), lambda i: (i, 0)),
        compiler_params=_params("parallel"),
    )(h, u, u, head_g)


def _hgate_bwd(dhg, h, u, head_g, du, *, S, D, name):
    tm = min(256, S)
    nh = D // DV_A

    def body(dhg_ref, h_ref, o_ref, z_ref, g_ref, du_in, du_ref, dh_ref, dg_ref):
        del du_in

        @pl.when(pl.program_id(0) == 0)
        def _():
            dg_ref[...] = jnp.zeros_like(dg_ref)

        for hd in range(nh):
            sl = slice(hd * DV_A, (hd + 1) * DV_A)
            hn, rstd = _head_norm(h_ref[:, sl])
            o, z, g, d = o_ref[:, sl], z_ref[:, sl], g_ref[:, sl], dhg_ref[:, sl]
            so, sz = _sigmoid(o), _sigmoid(z)
            silu_z = z * sz
            hng = hn * g
            du_ref[:, sl] = (d * hng * silu_z * so * (1.0 - so)).astype(BF16)
            du_ref[:, D + hd * DV_A:D + (hd + 1) * DV_A] = (
                d * so * hng * (sz * (1.0 + z * (1.0 - sz)))).astype(BF16)
            t = d * so * silu_z
            dg_ref[:, sl] += jnp.sum(t * hn, axis=0, keepdims=True)
            dhn = t * g
            dh_ref[:, sl] = rstd * (dhn - jnp.mean(dhn, axis=1, keepdims=True)
                                    - hn * jnp.mean(dhn * hn, axis=1, keepdims=True))

    row = lambda i: (i, 0)
    return pl.pallas_call(
        body, name=name,
        out_shape=(jax.ShapeDtypeStruct(du.shape, du.dtype), jax.ShapeDtypeStruct((S, D), F32),
                   jax.ShapeDtypeStruct((1, D), F32)),
        grid=(S // tm,),
        in_specs=[pl.BlockSpec((tm, D), row), pl.BlockSpec((tm, D), row),
                  pl.BlockSpec((tm, D), lambda i: (i, 2)), pl.BlockSpec((tm, D), lambda i: (i, 3)),
                  pl.BlockSpec((1, D), lambda i: (0, 0)), ANY],
        out_specs=(pl.BlockSpec((tm, 2 * D), lambda i: (i, 1)), pl.BlockSpec((tm, D), row),
                   pl.BlockSpec((1, D), lambda i: (0, 0))),
        input_output_aliases={5: 0},
        compiler_params=_params("arbitrary"),
    )(dhg, h, u, u, head_g, du)


def _ln_stats(r):
    mu = jnp.mean(r, axis=1, keepdims=True)
    xc = r - mu
    rstd = lax.rsqrt(jnp.mean(xc * xc, axis=1, keepdims=True) + LN_EPS)
    return xc * rstd, rstd


def _ln_back(dxhat, xhat, rstd):
    return rstd * (dxhat - jnp.mean(dxhat, axis=1, keepdims=True)
                   - xhat * jnp.mean(dxhat * xhat, axis=1, keepdims=True))


def _ln_fwd(x, y, g, b, *, S, D, name):
    tm = min(256, S)

    def body(x_ref, y_ref, g_ref, b_ref, o_ref, ob_ref):
        xhat, _ = _ln_stats(ALPHA * x_ref[...] + y_ref[...])
        o = xhat * g_ref[...] + b_ref[...]
        o_ref[...] = o
        ob_ref[...] = o.astype(BF16)

    row = lambda i: (i, 0)
    vec = pl.BlockSpec((1, D), lambda i: (0, 0))
    return pl.pallas_call(
        body, name=name,
        out_shape=(jax.ShapeDtypeStruct((S, D), F32), jax.ShapeDtypeStruct((S, D), BF16)),
        grid=(S // tm,),
        in_specs=[pl.BlockSpec((tm, D), row), pl.BlockSpec((tm, D), row), vec, vec],
        out_specs=(pl.BlockSpec((tm, D), row), pl.BlockSpec((tm, D), row)),
        compiler_params=_params("parallel"),
    )(x, y, g, b)


def _ln_loss_bwd(x1, y2, target, g, b, *, S, D, name):
    tm = min(256, S)

    def body(x_ref, y_ref, t_ref, g_ref, b_ref, dr_ref, drb_ref, dg_ref, db_ref, loss_ref):
        @pl.when(pl.program_id(0) == 0)
        def _():
            dg_ref[...] = jnp.zeros_like(dg_ref)
            db_ref[...] = jnp.zeros_like(db_ref)
            loss_ref[...] = jnp.zeros_like(loss_ref)

        xhat, rstd = _ln_stats(ALPHA * x_ref[...] + y_ref[...])
        diff = xhat * g_ref[...] + b_ref[...] - t_ref[...]
        loss_ref[...] += (0.5 / D) * jnp.sum(diff * diff)
        dx2 = diff * (1.0 / D)
        dg_ref[...] += jnp.sum(dx2 * xhat, axis=0, keepdims=True)
        db_ref[...] += jnp.sum(dx2, axis=0, keepdims=True)
        dr = _ln_back(dx2 * g_ref[...], xhat, rstd)
        dr_ref[...] = dr
        drb_ref[...] = dr.astype(BF16)

    row = lambda i: (i, 0)
    vec = pl.BlockSpec((1, D), lambda i: (0, 0))
    return pl.pallas_call(
        body, name=name,
        out_shape=(jax.ShapeDtypeStruct((S, D), F32), jax.ShapeDtypeStruct((S, D), BF16),
                   jax.ShapeDtypeStruct((1, D), F32), jax.ShapeDtypeStruct((1, D), F32),
                   jax.ShapeDtypeStruct((1, LANES), F32)),
        grid=(S // tm,),
        in_specs=[pl.BlockSpec((tm, D), row)] * 3 + [vec, vec],
        out_specs=(pl.BlockSpec((tm, D), row), pl.BlockSpec((tm, D), row), vec, vec,
                   pl.BlockSpec((1, LANES), lambda i: (0, 0))),
        compiler_params=_params("arbitrary"),
    )(x1, y2, target, g, b)


def _ln_bwd(x, y, g, dout, *, S, D, name):
    tm = min(256, S)

    def body(x_ref, y_ref, g_ref, d_ref, dr_ref, drb_ref, dg_ref, db_ref):
        @pl.when(pl.program_id(0) == 0)
        def _():
            dg_ref[...] = jnp.zeros_like(dg_ref)
            db_ref[...] = jnp.zeros_like(db_ref)

        xhat, rstd = _ln_stats(ALPHA * x_ref[...] + y_ref[...])
        d = d_ref[...]
        dg_ref[...] += jnp.sum(d * xhat, axis=0, keepdims=True)
        db_ref[...] += jnp.sum(d, axis=0, keepdims=True)
        dr = _ln_back(d * g_ref[...], xhat, rstd)
        dr_ref[...] = dr
        drb_ref[...] = dr.astype(BF16)

    row = lambda i: (i, 0)
    vec = pl.BlockSpec((1, D), lambda i: (0, 0))
    return pl.pallas_call(
        body, name=name,
        out_shape=(jax.ShapeDtypeStruct((S, D), F32), jax.ShapeDtypeStruct((S, D), BF16),
                   jax.ShapeDtypeStruct((1, D), F32), jax.ShapeDtypeStruct((1, D), F32)),
        grid=(S // tm,),
        in_specs=[pl.BlockSpec((tm, D), row), pl.BlockSpec((tm, D), row), vec, pl.BlockSpec((tm, D), row)],
        out_specs=(pl.BlockSpec((tm, D), row), pl.BlockSpec((tm, D), row), vec, vec),
        compiler_params=_params("arbitrary"),
    )(x, y, g, dout)


def _sb_scores(q, kj, diagonal, t):
    z = _dot_nt(q, kj) * (DH_B ** -0.5)
    ls = _log_sigmoid(z)
    if not diagonal:
        return None, ls, ls - z
    valid = lax.broadcasted_iota(jnp.int32, (t, t), 1) < lax.broadcasted_iota(jnp.int32, (t, t), 0)
    return valid, ls, jnp.where(valid, ls - z, 0.0)


def _keep(valid, x):
    return x if valid is None else jnp.where(valid, x, 0.0)


def _sb_fwd(q2, kv, *, S, D, name, t=SB_TILE):
    t = min(t, S)
    nq = S // t
    nh = D // DH_B

    def body(q_ref, k_ref, v_ref, o_ref, tot_ref):
        i = pl.program_id(1)
        q = q_ref[...]
        _, _, after = _tri(min(t, MXU_DEPTH), lambda r, c: r > c)
        tri_after = after.astype(BF16)

        def step(j, carry, diagonal):
            acc, cr = carry
            off = pl.multiple_of(j * t, t)
            valid, ls, lneg = _sb_scores(q, k_ref[pl.ds(off, t), :], diagonal, t)
            between = cr + _block_sums(lneg, tri_after, True)
            a = _keep(valid, jnp.exp(ls + between))
            acc = acc + _dot_nn(a.astype(BF16), v_ref[pl.ds(off, t), :])
            return acc, cr + jnp.sum(lneg, axis=1, keepdims=True)

        carry = step(i, (jnp.zeros((t, DH_B), F32), jnp.zeros((t, 1), F32)), True)
        acc, cr = lax.fori_loop(0, i, lambda jj, c: step(i - 1 - jj, c, False), carry)
        o_ref[...] = acc
        tot_ref[0] = cr

    return pl.pallas_call(
        body, name=name,
        out_shape=(jax.ShapeDtypeStruct((S, D), F32), jax.ShapeDtypeStruct((nh, S, 1), F32)), grid=(nh, nq),
        in_specs=[pl.BlockSpec((t, DH_B), lambda h, i: (i, h)),
                  pl.BlockSpec((S, DH_B), lambda h, i: (0, h)),
                  pl.BlockSpec((S, DH_B), lambda h, i: (0, nh + h))],
        out_specs=(pl.BlockSpec((t, DH_B), lambda h, i: (i, h)), pl.BlockSpec((1, t, 1), lambda h, i: (h, i, 0))),
        compiler_params=_params("parallel", "arbitrary"),
    )(q2, kv, kv)


def _sb_bwd(q2, kv, datt, tot, *, S, D, name, t=SB_TILE):
    t = min(t, S)
    nq = S // t
    nh = D // DH_B

    def body(q_ref, k_ref, v_ref, do_ref, tot_ref, dq_ref, dk_ref, dv_ref, dk_acc, dv_acc):
        i = pl.program_id(1)

        @pl.when(i == 0)
        def _():
            dk_acc[...] = jnp.zeros_like(dk_acc)
            dv_acc[...] = jnp.zeros_like(dv_acc)

        q = q_ref[...]
        do_b = do_ref[...].astype(BF16)
        _, _, after = _tri(min(t, MXU_DEPTH), lambda r, c: r > c)
        tri_after = after.astype(BF16)
        _, _, before = _tri(min(t, MXU_DEPTH), lambda r, c: r < c)
        tri_before = before.astype(BF16)

        def step(j, carry, diagonal):
            dq, rest, cg = carry
            off = pl.multiple_of(j * t, t)
            kj = k_ref[pl.ds(off, t), :]
            valid, ls, lneg = _sb_scores(q, kj, diagonal, t)
            rest = rest - jnp.sum(lneg, axis=1, keepdims=True)
            between = rest + _block_sums(lneg, tri_after, True)
            a = _keep(valid, jnp.exp(ls + between))
            g = _dot_nt(do_b, v_ref[pl.ds(off, t), :]) * a
            dv_acc[pl.ds(off, t), :] += _dot_tn(a.astype(BF16), do_b)
            e = cg + _block_sums(g, tri_before, False)
            dz = _keep(valid, g * jnp.exp(lneg) - e * jnp.exp(ls)) * (DH_B ** -0.5)
            dz_b = dz.astype(BF16)
            dq = dq + _dot_nn(dz_b, kj)
            dk_acc[pl.ds(off, t), :] += _dot_tn(dz_b, q)
            return dq, rest, cg + jnp.sum(g, axis=1, keepdims=True)

        carry = lax.fori_loop(0, i, lambda j, c: step(j, c, False),
                              (jnp.zeros((t, DH_B), F32), tot_ref[0], jnp.zeros((t, 1), F32)))
        dq, _, _ = step(i, carry, True)
        dq_ref[...] = dq.astype(BF16)

        @pl.when(i == nq - 1)
        def _():
            dk_ref[...] = dk_acc[...].astype(BF16)
            dv_ref[...] = dv_acc[...].astype(BF16)

    blk = pl.BlockSpec((t, DH_B), lambda h, i: (i, h))
    return pl.pallas_call(
        body, name=name,
        out_shape=(jax.ShapeDtypeStruct((S, D), BF16), jax.ShapeDtypeStruct((S, D), BF16),
                   jax.ShapeDtypeStruct((S, D), BF16)),
        grid=(nh, nq),
        in_specs=[blk, pl.BlockSpec((S, DH_B), lambda h, i: (0, h)),
                  pl.BlockSpec((S, DH_B), lambda h, i: (0, nh + h)), blk,
                  pl.BlockSpec((1, t, 1), lambda h, i: (h, i, 0))],
        out_specs=(blk, pl.BlockSpec((S, DH_B), lambda h, i: (0, h)),
                   pl.BlockSpec((S, DH_B), lambda h, i: (0, h))),
        scratch_shapes=[pltpu.VMEM((S, DH_B), F32), pltpu.VMEM((S, DH_B), F32)],
        compiler_params=_params("parallel", "arbitrary"),
    )(q2, kv, kv, datt, tot)


def _bgate_fwd(att, z2, *, S, D, name):
    tm = min(256, S)

    def body(a_ref, z_ref, o_ref):
        z = z_ref[...]
        o_ref[...] = (a_ref[...] * (z * _sigmoid(z))).astype(BF16)

    row = lambda i: (i, 0)
    return pl.pallas_call(
        body, name=name, out_shape=jax.ShapeDtypeStruct((S, D), BF16), grid=(S // tm,),
        in_specs=[pl.BlockSpec((tm, D), row)] * 2, out_specs=pl.BlockSpec((tm, D), row),
        compiler_params=_params("parallel"),
    )(att, z2)


def _bgate_bwd(dhb, att, z2, *, S, D, name):
    tm = min(256, S)

    def body(d_ref, a_ref, z_ref, da_ref, dz_ref):
        z, d = z_ref[...], d_ref[...]
        sz = _sigmoid(z)
        da_ref[...] = d * (z * sz)
        dz_ref[...] = (d * a_ref[...] * (sz * (1.0 + z * (1.0 - sz)))).astype(BF16)

    row = lambda i: (i, 0)
    return pl.pallas_call(
        body, name=name,
        out_shape=(jax.ShapeDtypeStruct((S, D), F32), jax.ShapeDtypeStruct((S, D), BF16)),
        grid=(S // tm,),
        in_specs=[pl.BlockSpec((tm, D), row)] * 3,
        out_specs=(pl.BlockSpec((tm, D), row), pl.BlockSpec((tm, D), row)),
        compiler_params=_params("parallel"),
    )(dhb, att, z2)


def _adamw_math(w, g, m, v):
    mn = ADAM_B1 * m + (1.0 - ADAM_B1) * g
    vn = ADAM_B2 * v + (1.0 - ADAM_B2) * (g * g)
    m_hat = mn / (1.0 - ADAM_B1 ** ADAM_STEP)
    v_hat = vn / (1.0 - ADAM_B2 ** ADAM_STEP)
    return -ADAM_LR * (m_hat / (jnp.sqrt(v_hat) + ADAM_EPS) + ADAM_WD * w), mn, vn


def _adamw(w, g, m, v, *, name, tr=128):
    rows, cols = w.shape
    tr = min(tr, rows)
    assert rows % tr == 0

    def body(w_ref, g_ref, m_ref, v_ref, d_ref, mo_ref, vo_ref):
        d, mn, vn = _adamw_math(w_ref[...], g_ref[...], m_ref[...], v_ref[...])
        d_ref[...] = d
        mo_ref[...] = mn
        vo_ref[...] = vn

    blk = pl.BlockSpec((tr, cols), lambda i: (i, 0))
    sd = jax.ShapeDtypeStruct((rows, cols), F32)
    return pl.pallas_call(
        body, name=name, out_shape=(sd, sd, sd), grid=(rows // tr,),
        in_specs=[blk] * 4, out_specs=(blk, blk, blk),
        compiler_params=_params("parallel"),
    )(w, g, m, v)


def _local_step(x, target, wa, wg, wao, wbi, wkv, wbo, gate_b, conv_w, conv_b, head_g,
                a_ln_g, a_ln_b, b_ln_g, b_ln_b, *, S, D, nha, rs):
    nb = S // LANES
    kw = dict(S=S, D=D)
    xb = x.astype(BF16)
    wa_chunk = D if wa.ndim == 3 else None
    u = _mm(xb, wa, mode="nn", M=S, N=4 * D, K=D, out_dtype=F32, name="a_in", b_chunk=wa_chunk)
    ug = _mm(xb, wg, mode="nn", M=S, N=LANES, K=D, out_dtype=F32, name="a_in_gates")
    qk = _conv_fwd(u, conv_w, conv_b, name="conv_fwd", **kw)
    gt = ug[:, :2 * nha].T.reshape(2 * nha * nb, LANES)
    gbias = jnp.repeat(gate_b.reshape(2 * nha), nb).reshape(2 * nha * nb, 1)
    fcs, brow = _gates_fwd(gt, gbias, nha=nha, nb=nb, name="gates_fwd")
    fcol = fcs.reshape(nha, S, 1)
    brow = brow.reshape(nha, 1, S)
    h, m, den = _mlstm_fwd(qk, u, fcol, brow, nha=nha, name="mlstm_fwd", **kw)
    hg = _hgate_fwd(h, u, head_g, name="hgate_fwd", **kw)
    y = _mm(hg, wao, mode="nn", M=S, N=D, K=D, out_dtype=F32, name="a_out")
    x1, x1b = _ln_fwd(x, y, a_ln_g, a_ln_b, name="ln_a_fwd", **kw)
    q2 = _mm(x1b, wbi, mode="nn", M=S, N=D, K=D, out_dtype=BF16, name="b_in_q")
    z2 = _mm(x1b, wbi, mode="nn", M=S, N=D, K=D, out_dtype=F32, name="b_in_z", b_off=(0, D))
    kv = _mm(x1b, wkv, mode="nn", M=S, N=2 * D, K=D, out_dtype=BF16, name="b_kv")
    att, sb_tot = _sb_fwd(q2, kv, name="sb_fwd", **kw)
    hb = _bgate_fwd(att, z2, name="bgate_fwd", **kw)
    y2 = _mm(hb, wbo, mode="nn", M=S, N=D, K=D, out_dtype=F32, name="b_out")
    dr2, dr2b, d_bln_g, d_bln_b, loss = _ln_loss_bwd(x1, y2, target, b_ln_g, b_ln_b, name="ln_b_loss", **kw)
    g_wbo = _mm(hb, dr2b, mode="tn", M=D, N=D, K=S, out_dtype=BF16, name="g_b_out")
    dhb = _mm(dr2b, wbo, mode="nt", M=S, N=D, K=D, out_dtype=F32, name="d_b_out")
    datt, dz2 = _bgate_bwd(dhb, att, z2, name="bgate_bwd", **kw)
    dq2, dk2, dv2 = _sb_bwd(q2, kv, datt, sb_tot, name="sb_bwd", **kw)
    g_wbi = _mm(x1b, dq2, mode="tn", M=D, N=D, K=S, out_dtype=BF16, name="g_b_in_q",
                into=lax.empty((D, 2 * D), BF16))
    g_wbi = _mm(x1b, dz2, mode="tn", M=D, N=D, K=S, out_dtype=BF16, name="g_b_in_z", into=g_wbi, out_off=(0, D))
    g_wkv = _mm(x1b, dk2, mode="tn", M=D, N=D, K=S, out_dtype=BF16, name="g_kv_k",
                into=lax.empty((D, 2 * D), BF16))
    g_wkv = _mm(x1b, dv2, mode="tn", M=D, N=D, K=S, out_dtype=BF16, name="g_kv_v", into=g_wkv, out_off=(0, D))
    dq2 = rs.begin("b", dict(wbo=g_wbo, wbi=g_wbi, wkv=g_wkv), dq2)
    dx1 = _mm(dq2, wbi, mode="nt", M=S, N=D, K=D, out_dtype=F32, name="d_b_in_q", acc_in=dr2, acc_scale=ALPHA)
    dx1 = _mm(dz2, wbi, mode="nt", M=S, N=D, K=D, out_dtype=F32, name="d_b_in_z", b_off=(0, D), acc_in=dx1)
    dx1 = _mm(dk2, wkv, mode="nt", M=S, N=D, K=D, out_dtype=F32, name="d_kv_k", acc_in=dx1)
    dx1 = _mm(dv2, wkv, mode="nt", M=S, N=D, K=D, out_dtype=F32, name="d_kv_v", b_off=(0, D), acc_in=dx1)
    dx1 = rs.pair("b", dx1)
    dr, drb, d_aln_g, d_aln_b = _ln_bwd(x, y, a_ln_g, dx1, name="ln_a_bwd", **kw)
    g_wao = _mm(hg, drb, mode="tn", M=D, N=D, K=S, out_dtype=BF16, name="g_a_out")
    drb = rs.begin("a", dict(wao=g_wao), drb)
    dhg = _mm(drb, wao, mode="nt", M=S, N=D, K=D, out_dtype=F32, name="d_a_out")
    du = lax.empty((S, 4 * D), BF16)
    du, dh, d_head_g = _hgate_bwd(dhg, h, u, head_g, du, name="hgate_bwd", **kw)
    dh = rs.pair("a", dh)
    du, dq, dk, rowsum, colsum = _mlstm_bwd(qk, u, fcol, brow, m, den, dh, h, du, nha=nha, name="mlstm_bwd", **kw)
    dgt, dgtot = _gates_bwd(rowsum.reshape(nha * nb, LANES), colsum.reshape(nha * nb, LANES), gt, gbias,
                            nha=nha, nb=nb, name="gates_bwd")
    d_gate_b = dgtot[::nb, 0].reshape(1, 2 * nha)
    dgp = jnp.pad(dgt.reshape(2 * nha, S).T, ((0, 0), (0, LANES - 2 * nha))).astype(BF16)
    du, d_conv_w, d_conv_b = _conv_bwd(u, dq, dk, conv_w, conv_b, du, name="conv_bwd", **kw)
    small = dict(gate_b=d_gate_b, conv_w=d_conv_w, conv_b=d_conv_b, head_g=d_head_g,
                 a_ln_g=d_aln_g, a_ln_b=d_aln_b, b_ln_g=d_bln_g, b_ln_b=d_bln_b)
    du = rs.chip("a", rs.chip("b", du))
    g_wa = _mm(xb, du, mode="tn", M=D, N=4 * D, K=S, out_dtype=BF16, name="g_a_in")
    g_wg = _mm(xb, dgp, mode="tn", M=D, N=LANES, K=S, out_dtype=BF16, name="g_a_in_gates")
    du = rs.begin("w", dict(wa=g_wa, wg=g_wg), du)
    du = rs.pair("w", rs.small(small, loss, du))
    du = rs.finish("a", rs.finish("b", du))
    dx = _mm(du, wa, mode="nt", M=S, N=D, K=4 * D, out_dtype=F32, name="d_a_in", acc_in=dr, acc_scale=ALPHA,
             b_chunk=wa_chunk)
    dx = _mm(dgp, wg, mode="nt", M=S, N=D, K=LANES, out_dtype=F32, name="d_a_in_gates", acc_in=dx)
    return dx


def _coords():
    return lax.axis_index("x"), lax.axis_index("y"), lax.axis_index("c")


def _other_chips(x, y):
    return [(1 - x, y), (x, 1 - y), (1 - x, 1 - y)]


def _rows(ref, start, size):
    return ref.at[pl.ds(pl.multiple_of(start, size), size), :]


def _window(kind, ref, shard_shape, j, hf=None, quarter=None):
    r, cw = shard_shape
    row0, nr = (0, r) if hf is None else (hf * (r // 2), r // 2)
    if quarter is not None:
        row0, nr = row0 + quarter * (r // 4), r // 4
    if kind == "stack":
        return ref.at[j, pl.ds(pl.multiple_of(row0, nr), nr), :]
    if kind == "rows":
        return ref.at[pl.ds(pl.multiple_of(j * r + row0, nr), nr), :]
    assert kind == "cols"
    return ref.at[pl.ds(pl.multiple_of(row0, nr), nr), pl.ds(pl.multiple_of(j * cw, cw), cw)]


def _gathered_shape(kind, shard_shape):
    r, cw = shard_shape
    return {"stack": (N_CHIPS, r, cw), "rows": (N_CHIPS * r, cw), "cols": (r, N_CHIPS * cw)}[kind]


def _remote(src, dst, send_sems, recv_sems, k, to):
    return pltpu.make_async_remote_copy(src_ref=src, dst_ref=dst, send_sem=send_sems.at[k],
                                        recv_sem=recv_sems.at[k], device_id=to, device_id_type=MESH)


def _allgather_copies(s_refs, sm_ref, g_refs, smg_ref, send_sems, recv_sems, kinds, shapes):
    n = len(s_refs)
    x, y, c = _coords()
    me, sibling, chips = 2 * x + y, (x, y, 1 - c), _other_chips(x, y)
    ids = [2 * cx + cy for cx, cy in chips]
    d2d, own_base, small_base = 4 * n, 7 * n, 8 * n

    def win(a, j, hf=None, quarter=None):
        return _window(kinds[a], g_refs[a], shapes[a], j, hf, quarter)

    def small_win(j):
        w = sm_ref.shape[1]
        return smg_ref.at[:, pl.ds(pl.multiple_of(j * w, w), w)]

    def rc(src, dst, k, to):
        return _remote(src, dst, send_sems, recv_sems, k, to)

    first = [rc(s_refs[a], win(a, me), own_base + a, sibling) for a in range(n)]
    for a in range(n):
        src = _rows(s_refs[a], c * (shapes[a][0] // 2), shapes[a][0] // 2)
        first += [rc(src, win(a, me, c), 4 * a + k, (*chips[k], c)) for k in range(2)]
    if sm_ref is not None:
        first.append(rc(sm_ref, small_win(me), small_base + 3, sibling))
        first += [rc(sm_ref, small_win(me), small_base + k, (*chip, c)) for k, chip in enumerate(chips)]
    for cp in first:
        cp.start()
    passed = []
    for a in range(n):
        for k in range(2):
            rc(win(a, ids[k], c), win(a, ids[k], c), 4 * a + k, sibling).wait_recv()
            part = win(a, ids[k], c, quarter=k)
            passed.append(rc(part, part, 4 * a + 2 + k, (*chips[1 - k], c)))
            passed.append(rc(win(a, ids[k], c), win(a, ids[k], c), d2d + 3 * a + k, sibling))
            passed[-2].start()
            passed[-1].start()
    for a in range(n):
        for k in range(2):
            part = win(a, ids[2], c, quarter=k)
            rc(part, part, 4 * a + 2 + k, sibling).wait_recv()
        passed.append(rc(win(a, ids[2], c), win(a, ids[2], c), d2d + 3 * a + 2, sibling))
        passed[-1].start()
    if sm_ref is not None:
        for k in range(3):
            rc(small_win(ids[k]), small_win(ids[k]), small_base + k, sibling).wait_recv()
        rc(small_win(me), small_win(me), small_base + 3, sibling).wait_recv()
    for a in range(n):
        for k in range(3):
            rc(win(a, ids[k], 1 - c), win(a, ids[k], 1 - c), d2d + 3 * a + k, sibling).wait_recv()
        rc(win(a, me), win(a, me), own_base + a, sibling).wait_recv()
    for cp in first + passed:
        cp.wait_send()


def _peer_handshake(peers):
    barrier = pltpu.get_barrier_semaphore()
    for peer in peers:
        pl.semaphore_signal(barrier, inc=1, device_id=peer, device_id_type=MESH)
    pl.semaphore_wait(barrier, len(peers))


def _behind(copies, inputs, out_structs, nsem, peers, *, name, collective_id):
    hbm = pltpu.MemorySpace.HBM
    in_refs = [jax.new_ref(a, memory_space=hbm) for a in inputs]
    out_refs = [jax.empty_ref(st, memory_space=hbm) for st in out_structs]

    @pl.kernel(mesh=plsc.ScalarSubcoreMesh(axis_name="sequencer", num_cores=1), name=name,
               scratch_types=(pltpu.SemaphoreType.DMA((nsem,)), pltpu.SemaphoreType.DMA((nsem,))),
               compiler_params=pltpu.CompilerParams(collective_id=collective_id))
    def launch(send_sems, recv_sems):
        _peer_handshake(peers(*_coords()))
        copies(in_refs, out_refs, send_sems, recv_sems)

    launch()
    return [r[...] for r in out_refs]


def _sibling(x, y, c):
    return [(x, y, 1 - c)]


def _same_core_of_other_chips(x, y, c):
    return [(cx, cy, c) for cx, cy in _other_chips(x, y)]


def _sibling_and_other_chips(x, y, c):
    return _sibling(x, y, c) + _same_core_of_other_chips(x, y, c)


ID_ALL_PEERS, ID_SIBLING, ID_CHIPS = 0, 1, 2


def _allgather_behind(shards, kinds, small=None, *, name):
    n = len(shards)
    shapes = [sh.shape for sh in shards]

    def copies(in_refs, out_refs, send_sems, recv_sems):
        if small is None:
            _allgather_copies(in_refs, None, out_refs, None, send_sems, recv_sems, kinds, shapes)
        else:
            _allgather_copies(in_refs[:n], in_refs[n], out_refs[:n], out_refs[n], send_sems, recv_sems, kinds, shapes)

    outs = [jax.ShapeDtypeStruct(_gathered_shape(kinds[a], shapes[a]), shards[a].dtype) for a in range(n)]
    inputs = list(shards)
    if small is not None:
        outs.append(jax.ShapeDtypeStruct((small.shape[0], N_CHIPS * small.shape[1]), small.dtype))
        inputs.append(small)
    return _behind(copies, inputs, outs, 8 * n + 4, _sibling_and_other_chips, name=name, collective_id=ID_ALL_PEERS)


def _rs_pair_exchange(views, *, name):
    n = len(views)

    def copies(g_refs, p_refs, send_sems, recv_sems):
        x, y, c = _coords()
        cps = []
        for a in range(n):
            r2 = views[a].shape[1] // 2
            src = g_refs[a].at[:, pl.ds(pl.multiple_of((1 - c) * r2, r2), r2), :]
            cps.append(_remote(src, p_refs[a], send_sems, recv_sems, a, (x, y, 1 - c)))
        for cp in cps:
            cp.start()
        for cp in cps:
            cp.wait()

    outs = [jax.ShapeDtypeStruct((v.shape[0], v.shape[1] // 2, v.shape[2]), v.dtype) for v in views]
    return _behind(copies, views, outs, n, _sibling, name=name, collective_id=ID_SIBLING)


def _add_half(view, part, core, *, name):
    nch, r, cols = view.shape
    r2 = r // 2
    tr = min(128, r2)
    nt = r2 // tr

    def body(c_ref, g_ref, p_ref, o_ref):
        del c_ref
        o_ref[...] = (g_ref[...].astype(F32) + p_ref[...].astype(F32)).astype(o_ref.dtype)

    return pl.pallas_call(
        body, name=name, out_shape=jax.ShapeDtypeStruct((nch, r2, cols), view.dtype),
        grid_spec=pltpu.PrefetchScalarGridSpec(
            num_scalar_prefetch=1, grid=(nch, nt),
            in_specs=[pl.BlockSpec((1, tr, cols), lambda ch, i, c_ref: (ch, c_ref[0] * nt + i, 0)),
                      pl.BlockSpec((1, tr, cols), lambda ch, i, c_ref: (ch, i, 0))],
            out_specs=pl.BlockSpec((1, tr, cols), lambda ch, i, c_ref: (ch, i, 0))),
        compiler_params=_params("parallel", "parallel"),
    )(core, view, part)


def _chunk(kind, ref, j, cw):
    if kind == "cols":
        return ref.at[0, :, pl.ds(pl.multiple_of(j * cw, cw), cw)]
    return ref.at[j]


def _rs_chips(pairs, kinds, *, name):
    n = len(pairs)
    half_shapes = []
    for a in range(n):
        nch, r2, cols = pairs[a].shape
        half_shapes.append((r2, cols // N_CHIPS) if kinds[a] == "cols" else (r2, cols))

    def copies(q_refs, t_refs, send_sems, recv_sems):
        x, y, c = _coords()
        chips = _other_chips(x, y)
        sends = []
        for a in range(n):
            for k, (cx, cy) in enumerate(chips):
                src = _chunk(kinds[a], q_refs[a], 2 * cx + cy, half_shapes[a][1])
                sends.append(_remote(src, t_refs[a].at[k], send_sems, recv_sems, 3 * a + k, (cx, cy, c)))
        for cp in sends:
            cp.start()
        for a in range(n):
            for k in range(3):
                slot = t_refs[a].at[k]
                _remote(slot, slot, send_sems, recv_sems, 3 * a + k, (x, y, c)).wait_recv()
        for cp in sends:
            cp.wait_send()

    outs = [jax.ShapeDtypeStruct((3, *half_shapes[a]), pairs[a].dtype) for a in range(n)]
    return _behind(copies, pairs, outs, 3 * n, _same_core_of_other_chips, name=name, collective_id=ID_CHIPS)


def _sum_chips(pair, parts, chip, kind, *, name):
    _, r, cols = parts.shape
    tr = min(128, r)
    if kind == "cols":
        own_spec = pl.BlockSpec((1, tr, cols), lambda i, chip_ref: (0, i, chip_ref[0]))
    else:
        own_spec = pl.BlockSpec((1, tr, cols), lambda i, chip_ref: (chip_ref[0], i, 0))

    def body(chip_ref, q_ref, p_ref, o_ref):
        del chip_ref
        acc = q_ref[0].astype(F32)
        for s in range(3):
            acc = acc + p_ref[s].astype(F32)
        o_ref[...] = acc

    return pl.pallas_call(
        body, name=name, out_shape=jax.ShapeDtypeStruct((r, cols), F32),
        grid_spec=pltpu.PrefetchScalarGridSpec(
            num_scalar_prefetch=1, grid=(r // tr,),
            in_specs=[own_spec, pl.BlockSpec((3, tr, cols), lambda i, chip_ref: (0, i, 0))],
            out_specs=pl.BlockSpec((tr, cols), lambda i, chip_ref: (i, 0))),
        compiler_params=_params("parallel"),
    )(chip, pair, parts)


def _rs_share(halves, *, name):
    n = len(halves)

    def copies(h_refs, g_refs, send_sems, recv_sems):
        x, y, c = _coords()
        cps = [_remote(h_refs[a], g_refs[a], send_sems, recv_sems, a, (x, y, 1 - c)) for a in range(n)]
        for cp in cps:
            cp.start()
        for cp in cps:
            cp.wait()

    outs = [jax.ShapeDtypeStruct(h.shape, h.dtype) for h in halves]
    return _behind(copies, halves, outs, n, _sibling, name=name, collective_id=ID_SIBLING)


class _GradReducer:
    def __init__(self, w, m, v, long_name, core, chip_ix, as_views, small_step):
        self.w, self.m, self.v, self.long_name = w, m, v, long_name
        self.core, self.chip_ix, self.as_views, self.small_step = core, chip_ix, as_views, small_step
        self.groups, self.results, self.small_results = {}, {}, None

    def small(self, small, loss_row, tie):
        (small, loss_row), tie = lax.optimization_barrier(((small, loss_row), tie))
        self.small_results, tie = lax.optimization_barrier((self.small_step(small, loss_row), tie))
        return tie

    def begin(self, tag, grads, tie):
        keys, views, kinds = self.as_views(grads)
        views, tie = lax.optimization_barrier((views, tie))
        parts = _rs_pair_exchange(views, name="rs_pair_exchange_" + tag)
        self.groups[tag] = dict(keys=keys, views=views, kinds=kinds, parts=parts)
        return tie

    def pair(self, tag, tie):
        g = self.groups[tag]
        g["parts"], tie = lax.optimization_barrier((g["parts"], tie))
        pairs = [_add_half(v, p, self.core, name="pair_sum_" + self.long_name[k])
                 for v, p, k in zip(g["views"], g["parts"], g["keys"])]
        pairs, tie = lax.optimization_barrier((pairs, tie))
        g["pairs"] = pairs
        g["slots"] = _rs_chips(pairs, g["kinds"], name="rs_chips_" + tag)
        return tie

    def chip(self, tag, tie):
        g = self.groups[tag]
        g["slots"], tie = lax.optimization_barrier((g["slots"], tie))
        halves = [_sum_chips(q, t, self.chip_ix, kd, name="chip_sum_" + self.long_name[k])
                  for q, t, kd, k in zip(g["pairs"], g["slots"], g["kinds"], g["keys"])]
        halves, tie = lax.optimization_barrier((halves, tie))
        g["halves"] = halves
        g["others"] = _rs_share(halves, name="rs_share_" + tag)
        return tie

    def finish(self, tag, tie):
        g = self.groups[tag]
        g["others"], tie = lax.optimization_barrier((g["others"], tie))
        out = [_adamw_halves(self.w[k], a, b, self.m[k], self.v[k], self.core, name="adamw_" + self.long_name[k])
               for k, a, b in zip(g["keys"], g["halves"], g["others"])]
        out, tie = lax.optimization_barrier((out, tie))
        self.results.update(zip(g["keys"], out))
        return tie


def _adamw_halves(w, mine, theirs, m, v, core, *, name, tr=128):
    rows, cols = w.shape
    hr = rows // 2
    tr = min(tr, hr)
    nt = hr // tr

    def body(c_ref, w_ref, a_ref, b_ref, m_ref, v_ref, g_ref, d_ref, mo_ref, vo_ref):
        gg = jnp.where(pl.program_id(0) == c_ref[0], a_ref[...], b_ref[...])
        g_ref[...] = gg
        d, mn, vn = _adamw_math(w_ref[...], gg, m_ref[...], v_ref[...])
        d_ref[...] = d
        mo_ref[...] = mn
        vo_ref[...] = vn

    full = pl.BlockSpec((tr, cols), lambda hf, i, c_ref: (hf * nt + i, 0))
    half = pl.BlockSpec((tr, cols), lambda hf, i, c_ref: (i, 0))
    sd = jax.ShapeDtypeStruct((rows, cols), F32)
    return pl.pallas_call(
        body, name=name, out_shape=(sd, sd, sd, sd),
        grid_spec=pltpu.PrefetchScalarGridSpec(
            num_scalar_prefetch=1, grid=(2, nt),
            in_specs=[full, half, half, full, full], out_specs=(full, full, full, full)),
        compiler_params=_params("parallel", "parallel"),
    )(core, w, mine, theirs, m, v)


def _allreduce_small(vec):
    m_per, ncol = vec.shape
    n_dev = 2 * N_CHIPS

    def body(x_ref, out_ref, sum_ref, send_sems, recv_sems, local_sem):
        x, y, c = _coords()
        me, sibling = (x, y, c), (x, y, 1 - c)
        chips = _other_chips(x, y)

        def rows(px, py, pc):
            return out_ref.at[pl.ds(pl.multiple_of((4 * px + 2 * py + pc) * m_per, m_per), m_per), :]

        def copy(k, block, to, src=None):
            return _remote(rows(*block) if src is None else src, rows(*block), send_sems, recv_sems, k, to)

        mine = pltpu.make_async_copy(x_ref, rows(*me), local_sem)
        mine.start()
        first = [copy(0, me, sibling, src=x_ref)]
        first += [copy(1 + j, me, (*chip, c), src=x_ref) for j, chip in enumerate(chips)]
        for cp in first:
            cp.start()
        passed = [copy(4 + j, (*chip, c), sibling) for j, chip in enumerate(chips)]
        for j, chip in enumerate(chips):
            copy(1 + j, (*chip, c), me).wait_recv()
            passed[j].start()
        copy(0, sibling, me).wait_recv()
        for j, chip in enumerate(chips):
            copy(4 + j, (*chip, 1 - c), me).wait_recv()
        for cp in first + passed:
            cp.wait_send()
        mine.wait()
        acc = out_ref[0:m_per, :]
        for d in range(1, n_dev):
            acc = acc + out_ref[d * m_per:(d + 1) * m_per, :]
        sum_ref[...] = acc

    vm = pl.BlockSpec(memory_space=pltpu.VMEM)
    return pl.pallas_call(
        body, name="allreduce_small",
        out_shape=(jax.ShapeDtypeStruct((n_dev * m_per, ncol), vec.dtype), jax.ShapeDtypeStruct((m_per, ncol), vec.dtype)),
        in_specs=[vm], out_specs=(vm, vm),
        scratch_shapes=[pltpu.SemaphoreType.DMA((7,)), pltpu.SemaphoreType.DMA((7,)), pltpu.SemaphoreType.DMA],
    )(vec)[1]


def _pack_rows(pieces, total_rows):
    rows = []
    for p in pieces:
        flat = p.reshape(-1)
        flat = jnp.pad(flat, (0, (-flat.shape[0]) % LANES))
        rows.append(flat.reshape(-1, LANES))
    out = jnp.concatenate(rows, axis=0)
    return jnp.pad(out, ((0, total_rows - out.shape[0]), (0, 0)))


def _unpack_rows(packed, shapes):
    out, r = [], 0
    for shp in shapes:
        size = 1
        for d in shp:
            size *= d
        nr = -(-size // LANES)
        out.append(packed[r:r + nr].reshape(-1)[:size].reshape(shp))
        r += nr
    return out


def _round_up(v, m):
    return -(-v // m) * m


def kernel(x, a_w_in, a_gate_b, a_conv_w, a_conv_b, a_head_g, a_w_out, a_ln_g, a_ln_b, kv_w, b_w_in, b_w_out, b_ln_g, b_ln_b, loss_target, m_a_w_in, m_a_gate_b, m_a_conv_w, m_a_conv_b, m_a_head_g, m_a_w_out, m_a_ln_g, m_a_ln_b, m_kv_w, m_b_w_in, m_b_w_out, m_b_ln_g, m_b_ln_b, v_a_w_in, v_a_gate_b, v_a_conv_w, v_a_conv_b, v_a_head_g, v_a_w_out, v_a_ln_g, v_a_ln_b, v_kv_w, v_b_w_in, v_b_w_out, v_b_ln_g, v_b_ln_b):
    _, S, D = x.shape
    nha = a_gate_b.shape[1] // 2
    chip = 2 * lax.axis_index("x") + lax.axis_index("y")
    core = lax.axis_index("c").astype(jnp.int32).reshape(1)
    dq = D // N_CHIPS

    ca = a_w_in.shape[2]
    edge = (N_CHIPS - 1) * (ca - D)
    wide = _round_up(ca + edge, LANES)
    shifted = lax.dynamic_slice_in_dim(
        jnp.pad(a_w_in[0].astype(BF16), ((0, 0), (edge, wide - ca))), edge - chip * (ca - D), wide, axis=1)
    shards = [shifted, a_w_out[0].astype(BF16), kv_w.astype(BF16), b_w_in[0].astype(BF16), b_w_out[0].astype(BF16)]
    kinds = ["stack", "rows", "cols", "cols", "rows"]
    small_shard = jnp.concatenate([a_conv_w[0], a_conv_b, a_head_g, a_ln_g, a_ln_b], axis=0)
    wa_g, small_full = _allgather_behind(shards[:1], kinds[:1], small_shard, name="allgather_first")
    wao, = _allgather_behind(shards[1:2], kinds[1:2], name="allgather_a_w_out")
    wkv, wbi = _allgather_behind(shards[2:4], kinds[2:4], name="allgather_b_in")
    wbo, = _allgather_behind(shards[4:], kinds[4:], name="allgather_b_w_out")
    wa = _join_chunk_edges(wa_g, D)
    wg = wa[N_CHIPS - 1, :, D:D + LANES]
    conv_w, conv_b, head_g, ln_g_a, ln_b_a = (small_full[0:CONV_A], small_full[4:5], small_full[5:6],
                                              small_full[6:7], small_full[7:8])

    chip_ix = chip.astype(jnp.int32).reshape(1)
    (w_wa, m_wa, v_wa), x = lax.optimization_barrier(((a_w_in[0], m_a_w_in[0], v_a_w_in[0]), x))
    w_big = dict(wa=w_wa, wao=a_w_out[0], wkv=kv_w, wbi=b_w_in[0], wbo=b_w_out[0])
    m_big = dict(wa=m_wa, wao=m_a_w_out[0], wkv=m_kv_w, wbi=m_b_w_in[0], wbo=m_b_w_out[0])
    v_big = dict(wa=v_wa, wao=v_a_w_out[0], wkv=v_kv_w, wbi=v_b_w_in[0], wbo=v_b_w_out[0])
    long_name = dict(wa="a_w_in", wao="a_w_out", wkv="kv_w", wbi="b_w_in", wbo="b_w_out")

    def as_views(g):
        if "wa" in g:
            g_full = jnp.concatenate([g["wa"], g["wg"][:, :2 * nha]], axis=1)
            ca = g_full.shape[1] // N_CHIPS
            return ["wa"], [jnp.stack([g_full[:, j * ca:(j + 1) * ca] for j in range(N_CHIPS)], axis=0)], ["stack"]
        keys = list(g)
        views = [g[k].reshape(N_CHIPS, dq, D) if k in ("wao", "wbo") else g[k][None] for k in keys]
        return keys, views, ["stack" if k in ("wao", "wbo") else "cols" for k in keys]

    w_small = [a_gate_b, a_conv_w[0], a_conv_b, a_head_g, a_ln_g, a_ln_b, b_ln_g, b_ln_b]
    m_small = [m_a_gate_b, m_a_conv_w[0], m_a_conv_b, m_a_head_g, m_a_ln_g, m_a_ln_b, m_b_ln_g, m_b_ln_b]
    v_small = [v_a_gate_b, v_a_conv_w[0], v_a_conv_b, v_a_head_g, v_a_ln_g, v_a_ln_b, v_b_ln_g, v_b_ln_b]

    def small_step(small, loss_row):
        order = ["conv_w", "conv_b", "head_g", "a_ln_g", "a_ln_b", "b_ln_g", "b_ln_b", "gate_b"]
        full_shapes = [(CONV_A, D), (1, D), (1, D), (1, D), (1, D), (1, D), (1, D), (1, 2 * nha)]
        n_rows = sum(-(-(s[0] * s[1]) // LANES) for s in full_shapes) + 1
        total = _allreduce_small(_pack_rows([small[k] for k in order] + [loss_row], _round_up(n_rows, 8)))
        sums = dict(zip(order, _unpack_rows(total, full_shapes)))

        def mine(v):
            return lax.dynamic_slice_in_dim(v, chip * dq, dq, axis=1)

        g_small = [sums["gate_b"], mine(sums["conv_w"]), mine(sums["conv_b"]), mine(sums["head_g"]),
                   mine(sums["a_ln_g"]), mine(sums["a_ln_b"]), sums["b_ln_g"], sums["b_ln_b"]]
        rows_small = _round_up(sum(-(-(w.shape[0] * w.shape[1]) // LANES) for w in w_small), 8)
        upd_small = _adamw(_pack_rows(w_small, rows_small), _pack_rows(g_small, rows_small),
                           _pack_rows(m_small, rows_small), _pack_rows(v_small, rows_small), name="adamw_small")
        return total[n_rows - 1, 0], g_small, upd_small

    rs = _GradReducer(w_big, m_big, v_big, long_name, core, chip_ix, as_views, small_step)
    grad_x = _local_step(
        x[0], loss_target[0], wa, wg, wao, wbi, wkv, wbo, a_gate_b, conv_w, conv_b, head_g,
        ln_g_a, ln_b_a, b_ln_g, b_ln_b, S=S, D=D, nha=nha, rs=rs)
    grad_x = rs.finish("w", rs.chip("w", grad_x))
    upd_big = [rs.results[k] for k in ("wa", "wao", "wkv", "wbi", "wbo")]
    g_big = [u[0] for u in upd_big]
    loss, g_small, upd_small = rs.small_results
    d_small, mn_small, vn_small = (_unpack_rows(u, [w.shape for w in w_small]) for u in upd_small)

    def assemble(big5, small8):
        awi, awo, kvw, bwi, bwo = big5
        gb, cw, cb, hg, alg, alb, blg, blb = small8
        return [awi[None], gb, cw[None], cb, hg, awo[None], alg, alb, kvw, bwi[None], bwo[None], blg, blb]

    grads = assemble(g_big, g_small)
    deltas = assemble([u[1] for u in upd_big], d_small)
    new_m = assemble([u[2] for u in upd_big], mn_small)
    new_v = assemble([u[3] for u in upd_big], vn_small)
    return (loss, grad_x[None], *grads, *deltas, *new_m, *new_v)
```

```python
import functools

import jax
import jax.numpy as jnp
from jax import lax
from jax.experimental import pallas as pl
from jax.experimental.pallas import tpu as pltpu
from jax.experimental.pallas import tpu_sc as plsc

F32 = jnp.float32
BF16 = jnp.bfloat16

DEPTH = 2
ALPHA = (2.0 * DEPTH) ** 0.25
LN_EPS = 1e-5
DK_A = 128
DV_A = 256
DH_B = 128
SB_TILE = 512
ML_TQ = 512
CONV_A = 4
ADAM_LR = 0.001
ADAM_B1 = 0.9
ADAM_B2 = 0.999
ADAM_EPS = 1e-08
ADAM_WD = 0.01
ADAM_STEP = 10
N_CHIPS = 4
LANES = 128
MXU_DEPTH = 256
V7X_VMEM_BYTES = 64 * 1024 * 1024
VMEM_LIMIT = (V7X_VMEM_BYTES * 3) // 4
NEG_BIG = -1e30
MESH = pl.DeviceIdType.MESH
ANY = pl.BlockSpec(memory_space=pl.ANY)


def _params(*sem):
    return pltpu.CompilerParams(dimension_semantics=sem, vmem_limit_bytes=VMEM_LIMIT)


def _dot(a, b, dims):
    return lax.dot_general(a, b, (dims, ((), ())), preferred_element_type=F32)


def _dot_nn(a, b):
    return _dot(a, b, ((1,), (0,)))


def _dot_nt(a, b):
    return _dot(a, b, ((1,), (1,)))


def _dot_tn(a, b):
    return _dot(a, b, ((0,), (0,)))


def _split2(x):
    hi = x.astype(BF16)
    lo = (x - hi.astype(F32)).astype(BF16)
    return hi, lo


def _split3(x):
    hi = x.astype(BF16)
    r = x - hi.astype(F32)
    mid = r.astype(BF16)
    lo = (r - mid.astype(F32)).astype(BF16)
    return hi, mid, lo


def _mask_dot2(x, t01):
    hi, lo = _split2(x)
    return _dot_nn(hi, t01) + _dot_nn(lo, t01)


def _block_sums(x, tri, later):
    sub = tri.shape[0]
    n = x.shape[1] // sub
    if n == 1:
        return _mask_dot2(x, tri)
    parts = [x[:, b * sub:(b + 1) * sub] for b in range(n)]
    sums = [jnp.sum(p, axis=1, keepdims=True) for p in parts]
    out = []
    for b in range(n):
        acc = _mask_dot2(parts[b], tri)
        for o in (range(b + 1, n) if later else range(b)):
            acc = acc + sums[o]
        out.append(acc)
    return jnp.concatenate(out, axis=1)


def _mask_dot3(x, t01):
    hi, mid, lo = _split3(x)
    return _dot_nn(hi, t01) + _dot_nn(mid, t01) + _dot_nn(lo, t01)


def _mask_dot3_left(t01, x):
    hi, mid, lo = _split3(x)
    return _dot_nn(t01, hi) + _dot_nn(t01, mid) + _dot_nn(t01, lo)


def _log_sigmoid(z):
    return jnp.minimum(z, 0.0) - jnp.log(1.0 + jnp.exp(-jnp.abs(z)))


def _sigmoid(z):
    return 1.0 / (1.0 + jnp.exp(-z))


def _mm(a, b, *, mode, M, N, K, out_dtype, name, tm=1024, tn=1024, tk=2048,
        a_off=(0, 0), b_off=(0, 0), acc_in=None, acc_scale=1.0, into=None, out_off=(0, 0), b_chunk=None):
    tm, tn, tk = min(tm, M), min(tn, N), min(tk, K)
    if b_chunk is not None:
        tn, tk = (min(tn, b_chunk), tk) if mode == "nn" else (tn, min(tk, b_chunk))
        assert b.ndim == 3 and b_off == (0, 0) and mode in ("nn", "nt")
    assert M % tm == 0 and N % tn == 0 and K % tk == 0
    nk = K // tk
    ar, ac = a_off
    br, bc = b_off
    orow, ocol = out_off
    if mode in ("nn", "nt"):
        assert ar % tm == 0 and ac % tk == 0
        a_spec = pl.BlockSpec((tm, tk), lambda i, j, k: (i + ar // tm, k + ac // tk))
    else:
        assert ar % tk == 0 and ac % tm == 0
        a_spec = pl.BlockSpec((tk, tm), lambda i, j, k: (k + ar // tk, i + ac // tm))
    if b_chunk is not None and mode == "nn":
        per = b_chunk // tn
        b_spec = pl.BlockSpec((None, tk, tn), lambda i, j, k: (j // per, k, j % per))
    elif b_chunk is not None:
        per = b_chunk // tk
        b_spec = pl.BlockSpec((None, tn, tk), lambda i, j, k: (k // per, j, k % per))
    elif mode == "nt":
        assert br % tn == 0 and bc % tk == 0
        b_spec = pl.BlockSpec((tn, tk), lambda i, j, k: (j + br // tn, k + bc // tk))
    else:
        assert br % tk == 0 and bc % tn == 0
        b_spec = pl.BlockSpec((tk, tn), lambda i, j, k: (k + br // tk, j + bc // tn))
    assert orow % tm == 0 and ocol % tn == 0
    o_spec = pl.BlockSpec((tm, tn), lambda i, j, k: (i + orow // tm, j + ocol // tn))
    dims = {"nn": ((1,), (0,)), "nt": ((1,), (1,)), "tn": ((0,), (0,))}[mode]
    inputs, in_specs = [a, b], [a_spec, b_spec]
    has_acc = acc_in is not None
    if has_acc:
        inputs.append(acc_in)
        in_specs.append(pl.BlockSpec((tm, tn), lambda i, j, k: (i, j)))
    aliases = {}
    if into is not None:
        inputs.append(into)
        in_specs.append(ANY)
        aliases = {len(inputs) - 1: 0}
        out_shape = jax.ShapeDtypeStruct(into.shape, into.dtype)
        assert into.dtype == out_dtype
    else:
        out_shape = jax.ShapeDtypeStruct((M, N), out_dtype)

    def body(*refs):
        a_ref, b_ref = refs[0], refs[1]
        acc_in_ref = refs[2] if has_acc else None
        n_in = len(inputs)
        o_ref = refs[n_in]

        def first():
            if has_acc:
                return acc_scale * acc_in_ref[...]
            return None

        if nk == 1:
            r = _dot(a_ref[...], b_ref[...], dims)
            f = first()
            if f is not None:
                r = r + f
            o_ref[...] = r.astype(o_ref.dtype)
        else:
            acc_ref = refs[n_in + 1]
            kk = pl.program_id(2)

            @pl.when(kk == 0)
            def _():
                f = first()
                acc_ref[...] = jnp.zeros_like(acc_ref) if f is None else f

            acc_ref[...] += _dot(a_ref[...], b_ref[...], dims)

            @pl.when(kk == nk - 1)
            def _():
                o_ref[...] = acc_ref[...].astype(o_ref.dtype)

    scratch = [] if nk == 1 else [pltpu.VMEM((tm, tn), F32)]
    return pl.pallas_call(
        body, name=name, out_shape=out_shape, grid=(M // tm, N // tn, nk),
        in_specs=in_specs, out_specs=o_spec, scratch_shapes=scratch,
        input_output_aliases=aliases,
        compiler_params=_params("parallel", "parallel", "arbitrary"),
    )(*inputs)


def _join_chunk_edges(w, d):
    nch, rows, _ = w.shape

    def body(w_ref, o_ref, first, tail, sems):
        del w_ref
        for j in range(1, nch):
            mine = o_ref.at[j, :, pl.ds(0, LANES)]
            loads = [pltpu.make_async_copy(mine, first, sems.at[0]),
                     pltpu.make_async_copy(o_ref.at[j - 1, :, pl.ds(d, LANES)], tail, sems.at[1])]
            for cp in loads:
                cp.start()
            for cp in loads:
                cp.wait()
            first[...] = first[...] + tail[...]
            store = pltpu.make_async_copy(first, mine, sems.at[0])
            store.start()
            store.wait()

    return pl.pallas_call(
        body, name="join_chunk_edges", out_shape=jax.ShapeDtypeStruct(w.shape, w.dtype),
        in_specs=[ANY], out_specs=ANY, input_output_aliases={0: 0},
        scratch_shapes=[pltpu.VMEM((rows, LANES), w.dtype), pltpu.VMEM((rows, LANES), w.dtype),
                        pltpu.SemaphoreType.DMA((2,))],
    )(w)


def _shift_down(x, d, row):
    if d == 0:
        return x
    return jnp.where(row >= d, pltpu.roll(x, d, 0), 0.0)


def _shift_up(x, d, row, n):
    if d == 0:
        return x
    return jnp.where(row < n - d, pltpu.roll(x, n - d, 0), 0.0)


def _conv_pre(x, w_ref, b_ref, row):
    c = b_ref[...] + w_ref[CONV_A - 1:CONV_A, :] * x
    for k in range(CONV_A - 1):
        c = c + w_ref[k:k + 1, :] * _shift_down(x, CONV_A - 1 - k, row)
    return c


def _conv_fwd(u, conv_w, conv_b, *, S, D, name):
    tc = 256
    nq_blocks = (D // 2) // tc

    def body(u_ref, w_ref, b_ref, o_ref):
        x = u_ref[...]
        row = lax.broadcasted_iota(jnp.int32, x.shape, 0)
        c = _conv_pre(x, w_ref, b_ref, row)
        scale = jnp.where(pl.program_id(0) >= nq_blocks, DK_A ** -0.5, 1.0).astype(F32)
        o_ref[...] = (c * _sigmoid(c) * scale).astype(BF16)

    return pl.pallas_call(
        body, name=name, out_shape=jax.ShapeDtypeStruct((S, D), BF16), grid=(D // tc,),
        in_specs=[pl.BlockSpec((S, tc), lambda j: (0, j)),
                  pl.BlockSpec((CONV_A, tc), lambda j: (0, j)),
                  pl.BlockSpec((1, tc), lambda j: (0, j))],
        out_specs=pl.BlockSpec((S, tc), lambda j: (0, j)),
        compiler_params=_params("parallel"),
    )(u, conv_w, conv_b)


def _conv_bwd(u, dq, dk, conv_w, conv_b, du, *, S, D, name):
    tc = 256
    nq_blocks = (D // 2) // tc

    def body(u_ref, dq_ref, dk_ref, w_ref, b_ref, du_in, du_ref, dw_ref, db_ref):
        del du_in
        x = u_ref[...]
        n = x.shape[0]
        row = lax.broadcasted_iota(jnp.int32, x.shape, 0)
        c = _conv_pre(x, w_ref, b_ref, row)
        is_k = pl.program_id(0) >= nq_blocks
        dy = jnp.where(is_k, dk_ref[...] * (DK_A ** -0.5), dq_ref[...])
        sg = _sigmoid(c)
        dc = dy * (sg * (1.0 + c * (1.0 - sg)))
        db_ref[...] = jnp.sum(dc, axis=0, keepdims=True)
        dx = w_ref[CONV_A - 1:CONV_A, :] * dc
        dw_ref[CONV_A - 1:CONV_A, :] = jnp.sum(dc * x, axis=0, keepdims=True)
        for k in range(CONV_A - 1):
            d = CONV_A - 1 - k
            dw_ref[k:k + 1, :] = jnp.sum(dc * _shift_down(x, d, row), axis=0, keepdims=True)
            dx = dx + w_ref[k:k + 1, :] * _shift_up(dc, d, row, n)
        du_ref[...] = dx.astype(BF16)

    half = lambda j: (0, j % nq_blocks)
    return pl.pallas_call(
        body, name=name,
        out_shape=(jax.ShapeDtypeStruct(du.shape, du.dtype),
                   jax.ShapeDtypeStruct((CONV_A, D), F32), jax.ShapeDtypeStruct((1, D), F32)),
        grid=(D // tc,),
        in_specs=[pl.BlockSpec((S, tc), lambda j: (0, j)),
                  pl.BlockSpec((S, tc), half), pl.BlockSpec((S, tc), half),
                  pl.BlockSpec((CONV_A, tc), lambda j: (0, j)),
                  pl.BlockSpec((1, tc), lambda j: (0, j)), ANY],
        out_specs=(pl.BlockSpec((S, tc), lambda j: (0, j)),
                   pl.BlockSpec((CONV_A, tc), lambda j: (0, j)),
                   pl.BlockSpec((1, tc), lambda j: (0, j))),
        input_output_aliases={5: 0},
        compiler_params=_params("parallel"),
    )(u, dq, dk, conv_w, conv_b, du)


def _tri(n, cmp):
    r = lax.broadcasted_iota(jnp.int32, (n, n), 0)
    c = lax.broadcasted_iota(jnp.int32, (n, n), 1)
    return r, c, cmp(r, c)


def _gates_fwd(gt, bias, *, nha, nb, name):
    half = nha * nb
    nb_shift = nb.bit_length() - 1
    assert nb == 1 << nb_shift

    def body(g_ref, b_ref, f_ref, brow_ref):
        ig = g_ref[0:half, :] + b_ref[0:half, :]
        fg = g_ref[half:2 * half, :] + b_ref[half:2 * half, :]
        lf = _log_sigmoid(fg)
        _, _, upper = _tri(LANES, lambda r, c: r <= c)
        cs = _mask_dot3(lf, upper.astype(BF16))
        tot = jnp.broadcast_to(cs[:, LANES - 1:LANES], cs.shape)
        r, c, _ = _tri(half, lambda r, c: r <= c)
        before = jnp.logical_and(r >> nb_shift == c >> nb_shift, c < r).astype(BF16)
        f = cs + _mask_dot3_left(before, tot)
        f_ref[...] = f
        brow_ref[...] = ig - f

    return pl.pallas_call(
        body, name=name,
        out_shape=(jax.ShapeDtypeStruct((half, LANES), F32), jax.ShapeDtypeStruct((half, LANES), F32)),
    )(gt, bias)


def _gates_bwd(rowsum, colsum, gt, bias, *, nha, nb, name):
    half = nha * nb
    nb_shift = nb.bit_length() - 1
    assert nb == 1 << nb_shift

    def body(rs_ref, cs_ref, g_ref, b_ref, dg_ref, tot_ref):
        col = cs_ref[...]
        df = rs_ref[...] - col
        _, _, lower = _tri(LANES, lambda r, c: r >= c)
        rc = _mask_dot3(df, lower.astype(BF16))
        tot = jnp.broadcast_to(rc[:, 0:1], rc.shape)
        r, c, _ = _tri(half, lambda r, c: r <= c)
        same = r >> nb_shift == c >> nb_shift
        after = jnp.logical_and(same, c > r).astype(BF16)
        dlf = rc + _mask_dot3_left(after, tot)
        fg = g_ref[half:2 * half, :] + b_ref[half:2 * half, :]
        dfg = dlf * _sigmoid(-fg)
        dg_ref[0:half, :] = col
        dg_ref[half:2 * half, :] = dfg
        grp = same.astype(BF16)
        ones = jnp.ones((LANES, LANES), BF16)
        tot_ref[0:half, :] = _mask_dot3_left(grp, _mask_dot3(col, ones))
        tot_ref[half:2 * half, :] = _mask_dot3_left(grp, _mask_dot3(dfg, ones))

    return pl.pallas_call(
        body, name=name,
        out_shape=(jax.ShapeDtypeStruct((2 * half, LANES), F32), jax.ShapeDtypeStruct((2 * half, LANES), F32)),
    )(rowsum, colsum, gt, bias)


def _mlstm_tile(q, k_ref, fcol, brow_ref, j, tq, diagonal):
    off = pl.multiple_of(j * tq, tq)
    kj = k_ref[pl.ds(off, tq), :]
    s = _dot_nt(q, kj)
    logd = fcol + brow_ref[0, :, pl.ds(off, tq)]
    if diagonal:
        valid = lax.broadcasted_iota(jnp.int32, (tq, tq), 1) <= lax.broadcasted_iota(jnp.int32, (tq, tq), 0)
        logd = jnp.where(valid, logd, NEG_BIG)
    return off, kj, s, logd


def _mlstm_fwd(qk, u, fcol, brow, head_g, *, S, D, nha, name):
    tq = min(ML_TQ, S)
    nq = S // tq
    kb, vb = (D // 2) // DK_A, D // DV_A

    def body(q_ref, k_ref, v_ref, fcol_ref, brow_ref, o_ref, z_ref, g_ref, h_ref, hg_ref, m_ref, den_ref):
        i = pl.program_id(1)
        q = q_ref[...]
        fc = fcol_ref[0]

        def step(j, carry, diagonal):
            acc, den, m = carry
            off, _, s, logd = _mlstm_tile(q, k_ref, fc, brow_ref, j, tq, diagonal)
            m_new = jnp.maximum(m, jnp.max(logd, axis=1, keepdims=True))
            a = s * jnp.exp(logd - m_new)
            alpha = jnp.exp(m - m_new)
            vj = v_ref[pl.ds(off, tq), :].astype(BF16)
            acc = alpha * acc + _dot_nn(a.astype(BF16), vj)
            den = alpha * den + jnp.sum(a, axis=1, keepdims=True)
            return acc, den, m_new

        first = step(i, (jnp.zeros((tq, DV_A), F32), jnp.zeros((tq, 1), F32), jnp.full((tq, 1), NEG_BIG, F32)), True)
        acc, den, m = lax.fori_loop(0, i, lambda j, c: step(j, c, False), first)
        hh = acc / jnp.maximum(jnp.abs(den), jnp.exp(-m))
        h_ref[...] = hh
        hn, _ = _head_norm(hh)
        z = z_ref[...]
        hg_ref[...] = (_sigmoid(o_ref[...]) * (hn * g_ref[...]) * (z * _sigmoid(z))).astype(BF16)
        m_ref[0] = m
        den_ref[0] = den

    stat = pl.BlockSpec((1, tq, 1), lambda h, i: (h, i, 0))
    hblk = pl.BlockSpec((tq, DV_A), lambda h, i: (i, h))
    return pl.pallas_call(
        body, name=name,
        out_shape=(jax.ShapeDtypeStruct((S, D), F32), jax.ShapeDtypeStruct((S, D), BF16),
                   jax.ShapeDtypeStruct((nha, S, 1), F32), jax.ShapeDtypeStruct((nha, S, 1), F32)),
        grid=(nha, nq),
        in_specs=[pl.BlockSpec((tq, DK_A), lambda h, i: (i, h)),
                  pl.BlockSpec((S, DK_A), lambda h, i: (0, kb + h)),
                  pl.BlockSpec((S, DV_A), lambda h, i: (0, vb + h)),
                  stat, pl.BlockSpec((1, 1, S), lambda h, i: (h, 0, 0)),
                  pl.BlockSpec((tq, DV_A), lambda h, i: (i, 2 * vb + h)),
                  pl.BlockSpec((tq, DV_A), lambda h, i: (i, 3 * vb + h)),
                  pl.BlockSpec((1, DV_A), lambda h, i: (0, h))],
        out_specs=(hblk, hblk, stat, stat),
        compiler_params=_params("parallel", "arbitrary"),
    )(qk, qk, u, fcol, brow, u, u, head_g)


def _mlstm_bwd(qk, u, fcol, brow, m, den, dh, h, du, *, S, D, nha, name):
    tq = min(ML_TQ, S)
    nq = S // tq
    kb, vb = (D // 2) // DK_A, D // DV_A

    def body(q_ref, k_ref, v_ref, fcol_ref, brow_ref, m_ref, den_ref, dh_ref, h_ref, du_in,
             du_ref, dq_ref, dk_ref, rs_ref, cs_ref, dv_acc):
        del du_in
        i = pl.program_id(1)

        @pl.when(i == 0)
        def _():
            dk_ref[...] = jnp.zeros_like(dk_ref)
            cs_ref[...] = jnp.zeros_like(cs_ref)
            dv_acc[...] = jnp.zeros_like(dv_acc)

        q = q_ref[...]
        fc = fcol_ref[0]
        mm = m_ref[0]
        dn = den_ref[0]
        floor = jnp.exp(-mm)
        nrm = jnp.maximum(jnp.abs(dn), floor)
        dhv = dh_ref[...]
        dnum = dhv / nrm
        dnrm = -jnp.sum(dhv * h_ref[...], axis=1, keepdims=True) / nrm
        dden = jnp.where(jnp.abs(dn) > floor, jnp.where(dn > 0.0, dnrm, -dnrm), 0.0)
        dnum_b = dnum.astype(BF16)

        def step(j, carry, diagonal):
            dq, rs = carry
            off, kj, s, logd = _mlstm_tile(q, k_ref, fc, brow_ref, j, tq, diagonal)
            p = jnp.exp(logd - mm)
            a = s * p
            vj = v_ref[pl.ds(off, tq), :].astype(BF16)
            da = _dot_nt(dnum_b, vj) + dden
            dv_acc[pl.ds(off, tq), :] += _dot_tn(a.astype(BF16), dnum_b)
            dqk = (da * p).astype(BF16)
            dq = dq + _dot_nn(dqk, kj)
            dk_ref[pl.ds(off, tq), :] += _dot_tn(dqk, q)
            pm = da * a
            cs_ref[0, :, pl.ds(off, tq)] += jnp.sum(pm, axis=0, keepdims=True)
            rs = rs + jnp.sum(pm, axis=1, keepdims=True)
            return dq, rs

        first = step(i, (jnp.zeros((tq, DK_A), F32), jnp.zeros((tq, 1), F32)), True)
        dq, rs = lax.fori_loop(0, i, lambda j, c: step(j, c, False), first)
        dq_ref[...] = dq
        rs_ref[0] = rs

        @pl.when(i == nq - 1)
        def _():
            du_ref[...] = dv_acc[...].astype(BF16)

    stat = pl.BlockSpec((1, tq, 1), lambda h, i: (h, i, 0))
    rowv = pl.BlockSpec((1, 1, S), lambda h, i: (h, 0, 0))
    hblk = pl.BlockSpec((tq, DV_A), lambda h, i: (i, h))
    return pl.pallas_call(
        body, name=name,
        out_shape=(jax.ShapeDtypeStruct(du.shape, du.dtype),
                   jax.ShapeDtypeStruct((S, D // 2), F32), jax.ShapeDtypeStruct((S, D // 2), F32),
                   jax.ShapeDtypeStruct((nha, S, 1), F32), jax.ShapeDtypeStruct((nha, 1, S), F32)),
        grid=(nha, nq),
        in_specs=[pl.BlockSpec((tq, DK_A), lambda h, i: (i, h)),
                  pl.BlockSpec((S, DK_A), lambda h, i: (0, kb + h)),
                  pl.BlockSpec((S, DV_A), lambda h, i: (0, vb + h)),
                  stat, rowv, stat, stat, hblk, hblk, ANY],
        out_specs=(pl.BlockSpec((S, DV_A), lambda h, i: (0, vb + h)),
                   pl.BlockSpec((tq, DK_A), lambda h, i: (i, h)),
                   pl.BlockSpec((S, DK_A), lambda h, i: (0, h)),
                   stat, rowv),
        scratch_shapes=[pltpu.VMEM((S, DV_A), F32)],
        input_output_aliases={9: 0},
        compiler_params=_params("parallel", "arbitrary"),
    )(qk, qk, u, fcol, brow, m, den, dh, h, du)


def _head_norm(hh):
    mu = jnp.mean(hh, axis=1, keepdims=True)
    hc = hh - mu
    rstd = lax.rsqrt(jnp.mean(hc * hc, axis=1, keepdims=True) + LN_EPS)
    return hc * rstd, rstd


def _hgate_bwd(dhg, h, u, head_g, du, *, S, D, name):
    tm = min(256, S)
    nh = D // DV_A

    def body(dhg_ref, h_ref, o_ref, z_ref, g_ref, du_in, du_ref, dh_ref, dg_ref):
        del du_in

        @pl.when(pl.program_id(0) == 0)
        def _():
            dg_ref[...] = jnp.zeros_like(dg_ref)

        for hd in range(nh):
            sl = slice(hd * DV_A, (hd + 1) * DV_A)
            hn, rstd = _head_norm(h_ref[:, sl])
            o, z, g, d = o_ref[:, sl], z_ref[:, sl], g_ref[:, sl], dhg_ref[:, sl]
            so, sz = _sigmoid(o), _sigmoid(z)
            silu_z = z * sz
            hng = hn * g
            du_ref[:, sl] = (d * hng * silu_z * so * (1.0 - so)).astype(BF16)
            du_ref[:, D + hd * DV_A:D + (hd + 1) * DV_A] = (
                d * so * hng * (sz * (1.0 + z * (1.0 - sz)))).astype(BF16)
            t = d * so * silu_z
            dg_ref[:, sl] += jnp.sum(t * hn, axis=0, keepdims=True)
            dhn = t * g
            dh_ref[:, sl] = rstd * (dhn - jnp.mean(dhn, axis=1, keepdims=True)
                                    - hn * jnp.mean(dhn * hn, axis=1, keepdims=True))

    row = lambda i: (i, 0)
    return pl.pallas_call(
        body, name=name,
        out_shape=(jax.ShapeDtypeStruct(du.shape, du.dtype), jax.ShapeDtypeStruct((S, D), F32),
                   jax.ShapeDtypeStruct((1, D), F32)),
        grid=(S // tm,),
        in_specs=[pl.BlockSpec((tm, D), row), pl.BlockSpec((tm, D), row),
                  pl.BlockSpec((tm, D), lambda i: (i, 2)), pl.BlockSpec((tm, D), lambda i: (i, 3)),
                  pl.BlockSpec((1, D), lambda i: (0, 0)), ANY],
        out_specs=(pl.BlockSpec((tm, 2 * D), lambda i: (i, 1)), pl.BlockSpec((tm, D), row),
                   pl.BlockSpec((1, D), lambda i: (0, 0))),
        input_output_aliases={5: 0},
        compiler_params=_params("arbitrary"),
    )(dhg, h, u, u, head_g, du)


def _ln_stats(r):
    mu = jnp.mean(r, axis=1, keepdims=True)
    xc = r - mu
    rstd = lax.rsqrt(jnp.mean(xc * xc, axis=1, keepdims=True) + LN_EPS)
    return xc * rstd, rstd


def _ln_back(dxhat, xhat, rstd):
    return rstd * (dxhat - jnp.mean(dxhat, axis=1, keepdims=True)
                   - xhat * jnp.mean(dxhat * xhat, axis=1, keepdims=True))


def _ln_fwd(x, y, g, b, *, S, D, name):
    tm = min(256, S)

    def body(x_ref, y_ref, g_ref, b_ref, o_ref, ob_ref):
        xhat, _ = _ln_stats(ALPHA * x_ref[...] + y_ref[...])
        o = xhat * g_ref[...] + b_ref[...]
        o_ref[...] = o
        ob_ref[...] = o.astype(BF16)

    row = lambda i: (i, 0)
    vec = pl.BlockSpec((1, D), lambda i: (0, 0))
    return pl.pallas_call(
        body, name=name,
        out_shape=(jax.ShapeDtypeStruct((S, D), F32), jax.ShapeDtypeStruct((S, D), BF16)),
        grid=(S // tm,),
        in_specs=[pl.BlockSpec((tm, D), row), pl.BlockSpec((tm, D), row), vec, vec],
        out_specs=(pl.BlockSpec((tm, D), row), pl.BlockSpec((tm, D), row)),
        compiler_params=_params("parallel"),
    )(x, y, g, b)


def _ln_loss_bwd(x1, y2, target, g, b, *, S, D, name):
    tm = min(256, S)

    def body(x_ref, y_ref, t_ref, g_ref, b_ref, dr_ref, drb_ref, dg_ref, db_ref, loss_ref):
        @pl.when(pl.program_id(0) == 0)
        def _():
            dg_ref[...] = jnp.zeros_like(dg_ref)
            db_ref[...] = jnp.zeros_like(db_ref)
            loss_ref[...] = jnp.zeros_like(loss_ref)

        xhat, rstd = _ln_stats(ALPHA * x_ref[...] + y_ref[...])
        diff = xhat * g_ref[...] + b_ref[...] - t_ref[...]
        loss_ref[...] += (0.5 / D) * jnp.sum(diff * diff)
        dx2 = diff * (1.0 / D)
        dg_ref[...] += jnp.sum(dx2 * xhat, axis=0, keepdims=True)
        db_ref[...] += jnp.sum(dx2, axis=0, keepdims=True)
        dr = _ln_back(dx2 * g_ref[...], xhat, rstd)
        dr_ref[...] = dr
        drb_ref[...] = dr.astype(BF16)

    row = lambda i: (i, 0)
    vec = pl.BlockSpec((1, D), lambda i: (0, 0))
    return pl.pallas_call(
        body, name=name,
        out_shape=(jax.ShapeDtypeStruct((S, D), F32), jax.ShapeDtypeStruct((S, D), BF16),
                   jax.ShapeDtypeStruct((1, D), F32), jax.ShapeDtypeStruct((1, D), F32),
                   jax.ShapeDtypeStruct((1, LANES), F32)),
        grid=(S // tm,),
        in_specs=[pl.BlockSpec((tm, D), row)] * 3 + [vec, vec],
        out_specs=(pl.BlockSpec((tm, D), row), pl.BlockSpec((tm, D), row), vec, vec,
                   pl.BlockSpec((1, LANES), lambda i: (0, 0))),
        compiler_params=_params("arbitrary"),
    )(x1, y2, target, g, b)


def _ln_bwd(x, y, g, dout, *, S, D, name):
    tm = min(256, S)

    def body(x_ref, y_ref, g_ref, d_ref, dr_ref, drb_ref, dg_ref, db_ref):
        @pl.when(pl.program_id(0) == 0)
        def _():
            dg_ref[...] = jnp.zeros_like(dg_ref)
            db_ref[...] = jnp.zeros_like(db_ref)

        xhat, rstd = _ln_stats(ALPHA * x_ref[...] + y_ref[...])
        d = d_ref[...]
        dg_ref[...] += jnp.sum(d * xhat, axis=0, keepdims=True)
        db_ref[...] += jnp.sum(d, axis=0, keepdims=True)
        dr = _ln_back(d * g_ref[...], xhat, rstd)
        dr_ref[...] = dr
        drb_ref[...] = dr.astype(BF16)

    row = lambda i: (i, 0)
    vec = pl.BlockSpec((1, D), lambda i: (0, 0))
    return pl.pallas_call(
        body, name=name,
        out_shape=(jax.ShapeDtypeStruct((S, D), F32), jax.ShapeDtypeStruct((S, D), BF16),
                   jax.ShapeDtypeStruct((1, D), F32), jax.ShapeDtypeStruct((1, D), F32)),
        grid=(S // tm,),
        in_specs=[pl.BlockSpec((tm, D), row), pl.BlockSpec((tm, D), row), vec, pl.BlockSpec((tm, D), row)],
        out_specs=(pl.BlockSpec((tm, D), row), pl.BlockSpec((tm, D), row), vec, vec),
        compiler_params=_params("arbitrary"),
    )(x, y, g, dout)


def _sb_scores(q, kj, diagonal, t):
    z = _dot_nt(q, kj) * (DH_B ** -0.5)
    ls = _log_sigmoid(z)
    if not diagonal:
        return None, ls, ls - z
    valid = lax.broadcasted_iota(jnp.int32, (t, t), 1) < lax.broadcasted_iota(jnp.int32, (t, t), 0)
    return valid, ls, jnp.where(valid, ls - z, 0.0)


def _keep(valid, x):
    return x if valid is None else jnp.where(valid, x, 0.0)


def _sb_fwd(q2, kv, z2, *, S, D, name, t=SB_TILE):
    t = min(t, S)
    nq = S // t
    nh = D // DH_B

    def body(q_ref, k_ref, v_ref, z_ref, o_ref, hb_ref, tot_ref):
        i = pl.program_id(1)
        q = q_ref[...]
        _, _, after = _tri(min(t, MXU_DEPTH), lambda r, c: r > c)
        tri_after = after.astype(BF16)

        def step(j, carry, diagonal):
            acc, cr = carry
            off = pl.multiple_of(j * t, t)
            valid, ls, lneg = _sb_scores(q, k_ref[pl.ds(off, t), :], diagonal, t)
            between = cr + _block_sums(lneg, tri_after, True)
            a = _keep(valid, jnp.exp(ls + between))
            acc = acc + _dot_nn(a.astype(BF16), v_ref[pl.ds(off, t), :])
            return acc, cr + jnp.sum(lneg, axis=1, keepdims=True)

        carry = step(i, (jnp.zeros((t, DH_B), F32), jnp.zeros((t, 1), F32)), True)
        acc, cr = lax.fori_loop(0, i, lambda jj, c: step(i - 1 - jj, c, False), carry)
        o_ref[...] = acc
        z = z_ref[...]
        hb_ref[...] = (acc * (z * _sigmoid(z))).astype(BF16)
        tot_ref[0] = cr

    blk = pl.BlockSpec((t, DH_B), lambda h, i: (i, h))
    return pl.pallas_call(
        body, name=name,
        out_shape=(jax.ShapeDtypeStruct((S, D), F32), jax.ShapeDtypeStruct((S, D), BF16),
                   jax.ShapeDtypeStruct((nh, S, 1), F32)), grid=(nh, nq),
        in_specs=[blk, pl.BlockSpec((S, DH_B), lambda h, i: (0, h)),
                  pl.BlockSpec((S, DH_B), lambda h, i: (0, nh + h)), blk],
        out_specs=(blk, blk, pl.BlockSpec((1, t, 1), lambda h, i: (h, i, 0))),
        compiler_params=_params("parallel", "arbitrary"),
    )(q2, kv, kv, z2)


def _sb_bwd(q2, kv, dhb, att, z2, tot, *, S, D, name, t=SB_TILE):
    t = min(t, S)
    nq = S // t
    nh = D // DH_B

    def body(q_ref, k_ref, v_ref, dhb_ref, att_ref, z_ref, tot_ref, dq_ref, dk_ref, dv_ref, dz2_ref, dk_acc, dv_acc):
        i = pl.program_id(1)

        @pl.when(i == 0)
        def _():
            dk_acc[...] = jnp.zeros_like(dk_acc)
            dv_acc[...] = jnp.zeros_like(dv_acc)

        q = q_ref[...]
        z, dhb = z_ref[...], dhb_ref[...]
        sz = _sigmoid(z)
        dz2_ref[...] = (dhb * att_ref[...] * (sz * (1.0 + z * (1.0 - sz)))).astype(BF16)
        do_b = (dhb * (z * sz)).astype(BF16)
        _, _, after = _tri(min(t, MXU_DEPTH), lambda r, c: r > c)
        tri_after = after.astype(BF16)
        _, _, before = _tri(min(t, MXU_DEPTH), lambda r, c: r < c)
        tri_before = before.astype(BF16)

        def step(j, carry, diagonal):
            dq, rest, cg = carry
            off = pl.multiple_of(j * t, t)
            kj = k_ref[pl.ds(off, t), :]
            valid, ls, lneg = _sb_scores(q, kj, diagonal, t)
            rest = rest - jnp.sum(lneg, axis=1, keepdims=True)
            between = rest + _block_sums(lneg, tri_after, True)
            a = _keep(valid, jnp.exp(ls + between))
            g = _dot_nt(do_b, v_ref[pl.ds(off, t), :]) * a
            dv_acc[pl.ds(off, t), :] += _dot_tn(a.astype(BF16), do_b)
            e = cg + _block_sums(g, tri_before, False)
            dz = _keep(valid, g * jnp.exp(lneg) - e * jnp.exp(ls)) * (DH_B ** -0.5)
            dz_b = dz.astype(BF16)
            dq = dq + _dot_nn(dz_b, kj)
            dk_acc[pl.ds(off, t), :] += _dot_tn(dz_b, q)
            return dq, rest, cg + jnp.sum(g, axis=1, keepdims=True)

        carry = lax.fori_loop(0, i, lambda j, c: step(j, c, False),
                              (jnp.zeros((t, DH_B), F32), tot_ref[0], jnp.zeros((t, 1), F32)))
        dq, _, _ = step(i, carry, True)
        dq_ref[...] = dq.astype(BF16)

        @pl.when(i == nq - 1)
        def _():
            dk_ref[...] = dk_acc[...].astype(BF16)
            dv_ref[...] = dv_acc[...].astype(BF16)

    blk = pl.BlockSpec((t, DH_B), lambda h, i: (i, h))
    sd = jax.ShapeDtypeStruct((S, D), BF16)
    return pl.pallas_call(
        body, name=name, out_shape=(sd, sd, sd, sd), grid=(nh, nq),
        in_specs=[blk, pl.BlockSpec((S, DH_B), lambda h, i: (0, h)),
                  pl.BlockSpec((S, DH_B), lambda h, i: (0, nh + h)), blk, blk, blk,
                  pl.BlockSpec((1, t, 1), lambda h, i: (h, i, 0))],
        out_specs=(blk, pl.BlockSpec((S, DH_B), lambda h, i: (0, h)),
                   pl.BlockSpec((S, DH_B), lambda h, i: (0, h)), blk),
        scratch_shapes=[pltpu.VMEM((S, DH_B), F32), pltpu.VMEM((S, DH_B), F32)],
        compiler_params=_params("parallel", "arbitrary"),
    )(q2, kv, kv, dhb, att, z2, tot)


def _adamw_math(w, g, m, v):
    mn = ADAM_B1 * m + (1.0 - ADAM_B1) * g
    vn = ADAM_B2 * v + (1.0 - ADAM_B2) * (g * g)
    m_hat = mn / (1.0 - ADAM_B1 ** ADAM_STEP)
    v_hat = vn / (1.0 - ADAM_B2 ** ADAM_STEP)
    return -ADAM_LR * (m_hat / (jnp.sqrt(v_hat) + ADAM_EPS) + ADAM_WD * w), mn, vn


def _adamw(w, g, m, v, *, name, tr=128):
    rows, cols = w.shape
    tr = min(tr, rows)
    assert rows % tr == 0

    def body(w_ref, g_ref, m_ref, v_ref, d_ref, mo_ref, vo_ref):
        d, mn, vn = _adamw_math(w_ref[...], g_ref[...], m_ref[...], v_ref[...])
        d_ref[...] = d
        mo_ref[...] = mn
        vo_ref[...] = vn

    blk = pl.BlockSpec((tr, cols), lambda i: (i, 0))
    sd = jax.ShapeDtypeStruct((rows, cols), F32)
    return pl.pallas_call(
        body, name=name, out_shape=(sd, sd, sd), grid=(rows // tr,),
        in_specs=[blk] * 4, out_specs=(blk, blk, blk),
        compiler_params=_params("parallel"),
    )(w, g, m, v)


def _local_step(x, target, wa, wg, wao, wbi, wkv, wbo, gate_b, conv_w, conv_b, head_g,
                a_ln_g, a_ln_b, b_ln_g, b_ln_b, *, S, D, nha, rs):
    nb = S // LANES
    kw = dict(S=S, D=D)
    xb = x.astype(BF16)
    wa_chunk = D if wa.ndim == 3 else None
    u = _mm(xb, wa, mode="nn", M=S, N=4 * D, K=D, out_dtype=F32, name="a_in", b_chunk=wa_chunk)
    ug = _mm(xb, wg, mode="nn", M=S, N=LANES, K=D, out_dtype=F32, name="a_in_gates")
    qk = _conv_fwd(u, conv_w, conv_b, name="conv_fwd", **kw)
    gt = ug[:, :2 * nha].T.reshape(2 * nha * nb, LANES)
    gbias = jnp.repeat(gate_b.reshape(2 * nha), nb).reshape(2 * nha * nb, 1)
    fcs, brow = _gates_fwd(gt, gbias, nha=nha, nb=nb, name="gates_fwd")
    fcol = fcs.reshape(nha, S, 1)
    brow = brow.reshape(nha, 1, S)
    h, hg, m, den = _mlstm_fwd(qk, u, fcol, brow, head_g, nha=nha, name="mlstm_fwd", **kw)
    y = _mm(hg, wao, mode="nn", M=S, N=D, K=D, out_dtype=F32, name="a_out")
    x1, x1b = _ln_fwd(x, y, a_ln_g, a_ln_b, name="ln_a_fwd", **kw)
    q2 = _mm(x1b, wbi, mode="nn", M=S, N=D, K=D, out_dtype=BF16, name="b_in_q")
    z2 = _mm(x1b, wbi, mode="nn", M=S, N=D, K=D, out_dtype=F32, name="b_in_z", b_off=(0, D))
    kv = _mm(x1b, wkv, mode="nn", M=S, N=2 * D, K=D, out_dtype=BF16, name="b_kv")
    att, hb, sb_tot = _sb_fwd(q2, kv, z2, name="sb_fwd", **kw)
    y2 = _mm(hb, wbo, mode="nn", M=S, N=D, K=D, out_dtype=F32, name="b_out")
    dr2, dr2b, d_bln_g, d_bln_b, loss = _ln_loss_bwd(x1, y2, target, b_ln_g, b_ln_b, name="ln_b_loss", **kw)
    g_wbo = _mm(hb, dr2b, mode="tn", M=D, N=D, K=S, out_dtype=BF16, name="g_b_out")
    dhb = _mm(dr2b, wbo, mode="nt", M=S, N=D, K=D, out_dtype=F32, name="d_b_out")
    dq2, dk2, dv2, dz2 = _sb_bwd(q2, kv, dhb, att, z2, sb_tot, name="sb_bwd", **kw)
    g_wbi = _mm(x1b, dq2, mode="tn", M=D, N=D, K=S, out_dtype=BF16, name="g_b_in_q",
                into=lax.empty((D, 2 * D), BF16))
    g_wbi = _mm(x1b, dz2, mode="tn", M=D, N=D, K=S, out_dtype=BF16, name="g_b_in_z", into=g_wbi, out_off=(0, D))
    g_wkv = _mm(x1b, dk2, mode="tn", M=D, N=D, K=S, out_dtype=BF16, name="g_kv_k",
                into=lax.empty((D, 2 * D), BF16))
    g_wkv = _mm(x1b, dv2, mode="tn", M=D, N=D, K=S, out_dtype=BF16, name="g_kv_v", into=g_wkv, out_off=(0, D))
    dq2 = rs.begin("b", dict(wbo=g_wbo, wbi=g_wbi, wkv=g_wkv), dq2)
    dx1 = _mm(dq2, wbi, mode="nt", M=S, N=D, K=D, out_dtype=F32, name="d_b_in_q", acc_in=dr2, acc_scale=ALPHA)
    dx1 = _mm(dz2, wbi, mode="nt", M=S, N=D, K=D, out_dtype=F32, name="d_b_in_z", b_off=(0, D), acc_in=dx1)
    dx1 = _mm(dk2, wkv, mode="nt", M=S, N=D, K=D, out_dtype=F32, name="d_kv_k", acc_in=dx1)
    dx1 = _mm(dv2, wkv, mode="nt", M=S, N=D, K=D, out_dtype=F32, name="d_kv_v", b_off=(0, D), acc_in=dx1)
    dx1 = rs.pair("b", dx1)
    dr, drb, d_aln_g, d_aln_b = _ln_bwd(x, y, a_ln_g, dx1, name="ln_a_bwd", **kw)
    g_wao = _mm(hg, drb, mode="tn", M=D, N=D, K=S, out_dtype=BF16, name="g_a_out")
    drb = rs.begin("a", dict(wao=g_wao), drb)
    dhg = _mm(drb, wao, mode="nt", M=S, N=D, K=D, out_dtype=F32, name="d_a_out")
    du = lax.empty((S, 4 * D), BF16)
    du, dh, d_head_g = _hgate_bwd(dhg, h, u, head_g, du, name="hgate_bwd", **kw)
    dh = rs.pair("a", dh)
    du, dq, dk, rowsum, colsum = _mlstm_bwd(qk, u, fcol, brow, m, den, dh, h, du, nha=nha, name="mlstm_bwd", **kw)
    dgt, dgtot = _gates_bwd(rowsum.reshape(nha * nb, LANES), colsum.reshape(nha * nb, LANES), gt, gbias,
                            nha=nha, nb=nb, name="gates_bwd")
    d_gate_b = dgtot[::nb, 0].reshape(1, 2 * nha)
    dgp = jnp.pad(dgt.reshape(2 * nha, S).T, ((0, 0), (0, LANES - 2 * nha))).astype(BF16)
    du, d_conv_w, d_conv_b = _conv_bwd(u, dq, dk, conv_w, conv_b, du, name="conv_bwd", **kw)
    small = dict(gate_b=d_gate_b, conv_w=d_conv_w, conv_b=d_conv_b, head_g=d_head_g,
                 a_ln_g=d_aln_g, a_ln_b=d_aln_b, b_ln_g=d_bln_g, b_ln_b=d_bln_b)
    du = rs.chip("a", rs.chip("b", du))
    g_wa = _mm(xb, du, mode="tn", M=D, N=4 * D, K=S, out_dtype=BF16, name="g_a_in")
    g_wg = _mm(xb, dgp, mode="tn", M=D, N=LANES, K=S, out_dtype=BF16, name="g_a_in_gates")
    du = rs.begin("w", dict(wa=g_wa, wg=g_wg), du)
    du = rs.pair("w", rs.small(small, loss, du))
    du = rs.finish("a", rs.finish("b", du))
    dx = _mm(du, wa, mode="nt", M=S, N=D, K=4 * D, out_dtype=F32, name="d_a_in", acc_in=dr, acc_scale=ALPHA,
             b_chunk=wa_chunk)
    dx = _mm(dgp, wg, mode="nt", M=S, N=D, K=LANES, out_dtype=F32, name="d_a_in_gates", acc_in=dx)
    return dx


def _coords():
    return lax.axis_index("x"), lax.axis_index("y"), lax.axis_index("c")


def _other_chips(x, y):
    return [(1 - x, y), (x, 1 - y), (1 - x, 1 - y)]


def _rows(ref, start, size):
    return ref.at[pl.ds(pl.multiple_of(start, size), size), :]


def _window(kind, ref, shard_shape, j, hf=None, quarter=None):
    r, cw = shard_shape
    row0, nr = (0, r) if hf is None else (hf * (r // 2), r // 2)
    if quarter is not None:
        row0, nr = row0 + quarter * (r // 4), r // 4
    if kind == "stack":
        return ref.at[j, pl.ds(pl.multiple_of(row0, nr), nr), :]
    if kind == "rows":
        return ref.at[pl.ds(pl.multiple_of(j * r + row0, nr), nr), :]
    assert kind == "cols"
    return ref.at[pl.ds(pl.multiple_of(row0, nr), nr), pl.ds(pl.multiple_of(j * cw, cw), cw)]


def _gathered_shape(kind, shard_shape):
    r, cw = shard_shape
    return {"stack": (N_CHIPS, r, cw), "rows": (N_CHIPS * r, cw), "cols": (r, N_CHIPS * cw)}[kind]


def _remote(src, dst, send_sems, recv_sems, k, to):
    return pltpu.make_async_remote_copy(src_ref=src, dst_ref=dst, send_sem=send_sems.at[k],
                                        recv_sem=recv_sems.at[k], device_id=to, device_id_type=MESH)


def _allgather_copies(s_refs, sm_ref, g_refs, smg_ref, send_sems, recv_sems, kinds, shapes):
    n = len(s_refs)
    x, y, c = _coords()
    me, sibling, chips = 2 * x + y, (x, y, 1 - c), _other_chips(x, y)
    ids = [2 * cx + cy for cx, cy in chips]
    d2d, own_base, small_base = 4 * n, 7 * n, 8 * n

    def win(a, j, hf=None, quarter=None):
        return _window(kinds[a], g_refs[a], shapes[a], j, hf, quarter)

    def small_win(j):
        w = sm_ref.shape[1]
        return smg_ref.at[:, pl.ds(pl.multiple_of(j * w, w), w)]

    def rc(src, dst, k, to):
        return _remote(src, dst, send_sems, recv_sems, k, to)

    first = [rc(s_refs[a], win(a, me), own_base + a, sibling) for a in range(n)]
    for a in range(n):
        src = _rows(s_refs[a], c * (shapes[a][0] // 2), shapes[a][0] // 2)
        first += [rc(src, win(a, me, c), 4 * a + k, (*chips[k], c)) for k in range(2)]
    if sm_ref is not None:
        first.append(rc(sm_ref, small_win(me), small_base + 3, sibling))
        first += [rc(sm_ref, small_win(me), small_base + k, (*chip, c)) for k, chip in enumerate(chips)]
    for cp in first:
        cp.start()
    passed = []
    for a in range(n):
        for k in range(2):
            rc(win(a, ids[k], c), win(a, ids[k], c), 4 * a + k, sibling).wait_recv()
            part = win(a, ids[k], c, quarter=k)
            passed.append(rc(part, part, 4 * a + 2 + k, (*chips[1 - k], c)))
            passed.append(rc(win(a, ids[k], c), win(a, ids[k], c), d2d + 3 * a + k, sibling))
            passed[-2].start()
            passed[-1].start()
    for a in range(n):
        for k in range(2):
            part = win(a, ids[2], c, quarter=k)
            rc(part, part, 4 * a + 2 + k, sibling).wait_recv()
        passed.append(rc(win(a, ids[2], c), win(a, ids[2], c), d2d + 3 * a + 2, sibling))
        passed[-1].start()
    if sm_ref is not None:
        for k in range(3):
            rc(small_win(ids[k]), small_win(ids[k]), small_base + k, sibling).wait_recv()
        rc(small_win(me), small_win(me), small_base + 3, sibling).wait_recv()
    for a in range(n):
        for k in range(3):
            rc(win(a, ids[k], 1 - c), win(a, ids[k], 1 - c), d2d + 3 * a + k, sibling).wait_recv()
        rc(win(a, me), win(a, me), own_base + a, sibling).wait_recv()
    for cp in first + passed:
        cp.wait_send()


def _peer_handshake(peers):
    barrier = pltpu.get_barrier_semaphore()
    for peer in peers:
        pl.semaphore_signal(barrier, inc=1, device_id=peer, device_id_type=MESH)
    pl.semaphore_wait(barrier, len(peers))


def _behind(copies, inputs, out_structs, nsem, peers, *, name, collective_id):
    hbm = pltpu.MemorySpace.HBM
    in_refs = [jax.new_ref(a, memory_space=hbm) for a in inputs]
    out_refs = [jax.empty_ref(st, memory_space=hbm) for st in out_structs]

    @pl.kernel(mesh=plsc.ScalarSubcoreMesh(axis_name="sequencer", num_cores=1), name=name,
               scratch_types=(pltpu.SemaphoreType.DMA((nsem,)), pltpu.SemaphoreType.DMA((nsem,))),
               compiler_params=pltpu.CompilerParams(collective_id=collective_id))
    def launch(send_sems, recv_sems):
        _peer_handshake(peers(*_coords()))
        copies(in_refs, out_refs, send_sems, recv_sems)

    launch()
    return [r[...] for r in out_refs]


def _sibling(x, y, c):
    return [(x, y, 1 - c)]


def _same_core_of_other_chips(x, y, c):
    return [(cx, cy, c) for cx, cy in _other_chips(x, y)]


def _sibling_and_other_chips(x, y, c):
    return _sibling(x, y, c) + _same_core_of_other_chips(x, y, c)


ID_ALL_PEERS, ID_SIBLING, ID_CHIPS = 0, 1, 2


def _allgather_behind(shards, kinds, small=None, *, name):
    n = len(shards)
    shapes = [sh.shape for sh in shards]

    def copies(in_refs, out_refs, send_sems, recv_sems):
        if small is None:
            _allgather_copies(in_refs, None, out_refs, None, send_sems, recv_sems, kinds, shapes)
        else:
            _allgather_copies(in_refs[:n], in_refs[n], out_refs[:n], out_refs[n], send_sems, recv_sems, kinds, shapes)

    outs = [jax.ShapeDtypeStruct(_gathered_shape(kinds[a], shapes[a]), shards[a].dtype) for a in range(n)]
    inputs = list(shards)
    if small is not None:
        outs.append(jax.ShapeDtypeStruct((small.shape[0], N_CHIPS * small.shape[1]), small.dtype))
        inputs.append(small)
    return _behind(copies, inputs, outs, 8 * n + 4, _sibling_and_other_chips, name=name, collective_id=ID_ALL_PEERS)


def _rs_pair_exchange(views, *, name):
    n = len(views)

    def copies(g_refs, p_refs, send_sems, recv_sems):
        x, y, c = _coords()
        cps = []
        for a in range(n):
            r2 = views[a].shape[1] // 2
            src = g_refs[a].at[:, pl.ds(pl.multiple_of((1 - c) * r2, r2), r2), :]
            cps.append(_remote(src, p_refs[a], send_sems, recv_sems, a, (x, y, 1 - c)))
        for cp in cps:
            cp.start()
        for cp in cps:
            cp.wait()

    outs = [jax.ShapeDtypeStruct((v.shape[0], v.shape[1] // 2, v.shape[2]), v.dtype) for v in views]
    return _behind(copies, views, outs, n, _sibling, name=name, collective_id=ID_SIBLING)


def _add_half(view, part, core, *, name):
    nch, r, cols = view.shape
    r2 = r // 2
    tr = min(128, r2)
    nt = r2 // tr

    def body(c_ref, g_ref, p_ref, o_ref):
        del c_ref
        o_ref[...] = (g_ref[...].astype(F32) + p_ref[...].astype(F32)).astype(o_ref.dtype)

    return pl.pallas_call(
        body, name=name, out_shape=jax.ShapeDtypeStruct((nch, r2, cols), view.dtype),
        grid_spec=pltpu.PrefetchScalarGridSpec(
            num_scalar_prefetch=1, grid=(nch, nt),
            in_specs=[pl.BlockSpec((1, tr, cols), lambda ch, i, c_ref: (ch, c_ref[0] * nt + i, 0)),
                      pl.BlockSpec((1, tr, cols), lambda ch, i, c_ref: (ch, i, 0))],
            out_specs=pl.BlockSpec((1, tr, cols), lambda ch, i, c_ref: (ch, i, 0))),
        compiler_params=_params("parallel", "parallel"),
    )(core, view, part)


def _chunk(kind, ref, j, cw):
    if kind == "cols":
        return ref.at[0, :, pl.ds(pl.multiple_of(j * cw, cw), cw)]
    return ref.at[j]


def _rs_chips(pairs, kinds, *, name):
    n = len(pairs)
    half_shapes = []
    for a in range(n):
        nch, r2, cols = pairs[a].shape
        half_shapes.append((r2, cols // N_CHIPS) if kinds[a] == "cols" else (r2, cols))

    def copies(q_refs, t_refs, send_sems, recv_sems):
        x, y, c = _coords()
        chips = _other_chips(x, y)
        sends = []
        for a in range(n):
            for k, (cx, cy) in enumerate(chips):
                src = _chunk(kinds[a], q_refs[a], 2 * cx + cy, half_shapes[a][1])
                sends.append(_remote(src, t_refs[a].at[k], send_sems, recv_sems, 3 * a + k, (cx, cy, c)))
        for cp in sends:
            cp.start()
        for a in range(n):
            for k in range(3):
                slot = t_refs[a].at[k]
                _remote(slot, slot, send_sems, recv_sems, 3 * a + k, (x, y, c)).wait_recv()
        for cp in sends:
            cp.wait_send()

    outs = [jax.ShapeDtypeStruct((3, *half_shapes[a]), pairs[a].dtype) for a in range(n)]
    return _behind(copies, pairs, outs, 3 * n, _same_core_of_other_chips, name=name, collective_id=ID_CHIPS)


def _sum_chips(pair, parts, chip, kind, *, name):
    _, r, cols = parts.shape
    tr = min(128, r)
    if kind == "cols":
        own_spec = pl.BlockSpec((1, tr, cols), lambda i, chip_ref: (0, i, chip_ref[0]))
    else:
        own_spec = pl.BlockSpec((1, tr, cols), lambda i, chip_ref: (chip_ref[0], i, 0))

    def body(chip_ref, q_ref, p_ref, o_ref):
        del chip_ref
        acc = q_ref[0].astype(F32)
        for s in range(3):
            acc = acc + p_ref[s].astype(F32)
        o_ref[...] = acc

    return pl.pallas_call(
        body, name=name, out_shape=jax.ShapeDtypeStruct((r, cols), F32),
        grid_spec=pltpu.PrefetchScalarGridSpec(
            num_scalar_prefetch=1, grid=(r // tr,),
            in_specs=[own_spec, pl.BlockSpec((3, tr, cols), lambda i, chip_ref: (0, i, 0))],
            out_specs=pl.BlockSpec((tr, cols), lambda i, chip_ref: (i, 0))),
        compiler_params=_params("parallel"),
    )(chip, pair, parts)


def _rs_share(halves, *, name):
    n = len(halves)

    def copies(h_refs, g_refs, send_sems, recv_sems):
        x, y, c = _coords()
        cps = [_remote(h_refs[a], g_refs[a], send_sems, recv_sems, a, (x, y, 1 - c)) for a in range(n)]
        for cp in cps:
            cp.start()
        for cp in cps:
            cp.wait()

    outs = [jax.ShapeDtypeStruct(h.shape, h.dtype) for h in halves]
    return _behind(copies, halves, outs, n, _sibling, name=name, collective_id=ID_SIBLING)


class _GradReducer:
    def __init__(self, w, m, v, long_name, core, chip_ix, as_views, small_step):
        self.w, self.m, self.v, self.long_name = w, m, v, long_name
        self.core, self.chip_ix, self.as_views, self.small_step = core, chip_ix, as_views, small_step
        self.groups, self.results, self.small_results = {}, {}, None

    def small(self, small, loss_row, tie):
        (small, loss_row), tie = lax.optimization_barrier(((small, loss_row), tie))
        self.small_results, tie = lax.optimization_barrier((self.small_step(small, loss_row), tie))
        return tie

    def begin(self, tag, grads, tie):
        keys, views, kinds = self.as_views(grads)
        views, tie = lax.optimization_barrier((views, tie))
        parts = _rs_pair_exchange(views, name="rs_pair_exchange_" + tag)
        self.groups[tag] = dict(keys=keys, views=views, kinds=kinds, parts=parts)
        return tie

    def pair(self, tag, tie):
        g = self.groups[tag]
        g["parts"], tie = lax.optimization_barrier((g["parts"], tie))
        pairs = [_add_half(v, p, self.core, name="pair_sum_" + self.long_name[k])
                 for v, p, k in zip(g["views"], g["parts"], g["keys"])]
        pairs, tie = lax.optimization_barrier((pairs, tie))
        g["pairs"] = pairs
        g["slots"] = _rs_chips(pairs, g["kinds"], name="rs_chips_" + tag)
        return tie

    def chip(self, tag, tie):
        g = self.groups[tag]
        g["slots"], tie = lax.optimization_barrier((g["slots"], tie))
        halves = [_sum_chips(q, t, self.chip_ix, kd, name="chip_sum_" + self.long_name[k])
                  for q, t, kd, k in zip(g["pairs"], g["slots"], g["kinds"], g["keys"])]
        halves, tie = lax.optimization_barrier((halves, tie))
        g["halves"] = halves
        g["others"] = _rs_share(halves, name="rs_share_" + tag)
        return tie

    def finish(self, tag, tie):
        g = self.groups[tag]
        g["others"], tie = lax.optimization_barrier((g["others"], tie))
        out = [_adamw_halves(self.w[k], a, b, self.m[k], self.v[k], self.core, name="adamw_" + self.long_name[k])
               for k, a, b in zip(g["keys"], g["halves"], g["others"])]
        out, tie = lax.optimization_barrier((out, tie))
        self.results.update(zip(g["keys"], out))
        return tie


def _adamw_halves(w, mine, theirs, m, v, core, *, name, tr=128):
    rows, cols = w.shape
    hr = rows // 2
    tr = min(tr, hr)
    nt = hr // tr

    def body(c_ref, w_ref, a_ref, b_ref, m_ref, v_ref, g_ref, d_ref, mo_ref, vo_ref):
        gg = jnp.where(pl.program_id(0) == c_ref[0], a_ref[...], b_ref[...])
        g_ref[...] = gg
        d, mn, vn = _adamw_math(w_ref[...], gg, m_ref[...], v_ref[...])
        d_ref[...] = d
        mo_ref[...] = mn
        vo_ref[...] = vn

    full = pl.BlockSpec((tr, cols), lambda hf, i, c_ref: (hf * nt + i, 0))
    half = pl.BlockSpec((tr, cols), lambda hf, i, c_ref: (i, 0))
    sd = jax.ShapeDtypeStruct((rows, cols), F32)
    return pl.pallas_call(
        body, name=name, out_shape=(sd, sd, sd, sd),
        grid_spec=pltpu.PrefetchScalarGridSpec(
            num_scalar_prefetch=1, grid=(2, nt),
            in_specs=[full, half, half, full, full], out_specs=(full, full, full, full)),
        compiler_params=_params("parallel", "parallel"),
    )(core, w, mine, theirs, m, v)


def _allreduce_small(vec):
    m_per, ncol = vec.shape
    n_dev = 2 * N_CHIPS

    def body(x_ref, out_ref, sum_ref, send_sems, recv_sems, local_sem):
        x, y, c = _coords()
        me, sibling = (x, y, c), (x, y, 1 - c)
        chips = _other_chips(x, y)

        def rows(px, py, pc):
            return out_ref.at[pl.ds(pl.multiple_of((4 * px + 2 * py + pc) * m_per, m_per), m_per), :]

        def copy(k, block, to, src=None):
            return _remote(rows(*block) if src is None else src, rows(*block), send_sems, recv_sems, k, to)

        mine = pltpu.make_async_copy(x_ref, rows(*me), local_sem)
        mine.start()
        first = [copy(0, me, sibling, src=x_ref)]
        first += [copy(1 + j, me, (*chip, c), src=x_ref) for j, chip in enumerate(chips)]
        for cp in first:
            cp.start()
        passed = [copy(4 + j, (*chip, c), sibling) for j, chip in enumerate(chips)]
        for j, chip in enumerate(chips):
            copy(1 + j, (*chip, c), me).wait_recv()
            passed[j].start()
        copy(0, sibling, me).wait_recv()
        for j, chip in enumerate(chips):
            copy(4 + j, (*chip, 1 - c), me).wait_recv()
        for cp in first + passed:
            cp.wait_send()
        mine.wait()
        acc = out_ref[0:m_per, :]
        for d in range(1, n_dev):
            acc = acc + out_ref[d * m_per:(d + 1) * m_per, :]
        sum_ref[...] = acc

    vm = pl.BlockSpec(memory_space=pltpu.VMEM)
    return pl.pallas_call(
        body, name="allreduce_small",
        out_shape=(jax.ShapeDtypeStruct((n_dev * m_per, ncol), vec.dtype), jax.ShapeDtypeStruct((m_per, ncol), vec.dtype)),
        in_specs=[vm], out_specs=(vm, vm),
        scratch_shapes=[pltpu.SemaphoreType.DMA((7,)), pltpu.SemaphoreType.DMA((7,)), pltpu.SemaphoreType.DMA],
    )(vec)[1]


def _pack_rows(pieces, total_rows):
    rows = []
    for p in pieces:
        flat = p.reshape(-1)
        flat = jnp.pad(flat, (0, (-flat.shape[0]) % LANES))
        rows.append(flat.reshape(-1, LANES))
    out = jnp.concatenate(rows, axis=0)
    return jnp.pad(out, ((0, total_rows - out.shape[0]), (0, 0)))


def _unpack_rows(packed, shapes):
    out, r = [], 0
    for shp in shapes:
        size = 1
        for d in shp:
            size *= d
        nr = -(-size // LANES)
        out.append(packed[r:r + nr].reshape(-1)[:size].reshape(shp))
        r += nr
    return out


def _round_up(v, m):
    return -(-v // m) * m


def kernel(x, a_w_in, a_gate_b, a_conv_w, a_conv_b, a_head_g, a_w_out, a_ln_g, a_ln_b, kv_w, b_w_in, b_w_out, b_ln_g, b_ln_b, loss_target, m_a_w_in, m_a_gate_b, m_a_conv_w, m_a_conv_b, m_a_head_g, m_a_w_out, m_a_ln_g, m_a_ln_b, m_kv_w, m_b_w_in, m_b_w_out, m_b_ln_g, m_b_ln_b, v_a_w_in, v_a_gate_b, v_a_conv_w, v_a_conv_b, v_a_head_g, v_a_w_out, v_a_ln_g, v_a_ln_b, v_kv_w, v_b_w_in, v_b_w_out, v_b_ln_g, v_b_ln_b):
    _, S, D = x.shape
    nha = a_gate_b.shape[1] // 2
    chip = 2 * lax.axis_index("x") + lax.axis_index("y")
    core = lax.axis_index("c").astype(jnp.int32).reshape(1)
    dq = D // N_CHIPS

    ca = a_w_in.shape[2]
    edge = (N_CHIPS - 1) * (ca - D)
    wide = _round_up(ca + edge, LANES)
    shifted = lax.dynamic_slice_in_dim(
        jnp.pad(a_w_in[0].astype(BF16), ((0, 0), (edge, wide - ca))), edge - chip * (ca - D), wide, axis=1)
    shards = [shifted, a_w_out[0].astype(BF16), kv_w.astype(BF16), b_w_in[0].astype(BF16), b_w_out[0].astype(BF16)]
    kinds = ["stack", "rows", "cols", "cols", "rows"]
    small_shard = jnp.concatenate([a_conv_w[0], a_conv_b, a_head_g, a_ln_g, a_ln_b], axis=0)
    wa_g, small_full = _allgather_behind(shards[:1], kinds[:1], small_shard, name="allgather_first")
    wao, = _allgather_behind(shards[1:2], kinds[1:2], name="allgather_a_w_out")
    wkv, wbi = _allgather_behind(shards[2:4], kinds[2:4], name="allgather_b_in")
    wbo, = _allgather_behind(shards[4:], kinds[4:], name="allgather_b_w_out")
    wa = _join_chunk_edges(wa_g, D)
    wg = wa[N_CHIPS - 1, :, D:D + LANES]
    conv_w, conv_b, head_g, ln_g_a, ln_b_a = (small_full[0:CONV_A], small_full[4:5], small_full[5:6],
                                              small_full[6:7], small_full[7:8])

    chip_ix = chip.astype(jnp.int32).reshape(1)
    (w_wa, m_wa, v_wa), x = lax.optimization_barrier(((a_w_in[0], m_a_w_in[0], v_a_w_in[0]), x))
    w_big = dict(wa=w_wa, wao=a_w_out[0], wkv=kv_w, wbi=b_w_in[0], wbo=b_w_out[0])
    m_big = dict(wa=m_wa, wao=m_a_w_out[0], wkv=m_kv_w, wbi=m_b_w_in[0], wbo=m_b_w_out[0])
    v_big = dict(wa=v_wa, wao=v_a_w_out[0], wkv=v_kv_w, wbi=v_b_w_in[0], wbo=v_b_w_out[0])
    long_name = dict(wa="a_w_in", wao="a_w_out", wkv="kv_w", wbi="b_w_in", wbo="b_w_out")

    def as_views(g):
        if "wa" in g:
            g_full = jnp.concatenate([g["wa"], g["wg"][:, :2 * nha]], axis=1)
            ca = g_full.shape[1] // N_CHIPS
            return ["wa"], [jnp.stack([g_full[:, j * ca:(j + 1) * ca] for j in range(N_CHIPS)], axis=0)], ["stack"]
        keys = list(g)
        views = [g[k].reshape(N_CHIPS, dq, D) if k in ("wao", "wbo") else g[k][None] for k in keys]
        return keys, views, ["stack" if k in ("wao", "wbo") else "cols" for k in keys]

    w_small = [a_gate_b, a_conv_w[0], a_conv_b, a_head_g, a_ln_g, a_ln_b, b_ln_g, b_ln_b]
    m_small = [m_a_gate_b, m_a_conv_w[0], m_a_conv_b, m_a_head_g, m_a_ln_g, m_a_ln_b, m_b_ln_g, m_b_ln_b]
    v_small = [v_a_gate_b, v_a_conv_w[0], v_a_conv_b, v_a_head_g, v_a_ln_g, v_a_ln_b, v_b_ln_g, v_b_ln_b]

    def small_step(small, loss_row):
        order = ["conv_w", "conv_b", "head_g", "a_ln_g", "a_ln_b", "b_ln_g", "b_ln_b", "gate_b"]
        full_shapes = [(CONV_A, D), (1, D), (1, D), (1, D), (1, D), (1, D), (1, D), (1, 2 * nha)]
        n_rows = sum(-(-(s[0] * s[1]) // LANES) for s in full_shapes) + 1
        total = _allreduce_small(_pack_rows([small[k] for k in order] + [loss_row], _round_up(n_rows, 8)))
        sums = dict(zip(order, _unpack_rows(total, full_shapes)))

        def mine(v):
            return lax.dynamic_slice_in_dim(v, chip * dq, dq, axis=1)

        g_small = [sums["gate_b"], mine(sums["conv_w"]), mine(sums["conv_b"]), mine(sums["head_g"]),
                   mine(sums["a_ln_g"]), mine(sums["a_ln_b"]), sums["b_ln_g"], sums["b_ln_b"]]
        rows_small = _round_up(sum(-(-(w.shape[0] * w.shape[1]) // LANES) for w in w_small), 8)
        upd_small = _adamw(_pack_rows(w_small, rows_small), _pack_rows(g_small, rows_small),
                           _pack_rows(m_small, rows_small), _pack_rows(v_small, rows_small), name="adamw_small")
        return total[n_rows - 1, 0], g_small, upd_small

    rs = _GradReducer(w_big, m_big, v_big, long_name, core, chip_ix, as_views, small_step)
    grad_x = _local_step(
        x[0], loss_target[0], wa, wg, wao, wbi, wkv, wbo, a_gate_b, conv_w, conv_b, head_g,
        ln_g_a, ln_b_a, b_ln_g, b_ln_b, S=S, D=D, nha=nha, rs=rs)
    grad_x = rs.finish("w", rs.chip("w", grad_x))
    upd_big = [rs.results[k] for k in ("wa", "wao", "wkv", "wbi", "wbo")]
    g_big = [u[0] for u in upd_big]
    loss, g_small, upd_small = rs.small_results
    d_small, mn_small, vn_small = (_unpack_rows(u, [w.shape for w in w_small]) for u in upd_small)

    def assemble(big5, small8):
        awi, awo, kvw, bwi, bwo = big5
        gb, cw, cb, hg, alg, alb, blg, blb = small8
        return [awi[None], gb, cw[None], cb, hg, awo[None], alg, alb, kvw, bwi[None], bwo[None], blg, blb]

    grads = assemble(g_big, g_small)
    deltas = assemble([u[1] for u in upd_big], d_small)
    new_m = assemble([u[2] for u in upd_big], mn_small)
    new_v = assemble([u[3] for u in upd_big], vn_small)
    return (loss, grad_x[None], *grads, *deltas, *new_m, *new_v)
```

```python
import functools

import jax
import jax.numpy as jnp
from jax import lax
from jax.experimental import pallas as pl
from jax.experimental.pallas import tpu as pltpu
from jax.experimental.pallas import tpu_sc as plsc

F32 = jnp.float32
BF16 = jnp.bfloat16

DEPTH = 2
ALPHA = (2.0 * DEPTH) ** 0.25
LN_EPS = 1e-5
DK_A = 128
DV_A = 256
DH_B = 128
SB_TILE = 512
ML_TQ = 512
CONV_A = 4
ADAM_LR = 0.001
ADAM_B1 = 0.9
ADAM_B2 = 0.999
ADAM_EPS = 1e-08
ADAM_WD = 0.01
ADAM_STEP = 10
N_CHIPS = 4
LANES = 128
MXU_DEPTH = 256
V7X_VMEM_BYTES = 64 * 1024 * 1024
VMEM_LIMIT = (V7X_VMEM_BYTES * 3) // 4
NEG_BIG = -1e30
MESH = pl.DeviceIdType.MESH
ANY = pl.BlockSpec(memory_space=pl.ANY)


def _params(*sem):
    return pltpu.CompilerParams(dimension_semantics=sem, vmem_limit_bytes=VMEM_LIMIT)


def _dot(a, b, dims):
    return lax.dot_general(a, b, (dims, ((), ())), preferred_element_type=F32)


def _dot_nn(a, b):
    return _dot(a, b, ((1,), (0,)))


def _dot_nt(a, b):
    return _dot(a, b, ((1,), (1,)))


def _dot_tn(a, b):
    return _dot(a, b, ((0,), (0,)))


def _split2(x):
    hi = x.astype(BF16)
    lo = (x - hi.astype(F32)).astype(BF16)
    return hi, lo


def _split3(x):
    hi = x.astype(BF16)
    r = x - hi.astype(F32)
    mid = r.astype(BF16)
    lo = (r - mid.astype(F32)).astype(BF16)
    return hi, mid, lo


def _mask_dot2(x, t01):
    hi, lo = _split2(x)
    return _dot_nn(hi, t01) + _dot_nn(lo, t01)


def _block_sums(x, tri, later, passes=2):
    dot = _mask_dot2 if passes == 2 else (lambda v, t01: _dot_nn(v.astype(BF16), t01))
    sub = tri.shape[0]
    n = x.shape[1] // sub
    if n == 1:
        return dot(x, tri)
    parts = [x[:, b * sub:(b + 1) * sub] for b in range(n)]
    sums = [jnp.sum(p, axis=1, keepdims=True) for p in parts]
    out = []
    for b in range(n):
        acc = dot(parts[b], tri)
        for o in (range(b + 1, n) if later else range(b)):
            acc = acc + sums[o]
        out.append(acc)
    return jnp.concatenate(out, axis=1)


def _mask_dot3(x, t01):
    hi, mid, lo = _split3(x)
    return _dot_nn(hi, t01) + _dot_nn(mid, t01) + _dot_nn(lo, t01)


def _mask_dot3_left(t01, x):
    hi, mid, lo = _split3(x)
    return _dot_nn(t01, hi) + _dot_nn(t01, mid) + _dot_nn(t01, lo)


def _log_sigmoid(z):
    return jnp.minimum(z, 0.0) - jnp.log(1.0 + jnp.exp(-jnp.abs(z)))


def _sigmoid(z):
    return 1.0 / (1.0 + jnp.exp(-z))


def _mm(a, b, *, mode, M, N, K, out_dtype, name, tm=1024, tn=1024, tk=2048,
        a_off=(0, 0), b_off=(0, 0), acc_in=None, acc_scale=1.0, into=None, out_off=(0, 0), b_chunk=None, out_scale=None):
    tm, tn, tk = min(tm, M), min(tn, N), min(tk, K)
    if b_chunk is not None:
        tn, tk = (min(tn, b_chunk), tk) if mode == "nn" else (tn, min(tk, b_chunk))
        assert b.ndim == 3 and b_off == (0, 0) and mode in ("nn", "nt")
    assert M % tm == 0 and N % tn == 0 and K % tk == 0
    nk = K // tk
    ar, ac = a_off
    br, bc = b_off
    orow, ocol = out_off
    if mode in ("nn", "nt"):
        assert ar % tm == 0 and ac % tk == 0
        a_spec = pl.BlockSpec((tm, tk), lambda i, j, k: (i + ar // tm, k + ac // tk))
    else:
        assert ar % tk == 0 and ac % tm == 0
        a_spec = pl.BlockSpec((tk, tm), lambda i, j, k: (k + ar // tk, i + ac // tm))
    if b_chunk is not None and mode == "nn":
        per = b_chunk // tn
        b_spec = pl.BlockSpec((None, tk, tn), lambda i, j, k: (j // per, k, j % per))
    elif b_chunk is not None:
        per = b_chunk // tk
        b_spec = pl.BlockSpec((None, tn, tk), lambda i, j, k: (k // per, j, k % per))
    elif mode == "nt":
        assert br % tn == 0 and bc % tk == 0
        b_spec = pl.BlockSpec((tn, tk), lambda i, j, k: (j + br // tn, k + bc // tk))
    else:
        assert br % tk == 0 and bc % tn == 0
        b_spec = pl.BlockSpec((tk, tn), lambda i, j, k: (k + br // tk, j + bc // tn))
    assert orow % tm == 0 and ocol % tn == 0
    o_spec = pl.BlockSpec((tm, tn), lambda i, j, k: (i + orow // tm, j + ocol // tn))
    dims = {"nn": ((1,), (0,)), "nt": ((1,), (1,)), "tn": ((0,), (0,))}[mode]
    inputs, in_specs = [a, b], [a_spec, b_spec]
    has_acc = acc_in is not None
    if has_acc:
        inputs.append(acc_in)
        in_specs.append(pl.BlockSpec((tm, tn), lambda i, j, k: (i, j)))
    aliases = {}
    if into is not None:
        inputs.append(into)
        in_specs.append(ANY)
        aliases = {len(inputs) - 1: 0}
        out_shape = jax.ShapeDtypeStruct(into.shape, into.dtype)
        assert into.dtype == out_dtype
    else:
        out_shape = jax.ShapeDtypeStruct((M, N), out_dtype)

    def body(*refs):
        a_ref, b_ref = refs[0], refs[1]
        acc_in_ref = refs[2] if has_acc else None
        n_in = len(inputs)
        o_ref = refs[n_in]

        def first():
            if has_acc:
                return acc_scale * acc_in_ref[...]
            return None

        def scaled(r):
            return r if out_scale is None else r * out_scale

        if nk == 1:
            r = _dot(a_ref[...], b_ref[...], dims)
            f = first()
            if f is not None:
                r = r + f
            o_ref[...] = scaled(r).astype(o_ref.dtype)
        else:
            acc_ref = refs[n_in + 1]
            kk = pl.program_id(2)

            @pl.when(kk == 0)
            def _():
                f = first()
                acc_ref[...] = jnp.zeros_like(acc_ref) if f is None else f

            acc_ref[...] += _dot(a_ref[...], b_ref[...], dims)

            @pl.when(kk == nk - 1)
            def _():
                o_ref[...] = scaled(acc_ref[...]).astype(o_ref.dtype)

    scratch = [] if nk == 1 else [pltpu.VMEM((tm, tn), F32)]
    return pl.pallas_call(
        body, name=name, out_shape=out_shape, grid=(M // tm, N // tn, nk),
        in_specs=in_specs, out_specs=o_spec, scratch_shapes=scratch,
        input_output_aliases=aliases,
        compiler_params=_params("parallel", "parallel", "arbitrary"),
    )(*inputs)


def _join_chunk_edges(w, d):
    nch, rows, _ = w.shape

    def body(w_ref, o_ref, first, tail, sems):
        del w_ref
        for j in range(1, nch):
            mine = o_ref.at[j, :, pl.ds(0, LANES)]
            loads = [pltpu.make_async_copy(mine, first, sems.at[0]),
                     pltpu.make_async_copy(o_ref.at[j - 1, :, pl.ds(d, LANES)], tail, sems.at[1])]
            for cp in loads:
                cp.start()
            for cp in loads:
                cp.wait()
            first[...] = first[...] + tail[...]
            store = pltpu.make_async_copy(first, mine, sems.at[0])
            store.start()
            store.wait()

    return pl.pallas_call(
        body, name="join_chunk_edges", out_shape=jax.ShapeDtypeStruct(w.shape, w.dtype),
        in_specs=[ANY], out_specs=ANY, input_output_aliases={0: 0},
        scratch_shapes=[pltpu.VMEM((rows, LANES), w.dtype), pltpu.VMEM((rows, LANES), w.dtype),
                        pltpu.SemaphoreType.DMA((2,))],
    )(w)


def _shift_down(x, d, row):
    if d == 0:
        return x
    return jnp.where(row >= d, pltpu.roll(x, d, 0), 0.0)


def _shift_up(x, d, row, n):
    if d == 0:
        return x
    return jnp.where(row < n - d, pltpu.roll(x, n - d, 0), 0.0)


def _conv_pre(x, w_ref, b_ref, row):
    c = b_ref[...] + w_ref[CONV_A - 1:CONV_A, :] * x
    for k in range(CONV_A - 1):
        c = c + w_ref[k:k + 1, :] * _shift_down(x, CONV_A - 1 - k, row)
    return c


def _conv_fwd(u, conv_w, conv_b, *, S, D, name):
    tc = 256
    nq_blocks = (D // 2) // tc

    def body(u_ref, w_ref, b_ref, o_ref):
        x = u_ref[...]
        row = lax.broadcasted_iota(jnp.int32, x.shape, 0)
        c = _conv_pre(x, w_ref, b_ref, row)
        scale = jnp.where(pl.program_id(0) >= nq_blocks, DK_A ** -0.5, 1.0).astype(F32)
        o_ref[...] = (c * _sigmoid(c) * scale).astype(BF16)

    return pl.pallas_call(
        body, name=name, out_shape=jax.ShapeDtypeStruct((S, D), BF16), grid=(D // tc,),
        in_specs=[pl.BlockSpec((S, tc), lambda j: (0, j)),
                  pl.BlockSpec((CONV_A, tc), lambda j: (0, j)),
                  pl.BlockSpec((1, tc), lambda j: (0, j))],
        out_specs=pl.BlockSpec((S, tc), lambda j: (0, j)),
        compiler_params=_params("parallel"),
    )(u, conv_w, conv_b)


def _conv_bwd(u, dq, dk, conv_w, conv_b, du, *, S, D, name):
    tc = 256
    nq_blocks = (D // 2) // tc

    def body(u_ref, dq_ref, dk_ref, w_ref, b_ref, du_in, du_ref, dw_ref, db_ref):
        del du_in
        x = u_ref[...]
        n = x.shape[0]
        row = lax.broadcasted_iota(jnp.int32, x.shape, 0)
        c = _conv_pre(x, w_ref, b_ref, row)
        is_k = pl.program_id(0) >= nq_blocks
        dy = jnp.where(is_k, dk_ref[...] * (DK_A ** -0.5), dq_ref[...])
        sg = _sigmoid(c)
        dc = dy * (sg * (1.0 + c * (1.0 - sg)))
        db_ref[...] = jnp.sum(dc, axis=0, keepdims=True)
        dx = w_ref[CONV_A - 1:CONV_A, :] * dc
        dw_ref[CONV_A - 1:CONV_A, :] = jnp.sum(dc * x, axis=0, keepdims=True)
        for k in range(CONV_A - 1):
            d = CONV_A - 1 - k
            dw_ref[k:k + 1, :] = jnp.sum(dc * _shift_down(x, d, row), axis=0, keepdims=True)
            dx = dx + w_ref[k:k + 1, :] * _shift_up(dc, d, row, n)
        du_ref[...] = dx.astype(BF16)

    half = lambda j: (0, j % nq_blocks)
    return pl.pallas_call(
        body, name=name,
        out_shape=(jax.ShapeDtypeStruct(du.shape, du.dtype),
                   jax.ShapeDtypeStruct((CONV_A, D), F32), jax.ShapeDtypeStruct((1, D), F32)),
        grid=(D // tc,),
        in_specs=[pl.BlockSpec((S, tc), lambda j: (0, j)),
                  pl.BlockSpec((S, tc), half), pl.BlockSpec((S, tc), half),
                  pl.BlockSpec((CONV_A, tc), lambda j: (0, j)),
                  pl.BlockSpec((1, tc), lambda j: (0, j)), ANY],
        out_specs=(pl.BlockSpec((S, tc), lambda j: (0, j)),
                   pl.BlockSpec((CONV_A, tc), lambda j: (0, j)),
                   pl.BlockSpec((1, tc), lambda j: (0, j))),
        input_output_aliases={5: 0},
        compiler_params=_params("parallel"),
    )(u, dq, dk, conv_w, conv_b, du)


def _tri(n, cmp):
    r = lax.broadcasted_iota(jnp.int32, (n, n), 0)
    c = lax.broadcasted_iota(jnp.int32, (n, n), 1)
    return r, c, cmp(r, c)


def _gates_fwd(gt, bias, *, nha, nb, name):
    half = nha * nb
    nb_shift = nb.bit_length() - 1
    assert nb == 1 << nb_shift

    def body(g_ref, b_ref, f_ref, brow_ref):
        ig = g_ref[0:half, :] + b_ref[0:half, :]
        fg = g_ref[half:2 * half, :] + b_ref[half:2 * half, :]
        lf = _log_sigmoid(fg)
        _, _, upper = _tri(LANES, lambda r, c: r <= c)
        cs = _mask_dot3(lf, upper.astype(BF16))
        tot = jnp.broadcast_to(cs[:, LANES - 1:LANES], cs.shape)
        r, c, _ = _tri(half, lambda r, c: r <= c)
        before = jnp.logical_and(r >> nb_shift == c >> nb_shift, c < r).astype(BF16)
        f = cs + _mask_dot3_left(before, tot)
        f_ref[...] = f
        brow_ref[...] = ig - f

    return pl.pallas_call(
        body, name=name,
        out_shape=(jax.ShapeDtypeStruct((half, LANES), F32), jax.ShapeDtypeStruct((half, LANES), F32)),
    )(gt, bias)


def _gates_bwd(rowsum, colsum, gt, bias, *, nha, nb, name):
    half = nha * nb
    nb_shift = nb.bit_length() - 1
    assert nb == 1 << nb_shift

    def body(rs_ref, cs_ref, g_ref, b_ref, dg_ref, tot_ref):
        col = cs_ref[...]
        df = rs_ref[...] - col
        _, _, lower = _tri(LANES, lambda r, c: r >= c)
        rc = _mask_dot3(df, lower.astype(BF16))
        tot = jnp.broadcast_to(rc[:, 0:1], rc.shape)
        r, c, _ = _tri(half, lambda r, c: r <= c)
        same = r >> nb_shift == c >> nb_shift
        after = jnp.logical_and(same, c > r).astype(BF16)
        dlf = rc + _mask_dot3_left(after, tot)
        fg = g_ref[half:2 * half, :] + b_ref[half:2 * half, :]
        dfg = dlf * _sigmoid(-fg)
        dg_ref[0:half, :] = col
        dg_ref[half:2 * half, :] = dfg
        grp = same.astype(BF16)
        ones = jnp.ones((LANES, LANES), BF16)
        tot_ref[0:half, :] = _mask_dot3_left(grp, _mask_dot3(col, ones))
        tot_ref[half:2 * half, :] = _mask_dot3_left(grp, _mask_dot3(dfg, ones))

    return pl.pallas_call(
        body, name=name,
        out_shape=(jax.ShapeDtypeStruct((2 * half, LANES), F32), jax.ShapeDtypeStruct((2 * half, LANES), F32)),
    )(rowsum, colsum, gt, bias)


def _mlstm_tile(q, k_ref, fcol, brow_ref, j, tq, diagonal):
    off = pl.multiple_of(j * tq, tq)
    kj = k_ref[pl.ds(off, tq), :]
    s = _dot_nt(q, kj)
    logd = fcol + brow_ref[0, :, pl.ds(off, tq)]
    if diagonal:
        valid = lax.broadcasted_iota(jnp.int32, (tq, tq), 1) <= lax.broadcasted_iota(jnp.int32, (tq, tq), 0)
        logd = jnp.where(valid, logd, NEG_BIG)
    return off, kj, s, logd


def _mlstm_fwd(qk, u, fcol, brow, head_g, *, S, D, nha, name):
    tq = min(ML_TQ, S)
    nq = S // tq
    kb, vb = (D // 2) // DK_A, D // DV_A

    def body(q_ref, k_ref, v_ref, fcol_ref, brow_ref, o_ref, z_ref, g_ref, h_ref, hg_ref, m_ref, den_ref):
        i = pl.program_id(1)
        q = q_ref[...]
        fc = fcol_ref[0]

        def step(j, carry, diagonal):
            acc, den, m = carry
            off, _, s, logd = _mlstm_tile(q, k_ref, fc, brow_ref, j, tq, diagonal)
            m_new = jnp.maximum(m, jnp.max(logd, axis=1, keepdims=True))
            a = s * jnp.exp(logd - m_new)
            alpha = jnp.exp(m - m_new)
            vj = v_ref[pl.ds(off, tq), :].astype(BF16)
            acc = alpha * acc + _dot_nn(a.astype(BF16), vj)
            den = alpha * den + jnp.sum(a, axis=1, keepdims=True)
            return acc, den, m_new

        first = step(i, (jnp.zeros((tq, DV_A), F32), jnp.zeros((tq, 1), F32), jnp.full((tq, 1), NEG_BIG, F32)), True)
        acc, den, m = lax.fori_loop(0, i, lambda j, c: step(j, c, False), first)
        hh = acc / jnp.maximum(jnp.abs(den), jnp.exp(-m))
        h_ref[...] = hh
        hn, _ = _head_norm(hh)
        z = z_ref[...]
        hg_ref[...] = (_sigmoid(o_ref[...]) * (hn * g_ref[...]) * (z * _sigmoid(z))).astype(BF16)
        m_ref[0] = m
        den_ref[0] = den

    stat = pl.BlockSpec((1, tq, 1), lambda h, i: (h, i, 0))
    hblk = pl.BlockSpec((tq, DV_A), lambda h, i: (i, h))
    return pl.pallas_call(
        body, name=name,
        out_shape=(jax.ShapeDtypeStruct((S, D), F32), jax.ShapeDtypeStruct((S, D), BF16),
                   jax.ShapeDtypeStruct((nha, S, 1), F32), jax.ShapeDtypeStruct((nha, S, 1), F32)),
        grid=(nha, nq),
        in_specs=[pl.BlockSpec((tq, DK_A), lambda h, i: (i, h)),
                  pl.BlockSpec((S, DK_A), lambda h, i: (0, kb + h)),
                  pl.BlockSpec((S, DV_A), lambda h, i: (0, vb + h)),
                  stat, pl.BlockSpec((1, 1, S), lambda h, i: (h, 0, 0)),
                  pl.BlockSpec((tq, DV_A), lambda h, i: (i, 2 * vb + h)),
                  pl.BlockSpec((tq, DV_A), lambda h, i: (i, 3 * vb + h)),
                  pl.BlockSpec((1, DV_A), lambda h, i: (0, h))],
        out_specs=(hblk, hblk, stat, stat),
        compiler_params=_params("parallel", "arbitrary"),
    )(qk, qk, u, fcol, brow, u, u, head_g)


def _mlstm_bwd(qk, u, fcol, brow, m, den, dh, h, du, *, S, D, nha, name):
    tq = min(ML_TQ, S)
    nq = S // tq
    kb, vb = (D // 2) // DK_A, D // DV_A

    def body(q_ref, k_ref, v_ref, fcol_ref, brow_ref, m_ref, den_ref, dh_ref, h_ref, du_in,
             du_ref, dq_ref, dk_ref, rs_ref, cs_ref, dv_acc):
        del du_in
        i = pl.program_id(1)

        @pl.when(i == 0)
        def _():
            dk_ref[...] = jnp.zeros_like(dk_ref)
            cs_ref[...] = jnp.zeros_like(cs_ref)
            dv_acc[...] = jnp.zeros_like(dv_acc)

        q = q_ref[...]
        fc = fcol_ref[0]
        mm = m_ref[0]
        dn = den_ref[0]
        floor = jnp.exp(-mm)
        nrm = jnp.maximum(jnp.abs(dn), floor)
        dhv = dh_ref[...]
        dnum = dhv / nrm
        dnrm = -jnp.sum(dhv * h_ref[...], axis=1, keepdims=True) / nrm
        dden = jnp.where(jnp.abs(dn) > floor, jnp.where(dn > 0.0, dnrm, -dnrm), 0.0)
        dnum_b = dnum.astype(BF16)

        def step(j, carry, diagonal):
            dq, rs = carry
            off, kj, s, logd = _mlstm_tile(q, k_ref, fc, brow_ref, j, tq, diagonal)
            p = jnp.exp(logd - mm)
            a = s * p
            vj = v_ref[pl.ds(off, tq), :].astype(BF16)
            da = _dot_nt(dnum_b, vj) + dden
            dv_acc[pl.ds(off, tq), :] += _dot_tn(a.astype(BF16), dnum_b)
            dqk = (da * p).astype(BF16)
            dq = dq + _dot_nn(dqk, kj)
            dk_ref[pl.ds(off, tq), :] += _dot_tn(dqk, q)
            pm = da * a
            cs_ref[0, :, pl.ds(off, tq)] += jnp.sum(pm, axis=0, keepdims=True)
            rs = rs + jnp.sum(pm, axis=1, keepdims=True)
            return dq, rs

        first = step(i, (jnp.zeros((tq, DK_A), F32), jnp.zeros((tq, 1), F32)), True)
        dq, rs = lax.fori_loop(0, i, lambda j, c: step(j, c, False), first)
        dq_ref[...] = dq
        rs_ref[0] = rs

        @pl.when(i == nq - 1)
        def _():
            du_ref[...] = dv_acc[...].astype(BF16)

    stat = pl.BlockSpec((1, tq, 1), lambda h, i: (h, i, 0))
    rowv = pl.BlockSpec((1, 1, S), lambda h, i: (h, 0, 0))
    hblk = pl.BlockSpec((tq, DV_A), lambda h, i: (i, h))
    return pl.pallas_call(
        body, name=name,
        out_shape=(jax.ShapeDtypeStruct(du.shape, du.dtype),
                   jax.ShapeDtypeStruct((S, D // 2), F32), jax.ShapeDtypeStruct((S, D // 2), F32),
                   jax.ShapeDtypeStruct((nha, S, 1), F32), jax.ShapeDtypeStruct((nha, 1, S), F32)),
        grid=(nha, nq),
        in_specs=[pl.BlockSpec((tq, DK_A), lambda h, i: (i, h)),
                  pl.BlockSpec((S, DK_A), lambda h, i: (0, kb + h)),
                  pl.BlockSpec((S, DV_A), lambda h, i: (0, vb + h)),
                  stat, rowv, stat, stat, hblk, hblk, ANY],
        out_specs=(pl.BlockSpec((S, DV_A), lambda h, i: (0, vb + h)),
                   pl.BlockSpec((tq, DK_A), lambda h, i: (i, h)),
                   pl.BlockSpec((S, DK_A), lambda h, i: (0, h)),
                   stat, rowv),
        scratch_shapes=[pltpu.VMEM((S, DV_A), F32)],
        input_output_aliases={9: 0},
        compiler_params=_params("parallel", "arbitrary"),
    )(qk, qk, u, fcol, brow, m, den, dh, h, du)


def _head_norm(hh):
    mu = jnp.mean(hh, axis=1, keepdims=True)
    hc = hh - mu
    rstd = lax.rsqrt(jnp.mean(hc * hc, axis=1, keepdims=True) + LN_EPS)
    return hc * rstd, rstd


def _hgate_bwd(dhg, h, u, head_g, du, *, S, D, name):
    tm = min(256, S)
    nh = D // DV_A

    def body(dhg_ref, h_ref, o_ref, z_ref, g_ref, du_in, du_ref, dh_ref, dg_ref):
        del du_in

        @pl.when(pl.program_id(0) == 0)
        def _():
            dg_ref[...] = jnp.zeros_like(dg_ref)

        for hd in range(nh):
            sl = slice(hd * DV_A, (hd + 1) * DV_A)
            hn, rstd = _head_norm(h_ref[:, sl])
            o, z, g, d = o_ref[:, sl], z_ref[:, sl], g_ref[:, sl], dhg_ref[:, sl]
            so, sz = _sigmoid(o), _sigmoid(z)
            silu_z = z * sz
            hng = hn * g
            du_ref[:, sl] = (d * hng * silu_z * so * (1.0 - so)).astype(BF16)
            du_ref[:, D + hd * DV_A:D + (hd + 1) * DV_A] = (
                d * so * hng * (sz * (1.0 + z * (1.0 - sz)))).astype(BF16)
            t = d * so * silu_z
            dg_ref[:, sl] += jnp.sum(t * hn, axis=0, keepdims=True)
            dhn = t * g
            dh_ref[:, sl] = rstd * (dhn - jnp.mean(dhn, axis=1, keepdims=True)
                                    - hn * jnp.mean(dhn * hn, axis=1, keepdims=True))

    row = lambda i: (i, 0)
    return pl.pallas_call(
        body, name=name,
        out_shape=(jax.ShapeDtypeStruct(du.shape, du.dtype), jax.ShapeDtypeStruct((S, D), F32),
                   jax.ShapeDtypeStruct((1, D), F32)),
        grid=(S // tm,),
        in_specs=[pl.BlockSpec((tm, D), row), pl.BlockSpec((tm, D), row),
                  pl.BlockSpec((tm, D), lambda i: (i, 2)), pl.BlockSpec((tm, D), lambda i: (i, 3)),
                  pl.BlockSpec((1, D), lambda i: (0, 0)), ANY],
        out_specs=(pl.BlockSpec((tm, 2 * D), lambda i: (i, 1)), pl.BlockSpec((tm, D), row),
                   pl.BlockSpec((1, D), lambda i: (0, 0))),
        input_output_aliases={5: 0},
        compiler_params=_params("arbitrary"),
    )(dhg, h, u, u, head_g, du)


def _ln_stats(r):
    mu = jnp.mean(r, axis=1, keepdims=True)
    xc = r - mu
    rstd = lax.rsqrt(jnp.mean(xc * xc, axis=1, keepdims=True) + LN_EPS)
    return xc * rstd, rstd


def _ln_back(dxhat, xhat, rstd):
    return rstd * (dxhat - jnp.mean(dxhat, axis=1, keepdims=True)
                   - xhat * jnp.mean(dxhat * xhat, axis=1, keepdims=True))


def _ln_fwd(x, y, g, b, *, S, D, name):
    tm = min(256, S)

    def body(x_ref, y_ref, g_ref, b_ref, o_ref, ob_ref):
        xhat, _ = _ln_stats(ALPHA * x_ref[...] + y_ref[...])
        o = xhat * g_ref[...] + b_ref[...]
        o_ref[...] = o
        ob_ref[...] = o.astype(BF16)

    row = lambda i: (i, 0)
    vec = pl.BlockSpec((1, D), lambda i: (0, 0))
    return pl.pallas_call(
        body, name=name,
        out_shape=(jax.ShapeDtypeStruct((S, D), F32), jax.ShapeDtypeStruct((S, D), BF16)),
        grid=(S // tm,),
        in_specs=[pl.BlockSpec((tm, D), row), pl.BlockSpec((tm, D), row), vec, vec],
        out_specs=(pl.BlockSpec((tm, D), row), pl.BlockSpec((tm, D), row)),
        compiler_params=_params("parallel"),
    )(x, y, g, b)


def _ln_loss_bwd(x1, y2, target, g, b, *, S, D, name):
    tm = min(256, S)

    def body(x_ref, y_ref, t_ref, g_ref, b_ref, dr_ref, drb_ref, dg_ref, db_ref, loss_ref):
        @pl.when(pl.program_id(0) == 0)
        def _():
            dg_ref[...] = jnp.zeros_like(dg_ref)
            db_ref[...] = jnp.zeros_like(db_ref)
            loss_ref[...] = jnp.zeros_like(loss_ref)

        xhat, rstd = _ln_stats(ALPHA * x_ref[...] + y_ref[...])
        diff = xhat * g_ref[...] + b_ref[...] - t_ref[...]
        loss_ref[...] += (0.5 / D) * jnp.sum(diff * diff)
        dx2 = diff * (1.0 / D)
        dg_ref[...] += jnp.sum(dx2 * xhat, axis=0, keepdims=True)
        db_ref[...] += jnp.sum(dx2, axis=0, keepdims=True)
        dr = _ln_back(dx2 * g_ref[...], xhat, rstd)
        dr_ref[...] = dr
        drb_ref[...] = dr.astype(BF16)

    row = lambda i: (i, 0)
    vec = pl.BlockSpec((1, D), lambda i: (0, 0))
    return pl.pallas_call(
        body, name=name,
        out_shape=(jax.ShapeDtypeStruct((S, D), F32), jax.ShapeDtypeStruct((S, D), BF16),
                   jax.ShapeDtypeStruct((1, D), F32), jax.ShapeDtypeStruct((1, D), F32),
                   jax.ShapeDtypeStruct((1, LANES), F32)),
        grid=(S // tm,),
        in_specs=[pl.BlockSpec((tm, D), row)] * 3 + [vec, vec],
        out_specs=(pl.BlockSpec((tm, D), row), pl.BlockSpec((tm, D), row), vec, vec,
                   pl.BlockSpec((1, LANES), lambda i: (0, 0))),
        compiler_params=_params("arbitrary"),
    )(x1, y2, target, g, b)


def _ln_bwd(x, y, g, dout, *, S, D, name):
    tm = min(256, S)

    def body(x_ref, y_ref, g_ref, d_ref, dr_ref, drb_ref, dg_ref, db_ref):
        @pl.when(pl.program_id(0) == 0)
        def _():
            dg_ref[...] = jnp.zeros_like(dg_ref)
            db_ref[...] = jnp.zeros_like(db_ref)

        xhat, rstd = _ln_stats(ALPHA * x_ref[...] + y_ref[...])
        d = d_ref[...]
        dg_ref[...] += jnp.sum(d * xhat, axis=0, keepdims=True)
        db_ref[...] += jnp.sum(d, axis=0, keepdims=True)
        dr = _ln_back(d * g_ref[...], xhat, rstd)
        dr_ref[...] = dr
        drb_ref[...] = dr.astype(BF16)

    row = lambda i: (i, 0)
    vec = pl.BlockSpec((1, D), lambda i: (0, 0))
    return pl.pallas_call(
        body, name=name,
        out_shape=(jax.ShapeDtypeStruct((S, D), F32), jax.ShapeDtypeStruct((S, D), BF16),
                   jax.ShapeDtypeStruct((1, D), F32), jax.ShapeDtypeStruct((1, D), F32)),
        grid=(S // tm,),
        in_specs=[pl.BlockSpec((tm, D), row), pl.BlockSpec((tm, D), row), vec, pl.BlockSpec((tm, D), row)],
        out_specs=(pl.BlockSpec((tm, D), row), pl.BlockSpec((tm, D), row), vec, vec),
        compiler_params=_params("arbitrary"),
    )(x, y, g, dout)


def _sb_scores(q, kj, diagonal, t):
    z = _dot_nt(q, kj)
    ls = _log_sigmoid(z)
    if not diagonal:
        return None, ls, ls - z
    valid = lax.broadcasted_iota(jnp.int32, (t, t), 1) < lax.broadcasted_iota(jnp.int32, (t, t), 0)
    return valid, ls, jnp.where(valid, ls - z, 0.0)


def _keep(valid, x):
    return x if valid is None else jnp.where(valid, x, 0.0)


def _sb_fwd(q2, kv, z2, *, S, D, name, t=SB_TILE):
    t = min(t, S)
    nq = S // t
    nh = D // DH_B

    def body(q_ref, k_ref, v_ref, z_ref, o_ref, hb_ref, tot_ref):
        i = pl.program_id(1)
        q = q_ref[...]
        _, _, after = _tri(min(t, MXU_DEPTH), lambda r, c: r > c)
        tri_after = after.astype(BF16)

        def step(j, carry, diagonal):
            acc, cr = carry
            off = pl.multiple_of(j * t, t)
            valid, ls, lneg = _sb_scores(q, k_ref[pl.ds(off, t), :], diagonal, t)
            between = cr + _block_sums(lneg, tri_after, True)
            a = _keep(valid, jnp.exp(ls + between))
            acc = acc + _dot_nn(a.astype(BF16), v_ref[pl.ds(off, t), :])
            return acc, cr + jnp.sum(lneg, axis=1, keepdims=True)

        carry = step(i, (jnp.zeros((t, DH_B), F32), jnp.zeros((t, 1), F32)), True)
        acc, cr = lax.fori_loop(0, i, lambda jj, c: step(i - 1 - jj, c, False), carry)
        o_ref[...] = acc
        z = z_ref[...]
        hb_ref[...] = (acc * (z * _sigmoid(z))).astype(BF16)
        tot_ref[0] = cr

    blk = pl.BlockSpec((t, DH_B), lambda h, i: (i, h))
    return pl.pallas_call(
        body, name=name,
        out_shape=(jax.ShapeDtypeStruct((S, D), F32), jax.ShapeDtypeStruct((S, D), BF16),
                   jax.ShapeDtypeStruct((nh, S, 1), F32)), grid=(nh, nq),
        in_specs=[blk, pl.BlockSpec((S, DH_B), lambda h, i: (0, h)),
                  pl.BlockSpec((S, DH_B), lambda h, i: (0, nh + h)), blk],
        out_specs=(blk, blk, pl.BlockSpec((1, t, 1), lambda h, i: (h, i, 0))),
        compiler_params=_params("parallel", "arbitrary"),
    )(q2, kv, kv, z2)


def _sb_bwd(q2, kv, dhb, att, z2, tot, *, S, D, name, t=SB_TILE):
    t = min(t, S)
    nq = S // t
    nh = D // DH_B

    def body(q_ref, k_ref, v_ref, dhb_ref, att_ref, z_ref, tot_ref, dq_ref, dk_ref, dv_ref, dz2_ref, dk_acc, dv_acc):
        i = pl.program_id(1)

        @pl.when(i == 0)
        def _():
            dk_acc[...] = jnp.zeros_like(dk_acc)
            dv_acc[...] = jnp.zeros_like(dv_acc)

        q = q_ref[...]
        z, dhb = z_ref[...], dhb_ref[...]
        sz = _sigmoid(z)
        dz2_ref[...] = (dhb * att_ref[...] * (sz * (1.0 + z * (1.0 - sz)))).astype(BF16)
        do_b = (dhb * (z * sz)).astype(BF16)
        _, _, after = _tri(min(t, MXU_DEPTH), lambda r, c: r > c)
        tri_after = after.astype(BF16)
        _, _, before = _tri(min(t, MXU_DEPTH), lambda r, c: r < c)
        tri_before = before.astype(BF16)

        def step(j, carry, diagonal):
            dq, rest, cg = carry
            off = pl.multiple_of(j * t, t)
            kj = k_ref[pl.ds(off, t), :]
            valid, ls, lneg = _sb_scores(q, kj, diagonal, t)
            rest = rest - jnp.sum(lneg, axis=1, keepdims=True)
            between = rest + _block_sums(lneg, tri_after, True)
            a = _keep(valid, jnp.exp(ls + between))
            g = _dot_nt(do_b, v_ref[pl.ds(off, t), :]) * a
            dv_acc[pl.ds(off, t), :] += _dot_tn(a.astype(BF16), do_b)
            e = cg + _block_sums(g, tri_before, False, passes=1)
            sig = jnp.exp(ls)
            dz = _keep(valid, g * (1.0 - sig) - e * sig)
            dz_b = dz.astype(BF16)
            dq = dq + _dot_nn(dz_b, kj)
            dk_acc[pl.ds(off, t), :] += _dot_tn(dz_b, q)
            return dq, rest, cg + jnp.sum(g, axis=1, keepdims=True)

        carry = lax.fori_loop(0, i, lambda j, c: step(j, c, False),
                              (jnp.zeros((t, DH_B), F32), tot_ref[0], jnp.zeros((t, 1), F32)))
        dq, _, _ = step(i, carry, True)
        dq_ref[...] = (dq * (DH_B ** -0.5)).astype(BF16)

        @pl.when(i == nq - 1)
        def _():
            dk_ref[...] = dk_acc[...].astype(BF16)
            dv_ref[...] = dv_acc[...].astype(BF16)

    blk = pl.BlockSpec((t, DH_B), lambda h, i: (i, h))
    sd = jax.ShapeDtypeStruct((S, D), BF16)
    return pl.pallas_call(
        body, name=name, out_shape=(sd, sd, sd, sd), grid=(nh, nq),
        in_specs=[blk, pl.BlockSpec((S, DH_B), lambda h, i: (0, h)),
                  pl.BlockSpec((S, DH_B), lambda h, i: (0, nh + h)), blk, blk, blk,
                  pl.BlockSpec((1, t, 1), lambda h, i: (h, i, 0))],
        out_specs=(blk, pl.BlockSpec((S, DH_B), lambda h, i: (0, h)),
                   pl.BlockSpec((S, DH_B), lambda h, i: (0, h)), blk),
        scratch_shapes=[pltpu.VMEM((S, DH_B), F32), pltpu.VMEM((S, DH_B), F32)],
        compiler_params=_params("parallel", "arbitrary"),
    )(q2, kv, kv, dhb, att, z2, tot)


def _adamw_math(w, g, m, v):
    mn = ADAM_B1 * m + (1.0 - ADAM_B1) * g
    vn = ADAM_B2 * v + (1.0 - ADAM_B2) * (g * g)
    m_hat = mn / (1.0 - ADAM_B1 ** ADAM_STEP)
    v_hat = vn / (1.0 - ADAM_B2 ** ADAM_STEP)
    return -ADAM_LR * (m_hat / (jnp.sqrt(v_hat) + ADAM_EPS) + ADAM_WD * w), mn, vn


def _adamw(w, g, m, v, *, name, tr=128):
    rows, cols = w.shape
    tr = min(tr, rows)
    assert rows % tr == 0

    def body(w_ref, g_ref, m_ref, v_ref, d_ref, mo_ref, vo_ref):
        d, mn, vn = _adamw_math(w_ref[...], g_ref[...], m_ref[...], v_ref[...])
        d_ref[...] = d
        mo_ref[...] = mn
        vo_ref[...] = vn

    blk = pl.BlockSpec((tr, cols), lambda i: (i, 0))
    sd = jax.ShapeDtypeStruct((rows, cols), F32)
    return pl.pallas_call(
        body, name=name, out_shape=(sd, sd, sd), grid=(rows // tr,),
        in_specs=[blk] * 4, out_specs=(blk, blk, blk),
        compiler_params=_params("parallel"),
    )(w, g, m, v)


def _local_step(x, target, wa, wg, wao, wbi, wkv, wbo, gate_b, conv_w, conv_b, head_g,
                a_ln_g, a_ln_b, b_ln_g, b_ln_b, *, S, D, nha, rs):
    nb = S // LANES
    kw = dict(S=S, D=D)
    xb = x.astype(BF16)
    wa_chunk = D if wa.ndim == 3 else None
    u = _mm(xb, wa, mode="nn", M=S, N=4 * D, K=D, out_dtype=F32, name="a_in", b_chunk=wa_chunk)
    ug = _mm(xb, wg, mode="nn", M=S, N=LANES, K=D, out_dtype=F32, name="a_in_gates")
    qk = _conv_fwd(u, conv_w, conv_b, name="conv_fwd", **kw)
    gt = ug[:, :2 * nha].T.reshape(2 * nha * nb, LANES)
    gbias = jnp.repeat(gate_b.reshape(2 * nha), nb).reshape(2 * nha * nb, 1)
    fcs, brow = _gates_fwd(gt, gbias, nha=nha, nb=nb, name="gates_fwd")
    fcol = fcs.reshape(nha, S, 1)
    brow = brow.reshape(nha, 1, S)
    h, hg, m, den = _mlstm_fwd(qk, u, fcol, brow, head_g, nha=nha, name="mlstm_fwd", **kw)
    y = _mm(hg, wao, mode="nn", M=S, N=D, K=D, out_dtype=F32, name="a_out")
    x1, x1b = _ln_fwd(x, y, a_ln_g, a_ln_b, name="ln_a_fwd", **kw)
    q2 = _mm(x1b, wbi, mode="nn", M=S, N=D, K=D, out_dtype=BF16, name="b_in_q", out_scale=DH_B ** -0.5)
    z2 = _mm(x1b, wbi, mode="nn", M=S, N=D, K=D, out_dtype=F32, name="b_in_z", b_off=(0, D))
    kv = _mm(x1b, wkv, mode="nn", M=S, N=2 * D, K=D, out_dtype=BF16, name="b_kv")
    att, hb, sb_tot = _sb_fwd(q2, kv, z2, name="sb_fwd", **kw)
    y2 = _mm(hb, wbo, mode="nn", M=S, N=D, K=D, out_dtype=F32, name="b_out")
    dr2, dr2b, d_bln_g, d_bln_b, loss = _ln_loss_bwd(x1, y2, target, b_ln_g, b_ln_b, name="ln_b_loss", **kw)
    g_wbo = _mm(hb, dr2b, mode="tn", M=D, N=D, K=S, out_dtype=BF16, name="g_b_out")
    dhb = _mm(dr2b, wbo, mode="nt", M=S, N=D, K=D, out_dtype=F32, name="d_b_out")
    dq2, dk2, dv2, dz2 = _sb_bwd(q2, kv, dhb, att, z2, sb_tot, name="sb_bwd", **kw)
    g_wbi = _mm(x1b, dq2, mode="tn", M=D, N=D, K=S, out_dtype=BF16, name="g_b_in_q",
                into=lax.empty((D, 2 * D), BF16))
    g_wbi = _mm(x1b, dz2, mode="tn", M=D, N=D, K=S, out_dtype=BF16, name="g_b_in_z", into=g_wbi, out_off=(0, D))
    g_wkv = _mm(x1b, dk2, mode="tn", M=D, N=D, K=S, out_dtype=BF16, name="g_kv_k",
                into=lax.empty((D, 2 * D), BF16))
    g_wkv = _mm(x1b, dv2, mode="tn", M=D, N=D, K=S, out_dtype=BF16, name="g_kv_v", into=g_wkv, out_off=(0, D))
    dq2 = rs.begin("b", dict(wbo=g_wbo, wbi=g_wbi, wkv=g_wkv), dq2)
    dx1 = _mm(dq2, wbi, mode="nt", M=S, N=D, K=D, out_dtype=F32, name="d_b_in_q", acc_in=dr2, acc_scale=ALPHA)
    dx1 = _mm(dz2, wbi, mode="nt", M=S, N=D, K=D, out_dtype=F32, name="d_b_in_z", b_off=(0, D), acc_in=dx1)
    dx1 = _mm(dk2, wkv, mode="nt", M=S, N=D, K=D, out_dtype=F32, name="d_kv_k", acc_in=dx1)
    dx1 = _mm(dv2, wkv, mode="nt", M=S, N=D, K=D, out_dtype=F32, name="d_kv_v", b_off=(0, D), acc_in=dx1)
    dx1 = rs.pair("b", dx1)
    dr, drb, d_aln_g, d_aln_b = _ln_bwd(x, y, a_ln_g, dx1, name="ln_a_bwd", **kw)
    g_wao = _mm(hg, drb, mode="tn", M=D, N=D, K=S, out_dtype=BF16, name="g_a_out")
    drb = rs.begin("a", dict(wao=g_wao), drb)
    dhg = _mm(drb, wao, mode="nt", M=S, N=D, K=D, out_dtype=F32, name="d_a_out")
    du = lax.empty((S, 4 * D), BF16)
    du, dh, d_head_g = _hgate_bwd(dhg, h, u, head_g, du, name="hgate_bwd", **kw)
    dh = rs.pair("a", dh)
    du, dq, dk, rowsum, colsum = _mlstm_bwd(qk, u, fcol, brow, m, den, dh, h, du, nha=nha, name="mlstm_bwd", **kw)
    dgt, dgtot = _gates_bwd(rowsum.reshape(nha * nb, LANES), colsum.reshape(nha * nb, LANES), gt, gbias,
                            nha=nha, nb=nb, name="gates_bwd")
    d_gate_b = dgtot[::nb, 0].reshape(1, 2 * nha)
    dgp = jnp.pad(dgt.reshape(2 * nha, S).T, ((0, 0), (0, LANES - 2 * nha))).astype(BF16)
    du, d_conv_w, d_conv_b = _conv_bwd(u, dq, dk, conv_w, conv_b, du, name="conv_bwd", **kw)
    small = dict(gate_b=d_gate_b, conv_w=d_conv_w, conv_b=d_conv_b, head_g=d_head_g,
                 a_ln_g=d_aln_g, a_ln_b=d_aln_b, b_ln_g=d_bln_g, b_ln_b=d_bln_b)
    du = rs.chip("a", rs.chip("b", du))
    g_wa = _mm(xb, du, mode="tn", M=D, N=4 * D, K=S, out_dtype=BF16, name="g_a_in")
    g_wg = _mm(xb, dgp, mode="tn", M=D, N=LANES, K=S, out_dtype=BF16, name="g_a_in_gates")
    du = rs.begin("w", dict(wa=g_wa, wg=g_wg), du)
    du = rs.pair("w", rs.small(small, loss, du))
    du = rs.finish("a", rs.finish("b", du))
    dx = _mm(du, wa, mode="nt", M=S, N=D, K=4 * D, out_dtype=F32, name="d_a_in", acc_in=dr, acc_scale=ALPHA,
             b_chunk=wa_chunk)
    dx = _mm(dgp, wg, mode="nt", M=S, N=D, K=LANES, out_dtype=F32, name="d_a_in_gates", acc_in=dx)
    return dx


def _coords():
    return lax.axis_index("x"), lax.axis_index("y"), lax.axis_index("c")


def _other_chips(x, y):
    return [(1 - x, y), (x, 1 - y), (1 - x, 1 - y)]


def _rows(ref, start, size):
    return ref.at[pl.ds(pl.multiple_of(start, size), size), :]


def _window(kind, ref, shard_shape, j, hf=None, quarter=None):
    r, cw = shard_shape
    row0, nr = (0, r) if hf is None else (hf * (r // 2), r // 2)
    if quarter is not None:
        row0, nr = row0 + quarter * (r // 4), r // 4
    if kind == "stack":
        return ref.at[j, pl.ds(pl.multiple_of(row0, nr), nr), :]
    if kind == "rows":
        return ref.at[pl.ds(pl.multiple_of(j * r + row0, nr), nr), :]
    assert kind == "cols"
    return ref.at[pl.ds(pl.multiple_of(row0, nr), nr), pl.ds(pl.multiple_of(j * cw, cw), cw)]


def _gathered_shape(kind, shard_shape):
    r, cw = shard_shape
    return {"stack": (N_CHIPS, r, cw), "rows": (N_CHIPS * r, cw), "cols": (r, N_CHIPS * cw)}[kind]


def _remote(src, dst, send_sems, recv_sems, k, to):
    return pltpu.make_async_remote_copy(src_ref=src, dst_ref=dst, send_sem=send_sems.at[k],
                                        recv_sem=recv_sems.at[k], device_id=to, device_id_type=MESH)


def _allgather_copies(s_refs, sm_ref, g_refs, smg_ref, send_sems, recv_sems, kinds, shapes):
    n = len(s_refs)
    x, y, c = _coords()
    me, sibling, chips = 2 * x + y, (x, y, 1 - c), _other_chips(x, y)
    ids = [2 * cx + cy for cx, cy in chips]
    d2d, own_base, small_base = 4 * n, 7 * n, 8 * n

    def win(a, j, hf=None, quarter=None):
        return _window(kinds[a], g_refs[a], shapes[a], j, hf, quarter)

    def small_win(j):
        w = sm_ref.shape[1]
        return smg_ref.at[:, pl.ds(pl.multiple_of(j * w, w), w)]

    def rc(src, dst, k, to):
        return _remote(src, dst, send_sems, recv_sems, k, to)

    first = [rc(s_refs[a], win(a, me), own_base + a, sibling) for a in range(n)]
    for a in range(n):
        src = _rows(s_refs[a], c * (shapes[a][0] // 2), shapes[a][0] // 2)
        first += [rc(src, win(a, me, c), 4 * a + k, (*chips[k], c)) for k in range(2)]
    if sm_ref is not None:
        first.append(rc(sm_ref, small_win(me), small_base + 3, sibling))
        first += [rc(sm_ref, small_win(me), small_base + k, (*chip, c)) for k, chip in enumerate(chips)]
    for cp in first:
        cp.start()
    passed = []
    for a in range(n):
        for k in range(2):
            rc(win(a, ids[k], c), win(a, ids[k], c), 4 * a + k, sibling).wait_recv()
            part = win(a, ids[k], c, quarter=k)
            passed.append(rc(part, part, 4 * a + 2 + k, (*chips[1 - k], c)))
            passed.append(rc(win(a, ids[k], c), win(a, ids[k], c), d2d + 3 * a + k, sibling))
            passed[-2].start()
            passed[-1].start()
    for a in range(n):
        for k in range(2):
            part = win(a, ids[2], c, quarter=k)
            rc(part, part, 4 * a + 2 + k, sibling).wait_recv()
        passed.append(rc(win(a, ids[2], c), win(a, ids[2], c), d2d + 3 * a + 2, sibling))
        passed[-1].start()
    if sm_ref is not None:
        for k in range(3):
            rc(small_win(ids[k]), small_win(ids[k]), small_base + k, sibling).wait_recv()
        rc(small_win(me), small_win(me), small_base + 3, sibling).wait_recv()
    for a in range(n):
        for k in range(3):
            rc(win(a, ids[k], 1 - c), win(a, ids[k], 1 - c), d2d + 3 * a + k, sibling).wait_recv()
        rc(win(a, me), win(a, me), own_base + a, sibling).wait_recv()
    for cp in first + passed:
        cp.wait_send()


def _peer_handshake(peers):
    barrier = pltpu.get_barrier_semaphore()
    for peer in peers:
        pl.semaphore_signal(barrier, inc=1, device_id=peer, device_id_type=MESH)
    pl.semaphore_wait(barrier, len(peers))


def _behind(copies, inputs, out_structs, nsem, peers, *, name, collective_id):
    hbm = pltpu.MemorySpace.HBM
    in_refs = [jax.new_ref(a, memory_space=hbm) for a in inputs]
    out_refs = [jax.empty_ref(st, memory_space=hbm) for st in out_structs]

    @pl.kernel(mesh=plsc.ScalarSubcoreMesh(axis_name="sequencer", num_cores=1), name=name,
               scratch_types=(pltpu.SemaphoreType.DMA((nsem,)), pltpu.SemaphoreType.DMA((nsem,))),
               compiler_params=pltpu.CompilerParams(collective_id=collective_id))
    def launch(send_sems, recv_sems):
        _peer_handshake(peers(*_coords()))
        copies(in_refs, out_refs, send_sems, recv_sems)

    launch()
    return [r[...] for r in out_refs]


def _sibling(x, y, c):
    return [(x, y, 1 - c)]


def _same_core_of_other_chips(x, y, c):
    return [(cx, cy, c) for cx, cy in _other_chips(x, y)]


def _sibling_and_other_chips(x, y, c):
    return _sibling(x, y, c) + _same_core_of_other_chips(x, y, c)


ID_ALL_PEERS, ID_SIBLING, ID_CHIPS = 0, 1, 2


def _allgather_behind(shards, kinds, small=None, *, name):
    n = len(shards)
    shapes = [sh.shape for sh in shards]

    def copies(in_refs, out_refs, send_sems, recv_sems):
        if small is None:
            _allgather_copies(in_refs, None, out_refs, None, send_sems, recv_sems, kinds, shapes)
        else:
            _allgather_copies(in_refs[:n], in_refs[n], out_refs[:n], out_refs[n], send_sems, recv_sems, kinds, shapes)

    outs = [jax.ShapeDtypeStruct(_gathered_shape(kinds[a], shapes[a]), shards[a].dtype) for a in range(n)]
    inputs = list(shards)
    if small is not None:
        outs.append(jax.ShapeDtypeStruct((small.shape[0], N_CHIPS * small.shape[1]), small.dtype))
        inputs.append(small)
    return _behind(copies, inputs, outs, 8 * n + 4, _sibling_and_other_chips, name=name, collective_id=ID_ALL_PEERS)


def _rs_pair_exchange(views, *, name):
    n = len(views)

    def copies(g_refs, p_refs, send_sems, recv_sems):
        x, y, c = _coords()
        cps = []
        for a in range(n):
            r2 = views[a].shape[1] // 2
            src = g_refs[a].at[:, pl.ds(pl.multiple_of((1 - c) * r2, r2), r2), :]
            cps.append(_remote(src, p_refs[a], send_sems, recv_sems, a, (x, y, 1 - c)))
        for cp in cps:
            cp.start()
        for cp in cps:
            cp.wait()

    outs = [jax.ShapeDtypeStruct((v.shape[0], v.shape[1] // 2, v.shape[2]), v.dtype) for v in views]
    return _behind(copies, views, outs, n, _sibling, name=name, collective_id=ID_SIBLING)


def _add_half(view, part, core, *, name):
    nch, r, cols = view.shape
    r2 = r // 2
    tr = min(128, r2)
    nt = r2 // tr

    def body(c_ref, g_ref, p_ref, o_ref):
        del c_ref
        o_ref[...] = (g_ref[...].astype(F32) + p_ref[...].astype(F32)).astype(o_ref.dtype)

    return pl.pallas_call(
        body, name=name, out_shape=jax.ShapeDtypeStruct((nch, r2, cols), view.dtype),
        grid_spec=pltpu.PrefetchScalarGridSpec(
            num_scalar_prefetch=1, grid=(nch, nt),
            in_specs=[pl.BlockSpec((1, tr, cols), lambda ch, i, c_ref: (ch, c_ref[0] * nt + i, 0)),
                      pl.BlockSpec((1, tr, cols), lambda ch, i, c_ref: (ch, i, 0))],
            out_specs=pl.BlockSpec((1, tr, cols), lambda ch, i, c_ref: (ch, i, 0))),
        compiler_params=_params("parallel", "parallel"),
    )(core, view, part)


def _chunk(kind, ref, j, cw):
    if kind == "cols":
        return ref.at[0, :, pl.ds(pl.multiple_of(j * cw, cw), cw)]
    return ref.at[j]


def _rs_chips(pairs, kinds, *, name):
    n = len(pairs)
    half_shapes = []
    for a in range(n):
        nch, r2, cols = pairs[a].shape
        half_shapes.append((r2, cols // N_CHIPS) if kinds[a] == "cols" else (r2, cols))

    def copies(q_refs, t_refs, send_sems, recv_sems):
        x, y, c = _coords()
        chips = _other_chips(x, y)
        sends = []
        for a in range(n):
            for k, (cx, cy) in enumerate(chips):
                src = _chunk(kinds[a], q_refs[a], 2 * cx + cy, half_shapes[a][1])
                sends.append(_remote(src, t_refs[a].at[k], send_sems, recv_sems, 3 * a + k, (cx, cy, c)))
        for cp in sends:
            cp.start()
        for a in range(n):
            for k in range(3):
                slot = t_refs[a].at[k]
                _remote(slot, slot, send_sems, recv_sems, 3 * a + k, (x, y, c)).wait_recv()
        for cp in sends:
            cp.wait_send()

    outs = [jax.ShapeDtypeStruct((3, *half_shapes[a]), pairs[a].dtype) for a in range(n)]
    return _behind(copies, pairs, outs, 3 * n, _same_core_of_other_chips, name=name, collective_id=ID_CHIPS)


def _sum_chips(pair, parts, chip, kind, *, name):
    _, r, cols = parts.shape
    tr = min(128, r)
    if kind == "cols":
        own_spec = pl.BlockSpec((1, tr, cols), lambda i, chip_ref: (0, i, chip_ref[0]))
    else:
        own_spec = pl.BlockSpec((1, tr, cols), lambda i, chip_ref: (chip_ref[0], i, 0))

    def body(chip_ref, q_ref, p_ref, o_ref):
        del chip_ref
        acc = q_ref[0].astype(F32)
        for s in range(3):
            acc = acc + p_ref[s].astype(F32)
        o_ref[...] = acc

    return pl.pallas_call(
        body, name=name, out_shape=jax.ShapeDtypeStruct((r, cols), F32),
        grid_spec=pltpu.PrefetchScalarGridSpec(
            num_scalar_prefetch=1, grid=(r // tr,),
            in_specs=[own_spec, pl.BlockSpec((3, tr, cols), lambda i, chip_ref: (0, i, 0))],
            out_specs=pl.BlockSpec((tr, cols), lambda i, chip_ref: (i, 0))),
        compiler_params=_params("parallel"),
    )(chip, pair, parts)


def _rs_share(halves, *, name):
    n = len(halves)

    def copies(h_refs, g_refs, send_sems, recv_sems):
        x, y, c = _coords()
        cps = [_remote(h_refs[a], g_refs[a], send_sems, recv_sems, a, (x, y, 1 - c)) for a in range(n)]
        for cp in cps:
            cp.start()
        for cp in cps:
            cp.wait()

    outs = [jax.ShapeDtypeStruct(h.shape, h.dtype) for h in halves]
    return _behind(copies, halves, outs, n, _sibling, name=name, collective_id=ID_SIBLING)


class _GradReducer:
    def __init__(self, w, m, v, long_name, core, chip_ix, as_views, small_step):
        self.w, self.m, self.v, self.long_name = w, m, v, long_name
        self.core, self.chip_ix, self.as_views, self.small_step = core, chip_ix, as_views, small_step
        self.groups, self.results, self.small_results = {}, {}, None

    def small(self, small, loss_row, tie):
        (small, loss_row), tie = lax.optimization_barrier(((small, loss_row), tie))
        self.small_results, tie = lax.optimization_barrier((self.small_step(small, loss_row), tie))
        return tie

    def begin(self, tag, grads, tie):
        keys, views, kinds = self.as_views(grads)
        views, tie = lax.optimization_barrier((views, tie))
        parts = _rs_pair_exchange(views, name="rs_pair_exchange_" + tag)
        self.groups[tag] = dict(keys=keys, views=views, kinds=kinds, parts=parts)
        return tie

    def pair(self, tag, tie):
        g = self.groups[tag]
        g["parts"], tie = lax.optimization_barrier((g["parts"], tie))
        pairs = [_add_half(v, p, self.core, name="pair_sum_" + self.long_name[k])
                 for v, p, k in zip(g["views"], g["parts"], g["keys"])]
        pairs, tie = lax.optimization_barrier((pairs, tie))
        g["pairs"] = pairs
        g["slots"] = _rs_chips(pairs, g["kinds"], name="rs_chips_" + tag)
        return tie

    def chip(self, tag, tie):
        g = self.groups[tag]
        g["slots"], tie = lax.optimization_barrier((g["slots"], tie))
        halves = [_sum_chips(q, t, self.chip_ix, kd, name="chip_sum_" + self.long_name[k])
                  for q, t, kd, k in zip(g["pairs"], g["slots"], g["kinds"], g["keys"])]
        halves, tie = lax.optimization_barrier((halves, tie))
        g["halves"] = halves
        g["others"] = _rs_share(halves, name="rs_share_" + tag)
        return tie

    def finish(self, tag, tie):
        g = self.groups[tag]
        g["others"], tie = lax.optimization_barrier((g["others"], tie))
        out = [_adamw_halves(self.w[k], a, b, self.m[k], self.v[k], self.core, name="adamw_" + self.long_name[k])
               for k, a, b in zip(g["keys"], g["halves"], g["others"])]
        out, tie = lax.optimization_barrier((out, tie))
        self.results.update(zip(g["keys"], out))
        return tie


def _adamw_halves(w, mine, theirs, m, v, core, *, name, tr=128):
    rows, cols = w.shape
    hr = rows // 2
    tr = min(tr, hr)
    nt = hr // tr

    def body(c_ref, w_ref, a_ref, b_ref, m_ref, v_ref, g_ref, d_ref, mo_ref, vo_ref):
        gg = jnp.where(pl.program_id(0) == c_ref[0], a_ref[...], b_ref[...])
        g_ref[...] = gg
        d, mn, vn = _adamw_math(w_ref[...], gg, m_ref[...], v_ref[...])
        d_ref[...] = d
        mo_ref[...] = mn
        vo_ref[...] = vn

    full = pl.BlockSpec((tr, cols), lambda hf, i, c_ref: (hf * nt + i, 0))
    half = pl.BlockSpec((tr, cols), lambda hf, i, c_ref: (i, 0))
    sd = jax.ShapeDtypeStruct((rows, cols), F32)
    return pl.pallas_call(
        body, name=name, out_shape=(sd, sd, sd, sd),
        grid_spec=pltpu.PrefetchScalarGridSpec(
            num_scalar_prefetch=1, grid=(2, nt),
            in_specs=[full, half, half, full, full], out_specs=(full, full, full, full)),
        compiler_params=_params("parallel", "parallel"),
    )(core, w, mine, theirs, m, v)


def _allreduce_small(vec):
    m_per, ncol = vec.shape
    n_dev = 2 * N_CHIPS

    def body(x_ref, out_ref, sum_ref, send_sems, recv_sems, local_sem):
        x, y, c = _coords()
        me, sibling = (x, y, c), (x, y, 1 - c)
        chips = _other_chips(x, y)

        def rows(px, py, pc):
            return out_ref.at[pl.ds(pl.multiple_of((4 * px + 2 * py + pc) * m_per, m_per), m_per), :]

        def copy(k, block, to, src=None):
            return _remote(rows(*block) if src is None else src, rows(*block), send_sems, recv_sems, k, to)

        mine = pltpu.make_async_copy(x_ref, rows(*me), local_sem)
        mine.start()
        first = [copy(0, me, sibling, src=x_ref)]
        first += [copy(1 + j, me, (*chip, c), src=x_ref) for j, chip in enumerate(chips)]
        for cp in first:
            cp.start()
        passed = [copy(4 + j, (*chip, c), sibling) for j, chip in enumerate(chips)]
        for j, chip in enumerate(chips):
            copy(1 + j, (*chip, c), me).wait_recv()
            passed[j].start()
        copy(0, sibling, me).wait_recv()
        for j, chip in enumerate(chips):
            copy(4 + j, (*chip, 1 - c), me).wait_recv()
        for cp in first + passed:
            cp.wait_send()
        mine.wait()
        acc = out_ref[0:m_per, :]
        for d in range(1, n_dev):
            acc = acc + out_ref[d * m_per:(d + 1) * m_per, :]
        sum_ref[...] = acc

    vm = pl.BlockSpec(memory_space=pltpu.VMEM)
    return pl.pallas_call(
        body, name="allreduce_small",
        out_shape=(jax.ShapeDtypeStruct((n_dev * m_per, ncol), vec.dtype), jax.ShapeDtypeStruct((m_per, ncol), vec.dtype)),
        in_specs=[vm], out_specs=(vm, vm),
        scratch_shapes=[pltpu.SemaphoreType.DMA((7,)), pltpu.SemaphoreType.DMA((7,)), pltpu.SemaphoreType.DMA],
    )(vec)[1]


def _pack_rows(pieces, total_rows):
    rows = []
    for p in pieces:
        flat = p.reshape(-1)
        flat = jnp.pad(flat, (0, (-flat.shape[0]) % LANES))
        rows.append(flat.reshape(-1, LANES))
    out = jnp.concatenate(rows, axis=0)
    return jnp.pad(out, ((0, total_rows - out.shape[0]), (0, 0)))


def _unpack_rows(packed, shapes):
    out, r = [], 0
    for shp in shapes:
        size = 1
        for d in shp:
            size *= d
        nr = -(-size // LANES)
        out.append(packed[r:r + nr].reshape(-1)[:size].reshape(shp))
        r += nr
    return out


def _round_up(v, m):
    return -(-v // m) * m


def kernel(x, a_w_in, a_gate_b, a_conv_w, a_conv_b, a_head_g, a_w_out, a_ln_g, a_ln_b, kv_w, b_w_in, b_w_out, b_ln_g, b_ln_b, loss_target, m_a_w_in, m_a_gate_b, m_a_conv_w, m_a_conv_b, m_a_head_g, m_a_w_out, m_a_ln_g, m_a_ln_b, m_kv_w, m_b_w_in, m_b_w_out, m_b_ln_g, m_b_ln_b, v_a_w_in, v_a_gate_b, v_a_conv_w, v_a_conv_b, v_a_head_g, v_a_w_out, v_a_ln_g, v_a_ln_b, v_kv_w, v_b_w_in, v_b_w_out, v_b_ln_g, v_b_ln_b):
    _, S, D = x.shape
    nha = a_gate_b.shape[1] // 2
    chip = 2 * lax.axis_index("x") + lax.axis_index("y")
    core = lax.axis_index("c").astype(jnp.int32).reshape(1)
    dq = D // N_CHIPS

    ca = a_w_in.shape[2]
    edge = (N_CHIPS - 1) * (ca - D)
    wide = _round_up(ca + edge, LANES)
    shifted = lax.dynamic_slice_in_dim(
        jnp.pad(a_w_in[0].astype(BF16), ((0, 0), (edge, wide - ca))), edge - chip * (ca - D), wide, axis=1)
    shards = [shifted, a_w_out[0].astype(BF16), kv_w.astype(BF16), b_w_in[0].astype(BF16), b_w_out[0].astype(BF16)]
    kinds = ["stack", "rows", "cols", "cols", "rows"]
    small_shard = jnp.concatenate([a_conv_w[0], a_conv_b, a_head_g, a_ln_g, a_ln_b], axis=0)
    wa_g, small_full = _allgather_behind(shards[:1], kinds[:1], small_shard, name="allgather_first")
    wao, = _allgather_behind(shards[1:2], kinds[1:2], name="allgather_a_w_out")
    wkv, wbi = _allgather_behind(shards[2:4], kinds[2:4], name="allgather_b_in")
    wbo, = _allgather_behind(shards[4:], kinds[4:], name="allgather_b_w_out")
    wa = _join_chunk_edges(wa_g, D)
    wg = wa[N_CHIPS - 1, :, D:D + LANES]
    conv_w, conv_b, head_g, ln_g_a, ln_b_a = (small_full[0:CONV_A], small_full[4:5], small_full[5:6],
                                              small_full[6:7], small_full[7:8])

    chip_ix = chip.astype(jnp.int32).reshape(1)
    (w_wa, m_wa, v_wa), x = lax.optimization_barrier(((a_w_in[0], m_a_w_in[0], v_a_w_in[0]), x))
    w_big = dict(wa=w_wa, wao=a_w_out[0], wkv=kv_w, wbi=b_w_in[0], wbo=b_w_out[0])
    m_big = dict(wa=m_wa, wao=m_a_w_out[0], wkv=m_kv_w, wbi=m_b_w_in[0], wbo=m_b_w_out[0])
    v_big = dict(wa=v_wa, wao=v_a_w_out[0], wkv=v_kv_w, wbi=v_b_w_in[0], wbo=v_b_w_out[0])
    long_name = dict(wa="a_w_in", wao="a_w_out", wkv="kv_w", wbi="b_w_in", wbo="b_w_out")

    def as_views(g):
        if "wa" in g:
            g_full = jnp.concatenate([g["wa"], g["wg"][:, :2 * nha]], axis=1)
            ca = g_full.shape[1] // N_CHIPS
            return ["wa"], [jnp.stack([g_full[:, j * ca:(j + 1) * ca] for j in range(N_CHIPS)], axis=0)], ["stack"]
        keys = list(g)
        views = [g[k].reshape(N_CHIPS, dq, D) if k in ("wao", "wbo") else g[k][None] for k in keys]
        return keys, views, ["stack" if k in ("wao", "wbo") else "cols" for k in keys]

    w_small = [a_gate_b, a_conv_w[0], a_conv_b, a_head_g, a_ln_g, a_ln_b, b_ln_g, b_ln_b]
    m_small = [m_a_gate_b, m_a_conv_w[0], m_a_conv_b, m_a_head_g, m_a_ln_g, m_a_ln_b, m_b_ln_g, m_b_ln_b]
    v_small = [v_a_gate_b, v_a_conv_w[0], v_a_conv_b, v_a_head_g, v_a_ln_g, v_a_ln_b, v_b_ln_g, v_b_ln_b]

    def small_step(small, loss_row):
        order = ["conv_w", "conv_b", "head_g", "a_ln_g", "a_ln_b", "b_ln_g", "b_ln_b", "gate_b"]
        full_shapes = [(CONV_A, D), (1, D), (1, D), (1, D), (1, D), (1, D), (1, D), (1, 2 * nha)]
        n_rows = sum(-(-(s[0] * s[1]) // LANES) for s in full_shapes) + 1
        total = _allreduce_small(_pack_rows([small[k] for k in order] + [loss_row], _round_up(n_rows, 8)))
        sums = dict(zip(order, _unpack_rows(total, full_shapes)))

        def mine(v):
            return lax.dynamic_slice_in_dim(v, chip * dq, dq, axis=1)

        g_small = [sums["gate_b"], mine(sums["conv_w"]), mine(sums["conv_b"]), mine(sums["head_g"]),
                   mine(sums["a_ln_g"]), mine(sums["a_ln_b"]), sums["b_ln_g"], sums["b_ln_b"]]
        rows_small = _round_up(sum(-(-(w.shape[0] * w.shape[1]) // LANES) for w in w_small), 8)
        upd_small = _adamw(_pack_rows(w_small, rows_small), _pack_rows(g_small, rows_small),
                           _pack_rows(m_small, rows_small), _pack_rows(v_small, rows_small), name="adamw_small")
        return total[n_rows - 1, 0], g_small, upd_small

    rs = _GradReducer(w_big, m_big, v_big, long_name, core, chip_ix, as_views, small_step)
    grad_x = _local_step(
        x[0], loss_target[0], wa, wg, wao, wbi, wkv, wbo, a_gate_b, conv_w, conv_b, head_g,
        ln_g_a, ln_b_a, b_ln_g, b_ln_b, S=S, D=D, nha=nha, rs=rs)
    grad_x = rs.finish("w", rs.chip("w", grad_x))
    upd_big = [rs.results[k] for k in ("wa", "wao", "wkv", "wbi", "wbo")]
    g_big = [u[0] for u in upd_big]
    loss, g_small, upd_small = rs.small_results
    d_small, mn_small, vn_small = (_unpack_rows(u, [w.shape for w in w_small]) for u in upd_small)

    def assemble(big5, small8):
        awi, awo, kvw, bwi, bwo = big5
        gb, cw, cb, hg, alg, alb, blg, blb = small8
        return [awi[None], gb, cw[None], cb, hg, awo[None], alg, alb, kvw, bwi[None], bwo[None], blg, blb]

    grads = assemble(g_big, g_small)
    deltas = assemble([u[1] for u in upd_big], d_small)
    new_m = assemble([u[2] for u in upd_big], mn_small)
    new_v = assemble([u[3] for u in upd_big], vn_small)
    return (loss, grad_x[None], *grads, *deltas, *new_m, *new_v)
```

```python
import functools

import jax
import jax.numpy as jnp
from jax import lax
from jax.experimental import pallas as pl
from jax.experimental.pallas import tpu as pltpu
from jax.experimental.pallas import tpu_sc as plsc

F32 = jnp.float32
BF16 = jnp.bfloat16

DEPTH = 2
ALPHA = (2.0 * DEPTH) ** 0.25
LN_EPS = 1e-5
DK_A = 128
DV_A = 256
DH_B = 128
SB_TILE = 512
ML_TQ = 512
CONV_A = 4
ADAM_LR = 0.001
ADAM_B1 = 0.9
ADAM_B2 = 0.999
ADAM_EPS = 1e-08
ADAM_WD = 0.01
ADAM_STEP = 10
N_CHIPS = 4
LANES = 128
MXU_DEPTH = 256
V7X_VMEM_BYTES = 64 * 1024 * 1024
VMEM_LIMIT = (V7X_VMEM_BYTES * 3) // 4
NEG_BIG = -1e30
MESH = pl.DeviceIdType.MESH
ANY = pl.BlockSpec(memory_space=pl.ANY)


def _params(*sem):
    return pltpu.CompilerParams(dimension_semantics=sem, vmem_limit_bytes=VMEM_LIMIT)


def _dot(a, b, dims):
    return lax.dot_general(a, b, (dims, ((), ())), preferred_element_type=F32)


def _dot_nn(a, b):
    return _dot(a, b, ((1,), (0,)))


def _dot_nt(a, b):
    return _dot(a, b, ((1,), (1,)))


def _dot_tn(a, b):
    return _dot(a, b, ((0,), (0,)))


def _split2(x):
    hi = x.astype(BF16)
    lo = (x - hi.astype(F32)).astype(BF16)
    return hi, lo


def _split3(x):
    hi = x.astype(BF16)
    r = x - hi.astype(F32)
    mid = r.astype(BF16)
    lo = (r - mid.astype(F32)).astype(BF16)
    return hi, mid, lo


def _mask_dot2(x, t01):
    hi, lo = _split2(x)
    return _dot_nn(hi, t01) + _dot_nn(lo, t01)


def _block_sums(x, tri, later, passes=2):
    dot = _mask_dot2 if passes == 2 else (lambda v, t01: _dot_nn(v.astype(BF16), t01))
    sub = tri.shape[0]
    n = x.shape[1] // sub
    if n == 1:
        return dot(x, tri)
    parts = [x[:, b * sub:(b + 1) * sub] for b in range(n)]
    sums = [jnp.sum(p, axis=1, keepdims=True) for p in parts]
    out = []
    for b in range(n):
        acc = dot(parts[b], tri)
        for o in (range(b + 1, n) if later else range(b)):
            acc = acc + sums[o]
        out.append(acc)
    return jnp.concatenate(out, axis=1)


def _mask_dot3(x, t01):
    hi, mid, lo = _split3(x)
    return _dot_nn(hi, t01) + _dot_nn(mid, t01) + _dot_nn(lo, t01)


def _mask_dot3_left(t01, x):
    hi, mid, lo = _split3(x)
    return _dot_nn(t01, hi) + _dot_nn(t01, mid) + _dot_nn(t01, lo)


def _log_sigmoid(z):
    return jnp.minimum(z, 0.0) - jnp.log(1.0 + jnp.exp(-jnp.abs(z)))


def _sigmoid(z):
    return 1.0 / (1.0 + jnp.exp(-z))


def _mm(a, b, *, mode, M, N, K, out_dtype, name, tm=1024, tn=1024, tk=2048,
        a_off=(0, 0), b_off=(0, 0), acc_in=None, acc_scale=1.0, into=None, out_off=(0, 0), b_chunk=None, out_scale=None):
    tm, tn, tk = min(tm, M), min(tn, N), min(tk, K)
    if b_chunk is not None:
        tn, tk = (min(tn, b_chunk), tk) if mode == "nn" else (tn, min(tk, b_chunk))
        assert b.ndim == 3 and b_off == (0, 0) and mode in ("nn", "nt")
    assert M % tm == 0 and N % tn == 0 and K % tk == 0
    nk = K // tk
    ar, ac = a_off
    br, bc = b_off
    orow, ocol = out_off
    if mode in ("nn", "nt"):
        assert ar % tm == 0 and ac % tk == 0
        a_spec = pl.BlockSpec((tm, tk), lambda i, j, k: (i + ar // tm, k + ac // tk))
    else:
        assert ar % tk == 0 and ac % tm == 0
        a_spec = pl.BlockSpec((tk, tm), lambda i, j, k: (k + ar // tk, i + ac // tm))
    if b_chunk is not None and mode == "nn":
        per = b_chunk // tn
        b_spec = pl.BlockSpec((None, tk, tn), lambda i, j, k: (j // per, k, j % per))
    elif b_chunk is not None:
        per = b_chunk // tk
        b_spec = pl.BlockSpec((None, tn, tk), lambda i, j, k: (k // per, j, k % per))
    elif mode == "nt":
        assert br % tn == 0 and bc % tk == 0
        b_spec = pl.BlockSpec((tn, tk), lambda i, j, k: (j + br // tn, k + bc // tk))
    else:
        assert br % tk == 0 and bc % tn == 0
        b_spec = pl.BlockSpec((tk, tn), lambda i, j, k: (k + br // tk, j + bc // tn))
    assert orow % tm == 0 and ocol % tn == 0
    o_spec = pl.BlockSpec((tm, tn), lambda i, j, k: (i + orow // tm, j + ocol // tn))
    dims = {"nn": ((1,), (0,)), "nt": ((1,), (1,)), "tn": ((0,), (0,))}[mode]
    inputs, in_specs = [a, b], [a_spec, b_spec]
    has_acc = acc_in is not None
    if has_acc:
        inputs.append(acc_in)
        in_specs.append(pl.BlockSpec((tm, tn), lambda i, j, k: (i, j)))
    aliases = {}
    if into is not None:
        inputs.append(into)
        in_specs.append(ANY)
        aliases = {len(inputs) - 1: 0}
        out_shape = jax.ShapeDtypeStruct(into.shape, into.dtype)
        assert into.dtype == out_dtype
    else:
        out_shape = jax.ShapeDtypeStruct((M, N), out_dtype)

    def body(*refs):
        a_ref, b_ref = refs[0], refs[1]
        acc_in_ref = refs[2] if has_acc else None
        n_in = len(inputs)
        o_ref = refs[n_in]

        def first():
            if has_acc:
                return acc_scale * acc_in_ref[...]
            return None

        def scaled(r):
            return r if out_scale is None else r * out_scale

        if nk == 1:
            r = _dot(a_ref[...], b_ref[...], dims)
            f = first()
            if f is not None:
                r = r + f
            o_ref[...] = scaled(r).astype(o_ref.dtype)
        else:
            acc_ref = refs[n_in + 1]
            kk = pl.program_id(2)

            @pl.when(kk == 0)
            def _():
                f = first()
                acc_ref[...] = jnp.zeros_like(acc_ref) if f is None else f

            acc_ref[...] += _dot(a_ref[...], b_ref[...], dims)

            @pl.when(kk == nk - 1)
            def _():
                o_ref[...] = scaled(acc_ref[...]).astype(o_ref.dtype)

    scratch = [] if nk == 1 else [pltpu.VMEM((tm, tn), F32)]
    return pl.pallas_call(
        body, name=name, out_shape=out_shape, grid=(M // tm, N // tn, nk),
        in_specs=in_specs, out_specs=o_spec, scratch_shapes=scratch,
        input_output_aliases=aliases,
        compiler_params=_params("parallel", "parallel", "arbitrary"),
    )(*inputs)


def _join_chunk_edges(w, d):
    nch, rows, _ = w.shape

    def body(w_ref, o_ref, first, tail, sems):
        del w_ref
        for j in range(1, nch):
            mine = o_ref.at[j, :, pl.ds(0, LANES)]
            loads = [pltpu.make_async_copy(mine, first, sems.at[0]),
                     pltpu.make_async_copy(o_ref.at[j - 1, :, pl.ds(d, LANES)], tail, sems.at[1])]
            for cp in loads:
                cp.start()
            for cp in loads:
                cp.wait()
            first[...] = first[...] + tail[...]
            store = pltpu.make_async_copy(first, mine, sems.at[0])
            store.start()
            store.wait()

    return pl.pallas_call(
        body, name="join_chunk_edges", out_shape=jax.ShapeDtypeStruct(w.shape, w.dtype),
        in_specs=[ANY], out_specs=ANY, input_output_aliases={0: 0},
        scratch_shapes=[pltpu.VMEM((rows, LANES), w.dtype), pltpu.VMEM((rows, LANES), w.dtype),
                        pltpu.SemaphoreType.DMA((2,))],
    )(w)


def _shift_down(x, d, row):
    if d == 0:
        return x
    return jnp.where(row >= d, pltpu.roll(x, d, 0), 0.0)


def _shift_up(x, d, row, n):
    if d == 0:
        return x
    return jnp.where(row < n - d, pltpu.roll(x, n - d, 0), 0.0)


def _conv_pre(x, w_ref, b_ref, row):
    c = b_ref[...] + w_ref[CONV_A - 1:CONV_A, :] * x
    for k in range(CONV_A - 1):
        c = c + w_ref[k:k + 1, :] * _shift_down(x, CONV_A - 1 - k, row)
    return c


def _conv_fwd(u, conv_w, conv_b, *, S, D, name):
    tc = 256
    nq_blocks = (D // 2) // tc

    def body(u_ref, w_ref, b_ref, o_ref):
        x = u_ref[...]
        row = lax.broadcasted_iota(jnp.int32, x.shape, 0)
        c = _conv_pre(x, w_ref, b_ref, row)
        scale = jnp.where(pl.program_id(0) >= nq_blocks, DK_A ** -0.5, 1.0).astype(F32)
        o_ref[...] = (c * _sigmoid(c) * scale).astype(BF16)

    return pl.pallas_call(
        body, name=name, out_shape=jax.ShapeDtypeStruct((S, D), BF16), grid=(D // tc,),
        in_specs=[pl.BlockSpec((S, tc), lambda j: (0, j)),
                  pl.BlockSpec((CONV_A, tc), lambda j: (0, j)),
                  pl.BlockSpec((1, tc), lambda j: (0, j))],
        out_specs=pl.BlockSpec((S, tc), lambda j: (0, j)),
        compiler_params=_params("parallel"),
    )(u, conv_w, conv_b)


def _conv_bwd(u, dq, dk, conv_w, conv_b, du, *, S, D, name):
    tc = 256
    nq_blocks = (D // 2) // tc

    def body(u_ref, dq_ref, dk_ref, w_ref, b_ref, du_in, du_ref, dw_ref, db_ref):
        del du_in
        x = u_ref[...]
        n = x.shape[0]
        row = lax.broadcasted_iota(jnp.int32, x.shape, 0)
        c = _conv_pre(x, w_ref, b_ref, row)
        is_k = pl.program_id(0) >= nq_blocks
        dy = jnp.where(is_k, dk_ref[...] * (DK_A ** -0.5), dq_ref[...])
        sg = _sigmoid(c)
        dc = dy * (sg * (1.0 + c * (1.0 - sg)))
        db_ref[...] = jnp.sum(dc, axis=0, keepdims=True)
        dx = w_ref[CONV_A - 1:CONV_A, :] * dc
        dw_ref[CONV_A - 1:CONV_A, :] = jnp.sum(dc * x, axis=0, keepdims=True)
        for k in range(CONV_A - 1):
            d = CONV_A - 1 - k
            dw_ref[k:k + 1, :] = jnp.sum(dc * _shift_down(x, d, row), axis=0, keepdims=True)
            dx = dx + w_ref[k:k + 1, :] * _shift_up(dc, d, row, n)
        du_ref[...] = dx.astype(BF16)

    half = lambda j: (0, j % nq_blocks)
    return pl.pallas_call(
        body, name=name,
        out_shape=(jax.ShapeDtypeStruct(du.shape, du.dtype),
                   jax.ShapeDtypeStruct((CONV_A, D), F32), jax.ShapeDtypeStruct((1, D), F32)),
        grid=(D // tc,),
        in_specs=[pl.BlockSpec((S, tc), lambda j: (0, j)),
                  pl.BlockSpec((S, tc), half), pl.BlockSpec((S, tc), half),
                  pl.BlockSpec((CONV_A, tc), lambda j: (0, j)),
                  pl.BlockSpec((1, tc), lambda j: (0, j)), ANY],
        out_specs=(pl.BlockSpec((S, tc), lambda j: (0, j)),
                   pl.BlockSpec((CONV_A, tc), lambda j: (0, j)),
                   pl.BlockSpec((1, tc), lambda j: (0, j))),
        input_output_aliases={5: 0},
        compiler_params=_params("parallel"),
    )(u, dq, dk, conv_w, conv_b, du)


def _tri(n, cmp):
    r = lax.broadcasted_iota(jnp.int32, (n, n), 0)
    c = lax.broadcasted_iota(jnp.int32, (n, n), 1)
    return r, c, cmp(r, c)


def _gates_fwd(gt, bias, *, nha, nb, name):
    half = nha * nb
    nb_shift = nb.bit_length() - 1
    assert nb == 1 << nb_shift

    def body(g_ref, b_ref, f_ref, brow_ref):
        ig = g_ref[0:half, :] + b_ref[0:half, :]
        fg = g_ref[half:2 * half, :] + b_ref[half:2 * half, :]
        lf = _log_sigmoid(fg)
        _, _, upper = _tri(LANES, lambda r, c: r <= c)
        cs = _mask_dot3(lf, upper.astype(BF16))
        tot = jnp.broadcast_to(cs[:, LANES - 1:LANES], cs.shape)
        r, c, _ = _tri(half, lambda r, c: r <= c)
        before = jnp.logical_and(r >> nb_shift == c >> nb_shift, c < r).astype(BF16)
        f = cs + _mask_dot3_left(before, tot)
        f_ref[...] = f
        brow_ref[...] = ig - f

    return pl.pallas_call(
        body, name=name,
        out_shape=(jax.ShapeDtypeStruct((half, LANES), F32), jax.ShapeDtypeStruct((half, LANES), F32)),
    )(gt, bias)


def _gates_bwd(rowsum, colsum, gt, bias, *, nha, nb, name):
    half = nha * nb
    nb_shift = nb.bit_length() - 1
    assert nb == 1 << nb_shift

    def body(rs_ref, cs_ref, g_ref, b_ref, dg_ref, tot_ref):
        col = cs_ref[...]
        df = rs_ref[...] - col
        _, _, lower = _tri(LANES, lambda r, c: r >= c)
        rc = _mask_dot3(df, lower.astype(BF16))
        tot = jnp.broadcast_to(rc[:, 0:1], rc.shape)
        r, c, _ = _tri(half, lambda r, c: r <= c)
        same = r >> nb_shift == c >> nb_shift
        after = jnp.logical_and(same, c > r).astype(BF16)
        dlf = rc + _mask_dot3_left(after, tot)
        fg = g_ref[half:2 * half, :] + b_ref[half:2 * half, :]
        dfg = dlf * _sigmoid(-fg)
        dg_ref[0:half, :] = col
        dg_ref[half:2 * half, :] = dfg
        grp = same.astype(BF16)
        ones = jnp.ones((LANES, LANES), BF16)
        tot_ref[0:half, :] = _mask_dot3_left(grp, _mask_dot3(col, ones))
        tot_ref[half:2 * half, :] = _mask_dot3_left(grp, _mask_dot3(dfg, ones))

    return pl.pallas_call(
        body, name=name,
        out_shape=(jax.ShapeDtypeStruct((2 * half, LANES), F32), jax.ShapeDtypeStruct((2 * half, LANES), F32)),
    )(rowsum, colsum, gt, bias)


def _mlstm_tile(q, k_ref, fcol, brow_ref, j, tq, diagonal):
    off = pl.multiple_of(j * tq, tq)
    kj = k_ref[pl.ds(off, tq), :]
    s = _dot_nt(q, kj)
    logd = fcol + brow_ref[0, :, pl.ds(off, tq)]
    if diagonal:
        valid = lax.broadcasted_iota(jnp.int32, (tq, tq), 1) <= lax.broadcasted_iota(jnp.int32, (tq, tq), 0)
        logd = jnp.where(valid, logd, NEG_BIG)
    return off, kj, s, logd


def _mlstm_fwd(qk, u, fcol, brow, head_g, *, S, D, nha, name):
    tq = min(ML_TQ, S)
    nq = S // tq
    kb, vb = (D // 2) // DK_A, D // DV_A

    def body(q_ref, k_ref, v_ref, fcol_ref, brow_ref, o_ref, z_ref, g_ref, h_ref, hg_ref, m_ref, den_ref):
        i = pl.program_id(1)
        q = q_ref[...]
        fc = fcol_ref[0]

        def step(j, carry, diagonal):
            acc, den, m = carry
            off, _, s, logd = _mlstm_tile(q, k_ref, fc, brow_ref, j, tq, diagonal)
            m_new = jnp.maximum(m, jnp.max(logd, axis=1, keepdims=True))
            a = s * jnp.exp(logd - m_new)
            alpha = jnp.exp(m - m_new)
            vj = v_ref[pl.ds(off, tq), :].astype(BF16)
            acc = alpha * acc + _dot_nn(a.astype(BF16), vj)
            den = alpha * den + jnp.sum(a, axis=1, keepdims=True)
            return acc, den, m_new

        first = step(i, (jnp.zeros((tq, DV_A), F32), jnp.zeros((tq, 1), F32), jnp.full((tq, 1), NEG_BIG, F32)), True)
        acc, den, m = lax.fori_loop(0, i, lambda j, c: step(j, c, False), first)
        hh = acc / jnp.maximum(jnp.abs(den), jnp.exp(-m))
        h_ref[...] = hh
        hn, _ = _head_norm(hh)
        z = z_ref[...]
        hg_ref[...] = (_sigmoid(o_ref[...]) * (hn * g_ref[...]) * (z * _sigmoid(z))).astype(BF16)
        m_ref[0] = m
        den_ref[0] = den

    stat = pl.BlockSpec((1, tq, 1), lambda h, i: (h, i, 0))
    hblk = pl.BlockSpec((tq, DV_A), lambda h, i: (i, h))
    return pl.pallas_call(
        body, name=name,
        out_shape=(jax.ShapeDtypeStruct((S, D), F32), jax.ShapeDtypeStruct((S, D), BF16),
                   jax.ShapeDtypeStruct((nha, S, 1), F32), jax.ShapeDtypeStruct((nha, S, 1), F32)),
        grid=(nha, nq),
        in_specs=[pl.BlockSpec((tq, DK_A), lambda h, i: (i, h)),
                  pl.BlockSpec((S, DK_A), lambda h, i: (0, kb + h)),
                  pl.BlockSpec((S, DV_A), lambda h, i: (0, vb + h)),
                  stat, pl.BlockSpec((1, 1, S), lambda h, i: (h, 0, 0)),
                  pl.BlockSpec((tq, DV_A), lambda h, i: (i, 2 * vb + h)),
                  pl.BlockSpec((tq, DV_A), lambda h, i: (i, 3 * vb + h)),
                  pl.BlockSpec((1, DV_A), lambda h, i: (0, h))],
        out_specs=(hblk, hblk, stat, stat),
        compiler_params=_params("parallel", "arbitrary"),
    )(qk, qk, u, fcol, brow, u, u, head_g)


def _mlstm_bwd(qk, u, fcol, brow, m, den, dh, h, du, *, S, D, nha, name):
    tq = min(ML_TQ, S)
    nq = S // tq
    kb, vb = (D // 2) // DK_A, D // DV_A

    def body(q_ref, k_ref, v_ref, fcol_ref, brow_ref, m_ref, den_ref, dh_ref, h_ref, du_in,
             du_ref, dq_ref, dk_ref, rs_ref, cs_ref, dv_acc):
        del du_in
        i = pl.program_id(1)

        @pl.when(i == 0)
        def _():
            dk_ref[...] = jnp.zeros_like(dk_ref)
            cs_ref[...] = jnp.zeros_like(cs_ref)
            dv_acc[...] = jnp.zeros_like(dv_acc)

        q = q_ref[...]
        fc = fcol_ref[0]
        mm = m_ref[0]
        dn = den_ref[0]
        floor = jnp.exp(-mm)
        nrm = jnp.maximum(jnp.abs(dn), floor)
        dhv = dh_ref[...]
        dnum = dhv / nrm
        dnrm = -jnp.sum(dhv * h_ref[...], axis=1, keepdims=True) / nrm
        dden = jnp.where(jnp.abs(dn) > floor, jnp.where(dn > 0.0, dnrm, -dnrm), 0.0)
        dnum_b = dnum.astype(BF16)

        def step(j, carry, diagonal):
            dq, rs = carry
            off, kj, s, logd = _mlstm_tile(q, k_ref, fc, brow_ref, j, tq, diagonal)
            p = jnp.exp(logd - mm)
            a = s * p
            vj = v_ref[pl.ds(off, tq), :].astype(BF16)
            da = _dot_nt(dnum_b, vj) + dden
            dv_acc[pl.ds(off, tq), :] += _dot_tn(a.astype(BF16), dnum_b)
            dqk = (da * p).astype(BF16)
            dq = dq + _dot_nn(dqk, kj)
            dk_ref[pl.ds(off, tq), :] += _dot_tn(dqk, q)
            pm = da * a
            cs_ref[0, :, pl.ds(off, tq)] += jnp.sum(pm, axis=0, keepdims=True)
            rs = rs + jnp.sum(pm, axis=1, keepdims=True)
            return dq, rs

        first = step(i, (jnp.zeros((tq, DK_A), F32), jnp.zeros((tq, 1), F32)), True)
        dq, rs = lax.fori_loop(0, i, lambda j, c: step(j, c, False), first)
        dq_ref[...] = dq
        rs_ref[0] = rs

        @pl.when(i == nq - 1)
        def _():
            du_ref[...] = dv_acc[...].astype(BF16)

    stat = pl.BlockSpec((1, tq, 1), lambda h, i: (h, i, 0))
    rowv = pl.BlockSpec((1, 1, S), lambda h, i: (h, 0, 0))
    hblk = pl.BlockSpec((tq, DV_A), lambda h, i: (i, h))
    return pl.pallas_call(
        body, name=name,
        out_shape=(jax.ShapeDtypeStruct(du.shape, du.dtype),
                   jax.ShapeDtypeStruct((S, D // 2), F32), jax.ShapeDtypeStruct((S, D // 2), F32),
                   jax.ShapeDtypeStruct((nha, S, 1), F32), jax.ShapeDtypeStruct((nha, 1, S), F32)),
        grid=(nha, nq),
        in_specs=[pl.BlockSpec((tq, DK_A), lambda h, i: (i, h)),
                  pl.BlockSpec((S, DK_A), lambda h, i: (0, kb + h)),
                  pl.BlockSpec((S, DV_A), lambda h, i: (0, vb + h)),
                  stat, rowv, stat, stat, hblk, hblk, ANY],
        out_specs=(pl.BlockSpec((S, DV_A), lambda h, i: (0, vb + h)),
                   pl.BlockSpec((tq, DK_A), lambda h, i: (i, h)),
                   pl.BlockSpec((S, DK_A), lambda h, i: (0, h)),
                   stat, rowv),
        scratch_shapes=[pltpu.VMEM((S, DV_A), F32)],
        input_output_aliases={9: 0},
        compiler_params=_params("parallel", "arbitrary"),
    )(qk, qk, u, fcol, brow, m, den, dh, h, du)


def _head_norm(hh):
    mu = jnp.mean(hh, axis=1, keepdims=True)
    hc = hh - mu
    rstd = lax.rsqrt(jnp.mean(hc * hc, axis=1, keepdims=True) + LN_EPS)
    return hc * rstd, rstd


def _hgate_bwd(dhg, h, u, head_g, du, *, S, D, name):
    tm = min(256, S)
    nh = D // DV_A

    def body(dhg_ref, h_ref, o_ref, z_ref, g_ref, du_in, du_ref, dh_ref, dg_ref):
        del du_in

        @pl.when(pl.program_id(0) == 0)
        def _():
            dg_ref[...] = jnp.zeros_like(dg_ref)

        for hd in range(nh):
            sl = slice(hd * DV_A, (hd + 1) * DV_A)
            hn, rstd = _head_norm(h_ref[:, sl])
            o, z, g, d = o_ref[:, sl], z_ref[:, sl], g_ref[:, sl], dhg_ref[:, sl]
            so, sz = _sigmoid(o), _sigmoid(z)
            silu_z = z * sz
            hng = hn * g
            du_ref[:, sl] = (d * hng * silu_z * so * (1.0 - so)).astype(BF16)
            du_ref[:, D + hd * DV_A:D + (hd + 1) * DV_A] = (
                d * so * hng * (sz * (1.0 + z * (1.0 - sz)))).astype(BF16)
            t = d * so * silu_z
            dg_ref[:, sl] += jnp.sum(t * hn, axis=0, keepdims=True)
            dhn = t * g
            dh_ref[:, sl] = rstd * (dhn - jnp.mean(dhn, axis=1, keepdims=True)
                                    - hn * jnp.mean(dhn * hn, axis=1, keepdims=True))

    row = lambda i: (i, 0)
    return pl.pallas_call(
        body, name=name,
        out_shape=(jax.ShapeDtypeStruct(du.shape, du.dtype), jax.ShapeDtypeStruct((S, D), F32),
                   jax.ShapeDtypeStruct((1, D), F32)),
        grid=(S // tm,),
        in_specs=[pl.BlockSpec((tm, D), row), pl.BlockSpec((tm, D), row),
                  pl.BlockSpec((tm, D), lambda i: (i, 2)), pl.BlockSpec((tm, D), lambda i: (i, 3)),
                  pl.BlockSpec((1, D), lambda i: (0, 0)), ANY],
        out_specs=(pl.BlockSpec((tm, 2 * D), lambda i: (i, 1)), pl.BlockSpec((tm, D), row),
                   pl.BlockSpec((1, D), lambda i: (0, 0))),
        input_output_aliases={5: 0},
        compiler_params=_params("arbitrary"),
    )(dhg, h, u, u, head_g, du)


def _ln_stats(r):
    mu = jnp.mean(r, axis=1, keepdims=True)
    xc = r - mu
    rstd = lax.rsqrt(jnp.mean(xc * xc, axis=1, keepdims=True) + LN_EPS)
    return xc * rstd, rstd


def _ln_back(dxhat, xhat, rstd):
    return rstd * (dxhat - jnp.mean(dxhat, axis=1, keepdims=True)
                   - xhat * jnp.mean(dxhat * xhat, axis=1, keepdims=True))


def _ln_fwd(x, y, g, b, *, S, D, name):
    tm = min(256, S)

    def body(x_ref, y_ref, g_ref, b_ref, o_ref, ob_ref):
        xhat, _ = _ln_stats(ALPHA * x_ref[...] + y_ref[...])
        o = xhat * g_ref[...] + b_ref[...]
        o_ref[...] = o
        ob_ref[...] = o.astype(BF16)

    row = lambda i: (i, 0)
    vec = pl.BlockSpec((1, D), lambda i: (0, 0))
    return pl.pallas_call(
        body, name=name,
        out_shape=(jax.ShapeDtypeStruct((S, D), F32), jax.ShapeDtypeStruct((S, D), BF16)),
        grid=(S // tm,),
        in_specs=[pl.BlockSpec((tm, D), row), pl.BlockSpec((tm, D), row), vec, vec],
        out_specs=(pl.BlockSpec((tm, D), row), pl.BlockSpec((tm, D), row)),
        compiler_params=_params("parallel"),
    )(x, y, g, b)


def _ln_loss_bwd(x1, y2, target, g, b, *, S, D, name):
    tm = min(256, S)

    def body(x_ref, y_ref, t_ref, g_ref, b_ref, dr_ref, drb_ref, dg_ref, db_ref, loss_ref):
        @pl.when(pl.program_id(0) == 0)
        def _():
            dg_ref[...] = jnp.zeros_like(dg_ref)
            db_ref[...] = jnp.zeros_like(db_ref)
            loss_ref[...] = jnp.zeros_like(loss_ref)

        xhat, rstd = _ln_stats(ALPHA * x_ref[...] + y_ref[...])
        diff = xhat * g_ref[...] + b_ref[...] - t_ref[...]
        loss_ref[...] += (0.5 / D) * jnp.sum(diff * diff)
        dx2 = diff * (1.0 / D)
        dg_ref[...] += jnp.sum(dx2 * xhat, axis=0, keepdims=True)
        db_ref[...] += jnp.sum(dx2, axis=0, keepdims=True)
        dr = _ln_back(dx2 * g_ref[...], xhat, rstd)
        dr_ref[...] = dr
        drb_ref[...] = dr.astype(BF16)

    row = lambda i: (i, 0)
    vec = pl.BlockSpec((1, D), lambda i: (0, 0))
    return pl.pallas_call(
        body, name=name,
        out_shape=(jax.ShapeDtypeStruct((S, D), F32), jax.ShapeDtypeStruct((S, D), BF16),
                   jax.ShapeDtypeStruct((1, D), F32), jax.ShapeDtypeStruct((1, D), F32),
                   jax.ShapeDtypeStruct((1, LANES), F32)),
        grid=(S // tm,),
        in_specs=[pl.BlockSpec((tm, D), row)] * 3 + [vec, vec],
        out_specs=(pl.BlockSpec((tm, D), row), pl.BlockSpec((tm, D), row), vec, vec,
                   pl.BlockSpec((1, LANES), lambda i: (0, 0))),
        compiler_params=_params("arbitrary"),
    )(x1, y2, target, g, b)


def _ln_bwd(x, y, g, dout, *, S, D, name):
    tm = min(256, S)

    def body(x_ref, y_ref, g_ref, d_ref, dr_ref, drb_ref, dg_ref, db_ref):
        @pl.when(pl.program_id(0) == 0)
        def _():
            dg_ref[...] = jnp.zeros_like(dg_ref)
            db_ref[...] = jnp.zeros_like(db_ref)

        xhat, rstd = _ln_stats(ALPHA * x_ref[...] + y_ref[...])
        d = d_ref[...]
        dg_ref[...] += jnp.sum(d * xhat, axis=0, keepdims=True)
        db_ref[...] += jnp.sum(d, axis=0, keepdims=True)
        dr = _ln_back(d * g_ref[...], xhat, rstd)
        dr_ref[...] = dr
        drb_ref[...] = dr.astype(BF16)

    row = lambda i: (i, 0)
    vec = pl.BlockSpec((1, D), lambda i: (0, 0))
    return pl.pallas_call(
        body, name=name,
        out_shape=(jax.ShapeDtypeStruct((S, D), F32), jax.ShapeDtypeStruct((S, D), BF16),
                   jax.ShapeDtypeStruct((1, D), F32), jax.ShapeDtypeStruct((1, D), F32)),
        grid=(S // tm,),
        in_specs=[pl.BlockSpec((tm, D), row), pl.BlockSpec((tm, D), row), vec, pl.BlockSpec((tm, D), row)],
        out_specs=(pl.BlockSpec((tm, D), row), pl.BlockSpec((tm, D), row), vec, vec),
        compiler_params=_params("arbitrary"),
    )(x, y, g, dout)


def _sb_scores(q, kj, diagonal, t):
    z = _dot_nt(q, kj)
    ls = _log_sigmoid(z)
    if not diagonal:
        return None, ls, ls - z
    valid = lax.broadcasted_iota(jnp.int32, (t, t), 1) < lax.broadcasted_iota(jnp.int32, (t, t), 0)
    return valid, ls, jnp.where(valid, ls - z, 0.0)


def _keep(valid, x):
    return x if valid is None else jnp.where(valid, x, 0.0)


def _sb_fwd(q2, kv, z2, *, S, D, name, t=SB_TILE):
    t = min(t, S)
    nq = S // t
    nh = D // DH_B

    def body(q_ref, k_ref, v_ref, z_ref, o_ref, hb_ref, tot_ref):
        i = pl.program_id(1)
        q = q_ref[...]
        _, _, after = _tri(min(t, MXU_DEPTH), lambda r, c: r > c)
        tri_after = after.astype(BF16)

        def step(j, carry, diagonal):
            acc, cr = carry
            off = pl.multiple_of(j * t, t)
            valid, ls, lneg = _sb_scores(q, k_ref[pl.ds(off, t), :], diagonal, t)
            between = cr + _block_sums(lneg, tri_after, True)
            a = _keep(valid, jnp.exp(ls + between))
            acc = acc + _dot_nn(a.astype(BF16), v_ref[pl.ds(off, t), :])
            return acc, cr + jnp.sum(lneg, axis=1, keepdims=True)

        carry = step(i, (jnp.zeros((t, DH_B), F32), jnp.zeros((t, 1), F32)), True)
        acc, cr = lax.fori_loop(0, i, lambda jj, c: step(i - 1 - jj, c, False), carry)
        o_ref[...] = acc
        z = z_ref[...]
        hb_ref[...] = (acc * (z * _sigmoid(z))).astype(BF16)
        tot_ref[0] = cr

    blk = pl.BlockSpec((t, DH_B), lambda h, i: (i, h))
    return pl.pallas_call(
        body, name=name,
        out_shape=(jax.ShapeDtypeStruct((S, D), F32), jax.ShapeDtypeStruct((S, D), BF16),
                   jax.ShapeDtypeStruct((nh, S, 1), F32)), grid=(nh, nq),
        in_specs=[blk, pl.BlockSpec((S, DH_B), lambda h, i: (0, h)),
                  pl.BlockSpec((S, DH_B), lambda h, i: (0, nh + h)), blk],
        out_specs=(blk, blk, pl.BlockSpec((1, t, 1), lambda h, i: (h, i, 0))),
        compiler_params=_params("parallel", "arbitrary"),
    )(q2, kv, kv, z2)


def _sb_bwd(q2, kv, dhb, att, z2, tot, *, S, D, name, t=SB_TILE):
    t = min(t, S)
    nq = S // t
    nh = D // DH_B

    def body(q_ref, k_ref, v_ref, dhb_ref, att_ref, z_ref, tot_ref, dq_ref, dk_ref, dv_ref, dz2_ref, dk_acc, dv_acc):
        i = pl.program_id(1)

        @pl.when(i == 0)
        def _():
            dk_acc[...] = jnp.zeros_like(dk_acc)
            dv_acc[...] = jnp.zeros_like(dv_acc)

        q = q_ref[...]
        z, dhb = z_ref[...], dhb_ref[...]
        sz = _sigmoid(z)
        dz2_ref[...] = (dhb * att_ref[...] * (sz * (1.0 + z * (1.0 - sz)))).astype(BF16)
        do_b = (dhb * (z * sz)).astype(BF16)
        _, _, after = _tri(min(t, MXU_DEPTH), lambda r, c: r > c)
        tri_after = after.astype(BF16)
        _, _, before = _tri(min(t, MXU_DEPTH), lambda r, c: r < c)
        tri_before = before.astype(BF16)

        def step(j, carry, diagonal):
            dq, rest, cg = carry
            off = pl.multiple_of(j * t, t)
            kj = k_ref[pl.ds(off, t), :]
            valid, ls, lneg = _sb_scores(q, kj, diagonal, t)
            rest = rest - jnp.sum(lneg, axis=1, keepdims=True)
            between = rest + _block_sums(lneg, tri_after, True)
            a = _keep(valid, jnp.exp(ls + between))
            g = _dot_nt(do_b, v_ref[pl.ds(off, t), :]) * a
            dv_acc[pl.ds(off, t), :] += _dot_tn(a.astype(BF16), do_b)
            e = cg + _block_sums(g, tri_before, False, passes=1)
            sig = jnp.exp(ls)
            dz = _keep(valid, g * (1.0 - sig) - e * sig)
            dz_b = dz.astype(BF16)
            dq = dq + _dot_nn(dz_b, kj)
            dk_acc[pl.ds(off, t), :] += _dot_tn(dz_b, q)
            return dq, rest, cg + jnp.sum(g, axis=1, keepdims=True)

        carry = lax.fori_loop(0, i, lambda j, c: step(j, c, False),
                              (jnp.zeros((t, DH_B), F32), tot_ref[0], jnp.zeros((t, 1), F32)))
        dq, _, _ = step(i, carry, True)
        dq_ref[...] = (dq * (DH_B ** -0.5)).astype(BF16)

        @pl.when(i == nq - 1)
        def _():
            dk_ref[...] = dk_acc[...].astype(BF16)
            dv_ref[...] = dv_acc[...].astype(BF16)

    blk = pl.BlockSpec((t, DH_B), lambda h, i: (i, h))
    sd = jax.ShapeDtypeStruct((S, D), BF16)
    return pl.pallas_call(
        body, name=name, out_shape=(sd, sd, sd, sd), grid=(nh, nq),
        in_specs=[blk, pl.BlockSpec((S, DH_B), lambda h, i: (0, h)),
                  pl.BlockSpec((S, DH_B), lambda h, i: (0, nh + h)), blk, blk, blk,
                  pl.BlockSpec((1, t, 1), lambda h, i: (h, i, 0))],
        out_specs=(blk, pl.BlockSpec((S, DH_B), lambda h, i: (0, h)),
                   pl.BlockSpec((S, DH_B), lambda h, i: (0, h)), blk),
        scratch_shapes=[pltpu.VMEM((S, DH_B), F32), pltpu.VMEM((S, DH_B), F32)],
        compiler_params=_params("parallel", "arbitrary"),
    )(q2, kv, kv, dhb, att, z2, tot)


def _adamw_math(w, g, m, v):
    mn = ADAM_B1 * m + (1.0 - ADAM_B1) * g
    vn = ADAM_B2 * v + (1.0 - ADAM_B2) * (g * g)
    m_hat = mn / (1.0 - ADAM_B1 ** ADAM_STEP)
    v_hat = vn / (1.0 - ADAM_B2 ** ADAM_STEP)
    return -ADAM_LR * (m_hat / (jnp.sqrt(v_hat) + ADAM_EPS) + ADAM_WD * w), mn, vn


def _adamw(w, g, m, v, *, name, tr=128):
    rows, cols = w.shape
    tr = min(tr, rows)
    assert rows % tr == 0

    def body(w_ref, g_ref, m_ref, v_ref, d_ref, mo_ref, vo_ref):
        d, mn, vn = _adamw_math(w_ref[...], g_ref[...], m_ref[...], v_ref[...])
        d_ref[...] = d
        mo_ref[...] = mn
        vo_ref[...] = vn

    blk = pl.BlockSpec((tr, cols), lambda i: (i, 0))
    sd = jax.ShapeDtypeStruct((rows, cols), F32)
    return pl.pallas_call(
        body, name=name, out_shape=(sd, sd, sd), grid=(rows // tr,),
        in_specs=[blk] * 4, out_specs=(blk, blk, blk),
        compiler_params=_params("parallel"),
    )(w, g, m, v)


def _local_step(x, target, wa, wg, wao, wbi, wkv, wbo, gate_b, conv_w, conv_b, head_g,
                a_ln_g, a_ln_b, b_ln_g, b_ln_b, *, S, D, nha, rs):
    nb = S // LANES
    kw = dict(S=S, D=D)
    xb = x.astype(BF16)
    wa_chunk = D if wa.ndim == 3 else None
    u = _mm(xb, wa, mode="nn", M=S, N=4 * D, K=D, out_dtype=F32, name="a_in", b_chunk=wa_chunk)
    ug = _mm(xb, wg, mode="nn", M=S, N=LANES, K=D, out_dtype=F32, name="a_in_gates")
    qk = _conv_fwd(u, conv_w, conv_b, name="conv_fwd", **kw)
    gt = ug[:, :2 * nha].T.reshape(2 * nha * nb, LANES)
    gbias = jnp.repeat(gate_b.reshape(2 * nha), nb).reshape(2 * nha * nb, 1)
    fcs, brow = _gates_fwd(gt, gbias, nha=nha, nb=nb, name="gates_fwd")
    fcol = fcs.reshape(nha, S, 1)
    brow = brow.reshape(nha, 1, S)
    h, hg, m, den = _mlstm_fwd(qk, u, fcol, brow, head_g, nha=nha, name="mlstm_fwd", **kw)
    y = _mm(hg, wao, mode="nn", M=S, N=D, K=D, out_dtype=F32, name="a_out")
    x1, x1b = _ln_fwd(x, y, a_ln_g, a_ln_b, name="ln_a_fwd", **kw)
    q2 = _mm(x1b, wbi, mode="nn", M=S, N=D, K=D, out_dtype=BF16, name="b_in_q", out_scale=DH_B ** -0.5)
    z2 = _mm(x1b, wbi, mode="nn", M=S, N=D, K=D, out_dtype=F32, name="b_in_z", b_off=(0, D))
    kv = _mm(x1b, wkv, mode="nn", M=S, N=2 * D, K=D, out_dtype=BF16, name="b_kv")
    att, hb, sb_tot = _sb_fwd(q2, kv, z2, name="sb_fwd", **kw)
    y2 = _mm(hb, wbo, mode="nn", M=S, N=D, K=D, out_dtype=F32, name="b_out")
    dr2, dr2b, d_bln_g, d_bln_b, loss = _ln_loss_bwd(x1, y2, target, b_ln_g, b_ln_b, name="ln_b_loss", **kw)
    g_wbo = _mm(hb, dr2b, mode="tn", M=D, N=D, K=S, out_dtype=BF16, name="g_b_out")
    dhb = _mm(dr2b, wbo, mode="nt", M=S, N=D, K=D, out_dtype=F32, name="d_b_out")
    dq2, dk2, dv2, dz2 = _sb_bwd(q2, kv, dhb, att, z2, sb_tot, name="sb_bwd", **kw)
    g_wbi = _mm(x1b, dq2, mode="tn", M=D, N=D, K=S, out_dtype=BF16, name="g_b_in_q",
                into=lax.empty((D, 2 * D), BF16))
    g_wbi = _mm(x1b, dz2, mode="tn", M=D, N=D, K=S, out_dtype=BF16, name="g_b_in_z", into=g_wbi, out_off=(0, D))
    g_wkv = _mm(x1b, dk2, mode="tn", M=D, N=D, K=S, out_dtype=BF16, name="g_kv_k",
                into=lax.empty((D, 2 * D), BF16))
    g_wkv = _mm(x1b, dv2, mode="tn", M=D, N=D, K=S, out_dtype=BF16, name="g_kv_v", into=g_wkv, out_off=(0, D))
    dq2 = rs.begin("b", dict(wbo=g_wbo, wbi=g_wbi, wkv=g_wkv), dq2)
    dx1 = _mm(dq2, wbi, mode="nt", M=S, N=D, K=D, out_dtype=F32, name="d_b_in_q", acc_in=dr2, acc_scale=ALPHA)
    dx1 = _mm(dz2, wbi, mode="nt", M=S, N=D, K=D, out_dtype=F32, name="d_b_in_z", b_off=(0, D), acc_in=dx1)
    dx1 = _mm(dk2, wkv, mode="nt", M=S, N=D, K=D, out_dtype=F32, name="d_kv_k", acc_in=dx1)
    dx1 = _mm(dv2, wkv, mode="nt", M=S, N=D, K=D, out_dtype=F32, name="d_kv_v", b_off=(0, D), acc_in=dx1)
    dx1 = rs.pair("b", dx1)
    dr, drb, d_aln_g, d_aln_b = _ln_bwd(x, y, a_ln_g, dx1, name="ln_a_bwd", **kw)
    g_wao = _mm(hg, drb, mode="tn", M=D, N=D, K=S, out_dtype=BF16, name="g_a_out")
    drb = rs.begin("a", dict(wao=g_wao), drb)
    dhg = _mm(drb, wao, mode="nt", M=S, N=D, K=D, out_dtype=F32, name="d_a_out")
    du = lax.empty((S, 4 * D), BF16)
    du, dh, d_head_g = _hgate_bwd(dhg, h, u, head_g, du, name="hgate_bwd", **kw)
    dh = rs.pair("a", dh)
    du, dq, dk, rowsum, colsum = _mlstm_bwd(qk, u, fcol, brow, m, den, dh, h, du, nha=nha, name="mlstm_bwd", **kw)
    dgt, dgtot = _gates_bwd(rowsum.reshape(nha * nb, LANES), colsum.reshape(nha * nb, LANES), gt, gbias,
                            nha=nha, nb=nb, name="gates_bwd")
    d_gate_b = dgtot[::nb, 0].reshape(1, 2 * nha)
    dgp = jnp.pad(dgt.reshape(2 * nha, S).T, ((0, 0), (0, LANES - 2 * nha))).astype(BF16)
    du, d_conv_w, d_conv_b = _conv_bwd(u, dq, dk, conv_w, conv_b, du, name="conv_bwd", **kw)
    small = dict(gate_b=d_gate_b, conv_w=d_conv_w, conv_b=d_conv_b, head_g=d_head_g,
                 a_ln_g=d_aln_g, a_ln_b=d_aln_b, b_ln_g=d_bln_g, b_ln_b=d_bln_b)
    du = rs.chip("b", du)
    g_wa = _mm(xb, du, mode="tn", M=D, N=4 * D, K=S, out_dtype=BF16, name="g_a_in")
    g_wg = _mm(xb, dgp, mode="tn", M=D, N=LANES, K=S, out_dtype=BF16, name="g_a_in_gates")
    du = rs.begin("w", dict(wa=g_wa, wg=g_wg), du)
    du = rs.pair("w", rs.chip("a", rs.small(small, loss, du)))
    du = rs.finish("a", rs.finish("b", du))
    dx = _mm(du, wa, mode="nt", M=S, N=D, K=4 * D, out_dtype=F32, name="d_a_in", acc_in=dr, acc_scale=ALPHA,
             b_chunk=wa_chunk)
    dx = _mm(dgp, wg, mode="nt", M=S, N=D, K=LANES, out_dtype=F32, name="d_a_in_gates", acc_in=dx)
    return dx


def _coords():
    return lax.axis_index("x"), lax.axis_index("y"), lax.axis_index("c")


def _other_chips(x, y):
    return [(1 - x, y), (x, 1 - y), (1 - x, 1 - y)]


def _rows(ref, start, size):
    return ref.at[pl.ds(pl.multiple_of(start, size), size), :]


def _window(kind, ref, shard_shape, j, hf=None, quarter=None):
    r, cw = shard_shape
    row0, nr = (0, r) if hf is None else (hf * (r // 2), r // 2)
    if quarter is not None:
        row0, nr = row0 + quarter * (r // 4), r // 4
    if kind == "stack":
        return ref.at[j, pl.ds(pl.multiple_of(row0, nr), nr), :]
    if kind == "rows":
        return ref.at[pl.ds(pl.multiple_of(j * r + row0, nr), nr), :]
    assert kind == "cols"
    return ref.at[pl.ds(pl.multiple_of(row0, nr), nr), pl.ds(pl.multiple_of(j * cw, cw), cw)]


def _gathered_shape(kind, shard_shape):
    r, cw = shard_shape
    return {"stack": (N_CHIPS, r, cw), "rows": (N_CHIPS * r, cw), "cols": (r, N_CHIPS * cw)}[kind]


def _remote(src, dst, send_sems, recv_sems, k, to):
    return pltpu.make_async_remote_copy(src_ref=src, dst_ref=dst, send_sem=send_sems.at[k],
                                        recv_sem=recv_sems.at[k], device_id=to, device_id_type=MESH)


def _allgather_copies(s_refs, sm_ref, g_refs, smg_ref, send_sems, recv_sems, kinds, shapes):
    n = len(s_refs)
    x, y, c = _coords()
    me, sibling, chips = 2 * x + y, (x, y, 1 - c), _other_chips(x, y)
    ids = [2 * cx + cy for cx, cy in chips]
    d2d, own_base, small_base = 4 * n, 7 * n, 8 * n

    def win(a, j, hf=None, quarter=None):
        return _window(kinds[a], g_refs[a], shapes[a], j, hf, quarter)

    def small_win(j):
        w = sm_ref.shape[1]
        return smg_ref.at[:, pl.ds(pl.multiple_of(j * w, w), w)]

    def rc(src, dst, k, to):
        return _remote(src, dst, send_sems, recv_sems, k, to)

    first = [rc(s_refs[a], win(a, me), own_base + a, sibling) for a in range(n)]
    for a in range(n):
        src = _rows(s_refs[a], c * (shapes[a][0] // 2), shapes[a][0] // 2)
        first += [rc(src, win(a, me, c), 4 * a + k, (*chips[k], c)) for k in range(2)]
    if sm_ref is not None:
        first.append(rc(sm_ref, small_win(me), small_base + 3, sibling))
        first += [rc(sm_ref, small_win(me), small_base + k, (*chip, c)) for k, chip in enumerate(chips)]
    for cp in first:
        cp.start()
    passed = []
    for a in range(n):
        for k in range(2):
            rc(win(a, ids[k], c), win(a, ids[k], c), 4 * a + k, sibling).wait_recv()
            part = win(a, ids[k], c, quarter=k)
            passed.append(rc(part, part, 4 * a + 2 + k, (*chips[1 - k], c)))
            passed.append(rc(win(a, ids[k], c), win(a, ids[k], c), d2d + 3 * a + k, sibling))
            passed[-2].start()
            passed[-1].start()
    for a in range(n):
        for k in range(2):
            part = win(a, ids[2], c, quarter=k)
            rc(part, part, 4 * a + 2 + k, sibling).wait_recv()
        passed.append(rc(win(a, ids[2], c), win(a, ids[2], c), d2d + 3 * a + 2, sibling))
        passed[-1].start()
    if sm_ref is not None:
        for k in range(3):
            rc(small_win(ids[k]), small_win(ids[k]), small_base + k, sibling).wait_recv()
        rc(small_win(me), small_win(me), small_base + 3, sibling).wait_recv()
    for a in range(n):
        for k in range(3):
            rc(win(a, ids[k], 1 - c), win(a, ids[k], 1 - c), d2d + 3 * a + k, sibling).wait_recv()
        rc(win(a, me), win(a, me), own_base + a, sibling).wait_recv()
    for cp in first + passed:
        cp.wait_send()


def _peer_handshake(peers):
    barrier = pltpu.get_barrier_semaphore()
    for peer in peers:
        pl.semaphore_signal(barrier, inc=1, device_id=peer, device_id_type=MESH)
    pl.semaphore_wait(barrier, len(peers))


def _behind(copies, inputs, out_structs, nsem, peers, *, name, collective_id):
    hbm = pltpu.MemorySpace.HBM
    in_refs = [jax.new_ref(a, memory_space=hbm) for a in inputs]
    out_refs = [jax.empty_ref(st, memory_space=hbm) for st in out_structs]

    @pl.kernel(mesh=plsc.ScalarSubcoreMesh(axis_name="sequencer", num_cores=1), name=name,
               scratch_types=(pltpu.SemaphoreType.DMA((nsem,)), pltpu.SemaphoreType.DMA((nsem,))),
               compiler_params=pltpu.CompilerParams(collective_id=collective_id))
    def launch(send_sems, recv_sems):
        _peer_handshake(peers(*_coords()))
        copies(in_refs, out_refs, send_sems, recv_sems)

    launch()
    return [r[...] for r in out_refs]


def _sibling(x, y, c):
    return [(x, y, 1 - c)]


def _same_core_of_other_chips(x, y, c):
    return [(cx, cy, c) for cx, cy in _other_chips(x, y)]


def _sibling_and_other_chips(x, y, c):
    return _sibling(x, y, c) + _same_core_of_other_chips(x, y, c)


ID_ALL_PEERS, ID_SIBLING, ID_CHIPS = 0, 1, 2


def _allgather_behind(shards, kinds, small=None, *, name):
    n = len(shards)
    shapes = [sh.shape for sh in shards]

    def copies(in_refs, out_refs, send_sems, recv_sems):
        if small is None:
            _allgather_copies(in_refs, None, out_refs, None, send_sems, recv_sems, kinds, shapes)
        else:
            _allgather_copies(in_refs[:n], in_refs[n], out_refs[:n], out_refs[n], send_sems, recv_sems, kinds, shapes)

    outs = [jax.ShapeDtypeStruct(_gathered_shape(kinds[a], shapes[a]), shards[a].dtype) for a in range(n)]
    inputs = list(shards)
    if small is not None:
        outs.append(jax.ShapeDtypeStruct((small.shape[0], N_CHIPS * small.shape[1]), small.dtype))
        inputs.append(small)
    return _behind(copies, inputs, outs, 8 * n + 4, _sibling_and_other_chips, name=name, collective_id=ID_ALL_PEERS)


def _rs_pair_exchange(views, *, name):
    n = len(views)

    def copies(g_refs, p_refs, send_sems, recv_sems):
        x, y, c = _coords()
        cps = []
        for a in range(n):
            r2 = views[a].shape[1] // 2
            src = g_refs[a].at[:, pl.ds(pl.multiple_of((1 - c) * r2, r2), r2), :]
            cps.append(_remote(src, p_refs[a], send_sems, recv_sems, a, (x, y, 1 - c)))
        for cp in cps:
            cp.start()
        for cp in cps:
            cp.wait()

    outs = [jax.ShapeDtypeStruct((v.shape[0], v.shape[1] // 2, v.shape[2]), v.dtype) for v in views]
    return _behind(copies, views, outs, n, _sibling, name=name, collective_id=ID_SIBLING)


def _add_half(view, part, core, *, name):
    nch, r, cols = view.shape
    r2 = r // 2
    tr = min(512, r2)
    nt = r2 // tr

    def body(c_ref, g_ref, p_ref, o_ref):
        del c_ref
        o_ref[...] = (g_ref[...].astype(F32) + p_ref[...].astype(F32)).astype(o_ref.dtype)

    return pl.pallas_call(
        body, name=name, out_shape=jax.ShapeDtypeStruct((nch, r2, cols), view.dtype),
        grid_spec=pltpu.PrefetchScalarGridSpec(
            num_scalar_prefetch=1, grid=(nch, nt),
            in_specs=[pl.BlockSpec((1, tr, cols), lambda ch, i, c_ref: (ch, c_ref[0] * nt + i, 0)),
                      pl.BlockSpec((1, tr, cols), lambda ch, i, c_ref: (ch, i, 0))],
            out_specs=pl.BlockSpec((1, tr, cols), lambda ch, i, c_ref: (ch, i, 0))),
        compiler_params=_params("parallel", "parallel"),
    )(core, view, part)


def _chunk(kind, ref, j, cw):
    if kind == "cols":
        return ref.at[0, :, pl.ds(pl.multiple_of(j * cw, cw), cw)]
    return ref.at[j]


def _rs_chips(pairs, kinds, *, name):
    n = len(pairs)
    half_shapes = []
    for a in range(n):
        nch, r2, cols = pairs[a].shape
        half_shapes.append((r2, cols // N_CHIPS) if kinds[a] == "cols" else (r2, cols))

    def copies(q_refs, t_refs, send_sems, recv_sems):
        x, y, c = _coords()
        chips = _other_chips(x, y)
        sends = []
        for a in range(n):
            for k, (cx, cy) in enumerate(chips):
                src = _chunk(kinds[a], q_refs[a], 2 * cx + cy, half_shapes[a][1])
                sends.append(_remote(src, t_refs[a].at[k], send_sems, recv_sems, 3 * a + k, (cx, cy, c)))
        for cp in sends:
            cp.start()
        for a in range(n):
            for k in range(3):
                slot = t_refs[a].at[k]
                _remote(slot, slot, send_sems, recv_sems, 3 * a + k, (x, y, c)).wait_recv()
        for cp in sends:
            cp.wait_send()

    outs = [jax.ShapeDtypeStruct((3, *half_shapes[a]), pairs[a].dtype) for a in range(n)]
    return _behind(copies, pairs, outs, 3 * n, _same_core_of_other_chips, name=name, collective_id=ID_CHIPS)


def _sum_chips(pair, parts, chip, kind, *, name):
    _, r, cols = parts.shape
    tr = min(128, r)
    if kind == "cols":
        own_spec = pl.BlockSpec((1, tr, cols), lambda i, chip_ref: (0, i, chip_ref[0]))
    else:
        own_spec = pl.BlockSpec((1, tr, cols), lambda i, chip_ref: (chip_ref[0], i, 0))

    def body(chip_ref, q_ref, p_ref, o_ref):
        del chip_ref
        acc = q_ref[0].astype(F32)
        for s in range(3):
            acc = acc + p_ref[s].astype(F32)
        o_ref[...] = acc

    return pl.pallas_call(
        body, name=name, out_shape=jax.ShapeDtypeStruct((r, cols), F32),
        grid_spec=pltpu.PrefetchScalarGridSpec(
            num_scalar_prefetch=1, grid=(r // tr,),
            in_specs=[own_spec, pl.BlockSpec((3, tr, cols), lambda i, chip_ref: (0, i, 0))],
            out_specs=pl.BlockSpec((tr, cols), lambda i, chip_ref: (i, 0))),
        compiler_params=_params("parallel"),
    )(chip, pair, parts)


def _rs_share(halves, *, name):
    n = len(halves)

    def copies(h_refs, g_refs, send_sems, recv_sems):
        x, y, c = _coords()
        cps = [_remote(h_refs[a], g_refs[a], send_sems, recv_sems, a, (x, y, 1 - c)) for a in range(n)]
        for cp in cps:
            cp.start()
        for cp in cps:
            cp.wait()

    outs = [jax.ShapeDtypeStruct(h.shape, h.dtype) for h in halves]
    return _behind(copies, halves, outs, n, _sibling, name=name, collective_id=ID_SIBLING)


class _GradReducer:
    def __init__(self, w, m, v, long_name, core, chip_ix, as_views, small_step):
        self.w, self.m, self.v, self.long_name = w, m, v, long_name
        self.core, self.chip_ix, self.as_views, self.small_step = core, chip_ix, as_views, small_step
        self.groups, self.results, self.small_results = {}, {}, None

    def small(self, small, loss_row, tie):
        (small, loss_row), tie = lax.optimization_barrier(((small, loss_row), tie))
        self.small_results, tie = lax.optimization_barrier((self.small_step(small, loss_row), tie))
        return tie

    def begin(self, tag, grads, tie):
        keys, views, kinds = self.as_views(grads)
        views, tie = lax.optimization_barrier((views, tie))
        parts = _rs_pair_exchange(views, name="rs_pair_exchange_" + tag)
        self.groups[tag] = dict(keys=keys, views=views, kinds=kinds, parts=parts)
        return tie

    def pair(self, tag, tie):
        g = self.groups[tag]
        g["parts"], tie = lax.optimization_barrier((g["parts"], tie))
        pairs = [_add_half(v, p, self.core, name="pair_sum_" + self.long_name[k])
                 for v, p, k in zip(g["views"], g["parts"], g["keys"])]
        pairs, tie = lax.optimization_barrier((pairs, tie))
        g["pairs"] = pairs
        g["slots"] = _rs_chips(pairs, g["kinds"], name="rs_chips_" + tag)
        return tie

    def chip(self, tag, tie, update_mine=False):
        g = self.groups[tag]
        g["slots"], tie = lax.optimization_barrier((g["slots"], tie))
        halves = [_sum_chips(q, t, self.chip_ix, kd, name="chip_sum_" + self.long_name[k])
                  for q, t, kd, k in zip(g["pairs"], g["slots"], g["kinds"], g["keys"])]
        halves, tie = lax.optimization_barrier((halves, tie))
        g["halves"] = halves
        g["others"] = _rs_share(halves, name="rs_share_" + tag)
        if update_mine:
            mine = [_adamw_rows(self.w[k], a, self.m[k], self.v[k], self.core, name="adamw_mine_" + self.long_name[k])
                    for k, a in zip(g["keys"], halves)]
            g["mine"], tie = lax.optimization_barrier((mine, tie))
        return tie

    def finish(self, tag, tie):
        g = self.groups[tag]
        g["others"], tie = lax.optimization_barrier((g["others"], tie))
        if "mine" in g:
            out = [_adamw_rows(self.w[k], b, self.m[k], self.v[k], 1 - self.core, into=done,
                               name="adamw_theirs_" + self.long_name[k])
                   for k, b, done in zip(g["keys"], g["others"], g["mine"])]
        else:
            out = [_adamw_halves(self.w[k], a, b, self.m[k], self.v[k], self.core, name="adamw_" + self.long_name[k])
                   for k, a, b in zip(g["keys"], g["halves"], g["others"])]
        out, tie = lax.optimization_barrier((out, tie))
        self.results.update(zip(g["keys"], out))
        return tie


def _adamw_halves(w, mine, theirs, m, v, core, *, name, tr=128):
    rows, cols = w.shape
    hr = rows // 2
    tr = min(tr, hr)
    nt = hr // tr

    def body(c_ref, w_ref, a_ref, b_ref, m_ref, v_ref, g_ref, d_ref, mo_ref, vo_ref):
        gg = jnp.where(pl.program_id(0) == c_ref[0], a_ref[...], b_ref[...])
        g_ref[...] = gg
        d, mn, vn = _adamw_math(w_ref[...], gg, m_ref[...], v_ref[...])
        d_ref[...] = d
        mo_ref[...] = mn
        vo_ref[...] = vn

    full = pl.BlockSpec((tr, cols), lambda hf, i, c_ref: (hf * nt + i, 0))
    half = pl.BlockSpec((tr, cols), lambda hf, i, c_ref: (i, 0))
    sd = jax.ShapeDtypeStruct((rows, cols), F32)
    return pl.pallas_call(
        body, name=name, out_shape=(sd, sd, sd, sd),
        grid_spec=pltpu.PrefetchScalarGridSpec(
            num_scalar_prefetch=1, grid=(2, nt),
            in_specs=[full, half, half, full, full], out_specs=(full, full, full, full)),
        compiler_params=_params("parallel", "parallel"),
    )(core, w, mine, theirs, m, v)


def _adamw_rows(w, g_half, m, v, half, *, name, into=None, tr=128):
    rows, cols = w.shape
    hr = rows // 2
    tr = min(tr, hr)
    nt = hr // tr

    def body(h_ref, w_ref, a_ref, m_ref, v_ref, *rest):
        g_ref, d_ref, mo_ref, vo_ref = rest[-4:]
        gg = a_ref[...]
        g_ref[...] = gg
        d, mn, vn = _adamw_math(w_ref[...], gg, m_ref[...], v_ref[...])
        d_ref[...] = d
        mo_ref[...] = mn
        vo_ref[...] = vn

    full = pl.BlockSpec((tr, cols), lambda i, h_ref: (h_ref[0] * nt + i, 0))
    sd = jax.ShapeDtypeStruct((rows, cols), F32)
    extra = [] if into is None else list(into)
    return pl.pallas_call(
        body, name=name, out_shape=(sd, sd, sd, sd),
        grid_spec=pltpu.PrefetchScalarGridSpec(
            num_scalar_prefetch=1, grid=(nt,),
            in_specs=[full, pl.BlockSpec((tr, cols), lambda i, h_ref: (i, 0)), full, full] + [ANY] * len(extra),
            out_specs=(full, full, full, full)),
        input_output_aliases={5 + k: k for k in range(len(extra))},
        compiler_params=_params("parallel"),
    )(half, w, g_half, m, v, *extra)


def _allreduce_small(vec):
    m_per, ncol = vec.shape
    n_dev = 2 * N_CHIPS

    def body(x_ref, out_ref, sum_ref, send_sems, recv_sems, local_sem):
        x, y, c = _coords()
        me, sibling = (x, y, c), (x, y, 1 - c)
        chips = _other_chips(x, y)

        def rows(px, py, pc):
            return out_ref.at[pl.ds(pl.multiple_of((4 * px + 2 * py + pc) * m_per, m_per), m_per), :]

        def copy(k, block, to, src=None):
            return _remote(rows(*block) if src is None else src, rows(*block), send_sems, recv_sems, k, to)

        mine = pltpu.make_async_copy(x_ref, rows(*me), local_sem)
        mine.start()
        first = [copy(0, me, sibling, src=x_ref)]
        first += [copy(1 + j, me, (*chip, c), src=x_ref) for j, chip in enumerate(chips)]
        for cp in first:
            cp.start()
        passed = [copy(4 + j, (*chip, c), sibling) for j, chip in enumerate(chips)]
        for j, chip in enumerate(chips):
            copy(1 + j, (*chip, c), me).wait_recv()
            passed[j].start()
        copy(0, sibling, me).wait_recv()
        for j, chip in enumerate(chips):
            copy(4 + j, (*chip, 1 - c), me).wait_recv()
        for cp in first + passed:
            cp.wait_send()
        mine.wait()
        acc = out_ref[0:m_per, :]
        for d in range(1, n_dev):
            acc = acc + out_ref[d * m_per:(d + 1) * m_per, :]
        sum_ref[...] = acc

    vm = pl.BlockSpec(memory_space=pltpu.VMEM)
    return pl.pallas_call(
        body, name="allreduce_small",
        out_shape=(jax.ShapeDtypeStruct((n_dev * m_per, ncol), vec.dtype), jax.ShapeDtypeStruct((m_per, ncol), vec.dtype)),
        in_specs=[vm], out_specs=(vm, vm),
        scratch_shapes=[pltpu.SemaphoreType.DMA((7,)), pltpu.SemaphoreType.DMA((7,)), pltpu.SemaphoreType.DMA],
    )(vec)[1]


def _pack_rows(pieces, total_rows):
    rows = []
    for p in pieces:
        flat = p.reshape(-1)
        flat = jnp.pad(flat, (0, (-flat.shape[0]) % LANES))
        rows.append(flat.reshape(-1, LANES))
    out = jnp.concatenate(rows, axis=0)
    return jnp.pad(out, ((0, total_rows - out.shape[0]), (0, 0)))


def _unpack_rows(packed, shapes):
    out, r = [], 0
    for shp in shapes:
        size = 1
        for d in shp:
            size *= d
        nr = -(-size // LANES)
        out.append(packed[r:r + nr].reshape(-1)[:size].reshape(shp))
        r += nr
    return out


def _round_up(v, m):
    return -(-v // m) * m


def kernel(x, a_w_in, a_gate_b, a_conv_w, a_conv_b, a_head_g, a_w_out, a_ln_g, a_ln_b, kv_w, b_w_in, b_w_out, b_ln_g, b_ln_b, loss_target, m_a_w_in, m_a_gate_b, m_a_conv_w, m_a_conv_b, m_a_head_g, m_a_w_out, m_a_ln_g, m_a_ln_b, m_kv_w, m_b_w_in, m_b_w_out, m_b_ln_g, m_b_ln_b, v_a_w_in, v_a_gate_b, v_a_conv_w, v_a_conv_b, v_a_head_g, v_a_w_out, v_a_ln_g, v_a_ln_b, v_kv_w, v_b_w_in, v_b_w_out, v_b_ln_g, v_b_ln_b):
    _, S, D = x.shape
    nha = a_gate_b.shape[1] // 2
    chip = 2 * lax.axis_index("x") + lax.axis_index("y")
    core = lax.axis_index("c").astype(jnp.int32).reshape(1)
    dq = D // N_CHIPS

    ca = a_w_in.shape[2]
    edge = (N_CHIPS - 1) * (ca - D)
    wide = _round_up(ca + edge, LANES)
    shifted = lax.dynamic_slice_in_dim(
        jnp.pad(a_w_in[0].astype(BF16), ((0, 0), (edge, wide - ca))), edge - chip * (ca - D), wide, axis=1)
    shards = [shifted, a_w_out[0].astype(BF16), kv_w.astype(BF16), b_w_in[0].astype(BF16), b_w_out[0].astype(BF16)]
    kinds = ["stack", "rows", "cols", "cols", "rows"]
    small_shard = jnp.concatenate([a_conv_w[0], a_conv_b, a_head_g, a_ln_g, a_ln_b], axis=0)
    wa_g, small_full = _allgather_behind(shards[:1], kinds[:1], small_shard, name="allgather_first")
    wao, = _allgather_behind(shards[1:2], kinds[1:2], name="allgather_a_w_out")
    wkv, wbi = _allgather_behind(shards[2:4], kinds[2:4], name="allgather_b_in")
    wbo, = _allgather_behind(shards[4:], kinds[4:], name="allgather_b_w_out")
    wa = _join_chunk_edges(wa_g, D)
    wg = wa[N_CHIPS - 1, :, D:D + LANES]
    conv_w, conv_b, head_g, ln_g_a, ln_b_a = (small_full[0:CONV_A], small_full[4:5], small_full[5:6],
                                              small_full[6:7], small_full[7:8])

    chip_ix = chip.astype(jnp.int32).reshape(1)
    (w_wa, m_wa, v_wa), x = lax.optimization_barrier(((a_w_in[0], m_a_w_in[0], v_a_w_in[0]), x))
    w_big = dict(wa=w_wa, wao=a_w_out[0], wkv=kv_w, wbi=b_w_in[0], wbo=b_w_out[0])
    m_big = dict(wa=m_wa, wao=m_a_w_out[0], wkv=m_kv_w, wbi=m_b_w_in[0], wbo=m_b_w_out[0])
    v_big = dict(wa=v_wa, wao=v_a_w_out[0], wkv=v_kv_w, wbi=v_b_w_in[0], wbo=v_b_w_out[0])
    long_name = dict(wa="a_w_in", wao="a_w_out", wkv="kv_w", wbi="b_w_in", wbo="b_w_out")

    def as_views(g):
        if "wa" in g:
            g_full = jnp.concatenate([g["wa"], g["wg"][:, :2 * nha]], axis=1)
            ca = g_full.shape[1] // N_CHIPS
            return ["wa"], [jnp.stack([g_full[:, j * ca:(j + 1) * ca] for j in range(N_CHIPS)], axis=0)], ["stack"]
        keys = list(g)
        views = [g[k].reshape(N_CHIPS, dq, D) if k in ("wao", "wbo") else g[k][None] for k in keys]
        return keys, views, ["stack" if k in ("wao", "wbo") else "cols" for k in keys]

    w_small = [a_gate_b, a_conv_w[0], a_conv_b, a_head_g, a_ln_g, a_ln_b, b_ln_g, b_ln_b]
    m_small = [m_a_gate_b, m_a_conv_w[0], m_a_conv_b, m_a_head_g, m_a_ln_g, m_a_ln_b, m_b_ln_g, m_b_ln_b]
    v_small = [v_a_gate_b, v_a_conv_w[0], v_a_conv_b, v_a_head_g, v_a_ln_g, v_a_ln_b, v_b_ln_g, v_b_ln_b]

    def small_step(small, loss_row):
        order = ["conv_w", "conv_b", "head_g", "a_ln_g", "a_ln_b", "b_ln_g", "b_ln_b", "gate_b"]
        full_shapes = [(CONV_A, D), (1, D), (1, D), (1, D), (1, D), (1, D), (1, D), (1, 2 * nha)]
        n_rows = sum(-(-(s[0] * s[1]) // LANES) for s in full_shapes) + 1
        total = _allreduce_small(_pack_rows([small[k] for k in order] + [loss_row], _round_up(n_rows, 8)))
        sums = dict(zip(order, _unpack_rows(total, full_shapes)))

        def mine(v):
            return lax.dynamic_slice_in_dim(v, chip * dq, dq, axis=1)

        g_small = [sums["gate_b"], mine(sums["conv_w"]), mine(sums["conv_b"]), mine(sums["head_g"]),
                   mine(sums["a_ln_g"]), mine(sums["a_ln_b"]), sums["b_ln_g"], sums["b_ln_b"]]
        rows_small = _round_up(sum(-(-(w.shape[0] * w.shape[1]) // LANES) for w in w_small), 8)
        upd_small = _adamw(_pack_rows(w_small, rows_small), _pack_rows(g_small, rows_small),
                           _pack_rows(m_small, rows_small), _pack_rows(v_small, rows_small), name="adamw_small")
        return total[n_rows - 1, 0], g_small, upd_small

    rs = _GradReducer(w_big, m_big, v_big, long_name, core, chip_ix, as_views, small_step)
    grad_x = _local_step(
        x[0], loss_target[0], wa, wg, wao, wbi, wkv, wbo, a_gate_b, conv_w, conv_b, head_g,
        ln_g_a, ln_b_a, b_ln_g, b_ln_b, S=S, D=D, nha=nha, rs=rs)
    grad_x = rs.finish("w", rs.chip("w", grad_x, update_mine=True))
    upd_big = [rs.results[k] for k in ("wa", "wao", "wkv", "wbi", "wbo")]
    g_big = [u[0] for u in upd_big]
    loss, g_small, upd_small = rs.small_results
    d_small, mn_small, vn_small = (_unpack_rows(u, [w.shape for w in w_small]) for u in upd_small)

    def assemble(big5, small8):
        awi, awo, kvw, bwi, bwo = big5
        gb, cw, cb, hg, alg, alb, blg, blb = small8
        return [awi[None], gb, cw[None], cb, hg, awo[None], alg, alb, kvw, bwi[None], bwo[None], blg, blb]

    grads = assemble(g_big, g_small)
    deltas = assemble([u[1] for u in upd_big], d_small)
    new_m = assemble([u[2] for u in upd_big], mn_small)
    new_v = assemble([u[3] for u in upd_big], vn_small)
    return (loss, grad_x[None], *grads, *deltas, *new_m, *new_v)
```

```python
import functools

import jax
import jax.numpy as jnp
from jax import lax
from jax.experimental import pallas as pl
from jax.experimental.pallas import tpu as pltpu
from jax.experimental.pallas import tpu_sc as plsc

F32 = jnp.float32
BF16 = jnp.bfloat16

DEPTH = 2
ALPHA = (2.0 * DEPTH) ** 0.25
LN_EPS = 1e-5
DK_A = 128
DV_A = 256
DH_B = 128
SB_TILE = 512
ML_TQ = 512
CONV_A = 4
ADAM_LR = 0.001
ADAM_B1 = 0.9
ADAM_B2 = 0.999
ADAM_EPS = 1e-08
ADAM_WD = 0.01
ADAM_STEP = 10
N_CHIPS = 4
LANES = 128
MXU_DEPTH = 256
V7X_VMEM_BYTES = 64 * 1024 * 1024
VMEM_LIMIT = (V7X_VMEM_BYTES * 3) // 4
NEG_BIG = -1e30
MESH = pl.DeviceIdType.MESH
ANY = pl.BlockSpec(memory_space=pl.ANY)


def _params(*sem):
    return pltpu.CompilerParams(dimension_semantics=sem, vmem_limit_bytes=VMEM_LIMIT)


def _dot(a, b, dims):
    return lax.dot_general(a, b, (dims, ((), ())), preferred_element_type=F32)


def _dot_nn(a, b):
    return _dot(a, b, ((1,), (0,)))


def _dot_nt(a, b):
    return _dot(a, b, ((1,), (1,)))


def _dot_tn(a, b):
    return _dot(a, b, ((0,), (0,)))


def _split2(x):
    hi = x.astype(BF16)
    lo = (x - hi.astype(F32)).astype(BF16)
    return hi, lo


def _split3(x):
    hi = x.astype(BF16)
    r = x - hi.astype(F32)
    mid = r.astype(BF16)
    lo = (r - mid.astype(F32)).astype(BF16)
    return hi, mid, lo


def _mask_dot2(x, t01):
    hi, lo = _split2(x)
    return _dot_nn(hi, t01) + _dot_nn(lo, t01)


def _block_sums(x, tri, later, passes=2):
    dot = _mask_dot2 if passes == 2 else (lambda v, t01: _dot_nn(v.astype(BF16), t01))
    sub = tri.shape[0]
    n = x.shape[1] // sub
    if n == 1:
        return dot(x, tri)
    parts = [x[:, b * sub:(b + 1) * sub] for b in range(n)]
    sums = [jnp.sum(p, axis=1, keepdims=True) for p in parts]
    out = []
    for b in range(n):
        acc = dot(parts[b], tri)
        for o in (range(b + 1, n) if later else range(b)):
            acc = acc + sums[o]
        out.append(acc)
    return jnp.concatenate(out, axis=1)


def _mask_dot3(x, t01):
    hi, mid, lo = _split3(x)
    return _dot_nn(hi, t01) + _dot_nn(mid, t01) + _dot_nn(lo, t01)


def _mask_dot3_left(t01, x):
    hi, mid, lo = _split3(x)
    return _dot_nn(t01, hi) + _dot_nn(t01, mid) + _dot_nn(t01, lo)


def _log_sigmoid(z):
    return jnp.minimum(z, 0.0) - jnp.log(1.0 + jnp.exp(-jnp.abs(z)))


def _sigmoid(z):
    return 1.0 / (1.0 + jnp.exp(-z))


def _mm(a, b, *, mode, M, N, K, out_dtype, name, tm=1024, tn=1024, tk=2048,
        a_off=(0, 0), b_off=(0, 0), acc_in=None, acc_scale=1.0, into=None, out_off=(0, 0), b_chunk=None, out_scale=None):
    tm, tn, tk = min(tm, M), min(tn, N), min(tk, K)
    if b_chunk is not None:
        tn, tk = (min(tn, b_chunk), tk) if mode == "nn" else (tn, min(tk, b_chunk))
        assert b.ndim == 3 and b_off == (0, 0) and mode in ("nn", "nt")
    assert M % tm == 0 and N % tn == 0 and K % tk == 0
    nk = K // tk
    ar, ac = a_off
    br, bc = b_off
    orow, ocol = out_off
    if mode in ("nn", "nt"):
        assert ar % tm == 0 and ac % tk == 0
        a_spec = pl.BlockSpec((tm, tk), lambda i, j, k: (i + ar // tm, k + ac // tk))
    else:
        assert ar % tk == 0 and ac % tm == 0
        a_spec = pl.BlockSpec((tk, tm), lambda i, j, k: (k + ar // tk, i + ac // tm))
    if b_chunk is not None and mode == "nn":
        per = b_chunk // tn
        b_spec = pl.BlockSpec((None, tk, tn), lambda i, j, k: (j // per, k, j % per))
    elif b_chunk is not None:
        per = b_chunk // tk
        b_spec = pl.BlockSpec((None, tn, tk), lambda i, j, k: (k // per, j, k % per))
    elif mode == "nt":
        assert br % tn == 0 and bc % tk == 0
        b_spec = pl.BlockSpec((tn, tk), lambda i, j, k: (j + br // tn, k + bc // tk))
    else:
        assert br % tk == 0 and bc % tn == 0
        b_spec = pl.BlockSpec((tk, tn), lambda i, j, k: (k + br // tk, j + bc // tn))
    assert orow % tm == 0 and ocol % tn == 0
    o_spec = pl.BlockSpec((tm, tn), lambda i, j, k: (i + orow // tm, j + ocol // tn))
    dims = {"nn": ((1,), (0,)), "nt": ((1,), (1,)), "tn": ((0,), (0,))}[mode]
    inputs, in_specs = [a, b], [a_spec, b_spec]
    has_acc = acc_in is not None
    if has_acc:
        inputs.append(acc_in)
        in_specs.append(pl.BlockSpec((tm, tn), lambda i, j, k: (i, j)))
    aliases = {}
    if into is not None:
        inputs.append(into)
        in_specs.append(ANY)
        aliases = {len(inputs) - 1: 0}
        out_shape = jax.ShapeDtypeStruct(into.shape, into.dtype)
        assert into.dtype == out_dtype
    else:
        out_shape = jax.ShapeDtypeStruct((M, N), out_dtype)

    def body(*refs):
        a_ref, b_ref = refs[0], refs[1]
        acc_in_ref = refs[2] if has_acc else None
        n_in = len(inputs)
        o_ref = refs[n_in]

        def first():
            if has_acc:
                return acc_scale * acc_in_ref[...]
            return None

        def scaled(r):
            return r if out_scale is None else r * out_scale

        if nk == 1:
            r = _dot(a_ref[...], b_ref[...], dims)
            f = first()
            if f is not None:
                r = r + f
            o_ref[...] = scaled(r).astype(o_ref.dtype)
        else:
            acc_ref = refs[n_in + 1]
            kk = pl.program_id(2)

            @pl.when(kk == 0)
            def _():
                f = first()
                acc_ref[...] = jnp.zeros_like(acc_ref) if f is None else f

            acc_ref[...] += _dot(a_ref[...], b_ref[...], dims)

            @pl.when(kk == nk - 1)
            def _():
                o_ref[...] = scaled(acc_ref[...]).astype(o_ref.dtype)

    scratch = [] if nk == 1 else [pltpu.VMEM((tm, tn), F32)]
    return pl.pallas_call(
        body, name=name, out_shape=out_shape, grid=(M // tm, N // tn, nk),
        in_specs=in_specs, out_specs=o_spec, scratch_shapes=scratch,
        input_output_aliases=aliases,
        compiler_params=_params("parallel", "parallel", "arbitrary"),
    )(*inputs)


def _join_chunk_edges(w, d):
    nch, rows, _ = w.shape

    def body(w_ref, o_ref, first, tail, sems):
        del w_ref
        for j in range(1, nch):
            mine = o_ref.at[j, :, pl.ds(0, LANES)]
            loads = [pltpu.make_async_copy(mine, first, sems.at[0]),
                     pltpu.make_async_copy(o_ref.at[j - 1, :, pl.ds(d, LANES)], tail, sems.at[1])]
            for cp in loads:
                cp.start()
            for cp in loads:
                cp.wait()
            first[...] = first[...] + tail[...]
            store = pltpu.make_async_copy(first, mine, sems.at[0])
            store.start()
            store.wait()

    return pl.pallas_call(
        body, name="join_chunk_edges", out_shape=jax.ShapeDtypeStruct(w.shape, w.dtype),
        in_specs=[ANY], out_specs=ANY, input_output_aliases={0: 0},
        scratch_shapes=[pltpu.VMEM((rows, LANES), w.dtype), pltpu.VMEM((rows, LANES), w.dtype),
                        pltpu.SemaphoreType.DMA((2,))],
    )(w)


def _shift_down(x, d, row):
    if d == 0:
        return x
    return jnp.where(row >= d, pltpu.roll(x, d, 0), 0.0)


def _shift_up(x, d, row, n):
    if d == 0:
        return x
    return jnp.where(row < n - d, pltpu.roll(x, n - d, 0), 0.0)


def _conv_pre(x, w_ref, b_ref, row):
    c = b_ref[...] + w_ref[CONV_A - 1:CONV_A, :] * x
    for k in range(CONV_A - 1):
        c = c + w_ref[k:k + 1, :] * _shift_down(x, CONV_A - 1 - k, row)
    return c


def _conv_fwd(u, conv_w, conv_b, *, S, D, name):
    tc = 256
    nq_blocks = (D // 2) // tc

    def body(u_ref, w_ref, b_ref, o_ref):
        x = u_ref[...]
        row = lax.broadcasted_iota(jnp.int32, x.shape, 0)
        c = _conv_pre(x, w_ref, b_ref, row)
        scale = jnp.where(pl.program_id(0) >= nq_blocks, DK_A ** -0.5, 1.0).astype(F32)
        o_ref[...] = (c * _sigmoid(c) * scale).astype(BF16)

    return pl.pallas_call(
        body, name=name, out_shape=jax.ShapeDtypeStruct((S, D), BF16), grid=(D // tc,),
        in_specs=[pl.BlockSpec((S, tc), lambda j: (0, j)),
                  pl.BlockSpec((CONV_A, tc), lambda j: (0, j)),
                  pl.BlockSpec((1, tc), lambda j: (0, j))],
        out_specs=pl.BlockSpec((S, tc), lambda j: (0, j)),
        compiler_params=_params("parallel"),
    )(u, conv_w, conv_b)


def _conv_bwd(u, dq, dk, conv_w, conv_b, du, *, S, D, name):
    tc = 256
    nq_blocks = (D // 2) // tc

    def body(u_ref, dq_ref, dk_ref, w_ref, b_ref, du_in, du_ref, dw_ref, db_ref):
        del du_in
        x = u_ref[...]
        n = x.shape[0]
        row = lax.broadcasted_iota(jnp.int32, x.shape, 0)
        c = _conv_pre(x, w_ref, b_ref, row)
        is_k = pl.program_id(0) >= nq_blocks
        dy = jnp.where(is_k, dk_ref[...] * (DK_A ** -0.5), dq_ref[...])
        sg = _sigmoid(c)
        dc = dy * (sg * (1.0 + c * (1.0 - sg)))
        db_ref[...] = jnp.sum(dc, axis=0, keepdims=True)
        dx = w_ref[CONV_A - 1:CONV_A, :] * dc
        dw_ref[CONV_A - 1:CONV_A, :] = jnp.sum(dc * x, axis=0, keepdims=True)
        for k in range(CONV_A - 1):
            d = CONV_A - 1 - k
            dw_ref[k:k + 1, :] = jnp.sum(dc * _shift_down(x, d, row), axis=0, keepdims=True)
            dx = dx + w_ref[k:k + 1, :] * _shift_up(dc, d, row, n)
        du_ref[...] = dx.astype(BF16)

    half = lambda j: (0, j % nq_blocks)
    return pl.pallas_call(
        body, name=name,
        out_shape=(jax.ShapeDtypeStruct(du.shape, du.dtype),
                   jax.ShapeDtypeStruct((CONV_A, D), F32), jax.ShapeDtypeStruct((1, D), F32)),
        grid=(D // tc,),
        in_specs=[pl.BlockSpec((S, tc), lambda j: (0, j)),
                  pl.BlockSpec((S, tc), half), pl.BlockSpec((S, tc), half),
                  pl.BlockSpec((CONV_A, tc), lambda j: (0, j)),
                  pl.BlockSpec((1, tc), lambda j: (0, j)), ANY],
        out_specs=(pl.BlockSpec((S, tc), lambda j: (0, j)),
                   pl.BlockSpec((CONV_A, tc), lambda j: (0, j)),
                   pl.BlockSpec((1, tc), lambda j: (0, j))),
        input_output_aliases={5: 0},
        compiler_params=_params("parallel"),
    )(u, dq, dk, conv_w, conv_b, du)


def _tri(n, cmp):
    r = lax.broadcasted_iota(jnp.int32, (n, n), 0)
    c = lax.broadcasted_iota(jnp.int32, (n, n), 1)
    return r, c, cmp(r, c)


def _gates_fwd(gt, bias, *, nha, nb, name):
    half = nha * nb
    nb_shift = nb.bit_length() - 1
    assert nb == 1 << nb_shift

    def body(g_ref, b_ref, f_ref, brow_ref):
        ig = g_ref[0:half, :] + b_ref[0:half, :]
        fg = g_ref[half:2 * half, :] + b_ref[half:2 * half, :]
        lf = _log_sigmoid(fg)
        _, _, upper = _tri(LANES, lambda r, c: r <= c)
        cs = _mask_dot3(lf, upper.astype(BF16))
        tot = jnp.broadcast_to(cs[:, LANES - 1:LANES], cs.shape)
        r, c, _ = _tri(half, lambda r, c: r <= c)
        before = jnp.logical_and(r >> nb_shift == c >> nb_shift, c < r).astype(BF16)
        f = cs + _mask_dot3_left(before, tot)
        f_ref[...] = f
        brow_ref[...] = ig - f

    return pl.pallas_call(
        body, name=name,
        out_shape=(jax.ShapeDtypeStruct((half, LANES), F32), jax.ShapeDtypeStruct((half, LANES), F32)),
    )(gt, bias)


def _gates_bwd(rowsum, colsum, gt, bias, *, nha, nb, name):
    half = nha * nb
    nb_shift = nb.bit_length() - 1
    assert nb == 1 << nb_shift

    def body(rs_ref, cs_ref, g_ref, b_ref, dg_ref, tot_ref):
        col = cs_ref[...]
        df = rs_ref[...] - col
        _, _, lower = _tri(LANES, lambda r, c: r >= c)
        rc = _mask_dot3(df, lower.astype(BF16))
        tot = jnp.broadcast_to(rc[:, 0:1], rc.shape)
        r, c, _ = _tri(half, lambda r, c: r <= c)
        same = r >> nb_shift == c >> nb_shift
        after = jnp.logical_and(same, c > r).astype(BF16)
        dlf = rc + _mask_dot3_left(after, tot)
        fg = g_ref[half:2 * half, :] + b_ref[half:2 * half, :]
        dfg = dlf * _sigmoid(-fg)
        dg_ref[0:half, :] = col
        dg_ref[half:2 * half, :] = dfg
        grp = same.astype(BF16)
        ones = jnp.ones((LANES, LANES), BF16)
        tot_ref[0:half, :] = _mask_dot3_left(grp, _mask_dot3(col, ones))
        tot_ref[half:2 * half, :] = _mask_dot3_left(grp, _mask_dot3(dfg, ones))

    return pl.pallas_call(
        body, name=name,
        out_shape=(jax.ShapeDtypeStruct((2 * half, LANES), F32), jax.ShapeDtypeStruct((2 * half, LANES), F32)),
    )(rowsum, colsum, gt, bias)


def _mlstm_tile(q, k_ref, fcol, brow_ref, j, tq, diagonal):
    off = pl.multiple_of(j * tq, tq)
    kj = k_ref[pl.ds(off, tq), :]
    s = _dot_nt(q, kj)
    logd = fcol + brow_ref[0, :, pl.ds(off, tq)]
    if diagonal:
        valid = lax.broadcasted_iota(jnp.int32, (tq, tq), 1) <= lax.broadcasted_iota(jnp.int32, (tq, tq), 0)
        logd = jnp.where(valid, logd, NEG_BIG)
    return off, kj, s, logd


def _mlstm_fwd(qk, u, fcol, brow, head_g, *, S, D, nha, name):
    tq = min(ML_TQ, S)
    nq = S // tq
    kb, vb = (D // 2) // DK_A, D // DV_A

    def body(q_ref, k_ref, v_ref, fcol_ref, brow_ref, o_ref, z_ref, g_ref, h_ref, hg_ref, m_ref, den_ref):
        i = pl.program_id(1)
        q = q_ref[...]
        fc = fcol_ref[0]

        def step(j, carry, diagonal):
            acc, den, m = carry
            off, _, s, logd = _mlstm_tile(q, k_ref, fc, brow_ref, j, tq, diagonal)
            m_new = jnp.maximum(m, jnp.max(logd, axis=1, keepdims=True))
            a = s * jnp.exp(logd - m_new)
            alpha = jnp.exp(m - m_new)
            vj = v_ref[pl.ds(off, tq), :].astype(BF16)
            acc = alpha * acc + _dot_nn(a.astype(BF16), vj)
            den = alpha * den + jnp.sum(a, axis=1, keepdims=True)
            return acc, den, m_new

        first = step(i, (jnp.zeros((tq, DV_A), F32), jnp.zeros((tq, 1), F32), jnp.full((tq, 1), NEG_BIG, F32)), True)
        acc, den, m = lax.fori_loop(0, i, lambda j, c: step(j, c, False), first)
        hh = acc / jnp.maximum(jnp.abs(den), jnp.exp(-m))
        h_ref[...] = hh
        hn, _ = _head_norm(hh)
        z = z_ref[...]
        hg_ref[...] = (_sigmoid(o_ref[...]) * (hn * g_ref[...]) * (z * _sigmoid(z))).astype(BF16)
        m_ref[0] = m
        den_ref[0] = den

    stat = pl.BlockSpec((1, tq, 1), lambda h, i: (h, i, 0))
    hblk = pl.BlockSpec((tq, DV_A), lambda h, i: (i, h))
    return pl.pallas_call(
        body, name=name,
        out_shape=(jax.ShapeDtypeStruct((S, D), F32), jax.ShapeDtypeStruct((S, D), BF16),
                   jax.ShapeDtypeStruct((nha, S, 1), F32), jax.ShapeDtypeStruct((nha, S, 1), F32)),
        grid=(nha, nq),
        in_specs=[pl.BlockSpec((tq, DK_A), lambda h, i: (i, h)),
                  pl.BlockSpec((S, DK_A), lambda h, i: (0, kb + h)),
                  pl.BlockSpec((S, DV_A), lambda h, i: (0, vb + h)),
                  stat, pl.BlockSpec((1, 1, S), lambda h, i: (h, 0, 0)),
                  pl.BlockSpec((tq, DV_A), lambda h, i: (i, 2 * vb + h)),
                  pl.BlockSpec((tq, DV_A), lambda h, i: (i, 3 * vb + h)),
                  pl.BlockSpec((1, DV_A), lambda h, i: (0, h))],
        out_specs=(hblk, hblk, stat, stat),
        compiler_params=_params("parallel", "arbitrary"),
    )(qk, qk, u, fcol, brow, u, u, head_g)


def _mlstm_bwd(qk, u, fcol, brow, m, den, dh, h, du, *, S, D, nha, name):
    tq = min(ML_TQ, S)
    nq = S // tq
    kb, vb = (D // 2) // DK_A, D // DV_A

    def body(q_ref, k_ref, v_ref, fcol_ref, brow_ref, m_ref, den_ref, dh_ref, h_ref, du_in,
             du_ref, dq_ref, dk_ref, rs_ref, cs_ref, dv_acc):
        del du_in
        i = pl.program_id(1)

        @pl.when(i == 0)
        def _():
            dk_ref[...] = jnp.zeros_like(dk_ref)
            cs_ref[...] = jnp.zeros_like(cs_ref)
            dv_acc[...] = jnp.zeros_like(dv_acc)

        q = q_ref[...]
        fc = fcol_ref[0]
        mm = m_ref[0]
        dn = den_ref[0]
        floor = jnp.exp(-mm)
        nrm = jnp.maximum(jnp.abs(dn), floor)
        dhv = dh_ref[...]
        dnum = dhv / nrm
        dnrm = -jnp.sum(dhv * h_ref[...], axis=1, keepdims=True) / nrm
        dden = jnp.where(jnp.abs(dn) > floor, jnp.where(dn > 0.0, dnrm, -dnrm), 0.0)
        dnum_b = dnum.astype(BF16)

        def step(j, carry, diagonal):
            dq, rs = carry
            off, kj, s, logd = _mlstm_tile(q, k_ref, fc, brow_ref, j, tq, diagonal)
            p = jnp.exp(logd - mm)
            a = s * p
            vj = v_ref[pl.ds(off, tq), :].astype(BF16)
            da = _dot_nt(dnum_b, vj) + dden
            dv_acc[pl.ds(off, tq), :] += _dot_tn(a.astype(BF16), dnum_b)
            dqk = (da * p).astype(BF16)
            dq = dq + _dot_nn(dqk, kj)
            dk_ref[pl.ds(off, tq), :] += _dot_tn(dqk, q)
            pm = da * a
            cs_ref[0, :, pl.ds(off, tq)] += jnp.sum(pm, axis=0, keepdims=True)
            rs = rs + jnp.sum(pm, axis=1, keepdims=True)
            return dq, rs

        first = step(i, (jnp.zeros((tq, DK_A), F32), jnp.zeros((tq, 1), F32)), True)
        dq, rs = lax.fori_loop(0, i, lambda j, c: step(j, c, False), first)
        dq_ref[...] = dq
        rs_ref[0] = rs

        @pl.when(i == nq - 1)
        def _():
            du_ref[...] = dv_acc[...].astype(BF16)

    stat = pl.BlockSpec((1, tq, 1), lambda h, i: (h, i, 0))
    rowv = pl.BlockSpec((1, 1, S), lambda h, i: (h, 0, 0))
    hblk = pl.BlockSpec((tq, DV_A), lambda h, i: (i, h))
    return pl.pallas_call(
        body, name=name,
        out_shape=(jax.ShapeDtypeStruct(du.shape, du.dtype),
                   jax.ShapeDtypeStruct((S, D // 2), F32), jax.ShapeDtypeStruct((S, D // 2), F32),
                   jax.ShapeDtypeStruct((nha, S, 1), F32), jax.ShapeDtypeStruct((nha, 1, S), F32)),
        grid=(nha, nq),
        in_specs=[pl.BlockSpec((tq, DK_A), lambda h, i: (i, h)),
                  pl.BlockSpec((S, DK_A), lambda h, i: (0, kb + h)),
                  pl.BlockSpec((S, DV_A), lambda h, i: (0, vb + h)),
                  stat, rowv, stat, stat, hblk, hblk, ANY],
        out_specs=(pl.BlockSpec((S, DV_A), lambda h, i: (0, vb + h)),
                   pl.BlockSpec((tq, DK_A), lambda h, i: (i, h)),
                   pl.BlockSpec((S, DK_A), lambda h, i: (0, h)),
                   stat, rowv),
        scratch_shapes=[pltpu.VMEM((S, DV_A), F32)],
        input_output_aliases={9: 0},
        compiler_params=_params("parallel", "arbitrary"),
    )(qk, qk, u, fcol, brow, m, den, dh, h, du)


def _head_norm(hh):
    mu = jnp.mean(hh, axis=1, keepdims=True)
    hc = hh - mu
    rstd = lax.rsqrt(jnp.mean(hc * hc, axis=1, keepdims=True) + LN_EPS)
    return hc * rstd, rstd


def _hgate_bwd(dhg, h, u, head_g, du, *, S, D, name):
    tm = min(256, S)
    nh = D // DV_A

    def body(dhg_ref, h_ref, o_ref, z_ref, g_ref, du_in, du_ref, dh_ref, dg_ref):
        del du_in

        @pl.when(pl.program_id(0) == 0)
        def _():
            dg_ref[...] = jnp.zeros_like(dg_ref)

        for hd in range(nh):
            sl = slice(hd * DV_A, (hd + 1) * DV_A)
            hn, rstd = _head_norm(h_ref[:, sl])
            o, z, g, d = o_ref[:, sl], z_ref[:, sl], g_ref[:, sl], dhg_ref[:, sl]
            so, sz = _sigmoid(o), _sigmoid(z)
            silu_z = z * sz
            hng = hn * g
            du_ref[:, sl] = (d * hng * silu_z * so * (1.0 - so)).astype(BF16)
            du_ref[:, D + hd * DV_A:D + (hd + 1) * DV_A] = (
                d * so * hng * (sz * (1.0 + z * (1.0 - sz)))).astype(BF16)
            t = d * so * silu_z
            dg_ref[:, sl] += jnp.sum(t * hn, axis=0, keepdims=True)
            dhn = t * g
            dh_ref[:, sl] = rstd * (dhn - jnp.mean(dhn, axis=1, keepdims=True)
                                    - hn * jnp.mean(dhn * hn, axis=1, keepdims=True))

    row = lambda i: (i, 0)
    return pl.pallas_call(
        body, name=name,
        out_shape=(jax.ShapeDtypeStruct(du.shape, du.dtype), jax.ShapeDtypeStruct((S, D), F32),
                   jax.ShapeDtypeStruct((1, D), F32)),
        grid=(S // tm,),
        in_specs=[pl.BlockSpec((tm, D), row), pl.BlockSpec((tm, D), row),
                  pl.BlockSpec((tm, D), lambda i: (i, 2)), pl.BlockSpec((tm, D), lambda i: (i, 3)),
                  pl.BlockSpec((1, D), lambda i: (0, 0)), ANY],
        out_specs=(pl.BlockSpec((tm, 2 * D), lambda i: (i, 1)), pl.BlockSpec((tm, D), row),
                   pl.BlockSpec((1, D), lambda i: (0, 0))),
        input_output_aliases={5: 0},
        compiler_params=_params("arbitrary"),
    )(dhg, h, u, u, head_g, du)


def _ln_stats(r):
    mu = jnp.mean(r, axis=1, keepdims=True)
    xc = r - mu
    rstd = lax.rsqrt(jnp.mean(xc * xc, axis=1, keepdims=True) + LN_EPS)
    return xc * rstd, rstd


def _ln_back(dxhat, xhat, rstd):
    return rstd * (dxhat - jnp.mean(dxhat, axis=1, keepdims=True)
                   - xhat * jnp.mean(dxhat * xhat, axis=1, keepdims=True))


def _ln_fwd(x, y, g, b, *, S, D, name):
    tm = min(256, S)

    def body(x_ref, y_ref, g_ref, b_ref, o_ref, ob_ref):
        xhat, _ = _ln_stats(ALPHA * x_ref[...] + y_ref[...])
        o = xhat * g_ref[...] + b_ref[...]
        o_ref[...] = o
        ob_ref[...] = o.astype(BF16)

    row = lambda i: (i, 0)
    vec = pl.BlockSpec((1, D), lambda i: (0, 0))
    return pl.pallas_call(
        body, name=name,
        out_shape=(jax.ShapeDtypeStruct((S, D), F32), jax.ShapeDtypeStruct((S, D), BF16)),
        grid=(S // tm,),
        in_specs=[pl.BlockSpec((tm, D), row), pl.BlockSpec((tm, D), row), vec, vec],
        out_specs=(pl.BlockSpec((tm, D), row), pl.BlockSpec((tm, D), row)),
        compiler_params=_params("parallel"),
    )(x, y, g, b)


def _ln_loss_bwd(x1, y2, target, g, b, *, S, D, name):
    tm = min(256, S)

    def body(x_ref, y_ref, t_ref, g_ref, b_ref, dr_ref, drb_ref, dg_ref, db_ref, loss_ref):
        @pl.when(pl.program_id(0) == 0)
        def _():
            dg_ref[...] = jnp.zeros_like(dg_ref)
            db_ref[...] = jnp.zeros_like(db_ref)
            loss_ref[...] = jnp.zeros_like(loss_ref)

        xhat, rstd = _ln_stats(ALPHA * x_ref[...] + y_ref[...])
        diff = xhat * g_ref[...] + b_ref[...] - t_ref[...]
        loss_ref[...] += (0.5 / D) * jnp.sum(diff * diff)
        dx2 = diff * (1.0 / D)
        dg_ref[...] += jnp.sum(dx2 * xhat, axis=0, keepdims=True)
        db_ref[...] += jnp.sum(dx2, axis=0, keepdims=True)
        dr = _ln_back(dx2 * g_ref[...], xhat, rstd)
        dr_ref[...] = dr
        drb_ref[...] = dr.astype(BF16)

    row = lambda i: (i, 0)
    vec = pl.BlockSpec((1, D), lambda i: (0, 0))
    return pl.pallas_call(
        body, name=name,
        out_shape=(jax.ShapeDtypeStruct((S, D), F32), jax.ShapeDtypeStruct((S, D), BF16),
                   jax.ShapeDtypeStruct((1, D), F32), jax.ShapeDtypeStruct((1, D), F32),
                   jax.ShapeDtypeStruct((1, LANES), F32)),
        grid=(S // tm,),
        in_specs=[pl.BlockSpec((tm, D), row)] * 3 + [vec, vec],
        out_specs=(pl.BlockSpec((tm, D), row), pl.BlockSpec((tm, D), row), vec, vec,
                   pl.BlockSpec((1, LANES), lambda i: (0, 0))),
        compiler_params=_params("arbitrary"),
    )(x1, y2, target, g, b)


def _ln_bwd(x, y, g, dout, *, S, D, name):
    tm = min(256, S)

    def body(x_ref, y_ref, g_ref, d_ref, dr_ref, drb_ref, dg_ref, db_ref):
        @pl.when(pl.program_id(0) == 0)
        def _():
            dg_ref[...] = jnp.zeros_like(dg_ref)
            db_ref[...] = jnp.zeros_like(db_ref)

        xhat, rstd = _ln_stats(ALPHA * x_ref[...] + y_ref[...])
        d = d_ref[...]
        dg_ref[...] += jnp.sum(d * xhat, axis=0, keepdims=True)
        db_ref[...] += jnp.sum(d, axis=0, keepdims=True)
        dr = _ln_back(d * g_ref[...], xhat, rstd)
        dr_ref[...] = dr
        drb_ref[...] = dr.astype(BF16)

    row = lambda i: (i, 0)
    vec = pl.BlockSpec((1, D), lambda i: (0, 0))
    return pl.pallas_call(
        body, name=name,
        out_shape=(jax.ShapeDtypeStruct((S, D), F32), jax.ShapeDtypeStruct((S, D), BF16),
                   jax.ShapeDtypeStruct((1, D), F32), jax.ShapeDtypeStruct((1, D), F32)),
        grid=(S // tm,),
        in_specs=[pl.BlockSpec((tm, D), row), pl.BlockSpec((tm, D), row), vec, pl.BlockSpec((tm, D), row)],
        out_specs=(pl.BlockSpec((tm, D), row), pl.BlockSpec((tm, D), row), vec, vec),
        compiler_params=_params("arbitrary"),
    )(x, y, g, dout)


def _sb_scores(q, kj, diagonal, t):
    z = _dot_nt(q, kj)
    ls = _log_sigmoid(z)
    if not diagonal:
        return None, ls, ls - z
    valid = lax.broadcasted_iota(jnp.int32, (t, t), 1) < lax.broadcasted_iota(jnp.int32, (t, t), 0)
    return valid, ls, jnp.where(valid, ls - z, 0.0)


def _keep(valid, x):
    return x if valid is None else jnp.where(valid, x, 0.0)


def _sb_fwd(q2, kv, z2, *, S, D, name, t=SB_TILE):
    t = min(t, S)
    nq = S // t
    nh = D // DH_B

    def body(q_ref, k_ref, v_ref, z_ref, o_ref, hb_ref, tot_ref):
        i = pl.program_id(1)
        q = q_ref[...]
        _, _, after = _tri(min(t, MXU_DEPTH), lambda r, c: r > c)
        tri_after = after.astype(BF16)

        def step(j, carry, diagonal):
            acc, cr = carry
            off = pl.multiple_of(j * t, t)
            valid, ls, lneg = _sb_scores(q, k_ref[pl.ds(off, t), :], diagonal, t)
            between = cr + _block_sums(lneg, tri_after, True)
            a = _keep(valid, jnp.exp(ls + between))
            acc = acc + _dot_nn(a.astype(BF16), v_ref[pl.ds(off, t), :])
            return acc, cr + jnp.sum(lneg, axis=1, keepdims=True)

        carry = step(i, (jnp.zeros((t, DH_B), F32), jnp.zeros((t, 1), F32)), True)
        acc, cr = lax.fori_loop(0, i, lambda jj, c: step(i - 1 - jj, c, False), carry)
        o_ref[...] = acc
        z = z_ref[...]
        hb_ref[...] = (acc * (z * _sigmoid(z))).astype(BF16)
        tot_ref[0] = cr

    blk = pl.BlockSpec((t, DH_B), lambda h, i: (i, h))
    return pl.pallas_call(
        body, name=name,
        out_shape=(jax.ShapeDtypeStruct((S, D), F32), jax.ShapeDtypeStruct((S, D), BF16),
                   jax.ShapeDtypeStruct((nh, S, 1), F32)), grid=(nh, nq),
        in_specs=[blk, pl.BlockSpec((S, DH_B), lambda h, i: (0, h)),
                  pl.BlockSpec((S, DH_B), lambda h, i: (0, nh + h)), blk],
        out_specs=(blk, blk, pl.BlockSpec((1, t, 1), lambda h, i: (h, i, 0))),
        compiler_params=_params("parallel", "arbitrary"),
    )(q2, kv, kv, z2)


def _sb_bwd(q2, kv, dhb, att, z2, tot, *, S, D, name, t=SB_TILE):
    t = min(t, S)
    nq = S // t
    nh = D // DH_B

    def body(q_ref, k_ref, v_ref, dhb_ref, att_ref, z_ref, tot_ref, dq_ref, dk_ref, dv_ref, dz2_ref, dk_acc, dv_acc):
        i = pl.program_id(1)

        @pl.when(i == 0)
        def _():
            dk_acc[...] = jnp.zeros_like(dk_acc)
            dv_acc[...] = jnp.zeros_like(dv_acc)

        q = q_ref[...]
        z, dhb = z_ref[...], dhb_ref[...]
        sz = _sigmoid(z)
        dz2_ref[...] = (dhb * att_ref[...] * (sz * (1.0 + z * (1.0 - sz)))).astype(BF16)
        do_b = (dhb * (z * sz)).astype(BF16)
        _, _, after = _tri(min(t, MXU_DEPTH), lambda r, c: r > c)
        tri_after = after.astype(BF16)
        _, _, before = _tri(min(t, MXU_DEPTH), lambda r, c: r < c)
        tri_before = before.astype(BF16)

        def step(j, carry, diagonal):
            dq, rest, cg = carry
            off = pl.multiple_of(j * t, t)
            kj = k_ref[pl.ds(off, t), :]
            valid, ls, lneg = _sb_scores(q, kj, diagonal, t)
            rest = rest - jnp.sum(lneg, axis=1, keepdims=True)
            between = rest + _block_sums(lneg, tri_after, True)
            a = _keep(valid, jnp.exp(ls + between))
            g = _dot_nt(do_b, v_ref[pl.ds(off, t), :]) * a
            dv_acc[pl.ds(off, t), :] += _dot_tn(a.astype(BF16), do_b)
            e = cg + _block_sums(g, tri_before, False, passes=1)
            sig = jnp.exp(ls)
            dz = _keep(valid, g * (1.0 - sig) - e * sig)
            dz_b = dz.astype(BF16)
            dq = dq + _dot_nn(dz_b, kj)
            dk_acc[pl.ds(off, t), :] += _dot_tn(dz_b, q)
            return dq, rest, cg + jnp.sum(g, axis=1, keepdims=True)

        carry = lax.fori_loop(0, i, lambda j, c: step(j, c, False),
                              (jnp.zeros((t, DH_B), F32), tot_ref[0], jnp.zeros((t, 1), F32)))
        dq, _, _ = step(i, carry, True)
        dq_ref[...] = (dq * (DH_B ** -0.5)).astype(BF16)

        @pl.when(i == nq - 1)
        def _():
            dk_ref[...] = dk_acc[...].astype(BF16)
            dv_ref[...] = dv_acc[...].astype(BF16)

    blk = pl.BlockSpec((t, DH_B), lambda h, i: (i, h))
    sd = jax.ShapeDtypeStruct((S, D), BF16)
    return pl.pallas_call(
        body, name=name, out_shape=(sd, sd, sd, sd), grid=(nh, nq),
        in_specs=[blk, pl.BlockSpec((S, DH_B), lambda h, i: (0, h)),
                  pl.BlockSpec((S, DH_B), lambda h, i: (0, nh + h)), blk, blk, blk,
                  pl.BlockSpec((1, t, 1), lambda h, i: (h, i, 0))],
        out_specs=(blk, pl.BlockSpec((S, DH_B), lambda h, i: (0, h)),
                   pl.BlockSpec((S, DH_B), lambda h, i: (0, h)), blk),
        scratch_shapes=[pltpu.VMEM((S, DH_B), F32), pltpu.VMEM((S, DH_B), F32)],
        compiler_params=_params("parallel", "arbitrary"),
    )(q2, kv, kv, dhb, att, z2, tot)


def _adamw_math(w, g, m, v):
    mn = ADAM_B1 * m + (1.0 - ADAM_B1) * g
    vn = ADAM_B2 * v + (1.0 - ADAM_B2) * (g * g)
    m_hat = mn / (1.0 - ADAM_B1 ** ADAM_STEP)
    v_hat = vn / (1.0 - ADAM_B2 ** ADAM_STEP)
    return -ADAM_LR * (m_hat / (jnp.sqrt(v_hat) + ADAM_EPS) + ADAM_WD * w), mn, vn


def _adamw(w, g, m, v, *, name, tr=128):
    rows, cols = w.shape
    tr = min(tr, rows)
    assert rows % tr == 0

    def body(w_ref, g_ref, m_ref, v_ref, d_ref, mo_ref, vo_ref):
        d, mn, vn = _adamw_math(w_ref[...], g_ref[...], m_ref[...], v_ref[...])
        d_ref[...] = d
        mo_ref[...] = mn
        vo_ref[...] = vn

    blk = pl.BlockSpec((tr, cols), lambda i: (i, 0))
    sd = jax.ShapeDtypeStruct((rows, cols), F32)
    return pl.pallas_call(
        body, name=name, out_shape=(sd, sd, sd), grid=(rows // tr,),
        in_specs=[blk] * 4, out_specs=(blk, blk, blk),
        compiler_params=_params("parallel"),
    )(w, g, m, v)


def _local_step(x, target, wa, wg, wao, wbi, wkv, wbo, gate_b, conv_w, conv_b, head_g,
                a_ln_g, a_ln_b, b_ln_g, b_ln_b, *, S, D, nha, rs):
    nb = S // LANES
    kw = dict(S=S, D=D)
    xb = x.astype(BF16)
    wa_chunk = D if wa.ndim == 3 else None
    u = _mm(xb, wa, mode="nn", M=S, N=4 * D, K=D, out_dtype=F32, name="a_in", b_chunk=wa_chunk)
    ug = _mm(xb, wg, mode="nn", M=S, N=LANES, K=D, out_dtype=F32, name="a_in_gates")
    qk = _conv_fwd(u, conv_w, conv_b, name="conv_fwd", **kw)
    gt = ug[:, :2 * nha].T.reshape(2 * nha * nb, LANES)
    gbias = jnp.repeat(gate_b.reshape(2 * nha), nb).reshape(2 * nha * nb, 1)
    fcs, brow = _gates_fwd(gt, gbias, nha=nha, nb=nb, name="gates_fwd")
    fcol = fcs.reshape(nha, S, 1)
    brow = brow.reshape(nha, 1, S)
    h, hg, m, den = _mlstm_fwd(qk, u, fcol, brow, head_g, nha=nha, name="mlstm_fwd", **kw)
    y = _mm(hg, wao, mode="nn", M=S, N=D, K=D, out_dtype=F32, name="a_out")
    x1, x1b = _ln_fwd(x, y, a_ln_g, a_ln_b, name="ln_a_fwd", **kw)
    q2 = _mm(x1b, wbi, mode="nn", M=S, N=D, K=D, out_dtype=BF16, name="b_in_q", out_scale=DH_B ** -0.5)
    z2 = _mm(x1b, wbi, mode="nn", M=S, N=D, K=D, out_dtype=F32, name="b_in_z", b_off=(0, D))
    kv = _mm(x1b, wkv, mode="nn", M=S, N=2 * D, K=D, out_dtype=BF16, name="b_kv")
    att, hb, sb_tot = _sb_fwd(q2, kv, z2, name="sb_fwd", **kw)
    y2 = _mm(hb, wbo, mode="nn", M=S, N=D, K=D, out_dtype=F32, name="b_out")
    dr2, dr2b, d_bln_g, d_bln_b, loss = _ln_loss_bwd(x1, y2, target, b_ln_g, b_ln_b, name="ln_b_loss", **kw)
    g_wbo = _mm(hb, dr2b, mode="tn", M=D, N=D, K=S, out_dtype=BF16, name="g_b_out")
    dhb = _mm(dr2b, wbo, mode="nt", M=S, N=D, K=D, out_dtype=F32, name="d_b_out")
    dq2, dk2, dv2, dz2 = _sb_bwd(q2, kv, dhb, att, z2, sb_tot, name="sb_bwd", **kw)
    g_wbi = _mm(x1b, dq2, mode="tn", M=D, N=D, K=S, out_dtype=BF16, name="g_b_in_q",
                into=lax.empty((D, 2 * D), BF16))
    g_wbi = _mm(x1b, dz2, mode="tn", M=D, N=D, K=S, out_dtype=BF16, name="g_b_in_z", into=g_wbi, out_off=(0, D))
    g_wkv = _mm(x1b, dk2, mode="tn", M=D, N=D, K=S, out_dtype=BF16, name="g_kv_k",
                into=lax.empty((D, 2 * D), BF16))
    g_wkv = _mm(x1b, dv2, mode="tn", M=D, N=D, K=S, out_dtype=BF16, name="g_kv_v", into=g_wkv, out_off=(0, D))
    dq2 = rs.begin("b", dict(wbo=g_wbo, wbi=g_wbi, wkv=g_wkv), dq2)
    dx1 = _mm(dq2, wbi, mode="nt", M=S, N=D, K=D, out_dtype=F32, name="d_b_in_q", acc_in=dr2, acc_scale=ALPHA)
    dx1 = _mm(dz2, wbi, mode="nt", M=S, N=D, K=D, out_dtype=F32, name="d_b_in_z", b_off=(0, D), acc_in=dx1)
    dx1 = _mm(dk2, wkv, mode="nt", M=S, N=D, K=D, out_dtype=F32, name="d_kv_k", acc_in=dx1)
    dx1 = _mm(dv2, wkv, mode="nt", M=S, N=D, K=D, out_dtype=F32, name="d_kv_v", b_off=(0, D), acc_in=dx1)
    dx1 = rs.pair("b", dx1)
    dr, drb, d_aln_g, d_aln_b = _ln_bwd(x, y, a_ln_g, dx1, name="ln_a_bwd", **kw)
    g_wao = _mm(hg, drb, mode="tn", M=D, N=D, K=S, out_dtype=BF16, name="g_a_out")
    drb = rs.begin("a", dict(wao=g_wao), drb)
    dhg = _mm(drb, wao, mode="nt", M=S, N=D, K=D, out_dtype=F32, name="d_a_out")
    du = lax.empty((S, 4 * D), BF16)
    du, dh, d_head_g = _hgate_bwd(dhg, h, u, head_g, du, name="hgate_bwd", **kw)
    dh = rs.pair("a", dh)
    du, dq, dk, rowsum, colsum = _mlstm_bwd(qk, u, fcol, brow, m, den, dh, h, du, nha=nha, name="mlstm_bwd", **kw)
    dgt, dgtot = _gates_bwd(rowsum.reshape(nha * nb, LANES), colsum.reshape(nha * nb, LANES), gt, gbias,
                            nha=nha, nb=nb, name="gates_bwd")
    d_gate_b = dgtot[::nb, 0].reshape(1, 2 * nha)
    dgp = jnp.pad(dgt.reshape(2 * nha, S).T, ((0, 0), (0, LANES - 2 * nha))).astype(BF16)
    du, d_conv_w, d_conv_b = _conv_bwd(u, dq, dk, conv_w, conv_b, du, name="conv_bwd", **kw)
    small = dict(gate_b=d_gate_b, conv_w=d_conv_w, conv_b=d_conv_b, head_g=d_head_g,
                 a_ln_g=d_aln_g, a_ln_b=d_aln_b, b_ln_g=d_bln_g, b_ln_b=d_bln_b)
    du = rs.chip("b", du)
    g_wa = _mm(xb, du, mode="tn", M=D, N=4 * D, K=S, out_dtype=BF16, name="g_a_in",
               into=lax.empty((D, 4 * D + LANES), BF16))
    g_wa = _mm(xb, dgp, mode="tn", M=D, N=LANES, K=S, out_dtype=BF16, name="g_a_in_gates", into=g_wa,
               out_off=(0, 4 * D))
    du = rs.begin("w", dict(wa_full=g_wa), du)
    du = rs.pair("w", rs.chip("a", rs.small(small, loss, du)))
    du = rs.finish("a", rs.finish("b", du))
    dx = _mm(du, wa, mode="nt", M=S, N=D, K=4 * D, out_dtype=F32, name="d_a_in", acc_in=dr, acc_scale=ALPHA,
             b_chunk=wa_chunk)
    dx = _mm(dgp, wg, mode="nt", M=S, N=D, K=LANES, out_dtype=F32, name="d_a_in_gates", acc_in=dx)
    return dx


def _coords():
    return lax.axis_index("x"), lax.axis_index("y"), lax.axis_index("c")


def _other_chips(x, y):
    return [(1 - x, y), (x, 1 - y), (1 - x, 1 - y)]


def _rows(ref, start, size):
    return ref.at[pl.ds(pl.multiple_of(start, size), size), :]


def _window(kind, ref, shard_shape, j, hf=None, quarter=None):
    r, cw = shard_shape
    row0, nr = (0, r) if hf is None else (hf * (r // 2), r // 2)
    if quarter is not None:
        row0, nr = row0 + quarter * (r // 4), r // 4
    if kind == "stack":
        return ref.at[j, pl.ds(pl.multiple_of(row0, nr), nr), :]
    if kind == "rows":
        return ref.at[pl.ds(pl.multiple_of(j * r + row0, nr), nr), :]
    assert kind == "cols"
    return ref.at[pl.ds(pl.multiple_of(row0, nr), nr), pl.ds(pl.multiple_of(j * cw, cw), cw)]


def _gathered_shape(kind, shard_shape):
    r, cw = shard_shape
    return {"stack": (N_CHIPS, r, cw), "rows": (N_CHIPS * r, cw), "cols": (r, N_CHIPS * cw)}[kind]


def _remote(src, dst, send_sems, recv_sems, k, to):
    return pltpu.make_async_remote_copy(src_ref=src, dst_ref=dst, send_sem=send_sems.at[k],
                                        recv_sem=recv_sems.at[k], device_id=to, device_id_type=MESH)


def _allgather_copies(s_refs, sm_ref, g_refs, smg_ref, send_sems, recv_sems, kinds, shapes):
    n = len(s_refs)
    x, y, c = _coords()
    me, sibling, chips = 2 * x + y, (x, y, 1 - c), _other_chips(x, y)
    ids = [2 * cx + cy for cx, cy in chips]
    d2d, own_base, small_base = 4 * n, 7 * n, 8 * n

    def win(a, j, hf=None, quarter=None):
        return _window(kinds[a], g_refs[a], shapes[a], j, hf, quarter)

    def small_win(j):
        w = sm_ref.shape[1]
        return smg_ref.at[:, pl.ds(pl.multiple_of(j * w, w), w)]

    def rc(src, dst, k, to):
        return _remote(src, dst, send_sems, recv_sems, k, to)

    first = [rc(s_refs[a], win(a, me), own_base + a, sibling) for a in range(n)]
    for a in range(n):
        src = _rows(s_refs[a], c * (shapes[a][0] // 2), shapes[a][0] // 2)
        first += [rc(src, win(a, me, c), 4 * a + k, (*chips[k], c)) for k in range(2)]
    if sm_ref is not None:
        first.append(rc(sm_ref, small_win(me), small_base + 3, sibling))
        first += [rc(sm_ref, small_win(me), small_base + k, (*chip, c)) for k, chip in enumerate(chips)]
    for cp in first:
        cp.start()
    passed = []
    for a in range(n):
        for k in range(2):
            rc(win(a, ids[k], c), win(a, ids[k], c), 4 * a + k, sibling).wait_recv()
            part = win(a, ids[k], c, quarter=k)
            passed.append(rc(part, part, 4 * a + 2 + k, (*chips[1 - k], c)))
            passed.append(rc(win(a, ids[k], c), win(a, ids[k], c), d2d + 3 * a + k, sibling))
            passed[-2].start()
            passed[-1].start()
    for a in range(n):
        for k in range(2):
            part = win(a, ids[2], c, quarter=k)
            rc(part, part, 4 * a + 2 + k, sibling).wait_recv()
        passed.append(rc(win(a, ids[2], c), win(a, ids[2], c), d2d + 3 * a + 2, sibling))
        passed[-1].start()
    if sm_ref is not None:
        for k in range(3):
            rc(small_win(ids[k]), small_win(ids[k]), small_base + k, sibling).wait_recv()
        rc(small_win(me), small_win(me), small_base + 3, sibling).wait_recv()
    for a in range(n):
        for k in range(3):
            rc(win(a, ids[k], 1 - c), win(a, ids[k], 1 - c), d2d + 3 * a + k, sibling).wait_recv()
        rc(win(a, me), win(a, me), own_base + a, sibling).wait_recv()
    for cp in first + passed:
        cp.wait_send()


def _peer_handshake(peers):
    barrier = pltpu.get_barrier_semaphore()
    for peer in peers:
        pl.semaphore_signal(barrier, inc=1, device_id=peer, device_id_type=MESH)
    pl.semaphore_wait(barrier, len(peers))


def _behind(copies, inputs, out_structs, nsem, peers, *, name, collective_id):
    hbm = pltpu.MemorySpace.HBM
    in_refs = [jax.new_ref(a, memory_space=hbm) for a in inputs]
    out_refs = [jax.empty_ref(st, memory_space=hbm) for st in out_structs]

    @pl.kernel(mesh=plsc.ScalarSubcoreMesh(axis_name="sequencer", num_cores=1), name=name,
               scratch_types=(pltpu.SemaphoreType.DMA((nsem,)), pltpu.SemaphoreType.DMA((nsem,))),
               compiler_params=pltpu.CompilerParams(collective_id=collective_id))
    def launch(send_sems, recv_sems):
        _peer_handshake(peers(*_coords()))
        copies(in_refs, out_refs, send_sems, recv_sems)

    launch()
    return [r[...] for r in out_refs]


def _sibling(x, y, c):
    return [(x, y, 1 - c)]


def _same_core_of_other_chips(x, y, c):
    return [(cx, cy, c) for cx, cy in _other_chips(x, y)]


def _sibling_and_other_chips(x, y, c):
    return _sibling(x, y, c) + _same_core_of_other_chips(x, y, c)


ID_ALL_PEERS, ID_SIBLING, ID_CHIPS = 0, 1, 2


def _allgather_behind(shards, kinds, small=None, *, name):
    n = len(shards)
    shapes = [sh.shape for sh in shards]

    def copies(in_refs, out_refs, send_sems, recv_sems):
        if small is None:
            _allgather_copies(in_refs, None, out_refs, None, send_sems, recv_sems, kinds, shapes)
        else:
            _allgather_copies(in_refs[:n], in_refs[n], out_refs[:n], out_refs[n], send_sems, recv_sems, kinds, shapes)

    outs = [jax.ShapeDtypeStruct(_gathered_shape(kinds[a], shapes[a]), shards[a].dtype) for a in range(n)]
    inputs = list(shards)
    if small is not None:
        outs.append(jax.ShapeDtypeStruct((small.shape[0], N_CHIPS * small.shape[1]), small.dtype))
        inputs.append(small)
    return _behind(copies, inputs, outs, 8 * n + 4, _sibling_and_other_chips, name=name, collective_id=ID_ALL_PEERS)


def _rs_pair_exchange(views, *, name):
    n = len(views)

    def copies(g_refs, p_refs, send_sems, recv_sems):
        x, y, c = _coords()
        cps = []
        for a in range(n):
            r2 = views[a].shape[1] // 2
            src = g_refs[a].at[:, pl.ds(pl.multiple_of((1 - c) * r2, r2), r2), :]
            cps.append(_remote(src, p_refs[a], send_sems, recv_sems, a, (x, y, 1 - c)))
        for cp in cps:
            cp.start()
        for cp in cps:
            cp.wait()

    outs = [jax.ShapeDtypeStruct((v.shape[0], v.shape[1] // 2, v.shape[2]), v.dtype) for v in views]
    return _behind(copies, views, outs, n, _sibling, name=name, collective_id=ID_SIBLING)


def _add_half(view, part, core, *, name):
    nch, r, cols = view.shape
    r2 = r // 2
    tr = min(512 if cols <= 4096 else 128, r2)
    nt = r2 // tr

    def body(c_ref, g_ref, p_ref, o_ref):
        del c_ref
        o_ref[...] = (g_ref[...].astype(F32) + p_ref[...].astype(F32)).astype(o_ref.dtype)

    return pl.pallas_call(
        body, name=name, out_shape=jax.ShapeDtypeStruct((nch, r2, cols), view.dtype),
        grid_spec=pltpu.PrefetchScalarGridSpec(
            num_scalar_prefetch=1, grid=(nch, nt),
            in_specs=[pl.BlockSpec((1, tr, cols), lambda ch, i, c_ref: (ch, c_ref[0] * nt + i, 0)),
                      pl.BlockSpec((1, tr, cols), lambda ch, i, c_ref: (ch, i, 0))],
            out_specs=pl.BlockSpec((1, tr, cols), lambda ch, i, c_ref: (ch, i, 0))),
        compiler_params=_params("parallel", "parallel"),
    )(core, view, part)


def _chunk(kind, ref, j, cw):
    if kind == "cols":
        return ref.at[0, :, pl.ds(pl.multiple_of(j * cw, cw), cw)]
    if isinstance(kind, tuple):
        return ref.at[0, :, pl.ds(pl.multiple_of(j * kind[1], LANES), cw)]
    return ref.at[j]


def _rs_chips(pairs, kinds, *, name):
    n = len(pairs)
    half_shapes = []
    for a in range(n):
        nch, r2, cols = pairs[a].shape
        if isinstance(kinds[a], tuple):
            half_shapes.append((r2, kinds[a][2]))
        else:
            half_shapes.append((r2, cols // N_CHIPS) if kinds[a] == "cols" else (r2, cols))

    def copies(q_refs, t_refs, send_sems, recv_sems):
        x, y, c = _coords()
        chips = _other_chips(x, y)
        sends = []
        for a in range(n):
            for k, (cx, cy) in enumerate(chips):
                src = _chunk(kinds[a], q_refs[a], 2 * cx + cy, half_shapes[a][1])
                sends.append(_remote(src, t_refs[a].at[k], send_sems, recv_sems, 3 * a + k, (cx, cy, c)))
        for cp in sends:
            cp.start()
        for a in range(n):
            for k in range(3):
                slot = t_refs[a].at[k]
                _remote(slot, slot, send_sems, recv_sems, 3 * a + k, (x, y, c)).wait_recv()
        for cp in sends:
            cp.wait_send()

    outs = [jax.ShapeDtypeStruct((3, *half_shapes[a]), pairs[a].dtype) for a in range(n)]
    return _behind(copies, pairs, outs, 3 * n, _same_core_of_other_chips, name=name, collective_id=ID_CHIPS)


def _sum_chips(pair, parts, chip, kind, *, name):
    _, r, cols = parts.shape
    tr = min(128, r)
    if kind == "cols":
        own_spec = pl.BlockSpec((1, tr, cols), lambda i, chip_ref: (0, i, chip_ref[0]))
    else:
        own_spec = pl.BlockSpec((1, tr, cols), lambda i, chip_ref: (chip_ref[0], i, 0))

    def body(chip_ref, q_ref, p_ref, o_ref):
        del chip_ref
        acc = q_ref[0].astype(F32)
        for s in range(3):
            acc = acc + p_ref[s].astype(F32)
        o_ref[...] = acc

    return pl.pallas_call(
        body, name=name, out_shape=jax.ShapeDtypeStruct((r, cols), F32),
        grid_spec=pltpu.PrefetchScalarGridSpec(
            num_scalar_prefetch=1, grid=(r // tr,),
            in_specs=[own_spec, pl.BlockSpec((3, tr, cols), lambda i, chip_ref: (0, i, 0))],
            out_specs=pl.BlockSpec((tr, cols), lambda i, chip_ref: (i, 0))),
        compiler_params=_params("parallel"),
    )(chip, pair, parts)


def _rs_share(halves, *, name):
    n = len(halves)

    def copies(h_refs, g_refs, send_sems, recv_sems):
        x, y, c = _coords()
        cps = [_remote(h_refs[a], g_refs[a], send_sems, recv_sems, a, (x, y, 1 - c)) for a in range(n)]
        for cp in cps:
            cp.start()
        for cp in cps:
            cp.wait()

    outs = [jax.ShapeDtypeStruct(h.shape, h.dtype) for h in halves]
    return _behind(copies, halves, outs, n, _sibling, name=name, collective_id=ID_SIBLING)


class _GradReducer:
    def __init__(self, w, m, v, long_name, core, chip_ix, as_views, small_step):
        self.w, self.m, self.v, self.long_name = w, m, v, long_name
        self.core, self.chip_ix, self.as_views, self.small_step = core, chip_ix, as_views, small_step
        self.groups, self.results, self.small_results = {}, {}, None

    def small(self, small, loss_row, tie):
        (small, loss_row), tie = lax.optimization_barrier(((small, loss_row), tie))
        self.small_results, tie = lax.optimization_barrier((self.small_step(small, loss_row), tie))
        return tie

    def begin(self, tag, grads, tie):
        keys, views, kinds = self.as_views(grads)
        views, tie = lax.optimization_barrier((views, tie))
        parts = _rs_pair_exchange(views, name="rs_pair_exchange_" + tag)
        self.groups[tag] = dict(keys=keys, views=views, kinds=kinds, parts=parts)
        return tie

    def pair(self, tag, tie):
        g = self.groups[tag]
        g["parts"], tie = lax.optimization_barrier((g["parts"], tie))
        pairs = [_add_half(v, p, self.core, name="pair_sum_" + self.long_name[k])
                 for v, p, k in zip(g["views"], g["parts"], g["keys"])]
        pairs, tie = lax.optimization_barrier((pairs, tie))
        g["pairs"] = pairs
        g["slots"] = _rs_chips(pairs, g["kinds"], name="rs_chips_" + tag)
        return tie

    def chip(self, tag, tie, update_mine=False):
        g = self.groups[tag]
        g["slots"], tie = lax.optimization_barrier((g["slots"], tie))
        halves = []
        for q, t, kd, k in zip(g["pairs"], g["slots"], g["kinds"], g["keys"]):
            if isinstance(kd, tuple):
                q = lax.dynamic_slice_in_dim(q[0], self.chip_ix[0] * kd[1], kd[2], axis=1)[None]
                halves.append(_sum_chips(q, t, jnp.zeros_like(self.chip_ix), "stack",
                                         name="chip_sum_" + self.long_name[k]))
            else:
                halves.append(_sum_chips(q, t, self.chip_ix, kd, name="chip_sum_" + self.long_name[k]))
        halves, tie = lax.optimization_barrier((halves, tie))
        g["halves"] = halves
        g["others"] = _rs_share(halves, name="rs_share_" + tag)
        if update_mine:
            mine = [_adamw_rows(self.w[k], self._shard(a, kd), self.m[k], self.v[k], self.core,
                                name="adamw_mine_" + self.long_name[k])
                    for k, a, kd in zip(g["keys"], halves, g["kinds"])]
            g["mine"], tie = lax.optimization_barrier((mine, tie))
        return tie

    def _shard(self, half, kind):
        if not isinstance(kind, tuple):
            return half
        return lax.dynamic_slice_in_dim(half, self.chip_ix[0] * kind[3], kind[4], axis=1)

    def finish(self, tag, tie):
        g = self.groups[tag]
        g["others"], tie = lax.optimization_barrier((g["others"], tie))
        if "mine" in g:
            out = [_adamw_rows(self.w[k], self._shard(b, kd), self.m[k], self.v[k], 1 - self.core, into=done,
                               name="adamw_theirs_" + self.long_name[k])
                   for k, b, kd, done in zip(g["keys"], g["others"], g["kinds"], g["mine"])]
        else:
            out = [_adamw_halves(self.w[k], a, b, self.m[k], self.v[k], self.core, name="adamw_" + self.long_name[k])
                   for k, a, b in zip(g["keys"], g["halves"], g["others"])]
        out, tie = lax.optimization_barrier((out, tie))
        self.results.update(zip(g["keys"], out))
        return tie


def _adamw_halves(w, mine, theirs, m, v, core, *, name, tr=128):
    rows, cols = w.shape
    hr = rows // 2
    tr = min(tr, hr)
    nt = hr // tr

    def body(c_ref, w_ref, a_ref, b_ref, m_ref, v_ref, g_ref, d_ref, mo_ref, vo_ref):
        gg = jnp.where(pl.program_id(0) == c_ref[0], a_ref[...], b_ref[...])
        g_ref[...] = gg
        d, mn, vn = _adamw_math(w_ref[...], gg, m_ref[...], v_ref[...])
        d_ref[...] = d
        mo_ref[...] = mn
        vo_ref[...] = vn

    full = pl.BlockSpec((tr, cols), lambda hf, i, c_ref: (hf * nt + i, 0))
    half = pl.BlockSpec((tr, cols), lambda hf, i, c_ref: (i, 0))
    sd = jax.ShapeDtypeStruct((rows, cols), F32)
    return pl.pallas_call(
        body, name=name, out_shape=(sd, sd, sd, sd),
        grid_spec=pltpu.PrefetchScalarGridSpec(
            num_scalar_prefetch=1, grid=(2, nt),
            in_specs=[full, half, half, full, full], out_specs=(full, full, full, full)),
        compiler_params=_params("parallel", "parallel"),
    )(core, w, mine, theirs, m, v)


def _adamw_rows(w, g_half, m, v, half, *, name, into=None, tr=128):
    rows, cols = w.shape
    hr = rows // 2
    tr = min(tr, hr)
    nt = hr // tr

    def body(h_ref, w_ref, a_ref, m_ref, v_ref, *rest):
        g_ref, d_ref, mo_ref, vo_ref = rest[-4:]
        gg = a_ref[...]
        g_ref[...] = gg
        d, mn, vn = _adamw_math(w_ref[...], gg, m_ref[...], v_ref[...])
        d_ref[...] = d
        mo_ref[...] = mn
        vo_ref[...] = vn

    full = pl.BlockSpec((tr, cols), lambda i, h_ref: (h_ref[0] * nt + i, 0))
    sd = jax.ShapeDtypeStruct((rows, cols), F32)
    extra = [] if into is None else list(into)
    return pl.pallas_call(
        body, name=name, out_shape=(sd, sd, sd, sd),
        grid_spec=pltpu.PrefetchScalarGridSpec(
            num_scalar_prefetch=1, grid=(nt,),
            in_specs=[full, pl.BlockSpec((tr, cols), lambda i, h_ref: (i, 0)), full, full] + [ANY] * len(extra),
            out_specs=(full, full, full, full)),
        input_output_aliases={5 + k: k for k in range(len(extra))},
        compiler_params=_params("parallel"),
    )(half, w, g_half, m, v, *extra)


def _allreduce_small(vec):
    m_per, ncol = vec.shape
    n_dev = 2 * N_CHIPS

    def body(x_ref, out_ref, sum_ref, send_sems, recv_sems, local_sem):
        x, y, c = _coords()
        me, sibling = (x, y, c), (x, y, 1 - c)
        chips = _other_chips(x, y)

        def rows(px, py, pc):
            return out_ref.at[pl.ds(pl.multiple_of((4 * px + 2 * py + pc) * m_per, m_per), m_per), :]

        def copy(k, block, to, src=None):
            return _remote(rows(*block) if src is None else src, rows(*block), send_sems, recv_sems, k, to)

        mine = pltpu.make_async_copy(x_ref, rows(*me), local_sem)
        mine.start()
        first = [copy(0, me, sibling, src=x_ref)]
        first += [copy(1 + j, me, (*chip, c), src=x_ref) for j, chip in enumerate(chips)]
        for cp in first:
            cp.start()
        passed = [copy(4 + j, (*chip, c), sibling) for j, chip in enumerate(chips)]
        for j, chip in enumerate(chips):
            copy(1 + j, (*chip, c), me).wait_recv()
            passed[j].start()
        copy(0, sibling, me).wait_recv()
        for j, chip in enumerate(chips):
            copy(4 + j, (*chip, 1 - c), me).wait_recv()
        for cp in first + passed:
            cp.wait_send()
        mine.wait()
        acc = out_ref[0:m_per, :]
        for d in range(1, n_dev):
            acc = acc + out_ref[d * m_per:(d + 1) * m_per, :]
        sum_ref[...] = acc

    vm = pl.BlockSpec(memory_space=pltpu.VMEM)
    return pl.pallas_call(
        body, name="allreduce_small",
        out_shape=(jax.ShapeDtypeStruct((n_dev * m_per, ncol), vec.dtype), jax.ShapeDtypeStruct((m_per, ncol), vec.dtype)),
        in_specs=[vm], out_specs=(vm, vm),
        scratch_shapes=[pltpu.SemaphoreType.DMA((7,)), pltpu.SemaphoreType.DMA((7,)), pltpu.SemaphoreType.DMA],
    )(vec)[1]


def _pack_rows(pieces, total_rows):
    rows = []
    for p in pieces:
        flat = p.reshape(-1)
        flat = jnp.pad(flat, (0, (-flat.shape[0]) % LANES))
        rows.append(flat.reshape(-1, LANES))
    out = jnp.concatenate(rows, axis=0)
    return jnp.pad(out, ((0, total_rows - out.shape[0]), (0, 0)))


def _unpack_rows(packed, shapes):
    out, r = [], 0
    for shp in shapes:
        size = 1
        for d in shp:
            size *= d
        nr = -(-size // LANES)
        out.append(packed[r:r + nr].reshape(-1)[:size].reshape(shp))
        r += nr
    return out


def _round_up(v, m):
    return -(-v // m) * m


def kernel(x, a_w_in, a_gate_b, a_conv_w, a_conv_b, a_head_g, a_w_out, a_ln_g, a_ln_b, kv_w, b_w_in, b_w_out, b_ln_g, b_ln_b, loss_target, m_a_w_in, m_a_gate_b, m_a_conv_w, m_a_conv_b, m_a_head_g, m_a_w_out, m_a_ln_g, m_a_ln_b, m_kv_w, m_b_w_in, m_b_w_out, m_b_ln_g, m_b_ln_b, v_a_w_in, v_a_gate_b, v_a_conv_w, v_a_conv_b, v_a_head_g, v_a_w_out, v_a_ln_g, v_a_ln_b, v_kv_w, v_b_w_in, v_b_w_out, v_b_ln_g, v_b_ln_b):
    _, S, D = x.shape
    nha = a_gate_b.shape[1] // 2
    chip = 2 * lax.axis_index("x") + lax.axis_index("y")
    core = lax.axis_index("c").astype(jnp.int32).reshape(1)
    dq = D // N_CHIPS

    ca = a_w_in.shape[2]
    edge = (N_CHIPS - 1) * (ca - D)
    wide = _round_up(ca + edge, LANES)
    shifted = lax.dynamic_slice_in_dim(
        jnp.pad(a_w_in[0].astype(BF16), ((0, 0), (edge, wide - ca))), edge - chip * (ca - D), wide, axis=1)
    shards = [shifted, a_w_out[0].astype(BF16), kv_w.astype(BF16), b_w_in[0].astype(BF16), b_w_out[0].astype(BF16)]
    kinds = ["stack", "rows", "cols", "cols", "rows"]
    small_shard = jnp.concatenate([a_conv_w[0], a_conv_b, a_head_g, a_ln_g, a_ln_b], axis=0)
    wa_g, small_full = _allgather_behind(shards[:1], kinds[:1], small_shard, name="allgather_first")
    wao, = _allgather_behind(shards[1:2], kinds[1:2], name="allgather_a_w_out")
    wkv, wbi = _allgather_behind(shards[2:4], kinds[2:4], name="allgather_b_in")
    wbo, = _allgather_behind(shards[4:], kinds[4:], name="allgather_b_w_out")
    wa = _join_chunk_edges(wa_g, D)
    wg = wa[N_CHIPS - 1, :, D:D + LANES]
    conv_w, conv_b, head_g, ln_g_a, ln_b_a = (small_full[0:CONV_A], small_full[4:5], small_full[5:6],
                                              small_full[6:7], small_full[7:8])

    chip_ix = chip.astype(jnp.int32).reshape(1)
    (w_wa, m_wa, v_wa), x = lax.optimization_barrier(((a_w_in[0], m_a_w_in[0], v_a_w_in[0]), x))
    w_big = dict(wa=w_wa, wao=a_w_out[0], wkv=kv_w, wbi=b_w_in[0], wbo=b_w_out[0])
    m_big = dict(wa=m_wa, wao=m_a_w_out[0], wkv=m_kv_w, wbi=m_b_w_in[0], wbo=m_b_w_out[0])
    v_big = dict(wa=v_wa, wao=v_a_w_out[0], wkv=v_kv_w, wbi=v_b_w_in[0], wbo=v_b_w_out[0])
    long_name = dict(wa="a_w_in", wao="a_w_out", wkv="kv_w", wbi="b_w_in", wbo="b_w_out")

    def as_views(g):
        if "wa_full" in g:
            return ["wa"], [g["wa_full"][None]], [("win", D, wide, ca - D, ca)]
        keys = list(g)
        views = [g[k].reshape(N_CHIPS, dq, D) if k in ("wao", "wbo") else g[k][None] for k in keys]
        return keys, views, ["stack" if k in ("wao", "wbo") else "cols" for k in keys]

    w_small = [a_gate_b, a_conv_w[0], a_conv_b, a_head_g, a_ln_g, a_ln_b, b_ln_g, b_ln_b]
    m_small = [m_a_gate_b, m_a_conv_w[0], m_a_conv_b, m_a_head_g, m_a_ln_g, m_a_ln_b, m_b_ln_g, m_b_ln_b]
    v_small = [v_a_gate_b, v_a_conv_w[0], v_a_conv_b, v_a_head_g, v_a_ln_g, v_a_ln_b, v_b_ln_g, v_b_ln_b]

    def small_step(small, loss_row):
        order = ["conv_w", "conv_b", "head_g", "a_ln_g", "a_ln_b", "b_ln_g", "b_ln_b", "gate_b"]
        full_shapes = [(CONV_A, D), (1, D), (1, D), (1, D), (1, D), (1, D), (1, D), (1, 2 * nha)]
        n_rows = sum(-(-(s[0] * s[1]) // LANES) for s in full_shapes) + 1
        total = _allreduce_small(_pack_rows([small[k] for k in order] + [loss_row], _round_up(n_rows, 8)))
        sums = dict(zip(order, _unpack_rows(total, full_shapes)))

        def mine(v):
            return lax.dynamic_slice_in_dim(v, chip * dq, dq, axis=1)

        g_small = [sums["gate_b"], mine(sums["conv_w"]), mine(sums["conv_b"]), mine(sums["head_g"]),
                   mine(sums["a_ln_g"]), mine(sums["a_ln_b"]), sums["b_ln_g"], sums["b_ln_b"]]
        rows_small = _round_up(sum(-(-(w.shape[0] * w.shape[1]) // LANES) for w in w_small), 8)
        upd_small = _adamw(_pack_rows(w_small, rows_small), _pack_rows(g_small, rows_small),
                           _pack_rows(m_small, rows_small), _pack_rows(v_small, rows_small), name="adamw_small")
        return total[n_rows - 1, 0], g_small, upd_small

    rs = _GradReducer(w_big, m_big, v_big, long_name, core, chip_ix, as_views, small_step)
    grad_x = _local_step(
        x[0], loss_target[0], wa, wg, wao, wbi, wkv, wbo, a_gate_b, conv_w, conv_b, head_g,
        ln_g_a, ln_b_a, b_ln_g, b_ln_b, S=S, D=D, nha=nha, rs=rs)
    grad_x = rs.finish("w", rs.chip("w", grad_x, update_mine=True))
    upd_big = [rs.results[k] for k in ("wa", "wao", "wkv", "wbi", "wbo")]
    g_big = [u[0] for u in upd_big]
    loss, g_small, upd_small = rs.small_results
    d_small, mn_small, vn_small = (_unpack_rows(u, [w.shape for w in w_small]) for u in upd_small)

    def assemble(big5, small8):
        awi, awo, kvw, bwi, bwo = big5
        gb, cw, cb, hg, alg, alb, blg, blb = small8
        return [awi[None], gb, cw[None], cb, hg, awo[None], alg, alb, kvw, bwi[None], bwo[None], blg, blb]

    grads = assemble(g_big, g_small)
    deltas = assemble([u[1] for u in upd_big], d_small)
    new_m = assemble([u[2] for u in upd_big], mn_small)
    new_v = assemble([u[3] for u in upd_big], vn_small)
    return (loss, grad_x[None], *grads, *deltas, *new_m, *new_v)
```

```python
import functools

import jax
import jax.numpy as jnp
from jax import lax
from jax.experimental import pallas as pl
from jax.experimental.pallas import tpu as pltpu
from jax.experimental.pallas import tpu_sc as plsc

F32 = jnp.float32
BF16 = jnp.bfloat16

DEPTH = 2
ALPHA = (2.0 * DEPTH) ** 0.25
LN_EPS = 1e-5
DK_A = 128
DV_A = 256
DH_B = 128
SB_TILE = 1024
ML_TQ = 1024
CONV_A = 4
ADAM_LR = 0.001
ADAM_B1 = 0.9
ADAM_B2 = 0.999
ADAM_EPS = 1e-08
ADAM_WD = 0.01
ADAM_STEP = 10
N_CHIPS = 4
LANES = 128
MXU_DEPTH = 256
V7X_VMEM_BYTES = 64 * 1024 * 1024
VMEM_LIMIT = (V7X_VMEM_BYTES * 3) // 4
NEG_BIG = -1e30
MESH = pl.DeviceIdType.MESH
ANY = pl.BlockSpec(memory_space=pl.ANY)


def _params(*sem):
    return pltpu.CompilerParams(dimension_semantics=sem, vmem_limit_bytes=VMEM_LIMIT)


def _dot(a, b, dims):
    return lax.dot_general(a, b, (dims, ((), ())), preferred_element_type=F32)


def _dot_nn(a, b):
    return _dot(a, b, ((1,), (0,)))


def _dot_nt(a, b):
    return _dot(a, b, ((1,), (1,)))


def _dot_tn(a, b):
    return _dot(a, b, ((0,), (0,)))


def _split2(x):
    hi = x.astype(BF16)
    lo = (x - hi.astype(F32)).astype(BF16)
    return hi, lo


def _split3(x):
    hi = x.astype(BF16)
    r = x - hi.astype(F32)
    mid = r.astype(BF16)
    lo = (r - mid.astype(F32)).astype(BF16)
    return hi, mid, lo


def _mask_dot2(x, t01):
    hi, lo = _split2(x)
    return _dot_nn(hi, t01) + _dot_nn(lo, t01)


def _block_sums(x, tri, later, passes=2):
    dot = _mask_dot2 if passes == 2 else (lambda v, t01: _dot_nn(v.astype(BF16), t01))
    sub = tri.shape[0]
    n = x.shape[1] // sub
    if n == 1:
        return dot(x, tri)
    parts = [x[:, b * sub:(b + 1) * sub] for b in range(n)]
    sums = [jnp.sum(p, axis=1, keepdims=True) for p in parts]
    out = []
    for b in range(n):
        acc = dot(parts[b], tri)
        for o in (range(b + 1, n) if later else range(b)):
            acc = acc + sums[o]
        out.append(acc)
    return jnp.concatenate(out, axis=1)


def _mask_dot3(x, t01):
    hi, mid, lo = _split3(x)
    return _dot_nn(hi, t01) + _dot_nn(mid, t01) + _dot_nn(lo, t01)


def _mask_dot3_left(t01, x):
    hi, mid, lo = _split3(x)
    return _dot_nn(t01, hi) + _dot_nn(t01, mid) + _dot_nn(t01, lo)


def _log_sigmoid(z):
    return jnp.minimum(z, 0.0) - jnp.log(1.0 + jnp.exp(-jnp.abs(z)))


def _sigmoid(z):
    return 1.0 / (1.0 + jnp.exp(-z))


def _mm(a, b, *, mode, M, N, K, out_dtype, name, tm=1024, tn=1024, tk=2048,
        a_off=(0, 0), b_off=(0, 0), acc_in=None, acc_scale=1.0, into=None, out_off=(0, 0), b_chunk=None, out_scale=None):
    tm, tn, tk = min(tm, M), min(tn, N), min(tk, K)
    if b_chunk is not None:
        tn, tk = (min(tn, b_chunk), tk) if mode == "nn" else (tn, min(tk, b_chunk))
        assert b.ndim == 3 and b_off == (0, 0) and mode in ("nn", "nt")
    assert M % tm == 0 and N % tn == 0 and K % tk == 0
    nk = K // tk
    ar, ac = a_off
    br, bc = b_off
    orow, ocol = out_off
    if mode in ("nn", "nt"):
        assert ar % tm == 0 and ac % tk == 0
        a_spec = pl.BlockSpec((tm, tk), lambda i, j, k: (i + ar // tm, k + ac // tk))
    else:
        assert ar % tk == 0 and ac % tm == 0
        a_spec = pl.BlockSpec((tk, tm), lambda i, j, k: (k + ar // tk, i + ac // tm))
    if b_chunk is not None and mode == "nn":
        per = b_chunk // tn
        b_spec = pl.BlockSpec((None, tk, tn), lambda i, j, k: (j // per, k, j % per))
    elif b_chunk is not None:
        per = b_chunk // tk
        b_spec = pl.BlockSpec((None, tn, tk), lambda i, j, k: (k // per, j, k % per))
    elif mode == "nt":
        assert br % tn == 0 and bc % tk == 0
        b_spec = pl.BlockSpec((tn, tk), lambda i, j, k: (j + br // tn, k + bc // tk))
    else:
        assert br % tk == 0 and bc % tn == 0
        b_spec = pl.BlockSpec((tk, tn), lambda i, j, k: (k + br // tk, j + bc // tn))
    assert orow % tm == 0 and ocol % tn == 0
    o_spec = pl.BlockSpec((tm, tn), lambda i, j, k: (i + orow // tm, j + ocol // tn))
    dims = {"nn": ((1,), (0,)), "nt": ((1,), (1,)), "tn": ((0,), (0,))}[mode]
    inputs, in_specs = [a, b], [a_spec, b_spec]
    has_acc = acc_in is not None
    if has_acc:
        inputs.append(acc_in)
        in_specs.append(pl.BlockSpec((tm, tn), lambda i, j, k: (i, j)))
    aliases = {}
    if into is not None:
        inputs.append(into)
        in_specs.append(ANY)
        aliases = {len(inputs) - 1: 0}
        out_shape = jax.ShapeDtypeStruct(into.shape, into.dtype)
        assert into.dtype == out_dtype
    else:
        out_shape = jax.ShapeDtypeStruct((M, N), out_dtype)

    def body(*refs):
        a_ref, b_ref = refs[0], refs[1]
        acc_in_ref = refs[2] if has_acc else None
        n_in = len(inputs)
        o_ref = refs[n_in]

        def first():
            if has_acc:
                return acc_scale * acc_in_ref[...]
            return None

        def scaled(r):
            return r if out_scale is None else r * out_scale

        if nk == 1:
            r = _dot(a_ref[...], b_ref[...], dims)
            f = first()
            if f is not None:
                r = r + f
            o_ref[...] = scaled(r).astype(o_ref.dtype)
        else:
            acc_ref = refs[n_in + 1]
            kk = pl.program_id(2)

            @pl.when(kk == 0)
            def _():
                f = first()
                acc_ref[...] = jnp.zeros_like(acc_ref) if f is None else f

            acc_ref[...] += _dot(a_ref[...], b_ref[...], dims)

            @pl.when(kk == nk - 1)
            def _():
                o_ref[...] = scaled(acc_ref[...]).astype(o_ref.dtype)

    scratch = [] if nk == 1 else [pltpu.VMEM((tm, tn), F32)]
    return pl.pallas_call(
        body, name=name, out_shape=out_shape, grid=(M // tm, N // tn, nk),
        in_specs=in_specs, out_specs=o_spec, scratch_shapes=scratch,
        input_output_aliases=aliases,
        compiler_params=_params("parallel", "parallel", "arbitrary"),
    )(*inputs)


def _join_chunk_edges(w, d):
    nch, rows, _ = w.shape

    def body(w_ref, o_ref, first, tail, sems):
        del w_ref
        for j in range(1, nch):
            mine = o_ref.at[j, :, pl.ds(0, LANES)]
            loads = [pltpu.make_async_copy(mine, first, sems.at[0]),
                     pltpu.make_async_copy(o_ref.at[j - 1, :, pl.ds(d, LANES)], tail, sems.at[1])]
            for cp in loads:
                cp.start()
            for cp in loads:
                cp.wait()
            first[...] = first[...] + tail[...]
            store = pltpu.make_async_copy(first, mine, sems.at[0])
            store.start()
            store.wait()

    return pl.pallas_call(
        body, name="join_chunk_edges", out_shape=jax.ShapeDtypeStruct(w.shape, w.dtype),
        in_specs=[ANY], out_specs=ANY, input_output_aliases={0: 0},
        scratch_shapes=[pltpu.VMEM((rows, LANES), w.dtype), pltpu.VMEM((rows, LANES), w.dtype),
                        pltpu.SemaphoreType.DMA((2,))],
    )(w)


def _shift_down(x, d, row):
    if d == 0:
        return x
    return jnp.where(row >= d, pltpu.roll(x, d, 0), 0.0)


def _shift_up(x, d, row, n):
    if d == 0:
        return x
    return jnp.where(row < n - d, pltpu.roll(x, n - d, 0), 0.0)


def _conv_pre(x, w_ref, b_ref, row):
    c = b_ref[...] + w_ref[CONV_A - 1:CONV_A, :] * x
    for k in range(CONV_A - 1):
        c = c + w_ref[k:k + 1, :] * _shift_down(x, CONV_A - 1 - k, row)
    return c


def _conv_fwd(u, conv_w, conv_b, *, S, D, name):
    tc = 256
    nq_blocks = (D // 2) // tc

    def body(u_ref, w_ref, b_ref, o_ref):
        x = u_ref[...]
        row = lax.broadcasted_iota(jnp.int32, x.shape, 0)
        c = _conv_pre(x, w_ref, b_ref, row)
        scale = jnp.where(pl.program_id(0) >= nq_blocks, DK_A ** -0.5, 1.0).astype(F32)
        o_ref[...] = (c * _sigmoid(c) * scale).astype(BF16)

    return pl.pallas_call(
        body, name=name, out_shape=jax.ShapeDtypeStruct((S, D), BF16), grid=(D // tc,),
        in_specs=[pl.BlockSpec((S, tc), lambda j: (0, j)),
                  pl.BlockSpec((CONV_A, tc), lambda j: (0, j)),
                  pl.BlockSpec((1, tc), lambda j: (0, j))],
        out_specs=pl.BlockSpec((S, tc), lambda j: (0, j)),
        compiler_params=_params("parallel"),
    )(u, conv_w, conv_b)


def _conv_bwd(u, dq, dk, conv_w, conv_b, du, *, S, D, name):
    tc = 256
    nq_blocks = (D // 2) // tc

    def body(u_ref, dq_ref, dk_ref, w_ref, b_ref, du_in, du_ref, dw_ref, db_ref):
        del du_in
        x = u_ref[...]
        n = x.shape[0]
        row = lax.broadcasted_iota(jnp.int32, x.shape, 0)
        c = _conv_pre(x, w_ref, b_ref, row)
        is_k = pl.program_id(0) >= nq_blocks
        dy = jnp.where(is_k, dk_ref[...] * (DK_A ** -0.5), dq_ref[...])
        sg = _sigmoid(c)
        dc = dy * (sg * (1.0 + c * (1.0 - sg)))
        db_ref[...] = jnp.sum(dc, axis=0, keepdims=True)
        dx = w_ref[CONV_A - 1:CONV_A, :] * dc
        dw_ref[CONV_A - 1:CONV_A, :] = jnp.sum(dc * x, axis=0, keepdims=True)
        for k in range(CONV_A - 1):
            d = CONV_A - 1 - k
            dw_ref[k:k + 1, :] = jnp.sum(dc * _shift_down(x, d, row), axis=0, keepdims=True)
            dx = dx + w_ref[k:k + 1, :] * _shift_up(dc, d, row, n)
        du_ref[...] = dx.astype(BF16)

    half = lambda j: (0, j % nq_blocks)
    return pl.pallas_call(
        body, name=name,
        out_shape=(jax.ShapeDtypeStruct(du.shape, du.dtype),
                   jax.ShapeDtypeStruct((CONV_A, D), F32), jax.ShapeDtypeStruct((1, D), F32)),
        grid=(D // tc,),
        in_specs=[pl.BlockSpec((S, tc), lambda j: (0, j)),
                  pl.BlockSpec((S, tc), half), pl.BlockSpec((S, tc), half),
                  pl.BlockSpec((CONV_A, tc), lambda j: (0, j)),
                  pl.BlockSpec((1, tc), lambda j: (0, j)), ANY],
        out_specs=(pl.BlockSpec((S, tc), lambda j: (0, j)),
                   pl.BlockSpec((CONV_A, tc), lambda j: (0, j)),
                   pl.BlockSpec((1, tc), lambda j: (0, j))),
        input_output_aliases={5: 0},
        compiler_params=_params("parallel"),
    )(u, dq, dk, conv_w, conv_b, du)


def _tri(n, cmp):
    r = lax.broadcasted_iota(jnp.int32, (n, n), 0)
    c = lax.broadcasted_iota(jnp.int32, (n, n), 1)
    return r, c, cmp(r, c)


def _gates_fwd(gt, bias, *, nha, nb, name):
    half = nha * nb
    nb_shift = nb.bit_length() - 1
    assert nb == 1 << nb_shift

    def body(g_ref, b_ref, f_ref, brow_ref):
        ig = g_ref[0:half, :] + b_ref[0:half, :]
        fg = g_ref[half:2 * half, :] + b_ref[half:2 * half, :]
        lf = _log_sigmoid(fg)
        _, _, upper = _tri(LANES, lambda r, c: r <= c)
        cs = _mask_dot3(lf, upper.astype(BF16))
        tot = jnp.broadcast_to(cs[:, LANES - 1:LANES], cs.shape)
        r, c, _ = _tri(half, lambda r, c: r <= c)
        before = jnp.logical_and(r >> nb_shift == c >> nb_shift, c < r).astype(BF16)
        f = cs + _mask_dot3_left(before, tot)
        f_ref[...] = f
        brow_ref[...] = ig - f

    return pl.pallas_call(
        body, name=name,
        out_shape=(jax.ShapeDtypeStruct((half, LANES), F32), jax.ShapeDtypeStruct((half, LANES), F32)),
    )(gt, bias)


def _gates_bwd(rowsum, colsum, gt, bias, *, nha, nb, name):
    half = nha * nb
    nb_shift = nb.bit_length() - 1
    assert nb == 1 << nb_shift

    def body(rs_ref, cs_ref, g_ref, b_ref, dg_ref, tot_ref):
        col = cs_ref[...]
        df = rs_ref[...] - col
        _, _, lower = _tri(LANES, lambda r, c: r >= c)
        rc = _mask_dot3(df, lower.astype(BF16))
        tot = jnp.broadcast_to(rc[:, 0:1], rc.shape)
        r, c, _ = _tri(half, lambda r, c: r <= c)
        same = r >> nb_shift == c >> nb_shift
        after = jnp.logical_and(same, c > r).astype(BF16)
        dlf = rc + _mask_dot3_left(after, tot)
        fg = g_ref[half:2 * half, :] + b_ref[half:2 * half, :]
        dfg = dlf * _sigmoid(-fg)
        dg_ref[0:half, :] = col
        dg_ref[half:2 * half, :] = dfg
        grp = same.astype(BF16)
        ones = jnp.ones((LANES, LANES), BF16)
        tot_ref[0:half, :] = _mask_dot3_left(grp, _mask_dot3(col, ones))
        tot_ref[half:2 * half, :] = _mask_dot3_left(grp, _mask_dot3(dfg, ones))

    return pl.pallas_call(
        body, name=name,
        out_shape=(jax.ShapeDtypeStruct((2 * half, LANES), F32), jax.ShapeDtypeStruct((2 * half, LANES), F32)),
    )(rowsum, colsum, gt, bias)


def _mlstm_tile(q, k_ref, fcol, brow_ref, j, tq, diagonal):
    off = pl.multiple_of(j * tq, tq)
    kj = k_ref[pl.ds(off, tq), :]
    s = _dot_nt(q, kj)
    logd = fcol + brow_ref[0, :, pl.ds(off, tq)]
    if diagonal:
        valid = lax.broadcasted_iota(jnp.int32, (tq, tq), 1) <= lax.broadcasted_iota(jnp.int32, (tq, tq), 0)
        logd = jnp.where(valid, logd, NEG_BIG)
    return off, kj, s, logd


def _mlstm_fwd(qk, u, fcol, brow, head_g, *, S, D, nha, name):
    tq = min(ML_TQ, S)
    nq = S // tq
    kb, vb = (D // 2) // DK_A, D // DV_A

    def body(q_ref, k_ref, v_ref, fcol_ref, brow_ref, o_ref, z_ref, g_ref, h_ref, hg_ref, m_ref, den_ref):
        i = pl.program_id(1)
        q = q_ref[...]
        fc = fcol_ref[0]

        def step(j, carry, diagonal):
            acc, den, m = carry
            off, _, s, logd = _mlstm_tile(q, k_ref, fc, brow_ref, j, tq, diagonal)
            m_new = jnp.maximum(m, jnp.max(logd, axis=1, keepdims=True))
            a = s * jnp.exp(logd - m_new)
            alpha = jnp.exp(m - m_new)
            vj = v_ref[pl.ds(off, tq), :].astype(BF16)
            acc = alpha * acc + _dot_nn(a.astype(BF16), vj)
            den = alpha * den + jnp.sum(a, axis=1, keepdims=True)
            return acc, den, m_new

        first = step(i, (jnp.zeros((tq, DV_A), F32), jnp.zeros((tq, 1), F32), jnp.full((tq, 1), NEG_BIG, F32)), True)
        acc, den, m = lax.fori_loop(0, i, lambda j, c: step(j, c, False), first)
        hh = acc / jnp.maximum(jnp.abs(den), jnp.exp(-m))
        h_ref[...] = hh
        hn, _ = _head_norm(hh)
        z = z_ref[...]
        hg_ref[...] = (_sigmoid(o_ref[...]) * (hn * g_ref[...]) * (z * _sigmoid(z))).astype(BF16)
        m_ref[0] = m
        den_ref[0] = den

    stat = pl.BlockSpec((1, tq, 1), lambda h, i: (h, i, 0))
    hblk = pl.BlockSpec((tq, DV_A), lambda h, i: (i, h))
    return pl.pallas_call(
        body, name=name,
        out_shape=(jax.ShapeDtypeStruct((S, D), F32), jax.ShapeDtypeStruct((S, D), BF16),
                   jax.ShapeDtypeStruct((nha, S, 1), F32), jax.ShapeDtypeStruct((nha, S, 1), F32)),
        grid=(nha, nq),
        in_specs=[pl.BlockSpec((tq, DK_A), lambda h, i: (i, h)),
                  pl.BlockSpec((S, DK_A), lambda h, i: (0, kb + h)),
                  pl.BlockSpec((S, DV_A), lambda h, i: (0, vb + h)),
                  stat, pl.BlockSpec((1, 1, S), lambda h, i: (h, 0, 0)),
                  pl.BlockSpec((tq, DV_A), lambda h, i: (i, 2 * vb + h)),
                  pl.BlockSpec((tq, DV_A), lambda h, i: (i, 3 * vb + h)),
                  pl.BlockSpec((1, DV_A), lambda h, i: (0, h))],
        out_specs=(hblk, hblk, stat, stat),
        compiler_params=_params("parallel", "arbitrary"),
    )(qk, qk, u, fcol, brow, u, u, head_g)


def _mlstm_bwd(qk, u, fcol, brow, m, den, dh, h, du, *, S, D, nha, name):
    tq = min(ML_TQ, S)
    nq = S // tq
    kb, vb = (D // 2) // DK_A, D // DV_A

    def body(q_ref, k_ref, v_ref, fcol_ref, brow_ref, m_ref, den_ref, dh_ref, h_ref, du_in,
             du_ref, dq_ref, dk_ref, rs_ref, cs_ref, dv_acc):
        del du_in
        i = pl.program_id(1)

        @pl.when(i == 0)
        def _():
            dk_ref[...] = jnp.zeros_like(dk_ref)
            cs_ref[...] = jnp.zeros_like(cs_ref)
            dv_acc[...] = jnp.zeros_like(dv_acc)

        q = q_ref[...]
        fc = fcol_ref[0]
        mm = m_ref[0]
        dn = den_ref[0]
        floor = jnp.exp(-mm)
        nrm = jnp.maximum(jnp.abs(dn), floor)
        dhv = dh_ref[...]
        dnum = dhv / nrm
        dnrm = -jnp.sum(dhv * h_ref[...], axis=1, keepdims=True) / nrm
        dden = jnp.where(jnp.abs(dn) > floor, jnp.where(dn > 0.0, dnrm, -dnrm), 0.0)
        dnum_b = dnum.astype(BF16)

        def step(j, carry, diagonal):
            dq, rs = carry
            off, kj, s, logd = _mlstm_tile(q, k_ref, fc, brow_ref, j, tq, diagonal)
            p = jnp.exp(logd - mm)
            a = s * p
            vj = v_ref[pl.ds(off, tq), :].astype(BF16)
            da = _dot_nt(dnum_b, vj) + dden
            dv_acc[pl.ds(off, tq), :] += _dot_tn(a.astype(BF16), dnum_b)
            dqk = (da * p).astype(BF16)
            dq = dq + _dot_nn(dqk, kj)
            dk_ref[pl.ds(off, tq), :] += _dot_tn(dqk, q)
            pm = da * a
            cs_ref[0, :, pl.ds(off, tq)] += jnp.sum(pm, axis=0, keepdims=True)
            rs = rs + jnp.sum(pm, axis=1, keepdims=True)
            return dq, rs

        first = step(i, (jnp.zeros((tq, DK_A), F32), jnp.zeros((tq, 1), F32)), True)
        dq, rs = lax.fori_loop(0, i, lambda j, c: step(j, c, False), first)
        dq_ref[...] = dq
        rs_ref[0] = rs

        @pl.when(i == nq - 1)
        def _():
            du_ref[...] = dv_acc[...].astype(BF16)

    stat = pl.BlockSpec((1, tq, 1), lambda h, i: (h, i, 0))
    rowv = pl.BlockSpec((1, 1, S), lambda h, i: (h, 0, 0))
    hblk = pl.BlockSpec((tq, DV_A), lambda h, i: (i, h))
    return pl.pallas_call(
        body, name=name,
        out_shape=(jax.ShapeDtypeStruct(du.shape, du.dtype),
                   jax.ShapeDtypeStruct((S, D // 2), F32), jax.ShapeDtypeStruct((S, D // 2), F32),
                   jax.ShapeDtypeStruct((nha, S, 1), F32), jax.ShapeDtypeStruct((nha, 1, S), F32)),
        grid=(nha, nq),
        in_specs=[pl.BlockSpec((tq, DK_A), lambda h, i: (i, h)),
                  pl.BlockSpec((S, DK_A), lambda h, i: (0, kb + h)),
                  pl.BlockSpec((S, DV_A), lambda h, i: (0, vb + h)),
                  stat, rowv, stat, stat, hblk, hblk, ANY],
        out_specs=(pl.BlockSpec((S, DV_A), lambda h, i: (0, vb + h)),
                   pl.BlockSpec((tq, DK_A), lambda h, i: (i, h)),
                   pl.BlockSpec((S, DK_A), lambda h, i: (0, h)),
                   stat, rowv),
        scratch_shapes=[pltpu.VMEM((S, DV_A), F32)],
        input_output_aliases={9: 0},
        compiler_params=_params("parallel", "arbitrary"),
    )(qk, qk, u, fcol, brow, m, den, dh, h, du)


def _head_norm(hh):
    mu = jnp.mean(hh, axis=1, keepdims=True)
    hc = hh - mu
    rstd = lax.rsqrt(jnp.mean(hc * hc, axis=1, keepdims=True) + LN_EPS)
    return hc * rstd, rstd


def _hgate_bwd(dhg, h, u, head_g, du, *, S, D, name):
    tm = min(256, S)
    nh = D // DV_A

    def body(dhg_ref, h_ref, o_ref, z_ref, g_ref, du_in, du_ref, dh_ref, dg_ref):
        del du_in

        @pl.when(pl.program_id(0) == 0)
        def _():
            dg_ref[...] = jnp.zeros_like(dg_ref)

        for hd in range(nh):
            sl = slice(hd * DV_A, (hd + 1) * DV_A)
            hn, rstd = _head_norm(h_ref[:, sl])
            o, z, g, d = o_ref[:, sl], z_ref[:, sl], g_ref[:, sl], dhg_ref[:, sl]
            so, sz = _sigmoid(o), _sigmoid(z)
            silu_z = z * sz
            hng = hn * g
            du_ref[:, sl] = (d * hng * silu_z * so * (1.0 - so)).astype(BF16)
            du_ref[:, D + hd * DV_A:D + (hd + 1) * DV_A] = (
                d * so * hng * (sz * (1.0 + z * (1.0 - sz)))).astype(BF16)
            t = d * so * silu_z
            dg_ref[:, sl] += jnp.sum(t * hn, axis=0, keepdims=True)
            dhn = t * g
            dh_ref[:, sl] = rstd * (dhn - jnp.mean(dhn, axis=1, keepdims=True)
                                    - hn * jnp.mean(dhn * hn, axis=1, keepdims=True))

    row = lambda i: (i, 0)
    return pl.pallas_call(
        body, name=name,
        out_shape=(jax.ShapeDtypeStruct(du.shape, du.dtype), jax.ShapeDtypeStruct((S, D), F32),
                   jax.ShapeDtypeStruct((1, D), F32)),
        grid=(S // tm,),
        in_specs=[pl.BlockSpec((tm, D), row), pl.BlockSpec((tm, D), row),
                  pl.BlockSpec((tm, D), lambda i: (i, 2)), pl.BlockSpec((tm, D), lambda i: (i, 3)),
                  pl.BlockSpec((1, D), lambda i: (0, 0)), ANY],
        out_specs=(pl.BlockSpec((tm, 2 * D), lambda i: (i, 1)), pl.BlockSpec((tm, D), row),
                   pl.BlockSpec((1, D), lambda i: (0, 0))),
        input_output_aliases={5: 0},
        compiler_params=_params("arbitrary"),
    )(dhg, h, u, u, head_g, du)


def _ln_stats(r):
    mu = jnp.mean(r, axis=1, keepdims=True)
    xc = r - mu
    rstd = lax.rsqrt(jnp.mean(xc * xc, axis=1, keepdims=True) + LN_EPS)
    return xc * rstd, rstd


def _ln_back(dxhat, xhat, rstd):
    return rstd * (dxhat - jnp.mean(dxhat, axis=1, keepdims=True)
                   - xhat * jnp.mean(dxhat * xhat, axis=1, keepdims=True))


def _ln_fwd(x, y, g, b, *, S, D, name):
    tm = min(256, S)

    def body(x_ref, y_ref, g_ref, b_ref, o_ref, ob_ref):
        xhat, _ = _ln_stats(ALPHA * x_ref[...] + y_ref[...])
        o = xhat * g_ref[...] + b_ref[...]
        o_ref[...] = o
        ob_ref[...] = o.astype(BF16)

    row = lambda i: (i, 0)
    vec = pl.BlockSpec((1, D), lambda i: (0, 0))
    return pl.pallas_call(
        body, name=name,
        out_shape=(jax.ShapeDtypeStruct((S, D), F32), jax.ShapeDtypeStruct((S, D), BF16)),
        grid=(S // tm,),
        in_specs=[pl.BlockSpec((tm, D), row), pl.BlockSpec((tm, D), row), vec, vec],
        out_specs=(pl.BlockSpec((tm, D), row), pl.BlockSpec((tm, D), row)),
        compiler_params=_params("parallel"),
    )(x, y, g, b)


def _ln_loss_bwd(x1, y2, target, g, b, *, S, D, name):
    tm = min(256, S)

    def body(x_ref, y_ref, t_ref, g_ref, b_ref, dr_ref, drb_ref, dg_ref, db_ref, loss_ref):
        @pl.when(pl.program_id(0) == 0)
        def _():
            dg_ref[...] = jnp.zeros_like(dg_ref)
            db_ref[...] = jnp.zeros_like(db_ref)
            loss_ref[...] = jnp.zeros_like(loss_ref)

        xhat, rstd = _ln_stats(ALPHA * x_ref[...] + y_ref[...])
        diff = xhat * g_ref[...] + b_ref[...] - t_ref[...]
        loss_ref[...] += (0.5 / D) * jnp.sum(diff * diff)
        dx2 = diff * (1.0 / D)
        dg_ref[...] += jnp.sum(dx2 * xhat, axis=0, keepdims=True)
        db_ref[...] += jnp.sum(dx2, axis=0, keepdims=True)
        dr = _ln_back(dx2 * g_ref[...], xhat, rstd)
        dr_ref[...] = dr
        drb_ref[...] = dr.astype(BF16)

    row = lambda i: (i, 0)
    vec = pl.BlockSpec((1, D), lambda i: (0, 0))
    return pl.pallas_call(
        body, name=name,
        out_shape=(jax.ShapeDtypeStruct((S, D), F32), jax.ShapeDtypeStruct((S, D), BF16),
                   jax.ShapeDtypeStruct((1, D), F32), jax.ShapeDtypeStruct((1, D), F32),
                   jax.ShapeDtypeStruct((1, LANES), F32)),
        grid=(S // tm,),
        in_specs=[pl.BlockSpec((tm, D), row)] * 3 + [vec, vec],
        out_specs=(pl.BlockSpec((tm, D), row), pl.BlockSpec((tm, D), row), vec, vec,
                   pl.BlockSpec((1, LANES), lambda i: (0, 0))),
        compiler_params=_params("arbitrary"),
    )(x1, y2, target, g, b)


def _ln_bwd(x, y, g, dout, *, S, D, name):
    tm = min(256, S)

    def body(x_ref, y_ref, g_ref, d_ref, dr_ref, drb_ref, dg_ref, db_ref):
        @pl.when(pl.program_id(0) == 0)
        def _():
            dg_ref[...] = jnp.zeros_like(dg_ref)
            db_ref[...] = jnp.zeros_like(db_ref)

        xhat, rstd = _ln_stats(ALPHA * x_ref[...] + y_ref[...])
        d = d_ref[...]
        dg_ref[...] += jnp.sum(d * xhat, axis=0, keepdims=True)
        db_ref[...] += jnp.sum(d, axis=0, keepdims=True)
        dr = _ln_back(d * g_ref[...], xhat, rstd)
        dr_ref[...] = dr
        drb_ref[...] = dr.astype(BF16)

    row = lambda i: (i, 0)
    vec = pl.BlockSpec((1, D), lambda i: (0, 0))
    return pl.pallas_call(
        body, name=name,
        out_shape=(jax.ShapeDtypeStruct((S, D), F32), jax.ShapeDtypeStruct((S, D), BF16),
                   jax.ShapeDtypeStruct((1, D), F32), jax.ShapeDtypeStruct((1, D), F32)),
        grid=(S // tm,),
        in_specs=[pl.BlockSpec((tm, D), row), pl.BlockSpec((tm, D), row), vec, pl.BlockSpec((tm, D), row)],
        out_specs=(pl.BlockSpec((tm, D), row), pl.BlockSpec((tm, D), row), vec, vec),
        compiler_params=_params("arbitrary"),
    )(x, y, g, dout)


def _sb_scores(q, kj, diagonal, t):
    z = _dot_nt(q, kj)
    ls = _log_sigmoid(z)
    if not diagonal:
        return None, ls, ls - z
    valid = lax.broadcasted_iota(jnp.int32, (t, t), 1) < lax.broadcasted_iota(jnp.int32, (t, t), 0)
    return valid, ls, jnp.where(valid, ls - z, 0.0)


def _keep(valid, x):
    return x if valid is None else jnp.where(valid, x, 0.0)


def _sb_fwd(q2, kv, z2, *, S, D, name, t=SB_TILE):
    t = min(t, S)
    nq = S // t
    nh = D // DH_B

    def body(q_ref, k_ref, v_ref, z_ref, o_ref, hb_ref, tot_ref):
        i = pl.program_id(1)
        q = q_ref[...]
        _, _, after = _tri(min(t, MXU_DEPTH), lambda r, c: r > c)
        tri_after = after.astype(BF16)

        def step(j, carry, diagonal):
            acc, cr = carry
            off = pl.multiple_of(j * t, t)
            valid, ls, lneg = _sb_scores(q, k_ref[pl.ds(off, t), :], diagonal, t)
            between = cr + _block_sums(lneg, tri_after, True)
            a = _keep(valid, jnp.exp(ls + between))
            acc = acc + _dot_nn(a.astype(BF16), v_ref[pl.ds(off, t), :])
            return acc, cr + jnp.sum(lneg, axis=1, keepdims=True)

        carry = step(i, (jnp.zeros((t, DH_B), F32), jnp.zeros((t, 1), F32)), True)
        acc, cr = lax.fori_loop(0, i, lambda jj, c: step(i - 1 - jj, c, False), carry)
        o_ref[...] = acc
        z = z_ref[...]
        hb_ref[...] = (acc * (z * _sigmoid(z))).astype(BF16)
        tot_ref[0] = cr

    blk = pl.BlockSpec((t, DH_B), lambda h, i: (i, h))
    return pl.pallas_call(
        body, name=name,
        out_shape=(jax.ShapeDtypeStruct((S, D), F32), jax.ShapeDtypeStruct((S, D), BF16),
                   jax.ShapeDtypeStruct((nh, S, 1), F32)), grid=(nh, nq),
        in_specs=[blk, pl.BlockSpec((S, DH_B), lambda h, i: (0, h)),
                  pl.BlockSpec((S, DH_B), lambda h, i: (0, nh + h)), blk],
        out_specs=(blk, blk, pl.BlockSpec((1, t, 1), lambda h, i: (h, i, 0))),
        compiler_params=_params("parallel", "arbitrary"),
    )(q2, kv, kv, z2)


def _sb_bwd(q2, kv, dhb, att, z2, tot, *, S, D, name, t=SB_TILE):
    t = min(t, S)
    nq = S // t
    nh = D // DH_B

    def body(q_ref, k_ref, v_ref, dhb_ref, att_ref, z_ref, tot_ref, dq_ref, dk_ref, dv_ref, dz2_ref, dk_acc, dv_acc):
        i = pl.program_id(1)

        @pl.when(i == 0)
        def _():
            dk_acc[...] = jnp.zeros_like(dk_acc)
            dv_acc[...] = jnp.zeros_like(dv_acc)

        q = q_ref[...]
        z, dhb = z_ref[...], dhb_ref[...]
        sz = _sigmoid(z)
        dz2_ref[...] = (dhb * att_ref[...] * (sz * (1.0 + z * (1.0 - sz)))).astype(BF16)
        do_b = (dhb * (z * sz)).astype(BF16)
        _, _, after = _tri(min(t, MXU_DEPTH), lambda r, c: r > c)
        tri_after = after.astype(BF16)
        _, _, before = _tri(min(t, MXU_DEPTH), lambda r, c: r < c)
        tri_before = before.astype(BF16)

        def step(j, carry, diagonal):
            dq, rest, cg = carry
            off = pl.multiple_of(j * t, t)
            kj = k_ref[pl.ds(off, t), :]
            valid, ls, lneg = _sb_scores(q, kj, diagonal, t)
            rest = rest - jnp.sum(lneg, axis=1, keepdims=True)
            between = rest + _block_sums(lneg, tri_after, True)
            a = _keep(valid, jnp.exp(ls + between))
            g = _dot_nt(do_b, v_ref[pl.ds(off, t), :]) * a
            dv_acc[pl.ds(off, t), :] += _dot_tn(a.astype(BF16), do_b)
            e = cg + _block_sums(g, tri_before, False, passes=1)
            sig = jnp.exp(ls)
            dz = _keep(valid, g * (1.0 - sig) - e * sig)
            dz_b = dz.astype(BF16)
            dq = dq + _dot_nn(dz_b, kj)
            dk_acc[pl.ds(off, t), :] += _dot_tn(dz_b, q)
            return dq, rest, cg + jnp.sum(g, axis=1, keepdims=True)

        carry = lax.fori_loop(0, i, lambda j, c: step(j, c, False),
                              (jnp.zeros((t, DH_B), F32), tot_ref[0], jnp.zeros((t, 1), F32)))
        dq, _, _ = step(i, carry, True)
        dq_ref[...] = (dq * (DH_B ** -0.5)).astype(BF16)

        @pl.when(i == nq - 1)
        def _():
            dk_ref[...] = dk_acc[...].astype(BF16)
            dv_ref[...] = dv_acc[...].astype(BF16)

    blk = pl.BlockSpec((t, DH_B), lambda h, i: (i, h))
    sd = jax.ShapeDtypeStruct((S, D), BF16)
    return pl.pallas_call(
        body, name=name, out_shape=(sd, sd, sd, sd), grid=(nh, nq),
        in_specs=[blk, pl.BlockSpec((S, DH_B), lambda h, i: (0, h)),
                  pl.BlockSpec((S, DH_B), lambda h, i: (0, nh + h)), blk, blk, blk,
                  pl.BlockSpec((1, t, 1), lambda h, i: (h, i, 0))],
        out_specs=(blk, pl.BlockSpec((S, DH_B), lambda h, i: (0, h)),
                   pl.BlockSpec((S, DH_B), lambda h, i: (0, h)), blk),
        scratch_shapes=[pltpu.VMEM((S, DH_B), F32), pltpu.VMEM((S, DH_B), F32)],
        compiler_params=_params("parallel", "arbitrary"),
    )(q2, kv, kv, dhb, att, z2, tot)


def _adamw_math(w, g, m, v):
    mn = ADAM_B1 * m + (1.0 - ADAM_B1) * g
    vn = ADAM_B2 * v + (1.0 - ADAM_B2) * (g * g)
    m_hat = mn / (1.0 - ADAM_B1 ** ADAM_STEP)
    v_hat = vn / (1.0 - ADAM_B2 ** ADAM_STEP)
    return -ADAM_LR * (m_hat / (jnp.sqrt(v_hat) + ADAM_EPS) + ADAM_WD * w), mn, vn


def _adamw(w, g, m, v, *, name, tr=128):
    rows, cols = w.shape
    tr = min(tr, rows)
    assert rows % tr == 0

    def body(w_ref, g_ref, m_ref, v_ref, d_ref, mo_ref, vo_ref):
        d, mn, vn = _adamw_math(w_ref[...], g_ref[...], m_ref[...], v_ref[...])
        d_ref[...] = d
        mo_ref[...] = mn
        vo_ref[...] = vn

    blk = pl.BlockSpec((tr, cols), lambda i: (i, 0))
    sd = jax.ShapeDtypeStruct((rows, cols), F32)
    return pl.pallas_call(
        body, name=name, out_shape=(sd, sd, sd), grid=(rows // tr,),
        in_specs=[blk] * 4, out_specs=(blk, blk, blk),
        compiler_params=_params("parallel"),
    )(w, g, m, v)


def _local_step(x, target, wa, wg, wao, wbi, wkv, wbo, gate_b, conv_w, conv_b, head_g,
                a_ln_g, a_ln_b, b_ln_g, b_ln_b, *, S, D, nha, rs):
    nb = S // LANES
    kw = dict(S=S, D=D)
    xb = x.astype(BF16)
    wa_chunk = D if wa.ndim == 3 else None
    u = _mm(xb, wa, mode="nn", M=S, N=4 * D, K=D, out_dtype=F32, name="a_in", b_chunk=wa_chunk)
    ug = _mm(xb, wg, mode="nn", M=S, N=LANES, K=D, out_dtype=F32, name="a_in_gates")
    qk = _conv_fwd(u, conv_w, conv_b, name="conv_fwd", **kw)
    gt = ug[:, :2 * nha].T.reshape(2 * nha * nb, LANES)
    gbias = jnp.repeat(gate_b.reshape(2 * nha), nb).reshape(2 * nha * nb, 1)
    fcs, brow = _gates_fwd(gt, gbias, nha=nha, nb=nb, name="gates_fwd")
    fcol = fcs.reshape(nha, S, 1)
    brow = brow.reshape(nha, 1, S)
    h, hg, m, den = _mlstm_fwd(qk, u, fcol, brow, head_g, nha=nha, name="mlstm_fwd", **kw)
    y = _mm(hg, wao, mode="nn", M=S, N=D, K=D, out_dtype=F32, name="a_out")
    x1, x1b = _ln_fwd(x, y, a_ln_g, a_ln_b, name="ln_a_fwd", **kw)
    q2 = _mm(x1b, wbi, mode="nn", M=S, N=D, K=D, out_dtype=BF16, name="b_in_q", out_scale=DH_B ** -0.5)
    z2 = _mm(x1b, wbi, mode="nn", M=S, N=D, K=D, out_dtype=F32, name="b_in_z", b_off=(0, D))
    kv = _mm(x1b, wkv, mode="nn", M=S, N=2 * D, K=D, out_dtype=BF16, name="b_kv")
    att, hb, sb_tot = _sb_fwd(q2, kv, z2, name="sb_fwd", **kw)
    y2 = _mm(hb, wbo, mode="nn", M=S, N=D, K=D, out_dtype=F32, name="b_out")
    dr2, dr2b, d_bln_g, d_bln_b, loss = _ln_loss_bwd(x1, y2, target, b_ln_g, b_ln_b, name="ln_b_loss", **kw)
    g_wbo = _mm(hb, dr2b, mode="tn", M=D, N=D, K=S, out_dtype=BF16, name="g_b_out")
    dhb = _mm(dr2b, wbo, mode="nt", M=S, N=D, K=D, out_dtype=F32, name="d_b_out")
    dq2, dk2, dv2, dz2 = _sb_bwd(q2, kv, dhb, att, z2, sb_tot, name="sb_bwd", **kw)
    g_wbi = _mm(x1b, dq2, mode="tn", M=D, N=D, K=S, out_dtype=BF16, name="g_b_in_q",
                into=lax.empty((D, 2 * D), BF16))
    g_wbi = _mm(x1b, dz2, mode="tn", M=D, N=D, K=S, out_dtype=BF16, name="g_b_in_z", into=g_wbi, out_off=(0, D))
    g_wkv = _mm(x1b, dk2, mode="tn", M=D, N=D, K=S, out_dtype=BF16, name="g_kv_k",
                into=lax.empty((D, 2 * D), BF16))
    g_wkv = _mm(x1b, dv2, mode="tn", M=D, N=D, K=S, out_dtype=BF16, name="g_kv_v", into=g_wkv, out_off=(0, D))
    dq2 = rs.begin("b", dict(wbo=g_wbo, wbi=g_wbi, wkv=g_wkv), dq2)
    dx1 = _mm(dq2, wbi, mode="nt", M=S, N=D, K=D, out_dtype=F32, name="d_b_in_q", acc_in=dr2, acc_scale=ALPHA)
    dx1 = _mm(dz2, wbi, mode="nt", M=S, N=D, K=D, out_dtype=F32, name="d_b_in_z", b_off=(0, D), acc_in=dx1)
    dx1 = _mm(dk2, wkv, mode="nt", M=S, N=D, K=D, out_dtype=F32, name="d_kv_k", acc_in=dx1)
    dx1 = _mm(dv2, wkv, mode="nt", M=S, N=D, K=D, out_dtype=F32, name="d_kv_v", b_off=(0, D), acc_in=dx1)
    dx1 = rs.pair("b", dx1)
    dr, drb, d_aln_g, d_aln_b = _ln_bwd(x, y, a_ln_g, dx1, name="ln_a_bwd", **kw)
    g_wao = _mm(hg, drb, mode="tn", M=D, N=D, K=S, out_dtype=BF16, name="g_a_out")
    drb = rs.begin("a", dict(wao=g_wao), drb)
    dhg = _mm(drb, wao, mode="nt", M=S, N=D, K=D, out_dtype=F32, name="d_a_out")
    du = lax.empty((S, 4 * D), BF16)
    du, dh, d_head_g = _hgate_bwd(dhg, h, u, head_g, du, name="hgate_bwd", **kw)
    dh = rs.pair("a", dh)
    du, dq, dk, rowsum, colsum = _mlstm_bwd(qk, u, fcol, brow, m, den, dh, h, du, nha=nha, name="mlstm_bwd", **kw)
    dgt, dgtot = _gates_bwd(rowsum.reshape(nha * nb, LANES), colsum.reshape(nha * nb, LANES), gt, gbias,
                            nha=nha, nb=nb, name="gates_bwd")
    d_gate_b = dgtot[::nb, 0].reshape(1, 2 * nha)
    dgp = jnp.pad(dgt.reshape(2 * nha, S).T, ((0, 0), (0, LANES - 2 * nha))).astype(BF16)
    du, d_conv_w, d_conv_b = _conv_bwd(u, dq, dk, conv_w, conv_b, du, name="conv_bwd", **kw)
    small = dict(gate_b=d_gate_b, conv_w=d_conv_w, conv_b=d_conv_b, head_g=d_head_g,
                 a_ln_g=d_aln_g, a_ln_b=d_aln_b, b_ln_g=d_bln_g, b_ln_b=d_bln_b)
    du = rs.chip("b", du)
    g_wa = _mm(xb, du, mode="tn", M=D, N=4 * D, K=S, out_dtype=BF16, name="g_a_in",
               into=lax.empty((D, 4 * D + LANES), BF16))
    g_wa = _mm(xb, dgp, mode="tn", M=D, N=LANES, K=S, out_dtype=BF16, name="g_a_in_gates", into=g_wa,
               out_off=(0, 4 * D))
    du = rs.begin("w", dict(wa_full=g_wa), du)
    du = rs.pair("w", rs.chip("a", rs.small(small, loss, du)))
    du = rs.finish("a", rs.finish("b", du))
    dx = _mm(du, wa, mode="nt", M=S, N=D, K=4 * D, out_dtype=F32, name="d_a_in", acc_in=dr, acc_scale=ALPHA,
             b_chunk=wa_chunk)
    dx = _mm(dgp, wg, mode="nt", M=S, N=D, K=LANES, out_dtype=F32, name="d_a_in_gates", acc_in=dx)
    return dx


def _coords():
    return lax.axis_index("x"), lax.axis_index("y"), lax.axis_index("c")


def _other_chips(x, y):
    return [(1 - x, y), (x, 1 - y), (1 - x, 1 - y)]


def _rows(ref, start, size):
    return ref.at[pl.ds(pl.multiple_of(start, size), size), :]


def _window(kind, ref, shard_shape, j, hf=None, quarter=None):
    r, cw = shard_shape
    row0, nr = (0, r) if hf is None else (hf * (r // 2), r // 2)
    if quarter is not None:
        row0, nr = row0 + quarter * (r // 4), r // 4
    if kind == "stack":
        return ref.at[j, pl.ds(pl.multiple_of(row0, nr), nr), :]
    if kind == "rows":
        return ref.at[pl.ds(pl.multiple_of(j * r + row0, nr), nr), :]
    assert kind == "cols"
    return ref.at[pl.ds(pl.multiple_of(row0, nr), nr), pl.ds(pl.multiple_of(j * cw, cw), cw)]


def _gathered_shape(kind, shard_shape):
    r, cw = shard_shape
    return {"stack": (N_CHIPS, r, cw), "rows": (N_CHIPS * r, cw), "cols": (r, N_CHIPS * cw)}[kind]


def _remote(src, dst, send_sems, recv_sems, k, to):
    return pltpu.make_async_remote_copy(src_ref=src, dst_ref=dst, send_sem=send_sems.at[k],
                                        recv_sem=recv_sems.at[k], device_id=to, device_id_type=MESH)


def _allgather_copies(s_refs, sm_ref, g_refs, smg_ref, send_sems, recv_sems, kinds, shapes):
    n = len(s_refs)
    x, y, c = _coords()
    me, sibling, chips = 2 * x + y, (x, y, 1 - c), _other_chips(x, y)
    ids = [2 * cx + cy for cx, cy in chips]
    d2d, own_base, small_base = 4 * n, 7 * n, 8 * n

    def win(a, j, hf=None, quarter=None):
        return _window(kinds[a], g_refs[a], shapes[a], j, hf, quarter)

    def small_win(j):
        w = sm_ref.shape[1]
        return smg_ref.at[:, pl.ds(pl.multiple_of(j * w, w), w)]

    def rc(src, dst, k, to):
        return _remote(src, dst, send_sems, recv_sems, k, to)

    first = [rc(s_refs[a], win(a, me), own_base + a, sibling) for a in range(n)]
    for a in range(n):
        src = _rows(s_refs[a], c * (shapes[a][0] // 2), shapes[a][0] // 2)
        first += [rc(src, win(a, me, c), 4 * a + k, (*chips[k], c)) for k in range(2)]
    if sm_ref is not None:
        first.append(rc(sm_ref, small_win(me), small_base + 3, sibling))
        first += [rc(sm_ref, small_win(me), small_base + k, (*chip, c)) for k, chip in enumerate(chips)]
    for cp in first:
        cp.start()
    passed = []
    for a in range(n):
        for k in range(2):
            rc(win(a, ids[k], c), win(a, ids[k], c), 4 * a + k, sibling).wait_recv()
            part = win(a, ids[k], c, quarter=k)
            passed.append(rc(part, part, 4 * a + 2 + k, (*chips[1 - k], c)))
            passed.append(rc(win(a, ids[k], c), win(a, ids[k], c), d2d + 3 * a + k, sibling))
            passed[-2].start()
            passed[-1].start()
    for a in range(n):
        for k in range(2):
            part = win(a, ids[2], c, quarter=k)
            rc(part, part, 4 * a + 2 + k, sibling).wait_recv()
        passed.append(rc(win(a, ids[2], c), win(a, ids[2], c), d2d + 3 * a + 2, sibling))
        passed[-1].start()
    if sm_ref is not None:
        for k in range(3):
            rc(small_win(ids[k]), small_win(ids[k]), small_base + k, sibling).wait_recv()
        rc(small_win(me), small_win(me), small_base + 3, sibling).wait_recv()
    for a in range(n):
        for k in range(3):
            rc(win(a, ids[k], 1 - c), win(a, ids[k], 1 - c), d2d + 3 * a + k, sibling).wait_recv()
        rc(win(a, me), win(a, me), own_base + a, sibling).wait_recv()
    for cp in first + passed:
        cp.wait_send()


def _peer_handshake(peers):
    barrier = pltpu.get_barrier_semaphore()
    for peer in peers:
        pl.semaphore_signal(barrier, inc=1, device_id=peer, device_id_type=MESH)
    pl.semaphore_wait(barrier, len(peers))


def _behind(copies, inputs, out_structs, nsem, peers, *, name, collective_id):
    hbm = pltpu.MemorySpace.HBM
    in_refs = [jax.new_ref(a, memory_space=hbm) for a in inputs]
    out_refs = [jax.empty_ref(st, memory_space=hbm) for st in out_structs]

    @pl.kernel(mesh=plsc.ScalarSubcoreMesh(axis_name="sequencer", num_cores=1), name=name,
               scratch_types=(pltpu.SemaphoreType.DMA((nsem,)), pltpu.SemaphoreType.DMA((nsem,))),
               compiler_params=pltpu.CompilerParams(collective_id=collective_id))
    def launch(send_sems, recv_sems):
        _peer_handshake(peers(*_coords()))
        copies(in_refs, out_refs, send_sems, recv_sems)

    launch()
    return [r[...] for r in out_refs]


def _sibling(x, y, c):
    return [(x, y, 1 - c)]


def _same_core_of_other_chips(x, y, c):
    return [(cx, cy, c) for cx, cy in _other_chips(x, y)]


def _sibling_and_other_chips(x, y, c):
    return _sibling(x, y, c) + _same_core_of_other_chips(x, y, c)


ID_ALL_PEERS, ID_SIBLING, ID_CHIPS = 0, 1, 2


def _allgather_behind(shards, kinds, small=None, *, name):
    n = len(shards)
    shapes = [sh.shape for sh in shards]

    def copies(in_refs, out_refs, send_sems, recv_sems):
        if small is None:
            _allgather_copies(in_refs, None, out_refs, None, send_sems, recv_sems, kinds, shapes)
        else:
            _allgather_copies(in_refs[:n], in_refs[n], out_refs[:n], out_refs[n], send_sems, recv_sems, kinds, shapes)

    outs = [jax.ShapeDtypeStruct(_gathered_shape(kinds[a], shapes[a]), shards[a].dtype) for a in range(n)]
    inputs = list(shards)
    if small is not None:
        outs.append(jax.ShapeDtypeStruct((small.shape[0], N_CHIPS * small.shape[1]), small.dtype))
        inputs.append(small)
    return _behind(copies, inputs, outs, 8 * n + 4, _sibling_and_other_chips, name=name, collective_id=ID_ALL_PEERS)


def _rs_pair_exchange(views, *, name):
    n = len(views)

    def copies(g_refs, p_refs, send_sems, recv_sems):
        x, y, c = _coords()
        cps = []
        for a in range(n):
            r2 = views[a].shape[1] // 2
            src = g_refs[a].at[:, pl.ds(pl.multiple_of((1 - c) * r2, r2), r2), :]
            cps.append(_remote(src, p_refs[a], send_sems, recv_sems, a, (x, y, 1 - c)))
        for cp in cps:
            cp.start()
        for cp in cps:
            cp.wait()

    outs = [jax.ShapeDtypeStruct((v.shape[0], v.shape[1] // 2, v.shape[2]), v.dtype) for v in views]
    return _behind(copies, views, outs, n, _sibling, name=name, collective_id=ID_SIBLING)


def _add_half(view, part, core, *, name):
    nch, r, cols = view.shape
    r2 = r // 2
    tr = min(512 if cols <= 4096 else 128, r2)
    nt = r2 // tr

    def body(c_ref, g_ref, p_ref, o_ref):
        del c_ref
        o_ref[...] = (g_ref[...].astype(F32) + p_ref[...].astype(F32)).astype(o_ref.dtype)

    return pl.pallas_call(
        body, name=name, out_shape=jax.ShapeDtypeStruct((nch, r2, cols), view.dtype),
        grid_spec=pltpu.PrefetchScalarGridSpec(
            num_scalar_prefetch=1, grid=(nch, nt),
            in_specs=[pl.BlockSpec((1, tr, cols), lambda ch, i, c_ref: (ch, c_ref[0] * nt + i, 0)),
                      pl.BlockSpec((1, tr, cols), lambda ch, i, c_ref: (ch, i, 0))],
            out_specs=pl.BlockSpec((1, tr, cols), lambda ch, i, c_ref: (ch, i, 0))),
        compiler_params=_params("parallel", "parallel"),
    )(core, view, part)


def _chunk(kind, ref, j, cw):
    if kind == "cols":
        return ref.at[0, :, pl.ds(pl.multiple_of(j * cw, cw), cw)]
    if isinstance(kind, tuple):
        return ref.at[0, :, pl.ds(pl.multiple_of(j * kind[1], LANES), cw)]
    return ref.at[j]


def _rs_chips(pairs, kinds, *, name):
    n = len(pairs)
    half_shapes = []
    for a in range(n):
        nch, r2, cols = pairs[a].shape
        if isinstance(kinds[a], tuple):
            half_shapes.append((r2, kinds[a][2]))
        else:
            half_shapes.append((r2, cols // N_CHIPS) if kinds[a] == "cols" else (r2, cols))

    def copies(q_refs, t_refs, send_sems, recv_sems):
        x, y, c = _coords()
        chips = _other_chips(x, y)
        sends = []
        for a in range(n):
            for k, (cx, cy) in enumerate(chips):
                src = _chunk(kinds[a], q_refs[a], 2 * cx + cy, half_shapes[a][1])
                sends.append(_remote(src, t_refs[a].at[k], send_sems, recv_sems, 3 * a + k, (cx, cy, c)))
        for cp in sends:
            cp.start()
        for a in range(n):
            for k in range(3):
                slot = t_refs[a].at[k]
                _remote(slot, slot, send_sems, recv_sems, 3 * a + k, (x, y, c)).wait_recv()
        for cp in sends:
            cp.wait_send()

    outs = [jax.ShapeDtypeStruct((3, *half_shapes[a]), pairs[a].dtype) for a in range(n)]
    return _behind(copies, pairs, outs, 3 * n, _same_core_of_other_chips, name=name, collective_id=ID_CHIPS)


def _sum_chips(pair, parts, chip, kind, *, name):
    _, r, cols = parts.shape
    tr = min(128, r)
    if kind == "cols":
        own_spec = pl.BlockSpec((1, tr, cols), lambda i, chip_ref: (0, i, chip_ref[0]))
    else:
        own_spec = pl.BlockSpec((1, tr, cols), lambda i, chip_ref: (chip_ref[0], i, 0))

    def body(chip_ref, q_ref, p_ref, o_ref):
        del chip_ref
        acc = q_ref[0].astype(F32)
        for s in range(3):
            acc = acc + p_ref[s].astype(F32)
        o_ref[...] = acc

    return pl.pallas_call(
        body, name=name, out_shape=jax.ShapeDtypeStruct((r, cols), F32),
        grid_spec=pltpu.PrefetchScalarGridSpec(
            num_scalar_prefetch=1, grid=(r // tr,),
            in_specs=[own_spec, pl.BlockSpec((3, tr, cols), lambda i, chip_ref: (0, i, 0))],
            out_specs=pl.BlockSpec((tr, cols), lambda i, chip_ref: (i, 0))),
        compiler_params=_params("parallel"),
    )(chip, pair, parts)


def _rs_share(halves, *, name):
    n = len(halves)

    def copies(h_refs, g_refs, send_sems, recv_sems):
        x, y, c = _coords()
        cps = [_remote(h_refs[a], g_refs[a], send_sems, recv_sems, a, (x, y, 1 - c)) for a in range(n)]
        for cp in cps:
            cp.start()
        for cp in cps:
            cp.wait()

    outs = [jax.ShapeDtypeStruct(h.shape, h.dtype) for h in halves]
    return _behind(copies, halves, outs, n, _sibling, name=name, collective_id=ID_SIBLING)


class _GradReducer:
    def __init__(self, w, m, v, long_name, core, chip_ix, as_views, small_step):
        self.w, self.m, self.v, self.long_name = w, m, v, long_name
        self.core, self.chip_ix, self.as_views, self.small_step = core, chip_ix, as_views, small_step
        self.groups, self.results, self.small_results = {}, {}, None

    def small(self, small, loss_row, tie):
        (small, loss_row), tie = lax.optimization_barrier(((small, loss_row), tie))
        self.small_results, tie = lax.optimization_barrier((self.small_step(small, loss_row), tie))
        return tie

    def begin(self, tag, grads, tie):
        keys, views, kinds = self.as_views(grads)
        views, tie = lax.optimization_barrier((views, tie))
        parts = _rs_pair_exchange(views, name="rs_pair_exchange_" + tag)
        self.groups[tag] = dict(keys=keys, views=views, kinds=kinds, parts=parts)
        return tie

    def pair(self, tag, tie):
        g = self.groups[tag]
        g["parts"], tie = lax.optimization_barrier((g["parts"], tie))
        pairs = [_add_half(v, p, self.core, name="pair_sum_" + self.long_name[k])
                 for v, p, k in zip(g["views"], g["parts"], g["keys"])]
        pairs, tie = lax.optimization_barrier((pairs, tie))
        g["pairs"] = pairs
        g["slots"] = _rs_chips(pairs, g["kinds"], name="rs_chips_" + tag)
        return tie

    def chip(self, tag, tie, update_mine=False):
        g = self.groups[tag]
        g["slots"], tie = lax.optimization_barrier((g["slots"], tie))
        halves = []
        for q, t, kd, k in zip(g["pairs"], g["slots"], g["kinds"], g["keys"]):
            if isinstance(kd, tuple):
                q = lax.dynamic_slice_in_dim(q[0], self.chip_ix[0] * kd[1], kd[2], axis=1)[None]
                halves.append(_sum_chips(q, t, jnp.zeros_like(self.chip_ix), "stack",
                                         name="chip_sum_" + self.long_name[k]))
            else:
                halves.append(_sum_chips(q, t, self.chip_ix, kd, name="chip_sum_" + self.long_name[k]))
        halves, tie = lax.optimization_barrier((halves, tie))
        g["halves"] = halves
        g["others"] = _rs_share(halves, name="rs_share_" + tag)
        if update_mine:
            mine = [_adamw_rows(self.w[k], self._shard(a, kd), self.m[k], self.v[k], self.core,
                                name="adamw_mine_" + self.long_name[k])
                    for k, a, kd in zip(g["keys"], halves, g["kinds"])]
            g["mine"], tie = lax.optimization_barrier((mine, tie))
        return tie

    def _shard(self, half, kind):
        if not isinstance(kind, tuple):
            return half
        return lax.dynamic_slice_in_dim(half, self.chip_ix[0] * kind[3], kind[4], axis=1)

    def finish(self, tag, tie):
        g = self.groups[tag]
        g["others"], tie = lax.optimization_barrier((g["others"], tie))
        if "mine" in g:
            out = [_adamw_rows(self.w[k], self._shard(b, kd), self.m[k], self.v[k], 1 - self.core, into=done,
                               name="adamw_theirs_" + self.long_name[k])
                   for k, b, kd, done in zip(g["keys"], g["others"], g["kinds"], g["mine"])]
        else:
            out = [_adamw_halves(self.w[k], a, b, self.m[k], self.v[k], self.core, name="adamw_" + self.long_name[k])
                   for k, a, b in zip(g["keys"], g["halves"], g["others"])]
        out, tie = lax.optimization_barrier((out, tie))
        self.results.update(zip(g["keys"], out))
        return tie


def _adamw_halves(w, mine, theirs, m, v, core, *, name, tr=128):
    rows, cols = w.shape
    hr = rows // 2
    tr = min(tr, hr)
    nt = hr // tr

    def body(c_ref, w_ref, a_ref, b_ref, m_ref, v_ref, g_ref, d_ref, mo_ref, vo_ref):
        gg = jnp.where(pl.program_id(0) == c_ref[0], a_ref[...], b_ref[...])
        g_ref[...] = gg
        d, mn, vn = _adamw_math(w_ref[...], gg, m_ref[...], v_ref[...])
        d_ref[...] = d
        mo_ref[...] = mn
        vo_ref[...] = vn

    full = pl.BlockSpec((tr, cols), lambda hf, i, c_ref: (hf * nt + i, 0))
    half = pl.BlockSpec((tr, cols), lambda hf, i, c_ref: (i, 0))
    sd = jax.ShapeDtypeStruct((rows, cols), F32)
    return pl.pallas_call(
        body, name=name, out_shape=(sd, sd, sd, sd),
        grid_spec=pltpu.PrefetchScalarGridSpec(
            num_scalar_prefetch=1, grid=(2, nt),
            in_specs=[full, half, half, full, full], out_specs=(full, full, full, full)),
        compiler_params=_params("parallel", "parallel"),
    )(core, w, mine, theirs, m, v)


def _adamw_rows(w, g_half, m, v, half, *, name, into=None, tr=128):
    rows, cols = w.shape
    hr = rows // 2
    tr = min(tr, hr)
    nt = hr // tr

    def body(h_ref, w_ref, a_ref, m_ref, v_ref, *rest):
        g_ref, d_ref, mo_ref, vo_ref = rest[-4:]
        gg = a_ref[...]
        g_ref[...] = gg
        d, mn, vn = _adamw_math(w_ref[...], gg, m_ref[...], v_ref[...])
        d_ref[...] = d
        mo_ref[...] = mn
        vo_ref[...] = vn

    full = pl.BlockSpec((tr, cols), lambda i, h_ref: (h_ref[0] * nt + i, 0))
    sd = jax.ShapeDtypeStruct((rows, cols), F32)
    extra = [] if into is None else list(into)
    return pl.pallas_call(
        body, name=name, out_shape=(sd, sd, sd, sd),
        grid_spec=pltpu.PrefetchScalarGridSpec(
            num_scalar_prefetch=1, grid=(nt,),
            in_specs=[full, pl.BlockSpec((tr, cols), lambda i, h_ref: (i, 0)), full, full] + [ANY] * len(extra),
            out_specs=(full, full, full, full)),
        input_output_aliases={5 + k: k for k in range(len(extra))},
        compiler_params=_params("parallel"),
    )(half, w, g_half, m, v, *extra)


def _allreduce_small(vec):
    m_per, ncol = vec.shape
    n_dev = 2 * N_CHIPS

    def body(x_ref, out_ref, sum_ref, send_sems, recv_sems, local_sem):
        x, y, c = _coords()
        me, sibling = (x, y, c), (x, y, 1 - c)
        chips = _other_chips(x, y)

        def rows(px, py, pc):
            return out_ref.at[pl.ds(pl.multiple_of((4 * px + 2 * py + pc) * m_per, m_per), m_per), :]

        def copy(k, block, to, src=None):
            return _remote(rows(*block) if src is None else src, rows(*block), send_sems, recv_sems, k, to)

        mine = pltpu.make_async_copy(x_ref, rows(*me), local_sem)
        mine.start()
        first = [copy(0, me, sibling, src=x_ref)]
        first += [copy(1 + j, me, (*chip, c), src=x_ref) for j, chip in enumerate(chips)]
        for cp in first:
            cp.start()
        passed = [copy(4 + j, (*chip, c), sibling) for j, chip in enumerate(chips)]
        for j, chip in enumerate(chips):
            copy(1 + j, (*chip, c), me).wait_recv()
            passed[j].start()
        copy(0, sibling, me).wait_recv()
        for j, chip in enumerate(chips):
            copy(4 + j, (*chip, 1 - c), me).wait_recv()
        for cp in first + passed:
            cp.wait_send()
        mine.wait()
        acc = out_ref[0:m_per, :]
        for d in range(1, n_dev):
            acc = acc + out_ref[d * m_per:(d + 1) * m_per, :]
        sum_ref[...] = acc

    vm = pl.BlockSpec(memory_space=pltpu.VMEM)
    return pl.pallas_call(
        body, name="allreduce_small",
        out_shape=(jax.ShapeDtypeStruct((n_dev * m_per, ncol), vec.dtype), jax.ShapeDtypeStruct((m_per, ncol), vec.dtype)),
        in_specs=[vm], out_specs=(vm, vm),
        scratch_shapes=[pltpu.SemaphoreType.DMA((7,)), pltpu.SemaphoreType.DMA((7,)), pltpu.SemaphoreType.DMA],
    )(vec)[1]


def _pack_rows(pieces, total_rows):
    rows = []
    for p in pieces:
        flat = p.reshape(-1)
        flat = jnp.pad(flat, (0, (-flat.shape[0]) % LANES))
        rows.append(flat.reshape(-1, LANES))
    out = jnp.concatenate(rows, axis=0)
    return jnp.pad(out, ((0, total_rows - out.shape[0]), (0, 0)))


def _unpack_rows(packed, shapes):
    out, r = [], 0
    for shp in shapes:
        size = 1
        for d in shp:
            size *= d
        nr = -(-size // LANES)
        out.append(packed[r:r + nr].reshape(-1)[:size].reshape(shp))
        r += nr
    return out


def _round_up(v, m):
    return -(-v // m) * m


def kernel(x, a_w_in, a_gate_b, a_conv_w, a_conv_b, a_head_g, a_w_out, a_ln_g, a_ln_b, kv_w, b_w_in, b_w_out, b_ln_g, b_ln_b, loss_target, m_a_w_in, m_a_gate_b, m_a_conv_w, m_a_conv_b, m_a_head_g, m_a_w_out, m_a_ln_g, m_a_ln_b, m_kv_w, m_b_w_in, m_b_w_out, m_b_ln_g, m_b_ln_b, v_a_w_in, v_a_gate_b, v_a_conv_w, v_a_conv_b, v_a_head_g, v_a_w_out, v_a_ln_g, v_a_ln_b, v_kv_w, v_b_w_in, v_b_w_out, v_b_ln_g, v_b_ln_b):
    _, S, D = x.shape
    nha = a_gate_b.shape[1] // 2
    chip = 2 * lax.axis_index("x") + lax.axis_index("y")
    core = lax.axis_index("c").astype(jnp.int32).reshape(1)
    dq = D // N_CHIPS

    ca = a_w_in.shape[2]
    edge = (N_CHIPS - 1) * (ca - D)
    wide = _round_up(ca + edge, LANES)
    shifted = lax.dynamic_slice_in_dim(
        jnp.pad(a_w_in[0].astype(BF16), ((0, 0), (edge, wide - ca))), edge - chip * (ca - D), wide, axis=1)
    shards = [shifted, a_w_out[0].astype(BF16), kv_w.astype(BF16), b_w_in[0].astype(BF16), b_w_out[0].astype(BF16)]
    kinds = ["stack", "rows", "cols", "cols", "rows"]
    small_shard = jnp.concatenate([a_conv_w[0], a_conv_b, a_head_g, a_ln_g, a_ln_b], axis=0)
    wa_g, small_full = _allgather_behind(shards[:1], kinds[:1], small_shard, name="allgather_first")
    wao, = _allgather_behind(shards[1:2], kinds[1:2], name="allgather_a_w_out")
    wkv, wbi = _allgather_behind(shards[2:4], kinds[2:4], name="allgather_b_in")
    wbo, = _allgather_behind(shards[4:], kinds[4:], name="allgather_b_w_out")
    wa = _join_chunk_edges(wa_g, D)
    wg = wa[N_CHIPS - 1, :, D:D + LANES]
    conv_w, conv_b, head_g, ln_g_a, ln_b_a = (small_full[0:CONV_A], small_full[4:5], small_full[5:6],
                                              small_full[6:7], small_full[7:8])

    chip_ix = chip.astype(jnp.int32).reshape(1)
    (w_wa, m_wa, v_wa), x = lax.optimization_barrier(((a_w_in[0], m_a_w_in[0], v_a_w_in[0]), x))
    w_big = dict(wa=w_wa, wao=a_w_out[0], wkv=kv_w, wbi=b_w_in[0], wbo=b_w_out[0])
    m_big = dict(wa=m_wa, wao=m_a_w_out[0], wkv=m_kv_w, wbi=m_b_w_in[0], wbo=m_b_w_out[0])
    v_big = dict(wa=v_wa, wao=v_a_w_out[0], wkv=v_kv_w, wbi=v_b_w_in[0], wbo=v_b_w_out[0])
    long_name = dict(wa="a_w_in", wao="a_w_out", wkv="kv_w", wbi="b_w_in", wbo="b_w_out")

    def as_views(g):
        if "wa_full" in g:
            return ["wa"], [g["wa_full"][None]], [("win", D, wide, ca - D, ca)]
        keys = list(g)
        views = [g[k].reshape(N_CHIPS, dq, D) if k in ("wao", "wbo") else g[k][None] for k in keys]
        return keys, views, ["stack" if k in ("wao", "wbo") else "cols" for k in keys]

    w_small = [a_gate_b, a_conv_w[0], a_conv_b, a_head_g, a_ln_g, a_ln_b, b_ln_g, b_ln_b]
    m_small = [m_a_gate_b, m_a_conv_w[0], m_a_conv_b, m_a_head_g, m_a_ln_g, m_a_ln_b, m_b_ln_g, m_b_ln_b]
    v_small = [v_a_gate_b, v_a_conv_w[0], v_a_conv_b, v_a_head_g, v_a_ln_g, v_a_ln_b, v_b_ln_g, v_b_ln_b]

    def small_step(small, loss_row):
        order = ["conv_w", "conv_b", "head_g", "a_ln_g", "a_ln_b", "b_ln_g", "b_ln_b", "gate_b"]
        full_shapes = [(CONV_A, D), (1, D), (1, D), (1, D), (1, D), (1, D), (1, D), (1, 2 * nha)]
        n_rows = sum(-(-(s[0] * s[1]) // LANES) for s in full_shapes) + 1
        total = _allreduce_small(_pack_rows([small[k] for k in order] + [loss_row], _round_up(n_rows, 8)))
        sums = dict(zip(order, _unpack_rows(total, full_shapes)))

        def mine(v):
            return lax.dynamic_slice_in_dim(v, chip * dq, dq, axis=1)

        g_small = [sums["gate_b"], mine(sums["conv_w"]), mine(sums["conv_b"]), mine(sums["head_g"]),
                   mine(sums["a_ln_g"]), mine(sums["a_ln_b"]), sums["b_ln_g"], sums["b_ln_b"]]
        rows_small = _round_up(sum(-(-(w.shape[0] * w.shape[1]) // LANES) for w in w_small), 8)
        upd_small = _adamw(_pack_rows(w_small, rows_small), _pack_rows(g_small, rows_small),
                           _pack_rows(m_small, rows_small), _pack_rows(v_small, rows_small), name="adamw_small")
        return total[n_rows - 1, 0], g_small, upd_small

    rs = _GradReducer(w_big, m_big, v_big, long_name, core, chip_ix, as_views, small_step)
    grad_x = _local_step(
        x[0], loss_target[0], wa, wg, wao, wbi, wkv, wbo, a_gate_b, conv_w, conv_b, head_g,
        ln_g_a, ln_b_a, b_ln_g, b_ln_b, S=S, D=D, nha=nha, rs=rs)
    grad_x = rs.finish("w", rs.chip("w", grad_x, update_mine=True))
    upd_big = [rs.results[k] for k in ("wa", "wao", "wkv", "wbi", "wbo")]
    g_big = [u[0] for u in upd_big]
    loss, g_small, upd_small = rs.small_results
    d_small, mn_small, vn_small = (_unpack_rows(u, [w.shape for w in w_small]) for u in upd_small)

    def assemble(big5, small8):
        awi, awo, kvw, bwi, bwo = big5
        gb, cw, cb, hg, alg, alb, blg, blb = small8
        return [awi[None], gb, cw[None], cb, hg, awo[None], alg, alb, kvw, bwi[None], bwo[None], blg, blb]

    grads = assemble(g_big, g_small)
    deltas = assemble([u[1] for u in upd_big], d_small)
    new_m = assemble([u[2] for u in upd_big], mn_small)
    new_v = assemble([u[3] for u in upd_big], vn_small)
    return (loss, grad_x[None], *grads, *deltas, *new_m, *new_v)
```

```python
import functools

import jax
import jax.numpy as jnp
from jax import lax
from jax.experimental import pallas as pl
from jax.experimental.pallas import tpu as pltpu
from jax.experimental.pallas import tpu_sc as plsc

F32 = jnp.float32
BF16 = jnp.bfloat16

DEPTH = 2
ALPHA = (2.0 * DEPTH) ** 0.25
LN_EPS = 1e-5
DK_A = 128
DV_A = 256
DH_B = 128
SB_TILE = 1024
ML_TQ = 1024
CONV_A = 4
ADAM_LR = 0.001
ADAM_B1 = 0.9
ADAM_B2 = 0.999
ADAM_EPS = 1e-08
ADAM_WD = 0.01
ADAM_STEP = 10
N_CHIPS = 4
LANES = 128
MXU_DEPTH = 256
V7X_VMEM_BYTES = 64 * 1024 * 1024
VMEM_LIMIT = (V7X_VMEM_BYTES * 3) // 4
NEG_BIG = -1e30
MESH = pl.DeviceIdType.MESH
ANY = pl.BlockSpec(memory_space=pl.ANY)


def _params(*sem):
    return pltpu.CompilerParams(dimension_semantics=sem, vmem_limit_bytes=VMEM_LIMIT)


def _dot(a, b, dims):
    return lax.dot_general(a, b, (dims, ((), ())), preferred_element_type=F32)


def _dot_nn(a, b):
    return _dot(a, b, ((1,), (0,)))


def _dot_nt(a, b):
    return _dot(a, b, ((1,), (1,)))


def _dot_tn(a, b):
    return _dot(a, b, ((0,), (0,)))


def _split2(x):
    hi = x.astype(BF16)
    lo = (x - hi.astype(F32)).astype(BF16)
    return hi, lo


def _split3(x):
    hi = x.astype(BF16)
    r = x - hi.astype(F32)
    mid = r.astype(BF16)
    lo = (r - mid.astype(F32)).astype(BF16)
    return hi, mid, lo


def _mask_dot2(x, t01):
    hi, lo = _split2(x)
    return _dot_nn(hi, t01) + _dot_nn(lo, t01)


def _block_sums(x, tri, later, passes=2):
    dot = _mask_dot2 if passes == 2 else (lambda v, t01: _dot_nn(v.astype(BF16), t01))
    sub = tri.shape[0]
    n = x.shape[1] // sub
    if n == 1:
        return dot(x, tri)
    parts = [x[:, b * sub:(b + 1) * sub] for b in range(n)]
    sums = [jnp.sum(p, axis=1, keepdims=True) for p in parts]
    out = []
    for b in range(n):
        acc = dot(parts[b], tri)
        for o in (range(b + 1, n) if later else range(b)):
            acc = acc + sums[o]
        out.append(acc)
    return jnp.concatenate(out, axis=1)


def _mask_dot3(x, t01):
    hi, mid, lo = _split3(x)
    return _dot_nn(hi, t01) + _dot_nn(mid, t01) + _dot_nn(lo, t01)


def _mask_dot3_left(t01, x):
    hi, mid, lo = _split3(x)
    return _dot_nn(t01, hi) + _dot_nn(t01, mid) + _dot_nn(t01, lo)


def _log_sigmoid(z):
    return jnp.minimum(z, 0.0) - jnp.log(1.0 + jnp.exp(-jnp.abs(z)))


def _sigmoid(z):
    return 1.0 / (1.0 + jnp.exp(-z))


def _mm(a, b, *, mode, M, N, K, out_dtype, name, tm=1024, tn=1024, tk=2048,
        a_off=(0, 0), b_off=(0, 0), acc_in=None, acc_scale=1.0, into=None, out_off=(0, 0), b_chunk=None, out_scale=None):
    tm, tn, tk = min(tm, M), min(tn, N), min(tk, K)
    if b_chunk is not None:
        tn, tk = (min(tn, b_chunk), tk) if mode == "nn" else (tn, min(tk, b_chunk))
        assert b.ndim == 3 and b_off == (0, 0) and mode in ("nn", "nt")
    assert M % tm == 0 and N % tn == 0 and K % tk == 0
    nk = K // tk
    ar, ac = a_off
    br, bc = b_off
    orow, ocol = out_off
    if mode in ("nn", "nt"):
        assert ar % tm == 0 and ac % tk == 0
        a_spec = pl.BlockSpec((tm, tk), lambda i, j, k: (i + ar // tm, k + ac // tk))
    else:
        assert ar % tk == 0 and ac % tm == 0
        a_spec = pl.BlockSpec((tk, tm), lambda i, j, k: (k + ar // tk, i + ac // tm))
    if b_chunk is not None and mode == "nn":
        per = b_chunk // tn
        b_spec = pl.BlockSpec((None, tk, tn), lambda i, j, k: (j // per, k, j % per))
    elif b_chunk is not None:
        per = b_chunk // tk
        b_spec = pl.BlockSpec((None, tn, tk), lambda i, j, k: (k // per, j, k % per))
    elif mode == "nt":
        assert br % tn == 0 and bc % tk == 0
        b_spec = pl.BlockSpec((tn, tk), lambda i, j, k: (j + br // tn, k + bc // tk))
    else:
        assert br % tk == 0 and bc % tn == 0
        b_spec = pl.BlockSpec((tk, tn), lambda i, j, k: (k + br // tk, j + bc // tn))
    assert orow % tm == 0 and ocol % tn == 0
    o_spec = pl.BlockSpec((tm, tn), lambda i, j, k: (i + orow // tm, j + ocol // tn))
    dims = {"nn": ((1,), (0,)), "nt": ((1,), (1,)), "tn": ((0,), (0,))}[mode]
    inputs, in_specs = [a, b], [a_spec, b_spec]
    has_acc = acc_in is not None
    if has_acc:
        inputs.append(acc_in)
        in_specs.append(pl.BlockSpec((tm, tn), lambda i, j, k: (i, j)))
    aliases = {}
    if into is not None:
        inputs.append(into)
        in_specs.append(ANY)
        aliases = {len(inputs) - 1: 0}
        out_shape = jax.ShapeDtypeStruct(into.shape, into.dtype)
        assert into.dtype == out_dtype
    else:
        out_shape = jax.ShapeDtypeStruct((M, N), out_dtype)

    def body(*refs):
        a_ref, b_ref = refs[0], refs[1]
        acc_in_ref = refs[2] if has_acc else None
        n_in = len(inputs)
        o_ref = refs[n_in]

        def first():
            if has_acc:
                return acc_scale * acc_in_ref[...]
            return None

        def scaled(r):
            return r if out_scale is None else r * out_scale

        if nk == 1:
            r = _dot(a_ref[...], b_ref[...], dims)
            f = first()
            if f is not None:
                r = r + f
            o_ref[...] = scaled(r).astype(o_ref.dtype)
        else:
            acc_ref = refs[n_in + 1]
            kk = pl.program_id(2)

            @pl.when(kk == 0)
            def _():
                f = first()
                acc_ref[...] = jnp.zeros_like(acc_ref) if f is None else f

            acc_ref[...] += _dot(a_ref[...], b_ref[...], dims)

            @pl.when(kk == nk - 1)
            def _():
                o_ref[...] = scaled(acc_ref[...]).astype(o_ref.dtype)

    scratch = [] if nk == 1 else [pltpu.VMEM((tm, tn), F32)]
    return pl.pallas_call(
        body, name=name, out_shape=out_shape, grid=(M // tm, N // tn, nk),
        in_specs=in_specs, out_specs=o_spec, scratch_shapes=scratch,
        input_output_aliases=aliases,
        compiler_params=_params("parallel", "parallel", "arbitrary"),
    )(*inputs)


def _join_chunk_edges(w, d):
    nch, rows, _ = w.shape

    def body(w_ref, o_ref, first, tail, sems):
        del w_ref
        for j in range(1, nch):
            mine = o_ref.at[j, :, pl.ds(0, LANES)]
            loads = [pltpu.make_async_copy(mine, first, sems.at[0]),
                     pltpu.make_async_copy(o_ref.at[j - 1, :, pl.ds(d, LANES)], tail, sems.at[1])]
            for cp in loads:
                cp.start()
            for cp in loads:
                cp.wait()
            first[...] = first[...] + tail[...]
            store = pltpu.make_async_copy(first, mine, sems.at[0])
            store.start()
            store.wait()

    return pl.pallas_call(
        body, name="join_chunk_edges", out_shape=jax.ShapeDtypeStruct(w.shape, w.dtype),
        in_specs=[ANY], out_specs=ANY, input_output_aliases={0: 0},
        scratch_shapes=[pltpu.VMEM((rows, LANES), w.dtype), pltpu.VMEM((rows, LANES), w.dtype),
                        pltpu.SemaphoreType.DMA((2,))],
    )(w)


def _shift_down(x, d, row):
    if d == 0:
        return x
    return jnp.where(row >= d, pltpu.roll(x, d, 0), 0.0)


def _shift_up(x, d, row, n):
    if d == 0:
        return x
    return jnp.where(row < n - d, pltpu.roll(x, n - d, 0), 0.0)


def _conv_pre(x, w_ref, b_ref, row):
    c = b_ref[...] + w_ref[CONV_A - 1:CONV_A, :] * x
    for k in range(CONV_A - 1):
        c = c + w_ref[k:k + 1, :] * _shift_down(x, CONV_A - 1 - k, row)
    return c


def _conv_fwd(u, conv_w, conv_b, *, S, D, name):
    tc = 256
    nq_blocks = (D // 2) // tc

    def body(u_ref, w_ref, b_ref, o_ref):
        x = u_ref[...]
        row = lax.broadcasted_iota(jnp.int32, x.shape, 0)
        c = _conv_pre(x, w_ref, b_ref, row)
        scale = jnp.where(pl.program_id(0) >= nq_blocks, DK_A ** -0.5, 1.0).astype(F32)
        o_ref[...] = (c * _sigmoid(c) * scale).astype(BF16)

    return pl.pallas_call(
        body, name=name, out_shape=jax.ShapeDtypeStruct((S, D), BF16), grid=(D // tc,),
        in_specs=[pl.BlockSpec((S, tc), lambda j: (0, j)),
                  pl.BlockSpec((CONV_A, tc), lambda j: (0, j)),
                  pl.BlockSpec((1, tc), lambda j: (0, j))],
        out_specs=pl.BlockSpec((S, tc), lambda j: (0, j)),
        compiler_params=_params("parallel"),
    )(u, conv_w, conv_b)


def _conv_bwd(u, dq, dk, conv_w, conv_b, du, *, S, D, name):
    tc = 256
    nq_blocks = (D // 2) // tc

    def body(u_ref, dq_ref, dk_ref, w_ref, b_ref, du_in, du_ref, dw_ref, db_ref):
        del du_in
        x = u_ref[...]
        n = x.shape[0]
        row = lax.broadcasted_iota(jnp.int32, x.shape, 0)
        c = _conv_pre(x, w_ref, b_ref, row)
        is_k = pl.program_id(0) >= nq_blocks
        dy = jnp.where(is_k, dk_ref[...] * (DK_A ** -0.5), dq_ref[...])
        sg = _sigmoid(c)
        dc = dy * (sg * (1.0 + c * (1.0 - sg)))
        db_ref[...] = jnp.sum(dc, axis=0, keepdims=True)
        dx = w_ref[CONV_A - 1:CONV_A, :] * dc
        dw_ref[CONV_A - 1:CONV_A, :] = jnp.sum(dc * x, axis=0, keepdims=True)
        for k in range(CONV_A - 1):
            d = CONV_A - 1 - k
            dw_ref[k:k + 1, :] = jnp.sum(dc * _shift_down(x, d, row), axis=0, keepdims=True)
            dx = dx + w_ref[k:k + 1, :] * _shift_up(dc, d, row, n)
        du_ref[...] = dx.astype(BF16)

    half = lambda j: (0, j % nq_blocks)
    return pl.pallas_call(
        body, name=name,
        out_shape=(jax.ShapeDtypeStruct(du.shape, du.dtype),
                   jax.ShapeDtypeStruct((CONV_A, D), F32), jax.ShapeDtypeStruct((1, D), F32)),
        grid=(D // tc,),
        in_specs=[pl.BlockSpec((S, tc), lambda j: (0, j)),
                  pl.BlockSpec((S, tc), half), pl.BlockSpec((S, tc), half),
                  pl.BlockSpec((CONV_A, tc), lambda j: (0, j)),
                  pl.BlockSpec((1, tc), lambda j: (0, j)), ANY],
        out_specs=(pl.BlockSpec((S, tc), lambda j: (0, j)),
                   pl.BlockSpec((CONV_A, tc), lambda j: (0, j)),
                   pl.BlockSpec((1, tc), lambda j: (0, j))),
        input_output_aliases={5: 0},
        compiler_params=_params("parallel"),
    )(u, dq, dk, conv_w, conv_b, du)


def _tri(n, cmp):
    r = lax.broadcasted_iota(jnp.int32, (n, n), 0)
    c = lax.broadcasted_iota(jnp.int32, (n, n), 1)
    return r, c, cmp(r, c)


def _gates_fwd(gt, bias, *, nha, nb, name):
    half = nha * nb
    nb_shift = nb.bit_length() - 1
    assert nb == 1 << nb_shift

    def body(g_ref, b_ref, f_ref, brow_ref):
        ig = g_ref[0:half, :] + b_ref[0:half, :]
        fg = g_ref[half:2 * half, :] + b_ref[half:2 * half, :]
        lf = _log_sigmoid(fg)
        _, _, upper = _tri(LANES, lambda r, c: r <= c)
        cs = _mask_dot3(lf, upper.astype(BF16))
        tot = jnp.broadcast_to(cs[:, LANES - 1:LANES], cs.shape)
        r, c, _ = _tri(half, lambda r, c: r <= c)
        before = jnp.logical_and(r >> nb_shift == c >> nb_shift, c < r).astype(BF16)
        f = cs + _mask_dot3_left(before, tot)
        f_ref[...] = f
        brow_ref[...] = ig - f

    return pl.pallas_call(
        body, name=name,
        out_shape=(jax.ShapeDtypeStruct((half, LANES), F32), jax.ShapeDtypeStruct((half, LANES), F32)),
    )(gt, bias)


def _gates_bwd(rowsum, colsum, gt, bias, *, nha, nb, name):
    half = nha * nb
    nb_shift = nb.bit_length() - 1
    assert nb == 1 << nb_shift

    def body(rs_ref, cs_ref, g_ref, b_ref, dg_ref, tot_ref):
        col = cs_ref[...]
        df = rs_ref[...] - col
        _, _, lower = _tri(LANES, lambda r, c: r >= c)
        rc = _mask_dot3(df, lower.astype(BF16))
        tot = jnp.broadcast_to(rc[:, 0:1], rc.shape)
        r, c, _ = _tri(half, lambda r, c: r <= c)
        same = r >> nb_shift == c >> nb_shift
        after = jnp.logical_and(same, c > r).astype(BF16)
        dlf = rc + _mask_dot3_left(after, tot)
        fg = g_ref[half:2 * half, :] + b_ref[half:2 * half, :]
        dfg = dlf * _sigmoid(-fg)
        dg_ref[0:half, :] = col
        dg_ref[half:2 * half, :] = dfg
        grp = same.astype(BF16)
        ones = jnp.ones((LANES, LANES), BF16)
        tot_ref[0:half, :] = _mask_dot3_left(grp, _mask_dot3(col, ones))
        tot_ref[half:2 * half, :] = _mask_dot3_left(grp, _mask_dot3(dfg, ones))

    return pl.pallas_call(
        body, name=name,
        out_shape=(jax.ShapeDtypeStruct((2 * half, LANES), F32), jax.ShapeDtypeStruct((2 * half, LANES), F32)),
    )(rowsum, colsum, gt, bias)


def _mlstm_tile(q, k_ref, fcol, brow_ref, j, tq, diagonal):
    off = pl.multiple_of(j * tq, tq)
    kj = k_ref[pl.ds(off, tq), :]
    s = _dot_nt(q, kj)
    logd = fcol + brow_ref[0, :, pl.ds(off, tq)]
    if diagonal:
        valid = lax.broadcasted_iota(jnp.int32, (tq, tq), 1) <= lax.broadcasted_iota(jnp.int32, (tq, tq), 0)
        logd = jnp.where(valid, logd, NEG_BIG)
    return off, kj, s, logd


def _mlstm_fwd(qk, v16, u, fcol, brow, head_g, *, S, D, nha, name):
    tq = min(ML_TQ, S)
    nq = S // tq
    kb, vb = (D // 2) // DK_A, D // DV_A

    def body(q_ref, k_ref, v_ref, fcol_ref, brow_ref, o_ref, z_ref, g_ref, h_ref, hg_ref, m_ref, den_ref):
        i = pl.program_id(1)
        q = q_ref[...]
        fc = fcol_ref[0]

        def step(j, carry, diagonal):
            acc, den, m = carry
            off, _, s, logd = _mlstm_tile(q, k_ref, fc, brow_ref, j, tq, diagonal)
            m_new = jnp.maximum(m, jnp.max(logd, axis=1, keepdims=True))
            a = s * jnp.exp(logd - m_new)
            alpha = jnp.exp(m - m_new)
            vj = v_ref[pl.ds(off, tq), :]
            acc = alpha * acc + _dot_nn(a.astype(BF16), vj)
            den = alpha * den + jnp.sum(a, axis=1, keepdims=True)
            return acc, den, m_new

        first = step(i, (jnp.zeros((tq, DV_A), F32), jnp.zeros((tq, 1), F32), jnp.full((tq, 1), NEG_BIG, F32)), True)
        acc, den, m = lax.fori_loop(0, i, lambda j, c: step(j, c, False), first)
        hh = acc / jnp.maximum(jnp.abs(den), jnp.exp(-m))
        h_ref[...] = hh
        hn, _ = _head_norm(hh)
        z = z_ref[...]
        hg_ref[...] = (_sigmoid(o_ref[...]) * (hn * g_ref[...]) * (z * _sigmoid(z))).astype(BF16)
        m_ref[0] = m
        den_ref[0] = den

    stat = pl.BlockSpec((1, tq, 1), lambda h, i: (h, i, 0))
    hblk = pl.BlockSpec((tq, DV_A), lambda h, i: (i, h))
    return pl.pallas_call(
        body, name=name,
        out_shape=(jax.ShapeDtypeStruct((S, D), F32), jax.ShapeDtypeStruct((S, D), BF16),
                   jax.ShapeDtypeStruct((nha, S, 1), F32), jax.ShapeDtypeStruct((nha, S, 1), F32)),
        grid=(nha, nq),
        in_specs=[pl.BlockSpec((tq, DK_A), lambda h, i: (i, h)),
                  pl.BlockSpec((S, DK_A), lambda h, i: (0, kb + h)),
                  pl.BlockSpec((S, DV_A), lambda h, i: (0, h)),
                  stat, pl.BlockSpec((1, 1, S), lambda h, i: (h, 0, 0)),
                  pl.BlockSpec((tq, DV_A), lambda h, i: (i, 2 * vb + h)),
                  pl.BlockSpec((tq, DV_A), lambda h, i: (i, 3 * vb + h)),
                  pl.BlockSpec((1, DV_A), lambda h, i: (0, h))],
        out_specs=(hblk, hblk, stat, stat),
        compiler_params=_params("parallel", "arbitrary"),
    )(qk, qk, v16, fcol, brow, u, u, head_g)


def _mlstm_bwd(qk, v16, fcol, brow, m, den, dh, h, du, *, S, D, nha, name):
    tq = min(ML_TQ, S)
    nq = S // tq
    kb, vb = (D // 2) // DK_A, D // DV_A

    def body(q_ref, k_ref, v_ref, fcol_ref, brow_ref, m_ref, den_ref, dh_ref, h_ref, du_in,
             du_ref, dq_ref, dk_ref, rs_ref, cs_ref, dv_acc):
        del du_in
        i = pl.program_id(1)

        @pl.when(i == 0)
        def _():
            dk_ref[...] = jnp.zeros_like(dk_ref)
            cs_ref[...] = jnp.zeros_like(cs_ref)
            dv_acc[...] = jnp.zeros_like(dv_acc)

        q = q_ref[...]
        fc = fcol_ref[0]
        mm = m_ref[0]
        dn = den_ref[0]
        floor = jnp.exp(-mm)
        nrm = jnp.maximum(jnp.abs(dn), floor)
        dhv = dh_ref[...]
        dnum = dhv / nrm
        dnrm = -jnp.sum(dhv * h_ref[...], axis=1, keepdims=True) / nrm
        dden = jnp.where(jnp.abs(dn) > floor, jnp.where(dn > 0.0, dnrm, -dnrm), 0.0)
        dnum_b = dnum.astype(BF16)

        def step(j, carry, diagonal):
            dq, rs = carry
            off, kj, s, logd = _mlstm_tile(q, k_ref, fc, brow_ref, j, tq, diagonal)
            p = jnp.exp(logd - mm)
            a = s * p
            vj = v_ref[pl.ds(off, tq), :]
            da = _dot_nt(dnum_b, vj) + dden
            dv_acc[pl.ds(off, tq), :] += _dot_tn(a.astype(BF16), dnum_b)
            dqk = (da * p).astype(BF16)
            dq = dq + _dot_nn(dqk, kj)
            dk_ref[pl.ds(off, tq), :] += _dot_tn(dqk, q)
            pm = da * a
            cs_ref[0, :, pl.ds(off, tq)] += jnp.sum(pm, axis=0, keepdims=True)
            rs = rs + jnp.sum(pm, axis=1, keepdims=True)
            return dq, rs

        first = step(i, (jnp.zeros((tq, DK_A), F32), jnp.zeros((tq, 1), F32)), True)
        dq, rs = lax.fori_loop(0, i, lambda j, c: step(j, c, False), first)
        dq_ref[...] = dq
        rs_ref[0] = rs

        @pl.when(i == nq - 1)
        def _():
            du_ref[...] = dv_acc[...].astype(BF16)

    stat = pl.BlockSpec((1, tq, 1), lambda h, i: (h, i, 0))
    rowv = pl.BlockSpec((1, 1, S), lambda h, i: (h, 0, 0))
    hblk = pl.BlockSpec((tq, DV_A), lambda h, i: (i, h))
    return pl.pallas_call(
        body, name=name,
        out_shape=(jax.ShapeDtypeStruct(du.shape, du.dtype),
                   jax.ShapeDtypeStruct((S, D // 2), F32), jax.ShapeDtypeStruct((S, D // 2), F32),
                   jax.ShapeDtypeStruct((nha, S, 1), F32), jax.ShapeDtypeStruct((nha, 1, S), F32)),
        grid=(nha, nq),
        in_specs=[pl.BlockSpec((tq, DK_A), lambda h, i: (i, h)),
                  pl.BlockSpec((S, DK_A), lambda h, i: (0, kb + h)),
                  pl.BlockSpec((S, DV_A), lambda h, i: (0, h)),
                  stat, rowv, stat, stat, hblk, hblk, ANY],
        out_specs=(pl.BlockSpec((S, DV_A), lambda h, i: (0, vb + h)),
                   pl.BlockSpec((tq, DK_A), lambda h, i: (i, h)),
                   pl.BlockSpec((S, DK_A), lambda h, i: (0, h)),
                   stat, rowv),
        scratch_shapes=[pltpu.VMEM((S, DV_A), F32)],
        input_output_aliases={9: 0},
        compiler_params=_params("parallel", "arbitrary"),
    )(qk, qk, v16, fcol, brow, m, den, dh, h, du)


def _head_norm(hh):
    mu = jnp.mean(hh, axis=1, keepdims=True)
    hc = hh - mu
    rstd = lax.rsqrt(jnp.mean(hc * hc, axis=1, keepdims=True) + LN_EPS)
    return hc * rstd, rstd


def _hgate_bwd(dhg, h, u, head_g, du, *, S, D, name):
    tm = min(256, S)
    nh = D // DV_A

    def body(dhg_ref, h_ref, o_ref, z_ref, g_ref, du_in, du_ref, dh_ref, dg_ref):
        del du_in

        @pl.when(pl.program_id(0) == 0)
        def _():
            dg_ref[...] = jnp.zeros_like(dg_ref)

        for hd in range(nh):
            sl = slice(hd * DV_A, (hd + 1) * DV_A)
            hn, rstd = _head_norm(h_ref[:, sl])
            o, z, g, d = o_ref[:, sl], z_ref[:, sl], g_ref[:, sl], dhg_ref[:, sl]
            so, sz = _sigmoid(o), _sigmoid(z)
            silu_z = z * sz
            hng = hn * g
            du_ref[:, sl] = (d * hng * silu_z * so * (1.0 - so)).astype(BF16)
            du_ref[:, D + hd * DV_A:D + (hd + 1) * DV_A] = (
                d * so * hng * (sz * (1.0 + z * (1.0 - sz)))).astype(BF16)
            t = d * so * silu_z
            dg_ref[:, sl] += jnp.sum(t * hn, axis=0, keepdims=True)
            dhn = t * g
            dh_ref[:, sl] = rstd * (dhn - jnp.mean(dhn, axis=1, keepdims=True)
                                    - hn * jnp.mean(dhn * hn, axis=1, keepdims=True))

    row = lambda i: (i, 0)
    return pl.pallas_call(
        body, name=name,
        out_shape=(jax.ShapeDtypeStruct(du.shape, du.dtype), jax.ShapeDtypeStruct((S, D), F32),
                   jax.ShapeDtypeStruct((1, D), F32)),
        grid=(S // tm,),
        in_specs=[pl.BlockSpec((tm, D), row), pl.BlockSpec((tm, D), row),
                  pl.BlockSpec((tm, D), lambda i: (i, 2)), pl.BlockSpec((tm, D), lambda i: (i, 3)),
                  pl.BlockSpec((1, D), lambda i: (0, 0)), ANY],
        out_specs=(pl.BlockSpec((tm, 2 * D), lambda i: (i, 1)), pl.BlockSpec((tm, D), row),
                   pl.BlockSpec((1, D), lambda i: (0, 0))),
        input_output_aliases={5: 0},
        compiler_params=_params("arbitrary"),
    )(dhg, h, u, u, head_g, du)


def _ln_stats(r):
    mu = jnp.mean(r, axis=1, keepdims=True)
    xc = r - mu
    rstd = lax.rsqrt(jnp.mean(xc * xc, axis=1, keepdims=True) + LN_EPS)
    return xc * rstd, rstd


def _ln_back(dxhat, xhat, rstd):
    return rstd * (dxhat - jnp.mean(dxhat, axis=1, keepdims=True)
                   - xhat * jnp.mean(dxhat * xhat, axis=1, keepdims=True))


def _ln_fwd(x, y, g, b, *, S, D, name):
    tm = min(256, S)

    def body(x_ref, y_ref, g_ref, b_ref, o_ref, ob_ref):
        xhat, _ = _ln_stats(ALPHA * x_ref[...] + y_ref[...])
        o = xhat * g_ref[...] + b_ref[...]
        o_ref[...] = o
        ob_ref[...] = o.astype(BF16)

    row = lambda i: (i, 0)
    vec = pl.BlockSpec((1, D), lambda i: (0, 0))
    return pl.pallas_call(
        body, name=name,
        out_shape=(jax.ShapeDtypeStruct((S, D), F32), jax.ShapeDtypeStruct((S, D), BF16)),
        grid=(S // tm,),
        in_specs=[pl.BlockSpec((tm, D), row), pl.BlockSpec((tm, D), row), vec, vec],
        out_specs=(pl.BlockSpec((tm, D), row), pl.BlockSpec((tm, D), row)),
        compiler_params=_params("parallel"),
    )(x, y, g, b)


def _ln_loss_bwd(x1, y2, target, g, b, *, S, D, name):
    tm = min(256, S)

    def body(x_ref, y_ref, t_ref, g_ref, b_ref, dr_ref, drb_ref, dg_ref, db_ref, loss_ref):
        @pl.when(pl.program_id(0) == 0)
        def _():
            dg_ref[...] = jnp.zeros_like(dg_ref)
            db_ref[...] = jnp.zeros_like(db_ref)
            loss_ref[...] = jnp.zeros_like(loss_ref)

        xhat, rstd = _ln_stats(ALPHA * x_ref[...] + y_ref[...])
        diff = xhat * g_ref[...] + b_ref[...] - t_ref[...]
        loss_ref[...] += (0.5 / D) * jnp.sum(diff * diff)
        dx2 = diff * (1.0 / D)
        dg_ref[...] += jnp.sum(dx2 * xhat, axis=0, keepdims=True)
        db_ref[...] += jnp.sum(dx2, axis=0, keepdims=True)
        dr = _ln_back(dx2 * g_ref[...], xhat, rstd)
        dr_ref[...] = dr
        drb_ref[...] = dr.astype(BF16)

    row = lambda i: (i, 0)
    vec = pl.BlockSpec((1, D), lambda i: (0, 0))
    return pl.pallas_call(
        body, name=name,
        out_shape=(jax.ShapeDtypeStruct((S, D), F32), jax.ShapeDtypeStruct((S, D), BF16),
                   jax.ShapeDtypeStruct((1, D), F32), jax.ShapeDtypeStruct((1, D), F32),
                   jax.ShapeDtypeStruct((1, LANES), F32)),
        grid=(S // tm,),
        in_specs=[pl.BlockSpec((tm, D), row)] * 3 + [vec, vec],
        out_specs=(pl.BlockSpec((tm, D), row), pl.BlockSpec((tm, D), row), vec, vec,
                   pl.BlockSpec((1, LANES), lambda i: (0, 0))),
        compiler_params=_params("arbitrary"),
    )(x1, y2, target, g, b)


def _ln_bwd(x, y, g, dout, *, S, D, name):
    tm = min(256, S)

    def body(x_ref, y_ref, g_ref, d_ref, dr_ref, drb_ref, dg_ref, db_ref):
        @pl.when(pl.program_id(0) == 0)
        def _():
            dg_ref[...] = jnp.zeros_like(dg_ref)
            db_ref[...] = jnp.zeros_like(db_ref)

        xhat, rstd = _ln_stats(ALPHA * x_ref[...] + y_ref[...])
        d = d_ref[...]
        dg_ref[...] += jnp.sum(d * xhat, axis=0, keepdims=True)
        db_ref[...] += jnp.sum(d, axis=0, keepdims=True)
        dr = _ln_back(d * g_ref[...], xhat, rstd)
        dr_ref[...] = dr
        drb_ref[...] = dr.astype(BF16)

    row = lambda i: (i, 0)
    vec = pl.BlockSpec((1, D), lambda i: (0, 0))
    return pl.pallas_call(
        body, name=name,
        out_shape=(jax.ShapeDtypeStruct((S, D), F32), jax.ShapeDtypeStruct((S, D), BF16),
                   jax.ShapeDtypeStruct((1, D), F32), jax.ShapeDtypeStruct((1, D), F32)),
        grid=(S // tm,),
        in_specs=[pl.BlockSpec((tm, D), row), pl.BlockSpec((tm, D), row), vec, pl.BlockSpec((tm, D), row)],
        out_specs=(pl.BlockSpec((tm, D), row), pl.BlockSpec((tm, D), row), vec, vec),
        compiler_params=_params("arbitrary"),
    )(x, y, g, dout)


def _sb_scores(q, kj, diagonal, t):
    z = _dot_nt(q, kj)
    ls = _log_sigmoid(z)
    if not diagonal:
        return None, ls, ls - z
    valid = lax.broadcasted_iota(jnp.int32, (t, t), 1) < lax.broadcasted_iota(jnp.int32, (t, t), 0)
    return valid, ls, jnp.where(valid, ls - z, 0.0)


def _keep(valid, x):
    return x if valid is None else jnp.where(valid, x, 0.0)


def _sb_fwd(q2, kv, z2, *, S, D, name, t=SB_TILE):
    t = min(t, S)
    nq = S // t
    nh = D // DH_B

    def body(q_ref, k_ref, v_ref, z_ref, o_ref, hb_ref, tot_ref):
        i = pl.program_id(1)
        q = q_ref[...]
        _, _, after = _tri(min(t, MXU_DEPTH), lambda r, c: r > c)
        tri_after = after.astype(BF16)

        def step(j, carry, diagonal):
            acc, cr = carry
            off = pl.multiple_of(j * t, t)
            valid, ls, lneg = _sb_scores(q, k_ref[pl.ds(off, t), :], diagonal, t)
            between = cr + _block_sums(lneg, tri_after, True)
            a = _keep(valid, jnp.exp(ls + between))
            acc = acc + _dot_nn(a.astype(BF16), v_ref[pl.ds(off, t), :])
            return acc, cr + jnp.sum(lneg, axis=1, keepdims=True)

        carry = step(i, (jnp.zeros((t, DH_B), F32), jnp.zeros((t, 1), F32)), True)
        acc, cr = lax.fori_loop(0, i, lambda jj, c: step(i - 1 - jj, c, False), carry)
        o_ref[...] = acc
        z = z_ref[...]
        hb_ref[...] = (acc * (z * _sigmoid(z))).astype(BF16)
        tot_ref[0] = cr

    blk = pl.BlockSpec((t, DH_B), lambda h, i: (i, h))
    return pl.pallas_call(
        body, name=name,
        out_shape=(jax.ShapeDtypeStruct((S, D), F32), jax.ShapeDtypeStruct((S, D), BF16),
                   jax.ShapeDtypeStruct((nh, S, 1), F32)), grid=(nh, nq),
        in_specs=[blk, pl.BlockSpec((S, DH_B), lambda h, i: (0, h)),
                  pl.BlockSpec((S, DH_B), lambda h, i: (0, nh + h)), blk],
        out_specs=(blk, blk, pl.BlockSpec((1, t, 1), lambda h, i: (h, i, 0))),
        compiler_params=_params("parallel", "arbitrary"),
    )(q2, kv, kv, z2)


def _sb_bwd(q2, kv, dhb, att, z2, tot, *, S, D, name, t=SB_TILE):
    t = min(t, S)
    nq = S // t
    nh = D // DH_B

    def body(q_ref, k_ref, v_ref, dhb_ref, att_ref, z_ref, tot_ref, dq_ref, dk_ref, dv_ref, dz2_ref, dk_acc, dv_acc):
        i = pl.program_id(1)

        @pl.when(i == 0)
        def _():
            dk_acc[...] = jnp.zeros_like(dk_acc)
            dv_acc[...] = jnp.zeros_like(dv_acc)

        q = q_ref[...]
        z, dhb = z_ref[...], dhb_ref[...]
        sz = _sigmoid(z)
        dz2_ref[...] = (dhb * att_ref[...] * (sz * (1.0 + z * (1.0 - sz)))).astype(BF16)
        do_b = (dhb * (z * sz)).astype(BF16)
        _, _, after = _tri(min(t, MXU_DEPTH), lambda r, c: r > c)
        tri_after = after.astype(BF16)
        _, _, before = _tri(min(t, MXU_DEPTH), lambda r, c: r < c)
        tri_before = before.astype(BF16)

        def step(j, carry, diagonal):
            dq, rest, cg = carry
            off = pl.multiple_of(j * t, t)
            kj = k_ref[pl.ds(off, t), :]
            valid, ls, lneg = _sb_scores(q, kj, diagonal, t)
            rest = rest - jnp.sum(lneg, axis=1, keepdims=True)
            between = rest + _block_sums(lneg, tri_after, True)
            a = _keep(valid, jnp.exp(ls + between))
            g = _dot_nt(do_b, v_ref[pl.ds(off, t), :]) * a
            dv_acc[pl.ds(off, t), :] += _dot_tn(a.astype(BF16), do_b)
            e = cg + _block_sums(g, tri_before, False, passes=1)
            sig = jnp.exp(ls)
            dz = _keep(valid, g * (1.0 - sig) - e * sig)
            dz_b = dz.astype(BF16)
            dq = dq + _dot_nn(dz_b, kj)
            dk_acc[pl.ds(off, t), :] += _dot_tn(dz_b, q)
            return dq, rest, cg + jnp.sum(g, axis=1, keepdims=True)

        carry = lax.fori_loop(0, i, lambda j, c: step(j, c, False),
                              (jnp.zeros((t, DH_B), F32), tot_ref[0], jnp.zeros((t, 1), F32)))
        dq, _, _ = step(i, carry, True)
        dq_ref[...] = (dq * (DH_B ** -0.5)).astype(BF16)

        @pl.when(i == nq - 1)
        def _():
            dk_ref[...] = dk_acc[...].astype(BF16)
            dv_ref[...] = dv_acc[...].astype(BF16)

    blk = pl.BlockSpec((t, DH_B), lambda h, i: (i, h))
    sd = jax.ShapeDtypeStruct((S, D), BF16)
    return pl.pallas_call(
        body, name=name, out_shape=(sd, sd, sd, sd), grid=(nh, nq),
        in_specs=[blk, pl.BlockSpec((S, DH_B), lambda h, i: (0, h)),
                  pl.BlockSpec((S, DH_B), lambda h, i: (0, nh + h)), blk, blk, blk,
                  pl.BlockSpec((1, t, 1), lambda h, i: (h, i, 0))],
        out_specs=(blk, pl.BlockSpec((S, DH_B), lambda h, i: (0, h)),
                   pl.BlockSpec((S, DH_B), lambda h, i: (0, h)), blk),
        scratch_shapes=[pltpu.VMEM((S, DH_B), F32), pltpu.VMEM((S, DH_B), F32)],
        compiler_params=_params("parallel", "arbitrary"),
    )(q2, kv, kv, dhb, att, z2, tot)


def _adamw_math(w, g, m, v):
    mn = ADAM_B1 * m + (1.0 - ADAM_B1) * g
    vn = ADAM_B2 * v + (1.0 - ADAM_B2) * (g * g)
    m_hat = mn / (1.0 - ADAM_B1 ** ADAM_STEP)
    v_hat = vn / (1.0 - ADAM_B2 ** ADAM_STEP)
    return -ADAM_LR * (m_hat / (jnp.sqrt(v_hat) + ADAM_EPS) + ADAM_WD * w), mn, vn


def _adamw(w, g, m, v, *, name, tr=128):
    rows, cols = w.shape
    tr = min(tr, rows)
    assert rows % tr == 0

    def body(w_ref, g_ref, m_ref, v_ref, d_ref, mo_ref, vo_ref):
        d, mn, vn = _adamw_math(w_ref[...], g_ref[...], m_ref[...], v_ref[...])
        d_ref[...] = d
        mo_ref[...] = mn
        vo_ref[...] = vn

    blk = pl.BlockSpec((tr, cols), lambda i: (i, 0))
    sd = jax.ShapeDtypeStruct((rows, cols), F32)
    return pl.pallas_call(
        body, name=name, out_shape=(sd, sd, sd), grid=(rows // tr,),
        in_specs=[blk] * 4, out_specs=(blk, blk, blk),
        compiler_params=_params("parallel"),
    )(w, g, m, v)


def _local_step(x, target, wa, wg, wao, wbi, wkv, wbo, gate_b, conv_w, conv_b, head_g,
                a_ln_g, a_ln_b, b_ln_g, b_ln_b, *, S, D, nha, rs):
    nb = S // LANES
    kw = dict(S=S, D=D)
    xb = x.astype(BF16)
    wa_chunk = D if wa.ndim == 3 else None
    u = _mm(xb, wa, mode="nn", M=S, N=4 * D, K=D, out_dtype=F32, name="a_in", b_chunk=wa_chunk)
    ug = _mm(xb, wg, mode="nn", M=S, N=LANES, K=D, out_dtype=F32, name="a_in_gates")
    qk = _conv_fwd(u, conv_w, conv_b, name="conv_fwd", **kw)
    gt = ug[:, :2 * nha].T.reshape(2 * nha * nb, LANES)
    gbias = jnp.repeat(gate_b.reshape(2 * nha), nb).reshape(2 * nha * nb, 1)
    fcs, brow = _gates_fwd(gt, gbias, nha=nha, nb=nb, name="gates_fwd")
    fcol = fcs.reshape(nha, S, 1)
    brow = brow.reshape(nha, 1, S)
    v16 = u[:, D:2 * D].astype(BF16)
    h, hg, m, den = _mlstm_fwd(qk, v16, u, fcol, brow, head_g, nha=nha, name="mlstm_fwd", **kw)
    y = _mm(hg, wao, mode="nn", M=S, N=D, K=D, out_dtype=F32, name="a_out")
    x1, x1b = _ln_fwd(x, y, a_ln_g, a_ln_b, name="ln_a_fwd", **kw)
    q2 = _mm(x1b, wbi, mode="nn", M=S, N=D, K=D, out_dtype=BF16, name="b_in_q", out_scale=DH_B ** -0.5)
    z2 = _mm(x1b, wbi, mode="nn", M=S, N=D, K=D, out_dtype=F32, name="b_in_z", b_off=(0, D))
    kv = _mm(x1b, wkv, mode="nn", M=S, N=2 * D, K=D, out_dtype=BF16, name="b_kv")
    att, hb, sb_tot = _sb_fwd(q2, kv, z2, name="sb_fwd", **kw)
    y2 = _mm(hb, wbo, mode="nn", M=S, N=D, K=D, out_dtype=F32, name="b_out")
    dr2, dr2b, d_bln_g, d_bln_b, loss = _ln_loss_bwd(x1, y2, target, b_ln_g, b_ln_b, name="ln_b_loss", **kw)
    g_wbo = _mm(hb, dr2b, mode="tn", M=D, N=D, K=S, out_dtype=BF16, name="g_b_out")
    dhb = _mm(dr2b, wbo, mode="nt", M=S, N=D, K=D, out_dtype=F32, name="d_b_out")
    dq2, dk2, dv2, dz2 = _sb_bwd(q2, kv, dhb, att, z2, sb_tot, name="sb_bwd", **kw)
    g_wbi = _mm(x1b, dq2, mode="tn", M=D, N=D, K=S, out_dtype=BF16, name="g_b_in_q",
                into=lax.empty((D, 2 * D), BF16))
    g_wbi = _mm(x1b, dz2, mode="tn", M=D, N=D, K=S, out_dtype=BF16, name="g_b_in_z", into=g_wbi, out_off=(0, D))
    g_wkv = _mm(x1b, dk2, mode="tn", M=D, N=D, K=S, out_dtype=BF16, name="g_kv_k",
                into=lax.empty((D, 2 * D), BF16))
    g_wkv = _mm(x1b, dv2, mode="tn", M=D, N=D, K=S, out_dtype=BF16, name="g_kv_v", into=g_wkv, out_off=(0, D))
    dq2 = rs.begin("b", dict(wbo=g_wbo, wbi=g_wbi, wkv=g_wkv), dq2)
    dx1 = _mm(dq2, wbi, mode="nt", M=S, N=D, K=D, out_dtype=F32, name="d_b_in_q", acc_in=dr2, acc_scale=ALPHA)
    dx1 = _mm(dz2, wbi, mode="nt", M=S, N=D, K=D, out_dtype=F32, name="d_b_in_z", b_off=(0, D), acc_in=dx1)
    dx1 = _mm(dk2, wkv, mode="nt", M=S, N=D, K=D, out_dtype=F32, name="d_kv_k", acc_in=dx1)
    dx1 = _mm(dv2, wkv, mode="nt", M=S, N=D, K=D, out_dtype=F32, name="d_kv_v", b_off=(0, D), acc_in=dx1)
    dx1 = rs.pair("b", dx1)
    dr, drb, d_aln_g, d_aln_b = _ln_bwd(x, y, a_ln_g, dx1, name="ln_a_bwd", **kw)
    g_wao = _mm(hg, drb, mode="tn", M=D, N=D, K=S, out_dtype=BF16, name="g_a_out")
    drb = rs.begin("a", dict(wao=g_wao), drb)
    dhg = _mm(drb, wao, mode="nt", M=S, N=D, K=D, out_dtype=F32, name="d_a_out")
    du = lax.empty((S, 4 * D), BF16)
    du, dh, d_head_g = _hgate_bwd(dhg, h, u, head_g, du, name="hgate_bwd", **kw)
    dh = rs.pair("a", dh)
    du, dq, dk, rowsum, colsum = _mlstm_bwd(qk, v16, fcol, brow, m, den, dh, h, du, nha=nha, name="mlstm_bwd", **kw)
    dgt, dgtot = _gates_bwd(rowsum.reshape(nha * nb, LANES), colsum.reshape(nha * nb, LANES), gt, gbias,
                            nha=nha, nb=nb, name="gates_bwd")
    d_gate_b = dgtot[::nb, 0].reshape(1, 2 * nha)
    dgp = jnp.pad(dgt.reshape(2 * nha, S).T, ((0, 0), (0, LANES - 2 * nha))).astype(BF16)
    du, d_conv_w, d_conv_b = _conv_bwd(u, dq, dk, conv_w, conv_b, du, name="conv_bwd", **kw)
    small = dict(gate_b=d_gate_b, conv_w=d_conv_w, conv_b=d_conv_b, head_g=d_head_g,
                 a_ln_g=d_aln_g, a_ln_b=d_aln_b, b_ln_g=d_bln_g, b_ln_b=d_bln_b)
    du = rs.chip("b", du)
    g_wa = _mm(xb, du, mode="tn", M=D, N=4 * D, K=S, out_dtype=BF16, name="g_a_in",
               into=lax.empty((D, 4 * D + LANES), BF16))
    g_wa = _mm(xb, dgp, mode="tn", M=D, N=LANES, K=S, out_dtype=BF16, name="g_a_in_gates", into=g_wa,
               out_off=(0, 4 * D))
    du = rs.begin("w", dict(wa_full=g_wa), du)
    du = rs.pair("w", rs.chip("a", rs.small(small, loss, du)))
    du = rs.finish("a", rs.finish("b", du))
    dx = _mm(du, wa, mode="nt", M=S, N=D, K=4 * D, out_dtype=F32, name="d_a_in", acc_in=dr, acc_scale=ALPHA,
             b_chunk=wa_chunk)
    dx = _mm(dgp, wg, mode="nt", M=S, N=D, K=LANES, out_dtype=F32, name="d_a_in_gates", acc_in=dx)
    return dx


def _coords():
    return lax.axis_index("x"), lax.axis_index("y"), lax.axis_index("c")


def _other_chips(x, y):
    return [(1 - x, y), (x, 1 - y), (1 - x, 1 - y)]


def _rows(ref, start, size):
    return ref.at[pl.ds(pl.multiple_of(start, size), size), :]


def _window(kind, ref, shard_shape, j, hf=None, quarter=None):
    r, cw = shard_shape
    row0, nr = (0, r) if hf is None else (hf * (r // 2), r // 2)
    if quarter is not None:
        row0, nr = row0 + quarter * (r // 4), r // 4
    if kind == "stack":
        return ref.at[j, pl.ds(pl.multiple_of(row0, nr), nr), :]
    if kind == "rows":
        return ref.at[pl.ds(pl.multiple_of(j * r + row0, nr), nr), :]
    assert kind == "cols"
    return ref.at[pl.ds(pl.multiple_of(row0, nr), nr), pl.ds(pl.multiple_of(j * cw, cw), cw)]


def _gathered_shape(kind, shard_shape):
    r, cw = shard_shape
    return {"stack": (N_CHIPS, r, cw), "rows": (N_CHIPS * r, cw), "cols": (r, N_CHIPS * cw)}[kind]


def _remote(src, dst, send_sems, recv_sems, k, to):
    return pltpu.make_async_remote_copy(src_ref=src, dst_ref=dst, send_sem=send_sems.at[k],
                                        recv_sem=recv_sems.at[k], device_id=to, device_id_type=MESH)


def _allgather_copies(s_refs, sm_ref, g_refs, smg_ref, send_sems, recv_sems, kinds, shapes):
    n = len(s_refs)
    x, y, c = _coords()
    me, sibling, chips = 2 * x + y, (x, y, 1 - c), _other_chips(x, y)
    ids = [2 * cx + cy for cx, cy in chips]
    d2d, own_base, small_base = 4 * n, 7 * n, 8 * n

    def win(a, j, hf=None, quarter=None):
        return _window(kinds[a], g_refs[a], shapes[a], j, hf, quarter)

    def small_win(j):
        w = sm_ref.shape[1]
        return smg_ref.at[:, pl.ds(pl.multiple_of(j * w, w), w)]

    def rc(src, dst, k, to):
        return _remote(src, dst, send_sems, recv_sems, k, to)

    first = [rc(s_refs[a], win(a, me), own_base + a, sibling) for a in range(n)]
    for a in range(n):
        src = _rows(s_refs[a], c * (shapes[a][0] // 2), shapes[a][0] // 2)
        first += [rc(src, win(a, me, c), 4 * a + k, (*chips[k], c)) for k in range(2)]
    if sm_ref is not None:
        first.append(rc(sm_ref, small_win(me), small_base + 3, sibling))
        first += [rc(sm_ref, small_win(me), small_base + k, (*chip, c)) for k, chip in enumerate(chips)]
    for cp in first:
        cp.start()
    passed = []
    for a in range(n):
        for k in range(2):
            rc(win(a, ids[k], c), win(a, ids[k], c), 4 * a + k, sibling).wait_recv()
            part = win(a, ids[k], c, quarter=k)
            passed.append(rc(part, part, 4 * a + 2 + k, (*chips[1 - k], c)))
            passed.append(rc(win(a, ids[k], c), win(a, ids[k], c), d2d + 3 * a + k, sibling))
            passed[-2].start()
            passed[-1].start()
    for a in range(n):
        for k in range(2):
            part = win(a, ids[2], c, quarter=k)
            rc(part, part, 4 * a + 2 + k, sibling).wait_recv()
        passed.append(rc(win(a, ids[2], c), win(a, ids[2], c), d2d + 3 * a + 2, sibling))
        passed[-1].start()
    if sm_ref is not None:
        for k in range(3):
            rc(small_win(ids[k]), small_win(ids[k]), small_base + k, sibling).wait_recv()
        rc(small_win(me), small_win(me), small_base + 3, sibling).wait_recv()
    for a in range(n):
        for k in range(3):
            rc(win(a, ids[k], 1 - c), win(a, ids[k], 1 - c), d2d + 3 * a + k, sibling).wait_recv()
        rc(win(a, me), win(a, me), own_base + a, sibling).wait_recv()
    for cp in first + passed:
        cp.wait_send()


def _peer_handshake(peers):
    barrier = pltpu.get_barrier_semaphore()
    for peer in peers:
        pl.semaphore_signal(barrier, inc=1, device_id=peer, device_id_type=MESH)
    pl.semaphore_wait(barrier, len(peers))


def _behind(copies, inputs, out_structs, nsem, peers, *, name, collective_id):
    hbm = pltpu.MemorySpace.HBM
    in_refs = [jax.new_ref(a, memory_space=hbm) for a in inputs]
    out_refs = [jax.empty_ref(st, memory_space=hbm) for st in out_structs]

    @pl.kernel(mesh=plsc.ScalarSubcoreMesh(axis_name="sequencer", num_cores=1), name=name,
               scratch_types=(pltpu.SemaphoreType.DMA((nsem,)), pltpu.SemaphoreType.DMA((nsem,))),
               compiler_params=pltpu.CompilerParams(collective_id=collective_id))
    def launch(send_sems, recv_sems):
        _peer_handshake(peers(*_coords()))
        copies(in_refs, out_refs, send_sems, recv_sems)

    launch()
    return [r[...] for r in out_refs]


def _sibling(x, y, c):
    return [(x, y, 1 - c)]


def _same_core_of_other_chips(x, y, c):
    return [(cx, cy, c) for cx, cy in _other_chips(x, y)]


def _sibling_and_other_chips(x, y, c):
    return _sibling(x, y, c) + _same_core_of_other_chips(x, y, c)


ID_ALL_PEERS, ID_SIBLING, ID_CHIPS = 0, 1, 2


def _allgather_behind(shards, kinds, small=None, *, name):
    n = len(shards)
    shapes = [sh.shape for sh in shards]

    def copies(in_refs, out_refs, send_sems, recv_sems):
        if small is None:
            _allgather_copies(in_refs, None, out_refs, None, send_sems, recv_sems, kinds, shapes)
        else:
            _allgather_copies(in_refs[:n], in_refs[n], out_refs[:n], out_refs[n], send_sems, recv_sems, kinds, shapes)

    outs = [jax.ShapeDtypeStruct(_gathered_shape(kinds[a], shapes[a]), shards[a].dtype) for a in range(n)]
    inputs = list(shards)
    if small is not None:
        outs.append(jax.ShapeDtypeStruct((small.shape[0], N_CHIPS * small.shape[1]), small.dtype))
        inputs.append(small)
    return _behind(copies, inputs, outs, 8 * n + 4, _sibling_and_other_chips, name=name, collective_id=ID_ALL_PEERS)


def _rs_pair_exchange(views, *, name):
    n = len(views)

    def copies(g_refs, p_refs, send_sems, recv_sems):
        x, y, c = _coords()
        cps = []
        for a in range(n):
            r2 = views[a].shape[1] // 2
            src = g_refs[a].at[:, pl.ds(pl.multiple_of((1 - c) * r2, r2), r2), :]
            cps.append(_remote(src, p_refs[a], send_sems, recv_sems, a, (x, y, 1 - c)))
        for cp in cps:
            cp.start()
        for cp in cps:
            cp.wait()

    outs = [jax.ShapeDtypeStruct((v.shape[0], v.shape[1] // 2, v.shape[2]), v.dtype) for v in views]
    return _behind(copies, views, outs, n, _sibling, name=name, collective_id=ID_SIBLING)


def _add_half(view, part, core, *, name):
    nch, r, cols = view.shape
    r2 = r // 2
    tr = min(512 if cols <= 4096 else 128, r2)
    nt = r2 // tr

    def body(c_ref, g_ref, p_ref, o_ref):
        del c_ref
        o_ref[...] = (g_ref[...].astype(F32) + p_ref[...].astype(F32)).astype(o_ref.dtype)

    return pl.pallas_call(
        body, name=name, out_shape=jax.ShapeDtypeStruct((nch, r2, cols), view.dtype),
        grid_spec=pltpu.PrefetchScalarGridSpec(
            num_scalar_prefetch=1, grid=(nch, nt),
            in_specs=[pl.BlockSpec((1, tr, cols), lambda ch, i, c_ref: (ch, c_ref[0] * nt + i, 0)),
                      pl.BlockSpec((1, tr, cols), lambda ch, i, c_ref: (ch, i, 0))],
            out_specs=pl.BlockSpec((1, tr, cols), lambda ch, i, c_ref: (ch, i, 0))),
        compiler_params=_params("parallel", "parallel"),
    )(core, view, part)


def _chunk(kind, ref, j, cw):
    if kind == "cols":
        return ref.at[0, :, pl.ds(pl.multiple_of(j * cw, cw), cw)]
    if isinstance(kind, tuple):
        return ref.at[0, :, pl.ds(pl.multiple_of(j * kind[1], LANES), cw)]
    return ref.at[j]


def _rs_chips(pairs, kinds, *, name):
    n = len(pairs)
    half_shapes = []
    for a in range(n):
        nch, r2, cols = pairs[a].shape
        if isinstance(kinds[a], tuple):
            half_shapes.append((r2, kinds[a][2]))
        else:
            half_shapes.append((r2, cols // N_CHIPS) if kinds[a] == "cols" else (r2, cols))

    def copies(q_refs, t_refs, send_sems, recv_sems):
        x, y, c = _coords()
        chips = _other_chips(x, y)
        sends = []
        for a in range(n):
            for k, (cx, cy) in enumerate(chips):
                src = _chunk(kinds[a], q_refs[a], 2 * cx + cy, half_shapes[a][1])
                sends.append(_remote(src, t_refs[a].at[k], send_sems, recv_sems, 3 * a + k, (cx, cy, c)))
        for cp in sends:
            cp.start()
        for a in range(n):
            for k in range(3):
                slot = t_refs[a].at[k]
                _remote(slot, slot, send_sems, recv_sems, 3 * a + k, (x, y, c)).wait_recv()
        for cp in sends:
            cp.wait_send()

    outs = [jax.ShapeDtypeStruct((3, *half_shapes[a]), pairs[a].dtype) for a in range(n)]
    return _behind(copies, pairs, outs, 3 * n, _same_core_of_other_chips, name=name, collective_id=ID_CHIPS)


def _sum_chips(pair, parts, chip, kind, *, name):
    _, r, cols = parts.shape
    tr = min(128, r)
    if kind == "cols":
        own_spec = pl.BlockSpec((1, tr, cols), lambda i, chip_ref: (0, i, chip_ref[0]))
    else:
        own_spec = pl.BlockSpec((1, tr, cols), lambda i, chip_ref: (chip_ref[0], i, 0))

    def body(chip_ref, q_ref, p_ref, o_ref):
        del chip_ref
        acc = q_ref[0].astype(F32)
        for s in range(3):
            acc = acc + p_ref[s].astype(F32)
        o_ref[...] = acc

    return pl.pallas_call(
        body, name=name, out_shape=jax.ShapeDtypeStruct((r, cols), F32),
        grid_spec=pltpu.PrefetchScalarGridSpec(
            num_scalar_prefetch=1, grid=(r // tr,),
            in_specs=[own_spec, pl.BlockSpec((3, tr, cols), lambda i, chip_ref: (0, i, 0))],
            out_specs=pl.BlockSpec((tr, cols), lambda i, chip_ref: (i, 0))),
        compiler_params=_params("parallel"),
    )(chip, pair, parts)


def _rs_share(halves, *, name):
    n = len(halves)

    def copies(h_refs, g_refs, send_sems, recv_sems):
        x, y, c = _coords()
        cps = [_remote(h_refs[a], g_refs[a], send_sems, recv_sems, a, (x, y, 1 - c)) for a in range(n)]
        for cp in cps:
            cp.start()
        for cp in cps:
            cp.wait()

    outs = [jax.ShapeDtypeStruct(h.shape, h.dtype) for h in halves]
    return _behind(copies, halves, outs, n, _sibling, name=name, collective_id=ID_SIBLING)


class _GradReducer:
    def __init__(self, w, m, v, long_name, core, chip_ix, as_views, small_step):
        self.w, self.m, self.v, self.long_name = w, m, v, long_name
        self.core, self.chip_ix, self.as_views, self.small_step = core, chip_ix, as_views, small_step
        self.groups, self.results, self.small_results = {}, {}, None

    def small(self, small, loss_row, tie):
        (small, loss_row), tie = lax.optimization_barrier(((small, loss_row), tie))
        self.small_results, tie = lax.optimization_barrier((self.small_step(small, loss_row), tie))
        return tie

    def begin(self, tag, grads, tie):
        keys, views, kinds = self.as_views(grads)
        views, tie = lax.optimization_barrier((views, tie))
        parts = _rs_pair_exchange(views, name="rs_pair_exchange_" + tag)
        self.groups[tag] = dict(keys=keys, views=views, kinds=kinds, parts=parts)
        return tie

    def pair(self, tag, tie):
        g = self.groups[tag]
        g["parts"], tie = lax.optimization_barrier((g["parts"], tie))
        pairs = [_add_half(v, p, self.core, name="pair_sum_" + self.long_name[k])
                 for v, p, k in zip(g["views"], g["parts"], g["keys"])]
        pairs, tie = lax.optimization_barrier((pairs, tie))
        g["pairs"] = pairs
        g["slots"] = _rs_chips(pairs, g["kinds"], name="rs_chips_" + tag)
        return tie

    def chip(self, tag, tie, update_mine=False):
        g = self.groups[tag]
        g["slots"], tie = lax.optimization_barrier((g["slots"], tie))
        halves = []
        for q, t, kd, k in zip(g["pairs"], g["slots"], g["kinds"], g["keys"]):
            if isinstance(kd, tuple):
                q = lax.dynamic_slice_in_dim(q[0], self.chip_ix[0] * kd[1], kd[2], axis=1)[None]
                halves.append(_sum_chips(q, t, jnp.zeros_like(self.chip_ix), "stack",
                                         name="chip_sum_" + self.long_name[k]))
            else:
                halves.append(_sum_chips(q, t, self.chip_ix, kd, name="chip_sum_" + self.long_name[k]))
        halves, tie = lax.optimization_barrier((halves, tie))
        g["halves"] = halves
        g["others"] = _rs_share(halves, name="rs_share_" + tag)
        if update_mine:
            mine = [_adamw_rows(self.w[k], self._shard(a, kd), self.m[k], self.v[k], self.core,
                                name="adamw_mine_" + self.long_name[k])
                    for k, a, kd in zip(g["keys"], halves, g["kinds"])]
            g["mine"], tie = lax.optimization_barrier((mine, tie))
        return tie

    def _shard(self, half, kind):
        if not isinstance(kind, tuple):
            return half
        return lax.dynamic_slice_in_dim(half, self.chip_ix[0] * kind[3], kind[4], axis=1)

    def finish(self, tag, tie):
        g = self.groups[tag]
        g["others"], tie = lax.optimization_barrier((g["others"], tie))
        if "mine" in g:
            out = [_adamw_rows(self.w[k], self._shard(b, kd), self.m[k], self.v[k], 1 - self.core, into=done,
                               name="adamw_theirs_" + self.long_name[k])
                   for k, b, kd, done in zip(g["keys"], g["others"], g["kinds"], g["mine"])]
        else:
            out = [_adamw_halves(self.w[k], a, b, self.m[k], self.v[k], self.core, name="adamw_" + self.long_name[k])
                   for k, a, b in zip(g["keys"], g["halves"], g["others"])]
        out, tie = lax.optimization_barrier((out, tie))
        self.results.update(zip(g["keys"], out))
        return tie


def _adamw_halves(w, mine, theirs, m, v, core, *, name, tr=128):
    rows, cols = w.shape
    hr = rows // 2
    tr = min(tr, hr)
    nt = hr // tr

    def body(c_ref, w_ref, a_ref, b_ref, m_ref, v_ref, g_ref, d_ref, mo_ref, vo_ref):
        gg = jnp.where(pl.program_id(0) == c_ref[0], a_ref[...], b_ref[...])
        g_ref[...] = gg
        d, mn, vn = _adamw_math(w_ref[...], gg, m_ref[...], v_ref[...])
        d_ref[...] = d
        mo_ref[...] = mn
        vo_ref[...] = vn

    full = pl.BlockSpec((tr, cols), lambda hf, i, c_ref: (hf * nt + i, 0))
    half = pl.BlockSpec((tr, cols), lambda hf, i, c_ref: (i, 0))
    sd = jax.ShapeDtypeStruct((rows, cols), F32)
    return pl.pallas_call(
        body, name=name, out_shape=(sd, sd, sd, sd),
        grid_spec=pltpu.PrefetchScalarGridSpec(
            num_scalar_prefetch=1, grid=(2, nt),
            in_specs=[full, half, half, full, full], out_specs=(full, full, full, full)),
        compiler_params=_params("parallel", "parallel"),
    )(core, w, mine, theirs, m, v)


def _adamw_rows(w, g_half, m, v, half, *, name, into=None, tr=128):
    rows, cols = w.shape
    hr = rows // 2
    tr = min(tr, hr)
    nt = hr // tr

    def body(h_ref, w_ref, a_ref, m_ref, v_ref, *rest):
        g_ref, d_ref, mo_ref, vo_ref = rest[-4:]
        gg = a_ref[...]
        g_ref[...] = gg
        d, mn, vn = _adamw_math(w_ref[...], gg, m_ref[...], v_ref[...])
        d_ref[...] = d
        mo_ref[...] = mn
        vo_ref[...] = vn

    full = pl.BlockSpec((tr, cols), lambda i, h_ref: (h_ref[0] * nt + i, 0))
    sd = jax.ShapeDtypeStruct((rows, cols), F32)
    extra = [] if into is None else list(into)
    return pl.pallas_call(
        body, name=name, out_shape=(sd, sd, sd, sd),
        grid_spec=pltpu.PrefetchScalarGridSpec(
            num_scalar_prefetch=1, grid=(nt,),
            in_specs=[full, pl.BlockSpec((tr, cols), lambda i, h_ref: (i, 0)), full, full] + [ANY] * len(extra),
            out_specs=(full, full, full, full)),
        input_output_aliases={5 + k: k for k in range(len(extra))},
        compiler_params=_params("parallel"),
    )(half, w, g_half, m, v, *extra)


def _allreduce_small(vec):
    m_per, ncol = vec.shape
    n_dev = 2 * N_CHIPS

    def body(x_ref, out_ref, sum_ref, send_sems, recv_sems, local_sem):
        x, y, c = _coords()
        me, sibling = (x, y, c), (x, y, 1 - c)
        chips = _other_chips(x, y)

        def rows(px, py, pc):
            return out_ref.at[pl.ds(pl.multiple_of((4 * px + 2 * py + pc) * m_per, m_per), m_per), :]

        def copy(k, block, to, src=None):
            return _remote(rows(*block) if src is None else src, rows(*block), send_sems, recv_sems, k, to)

        mine = pltpu.make_async_copy(x_ref, rows(*me), local_sem)
        mine.start()
        first = [copy(0, me, sibling, src=x_ref)]
        first += [copy(1 + j, me, (*chip, c), src=x_ref) for j, chip in enumerate(chips)]
        for cp in first:
            cp.start()
        passed = [copy(4 + j, (*chip, c), sibling) for j, chip in enumerate(chips)]
        for j, chip in enumerate(chips):
            copy(1 + j, (*chip, c), me).wait_recv()
            passed[j].start()
        copy(0, sibling, me).wait_recv()
        for j, chip in enumerate(chips):
            copy(4 + j, (*chip, 1 - c), me).wait_recv()
        for cp in first + passed:
            cp.wait_send()
        mine.wait()
        acc = out_ref[0:m_per, :]
        for d in range(1, n_dev):
            acc = acc + out_ref[d * m_per:(d + 1) * m_per, :]
        sum_ref[...] = acc

    vm = pl.BlockSpec(memory_space=pltpu.VMEM)
    return pl.pallas_call(
        body, name="allreduce_small",
        out_shape=(jax.ShapeDtypeStruct((n_dev * m_per, ncol), vec.dtype), jax.ShapeDtypeStruct((m_per, ncol), vec.dtype)),
        in_specs=[vm], out_specs=(vm, vm),
        scratch_shapes=[pltpu.SemaphoreType.DMA((7,)), pltpu.SemaphoreType.DMA((7,)), pltpu.SemaphoreType.DMA],
    )(vec)[1]


def _pack_rows(pieces, total_rows):
    rows = []
    for p in pieces:
        flat = p.reshape(-1)
        flat = jnp.pad(flat, (0, (-flat.shape[0]) % LANES))
        rows.append(flat.reshape(-1, LANES))
    out = jnp.concatenate(rows, axis=0)
    return jnp.pad(out, ((0, total_rows - out.shape[0]), (0, 0)))


def _unpack_rows(packed, shapes):
    out, r = [], 0
    for shp in shapes:
        size = 1
        for d in shp:
            size *= d
        nr = -(-size // LANES)
        out.append(packed[r:r + nr].reshape(-1)[:size].reshape(shp))
        r += nr
    return out


def _round_up(v, m):
    return -(-v // m) * m


def kernel(x, a_w_in, a_gate_b, a_conv_w, a_conv_b, a_head_g, a_w_out, a_ln_g, a_ln_b, kv_w, b_w_in, b_w_out, b_ln_g, b_ln_b, loss_target, m_a_w_in, m_a_gate_b, m_a_conv_w, m_a_conv_b, m_a_head_g, m_a_w_out, m_a_ln_g, m_a_ln_b, m_kv_w, m_b_w_in, m_b_w_out, m_b_ln_g, m_b_ln_b, v_a_w_in, v_a_gate_b, v_a_conv_w, v_a_conv_b, v_a_head_g, v_a_w_out, v_a_ln_g, v_a_ln_b, v_kv_w, v_b_w_in, v_b_w_out, v_b_ln_g, v_b_ln_b):
    _, S, D = x.shape
    nha = a_gate_b.shape[1] // 2
    chip = 2 * lax.axis_index("x") + lax.axis_index("y")
    core = lax.axis_index("c").astype(jnp.int32).reshape(1)
    dq = D // N_CHIPS

    ca = a_w_in.shape[2]
    edge = (N_CHIPS - 1) * (ca - D)
    wide = _round_up(ca + edge, LANES)
    shifted = lax.dynamic_slice_in_dim(
        jnp.pad(a_w_in[0].astype(BF16), ((0, 0), (edge, wide - ca))), edge - chip * (ca - D), wide, axis=1)
    shards = [shifted, a_w_out[0].astype(BF16), kv_w.astype(BF16), b_w_in[0].astype(BF16), b_w_out[0].astype(BF16)]
    kinds = ["stack", "rows", "cols", "cols", "rows"]
    small_shard = jnp.concatenate([a_conv_w[0], a_conv_b, a_head_g, a_ln_g, a_ln_b], axis=0)
    wa_g, small_full = _allgather_behind(shards[:1], kinds[:1], small_shard, name="allgather_first")
    wao, = _allgather_behind(shards[1:2], kinds[1:2], name="allgather_a_w_out")
    wkv, wbi = _allgather_behind(shards[2:4], kinds[2:4], name="allgather_b_in")
    wbo, = _allgather_behind(shards[4:], kinds[4:], name="allgather_b_w_out")
    wa = _join_chunk_edges(wa_g, D)
    wg = wa[N_CHIPS - 1, :, D:D + LANES]
    conv_w, conv_b, head_g, ln_g_a, ln_b_a = (small_full[0:CONV_A], small_full[4:5], small_full[5:6],
                                              small_full[6:7], small_full[7:8])

    chip_ix = chip.astype(jnp.int32).reshape(1)
    (w_wa, m_wa, v_wa), x = lax.optimization_barrier(((a_w_in[0], m_a_w_in[0], v_a_w_in[0]), x))
    w_big = dict(wa=w_wa, wao=a_w_out[0], wkv=kv_w, wbi=b_w_in[0], wbo=b_w_out[0])
    m_big = dict(wa=m_wa, wao=m_a_w_out[0], wkv=m_kv_w, wbi=m_b_w_in[0], wbo=m_b_w_out[0])
    v_big = dict(wa=v_wa, wao=v_a_w_out[0], wkv=v_kv_w, wbi=v_b_w_in[0], wbo=v_b_w_out[0])
    long_name = dict(wa="a_w_in", wao="a_w_out", wkv="kv_w", wbi="b_w_in", wbo="b_w_out")

    def as_views(g):
        if "wa_full" in g:
            return ["wa"], [g["wa_full"][None]], [("win", D, wide, ca - D, ca)]
        keys = list(g)
        views = [g[k].reshape(N_CHIPS, dq, D) if k in ("wao", "wbo") else g[k][None] for k in keys]
        return keys, views, ["stack" if k in ("wao", "wbo") else "cols" for k in keys]

    w_small = [a_gate_b, a_conv_w[0], a_conv_b, a_head_g, a_ln_g, a_ln_b, b_ln_g, b_ln_b]
    m_small = [m_a_gate_b, m_a_conv_w[0], m_a_conv_b, m_a_head_g, m_a_ln_g, m_a_ln_b, m_b_ln_g, m_b_ln_b]
    v_small = [v_a_gate_b, v_a_conv_w[0], v_a_conv_b, v_a_head_g, v_a_ln_g, v_a_ln_b, v_b_ln_g, v_b_ln_b]

    def small_step(small, loss_row):
        order = ["conv_w", "conv_b", "head_g", "a_ln_g", "a_ln_b", "b_ln_g", "b_ln_b", "gate_b"]
        full_shapes = [(CONV_A, D), (1, D), (1, D), (1, D), (1, D), (1, D), (1, D), (1, 2 * nha)]
        n_rows = sum(-(-(s[0] * s[1]) // LANES) for s in full_shapes) + 1
        total = _allreduce_small(_pack_rows([small[k] for k in order] + [loss_row], _round_up(n_rows, 8)))
        sums = dict(zip(order, _unpack_rows(total, full_shapes)))

        def mine(v):
            return lax.dynamic_slice_in_dim(v, chip * dq, dq, axis=1)

        g_small = [sums["gate_b"], mine(sums["conv_w"]), mine(sums["conv_b"]), mine(sums["head_g"]),
                   mine(sums["a_ln_g"]), mine(sums["a_ln_b"]), sums["b_ln_g"], sums["b_ln_b"]]
        rows_small = _round_up(sum(-(-(w.shape[0] * w.shape[1]) // LANES) for w in w_small), 8)
        upd_small = _adamw(_pack_rows(w_small, rows_small), _pack_rows(g_small, rows_small),
                           _pack_rows(m_small, rows_small), _pack_rows(v_small, rows_small), name="adamw_small")
        return total[n_rows - 1, 0], g_small, upd_small

    rs = _GradReducer(w_big, m_big, v_big, long_name, core, chip_ix, as_views, small_step)
    grad_x = _local_step(
        x[0], loss_target[0], wa, wg, wao, wbi, wkv, wbo, a_gate_b, conv_w, conv_b, head_g,
        ln_g_a, ln_b_a, b_ln_g, b_ln_b, S=S, D=D, nha=nha, rs=rs)
    grad_x = rs.finish("w", rs.chip("w", grad_x, update_mine=True))
    upd_big = [rs.results[k] for k in ("wa", "wao", "wkv", "wbi", "wbo")]
    g_big = [u[0] for u in upd_big]
    loss, g_small, upd_small = rs.small_results
    d_small, mn_small, vn_small = (_unpack_rows(u, [w.shape for w in w_small]) for u in upd_small)

    def assemble(big5, small8):
        awi, awo, kvw, bwi, bwo = big5
        gb, cw, cb, hg, alg, alb, blg, blb = small8
        return [awi[None], gb, cw[None], cb, hg, awo[None], alg, alb, kvw, bwi[None], bwo[None], blg, blb]

    grads = assemble(g_big, g_small)
    deltas = assemble([u[1] for u in upd_big], d_small)
    new_m = assemble([u[2] for u in upd_big], mn_small)
    new_v = assemble([u[3] for u in upd_big], vn_small)
    return (loss, grad_x[None], *grads, *deltas, *new_m, *new_v)
```
